```python
import math
import jax
import jax.numpy as jnp
from jax import lax
import numpy as np

D_MODEL = 1024
BATCH = 4
SEQ = 4096
DEPTH = 1

N_META = 16
SSM_WIDTH = D_MODEL // 2
SSM_GROUP = 16
SSM_GROUPS = SSM_WIDTH // SSM_GROUP
SSM_STATE = 64
HEAD_DIM = 64
N_HEADS = D_MODEL // HEAD_DIM
N_KV_HEADS = 2
KV_REP = N_HEADS // N_KV_HEADS
WINDOW = 128
ATTN_BLOCK = 128
Q_WIDTH = N_HEADS * HEAD_DIM
KV_WIDTH = N_KV_HEADS * HEAD_DIM
IN_WIDTH = SSM_WIDTH + Q_WIDTH + 2 * KV_WIDTH + 2 * D_MODEL
N_EXPERTS = 32
TOP_K = 4
D_FF = D_MODEL
SWIGLU_ALPHA = 1.702
SWIGLU_LIMIT = 7.0
MOE_BLOCK = 128
RMS_EPS = 1e-5
NEG_INF = -1e30

kernel_name = "hybrid_s5_swa_sink_moe_meta"


def rms_norm(x, g):
    xf = x.astype(jnp.float32)
    y = xf * lax.rsqrt(jnp.mean(xf * xf, axis=-1, keepdims=True) + RMS_EPS)
    return (y * g.astype(jnp.float32)).astype(x.dtype)


def s5_mixer(u, a_re, a_im, log_dt, b_re, b_im, c_re, c_im, d_skip):
    bsz, seq_len, _ = u.shape
    f32 = jnp.float32
    uf = u.astype(f32)
    ug = uf.reshape(bsz, seq_len, SSM_GROUPS, SSM_GROUP)
    a_re, a_im = a_re.astype(f32), a_im.astype(f32)
    b_re, b_im = b_re.astype(f32), b_im.astype(f32)
    c_re, c_im = c_re.astype(f32), c_im.astype(f32)
    dt = jnp.exp(log_dt.astype(f32))[:, None]
    mag = jnp.exp(a_re * dt)
    ang = a_im * dt
    abar_re, abar_im = mag * jnp.cos(ang), mag * jnp.sin(ang)
    den = a_re * a_re + a_im * a_im
    nr, ni = abar_re - 1.0, abar_im
    coef_re = ((nr * a_re + ni * a_im) / den)[..., None]
    coef_im = ((ni * a_re - nr * a_im) / den)[..., None]
    bbar_re = coef_re * b_re - coef_im * b_im
    bbar_im = coef_re * b_im + coef_im * b_re
    bu_re = jnp.einsum('blgc,gpc->blgp', ug, bbar_re)
    bu_im = jnp.einsum('blgc,gpc->blgp', ug, bbar_im)
    ar = jnp.broadcast_to(abar_re[None, None], (1, seq_len, SSM_GROUPS, SSM_STATE))
    ai = jnp.broadcast_to(abar_im[None, None], (1, seq_len, SSM_GROUPS, SSM_STATE))

    def combine(e1, e2):
        a1r, a1i, b1r, b1i = e1
        a2r, a2i, b2r, b2i = e2
        return (a2r * a1r - a2i * a1i,
                a2r * a1i + a2i * a1r,
                a2r * b1r - a2i * b1i + b2r,
                a2r * b1i + a2i * b1r + b2i)

    _, _, s_re, s_im = lax.associative_scan(combine, (ar, ai, bu_re, bu_im), axis=1)
    y = jnp.einsum('blgp,gcp->blgc', s_re, c_re) - jnp.einsum('blgp,gcp->blgc', s_im, c_im)
    y = y.reshape(bsz, seq_len, SSM_WIDTH) + d_skip.astype(f32) * uf
    return y.astype(u.dtype)


def window_attention(q, k, v, sinks):
    bsz, seq_len = q.shape[:2]
    s_real = seq_len - N_META
    nb = s_real // ATTN_BLOCK
    scale = HEAD_DIM ** -0.5
    f32 = jnp.float32
    q = q.reshape(bsz, seq_len, N_KV_HEADS, KV_REP, HEAD_DIM)
    qm, qr = q[:, :N_META], q[:, N_META:]
    km, kr = k[:, :N_META], k[:, N_META:]
    vm, vr = v[:, :N_META], v[:, N_META:]
    sink = sinks.astype(f32).reshape(N_KV_HEADS, KV_REP)

    s_mm = jnp.einsum('bqkrd,bmkd->bkrqm', qm, km).astype(f32) * scale
    causal = jnp.arange(N_META)[:, None] >= jnp.arange(N_META)[None, :]
    s_mm = jnp.where(causal, s_mm, NEG_INF)
    sink_m = jnp.broadcast_to(sink[None, :, :, None, None], (bsz, N_KV_HEADS, KV_REP, N_META, 1))
    p_mm = jax.nn.softmax(jnp.concatenate([s_mm, sink_m], axis=-1), axis=-1)[..., :N_META]
    out_m = jnp.einsum('bkrqm,bmkd->bqkrd', p_mm.astype(v.dtype), vm)

    qb = qr.reshape(bsz, nb, ATTN_BLOCK, N_KV_HEADS, KV_REP, HEAD_DIM)

    def band(t):
        tp = jnp.pad(t, ((0, 0), (ATTN_BLOCK, 0), (0, 0), (0, 0)))
        tp = tp.reshape(bsz, nb + 1, ATTN_BLOCK, N_KV_HEADS, HEAD_DIM)
        return jnp.concatenate([tp[:, :-1], tp[:, 1:]], axis=2)

    kb, vb = band(kr), band(vr)
    s_band = jnp.einsum('bnqkrd,bnskd->bnkrqs', qb, kb).astype(f32) * scale
    qpos = jnp.arange(nb)[:, None, None] * ATTN_BLOCK + jnp.arange(ATTN_BLOCK)[None, :, None]
    kpos = (jnp.arange(nb)[:, None, None] - 1) * ATTN_BLOCK + jnp.arange(2 * ATTN_BLOCK)[None, None, :]
    rel = qpos - kpos
    vis = (rel >= 0) & (rel < WINDOW) & (kpos >= 0)
    s_band = jnp.where(vis[None, :, None, None], s_band, NEG_INF)
    s_meta = jnp.einsum('bnqkrd,bmkd->bnkrqm', qb, km).astype(f32) * scale
    sink_r = jnp.broadcast_to(sink[None, None, :, :, None, None],
                              (bsz, nb, N_KV_HEADS, KV_REP, ATTN_BLOCK, 1))
    p = jax.nn.softmax(jnp.concatenate([s_meta, s_band, sink_r], axis=-1), axis=-1).astype(v.dtype)
    out_r = (jnp.einsum('bnkrqm,bmkd->bnqkrd', p[..., :N_META], vm)
             + jnp.einsum('bnkrqs,bnskd->bnqkrd', p[..., N_META:N_META + 2 * ATTN_BLOCK], vb))
    out_r = out_r.reshape(bsz, s_real, N_KV_HEADS, KV_REP, HEAD_DIM)
    out = jnp.concatenate([out_m, out_r], axis=1)
    return out.reshape(bsz, seq_len, Q_WIDTH)


def clamped_swiglu(x_glu, x_lin):
    x_glu = jnp.minimum(x_glu, SWIGLU_LIMIT)
    x_lin = jnp.clip(x_lin, -SWIGLU_LIMIT, SWIGLU_LIMIT)
    return x_glu * jax.nn.sigmoid(SWIGLU_ALPHA * x_glu) * (x_lin + 1.0)


def moe_ffn(h, router_w, router_b, w_up, b_up, w_down, b_down):
    n_tok = h.shape[0]
    n_assign = n_tok * TOP_K
    logits = (h @ router_w + router_b).astype(jnp.float32)
    top_vals, top_idx = lax.top_k(logits, TOP_K)
    gates = jax.nn.softmax(top_vals, axis=-1)
    e_flat = top_idx.reshape(-1)
    tok_flat = jnp.repeat(jnp.arange(n_tok, dtype=jnp.int32), TOP_K)
    g_flat = gates.reshape(-1)
    order = jnp.argsort(e_flat)
    e_s, tok_s, g_s = e_flat[order], tok_flat[order], g_flat[order]
    counts = jnp.bincount(e_flat, length=N_EXPERTS)
    padded = ((counts + MOE_BLOCK - 1) // MOE_BLOCK) * MOE_BLOCK
    start = jnp.cumsum(counts) - counts
    block_end = jnp.cumsum(padded)
    pstart = block_end - padded
    dest = pstart[e_s] + (jnp.arange(n_assign) - start[e_s])
    n_blocks = (n_assign + N_EXPERTS * (MOE_BLOCK - 1) + MOE_BLOCK - 1) // MOE_BLOCK
    n_rows = n_blocks * MOE_BLOCK
    row_tok = jnp.zeros((n_rows,), jnp.int32).at[dest].set(tok_s)
    row_g = jnp.zeros((n_rows,), jnp.float32).at[dest].set(g_s)
    block_expert = jnp.minimum(
        jnp.searchsorted(block_end, jnp.arange(n_blocks) * MOE_BLOCK, side='right'), N_EXPERTS - 1)

    def expert_block(args):
        tok, e = args
        xb = h[tok]
        up = xb @ w_up[e] + b_up[e]
        act = clamped_swiglu(up[:, :D_FF], up[:, D_FF:])
        return act @ w_down[e] + b_down[e]

    y = lax.map(expert_block, (row_tok.reshape(n_blocks, MOE_BLOCK), block_expert))
    y = y.reshape(n_rows, -1) * row_g[:, None].astype(h.dtype)
    return jax.ops.segment_sum(y, row_tok, num_segments=n_tok)


def setup_inputs(seed: int = 0) -> dict:
    key = jax.random.key(seed)
    ks = jax.random.split(key, 24)
    f32 = jnp.float32
    nrm = lambda k, shape, s: jax.random.normal(k, shape, f32) * s
    x = jax.random.normal(ks[0], (BATCH, SEQ, D_MODEL), f32)
    meta_tokens = nrm(ks[1], (N_META, D_MODEL), 1.0)
    mix_norm_g = 1.0 + nrm(ks[2], (DEPTH, D_MODEL), 0.05)
    w_in = nrm(ks[3], (DEPTH, D_MODEL, IN_WIDTH), D_MODEL ** -0.5)
    ssm_a_re = -0.5 * jnp.exp(nrm(ks[4], (DEPTH, SSM_GROUPS, SSM_STATE), 0.02))
    ssm_a_im = jnp.broadcast_to(math.pi * jnp.arange(SSM_STATE, dtype=f32),
                                (DEPTH, SSM_GROUPS, SSM_STATE))
    ssm_log_dt = jax.random.uniform(ks[5], (DEPTH, SSM_GROUPS), f32,
                                    math.log(0.001), math.log(0.1))
    ssm_b_re = nrm(ks[6], (DEPTH, SSM_GROUPS, SSM_STATE, SSM_GROUP), (2 * SSM_GROUP) ** -0.5)
    ssm_b_im = nrm(ks[7], (DEPTH, SSM_GROUPS, SSM_STATE, SSM_GROUP), (2 * SSM_GROUP) ** -0.5)
    ssm_c_re = nrm(ks[8], (DEPTH, SSM_GROUPS, SSM_GROUP, SSM_STATE), (2 * SSM_STATE) ** -0.5)
    ssm_c_im = nrm(ks[9], (DEPTH, SSM_GROUPS, SSM_GROUP, SSM_STATE), (2 * SSM_STATE) ** -0.5)
    ssm_d = nrm(ks[10], (DEPTH, SSM_WIDTH), 1.0)
    w_ssm_glu = nrm(ks[11], (DEPTH, SSM_WIDTH, 2 * D_MODEL), SSM_WIDTH ** -0.5)
    attn_sinks = nrm(ks[12], (DEPTH, N_HEADS), 0.5)
    w_attn_o = nrm(ks[13], (DEPTH, Q_WIDTH, D_MODEL), Q_WIDTH ** -0.5)
    w_out = nrm(ks[14], (DEPTH, D_MODEL, D_MODEL), D_MODEL ** -0.5)
    ffn_norm_g = 1.0 + nrm(ks[15], (DEPTH, D_MODEL), 0.05)
    router_w = nrm(ks[16], (DEPTH, D_MODEL, N_EXPERTS), D_MODEL ** -0.5)
    router_b = nrm(ks[17], (DEPTH, N_EXPERTS), 0.01)
    w_up = nrm(ks[18], (DEPTH, N_EXPERTS, D_MODEL, 2 * D_FF), D_MODEL ** -0.5)
    b_up = nrm(ks[19], (DEPTH, N_EXPERTS, 2 * D_FF), 0.01)
    w_down = nrm(ks[20], (DEPTH, N_EXPERTS, D_FF, D_MODEL), D_FF ** -0.5)
    b_down = nrm(ks[21], (DEPTH, N_EXPERTS, D_MODEL), 0.01)
    final_norm_g = 1.0 + nrm(ks[22], (D_MODEL,), 0.05)
    return {"x": x, "meta_tokens": meta_tokens, "mix_norm_g": mix_norm_g, "w_in": w_in,
            "ssm_a_re": ssm_a_re, "ssm_a_im": ssm_a_im, "ssm_log_dt": ssm_log_dt,
            "ssm_b_re": ssm_b_re, "ssm_b_im": ssm_b_im, "ssm_c_re": ssm_c_re, "ssm_c_im": ssm_c_im,
            "ssm_d": ssm_d, "w_ssm_glu": w_ssm_glu, "attn_sinks": attn_sinks, "w_attn_o": w_attn_o,
            "w_out": w_out, "ffn_norm_g": ffn_norm_g, "router_w": router_w, "router_b": router_b,
            "w_up": w_up, "b_up": b_up, "w_down": w_down, "b_down": b_down,
            "final_norm_g": final_norm_g}


def reference(x, meta_tokens, mix_norm_g, w_in, ssm_a_re, ssm_a_im, ssm_log_dt, ssm_b_re, ssm_b_im,
              ssm_c_re, ssm_c_im, ssm_d, w_ssm_glu, attn_sinks, w_attn_o, w_out, ffn_norm_g,
              router_w, router_b, w_up, b_up, w_down, b_down, final_norm_g):
    bsz = x.shape[0]
    meta = jnp.broadcast_to(meta_tokens[None].astype(x.dtype), (bsz, N_META, D_MODEL))
    h = jnp.concatenate([meta, x], axis=1)
    seq_len = h.shape[1]
    splits = [SSM_WIDTH, SSM_WIDTH + Q_WIDTH, SSM_WIDTH + Q_WIDTH + KV_WIDTH,
              SSM_WIDTH + Q_WIDTH + 2 * KV_WIDTH, SSM_WIDTH + Q_WIDTH + 2 * KV_WIDTH + D_MODEL]
    for layer in range(DEPTH):
        hn = rms_norm(h, mix_norm_g[layer])
        proj = hn @ w_in[layer]
        u, q, k, v, g_ssm, g_attn = jnp.split(proj, splits, axis=-1)
        y_ssm = s5_mixer(u, ssm_a_re[layer], ssm_a_im[layer], ssm_log_dt[layer], ssm_b_re[layer],
                         ssm_b_im[layer], ssm_c_re[layer], ssm_c_im[layer], ssm_d[layer])
        glu = jax.nn.gelu(y_ssm) @ w_ssm_glu[layer]
        branch_ssm = glu[..., :D_MODEL] * jax.nn.sigmoid(glu[..., D_MODEL:])
        attn = window_attention(q.reshape(bsz, seq_len, N_HEADS, HEAD_DIM),
                                k.reshape(bsz, seq_len, N_KV_HEADS, HEAD_DIM),
                                v.reshape(bsz, seq_len, N_KV_HEADS, HEAD_DIM),
                                attn_sinks[layer])
        branch_attn = attn @ w_attn_o[layer]
        merged = jax.nn.sigmoid(g_ssm) * branch_ssm + jax.nn.sigmoid(g_attn) * branch_attn
        h = h + merged @ w_out[layer]
        hf = rms_norm(h, ffn_norm_g[layer]).reshape(bsz * seq_len, D_MODEL)
        h = h + moe_ffn(hf, router_w[layer], router_b[layer], w_up[layer], b_up[layer],
                        w_down[layer], b_down[layer]).reshape(bsz, seq_len, D_MODEL)
    h = rms_norm(h, final_norm_g)
    return h[:, N_META:]
```

```python
import functools
import math

import jax
import jax.numpy as jnp
from jax import lax
from jax.experimental import pallas as pl
from jax.experimental.pallas import tpu as pltpu

F32 = jnp.float32
BF16 = jnp.bfloat16

D_MODEL = 1024
N_META = 16
SSM_WIDTH = 512
SSM_GROUP = 16
SSM_GROUPS = 32
SSM_STATE = 64
HEAD_DIM = 64
N_HEADS = 16
N_KV_HEADS = 2
KV_REP = N_HEADS // N_KV_HEADS
WINDOW = 128
Q_WIDTH = N_HEADS * HEAD_DIM
KV_WIDTH = N_KV_HEADS * HEAD_DIM
N_EXPERTS = 32
TOP_K = 4
D_FF = 1024
SWIGLU_ALPHA = 1.702
SWIGLU_LIMIT = 7.0
RMS_EPS = 1e-5
NEG_INF = -1e30

_U0, _Q0, _KV0, _GS0, _GA0, _IN_END = 0, 512, 1536, 1792, 2816, 3840

SSM_CH_BLOCK = 128
SSM_HALF = (SSM_CH_BLOCK // SSM_GROUP) * SSM_STATE
SSM_CHUNK = 32
EXPERT_ROWS = 256
VMEM_LIMIT = 56 * 1024 * 1024


def _dot(a, b):
    return jnp.dot(a, b, preferred_element_type=F32)


def _dot_nt(a, b):
    return lax.dot_general(a, b, (((1,), (1,)), ((), ())), preferred_element_type=F32)


def _in_proj_kernel(x_ref, g_ref, w_ref, u_ref, q_ref, kv_ref, gs_ref, ga_ref):
    x = x_ref[...]
    ms = jnp.mean(x * x, axis=-1, keepdims=True)
    hn = (x * lax.rsqrt(ms + RMS_EPS) * g_ref[...]).astype(BF16)
    u_ref[...] = _dot(hn, w_ref[:, _U0:_Q0])
    q_ref[...] = (_dot(hn, w_ref[:, _Q0:_KV0]) * (HEAD_DIM ** -0.5)).astype(BF16)
    kv_ref[...] = _dot(hn, w_ref[:, _KV0:_GS0]).astype(BF16)
    gs_ref[...] = jax.nn.sigmoid(_dot(hn, w_ref[:, _GS0:_GA0])).astype(BF16)
    ga_ref[...] = jax.nn.sigmoid(_dot(hn, w_ref[:, _GA0:_IN_END])).astype(BF16)


def _in_proj(x2, g, w_bf, tm):
    n = x2.shape[0]
    row = lambda w: pl.BlockSpec((tm, w), lambda i: (i, 0))
    full = lambda a: pl.BlockSpec(a.shape, lambda i: (0,) * a.ndim)
    return pl.pallas_call(
        _in_proj_kernel,
        grid=(n // tm,),
        in_specs=[row(D_MODEL), full(g), full(w_bf)],
        out_specs=[row(SSM_WIDTH), row(Q_WIDTH), row(2 * KV_WIDTH), row(D_MODEL), row(D_MODEL)],
        out_shape=[
            jax.ShapeDtypeStruct((n, SSM_WIDTH), F32),
            jax.ShapeDtypeStruct((n, Q_WIDTH), BF16),
            jax.ShapeDtypeStruct((n, 2 * KV_WIDTH), BF16),
            jax.ShapeDtypeStruct((n, D_MODEL), BF16),
            jax.ShapeDtypeStruct((n, D_MODEL), BF16),
        ],
        compiler_params=pltpu.CompilerParams(
            dimension_semantics=("arbitrary",), vmem_limit_bytes=VMEM_LIMIT),
        name="in_proj",
    )(x2, g, w_bf)


def _ssm_kernel(u_ref, um_ref, bm_ref, cm_ref, ar_ref, ai_ref, atr_ref, ati_ref, d_ref,
                y_ref, sre, sim, *, chunk, rows, batch):
    h = SSM_HALF
    bm = bm_ref[0]
    cm = cm_ref[0]
    ar, ai = ar_ref[0], ai_ref[0]
    atr, ati = atr_ref[0], ati_ref[0]
    dsk = d_ref[0]
    n_chunks = rows // batch

    def advance(sr, si, bu):
        return ar * sr - ai * si + bu[:, :h], ar * si + ai * sr + bu[:, h:]

    bum = _dot(um_ref[...].astype(BF16), bm)
    mr = jnp.zeros((1, h), F32)
    mi = jnp.zeros((1, h), F32)
    for j in range(N_META):
        mr, mi = advance(mr, mi, bum[j:j + 1, :])

    def u_step(t):
        return u_ref[pl.ds(t, rows, stride=chunk), :]

    sre[...] = jnp.zeros_like(sre)
    sim[...] = jnp.zeros_like(sim)

    def pass_a(t, carry):
        bu = _dot(u_step(t).astype(BF16), bm)
        nr, ni = advance(sre[...], sim[...], bu)
        sre[...] = nr
        sim[...] = ni
        return carry

    lax.fori_loop(0, chunk, pass_a, 0)

    def over_chunks(c, carry):
        new = []
        for b in range(batch):
            cr, ci = carry[2 * b], carry[2 * b + 1]
            row = pl.ds(b * n_chunks + c, 1)
            er, ei = sre[row, :], sim[row, :]
            sre[row, :] = cr
            sim[row, :] = ci
            new += [atr * cr - ati * ci + er, atr * ci + ati * cr + ei]
        return tuple(new)

    lax.fori_loop(0, n_chunks, over_chunks, (mr, mi) * batch)

    def pass_b(t, carry):
        ut = u_step(t)
        bu = _dot(ut.astype(BF16), bm)
        nr, ni = advance(sre[...], sim[...], bu)
        sre[...] = nr
        sim[...] = ni
        y = _dot(nr.astype(BF16), cm[:h, :]) + _dot(ni.astype(BF16), cm[h:, :]) + dsk * ut
        y_ref[pl.ds(t, rows, stride=chunk), :] = y
        return carry

    lax.fori_loop(0, chunk, pass_b, 0)


def _ssm(u, u_meta, bmat, cmat, a_re, a_im, at_re, at_im, dskip, batch, chunk):
    n = u.shape[0]
    rows = n // chunk
    nblk = SSM_WIDTH // SSM_CH_BLOCK
    col = lambda r: pl.BlockSpec((r, SSM_CH_BLOCK), lambda j: (0, j))
    par = lambda a: pl.BlockSpec((1,) + a.shape[1:], lambda j: (j, 0, 0))
    return pl.pallas_call(
        functools.partial(_ssm_kernel, chunk=chunk, rows=rows, batch=batch),
        grid=(nblk,),
        in_specs=[col(n), col(N_META), par(bmat), par(cmat), par(a_re), par(a_im),
                  par(at_re), par(at_im), par(dskip)],
        out_specs=col(n),
        out_shape=jax.ShapeDtypeStruct((n, SSM_WIDTH), F32),
        scratch_shapes=[pltpu.VMEM((rows, SSM_HALF), F32), pltpu.VMEM((rows, SSM_HALF), F32)],
        compiler_params=pltpu.CompilerParams(
            dimension_semantics=("arbitrary",), vmem_limit_bytes=VMEM_LIMIT),
        name="ssm",
    )(u, u_meta, bmat, cmat, a_re, a_im, at_re, at_im, dskip)


def _ssm_params(a_re, a_im, log_dt, b_re, b_im, c_re, c_im, d_skip, chunk):
    dt = jnp.exp(log_dt)[:, None]
    mag = jnp.exp(a_re * dt)
    ang = a_im * dt
    abar_re, abar_im = mag * jnp.cos(ang), mag * jnp.sin(ang)
    den = a_re * a_re + a_im * a_im
    nr, ni = abar_re - 1.0, abar_im
    coef_re = ((nr * a_re + ni * a_im) / den)[..., None]
    coef_im = ((ni * a_re - nr * a_im) / den)[..., None]
    bbar_re = coef_re * b_re - coef_im * b_im
    bbar_im = coef_re * b_im + coef_im * b_re
    magt = jnp.exp(a_re * dt * chunk)
    at_re, at_im = magt * jnp.cos(ang * chunk), magt * jnp.sin(ang * chunk)

    nblk = SSM_WIDTH // SSM_CH_BLOCK
    gpb = SSM_GROUPS // nblk
    eye = jnp.eye(gpb, dtype=F32)

    def in_map(b):
        b = b.reshape(nblk, gpb, SSM_STATE, SSM_GROUP)
        return jnp.einsum('jgpc,gh->jgchp', b, eye).reshape(nblk, SSM_CH_BLOCK, gpb * SSM_STATE)

    def out_map(c):
        c = c.reshape(nblk, gpb, SSM_GROUP, SSM_STATE)
        return jnp.einsum('jgcp,gh->jgphc', c, eye).reshape(nblk, gpb * SSM_STATE, SSM_CH_BLOCK)

    bmat = jnp.concatenate([in_map(bbar_re), in_map(bbar_im)], axis=2).astype(BF16)
    cmat = jnp.concatenate([out_map(c_re), -out_map(c_im)], axis=1).astype(BF16)
    vec = lambda v: v.reshape(nblk, 1, SSM_HALF)
    return (bmat, cmat, vec(abar_re), vec(abar_im), vec(at_re), vec(at_im),
            d_skip.reshape(nblk, 1, SSM_CH_BLOCK))


def _attn_kernel(sink_ref, q_ref, kvc_ref, kvp_ref, kvm_ref, o_ref, *, blocks_per_seq):
    n = pl.program_id(0) % blocks_per_seq
    w = WINDOW
    hd = HEAD_DIM
    qi = lax.broadcasted_iota(jnp.int32, (w, w), 0)
    lane = lax.broadcasted_iota(jnp.int32, (w, w), 1)
    vis_prev = (lane > qi) & (n > 0)
    vis_cur = lane <= qi
    left = lane < hd
    meta_l = lane < N_META
    meta_r = (lane >= N_META) & (lane < 2 * N_META)

    def placed(x_bf):
        x = x_bf.astype(F32)
        xr = pltpu.roll(x, hd, 1)
        lm = lax.broadcasted_iota(jnp.int32, x.shape, 1) < hd
        z = jnp.zeros_like(x)
        return {(0, 0): jnp.where(lm, x, z), (0, 1): jnp.where(lm, z, xr),
                (1, 0): jnp.where(lm, xr, z), (1, 1): jnp.where(lm, z, x)}

    kp, kc, km = placed(kvp_ref[:, :KV_WIDTH]), placed(kvc_ref[:, :KV_WIDTH]), placed(kvm_ref[:, :KV_WIDTH])
    vp, vc, vm = placed(kvp_ref[:, KV_WIDTH:]), placed(kvc_ref[:, KV_WIDTH:]), placed(kvm_ref[:, KV_WIDTH:])
    pad_rows = w - 2 * N_META
    zpad = jnp.zeros((pad_rows, w), F32)
    krow = lax.broadcasted_iota(jnp.int32, (5 * w, w), 0)
    klane = lax.broadcasted_iota(jnp.int32, (5 * w, w), 1)
    row_l = (krow < 2 * w) | ((krow >= 4 * w) & (krow < 4 * w + N_META))
    row_r = ((krow >= 2 * w) & (krow < 4 * w)) | ((krow >= 4 * w + N_META) & (krow < 4 * w + 2 * N_META))
    den_cols = jnp.where((row_l & (klane < hd)) | (row_r & (klane >= hd)), 1.0, 0.0)

    for j in range(N_KV_HEADS):
        kcat = jnp.concatenate(
            [kp[j, 0], kc[j, 0], kp[j, 1], kc[j, 1], km[j, 0], km[j, 1], zpad], axis=0).astype(BF16)
        vcat = jnp.concatenate(
            [jnp.concatenate([vp[j, 0], vc[j, 0], vp[j, 1], vc[j, 1], vm[j, 0], vm[j, 1], zpad], axis=0),
             den_cols], axis=1).astype(BF16)
        for r in range(KV_REP // 2):
            pr = j * (KV_REP // 2) + r
            s = _dot_nt(q_ref[:, pr * w:(pr + 1) * w], kcat)
            s_lp = jnp.where(vis_prev, s[:, 0:w], NEG_INF)
            s_lc = jnp.where(vis_cur, s[:, w:2 * w], NEG_INF)
            s_rp = jnp.where(vis_prev, s[:, 2 * w:3 * w], NEG_INF)
            s_rc = jnp.where(vis_cur, s[:, 3 * w:4 * w], NEG_INF)
            s_m = s[:, 4 * w:]
            sink_l, sink_r = sink_ref[2 * pr], sink_ref[2 * pr + 1]
            m_l = jnp.maximum(jnp.max(jnp.maximum(jnp.maximum(s_lp, s_lc), jnp.where(meta_l, s_m, NEG_INF)),
                                      axis=1, keepdims=True), sink_l)
            m_r = jnp.maximum(jnp.max(jnp.maximum(jnp.maximum(s_rp, s_rc), jnp.where(meta_r, s_m, NEG_INF)),
                                      axis=1, keepdims=True), sink_r)
            s_m = jnp.where(meta_l, s_m - m_l, jnp.where(meta_r, s_m - m_r, NEG_INF))
            e = jnp.concatenate([jnp.exp(s_lp - m_l), jnp.exp(s_lc - m_l), jnp.exp(s_rp - m_r),
                                 jnp.exp(s_rc - m_r), jnp.exp(s_m)], axis=1).astype(BF16)
            acc = _dot(e, vcat)
            den = acc[:, w:] + jnp.where(left, jnp.exp(sink_l - m_l), jnp.exp(sink_r - m_r))
            o_ref[:, pr * w:(pr + 1) * w] = (acc[:, :w] / den).astype(BF16)


def _attention(sinks, q, kv, kv_meta, blocks_per_seq):
    n = q.shape[0]
    return pl.pallas_call(
        functools.partial(_attn_kernel, blocks_per_seq=blocks_per_seq),
        grid=(n // WINDOW,),
        in_specs=[
            pl.BlockSpec(memory_space=pltpu.SMEM),
            pl.BlockSpec((WINDOW, Q_WIDTH), lambda g: (g, 0)),
            pl.BlockSpec((WINDOW, 2 * KV_WIDTH), lambda g: (g, 0)),
            pl.BlockSpec((WINDOW, 2 * KV_WIDTH), lambda g: (jnp.maximum(g - 1, 0), 0)),
            pl.BlockSpec((N_META, 2 * KV_WIDTH), lambda g: (0, 0)),
        ],
        out_specs=pl.BlockSpec((WINDOW, Q_WIDTH), lambda g: (g, 0)),
        out_shape=jax.ShapeDtypeStruct((n, Q_WIDTH), BF16),
        compiler_params=pltpu.CompilerParams(dimension_semantics=("arbitrary",)),
        name="attn",
    )(sinks, q, kv, kv, kv_meta)


_REC_IDX, _REC_RANK, _REC_GATE, _REC_W = 0, TOP_K, 2 * TOP_K, 128


def _mix_kernel(x_ref, y_ref, at_ref, gs_ref, ga_ref, wglu_ref, wo_ref, wout_ref, fg_ref, rw_ref, rb_ref,
                h1_ref, hf_ref, rec_ref, cnt_ref, cnt_scr):
    tm = x_ref.shape[0]

    @pl.when(pl.program_id(0) == 0)
    def _():
        cnt_scr[...] = jnp.zeros_like(cnt_scr)

    glu = _dot(jax.nn.gelu(y_ref[...]).astype(BF16), wglu_ref[...])
    branch_ssm = glu[:, :D_MODEL] * jax.nn.sigmoid(glu[:, D_MODEL:])
    branch_attn = _dot(at_ref[...], wo_ref[...])
    merged = gs_ref[...].astype(F32) * branch_ssm + ga_ref[...].astype(F32) * branch_attn
    h1 = x_ref[...] + _dot(merged.astype(BF16), wout_ref[...])
    h1_ref[...] = h1
    ms = jnp.mean(h1 * h1, axis=-1, keepdims=True)
    hf = h1 * lax.rsqrt(ms + RMS_EPS) * fg_ref[...]
    hf_ref[...] = hf

    logits = jnp.dot(hf, rw_ref[...], precision=lax.Precision.HIGHEST,
                     preferred_element_type=F32) + rb_ref[...]
    col = lax.broadcasted_iota(jnp.int32, (tm, N_EXPERTS), 1)
    vals, idxs, hots = [], [], []
    rest = logits
    for _ in range(TOP_K):
        m = jnp.max(rest, axis=1, keepdims=True)
        first = jnp.min(jnp.where(rest == m, col, N_EXPERTS), axis=1, keepdims=True)
        hot = col == first
        vals.append(m)
        idxs.append(first)
        hots.append(hot)
        rest = jnp.where(hot, -jnp.inf, rest)
    exps = [jnp.exp(v - vals[0]) for v in vals]
    tot = exps[0] + exps[1] + exps[2] + exps[3]

    sel = (hots[0] | hots[1] | hots[2] | hots[3]).astype(F32)
    ti = lax.broadcasted_iota(jnp.int32, (tm, tm), 0)
    tj = lax.broadcasted_iota(jnp.int32, (tm, tm), 1)
    lower = (tj < ti).astype(BF16)
    rank_e = _dot(lower, sel.astype(BF16)) + cnt_scr[...]
    cnt_scr[...] = cnt_scr[...] + jnp.sum(sel, axis=0, keepdims=True)
    cnt_ref[...] = cnt_scr[...]

    lane = lax.broadcasted_iota(jnp.int32, (tm, _REC_W), 1)
    rec = jnp.zeros((tm, _REC_W), F32)
    for k in range(TOP_K):
        rank_k = jnp.sum(jnp.where(hots[k], rank_e, 0.0), axis=1, keepdims=True)
        rec = jnp.where(lane == _REC_IDX + k, idxs[k].astype(F32), rec)
        rec = jnp.where(lane == _REC_RANK + k, rank_k, rec)
        rec = jnp.where(lane == _REC_GATE + k, exps[k] / tot, rec)
    rec_ref[...] = rec


def _mix(x2, y, attn, gs, ga, wglu, wo, wout, fg, rw, rb, tm):
    n = x2.shape[0]
    row = lambda w: pl.BlockSpec((tm, w), lambda i: (i, 0))
    full = lambda a: pl.BlockSpec(a.shape, lambda i: (0,) * a.ndim)
    return pl.pallas_call(
        _mix_kernel,
        grid=(n // tm,),
        in_specs=[row(D_MODEL), row(SSM_WIDTH), row(Q_WIDTH), row(D_MODEL), row(D_MODEL),
                  full(wglu), full(wo), full(wout), full(fg), full(rw), full(rb)],
        out_specs=[row(D_MODEL), row(D_MODEL), row(_REC_W),
                   pl.BlockSpec((1, N_EXPERTS), lambda i: (0, 0))],
        out_shape=[
            jax.ShapeDtypeStruct((n, D_MODEL), F32),
            jax.ShapeDtypeStruct((n, D_MODEL), F32),
            jax.ShapeDtypeStruct((n, _REC_W), F32),
            jax.ShapeDtypeStruct((1, N_EXPERTS), F32),
        ],
        scratch_shapes=[pltpu.VMEM((1, N_EXPERTS), F32)],
        compiler_params=pltpu.CompilerParams(
            dimension_semantics=("arbitrary",), vmem_limit_bytes=VMEM_LIMIT),
        name="mix_router",
    )(x2, y, attn, gs, ga, wglu, wo, wout, fg, rw, rb)


def _row_gather_copy(src_hbm, src_row, dst_buf, slot, dst_row, sem):
    return pltpu.make_async_copy(src_hbm.at[pl.ds(src_row, 1), :],
                                 dst_buf.at[slot, pl.ds(dst_row, 1), :], sem.at[slot])


def _slot_wait_copy(src_hbm, dst_buf, slot, sem):
    rows = dst_buf.shape[1]
    return pltpu.make_async_copy(src_hbm.at[pl.ds(0, rows), :], dst_buf.at[slot], sem.at[slot])


_ISSUE_UNROLL = 8


def _issue_rows(idx_ref, src_hbm, dst_buf, slot, sem):
    rows = dst_buf.shape[1]

    def body(o, carry):
        for r in range(_ISSUE_UNROLL):
            row = o * _ISSUE_UNROLL + r
            _row_gather_copy(src_hbm, idx_ref[row], dst_buf, slot, row, sem).start()
        return carry

    lax.fori_loop(0, rows // _ISSUE_UNROLL, body, 0)


def _issue_rows_inline(idx_ref, src_hbm, dst_buf, slot, sem):
    for row in range(dst_buf.shape[1]):
        _row_gather_copy(src_hbm, idx_ref[row], dst_buf, slot, row, sem).start(priority=row % 2)


def _expert_kernel(be_ref, nu_ref, tok_cur_ref, tok_nxt_ref, hf_hbm, wu_ref, bu_ref, wd_ref, bd_ref,
                   y_ref, xbuf, sem, wu_bf, wd_bf):
    i = pl.program_id(0)
    last = pl.num_programs(0) - 1
    n_used = nu_ref[0]
    slot = i % 2

    @pl.when(i == 0)
    def _():
        _issue_rows(tok_cur_ref, hf_hbm, xbuf, 0, sem)

    @pl.when((i == 0) | (be_ref[i] != be_ref[jnp.maximum(i - 1, 0)]))
    def _():
        wu_bf[...] = wu_ref[0].astype(BF16)
        wd_bf[...] = wd_ref[0].astype(BF16)

    @pl.when(i < n_used)
    def _():
        _issue_rows_inline(tok_nxt_ref, hf_hbm, xbuf, 1 - slot, sem)
        _slot_wait_copy(hf_hbm, xbuf, slot, sem).wait()
        up = _dot(xbuf[slot].astype(BF16), wu_bf[...]) + bu_ref[0]
        x_glu = jnp.minimum(up[:, :D_FF], SWIGLU_LIMIT)
        x_lin = jnp.clip(up[:, D_FF:], -SWIGLU_LIMIT, SWIGLU_LIMIT)
        act = x_glu * jax.nn.sigmoid(SWIGLU_ALPHA * x_glu) * (x_lin + 1.0)
        y_ref[...] = _dot(act.astype(BF16), wd_bf[...]) + bd_ref[0]

        @pl.when(i == last)
        def _():
            _slot_wait_copy(hf_hbm, xbuf, 1 - slot, sem).wait()

    @pl.when(i >= n_used)
    def _():
        @pl.when(i == n_used)
        def _():
            _slot_wait_copy(hf_hbm, xbuf, slot, sem).wait()

        y_ref[...] = jnp.zeros_like(y_ref)


def _experts(block_expert, n_used, row_tok, hf, w_up, b_up, w_down, b_down):
    n_blocks = block_expert.shape[0]
    tm = EXPERT_ROWS
    last = n_blocks - 1
    grid_spec = pltpu.PrefetchScalarGridSpec(
        num_scalar_prefetch=2,
        grid=(n_blocks,),
        in_specs=[
            pl.BlockSpec((tm,), lambda i, be, nu: (i,), memory_space=pltpu.SMEM),
            pl.BlockSpec((tm,), lambda i, be, nu: (jnp.minimum(i + 1, last),), memory_space=pltpu.SMEM),
            pl.BlockSpec(memory_space=pl.ANY),
            pl.BlockSpec((1, D_MODEL, 2 * D_FF), lambda i, be, nu: (be[i], 0, 0)),
            pl.BlockSpec((1, 1, 2 * D_FF), lambda i, be, nu: (be[i], 0, 0)),
            pl.BlockSpec((1, D_FF, D_MODEL), lambda i, be, nu: (be[i], 0, 0)),
            pl.BlockSpec((1, 1, D_MODEL), lambda i, be, nu: (be[i], 0, 0)),
        ],
        out_specs=pl.BlockSpec((tm, D_MODEL), lambda i, be, nu: (i, 0)),
        scratch_shapes=[
            pltpu.VMEM((2, tm, D_MODEL), F32),
            pltpu.SemaphoreType.DMA((2,)),
            pltpu.VMEM((D_MODEL, 2 * D_FF), BF16),
            pltpu.VMEM((D_FF, D_MODEL), BF16),
        ],
    )
    return pl.pallas_call(
        _expert_kernel,
        grid_spec=grid_spec,
        out_shape=jax.ShapeDtypeStruct((n_blocks * tm, D_MODEL), F32),
        compiler_params=pltpu.CompilerParams(
            dimension_semantics=("arbitrary",), vmem_limit_bytes=VMEM_LIMIT),
        name="experts",
    )(block_expert, n_used, row_tok, row_tok, hf, w_up, b_up[:, None, :], w_down, b_down[:, None, :])


def _combine_kernel(dst_cur_ref, dst_nxt_ref, y_hbm, h1_ref, rec_ref, g_ref, o_ref, gbuf, sem):
    i = pl.program_id(0)
    last = pl.num_programs(0) - 1
    tm = h1_ref.shape[0]
    slot = i % 2

    @pl.when(i == 0)
    def _():
        _issue_rows(dst_cur_ref, y_hbm, gbuf, 0, sem)

    _issue_rows_inline(dst_nxt_ref, y_hbm, gbuf, 1 - slot, sem)
    _slot_wait_copy(y_hbm, gbuf, slot, sem).wait()
    acc = h1_ref[...]
    for k in range(TOP_K):
        gate = rec_ref[:, _REC_GATE + k:_REC_GATE + k + 1]
        acc = acc + gate * gbuf[slot, pl.ds(k * tm, tm), :]
    ms = jnp.mean(acc * acc, axis=-1, keepdims=True)
    o_ref[...] = acc * lax.rsqrt(ms + RMS_EPS) * g_ref[...]

    @pl.when(i == last)
    def _():
        _slot_wait_copy(y_hbm, gbuf, 1 - slot, sem).wait()


def _combine(dest_kmajor, y, h1, rec, g, tm):
    n = h1.shape[0]
    n_tiles = n // tm
    return pl.pallas_call(
        _combine_kernel,
        grid=(n_tiles,),
        in_specs=[
            pl.BlockSpec((TOP_K * tm,), lambda i: (i,), memory_space=pltpu.SMEM),
            pl.BlockSpec((TOP_K * tm,), lambda i: (jnp.minimum(i + 1, n_tiles - 1),), memory_space=pltpu.SMEM),
            pl.BlockSpec(memory_space=pl.ANY),
            pl.BlockSpec((tm, D_MODEL), lambda i: (i, 0)),
            pl.BlockSpec((tm, _REC_W), lambda i: (i, 0)),
            pl.BlockSpec((1, D_MODEL), lambda i: (0, 0)),
        ],
        out_specs=pl.BlockSpec((tm, D_MODEL), lambda i: (i, 0)),
        out_shape=jax.ShapeDtypeStruct((n, D_MODEL), F32),
        scratch_shapes=[pltpu.VMEM((2, TOP_K * tm, D_MODEL), F32),
                        pltpu.SemaphoreType.DMA((2,))],
        compiler_params=pltpu.CompilerParams(
            dimension_semantics=("arbitrary",), vmem_limit_bytes=VMEM_LIMIT),
        name="combine",
    )(dest_kmajor, dest_kmajor, y, h1, rec, g)


def _routing_tables(rec, counts, n, tm_combine):
    tm = EXPERT_ROWS
    n_blocks = (n * TOP_K + N_EXPERTS * (tm - 1)) // tm
    idx = rec[:, _REC_IDX:_REC_IDX + TOP_K].astype(jnp.int32)
    rank = rec[:, _REC_RANK:_REC_RANK + TOP_K].astype(jnp.int32)
    cnt = counts[0].astype(jnp.int32)
    blocks_e = (cnt + tm - 1) // tm
    blocks_end = jnp.cumsum(blocks_e)
    row_start = (blocks_end - blocks_e) * tm
    n_used = blocks_end[-1]
    dest = row_start[idx] + rank
    blk = jnp.arange(n_blocks, dtype=jnp.int32)
    be = jnp.minimum(jnp.sum(blocks_end[None, :] <= blk[:, None], axis=1), N_EXPERTS - 1).astype(jnp.int32)
    be = jnp.where(blk < n_used, be, be[jnp.maximum(n_used - 1, 0)])
    tok = jnp.broadcast_to(jnp.arange(n, dtype=jnp.int32)[:, None], (n, TOP_K))
    row_tok = jnp.zeros((n_blocks * tm,), jnp.int32).at[dest.reshape(-1)].set(tok.reshape(-1))
    dest_kmajor = dest.reshape(n // tm_combine, tm_combine, TOP_K).transpose(0, 2, 1).reshape(-1)
    return be, n_used.reshape(1).astype(jnp.int32), row_tok, dest_kmajor


def kernel(x, meta_tokens, mix_norm_g, w_in, ssm_a_re, ssm_a_im, ssm_log_dt, ssm_b_re, ssm_b_im,
           ssm_c_re, ssm_c_im, ssm_d, w_ssm_glu, attn_sinks, w_attn_o, w_out, ffn_norm_g,
           router_w, router_b, w_up, b_up, w_down, b_down, final_norm_g):
    bsz, seq, d = x.shape
    assert d == D_MODEL and seq % max(WINDOW, SSM_CHUNK) == 0
    assert mix_norm_g.shape[0] == 1, "single-layer trunk"
    n = bsz * seq
    tm_proj = min(512, n)
    tm_mix = min(256, n)
    tm_comb = min(128, n)
    x2 = x.reshape(n, D_MODEL)

    w_in_bf = w_in[0].astype(BF16)
    g_mix = mix_norm_g[0][None, :]
    u, q, kv, gs, ga = _in_proj(x2, g_mix, w_in_bf, tm_proj)
    u_m, _, kv_m, _, _ = _in_proj(meta_tokens, g_mix, w_in_bf, N_META)

    ssm_par = _ssm_params(ssm_a_re[0], ssm_a_im[0], ssm_log_dt[0], ssm_b_re[0], ssm_b_im[0],
                          ssm_c_re[0], ssm_c_im[0], ssm_d[0], SSM_CHUNK)
    y_ssm = _ssm(u, u_m, *ssm_par, batch=bsz, chunk=SSM_CHUNK)

    attn = _attention(attn_sinks[0], q, kv, kv_m, seq // WINDOW)

    h1, hf, rec, counts = _mix(
        x2, y_ssm, attn, gs, ga, w_ssm_glu[0].astype(BF16), w_attn_o[0].astype(BF16),
        w_out[0].astype(BF16), ffn_norm_g[0][None, :], router_w[0], router_b[0][None, :], tm_mix)

    be, n_used, row_tok, dest_kmajor = _routing_tables(rec, counts, n, tm_comb)
    y = _experts(be, n_used, row_tok, hf, w_up[0], b_up[0], w_down[0], b_down[0])
    out = _combine(dest_kmajor, y, h1, rec, final_norm_g[None, :], tm_comb)
    return out.reshape(bsz, seq, D_MODEL)
```

```python
import functools
import math

import jax
import jax.numpy as jnp
from jax import lax
from jax.experimental import pallas as pl
from jax.experimental.pallas import tpu as pltpu

F32 = jnp.float32
BF16 = jnp.bfloat16

D_MODEL = 1024
N_META = 16
SSM_WIDTH = 512
SSM_GROUP = 16
SSM_GROUPS = 32
SSM_STATE = 64
HEAD_DIM = 64
N_HEADS = 16
N_KV_HEADS = 2
KV_REP = N_HEADS // N_KV_HEADS
WINDOW = 128
Q_WIDTH = N_HEADS * HEAD_DIM
KV_WIDTH = N_KV_HEADS * HEAD_DIM
N_EXPERTS = 32
TOP_K = 4
D_FF = 1024
SWIGLU_ALPHA = 1.702
SWIGLU_LIMIT = 7.0
RMS_EPS = 1e-5
NEG_INF = -1e30

_U0, _Q0, _KV0, _GS0, _GA0, _IN_END = 0, 512, 1536, 1792, 2816, 3840

SSM_CH_BLOCK = 128
SSM_HALF = (SSM_CH_BLOCK // SSM_GROUP) * SSM_STATE
SSM_CHUNK = 32
EXPERT_ROWS = 256
VMEM_LIMIT = 56 * 1024 * 1024


def _dot(a, b):
    return jnp.dot(a, b, preferred_element_type=F32)


def _dot_nt(a, b):
    return lax.dot_general(a, b, (((1,), (1,)), ((), ())), preferred_element_type=F32)


LANES = 128
TILE_ROWS = D_MODEL // LANES


def _store_token_tiles(ref, x):
    rows = x.shape[0]
    for j in range(TILE_ROWS):
        ref[pl.ds(j, rows, stride=TILE_ROWS), :] = x[:, j * LANES:(j + 1) * LANES]


def _load_token_tiles(ref, start_row, rows):
    return jnp.concatenate(
        [ref[pl.ds(start_row * TILE_ROWS + j, rows, stride=TILE_ROWS), :] for j in range(TILE_ROWS)], axis=1)


def _token_tile(ref, row):
    return ref.at[pl.ds(pl.multiple_of(row * TILE_ROWS, TILE_ROWS), TILE_ROWS), :]


def _in_proj_kernel(x_ref, g_ref, w_ref, u_ref, q_ref, kv_ref, gs_ref, ga_ref):
    x = x_ref[...]
    ms = jnp.mean(x * x, axis=-1, keepdims=True)
    hn = (x * lax.rsqrt(ms + RMS_EPS) * g_ref[...]).astype(BF16)
    u_ref[...] = _dot(hn, w_ref[:, _U0:_Q0])
    q_ref[...] = (_dot(hn, w_ref[:, _Q0:_KV0]) * (HEAD_DIM ** -0.5)).astype(BF16)
    kv_ref[...] = _dot(hn, w_ref[:, _KV0:_GS0]).astype(BF16)
    gs_ref[...] = jax.nn.sigmoid(_dot(hn, w_ref[:, _GS0:_GA0])).astype(BF16)
    ga_ref[...] = jax.nn.sigmoid(_dot(hn, w_ref[:, _GA0:_IN_END])).astype(BF16)


def _in_proj(x2, g, w_bf, tm):
    n = x2.shape[0]
    row = lambda w: pl.BlockSpec((tm, w), lambda i: (i, 0))
    full = lambda a: pl.BlockSpec(a.shape, lambda i: (0,) * a.ndim)
    return pl.pallas_call(
        _in_proj_kernel,
        grid=(n // tm,),
        in_specs=[row(D_MODEL), full(g), full(w_bf)],
        out_specs=[row(SSM_WIDTH), row(Q_WIDTH), row(2 * KV_WIDTH), row(D_MODEL), row(D_MODEL)],
        out_shape=[
            jax.ShapeDtypeStruct((n, SSM_WIDTH), F32),
            jax.ShapeDtypeStruct((n, Q_WIDTH), BF16),
            jax.ShapeDtypeStruct((n, 2 * KV_WIDTH), BF16),
            jax.ShapeDtypeStruct((n, D_MODEL), BF16),
            jax.ShapeDtypeStruct((n, D_MODEL), BF16),
        ],
        compiler_params=pltpu.CompilerParams(
            dimension_semantics=("arbitrary",), vmem_limit_bytes=VMEM_LIMIT),
        name="in_proj",
    )(x2, g, w_bf)


def _ssm_kernel(u_ref, um_ref, bm_ref, cm_ref, ar_ref, ai_ref, atr_ref, ati_ref, d_ref,
                y_ref, sre, sim, *, chunk, rows, batch):
    h = SSM_HALF
    bm = bm_ref[0]
    cm = cm_ref[0]
    ar, ai = ar_ref[0], ai_ref[0]
    atr, ati = atr_ref[0], ati_ref[0]
    dsk = d_ref[0]
    n_chunks = rows // batch

    def advance(sr, si, bu):
        return ar * sr - ai * si + bu[:, :h], ar * si + ai * sr + bu[:, h:]

    bum = _dot(um_ref[...].astype(BF16), bm)
    mr = jnp.zeros((1, h), F32)
    mi = jnp.zeros((1, h), F32)
    for j in range(N_META):
        mr, mi = advance(mr, mi, bum[j:j + 1, :])

    def u_step(t):
        return u_ref[pl.ds(t, rows, stride=chunk), :]

    sre[...] = jnp.zeros_like(sre)
    sim[...] = jnp.zeros_like(sim)

    def pass_a(t, carry):
        bu = _dot(u_step(t).astype(BF16), bm)
        nr, ni = advance(sre[...], sim[...], bu)
        sre[...] = nr
        sim[...] = ni
        return carry

    lax.fori_loop(0, chunk, pass_a, 0)

    def over_chunks(c, carry):
        new = []
        for b in range(batch):
            cr, ci = carry[2 * b], carry[2 * b + 1]
            row = pl.ds(b * n_chunks + c, 1)
            er, ei = sre[row, :], sim[row, :]
            sre[row, :] = cr
            sim[row, :] = ci
            new += [atr * cr - ati * ci + er, atr * ci + ati * cr + ei]
        return tuple(new)

    lax.fori_loop(0, n_chunks, over_chunks, (mr, mi) * batch)

    def pass_b(t, carry):
        ut = u_step(t)
        bu = _dot(ut.astype(BF16), bm)
        nr, ni = advance(sre[...], sim[...], bu)
        sre[...] = nr
        sim[...] = ni
        y = _dot(nr.astype(BF16), cm[:h, :]) + _dot(ni.astype(BF16), cm[h:, :]) + dsk * ut
        y_ref[pl.ds(t, rows, stride=chunk), :] = y
        return carry

    lax.fori_loop(0, chunk, pass_b, 0)


def _ssm(u, u_meta, bmat, cmat, a_re, a_im, at_re, at_im, dskip, batch, chunk):
    n = u.shape[0]
    rows = n // chunk
    nblk = SSM_WIDTH // SSM_CH_BLOCK
    col = lambda r: pl.BlockSpec((r, SSM_CH_BLOCK), lambda j: (0, j))
    par = lambda a: pl.BlockSpec((1,) + a.shape[1:], lambda j: (j, 0, 0))
    return pl.pallas_call(
        functools.partial(_ssm_kernel, chunk=chunk, rows=rows, batch=batch),
        grid=(nblk,),
        in_specs=[col(n), col(N_META), par(bmat), par(cmat), par(a_re), par(a_im),
                  par(at_re), par(at_im), par(dskip)],
        out_specs=col(n),
        out_shape=jax.ShapeDtypeStruct((n, SSM_WIDTH), F32),
        scratch_shapes=[pltpu.VMEM((rows, SSM_HALF), F32), pltpu.VMEM((rows, SSM_HALF), F32)],
        compiler_params=pltpu.CompilerParams(
            dimension_semantics=("arbitrary",), vmem_limit_bytes=VMEM_LIMIT),
        name="ssm",
    )(u, u_meta, bmat, cmat, a_re, a_im, at_re, at_im, dskip)


def _ssm_params(a_re, a_im, log_dt, b_re, b_im, c_re, c_im, d_skip, chunk):
    dt = jnp.exp(log_dt)[:, None]
    mag = jnp.exp(a_re * dt)
    ang = a_im * dt
    abar_re, abar_im = mag * jnp.cos(ang), mag * jnp.sin(ang)
    den = a_re * a_re + a_im * a_im
    nr, ni = abar_re - 1.0, abar_im
    coef_re = ((nr * a_re + ni * a_im) / den)[..., None]
    coef_im = ((ni * a_re - nr * a_im) / den)[..., None]
    bbar_re = coef_re * b_re - coef_im * b_im
    bbar_im = coef_re * b_im + coef_im * b_re
    magt = jnp.exp(a_re * dt * chunk)
    at_re, at_im = magt * jnp.cos(ang * chunk), magt * jnp.sin(ang * chunk)

    nblk = SSM_WIDTH // SSM_CH_BLOCK
    gpb = SSM_GROUPS // nblk
    eye = jnp.eye(gpb, dtype=F32)

    def in_map(b):
        b = b.reshape(nblk, gpb, SSM_STATE, SSM_GROUP)
        return jnp.einsum('jgpc,gh->jgchp', b, eye).reshape(nblk, SSM_CH_BLOCK, gpb * SSM_STATE)

    def out_map(c):
        c = c.reshape(nblk, gpb, SSM_GROUP, SSM_STATE)
        return jnp.einsum('jgcp,gh->jgphc', c, eye).reshape(nblk, gpb * SSM_STATE, SSM_CH_BLOCK)

    bmat = jnp.concatenate([in_map(bbar_re), in_map(bbar_im)], axis=2).astype(BF16)
    cmat = jnp.concatenate([out_map(c_re), -out_map(c_im)], axis=1).astype(BF16)
    vec = lambda v: v.reshape(nblk, 1, SSM_HALF)
    return (bmat, cmat, vec(abar_re), vec(abar_im), vec(at_re), vec(at_im),
            d_skip.reshape(nblk, 1, SSM_CH_BLOCK))


def _attn_kernel(sink_ref, q_ref, kvc_ref, kvp_ref, kvm_ref, o_ref, *, blocks_per_seq):
    n = pl.program_id(0) % blocks_per_seq
    w = WINDOW
    hd = HEAD_DIM
    qi = lax.broadcasted_iota(jnp.int32, (w, w), 0)
    lane = lax.broadcasted_iota(jnp.int32, (w, w), 1)
    vis_prev = (lane > qi) & (n > 0)
    vis_cur = lane <= qi
    left = lane < hd
    meta_l = lane < N_META
    meta_r = (lane >= N_META) & (lane < 2 * N_META)

    def placed(x_bf):
        x = x_bf.astype(F32)
        xr = pltpu.roll(x, hd, 1)
        lm = lax.broadcasted_iota(jnp.int32, x.shape, 1) < hd
        z = jnp.zeros_like(x)
        return {(0, 0): jnp.where(lm, x, z), (0, 1): jnp.where(lm, z, xr),
                (1, 0): jnp.where(lm, xr, z), (1, 1): jnp.where(lm, z, x)}

    kp, kc, km = placed(kvp_ref[:, :KV_WIDTH]), placed(kvc_ref[:, :KV_WIDTH]), placed(kvm_ref[:, :KV_WIDTH])
    vp, vc, vm = placed(kvp_ref[:, KV_WIDTH:]), placed(kvc_ref[:, KV_WIDTH:]), placed(kvm_ref[:, KV_WIDTH:])
    pad_rows = w - 2 * N_META
    zpad = jnp.zeros((pad_rows, w), F32)
    krow = lax.broadcasted_iota(jnp.int32, (5 * w, w), 0)
    klane = lax.broadcasted_iota(jnp.int32, (5 * w, w), 1)
    row_l = (krow < 2 * w) | ((krow >= 4 * w) & (krow < 4 * w + N_META))
    row_r = ((krow >= 2 * w) & (krow < 4 * w)) | ((krow >= 4 * w + N_META) & (krow < 4 * w + 2 * N_META))
    den_cols = jnp.where((row_l & (klane < hd)) | (row_r & (klane >= hd)), 1.0, 0.0)

    for j in range(N_KV_HEADS):
        kcat = jnp.concatenate(
            [kp[j, 0], kc[j, 0], kp[j, 1], kc[j, 1], km[j, 0], km[j, 1], zpad], axis=0).astype(BF16)
        vcat = jnp.concatenate(
            [jnp.concatenate([vp[j, 0], vc[j, 0], vp[j, 1], vc[j, 1], vm[j, 0], vm[j, 1], zpad], axis=0),
             den_cols], axis=1).astype(BF16)
        for r in range(KV_REP // 2):
            pr = j * (KV_REP // 2) + r
            s = _dot_nt(q_ref[:, pr * w:(pr + 1) * w], kcat)
            s_lp = jnp.where(vis_prev, s[:, 0:w], NEG_INF)
            s_lc = jnp.where(vis_cur, s[:, w:2 * w], NEG_INF)
            s_rp = jnp.where(vis_prev, s[:, 2 * w:3 * w], NEG_INF)
            s_rc = jnp.where(vis_cur, s[:, 3 * w:4 * w], NEG_INF)
            s_m = s[:, 4 * w:]
            sink_l, sink_r = sink_ref[2 * pr], sink_ref[2 * pr + 1]
            m_l = jnp.maximum(jnp.max(jnp.maximum(jnp.maximum(s_lp, s_lc), jnp.where(meta_l, s_m, NEG_INF)),
                                      axis=1, keepdims=True), sink_l)
            m_r = jnp.maximum(jnp.max(jnp.maximum(jnp.maximum(s_rp, s_rc), jnp.where(meta_r, s_m, NEG_INF)),
                                      axis=1, keepdims=True), sink_r)
            s_m = jnp.where(meta_l, s_m - m_l, jnp.where(meta_r, s_m - m_r, NEG_INF))
            e = jnp.concatenate([jnp.exp(s_lp - m_l), jnp.exp(s_lc - m_l), jnp.exp(s_rp - m_r),
                                 jnp.exp(s_rc - m_r), jnp.exp(s_m)], axis=1).astype(BF16)
            acc = _dot(e, vcat)
            den = acc[:, w:] + jnp.where(left, jnp.exp(sink_l - m_l), jnp.exp(sink_r - m_r))
            o_ref[:, pr * w:(pr + 1) * w] = (acc[:, :w] / den).astype(BF16)


def _attention(sinks, q, kv, kv_meta, blocks_per_seq):
    n = q.shape[0]
    return pl.pallas_call(
        functools.partial(_attn_kernel, blocks_per_seq=blocks_per_seq),
        grid=(n // WINDOW,),
        in_specs=[
            pl.BlockSpec(memory_space=pltpu.SMEM),
            pl.BlockSpec((WINDOW, Q_WIDTH), lambda g: (g, 0)),
            pl.BlockSpec((WINDOW, 2 * KV_WIDTH), lambda g: (g, 0)),
            pl.BlockSpec((WINDOW, 2 * KV_WIDTH), lambda g: (jnp.maximum(g - 1, 0), 0)),
            pl.BlockSpec((N_META, 2 * KV_WIDTH), lambda g: (0, 0)),
        ],
        out_specs=pl.BlockSpec((WINDOW, Q_WIDTH), lambda g: (g, 0)),
        out_shape=jax.ShapeDtypeStruct((n, Q_WIDTH), BF16),
        compiler_params=pltpu.CompilerParams(dimension_semantics=("arbitrary",)),
        name="attn",
    )(sinks, q, kv, kv, kv_meta)


_REC_IDX, _REC_RANK, _REC_GATE, _REC_W = 0, TOP_K, 2 * TOP_K, 128


def _mix_kernel(x_ref, y_ref, at_ref, gs_ref, ga_ref, wglu_ref, wo_ref, wout_ref, fg_ref, rw_ref, rb_ref,
                h1_ref, hf_ref, rec_ref, cnt_ref, cnt_scr):
    tm = x_ref.shape[0]

    @pl.when(pl.program_id(0) == 0)
    def _():
        cnt_scr[...] = jnp.zeros_like(cnt_scr)

    glu = _dot(jax.nn.gelu(y_ref[...]).astype(BF16), wglu_ref[...])
    branch_ssm = glu[:, :D_MODEL] * jax.nn.sigmoid(glu[:, D_MODEL:])
    branch_attn = _dot(at_ref[...], wo_ref[...])
    merged = gs_ref[...].astype(F32) * branch_ssm + ga_ref[...].astype(F32) * branch_attn
    h1 = x_ref[...] + _dot(merged.astype(BF16), wout_ref[...])
    h1_ref[...] = h1
    ms = jnp.mean(h1 * h1, axis=-1, keepdims=True)
    hf = h1 * lax.rsqrt(ms + RMS_EPS) * fg_ref[...]
    _store_token_tiles(hf_ref, hf)

    logits = jnp.dot(hf, rw_ref[...], precision=lax.Precision.HIGHEST,
                     preferred_element_type=F32) + rb_ref[...]
    col = lax.broadcasted_iota(jnp.int32, (tm, N_EXPERTS), 1)
    vals, idxs, hots = [], [], []
    rest = logits
    for _ in range(TOP_K):
        m = jnp.max(rest, axis=1, keepdims=True)
        first = jnp.min(jnp.where(rest == m, col, N_EXPERTS), axis=1, keepdims=True)
        hot = col == first
        vals.append(m)
        idxs.append(first)
        hots.append(hot)
        rest = jnp.where(hot, -jnp.inf, rest)
    exps = [jnp.exp(v - vals[0]) for v in vals]
    tot = exps[0] + exps[1] + exps[2] + exps[3]

    sel = (hots[0] | hots[1] | hots[2] | hots[3]).astype(F32)
    ti = lax.broadcasted_iota(jnp.int32, (tm, tm), 0)
    tj = lax.broadcasted_iota(jnp.int32, (tm, tm), 1)
    lower = (tj < ti).astype(BF16)
    rank_e = _dot(lower, sel.astype(BF16)) + cnt_scr[...]
    cnt_scr[...] = cnt_scr[...] + jnp.sum(sel, axis=0, keepdims=True)
    cnt_ref[...] = cnt_scr[...]

    lane = lax.broadcasted_iota(jnp.int32, (tm, _REC_W), 1)
    rec = jnp.zeros((tm, _REC_W), F32)
    for k in range(TOP_K):
        rank_k = jnp.sum(jnp.where(hots[k], rank_e, 0.0), axis=1, keepdims=True)
        rec = jnp.where(lane == _REC_IDX + k, idxs[k].astype(F32), rec)
        rec = jnp.where(lane == _REC_RANK + k, rank_k, rec)
        rec = jnp.where(lane == _REC_GATE + k, exps[k] / tot, rec)
    rec_ref[...] = rec


def _mix(x2, y, attn, gs, ga, wglu, wo, wout, fg, rw, rb, tm):
    n = x2.shape[0]
    row = lambda w: pl.BlockSpec((tm, w), lambda i: (i, 0))
    full = lambda a: pl.BlockSpec(a.shape, lambda i: (0,) * a.ndim)
    return pl.pallas_call(
        _mix_kernel,
        grid=(n // tm,),
        in_specs=[row(D_MODEL), row(SSM_WIDTH), row(Q_WIDTH), row(D_MODEL), row(D_MODEL),
                  full(wglu), full(wo), full(wout), full(fg), full(rw), full(rb)],
        out_specs=[row(D_MODEL), pl.BlockSpec((tm * TILE_ROWS, LANES), lambda i: (i, 0)), row(_REC_W),
                   pl.BlockSpec((1, N_EXPERTS), lambda i: (0, 0))],
        out_shape=[
            jax.ShapeDtypeStruct((n, D_MODEL), F32),
            jax.ShapeDtypeStruct((n * TILE_ROWS, LANES), F32),
            jax.ShapeDtypeStruct((n, _REC_W), F32),
            jax.ShapeDtypeStruct((1, N_EXPERTS), F32),
        ],
        scratch_shapes=[pltpu.VMEM((1, N_EXPERTS), F32)],
        compiler_params=pltpu.CompilerParams(
            dimension_semantics=("arbitrary",), vmem_limit_bytes=VMEM_LIMIT),
        name="mix_router",
    )(x2, y, attn, gs, ga, wglu, wo, wout, fg, rw, rb)


def _tiles_wait_copy(src_hbm, dst, n_tiles, sem):
    rows = n_tiles * TILE_ROWS
    return pltpu.make_async_copy(src_hbm.at[pl.ds(0, rows), :], dst.at[pl.ds(0, rows), :], sem)


_ISSUE_UNROLL = 16


def _dispatch_kernel(dst_ref, pad_start_ref, pad_len_ref, nu_ref, hf_hbm, xs_hbm, zero_blk, sem, pad_sem):
    i = pl.program_id(0)
    last = pl.num_programs(0) - 1
    pairs = dst_ref.shape[0]
    slot = i % 2
    n_blocks = xs_hbm.shape[0] // (EXPERT_ROWS * TILE_ROWS)

    def zero_copy(row, rows):
        src = zero_blk.at[pl.ds(0, rows * TILE_ROWS), :]
        dst = xs_hbm.at[pl.ds(pl.multiple_of(row * TILE_ROWS, TILE_ROWS), rows * TILE_ROWS), :]
        return pltpu.make_async_copy(src, dst, pad_sem)

    def for_each_pad(fn):
        def per_expert(e, carry):
            row, left = pad_start_ref[e], pad_len_ref[e]
            size = EXPERT_ROWS // 2
            while size >= 1:
                take = left & size

                @pl.when(take != 0)
                def _(row=row, size=size):
                    fn(zero_copy(row, size))

                row = row + take
                size //= 2
            return carry
        lax.fori_loop(0, N_EXPERTS, per_expert, 0)

        def per_block(b, carry):
            fn(zero_copy(b * EXPERT_ROWS, EXPERT_ROWS))
            return carry
        lax.fori_loop(nu_ref[0], n_blocks, per_block, 0)

    @pl.when(i == 0)
    def _():
        zero_blk[...] = jnp.zeros_like(zero_blk)
        for_each_pad(lambda cp: cp.start())

    def issue(o, carry):
        dsts = [dst_ref[o * _ISSUE_UNROLL + r] for r in range(_ISSUE_UNROLL)]
        tok0 = i * (pairs // TOP_K) + o * (_ISSUE_UNROLL // TOP_K)
        for r in range(_ISSUE_UNROLL):
            pltpu.make_async_copy(_token_tile(hf_hbm, tok0 + r // TOP_K), _token_tile(xs_hbm, dsts[r]),
                                  sem.at[slot]).start(priority=r % 2)
        return carry

    lax.fori_loop(0, pairs // _ISSUE_UNROLL, issue, 0)

    @pl.when(i > 0)
    def _():
        _tiles_wait_copy(hf_hbm, xs_hbm, pairs, sem.at[1 - slot]).wait()

    @pl.when(i == last)
    def _():
        _tiles_wait_copy(hf_hbm, xs_hbm, pairs, sem.at[slot]).wait()
        for_each_pad(lambda cp: cp.wait())


def _dispatch(dest, pad_start, pad_len, n_used, hf_tiles, n_rows, tokens_per_step):
    n = dest.shape[0] // TOP_K
    pairs = tokens_per_step * TOP_K
    assert TOP_K == 4 and n % tokens_per_step == 0 and n >= pairs
    smem = lambda: pl.BlockSpec(memory_space=pltpu.SMEM)
    return pl.pallas_call(
        _dispatch_kernel,
        grid=(n // tokens_per_step,),
        in_specs=[pl.BlockSpec((pairs,), lambda i: (i,), memory_space=pltpu.SMEM), smem(), smem(), smem(),
                  pl.BlockSpec(memory_space=pl.ANY)],
        out_specs=pl.BlockSpec(memory_space=pl.ANY),
        out_shape=jax.ShapeDtypeStruct((n_rows * TILE_ROWS, LANES), F32),
        scratch_shapes=[pltpu.VMEM((EXPERT_ROWS * TILE_ROWS, LANES), F32), pltpu.SemaphoreType.DMA((2,)),
                        pltpu.SemaphoreType.DMA],
        compiler_params=pltpu.CompilerParams(dimension_semantics=("arbitrary",)),
        name="dispatch",
    )(dest, pad_start, pad_len, n_used, hf_tiles)


def _expert_kernel(be_ref, nu_ref, xs_ref, wu_ref, bu_ref, wd_ref, bd_ref, y_ref, wu_bf, wd_bf):
    i = pl.program_id(0)

    @pl.when((i == 0) | (be_ref[i] != be_ref[jnp.maximum(i - 1, 0)]))
    def _():
        wu_bf[...] = wu_ref[0].astype(BF16)
        wd_bf[...] = wd_ref[0].astype(BF16)

    @pl.when(i < nu_ref[0])
    def _():
        xb = _load_token_tiles(xs_ref, 0, EXPERT_ROWS).astype(BF16)
        up = _dot(xb, wu_bf[...]) + bu_ref[0]
        x_glu = jnp.minimum(up[:, :D_FF], SWIGLU_LIMIT)
        x_lin = jnp.clip(up[:, D_FF:], -SWIGLU_LIMIT, SWIGLU_LIMIT)
        act = x_glu * jax.nn.sigmoid(SWIGLU_ALPHA * x_glu) * (x_lin + 1.0)
        _store_token_tiles(y_ref, _dot(act.astype(BF16), wd_bf[...]) + bd_ref[0])

    @pl.when(i >= nu_ref[0])
    def _():
        y_ref[...] = jnp.zeros_like(y_ref)


def _experts(block_expert, n_used, xs_tiles, w_up, b_up, w_down, b_down):
    n_blocks = block_expert.shape[0]
    blk = (EXPERT_ROWS * TILE_ROWS, LANES)
    grid_spec = pltpu.PrefetchScalarGridSpec(
        num_scalar_prefetch=2,
        grid=(n_blocks,),
        in_specs=[
            pl.BlockSpec(blk, lambda i, be, nu: (jnp.minimum(i, nu[0] - 1), 0)),
            pl.BlockSpec((1, D_MODEL, 2 * D_FF), lambda i, be, nu: (be[i], 0, 0)),
            pl.BlockSpec((1, 1, 2 * D_FF), lambda i, be, nu: (be[i], 0, 0)),
            pl.BlockSpec((1, D_FF, D_MODEL), lambda i, be, nu: (be[i], 0, 0)),
            pl.BlockSpec((1, 1, D_MODEL), lambda i, be, nu: (be[i], 0, 0)),
        ],
        out_specs=pl.BlockSpec(blk, lambda i, be, nu: (i, 0)),
        scratch_shapes=[
            pltpu.VMEM((D_MODEL, 2 * D_FF), BF16),
            pltpu.VMEM((D_FF, D_MODEL), BF16),
        ],
    )
    return pl.pallas_call(
        _expert_kernel,
        grid_spec=grid_spec,
        out_shape=jax.ShapeDtypeStruct((n_blocks * blk[0], LANES), F32),
        compiler_params=pltpu.CompilerParams(
            dimension_semantics=("arbitrary",), vmem_limit_bytes=VMEM_LIMIT),
        name="experts",
    )(block_expert, n_used, xs_tiles, w_up, b_up[:, None, :], w_down, b_down[:, None, :])


def _combine_kernel(dst_cur_ref, dst_nxt_ref, y_hbm, h1_ref, rec_ref, g_ref, o_ref, gbuf, sem):
    i = pl.program_id(0)
    last = pl.num_programs(0) - 1
    tm = h1_ref.shape[0]
    rows = TOP_K * tm
    slot = i % 2

    def gather_group(idx_ref, s, row0):
        srcs = [idx_ref[row0 + r] for r in range(_ISSUE_UNROLL)]
        for r in range(_ISSUE_UNROLL):
            dst = gbuf.at[s, pl.ds((row0 + r) * TILE_ROWS, TILE_ROWS), :]
            pltpu.make_async_copy(_token_tile(y_hbm, srcs[r]), dst, sem.at[s]).start(priority=r % 2)

    @pl.when(i == 0)
    def _():
        def body(o, carry):
            gather_group(dst_cur_ref, 0, pl.multiple_of(o * _ISSUE_UNROLL, _ISSUE_UNROLL))
            return carry
        lax.fori_loop(0, rows // _ISSUE_UNROLL, body, 0)

    for g in range(rows // _ISSUE_UNROLL):
        gather_group(dst_nxt_ref, 1 - slot, g * _ISSUE_UNROLL)
    _tiles_wait_copy(y_hbm, gbuf.at[slot], rows, sem.at[slot]).wait()
    acc = h1_ref[...]
    for k in range(TOP_K):
        gate = rec_ref[:, _REC_GATE + k:_REC_GATE + k + 1]
        acc = acc + gate * _load_token_tiles(gbuf.at[slot], k * tm, tm)
    ms = jnp.mean(acc * acc, axis=-1, keepdims=True)
    o_ref[...] = acc * lax.rsqrt(ms + RMS_EPS) * g_ref[...]

    @pl.when(i == last)
    def _():
        _tiles_wait_copy(y_hbm, gbuf.at[1 - slot], rows, sem.at[1 - slot]).wait()


def _combine(dest_kmajor, y, h1, rec, g, tm):
    n = h1.shape[0]
    n_tiles = n // tm
    return pl.pallas_call(
        _combine_kernel,
        grid=(n_tiles,),
        in_specs=[
            pl.BlockSpec((TOP_K * tm,), lambda i: (i,), memory_space=pltpu.SMEM),
            pl.BlockSpec((TOP_K * tm,), lambda i: (jnp.minimum(i + 1, n_tiles - 1),), memory_space=pltpu.SMEM),
            pl.BlockSpec(memory_space=pl.ANY),
            pl.BlockSpec((tm, D_MODEL), lambda i: (i, 0)),
            pl.BlockSpec((tm, _REC_W), lambda i: (i, 0)),
            pl.BlockSpec((1, D_MODEL), lambda i: (0, 0)),
        ],
        out_specs=pl.BlockSpec((tm, D_MODEL), lambda i: (i, 0)),
        out_shape=jax.ShapeDtypeStruct((n, D_MODEL), F32),
        scratch_shapes=[pltpu.VMEM((2, TOP_K * tm * TILE_ROWS, LANES), F32),
                        pltpu.SemaphoreType.DMA((2,))],
        compiler_params=pltpu.CompilerParams(
            dimension_semantics=("arbitrary",), vmem_limit_bytes=VMEM_LIMIT),
        name="combine",
    )(dest_kmajor, dest_kmajor, y, h1, rec, g)


def _routing_tables(rec, counts, n, tm_combine):
    tm = EXPERT_ROWS
    n_blocks = (n * TOP_K + N_EXPERTS * (tm - 1)) // tm
    idx = rec[:, _REC_IDX:_REC_IDX + TOP_K].astype(jnp.int32)
    rank = rec[:, _REC_RANK:_REC_RANK + TOP_K].astype(jnp.int32)
    cnt = counts[0].astype(jnp.int32)
    blocks_e = (cnt + tm - 1) // tm
    blocks_end = jnp.cumsum(blocks_e)
    row_start = (blocks_end - blocks_e) * tm
    n_used = blocks_end[-1]
    dest = row_start[idx] + rank
    blk = jnp.arange(n_blocks, dtype=jnp.int32)
    be = jnp.minimum(jnp.sum(blocks_end[None, :] <= blk[:, None], axis=1), N_EXPERTS - 1).astype(jnp.int32)
    be = jnp.where(blk < n_used, be, be[jnp.maximum(n_used - 1, 0)])
    dest_kmajor = dest.reshape(n // tm_combine, tm_combine, TOP_K).transpose(0, 2, 1).reshape(-1)
    pad_start = row_start + cnt
    pad_len = blocks_e * tm - cnt
    return (be, n_used.reshape(1).astype(jnp.int32), dest.reshape(-1), dest_kmajor,
            pad_start.astype(jnp.int32), pad_len.astype(jnp.int32), n_blocks * tm)


def kernel(x, meta_tokens, mix_norm_g, w_in, ssm_a_re, ssm_a_im, ssm_log_dt, ssm_b_re, ssm_b_im,
           ssm_c_re, ssm_c_im, ssm_d, w_ssm_glu, attn_sinks, w_attn_o, w_out, ffn_norm_g,
           router_w, router_b, w_up, b_up, w_down, b_down, final_norm_g):
    bsz, seq, d = x.shape
    assert d == D_MODEL and seq % max(WINDOW, SSM_CHUNK) == 0
    assert mix_norm_g.shape[0] == 1, "single-layer trunk"
    n = bsz * seq
    tm_proj = min(512, n)
    tm_mix = min(256, n)
    tm_comb = min(128, n)
    x2 = x.reshape(n, D_MODEL)

    w_in_bf = w_in[0].astype(BF16)
    g_mix = mix_norm_g[0][None, :]
    u, q, kv, gs, ga = _in_proj(x2, g_mix, w_in_bf, tm_proj)
    u_m, _, kv_m, _, _ = _in_proj(meta_tokens, g_mix, w_in_bf, N_META)

    ssm_par = _ssm_params(ssm_a_re[0], ssm_a_im[0], ssm_log_dt[0], ssm_b_re[0], ssm_b_im[0],
                          ssm_c_re[0], ssm_c_im[0], ssm_d[0], SSM_CHUNK)
    y_ssm = _ssm(u, u_m, *ssm_par, batch=bsz, chunk=SSM_CHUNK)

    attn = _attention(attn_sinks[0], q, kv, kv_m, seq // WINDOW)

    h1, hf, rec, counts = _mix(
        x2, y_ssm, attn, gs, ga, w_ssm_glu[0].astype(BF16), w_attn_o[0].astype(BF16),
        w_out[0].astype(BF16), ffn_norm_g[0][None, :], router_w[0], router_b[0][None, :], tm_mix)

    be, n_used, dest, dest_kmajor, pad_start, pad_len, n_rows = _routing_tables(rec, counts, n, tm_comb)
    xs = _dispatch(dest, pad_start, pad_len, n_used, hf, n_rows, min(512, n // TOP_K))
    y = _experts(be, n_used, xs, w_up[0], b_up[0], w_down[0], b_down[0])
    out = _combine(dest_kmajor, y, h1, rec, final_norm_g[None, :], tm_comb)
    return out.reshape(bsz, seq, D_MODEL)
```

```python
import functools
import math

import jax
import jax.numpy as jnp
from jax import lax
from jax.experimental import pallas as pl
from jax.experimental.pallas import tpu as pltpu

F32 = jnp.float32
BF16 = jnp.bfloat16

D_MODEL = 1024
N_META = 16
SSM_WIDTH = 512
SSM_GROUP = 16
SSM_GROUPS = 32
SSM_STATE = 64
HEAD_DIM = 64
N_HEADS = 16
N_KV_HEADS = 2
KV_REP = N_HEADS // N_KV_HEADS
WINDOW = 128
Q_WIDTH = N_HEADS * HEAD_DIM
KV_WIDTH = N_KV_HEADS * HEAD_DIM
N_EXPERTS = 32
TOP_K = 4
D_FF = 1024
SWIGLU_ALPHA = 1.702
SWIGLU_LIMIT = 7.0
RMS_EPS = 1e-5
NEG_INF = -1e30

_U0, _Q0, _KV0, _GS0, _GA0, _IN_END = 0, 512, 1536, 1792, 2816, 3840

SSM_CH_BLOCK = 128
SSM_HALF = (SSM_CH_BLOCK // SSM_GROUP) * SSM_STATE
SSM_CHUNK = 32
EXPERT_ROWS = 256
VMEM_LIMIT = 56 * 1024 * 1024


def _dot(a, b):
    return jnp.dot(a, b, preferred_element_type=F32)


def _dot_nt(a, b):
    return lax.dot_general(a, b, (((1,), (1,)), ((), ())), preferred_element_type=F32)


LANES = 128
TILE_ROWS = D_MODEL // LANES


def _store_token_tiles(ref, x):
    rows = x.shape[0]
    for j in range(TILE_ROWS):
        ref[pl.ds(j, rows, stride=TILE_ROWS), :] = x[:, j * LANES:(j + 1) * LANES]


def _load_token_tiles(ref, start_row, rows):
    return jnp.concatenate(
        [ref[pl.ds(start_row * TILE_ROWS + j, rows, stride=TILE_ROWS), :] for j in range(TILE_ROWS)], axis=1)


def _token_tile(ref, row):
    return ref.at[pl.ds(pl.multiple_of(row * TILE_ROWS, TILE_ROWS), TILE_ROWS), :]


def _in_proj_kernel(x_ref, g_ref, w_ref, u_ref, q_ref, kv_ref, gs_ref, ga_ref):
    x = x_ref[...]
    ms = jnp.mean(x * x, axis=-1, keepdims=True)
    hn = (x * lax.rsqrt(ms + RMS_EPS) * g_ref[...]).astype(BF16)
    u_ref[...] = _dot(hn, w_ref[:, _U0:_Q0])
    q_ref[...] = (_dot(hn, w_ref[:, _Q0:_KV0]) * (HEAD_DIM ** -0.5)).astype(BF16)
    kv_ref[...] = _dot(hn, w_ref[:, _KV0:_GS0]).astype(BF16)
    gs_ref[...] = jax.nn.sigmoid(_dot(hn, w_ref[:, _GS0:_GA0])).astype(BF16)
    ga_ref[...] = jax.nn.sigmoid(_dot(hn, w_ref[:, _GA0:_IN_END])).astype(BF16)


def _in_proj(x2, g, w_bf, tm):
    n = x2.shape[0]
    row = lambda w: pl.BlockSpec((tm, w), lambda i: (i, 0))
    full = lambda a: pl.BlockSpec(a.shape, lambda i: (0,) * a.ndim)
    return pl.pallas_call(
        _in_proj_kernel,
        grid=(n // tm,),
        in_specs=[row(D_MODEL), full(g), full(w_bf)],
        out_specs=[row(SSM_WIDTH), row(Q_WIDTH), row(2 * KV_WIDTH), row(D_MODEL), row(D_MODEL)],
        out_shape=[
            jax.ShapeDtypeStruct((n, SSM_WIDTH), F32),
            jax.ShapeDtypeStruct((n, Q_WIDTH), BF16),
            jax.ShapeDtypeStruct((n, 2 * KV_WIDTH), BF16),
            jax.ShapeDtypeStruct((n, D_MODEL), BF16),
            jax.ShapeDtypeStruct((n, D_MODEL), BF16),
        ],
        compiler_params=pltpu.CompilerParams(
            dimension_semantics=("arbitrary",), vmem_limit_bytes=VMEM_LIMIT),
        name="in_proj",
    )(x2, g, w_bf)


def _ssm_kernel(u_ref, um_ref, bm_ref, cm_ref, ar_ref, ai_ref, atr_ref, ati_ref, d_ref,
                y_ref, sre, sim, *, chunk, rows, batch):
    h = SSM_HALF
    bm = bm_ref[0]
    cm = cm_ref[0]
    ar, ai = ar_ref[0], ai_ref[0]
    atr, ati = atr_ref[0], ati_ref[0]
    dsk = d_ref[0]
    n_chunks = rows // batch

    def advance(sr, si, bu):
        return ar * sr - ai * si + bu[:, :h], ar * si + ai * sr + bu[:, h:]

    bum = _dot(um_ref[...].astype(BF16), bm)
    mr = jnp.zeros((1, h), F32)
    mi = jnp.zeros((1, h), F32)
    for j in range(N_META):
        mr, mi = advance(mr, mi, bum[j:j + 1, :])

    def u_step(t):
        return u_ref[pl.ds(t, rows, stride=chunk), :]

    sre[...] = jnp.zeros_like(sre)
    sim[...] = jnp.zeros_like(sim)

    def pass_a(t, carry):
        bu = _dot(u_step(t).astype(BF16), bm)
        nr, ni = advance(sre[...], sim[...], bu)
        sre[...] = nr
        sim[...] = ni
        return carry

    lax.fori_loop(0, chunk, pass_a, 0)

    def over_chunks(c, carry):
        new = []
        for b in range(batch):
            cr, ci = carry[2 * b], carry[2 * b + 1]
            row = pl.ds(b * n_chunks + c, 1)
            er, ei = sre[row, :], sim[row, :]
            sre[row, :] = cr
            sim[row, :] = ci
            new += [atr * cr - ati * ci + er, atr * ci + ati * cr + ei]
        return tuple(new)

    lax.fori_loop(0, n_chunks, over_chunks, (mr, mi) * batch)

    def pass_b(t, carry):
        ut = u_step(t)
        bu = _dot(ut.astype(BF16), bm)
        nr, ni = advance(sre[...], sim[...], bu)
        sre[...] = nr
        sim[...] = ni
        y = _dot(nr.astype(BF16), cm[:h, :]) + _dot(ni.astype(BF16), cm[h:, :]) + dsk * ut
        y_ref[pl.ds(t, rows, stride=chunk), :] = y
        return carry

    lax.fori_loop(0, chunk, pass_b, 0)


def _ssm(u, u_meta, bmat, cmat, a_re, a_im, at_re, at_im, dskip, batch, chunk):
    n = u.shape[0]
    rows = n // chunk
    nblk = SSM_WIDTH // SSM_CH_BLOCK
    col = lambda r: pl.BlockSpec((r, SSM_CH_BLOCK), lambda j: (0, j))
    par = lambda a: pl.BlockSpec((1,) + a.shape[1:], lambda j: (j, 0, 0))
    return pl.pallas_call(
        functools.partial(_ssm_kernel, chunk=chunk, rows=rows, batch=batch),
        grid=(nblk,),
        in_specs=[col(n), col(N_META), par(bmat), par(cmat), par(a_re), par(a_im),
                  par(at_re), par(at_im), par(dskip)],
        out_specs=col(n),
        out_shape=jax.ShapeDtypeStruct((n, SSM_WIDTH), F32),
        scratch_shapes=[pltpu.VMEM((rows, SSM_HALF), F32), pltpu.VMEM((rows, SSM_HALF), F32)],
        compiler_params=pltpu.CompilerParams(
            dimension_semantics=("arbitrary",), vmem_limit_bytes=VMEM_LIMIT),
        name="ssm",
    )(u, u_meta, bmat, cmat, a_re, a_im, at_re, at_im, dskip)


def _ssm_params(a_re, a_im, log_dt, b_re, b_im, c_re, c_im, d_skip, chunk):
    dt = jnp.exp(log_dt)[:, None]
    mag = jnp.exp(a_re * dt)
    ang = a_im * dt
    abar_re, abar_im = mag * jnp.cos(ang), mag * jnp.sin(ang)
    den = a_re * a_re + a_im * a_im
    nr, ni = abar_re - 1.0, abar_im
    coef_re = ((nr * a_re + ni * a_im) / den)[..., None]
    coef_im = ((ni * a_re - nr * a_im) / den)[..., None]
    bbar_re = coef_re * b_re - coef_im * b_im
    bbar_im = coef_re * b_im + coef_im * b_re
    magt = jnp.exp(a_re * dt * chunk)
    at_re, at_im = magt * jnp.cos(ang * chunk), magt * jnp.sin(ang * chunk)

    nblk = SSM_WIDTH // SSM_CH_BLOCK
    gpb = SSM_GROUPS // nblk
    eye = jnp.eye(gpb, dtype=F32)

    def in_map(b):
        b = b.reshape(nblk, gpb, SSM_STATE, SSM_GROUP)
        return jnp.einsum('jgpc,gh->jgchp', b, eye).reshape(nblk, SSM_CH_BLOCK, gpb * SSM_STATE)

    def out_map(c):
        c = c.reshape(nblk, gpb, SSM_GROUP, SSM_STATE)
        return jnp.einsum('jgcp,gh->jgphc', c, eye).reshape(nblk, gpb * SSM_STATE, SSM_CH_BLOCK)

    bmat = jnp.concatenate([in_map(bbar_re), in_map(bbar_im)], axis=2).astype(BF16)
    cmat = jnp.concatenate([out_map(c_re), -out_map(c_im)], axis=1).astype(BF16)
    vec = lambda v: v.reshape(nblk, 1, SSM_HALF)
    return (bmat, cmat, vec(abar_re), vec(abar_im), vec(at_re), vec(at_im),
            d_skip.reshape(nblk, 1, SSM_CH_BLOCK))


def _attn_kernel(sink_ref, q_ref, kvc_ref, kvp_ref, kvm_ref, o_ref, *, blocks_per_seq):
    n = pl.program_id(0) % blocks_per_seq
    w = WINDOW
    hd = HEAD_DIM
    qi = lax.broadcasted_iota(jnp.int32, (w, w), 0)
    lane = lax.broadcasted_iota(jnp.int32, (w, w), 1)
    vis_prev = (lane > qi) & (n > 0)
    vis_cur = lane <= qi
    left = lane < hd
    meta_l = lane < N_META
    meta_r = (lane >= N_META) & (lane < 2 * N_META)

    def placed(x_bf):
        x = x_bf.astype(F32)
        xr = pltpu.roll(x, hd, 1)
        lm = lax.broadcasted_iota(jnp.int32, x.shape, 1) < hd
        z = jnp.zeros_like(x)
        return {(0, 0): jnp.where(lm, x, z), (0, 1): jnp.where(lm, z, xr),
                (1, 0): jnp.where(lm, xr, z), (1, 1): jnp.where(lm, z, x)}

    kp, kc, km = placed(kvp_ref[:, :KV_WIDTH]), placed(kvc_ref[:, :KV_WIDTH]), placed(kvm_ref[:, :KV_WIDTH])
    vp, vc, vm = placed(kvp_ref[:, KV_WIDTH:]), placed(kvc_ref[:, KV_WIDTH:]), placed(kvm_ref[:, KV_WIDTH:])
    pad_rows = w - 2 * N_META
    zpad = jnp.zeros((pad_rows, w), F32)
    krow = lax.broadcasted_iota(jnp.int32, (5 * w, w), 0)
    klane = lax.broadcasted_iota(jnp.int32, (5 * w, w), 1)
    row_l = (krow < 2 * w) | ((krow >= 4 * w) & (krow < 4 * w + N_META))
    row_r = ((krow >= 2 * w) & (krow < 4 * w)) | ((krow >= 4 * w + N_META) & (krow < 4 * w + 2 * N_META))
    den_cols = jnp.where((row_l & (klane < hd)) | (row_r & (klane >= hd)), 1.0, 0.0)

    for j in range(N_KV_HEADS):
        kcat = jnp.concatenate(
            [kp[j, 0], kc[j, 0], kp[j, 1], kc[j, 1], km[j, 0], km[j, 1], zpad], axis=0).astype(BF16)
        vcat = jnp.concatenate(
            [jnp.concatenate([vp[j, 0], vc[j, 0], vp[j, 1], vc[j, 1], vm[j, 0], vm[j, 1], zpad], axis=0),
             den_cols], axis=1).astype(BF16)
        for r in range(KV_REP // 2):
            pr = j * (KV_REP // 2) + r
            s = _dot_nt(q_ref[:, pr * w:(pr + 1) * w], kcat)
            s_lp = jnp.where(vis_prev, s[:, 0:w], NEG_INF)
            s_lc = jnp.where(vis_cur, s[:, w:2 * w], NEG_INF)
            s_rp = jnp.where(vis_prev, s[:, 2 * w:3 * w], NEG_INF)
            s_rc = jnp.where(vis_cur, s[:, 3 * w:4 * w], NEG_INF)
            s_m = s[:, 4 * w:]
            sink_l, sink_r = sink_ref[2 * pr], sink_ref[2 * pr + 1]
            m_l = jnp.maximum(jnp.max(jnp.maximum(jnp.maximum(s_lp, s_lc), jnp.where(meta_l, s_m, NEG_INF)),
                                      axis=1, keepdims=True), sink_l)
            m_r = jnp.maximum(jnp.max(jnp.maximum(jnp.maximum(s_rp, s_rc), jnp.where(meta_r, s_m, NEG_INF)),
                                      axis=1, keepdims=True), sink_r)
            s_m = jnp.where(meta_l, s_m - m_l, jnp.where(meta_r, s_m - m_r, NEG_INF))
            e = jnp.concatenate([jnp.exp(s_lp - m_l), jnp.exp(s_lc - m_l), jnp.exp(s_rp - m_r),
                                 jnp.exp(s_rc - m_r), jnp.exp(s_m)], axis=1).astype(BF16)
            acc = _dot(e, vcat)
            den = acc[:, w:] + jnp.where(left, jnp.exp(sink_l - m_l), jnp.exp(sink_r - m_r))
            o_ref[:, pr * w:(pr + 1) * w] = (acc[:, :w] / den).astype(BF16)


def _attention(sinks, q, kv, kv_meta, blocks_per_seq):
    n = q.shape[0]
    return pl.pallas_call(
        functools.partial(_attn_kernel, blocks_per_seq=blocks_per_seq),
        grid=(n // WINDOW,),
        in_specs=[
            pl.BlockSpec(memory_space=pltpu.SMEM),
            pl.BlockSpec((WINDOW, Q_WIDTH), lambda g: (g, 0)),
            pl.BlockSpec((WINDOW, 2 * KV_WIDTH), lambda g: (g, 0)),
            pl.BlockSpec((WINDOW, 2 * KV_WIDTH), lambda g: (jnp.maximum(g - 1, 0), 0)),
            pl.BlockSpec((N_META, 2 * KV_WIDTH), lambda g: (0, 0)),
        ],
        out_specs=pl.BlockSpec((WINDOW, Q_WIDTH), lambda g: (g, 0)),
        out_shape=jax.ShapeDtypeStruct((n, Q_WIDTH), BF16),
        compiler_params=pltpu.CompilerParams(dimension_semantics=("arbitrary",)),
        name="attn",
    )(sinks, q, kv, kv, kv_meta)


_REC_IDX, _REC_RANK, _REC_GATE, _REC_W = 0, TOP_K, 2 * TOP_K, 128


def _mix_kernel(x_ref, y_ref, at_ref, gs_ref, ga_ref, wglu_ref, wo_ref, wout_ref, fg_ref, rw_ref, rb_ref,
                h1_ref, hf_ref, rec_ref, cnt_ref, cnt_scr):
    tm = x_ref.shape[0]

    @pl.when(pl.program_id(0) == 0)
    def _():
        cnt_scr[...] = jnp.zeros_like(cnt_scr)

    glu = _dot(jax.nn.gelu(y_ref[...]).astype(BF16), wglu_ref[...])
    branch_ssm = glu[:, :D_MODEL] * jax.nn.sigmoid(glu[:, D_MODEL:])
    branch_attn = _dot(at_ref[...], wo_ref[...])
    merged = gs_ref[...].astype(F32) * branch_ssm + ga_ref[...].astype(F32) * branch_attn
    h1 = x_ref[...] + _dot(merged.astype(BF16), wout_ref[...])
    h1_ref[...] = h1
    ms = jnp.mean(h1 * h1, axis=-1, keepdims=True)
    hf = h1 * lax.rsqrt(ms + RMS_EPS) * fg_ref[...]
    _store_token_tiles(hf_ref, hf)

    logits = jnp.dot(hf, rw_ref[...], precision=lax.Precision.HIGHEST,
                     preferred_element_type=F32) + rb_ref[...]
    col = lax.broadcasted_iota(jnp.int32, (tm, N_EXPERTS), 1)
    vals, idxs, hots = [], [], []
    rest = logits
    for _ in range(TOP_K):
        m = jnp.max(rest, axis=1, keepdims=True)
        first = jnp.min(jnp.where(rest == m, col, N_EXPERTS), axis=1, keepdims=True)
        hot = col == first
        vals.append(m)
        idxs.append(first)
        hots.append(hot)
        rest = jnp.where(hot, -jnp.inf, rest)
    exps = [jnp.exp(v - vals[0]) for v in vals]
    tot = exps[0] + exps[1] + exps[2] + exps[3]

    sel = (hots[0] | hots[1] | hots[2] | hots[3]).astype(F32)
    ti = lax.broadcasted_iota(jnp.int32, (tm, tm), 0)
    tj = lax.broadcasted_iota(jnp.int32, (tm, tm), 1)
    lower = (tj < ti).astype(BF16)
    rank_e = _dot(lower, sel.astype(BF16)) + cnt_scr[...]
    cnt_scr[...] = cnt_scr[...] + jnp.sum(sel, axis=0, keepdims=True)
    cnt_ref[...] = cnt_scr[...]

    lane = lax.broadcasted_iota(jnp.int32, (tm, _REC_W), 1)
    rec = jnp.zeros((tm, _REC_W), F32)
    for k in range(TOP_K):
        rank_k = jnp.sum(jnp.where(hots[k], rank_e, 0.0), axis=1, keepdims=True)
        rec = jnp.where(lane == _REC_IDX + k, idxs[k].astype(F32), rec)
        rec = jnp.where(lane == _REC_RANK + k, rank_k, rec)
        rec = jnp.where(lane == _REC_GATE + k, exps[k] / tot, rec)
    rec_ref[...] = rec


def _mix(x2, y, attn, gs, ga, wglu, wo, wout, fg, rw, rb, tm):
    n = x2.shape[0]
    row = lambda w: pl.BlockSpec((tm, w), lambda i: (i, 0))
    full = lambda a: pl.BlockSpec(a.shape, lambda i: (0,) * a.ndim)
    return pl.pallas_call(
        _mix_kernel,
        grid=(n // tm,),
        in_specs=[row(D_MODEL), row(SSM_WIDTH), row(Q_WIDTH), row(D_MODEL), row(D_MODEL),
                  full(wglu), full(wo), full(wout), full(fg), full(rw), full(rb)],
        out_specs=[row(D_MODEL), pl.BlockSpec((tm * TILE_ROWS, LANES), lambda i: (i, 0)), row(_REC_W),
                   pl.BlockSpec((1, N_EXPERTS), lambda i: (0, 0))],
        out_shape=[
            jax.ShapeDtypeStruct((n, D_MODEL), F32),
            jax.ShapeDtypeStruct((n * TILE_ROWS, LANES), F32),
            jax.ShapeDtypeStruct((n, _REC_W), F32),
            jax.ShapeDtypeStruct((1, N_EXPERTS), F32),
        ],
        scratch_shapes=[pltpu.VMEM((1, N_EXPERTS), F32)],
        compiler_params=pltpu.CompilerParams(
            dimension_semantics=("arbitrary",), vmem_limit_bytes=VMEM_LIMIT),
        name="mix_router",
    )(x2, y, attn, gs, ga, wglu, wo, wout, fg, rw, rb)


def _tiles_wait_copy(src_hbm, dst, n_tiles, sem):
    rows = n_tiles * TILE_ROWS
    return pltpu.make_async_copy(src_hbm.at[pl.ds(0, rows), :], dst.at[pl.ds(0, rows), :], sem)


_ISSUE_UNROLL = 16


def _dispatch_kernel(dst_ref, pad_start_ref, pad_len_ref, nu_ref, hf_ref, xs_hbm, zero_blk, sem, pad_sem):
    i = pl.program_id(0)
    last = pl.num_programs(0) - 1
    pairs = dst_ref.shape[0]
    n_blocks = xs_hbm.shape[0] // (EXPERT_ROWS * TILE_ROWS)

    def zero_copy(row, rows):
        src = zero_blk.at[pl.ds(0, rows * TILE_ROWS), :]
        dst = xs_hbm.at[pl.ds(pl.multiple_of(row * TILE_ROWS, TILE_ROWS), rows * TILE_ROWS), :]
        return pltpu.make_async_copy(src, dst, pad_sem)

    def for_each_pad(fn):
        def per_expert(e, carry):
            row, left = pad_start_ref[e], pad_len_ref[e]
            size = EXPERT_ROWS // 2
            while size >= 1:
                take = left & size

                @pl.when(take != 0)
                def _(row=row, size=size):
                    fn(zero_copy(row, size))

                row = row + take
                size //= 2
            return carry
        lax.fori_loop(0, N_EXPERTS, per_expert, 0)

        def per_block(b, carry):
            fn(zero_copy(b * EXPERT_ROWS, EXPERT_ROWS))
            return carry
        lax.fori_loop(nu_ref[0], n_blocks, per_block, 0)

    @pl.when(i == 0)
    def _():
        zero_blk[...] = jnp.zeros_like(zero_blk)
        for_each_pad(lambda cp: cp.start())

    def issue(o, carry):
        dsts = [dst_ref[o * _ISSUE_UNROLL + r] for r in range(_ISSUE_UNROLL)]
        tok0 = o * (_ISSUE_UNROLL // TOP_K)
        for r in range(_ISSUE_UNROLL):
            pltpu.make_async_copy(_token_tile(hf_ref, tok0 + r // TOP_K), _token_tile(xs_hbm, dsts[r]),
                                  sem).start(priority=r % 2)
        return carry

    lax.fori_loop(0, pairs // _ISSUE_UNROLL, issue, 0)
    for _ in range(TOP_K):
        pltpu.make_async_copy(hf_ref, xs_hbm.at[pl.ds(0, hf_ref.shape[0]), :], sem).wait()

    @pl.when(i == last)
    def _():
        for_each_pad(lambda cp: cp.wait())


def _dispatch(dest, pad_start, pad_len, n_used, hf_tiles, n_rows, tokens_per_step):
    n = dest.shape[0] // TOP_K
    pairs = tokens_per_step * TOP_K
    assert n % tokens_per_step == 0 and _ISSUE_UNROLL % TOP_K == 0
    smem = lambda: pl.BlockSpec(memory_space=pltpu.SMEM)
    return pl.pallas_call(
        _dispatch_kernel,
        grid=(n // tokens_per_step,),
        in_specs=[pl.BlockSpec((pairs,), lambda i: (i,), memory_space=pltpu.SMEM), smem(), smem(), smem(),
                  pl.BlockSpec((tokens_per_step * TILE_ROWS, LANES), lambda i: (i, 0))],
        out_specs=pl.BlockSpec(memory_space=pl.ANY),
        out_shape=jax.ShapeDtypeStruct((n_rows * TILE_ROWS, LANES), F32),
        scratch_shapes=[pltpu.VMEM((EXPERT_ROWS * TILE_ROWS, LANES), F32), pltpu.SemaphoreType.DMA,
                        pltpu.SemaphoreType.DMA],
        compiler_params=pltpu.CompilerParams(
            dimension_semantics=("arbitrary",), vmem_limit_bytes=VMEM_LIMIT),
        name="dispatch",
    )(dest, pad_start, pad_len, n_used, hf_tiles)


def _expert_kernel(be_ref, nu_ref, xs_ref, wu_ref, bu_ref, wd_ref, bd_ref, y_ref, wu_bf, wd_bf):
    i = pl.program_id(0)

    @pl.when((i == 0) | (be_ref[i] != be_ref[jnp.maximum(i - 1, 0)]))
    def _():
        wu_bf[...] = wu_ref[0].astype(BF16)
        wd_bf[...] = wd_ref[0].astype(BF16)

    @pl.when(i < nu_ref[0])
    def _():
        xb = _load_token_tiles(xs_ref, 0, EXPERT_ROWS).astype(BF16)
        up = _dot(xb, wu_bf[...]) + bu_ref[0]
        x_glu = jnp.minimum(up[:, :D_FF], SWIGLU_LIMIT)
        x_lin = jnp.clip(up[:, D_FF:], -SWIGLU_LIMIT, SWIGLU_LIMIT)
        act = x_glu * jax.nn.sigmoid(SWIGLU_ALPHA * x_glu) * (x_lin + 1.0)
        _store_token_tiles(y_ref, _dot(act.astype(BF16), wd_bf[...]) + bd_ref[0])

    @pl.when(i >= nu_ref[0])
    def _():
        y_ref[...] = jnp.zeros_like(y_ref)


def _experts(block_expert, n_used, xs_tiles, w_up, b_up, w_down, b_down):
    n_blocks = block_expert.shape[0]
    blk = (EXPERT_ROWS * TILE_ROWS, LANES)
    grid_spec = pltpu.PrefetchScalarGridSpec(
        num_scalar_prefetch=2,
        grid=(n_blocks,),
        in_specs=[
            pl.BlockSpec(blk, lambda i, be, nu: (jnp.minimum(i, nu[0] - 1), 0)),
            pl.BlockSpec((1, D_MODEL, 2 * D_FF), lambda i, be, nu: (be[i], 0, 0)),
            pl.BlockSpec((1, 1, 2 * D_FF), lambda i, be, nu: (be[i], 0, 0)),
            pl.BlockSpec((1, D_FF, D_MODEL), lambda i, be, nu: (be[i], 0, 0)),
            pl.BlockSpec((1, 1, D_MODEL), lambda i, be, nu: (be[i], 0, 0)),
        ],
        out_specs=pl.BlockSpec(blk, lambda i, be, nu: (i, 0)),
        scratch_shapes=[
            pltpu.VMEM((D_MODEL, 2 * D_FF), BF16),
            pltpu.VMEM((D_FF, D_MODEL), BF16),
        ],
    )
    return pl.pallas_call(
        _expert_kernel,
        grid_spec=grid_spec,
        out_shape=jax.ShapeDtypeStruct((n_blocks * blk[0], LANES), F32),
        compiler_params=pltpu.CompilerParams(
            dimension_semantics=("arbitrary",), vmem_limit_bytes=VMEM_LIMIT),
        name="experts",
    )(block_expert, n_used, xs_tiles, w_up, b_up[:, None, :], w_down, b_down[:, None, :])


def _combine_kernel(dst_cur_ref, dst_nxt_ref, y_hbm, h1_ref, rec_ref, g_ref, o_ref, gbuf, sem):
    i = pl.program_id(0)
    last = pl.num_programs(0) - 1
    tm = h1_ref.shape[0]
    rows = TOP_K * tm
    slot = i % 2

    def gather_group(idx_ref, s, row0):
        srcs = [idx_ref[row0 + r] for r in range(_ISSUE_UNROLL)]
        for r in range(_ISSUE_UNROLL):
            dst = gbuf.at[s, pl.ds((row0 + r) * TILE_ROWS, TILE_ROWS), :]
            pltpu.make_async_copy(_token_tile(y_hbm, srcs[r]), dst, sem.at[s]).start(priority=r % 2)

    @pl.when(i == 0)
    def _():
        def body(o, carry):
            gather_group(dst_cur_ref, 0, pl.multiple_of(o * _ISSUE_UNROLL, _ISSUE_UNROLL))
            return carry
        lax.fori_loop(0, rows // _ISSUE_UNROLL, body, 0)

    for g in range(rows // _ISSUE_UNROLL):
        gather_group(dst_nxt_ref, 1 - slot, g * _ISSUE_UNROLL)
    _tiles_wait_copy(y_hbm, gbuf.at[slot], rows, sem.at[slot]).wait()
    acc = h1_ref[...]
    for k in range(TOP_K):
        gate = rec_ref[:, _REC_GATE + k:_REC_GATE + k + 1]
        acc = acc + gate * _load_token_tiles(gbuf.at[slot], k * tm, tm)
    ms = jnp.mean(acc * acc, axis=-1, keepdims=True)
    o_ref[...] = acc * lax.rsqrt(ms + RMS_EPS) * g_ref[...]

    @pl.when(i == last)
    def _():
        _tiles_wait_copy(y_hbm, gbuf.at[1 - slot], rows, sem.at[1 - slot]).wait()


def _combine(dest_kmajor, y, h1, rec, g, tm):
    n = h1.shape[0]
    n_tiles = n // tm
    return pl.pallas_call(
        _combine_kernel,
        grid=(n_tiles,),
        in_specs=[
            pl.BlockSpec((TOP_K * tm,), lambda i: (i,), memory_space=pltpu.SMEM),
            pl.BlockSpec((TOP_K * tm,), lambda i: (jnp.minimum(i + 1, n_tiles - 1),), memory_space=pltpu.SMEM),
            pl.BlockSpec(memory_space=pl.ANY),
            pl.BlockSpec((tm, D_MODEL), lambda i: (i, 0)),
            pl.BlockSpec((tm, _REC_W), lambda i: (i, 0)),
            pl.BlockSpec((1, D_MODEL), lambda i: (0, 0)),
        ],
        out_specs=pl.BlockSpec((tm, D_MODEL), lambda i: (i, 0)),
        out_shape=jax.ShapeDtypeStruct((n, D_MODEL), F32),
        scratch_shapes=[pltpu.VMEM((2, TOP_K * tm * TILE_ROWS, LANES), F32),
                        pltpu.SemaphoreType.DMA((2,))],
        compiler_params=pltpu.CompilerParams(
            dimension_semantics=("arbitrary",), vmem_limit_bytes=VMEM_LIMIT),
        name="combine",
    )(dest_kmajor, dest_kmajor, y, h1, rec, g)


def _routing_tables(rec, counts, n, tm_combine):
    tm = EXPERT_ROWS
    n_blocks = (n * TOP_K + N_EXPERTS * (tm - 1)) // tm
    idx = rec[:, _REC_IDX:_REC_IDX + TOP_K].astype(jnp.int32)
    rank = rec[:, _REC_RANK:_REC_RANK + TOP_K].astype(jnp.int32)
    cnt = counts[0].astype(jnp.int32)
    blocks_e = (cnt + tm - 1) // tm
    blocks_end = jnp.cumsum(blocks_e)
    row_start = (blocks_end - blocks_e) * tm
    n_used = blocks_end[-1]
    dest = row_start[idx] + rank
    blk = jnp.arange(n_blocks, dtype=jnp.int32)
    be = jnp.minimum(jnp.sum(blocks_end[None, :] <= blk[:, None], axis=1), N_EXPERTS - 1).astype(jnp.int32)
    be = jnp.where(blk < n_used, be, be[jnp.maximum(n_used - 1, 0)])
    dest_kmajor = dest.reshape(n // tm_combine, tm_combine, TOP_K).transpose(0, 2, 1).reshape(-1)
    pad_start = row_start + cnt
    pad_len = blocks_e * tm - cnt
    return (be, n_used.reshape(1).astype(jnp.int32), dest.reshape(-1), dest_kmajor,
            pad_start.astype(jnp.int32), pad_len.astype(jnp.int32), n_blocks * tm)


def kernel(x, meta_tokens, mix_norm_g, w_in, ssm_a_re, ssm_a_im, ssm_log_dt, ssm_b_re, ssm_b_im,
           ssm_c_re, ssm_c_im, ssm_d, w_ssm_glu, attn_sinks, w_attn_o, w_out, ffn_norm_g,
           router_w, router_b, w_up, b_up, w_down, b_down, final_norm_g):
    bsz, seq, d = x.shape
    assert d == D_MODEL and seq % max(WINDOW, SSM_CHUNK) == 0
    assert mix_norm_g.shape[0] == 1, "single-layer trunk"
    n = bsz * seq
    tm_proj = min(512, n)
    tm_mix = min(256, n)
    tm_comb = min(128, n)
    x2 = x.reshape(n, D_MODEL)

    w_in_bf = w_in[0].astype(BF16)
    g_mix = mix_norm_g[0][None, :]
    u, q, kv, gs, ga = _in_proj(x2, g_mix, w_in_bf, tm_proj)
    u_m, _, kv_m, _, _ = _in_proj(meta_tokens, g_mix, w_in_bf, N_META)

    ssm_par = _ssm_params(ssm_a_re[0], ssm_a_im[0], ssm_log_dt[0], ssm_b_re[0], ssm_b_im[0],
                          ssm_c_re[0], ssm_c_im[0], ssm_d[0], SSM_CHUNK)
    y_ssm = _ssm(u, u_m, *ssm_par, batch=bsz, chunk=SSM_CHUNK)

    attn = _attention(attn_sinks[0], q, kv, kv_m, seq // WINDOW)

    h1, hf, rec, counts = _mix(
        x2, y_ssm, attn, gs, ga, w_ssm_glu[0].astype(BF16), w_attn_o[0].astype(BF16),
        w_out[0].astype(BF16), ffn_norm_g[0][None, :], router_w[0], router_b[0][None, :], tm_mix)

    be, n_used, dest, dest_kmajor, pad_start, pad_len, n_rows = _routing_tables(rec, counts, n, tm_comb)
    xs = _dispatch(dest, pad_start, pad_len, n_used, hf, n_rows, min(1024, n))
    y = _experts(be, n_used, xs, w_up[0], b_up[0], w_down[0], b_down[0])
    out = _combine(dest_kmajor, y, h1, rec, final_norm_g[None, :], tm_comb)
    return out.reshape(bsz, seq, D_MODEL)
```

```python
import functools
import math

import jax
import jax.numpy as jnp
from jax import lax
from jax.experimental import pallas as pl
from jax.experimental.pallas import tpu as pltpu

F32 = jnp.float32
BF16 = jnp.bfloat16

D_MODEL = 1024
N_META = 16
SSM_WIDTH = 512
SSM_GROUP = 16
SSM_GROUPS = 32
SSM_STATE = 64
HEAD_DIM = 64
N_HEADS = 16
N_KV_HEADS = 2
KV_REP = N_HEADS // N_KV_HEADS
WINDOW = 128
Q_WIDTH = N_HEADS * HEAD_DIM
KV_WIDTH = N_KV_HEADS * HEAD_DIM
N_EXPERTS = 32
TOP_K = 4
D_FF = 1024
SWIGLU_ALPHA = 1.702
SWIGLU_LIMIT = 7.0
RMS_EPS = 1e-5
NEG_INF = -1e30

_U0, _Q0, _KV0, _GS0, _GA0, _IN_END = 0, 512, 1536, 1792, 2816, 3840

SSM_CH_BLOCK = 128
SSM_HALF = (SSM_CH_BLOCK // SSM_GROUP) * SSM_STATE
SSM_CHUNK = 32
EXPERT_ROWS = 256
VMEM_LIMIT = 56 * 1024 * 1024


def _dot(a, b):
    return jnp.dot(a, b, preferred_element_type=F32)


def _dot_nt(a, b):
    return lax.dot_general(a, b, (((1,), (1,)), ((), ())), preferred_element_type=F32)


LANES = 128
TILE_ROWS = D_MODEL // LANES


def _store_token_tiles(ref, x):
    rows = x.shape[0]
    for j in range(TILE_ROWS):
        ref[pl.ds(j, rows, stride=TILE_ROWS), :] = x[:, j * LANES:(j + 1) * LANES]


def _load_token_tiles(ref, start_row, rows):
    return jnp.concatenate(
        [ref[pl.ds(start_row * TILE_ROWS + j, rows, stride=TILE_ROWS), :] for j in range(TILE_ROWS)], axis=1)


def _token_tile(ref, row):
    return ref.at[pl.ds(pl.multiple_of(row * TILE_ROWS, TILE_ROWS), TILE_ROWS), :]


def _in_proj_kernel(x_ref, g_ref, w_ref, u_ref, q_ref, kv_ref, gs_ref, ga_ref):
    x = x_ref[...]
    ms = jnp.mean(x * x, axis=-1, keepdims=True)
    hn = (x * lax.rsqrt(ms + RMS_EPS) * g_ref[...]).astype(BF16)
    u_ref[...] = _dot(hn, w_ref[:, _U0:_Q0])
    q_ref[...] = (_dot(hn, w_ref[:, _Q0:_KV0]) * (HEAD_DIM ** -0.5)).astype(BF16)
    kv_ref[...] = _dot(hn, w_ref[:, _KV0:_GS0]).astype(BF16)
    gs_ref[...] = jax.nn.sigmoid(_dot(hn, w_ref[:, _GS0:_GA0])).astype(BF16)
    ga_ref[...] = jax.nn.sigmoid(_dot(hn, w_ref[:, _GA0:_IN_END])).astype(BF16)


def _in_proj(x2, g, w_bf, tm):
    n = x2.shape[0]
    row = lambda w: pl.BlockSpec((tm, w), lambda i: (i, 0))
    full = lambda a: pl.BlockSpec(a.shape, lambda i: (0,) * a.ndim)
    return pl.pallas_call(
        _in_proj_kernel,
        grid=(n // tm,),
        in_specs=[row(D_MODEL), full(g), full(w_bf)],
        out_specs=[row(SSM_WIDTH), row(Q_WIDTH), row(2 * KV_WIDTH), row(D_MODEL), row(D_MODEL)],
        out_shape=[
            jax.ShapeDtypeStruct((n, SSM_WIDTH), F32),
            jax.ShapeDtypeStruct((n, Q_WIDTH), BF16),
            jax.ShapeDtypeStruct((n, 2 * KV_WIDTH), BF16),
            jax.ShapeDtypeStruct((n, D_MODEL), BF16),
            jax.ShapeDtypeStruct((n, D_MODEL), BF16),
        ],
        compiler_params=pltpu.CompilerParams(
            dimension_semantics=("arbitrary",), vmem_limit_bytes=VMEM_LIMIT),
        name="in_proj",
    )(x2, g, w_bf)


def _ssm_kernel(u_ref, um_ref, bm_ref, cm_ref, ar_ref, ai_ref, atr_ref, ati_ref, d_ref,
                y_ref, sre, sim, *, chunk, rows, batch):
    h = SSM_HALF
    bm = bm_ref[0]
    cm = cm_ref[0]
    ar, ai = ar_ref[0], ai_ref[0]
    atr, ati = atr_ref[0], ati_ref[0]
    dsk = d_ref[0]
    n_chunks = rows // batch

    def advance(sr, si, bu):
        return ar * sr - ai * si + bu[:, :h], ar * si + ai * sr + bu[:, h:]

    bum = _dot(um_ref[...].astype(BF16), bm)
    mr = jnp.zeros((1, h), F32)
    mi = jnp.zeros((1, h), F32)
    for j in range(N_META):
        mr, mi = advance(mr, mi, bum[j:j + 1, :])

    def u_step(t):
        return u_ref[pl.ds(t, rows, stride=chunk), :]

    sre[...] = jnp.zeros_like(sre)
    sim[...] = jnp.zeros_like(sim)

    def pass_a(t, carry):
        bu = _dot(u_step(t).astype(BF16), bm)
        nr, ni = advance(sre[...], sim[...], bu)
        sre[...] = nr
        sim[...] = ni
        return carry

    lax.fori_loop(0, chunk, pass_a, 0)

    def over_chunks(c, carry):
        new = []
        for b in range(batch):
            cr, ci = carry[2 * b], carry[2 * b + 1]
            row = pl.ds(b * n_chunks + c, 1)
            er, ei = sre[row, :], sim[row, :]
            sre[row, :] = cr
            sim[row, :] = ci
            new += [atr * cr - ati * ci + er, atr * ci + ati * cr + ei]
        return tuple(new)

    lax.fori_loop(0, n_chunks, over_chunks, (mr, mi) * batch)

    def pass_b(t, carry):
        ut = u_step(t)
        bu = _dot(ut.astype(BF16), bm)
        nr, ni = advance(sre[...], sim[...], bu)
        sre[...] = nr
        sim[...] = ni
        y = _dot(nr.astype(BF16), cm[:h, :]) + _dot(ni.astype(BF16), cm[h:, :]) + dsk * ut
        y_ref[pl.ds(t, rows, stride=chunk), :] = y
        return carry

    lax.fori_loop(0, chunk, pass_b, 0)


def _ssm(u, u_meta, bmat, cmat, a_re, a_im, at_re, at_im, dskip, batch, chunk):
    n = u.shape[0]
    rows = n // chunk
    nblk = SSM_WIDTH // SSM_CH_BLOCK
    col = lambda r: pl.BlockSpec((r, SSM_CH_BLOCK), lambda j: (0, j))
    par = lambda a: pl.BlockSpec((1,) + a.shape[1:], lambda j: (j, 0, 0))
    return pl.pallas_call(
        functools.partial(_ssm_kernel, chunk=chunk, rows=rows, batch=batch),
        grid=(nblk,),
        in_specs=[col(n), col(N_META), par(bmat), par(cmat), par(a_re), par(a_im),
                  par(at_re), par(at_im), par(dskip)],
        out_specs=col(n),
        out_shape=jax.ShapeDtypeStruct((n, SSM_WIDTH), F32),
        scratch_shapes=[pltpu.VMEM((rows, SSM_HALF), F32), pltpu.VMEM((rows, SSM_HALF), F32)],
        compiler_params=pltpu.CompilerParams(
            dimension_semantics=("arbitrary",), vmem_limit_bytes=VMEM_LIMIT),
        name="ssm",
    )(u, u_meta, bmat, cmat, a_re, a_im, at_re, at_im, dskip)


def _ssm_params(a_re, a_im, log_dt, b_re, b_im, c_re, c_im, d_skip, chunk):
    dt = jnp.exp(log_dt)[:, None]
    mag = jnp.exp(a_re * dt)
    ang = a_im * dt
    abar_re, abar_im = mag * jnp.cos(ang), mag * jnp.sin(ang)
    den = a_re * a_re + a_im * a_im
    nr, ni = abar_re - 1.0, abar_im
    coef_re = ((nr * a_re + ni * a_im) / den)[..., None]
    coef_im = ((ni * a_re - nr * a_im) / den)[..., None]
    bbar_re = coef_re * b_re - coef_im * b_im
    bbar_im = coef_re * b_im + coef_im * b_re
    magt = jnp.exp(a_re * dt * chunk)
    at_re, at_im = magt * jnp.cos(ang * chunk), magt * jnp.sin(ang * chunk)

    nblk = SSM_WIDTH // SSM_CH_BLOCK
    gpb = SSM_GROUPS // nblk
    eye = jnp.eye(gpb, dtype=F32)

    def in_map(b):
        b = b.reshape(nblk, gpb, SSM_STATE, SSM_GROUP)
        return jnp.einsum('jgpc,gh->jgchp', b, eye).reshape(nblk, SSM_CH_BLOCK, gpb * SSM_STATE)

    def out_map(c):
        c = c.reshape(nblk, gpb, SSM_GROUP, SSM_STATE)
        return jnp.einsum('jgcp,gh->jgphc', c, eye).reshape(nblk, gpb * SSM_STATE, SSM_CH_BLOCK)

    bmat = jnp.concatenate([in_map(bbar_re), in_map(bbar_im)], axis=2).astype(BF16)
    cmat = jnp.concatenate([out_map(c_re), -out_map(c_im)], axis=1).astype(BF16)
    vec = lambda v: v.reshape(nblk, 1, SSM_HALF)
    return (bmat, cmat, vec(abar_re), vec(abar_im), vec(at_re), vec(at_im),
            d_skip.reshape(nblk, 1, SSM_CH_BLOCK))


def _attn_kernel(sink_ref, q_ref, kvc_ref, kvp_ref, kvm_ref, o_ref, *, blocks_per_seq):
    n = pl.program_id(0) % blocks_per_seq
    w = WINDOW
    hd = HEAD_DIM
    qi = lax.broadcasted_iota(jnp.int32, (w, w), 0)
    lane = lax.broadcasted_iota(jnp.int32, (w, w), 1)
    vis_prev = (lane > qi) & (n > 0)
    vis_cur = lane <= qi
    left = lane < hd
    meta_l = lane < N_META
    meta_r = (lane >= N_META) & (lane < 2 * N_META)

    def placed(x_bf):
        x = x_bf.astype(F32)
        xr = pltpu.roll(x, hd, 1)
        lm = lax.broadcasted_iota(jnp.int32, x.shape, 1) < hd
        z = jnp.zeros_like(x)
        return {(0, 0): jnp.where(lm, x, z), (0, 1): jnp.where(lm, z, xr),
                (1, 0): jnp.where(lm, xr, z), (1, 1): jnp.where(lm, z, x)}

    kp, kc, km = placed(kvp_ref[:, :KV_WIDTH]), placed(kvc_ref[:, :KV_WIDTH]), placed(kvm_ref[:, :KV_WIDTH])
    vp, vc, vm = placed(kvp_ref[:, KV_WIDTH:]), placed(kvc_ref[:, KV_WIDTH:]), placed(kvm_ref[:, KV_WIDTH:])
    pad_rows = w - 2 * N_META
    zpad = jnp.zeros((pad_rows, w), F32)
    krow = lax.broadcasted_iota(jnp.int32, (5 * w, w), 0)
    klane = lax.broadcasted_iota(jnp.int32, (5 * w, w), 1)
    row_l = (krow < 2 * w) | ((krow >= 4 * w) & (krow < 4 * w + N_META))
    row_r = ((krow >= 2 * w) & (krow < 4 * w)) | ((krow >= 4 * w + N_META) & (krow < 4 * w + 2 * N_META))
    den_cols = jnp.where((row_l & (klane < hd)) | (row_r & (klane >= hd)), 1.0, 0.0)

    for j in range(N_KV_HEADS):
        kcat = jnp.concatenate(
            [kp[j, 0], kc[j, 0], kp[j, 1], kc[j, 1], km[j, 0], km[j, 1], zpad], axis=0).astype(BF16)
        vcat = jnp.concatenate(
            [jnp.concatenate([vp[j, 0], vc[j, 0], vp[j, 1], vc[j, 1], vm[j, 0], vm[j, 1], zpad], axis=0),
             den_cols], axis=1).astype(BF16)
        for r in range(KV_REP // 2):
            pr = j * (KV_REP // 2) + r
            s = _dot_nt(q_ref[:, pr * w:(pr + 1) * w], kcat)
            s_lp = jnp.where(vis_prev, s[:, 0:w], NEG_INF)
            s_lc = jnp.where(vis_cur, s[:, w:2 * w], NEG_INF)
            s_rp = jnp.where(vis_prev, s[:, 2 * w:3 * w], NEG_INF)
            s_rc = jnp.where(vis_cur, s[:, 3 * w:4 * w], NEG_INF)
            s_m = s[:, 4 * w:]
            sink_l, sink_r = sink_ref[2 * pr], sink_ref[2 * pr + 1]
            m_l = jnp.maximum(jnp.max(jnp.maximum(jnp.maximum(s_lp, s_lc), jnp.where(meta_l, s_m, NEG_INF)),
                                      axis=1, keepdims=True), sink_l)
            m_r = jnp.maximum(jnp.max(jnp.maximum(jnp.maximum(s_rp, s_rc), jnp.where(meta_r, s_m, NEG_INF)),
                                      axis=1, keepdims=True), sink_r)
            s_m = jnp.where(meta_l, s_m - m_l, jnp.where(meta_r, s_m - m_r, NEG_INF))
            e = jnp.concatenate([jnp.exp(s_lp - m_l), jnp.exp(s_lc - m_l), jnp.exp(s_rp - m_r),
                                 jnp.exp(s_rc - m_r), jnp.exp(s_m)], axis=1).astype(BF16)
            acc = _dot(e, vcat)
            den = acc[:, w:] + jnp.where(left, jnp.exp(sink_l - m_l), jnp.exp(sink_r - m_r))
            o_ref[:, pr * w:(pr + 1) * w] = (acc[:, :w] / den).astype(BF16)


def _attention(sinks, q, kv, kv_meta, blocks_per_seq):
    n = q.shape[0]
    return pl.pallas_call(
        functools.partial(_attn_kernel, blocks_per_seq=blocks_per_seq),
        grid=(n // WINDOW,),
        in_specs=[
            pl.BlockSpec(memory_space=pltpu.SMEM),
            pl.BlockSpec((WINDOW, Q_WIDTH), lambda g: (g, 0)),
            pl.BlockSpec((WINDOW, 2 * KV_WIDTH), lambda g: (g, 0)),
            pl.BlockSpec((WINDOW, 2 * KV_WIDTH), lambda g: (jnp.maximum(g - 1, 0), 0)),
            pl.BlockSpec((N_META, 2 * KV_WIDTH), lambda g: (0, 0)),
        ],
        out_specs=pl.BlockSpec((WINDOW, Q_WIDTH), lambda g: (g, 0)),
        out_shape=jax.ShapeDtypeStruct((n, Q_WIDTH), BF16),
        compiler_params=pltpu.CompilerParams(dimension_semantics=("arbitrary",)),
        name="attn",
    )(sinks, q, kv, kv, kv_meta)


_REC_IDX, _REC_RANK, _REC_GATE, _REC_W = 0, TOP_K, 2 * TOP_K, 128


def _mix_kernel(x_ref, y_ref, at_ref, gs_ref, ga_ref, wglu_ref, wo_ref, wout_ref, fg_ref, rwh_ref, rwl_ref, rb_ref,
                h1_ref, hf_ref, rec_ref, cnt_ref, cnt_scr):
    tm = x_ref.shape[0]

    @pl.when(pl.program_id(0) == 0)
    def _():
        cnt_scr[...] = jnp.zeros_like(cnt_scr)

    glu = _dot(jax.nn.gelu(y_ref[...]).astype(BF16), wglu_ref[...])
    branch_ssm = glu[:, :D_MODEL] * jax.nn.sigmoid(glu[:, D_MODEL:])
    branch_attn = _dot(at_ref[...], wo_ref[...])
    merged = gs_ref[...].astype(F32) * branch_ssm + ga_ref[...].astype(F32) * branch_attn
    h1 = x_ref[...] + _dot(merged.astype(BF16), wout_ref[...])
    h1_ref[...] = h1
    ms = jnp.mean(h1 * h1, axis=-1, keepdims=True)
    hf = h1 * lax.rsqrt(ms + RMS_EPS) * fg_ref[...]
    _store_token_tiles(hf_ref, hf)

    hf_hi = hf.astype(BF16)
    hf_lo = (hf - hf_hi.astype(F32)).astype(BF16)
    logits = (_dot(hf_hi, rwh_ref[...]) + (_dot(hf_hi, rwl_ref[...]) + _dot(hf_lo, rwh_ref[...]))
              + rb_ref[...])
    col = lax.broadcasted_iota(jnp.int32, (tm, N_EXPERTS), 1)
    vals, idxs, hots = [], [], []
    rest = logits
    for _ in range(TOP_K):
        m = jnp.max(rest, axis=1, keepdims=True)
        first = jnp.min(jnp.where(rest == m, col, N_EXPERTS), axis=1, keepdims=True)
        hot = col == first
        vals.append(m)
        idxs.append(first)
        hots.append(hot)
        rest = jnp.where(hot, -jnp.inf, rest)
    exps = [jnp.exp(v - vals[0]) for v in vals]
    tot = exps[0] + exps[1] + exps[2] + exps[3]

    sel = (hots[0] | hots[1] | hots[2] | hots[3]).astype(F32)
    ti = lax.broadcasted_iota(jnp.int32, (tm, tm), 0)
    tj = lax.broadcasted_iota(jnp.int32, (tm, tm), 1)
    lower = (tj < ti).astype(BF16)
    rank_e = _dot(lower, sel.astype(BF16)) + cnt_scr[...]
    cnt_scr[...] = cnt_scr[...] + jnp.sum(sel, axis=0, keepdims=True)
    cnt_ref[...] = cnt_scr[...]

    lane = lax.broadcasted_iota(jnp.int32, (tm, _REC_W), 1)
    rec = jnp.zeros((tm, _REC_W), F32)
    for k in range(TOP_K):
        rank_k = jnp.sum(jnp.where(hots[k], rank_e, 0.0), axis=1, keepdims=True)
        rec = jnp.where(lane == _REC_IDX + k, idxs[k].astype(F32), rec)
        rec = jnp.where(lane == _REC_RANK + k, rank_k, rec)
        rec = jnp.where(lane == _REC_GATE + k, exps[k] / tot, rec)
    rec_ref[...] = rec


def _mix(x2, y, attn, gs, ga, wglu, wo, wout, fg, rw, rb, tm):
    n = x2.shape[0]
    rw_hi = rw.astype(BF16)
    rw_lo = (rw - rw_hi.astype(F32)).astype(BF16)
    row = lambda w: pl.BlockSpec((tm, w), lambda i: (i, 0))
    full = lambda a: pl.BlockSpec(a.shape, lambda i: (0,) * a.ndim)
    return pl.pallas_call(
        _mix_kernel,
        grid=(n // tm,),
        in_specs=[row(D_MODEL), row(SSM_WIDTH), row(Q_WIDTH), row(D_MODEL), row(D_MODEL),
                  full(wglu), full(wo), full(wout), full(fg), full(rw_hi), full(rw_lo), full(rb)],
        out_specs=[row(D_MODEL), pl.BlockSpec((tm * TILE_ROWS, LANES), lambda i: (i, 0)), row(_REC_W),
                   pl.BlockSpec((1, N_EXPERTS), lambda i: (0, 0))],
        out_shape=[
            jax.ShapeDtypeStruct((n, D_MODEL), F32),
            jax.ShapeDtypeStruct((n * TILE_ROWS, LANES), F32),
            jax.ShapeDtypeStruct((n, _REC_W), F32),
            jax.ShapeDtypeStruct((1, N_EXPERTS), F32),
        ],
        scratch_shapes=[pltpu.VMEM((1, N_EXPERTS), F32)],
        compiler_params=pltpu.CompilerParams(
            dimension_semantics=("arbitrary",), vmem_limit_bytes=VMEM_LIMIT),
        name="mix_router",
    )(x2, y, attn, gs, ga, wglu, wo, wout, fg, rw_hi, rw_lo, rb)


def _tiles_wait_copy(src_hbm, dst, n_tiles, sem):
    rows = n_tiles * TILE_ROWS
    return pltpu.make_async_copy(src_hbm.at[pl.ds(0, rows), :], dst.at[pl.ds(0, rows), :], sem)


_ISSUE_UNROLL = 16


def _dispatch_kernel(dst_ref, pad_start_ref, pad_len_ref, nu_ref, hf_ref, xs_hbm, zero_blk, sem, pad_sem):
    i = pl.program_id(0)
    last = pl.num_programs(0) - 1
    pairs = dst_ref.shape[0]
    n_blocks = xs_hbm.shape[0] // (EXPERT_ROWS * TILE_ROWS)

    def zero_copy(row, rows):
        src = zero_blk.at[pl.ds(0, rows * TILE_ROWS), :]
        dst = xs_hbm.at[pl.ds(pl.multiple_of(row * TILE_ROWS, TILE_ROWS), rows * TILE_ROWS), :]
        return pltpu.make_async_copy(src, dst, pad_sem)

    def for_each_pad(fn):
        def per_expert(e, carry):
            row, left = pad_start_ref[e], pad_len_ref[e]
            size = EXPERT_ROWS // 2
            while size >= 1:
                take = left & size

                @pl.when(take != 0)
                def _(row=row, size=size):
                    fn(zero_copy(row, size))

                row = row + take
                size //= 2
            return carry
        lax.fori_loop(0, N_EXPERTS, per_expert, 0)

        def per_block(b, carry):
            fn(zero_copy(b * EXPERT_ROWS, EXPERT_ROWS))
            return carry
        lax.fori_loop(nu_ref[0], n_blocks, per_block, 0)

    @pl.when(i == 0)
    def _():
        zero_blk[...] = jnp.zeros_like(zero_blk)
        for_each_pad(lambda cp: cp.start())

    def issue(o, carry):
        dsts = [dst_ref[o * _ISSUE_UNROLL + r] for r in range(_ISSUE_UNROLL)]
        tok0 = o * (_ISSUE_UNROLL // TOP_K)
        for r in range(_ISSUE_UNROLL):
            pltpu.make_async_copy(_token_tile(hf_ref, tok0 + r // TOP_K), _token_tile(xs_hbm, dsts[r]),
                                  sem).start(priority=r % 2)
        return carry

    lax.fori_loop(0, pairs // _ISSUE_UNROLL, issue, 0)
    for _ in range(TOP_K):
        pltpu.make_async_copy(hf_ref, xs_hbm.at[pl.ds(0, hf_ref.shape[0]), :], sem).wait()

    @pl.when(i == last)
    def _():
        for_each_pad(lambda cp: cp.wait())


def _dispatch(dest, pad_start, pad_len, n_used, hf_tiles, n_rows, tokens_per_step):
    n = dest.shape[0] // TOP_K
    pairs = tokens_per_step * TOP_K
    assert n % tokens_per_step == 0 and _ISSUE_UNROLL % TOP_K == 0
    smem = lambda: pl.BlockSpec(memory_space=pltpu.SMEM)
    return pl.pallas_call(
        _dispatch_kernel,
        grid=(n // tokens_per_step,),
        in_specs=[pl.BlockSpec((pairs,), lambda i: (i,), memory_space=pltpu.SMEM), smem(), smem(), smem(),
                  pl.BlockSpec((tokens_per_step * TILE_ROWS, LANES), lambda i: (i, 0))],
        out_specs=pl.BlockSpec(memory_space=pl.ANY),
        out_shape=jax.ShapeDtypeStruct((n_rows * TILE_ROWS, LANES), F32),
        scratch_shapes=[pltpu.VMEM((EXPERT_ROWS * TILE_ROWS, LANES), F32), pltpu.SemaphoreType.DMA,
                        pltpu.SemaphoreType.DMA],
        compiler_params=pltpu.CompilerParams(
            dimension_semantics=("arbitrary",), vmem_limit_bytes=VMEM_LIMIT),
        name="dispatch",
    )(dest, pad_start, pad_len, n_used, hf_tiles)


def _expert_kernel(be_ref, nu_ref, nxt_ref, par_ref, xs_ref, wu_hbm, bu_ref, wd_hbm, bd_ref, y_ref,
                   wu_f32, wd_f32, wu_bf, wd_bf, wsem):
    i = pl.program_id(0)

    def weight_copies(expert, s):
        return (pltpu.make_async_copy(wu_hbm.at[expert], wu_f32.at[s], wsem.at[0, s]),
                pltpu.make_async_copy(wd_hbm.at[expert], wd_f32.at[s], wsem.at[1, s]))

    @pl.when(i == 0)
    def _():
        for cp in weight_copies(be_ref[0], par_ref[0]):
            cp.start()

    @pl.when((i == 0) | (be_ref[i] != be_ref[jnp.maximum(i - 1, 0)]))
    def _():
        s = par_ref[i]
        for cp in weight_copies(be_ref[i], s):
            cp.wait()

        @pl.when(nxt_ref[i] >= 0)
        def _():
            for cp in weight_copies(nxt_ref[i], 1 - s):
                cp.start()

        wu_bf[...] = wu_f32[s].astype(BF16)
        wd_bf[...] = wd_f32[s].astype(BF16)

    @pl.when(i < nu_ref[0])
    def _():
        xb = _load_token_tiles(xs_ref, 0, EXPERT_ROWS).astype(BF16)
        up = _dot(xb, wu_bf[...]) + bu_ref[0]
        x_glu = jnp.minimum(up[:, :D_FF], SWIGLU_LIMIT)
        x_lin = jnp.clip(up[:, D_FF:], -SWIGLU_LIMIT, SWIGLU_LIMIT)
        act = x_glu * jax.nn.sigmoid(SWIGLU_ALPHA * x_glu) * (x_lin + 1.0)
        _store_token_tiles(y_ref, _dot(act.astype(BF16), wd_bf[...]) + bd_ref[0])

    @pl.when(i >= nu_ref[0])
    def _():
        y_ref[...] = jnp.zeros_like(y_ref)


def _experts(block_expert, n_used, next_expert, parity, xs_tiles, w_up, b_up, w_down, b_down):
    n_blocks = block_expert.shape[0]
    blk = (EXPERT_ROWS * TILE_ROWS, LANES)
    grid_spec = pltpu.PrefetchScalarGridSpec(
        num_scalar_prefetch=4,
        grid=(n_blocks,),
        in_specs=[
            pl.BlockSpec(blk, lambda i, be, nu, nx, pa: (jnp.minimum(i, nu[0] - 1), 0)),
            pl.BlockSpec(memory_space=pl.ANY),
            pl.BlockSpec((1, 1, 2 * D_FF), lambda i, be, nu, nx, pa: (be[i], 0, 0)),
            pl.BlockSpec(memory_space=pl.ANY),
            pl.BlockSpec((1, 1, D_MODEL), lambda i, be, nu, nx, pa: (be[i], 0, 0)),
        ],
        out_specs=pl.BlockSpec(blk, lambda i, be, nu, nx, pa: (i, 0)),
        scratch_shapes=[
            pltpu.VMEM((2, D_MODEL, 2 * D_FF), F32),
            pltpu.VMEM((2, D_FF, D_MODEL), F32),
            pltpu.VMEM((D_MODEL, 2 * D_FF), BF16),
            pltpu.VMEM((D_FF, D_MODEL), BF16),
            pltpu.SemaphoreType.DMA((2, 2)),
        ],
    )
    return pl.pallas_call(
        _expert_kernel,
        grid_spec=grid_spec,
        out_shape=jax.ShapeDtypeStruct((n_blocks * blk[0], LANES), F32),
        compiler_params=pltpu.CompilerParams(
            dimension_semantics=("arbitrary",), vmem_limit_bytes=VMEM_LIMIT),
        name="experts",
    )(block_expert, n_used, next_expert, parity, xs_tiles, w_up, b_up[:, None, :], w_down, b_down[:, None, :])


def _combine_kernel(dst_cur_ref, dst_nxt_ref, y_hbm, h1_ref, rec_ref, g_ref, o_ref, gbuf, sem):
    i = pl.program_id(0)
    last = pl.num_programs(0) - 1
    tm = h1_ref.shape[0]
    rows = TOP_K * tm
    slot = i % 2

    def gather_group(idx_ref, s, row0):
        srcs = [idx_ref[row0 + r] for r in range(_ISSUE_UNROLL)]
        for r in range(_ISSUE_UNROLL):
            dst = gbuf.at[s, pl.ds((row0 + r) * TILE_ROWS, TILE_ROWS), :]
            pltpu.make_async_copy(_token_tile(y_hbm, srcs[r]), dst, sem.at[s]).start(priority=r % 2)

    @pl.when(i == 0)
    def _():
        def body(o, carry):
            gather_group(dst_cur_ref, 0, pl.multiple_of(o * _ISSUE_UNROLL, _ISSUE_UNROLL))
            return carry
        lax.fori_loop(0, rows // _ISSUE_UNROLL, body, 0)

    for g in range(rows // _ISSUE_UNROLL):
        gather_group(dst_nxt_ref, 1 - slot, g * _ISSUE_UNROLL)
    _tiles_wait_copy(y_hbm, gbuf.at[slot], rows, sem.at[slot]).wait()
    acc = h1_ref[...]
    for k in range(TOP_K):
        gate = rec_ref[:, _REC_GATE + k:_REC_GATE + k + 1]
        acc = acc + gate * _load_token_tiles(gbuf.at[slot], k * tm, tm)
    ms = jnp.mean(acc * acc, axis=-1, keepdims=True)
    o_ref[...] = acc * lax.rsqrt(ms + RMS_EPS) * g_ref[...]

    @pl.when(i == last)
    def _():
        _tiles_wait_copy(y_hbm, gbuf.at[1 - slot], rows, sem.at[1 - slot]).wait()


def _combine(dest_kmajor, y, h1, rec, g, tm):
    n = h1.shape[0]
    n_tiles = n // tm
    return pl.pallas_call(
        _combine_kernel,
        grid=(n_tiles,),
        in_specs=[
            pl.BlockSpec((TOP_K * tm,), lambda i: (i,), memory_space=pltpu.SMEM),
            pl.BlockSpec((TOP_K * tm,), lambda i: (jnp.minimum(i + 1, n_tiles - 1),), memory_space=pltpu.SMEM),
            pl.BlockSpec(memory_space=pl.ANY),
            pl.BlockSpec((tm, D_MODEL), lambda i: (i, 0)),
            pl.BlockSpec((tm, _REC_W), lambda i: (i, 0)),
            pl.BlockSpec((1, D_MODEL), lambda i: (0, 0)),
        ],
        out_specs=pl.BlockSpec((tm, D_MODEL), lambda i: (i, 0)),
        out_shape=jax.ShapeDtypeStruct((n, D_MODEL), F32),
        scratch_shapes=[pltpu.VMEM((2, TOP_K * tm * TILE_ROWS, LANES), F32),
                        pltpu.SemaphoreType.DMA((2,))],
        compiler_params=pltpu.CompilerParams(
            dimension_semantics=("arbitrary",), vmem_limit_bytes=VMEM_LIMIT),
        name="combine",
    )(dest_kmajor, dest_kmajor, y, h1, rec, g)


def _routing_tables(rec, counts, n, tm_combine):
    tm = EXPERT_ROWS
    n_blocks = (n * TOP_K + N_EXPERTS * (tm - 1)) // tm
    idx = rec[:, _REC_IDX:_REC_IDX + TOP_K].astype(jnp.int32)
    rank = rec[:, _REC_RANK:_REC_RANK + TOP_K].astype(jnp.int32)
    cnt = counts[0].astype(jnp.int32)
    blocks_e = (cnt + tm - 1) // tm
    blocks_end = jnp.cumsum(blocks_e)
    row_start = (blocks_end - blocks_e) * tm
    n_used = blocks_end[-1]
    dest = row_start[idx] + rank
    blk = jnp.arange(n_blocks, dtype=jnp.int32)
    be = jnp.minimum(jnp.sum(blocks_end[None, :] <= blk[:, None], axis=1), N_EXPERTS - 1).astype(jnp.int32)
    be = jnp.where(blk < n_used, be, be[jnp.maximum(n_used - 1, 0)])
    dest_kmajor = dest.reshape(n // tm_combine, tm_combine, TOP_K).transpose(0, 2, 1).reshape(-1)
    pad_start = row_start + cnt
    pad_len = blocks_e * tm - cnt
    used = blocks_e > 0
    eid = jnp.arange(N_EXPERTS, dtype=jnp.int32)
    at_or_after = lax.cummin(jnp.where(used, eid, N_EXPERTS)[::-1])[::-1]
    after = jnp.concatenate([at_or_after[1:], jnp.full((1,), N_EXPERTS, jnp.int32)])
    next_e = jnp.where(after < N_EXPERTS, after, -1)
    parity_e = (jnp.cumsum(used.astype(jnp.int32)) - 1) % 2
    return (be, n_used.reshape(1).astype(jnp.int32), next_e[be].astype(jnp.int32),
            parity_e[be].astype(jnp.int32), dest.reshape(-1), dest_kmajor,
            pad_start.astype(jnp.int32), pad_len.astype(jnp.int32), n_blocks * tm)


def kernel(x, meta_tokens, mix_norm_g, w_in, ssm_a_re, ssm_a_im, ssm_log_dt, ssm_b_re, ssm_b_im,
           ssm_c_re, ssm_c_im, ssm_d, w_ssm_glu, attn_sinks, w_attn_o, w_out, ffn_norm_g,
           router_w, router_b, w_up, b_up, w_down, b_down, final_norm_g):
    bsz, seq, d = x.shape
    assert d == D_MODEL and seq % max(WINDOW, SSM_CHUNK) == 0
    assert mix_norm_g.shape[0] == 1, "single-layer trunk"
    n = bsz * seq
    tm_proj = min(512, n)
    tm_mix = min(256, n)
    tm_comb = min(128, n)
    x2 = x.reshape(n, D_MODEL)

    w_in_bf = w_in[0].astype(BF16)
    g_mix = mix_norm_g[0][None, :]
    u, q, kv, gs, ga = _in_proj(x2, g_mix, w_in_bf, tm_proj)
    u_m, _, kv_m, _, _ = _in_proj(meta_tokens, g_mix, w_in_bf, N_META)

    ssm_par = _ssm_params(ssm_a_re[0], ssm_a_im[0], ssm_log_dt[0], ssm_b_re[0], ssm_b_im[0],
                          ssm_c_re[0], ssm_c_im[0], ssm_d[0], SSM_CHUNK)
    y_ssm = _ssm(u, u_m, *ssm_par, batch=bsz, chunk=SSM_CHUNK)

    attn = _attention(attn_sinks[0], q, kv, kv_m, seq // WINDOW)

    h1, hf, rec, counts = _mix(
        x2, y_ssm, attn, gs, ga, w_ssm_glu[0].astype(BF16), w_attn_o[0].astype(BF16),
        w_out[0].astype(BF16), ffn_norm_g[0][None, :], router_w[0], router_b[0][None, :], tm_mix)

    be, n_used, next_e, parity, dest, dest_kmajor, pad_start, pad_len, n_rows = _routing_tables(
        rec, counts, n, tm_comb)
    xs = _dispatch(dest, pad_start, pad_len, n_used, hf, n_rows, min(1024, n))
    y = _experts(be, n_used, next_e, parity, xs, w_up[0], b_up[0], w_down[0], b_down[0])
    out = _combine(dest_kmajor, y, h1, rec, final_norm_g[None, :], tm_comb)
    return out.reshape(bsz, seq, D_MODEL)
```

```python
import functools
import math

import jax
import jax.numpy as jnp
from jax import lax
from jax.experimental import pallas as pl
from jax.experimental.pallas import tpu as pltpu

F32 = jnp.float32
BF16 = jnp.bfloat16

D_MODEL = 1024
N_META = 16
SSM_WIDTH = 512
SSM_GROUP = 16
SSM_GROUPS = 32
SSM_STATE = 64
HEAD_DIM = 64
N_HEADS = 16
N_KV_HEADS = 2
KV_REP = N_HEADS // N_KV_HEADS
WINDOW = 128
Q_WIDTH = N_HEADS * HEAD_DIM
KV_WIDTH = N_KV_HEADS * HEAD_DIM
N_EXPERTS = 32
TOP_K = 4
D_FF = 1024
SWIGLU_ALPHA = 1.702
SWIGLU_LIMIT = 7.0
RMS_EPS = 1e-5
NEG_INF = -1e30

_U0, _Q0, _KV0, _GS0, _GA0, _IN_END = 0, 512, 1536, 1792, 2816, 3840

SSM_CH_BLOCK = 128
SSM_HALF = (SSM_CH_BLOCK // SSM_GROUP) * SSM_STATE
SSM_CHUNK = 32
EXPERT_ROWS = 256
VMEM_LIMIT = 56 * 1024 * 1024


def _dot(a, b):
    return jnp.dot(a, b, preferred_element_type=F32)


def _dot_nt(a, b):
    return lax.dot_general(a, b, (((1,), (1,)), ((), ())), preferred_element_type=F32)


LANES = 128
TILE_ROWS = D_MODEL // LANES


def _store_token_tiles(ref, x):
    rows = x.shape[0]
    for j in range(TILE_ROWS):
        ref[pl.ds(j, rows, stride=TILE_ROWS), :] = x[:, j * LANES:(j + 1) * LANES]


def _load_token_tiles(ref, start_row, rows):
    return jnp.concatenate(
        [ref[pl.ds(start_row * TILE_ROWS + j, rows, stride=TILE_ROWS), :] for j in range(TILE_ROWS)], axis=1)


def _token_tile(ref, row):
    return ref.at[pl.ds(pl.multiple_of(row * TILE_ROWS, TILE_ROWS), TILE_ROWS), :]


def _in_proj_kernel(x_ref, g_ref, w_ref, u_ref, q_ref, kv_ref, gs_ref, ga_ref):
    x = x_ref[...]
    ms = jnp.mean(x * x, axis=-1, keepdims=True)
    hn = (x * lax.rsqrt(ms + RMS_EPS) * g_ref[...]).astype(BF16)
    u_ref[...] = _dot(hn, w_ref[:, _U0:_Q0])
    q_ref[...] = (_dot(hn, w_ref[:, _Q0:_KV0]) * (HEAD_DIM ** -0.5)).astype(BF16)
    kv_ref[...] = _dot(hn, w_ref[:, _KV0:_GS0]).astype(BF16)
    gs_ref[...] = jax.nn.sigmoid(_dot(hn, w_ref[:, _GS0:_GA0])).astype(BF16)
    ga_ref[...] = jax.nn.sigmoid(_dot(hn, w_ref[:, _GA0:_IN_END])).astype(BF16)


def _in_proj(x2, g, w_bf, tm):
    n = x2.shape[0]
    row = lambda w: pl.BlockSpec((tm, w), lambda i: (i, 0))
    full = lambda a: pl.BlockSpec(a.shape, lambda i: (0,) * a.ndim)
    return pl.pallas_call(
        _in_proj_kernel,
        grid=(n // tm,),
        in_specs=[row(D_MODEL), full(g), full(w_bf)],
        out_specs=[row(SSM_WIDTH), row(Q_WIDTH), row(2 * KV_WIDTH), row(D_MODEL), row(D_MODEL)],
        out_shape=[
            jax.ShapeDtypeStruct((n, SSM_WIDTH), F32),
            jax.ShapeDtypeStruct((n, Q_WIDTH), BF16),
            jax.ShapeDtypeStruct((n, 2 * KV_WIDTH), BF16),
            jax.ShapeDtypeStruct((n, D_MODEL), BF16),
            jax.ShapeDtypeStruct((n, D_MODEL), BF16),
        ],
        compiler_params=pltpu.CompilerParams(
            dimension_semantics=("arbitrary",), vmem_limit_bytes=VMEM_LIMIT),
        name="in_proj",
    )(x2, g, w_bf)


def _ssm_kernel(u_ref, um_ref, bm_ref, cm_ref, ar_ref, ai_ref, atr_ref, ati_ref, d_ref,
                y_ref, sre, sim, *, chunk, rows, batch):
    h = SSM_HALF
    bm = bm_ref[0]
    cm = cm_ref[0]
    ar, ai = ar_ref[0], ai_ref[0]
    atr, ati = atr_ref[0], ati_ref[0]
    dsk = d_ref[0]
    n_chunks = rows // batch

    def advance(sr, si, bu):
        return ar * sr - ai * si + bu[:, :h], ar * si + ai * sr + bu[:, h:]

    bum = _dot(um_ref[...].astype(BF16), bm)
    mr = jnp.zeros((1, h), F32)
    mi = jnp.zeros((1, h), F32)
    for j in range(N_META):
        mr, mi = advance(mr, mi, bum[j:j + 1, :])

    def u_step(t):
        return u_ref[pl.ds(t, rows, stride=chunk), :]

    sre[...] = jnp.zeros_like(sre)
    sim[...] = jnp.zeros_like(sim)

    def pass_a(t, carry):
        bu = _dot(u_step(t).astype(BF16), bm)
        nr, ni = advance(sre[...], sim[...], bu)
        sre[...] = nr
        sim[...] = ni
        return carry

    lax.fori_loop(0, chunk, pass_a, 0)

    def over_chunks(c, carry):
        new = []
        for b in range(batch):
            cr, ci = carry[2 * b], carry[2 * b + 1]
            row = pl.ds(b * n_chunks + c, 1)
            er, ei = sre[row, :], sim[row, :]
            sre[row, :] = cr
            sim[row, :] = ci
            new += [atr * cr - ati * ci + er, atr * ci + ati * cr + ei]
        return tuple(new)

    lax.fori_loop(0, n_chunks, over_chunks, (mr, mi) * batch)

    def pass_b(t, carry):
        ut = u_step(t)
        bu = _dot(ut.astype(BF16), bm)
        nr, ni = advance(sre[...], sim[...], bu)
        sre[...] = nr
        sim[...] = ni
        y = _dot(nr.astype(BF16), cm[:h, :]) + _dot(ni.astype(BF16), cm[h:, :]) + dsk * ut
        y_ref[pl.ds(t, rows, stride=chunk), :] = y
        return carry

    lax.fori_loop(0, chunk, pass_b, 0)


def _ssm(u, u_meta, bmat, cmat, a_re, a_im, at_re, at_im, dskip, batch, chunk):
    n = u.shape[0]
    rows = n // chunk
    nblk = SSM_WIDTH // SSM_CH_BLOCK
    col = lambda r: pl.BlockSpec((r, SSM_CH_BLOCK), lambda j: (0, j))
    par = lambda a: pl.BlockSpec((1,) + a.shape[1:], lambda j: (j, 0, 0))
    return pl.pallas_call(
        functools.partial(_ssm_kernel, chunk=chunk, rows=rows, batch=batch),
        grid=(nblk,),
        in_specs=[col(n), col(N_META), par(bmat), par(cmat), par(a_re), par(a_im),
                  par(at_re), par(at_im), par(dskip)],
        out_specs=col(n),
        out_shape=jax.ShapeDtypeStruct((n, SSM_WIDTH), F32),
        scratch_shapes=[pltpu.VMEM((rows, SSM_HALF), F32), pltpu.VMEM((rows, SSM_HALF), F32)],
        compiler_params=pltpu.CompilerParams(
            dimension_semantics=("arbitrary",), vmem_limit_bytes=VMEM_LIMIT),
        name="ssm",
    )(u, u_meta, bmat, cmat, a_re, a_im, at_re, at_im, dskip)


def _ssm_params(a_re, a_im, log_dt, b_re, b_im, c_re, c_im, d_skip, chunk):
    dt = jnp.exp(log_dt)[:, None]
    mag = jnp.exp(a_re * dt)
    ang = a_im * dt
    abar_re, abar_im = mag * jnp.cos(ang), mag * jnp.sin(ang)
    den = a_re * a_re + a_im * a_im
    nr, ni = abar_re - 1.0, abar_im
    coef_re = ((nr * a_re + ni * a_im) / den)[..., None]
    coef_im = ((ni * a_re - nr * a_im) / den)[..., None]
    bbar_re = coef_re * b_re - coef_im * b_im
    bbar_im = coef_re * b_im + coef_im * b_re
    magt = jnp.exp(a_re * dt * chunk)
    at_re, at_im = magt * jnp.cos(ang * chunk), magt * jnp.sin(ang * chunk)

    nblk = SSM_WIDTH // SSM_CH_BLOCK
    gpb = SSM_GROUPS // nblk
    eye = jnp.eye(gpb, dtype=F32)

    def in_map(b):
        b = b.reshape(nblk, gpb, SSM_STATE, SSM_GROUP)
        return jnp.einsum('jgpc,gh->jgchp', b, eye).reshape(nblk, SSM_CH_BLOCK, gpb * SSM_STATE)

    def out_map(c):
        c = c.reshape(nblk, gpb, SSM_GROUP, SSM_STATE)
        return jnp.einsum('jgcp,gh->jgphc', c, eye).reshape(nblk, gpb * SSM_STATE, SSM_CH_BLOCK)

    bmat = jnp.concatenate([in_map(bbar_re), in_map(bbar_im)], axis=2).astype(BF16)
    cmat = jnp.concatenate([out_map(c_re), -out_map(c_im)], axis=1).astype(BF16)
    vec = lambda v: v.reshape(nblk, 1, SSM_HALF)
    return (bmat, cmat, vec(abar_re), vec(abar_im), vec(at_re), vec(at_im),
            d_skip.reshape(nblk, 1, SSM_CH_BLOCK))


def _attn_kernel(sink_ref, q_ref, kvc_ref, kvp_ref, kvm_ref, o_ref, *, blocks_per_seq):
    n = pl.program_id(0) % blocks_per_seq
    w = WINDOW
    hd = HEAD_DIM
    qi = lax.broadcasted_iota(jnp.int32, (w, w), 0)
    lane = lax.broadcasted_iota(jnp.int32, (w, w), 1)
    vis_prev = (lane > qi) & (n > 0)
    vis_cur = lane <= qi
    left = lane < hd
    meta_l = lane < N_META
    meta_r = (lane >= N_META) & (lane < 2 * N_META)

    def placed(x_bf):
        x = x_bf.astype(F32)
        xr = pltpu.roll(x, hd, 1)
        lm = lax.broadcasted_iota(jnp.int32, x.shape, 1) < hd
        z = jnp.zeros_like(x)
        return {(0, 0): jnp.where(lm, x, z), (0, 1): jnp.where(lm, z, xr),
                (1, 0): jnp.where(lm, xr, z), (1, 1): jnp.where(lm, z, x)}

    kp, kc, km = placed(kvp_ref[:, :KV_WIDTH]), placed(kvc_ref[:, :KV_WIDTH]), placed(kvm_ref[:, :KV_WIDTH])
    vp, vc, vm = placed(kvp_ref[:, KV_WIDTH:]), placed(kvc_ref[:, KV_WIDTH:]), placed(kvm_ref[:, KV_WIDTH:])
    pad_rows = w - 2 * N_META
    zpad = jnp.zeros((pad_rows, w), F32)
    krow = lax.broadcasted_iota(jnp.int32, (5 * w, w), 0)
    klane = lax.broadcasted_iota(jnp.int32, (5 * w, w), 1)
    row_l = (krow < 2 * w) | ((krow >= 4 * w) & (krow < 4 * w + N_META))
    row_r = ((krow >= 2 * w) & (krow < 4 * w)) | ((krow >= 4 * w + N_META) & (krow < 4 * w + 2 * N_META))
    den_cols = jnp.where((row_l & (klane < hd)) | (row_r & (klane >= hd)), 1.0, 0.0)

    for j in range(N_KV_HEADS):
        kcat = jnp.concatenate(
            [kp[j, 0], kc[j, 0], kp[j, 1], kc[j, 1], km[j, 0], km[j, 1], zpad], axis=0).astype(BF16)
        vcat = jnp.concatenate(
            [jnp.concatenate([vp[j, 0], vc[j, 0], vp[j, 1], vc[j, 1], vm[j, 0], vm[j, 1], zpad], axis=0),
             den_cols], axis=1).astype(BF16)
        for r in range(KV_REP // 2):
            pr = j * (KV_REP // 2) + r
            s = _dot_nt(q_ref[:, pr * w:(pr + 1) * w], kcat)
            s_lp = jnp.where(vis_prev, s[:, 0:w], NEG_INF)
            s_lc = jnp.where(vis_cur, s[:, w:2 * w], NEG_INF)
            s_rp = jnp.where(vis_prev, s[:, 2 * w:3 * w], NEG_INF)
            s_rc = jnp.where(vis_cur, s[:, 3 * w:4 * w], NEG_INF)
            s_m = s[:, 4 * w:]
            sink_l, sink_r = sink_ref[2 * pr], sink_ref[2 * pr + 1]
            m_l = jnp.maximum(jnp.max(jnp.maximum(jnp.maximum(s_lp, s_lc), jnp.where(meta_l, s_m, NEG_INF)),
                                      axis=1, keepdims=True), sink_l)
            m_r = jnp.maximum(jnp.max(jnp.maximum(jnp.maximum(s_rp, s_rc), jnp.where(meta_r, s_m, NEG_INF)),
                                      axis=1, keepdims=True), sink_r)
            s_m = jnp.where(meta_l, s_m - m_l, jnp.where(meta_r, s_m - m_r, NEG_INF))
            e = jnp.concatenate([jnp.exp(s_lp - m_l), jnp.exp(s_lc - m_l), jnp.exp(s_rp - m_r),
                                 jnp.exp(s_rc - m_r), jnp.exp(s_m)], axis=1).astype(BF16)
            acc = _dot(e, vcat)
            den = acc[:, w:] + jnp.where(left, jnp.exp(sink_l - m_l), jnp.exp(sink_r - m_r))
            o_ref[:, pr * w:(pr + 1) * w] = (acc[:, :w] / den).astype(BF16)


def _attention(sinks, q, kv, kv_meta, blocks_per_seq):
    n = q.shape[0]
    return pl.pallas_call(
        functools.partial(_attn_kernel, blocks_per_seq=blocks_per_seq),
        grid=(n // WINDOW,),
        in_specs=[
            pl.BlockSpec(memory_space=pltpu.SMEM),
            pl.BlockSpec((WINDOW, Q_WIDTH), lambda g: (g, 0)),
            pl.BlockSpec((WINDOW, 2 * KV_WIDTH), lambda g: (g, 0)),
            pl.BlockSpec((WINDOW, 2 * KV_WIDTH), lambda g: (jnp.maximum(g - 1, 0), 0)),
            pl.BlockSpec((N_META, 2 * KV_WIDTH), lambda g: (0, 0)),
        ],
        out_specs=pl.BlockSpec((WINDOW, Q_WIDTH), lambda g: (g, 0)),
        out_shape=jax.ShapeDtypeStruct((n, Q_WIDTH), BF16),
        compiler_params=pltpu.CompilerParams(dimension_semantics=("arbitrary",)),
        name="attn",
    )(sinks, q, kv, kv, kv_meta)


_ROUTE_IDX, _ROUTE_RANK, _ROUTE_GATE, _ROUTE_ROWS = 0, TOP_K, 2 * TOP_K, 16


def _mix_kernel(x_ref, y_ref, at_ref, gs_ref, ga_ref, wglu_ref, wo_ref, wout_ref, fg_ref, rwh_ref, rwl_ref, rb_ref,
                h1_ref, hf_ref, route_ref, cnt_ref, cnt_scr):
    tm = x_ref.shape[0]

    @pl.when(pl.program_id(0) == 0)
    def _():
        cnt_scr[...] = jnp.zeros_like(cnt_scr)

    glu = _dot(jax.nn.gelu(y_ref[...]).astype(BF16), wglu_ref[...])
    branch_ssm = glu[:, :D_MODEL] * jax.nn.sigmoid(glu[:, D_MODEL:])
    branch_attn = _dot(at_ref[...], wo_ref[...])
    merged = gs_ref[...].astype(F32) * branch_ssm + ga_ref[...].astype(F32) * branch_attn
    h1 = x_ref[...] + _dot(merged.astype(BF16), wout_ref[...])
    h1_ref[...] = h1
    ms = jnp.mean(h1 * h1, axis=-1, keepdims=True)
    hf = h1 * lax.rsqrt(ms + RMS_EPS) * fg_ref[...]
    _store_token_tiles(hf_ref, hf)

    hf_hi = hf.astype(BF16)
    hf_lo = (hf - hf_hi.astype(F32)).astype(BF16)
    logits = (_dot(hf_hi, rwh_ref[...]) + (_dot(hf_hi, rwl_ref[...]) + _dot(hf_lo, rwh_ref[...]))
              + rb_ref[...])
    lt = logits.T[:N_EXPERTS, :]
    erow = lax.broadcasted_iota(jnp.int32, (N_EXPERTS, tm), 0)
    vals, idxs, hots = [], [], []
    rest = lt
    for _ in range(TOP_K):
        m = jnp.max(rest, axis=0, keepdims=True)
        first = jnp.min(jnp.where(rest == m, erow, N_EXPERTS), axis=0, keepdims=True)
        hot = erow == first
        vals.append(m)
        idxs.append(first)
        hots.append(hot)
        rest = jnp.where(hot, -jnp.inf, rest)
    exps = [jnp.exp(v - vals[0]) for v in vals]
    tot = exps[0] + exps[1] + exps[2] + exps[3]

    sel = (hots[0] | hots[1] | hots[2] | hots[3]).astype(F32)
    ti = lax.broadcasted_iota(jnp.int32, (tm, tm), 0)
    tj = lax.broadcasted_iota(jnp.int32, (tm, tm), 1)
    earlier = (ti < tj).astype(BF16)
    rank_e = _dot(sel.astype(BF16), earlier) + cnt_scr[...]
    cnt_scr[...] = cnt_scr[...] + jnp.sum(sel, axis=1, keepdims=True)
    cnt_ref[...] = cnt_scr[...]

    rrow = lax.broadcasted_iota(jnp.int32, (_ROUTE_ROWS, tm), 0)
    route = jnp.zeros((_ROUTE_ROWS, tm), F32)
    for k in range(TOP_K):
        rank_k = jnp.sum(jnp.where(hots[k], rank_e, 0.0), axis=0, keepdims=True)
        route = jnp.where(rrow == _ROUTE_IDX + k, idxs[k].astype(F32), route)
        route = jnp.where(rrow == _ROUTE_RANK + k, rank_k, route)
        route = jnp.where(rrow == _ROUTE_GATE + k, exps[k] / tot, route)
    route_ref[...] = route


def _mix(x2, y, attn, gs, ga, wglu, wo, wout, fg, rw, rb, tm):
    n = x2.shape[0]
    rw = jnp.pad(rw, ((0, 0), (0, LANES - N_EXPERTS)))
    rb = jnp.pad(rb, ((0, 0), (0, LANES - N_EXPERTS)), constant_values=NEG_INF)
    rw_hi = rw.astype(BF16)
    rw_lo = (rw - rw_hi.astype(F32)).astype(BF16)
    row = lambda w: pl.BlockSpec((tm, w), lambda i: (i, 0))
    full = lambda a: pl.BlockSpec(a.shape, lambda i: (0,) * a.ndim)
    return pl.pallas_call(
        _mix_kernel,
        grid=(n // tm,),
        in_specs=[row(D_MODEL), row(SSM_WIDTH), row(Q_WIDTH), row(D_MODEL), row(D_MODEL),
                  full(wglu), full(wo), full(wout), full(fg), full(rw_hi), full(rw_lo), full(rb)],
        out_specs=[row(D_MODEL), pl.BlockSpec((tm * TILE_ROWS, LANES), lambda i: (i, 0)),
                   pl.BlockSpec((_ROUTE_ROWS, tm), lambda i: (0, i)),
                   pl.BlockSpec((N_EXPERTS, 1), lambda i: (0, 0))],
        out_shape=[
            jax.ShapeDtypeStruct((n, D_MODEL), F32),
            jax.ShapeDtypeStruct((n * TILE_ROWS, LANES), F32),
            jax.ShapeDtypeStruct((_ROUTE_ROWS, n), F32),
            jax.ShapeDtypeStruct((N_EXPERTS, 1), F32),
        ],
        scratch_shapes=[pltpu.VMEM((N_EXPERTS, 1), F32)],
        compiler_params=pltpu.CompilerParams(
            dimension_semantics=("arbitrary",), vmem_limit_bytes=VMEM_LIMIT),
        name="mix_router",
    )(x2, y, attn, gs, ga, wglu, wo, wout, fg, rw_hi, rw_lo, rb)


def _tiles_wait_copy(src_hbm, dst, n_tiles, sem):
    rows = n_tiles * TILE_ROWS
    return pltpu.make_async_copy(src_hbm.at[pl.ds(0, rows), :], dst.at[pl.ds(0, rows), :], sem)


_ISSUE_UNROLL = 16


def _dispatch_kernel(dst_ref, pad_start_ref, pad_len_ref, nu_ref, hf_ref, xs_hbm, zero_blk, sem, pad_sem):
    i = pl.program_id(0)
    last = pl.num_programs(0) - 1
    pairs = dst_ref.shape[0]
    n_blocks = xs_hbm.shape[0] // (EXPERT_ROWS * TILE_ROWS)

    def zero_copy(row, rows):
        src = zero_blk.at[pl.ds(0, rows * TILE_ROWS), :]
        dst = xs_hbm.at[pl.ds(pl.multiple_of(row * TILE_ROWS, TILE_ROWS), rows * TILE_ROWS), :]
        return pltpu.make_async_copy(src, dst, pad_sem)

    def for_each_pad(fn):
        def per_expert(e, carry):
            row, left = pad_start_ref[e], pad_len_ref[e]
            size = EXPERT_ROWS // 2
            while size >= 1:
                take = left & size

                @pl.when(take != 0)
                def _(row=row, size=size):
                    fn(zero_copy(row, size))

                row = row + take
                size //= 2
            return carry
        lax.fori_loop(0, N_EXPERTS, per_expert, 0)

        def per_block(b, carry):
            fn(zero_copy(b * EXPERT_ROWS, EXPERT_ROWS))
            return carry
        lax.fori_loop(nu_ref[0], n_blocks, per_block, 0)

    @pl.when(i == 0)
    def _():
        zero_blk[...] = jnp.zeros_like(zero_blk)
        for_each_pad(lambda cp: cp.start())

    tokens = pairs // TOP_K
    for k in range(TOP_K):
        def issue(o, carry, k=k):
            tok0 = pl.multiple_of(o * _ISSUE_UNROLL, _ISSUE_UNROLL)
            dsts = [dst_ref[k * tokens + tok0 + r] for r in range(_ISSUE_UNROLL)]
            for r in range(_ISSUE_UNROLL):
                pltpu.make_async_copy(_token_tile(hf_ref, tok0 + r), _token_tile(xs_hbm, dsts[r]),
                                      sem).start(priority=r % 2)
            return carry

        lax.fori_loop(0, tokens // _ISSUE_UNROLL, issue, 0)
    for _ in range(TOP_K):
        pltpu.make_async_copy(hf_ref, xs_hbm.at[pl.ds(0, hf_ref.shape[0]), :], sem).wait()

    @pl.when(i == last)
    def _():
        for_each_pad(lambda cp: cp.wait())


def _dispatch(dest, pad_start, pad_len, n_used, hf_tiles, n_rows, tokens_per_step):
    n = dest.shape[0] // TOP_K
    pairs = tokens_per_step * TOP_K
    assert n % tokens_per_step == 0 and tokens_per_step % _ISSUE_UNROLL == 0
    smem = lambda: pl.BlockSpec(memory_space=pltpu.SMEM)
    return pl.pallas_call(
        _dispatch_kernel,
        grid=(n // tokens_per_step,),
        in_specs=[pl.BlockSpec((pairs,), lambda i: (i,), memory_space=pltpu.SMEM), smem(), smem(), smem(),
                  pl.BlockSpec((tokens_per_step * TILE_ROWS, LANES), lambda i: (i, 0))],
        out_specs=pl.BlockSpec(memory_space=pl.ANY),
        out_shape=jax.ShapeDtypeStruct((n_rows * TILE_ROWS, LANES), F32),
        scratch_shapes=[pltpu.VMEM((EXPERT_ROWS * TILE_ROWS, LANES), F32), pltpu.SemaphoreType.DMA,
                        pltpu.SemaphoreType.DMA],
        compiler_params=pltpu.CompilerParams(
            dimension_semantics=("arbitrary",), vmem_limit_bytes=VMEM_LIMIT),
        name="dispatch",
    )(dest, pad_start, pad_len, n_used, hf_tiles)


def _expert_kernel(be_ref, nu_ref, nxt_ref, par_ref, xs_ref, wu_hbm, bu_ref, wd_hbm, bd_ref, y_ref,
                   wu_f32, wd_f32, wu_bf, wd_bf, wsem):
    i = pl.program_id(0)

    def weight_copies(expert, s):
        return (pltpu.make_async_copy(wu_hbm.at[expert], wu_f32.at[s], wsem.at[0, s]),
                pltpu.make_async_copy(wd_hbm.at[expert], wd_f32.at[s], wsem.at[1, s]))

    @pl.when(i == 0)
    def _():
        for cp in weight_copies(be_ref[0], par_ref[0]):
            cp.start()

    @pl.when((i == 0) | (be_ref[i] != be_ref[jnp.maximum(i - 1, 0)]))
    def _():
        s = par_ref[i]
        for cp in weight_copies(be_ref[i], s):
            cp.wait()

        @pl.when(nxt_ref[i] >= 0)
        def _():
            for cp in weight_copies(nxt_ref[i], 1 - s):
                cp.start()

        wu_bf[...] = wu_f32[s].astype(BF16)
        wd_bf[...] = wd_f32[s].astype(BF16)

    @pl.when(i < nu_ref[0])
    def _():
        xb = _load_token_tiles(xs_ref, 0, EXPERT_ROWS).astype(BF16)
        up = _dot(xb, wu_bf[...]) + bu_ref[0]
        x_glu = jnp.minimum(up[:, :D_FF], SWIGLU_LIMIT)
        x_lin = jnp.clip(up[:, D_FF:], -SWIGLU_LIMIT, SWIGLU_LIMIT)
        act = x_glu * jax.nn.sigmoid(SWIGLU_ALPHA * x_glu) * (x_lin + 1.0)
        _store_token_tiles(y_ref, _dot(act.astype(BF16), wd_bf[...]) + bd_ref[0])

    @pl.when(i >= nu_ref[0])
    def _():
        y_ref[...] = jnp.zeros_like(y_ref)


def _experts(block_expert, n_used, next_expert, parity, xs_tiles, w_up, b_up, w_down, b_down):
    n_blocks = block_expert.shape[0]
    blk = (EXPERT_ROWS * TILE_ROWS, LANES)
    grid_spec = pltpu.PrefetchScalarGridSpec(
        num_scalar_prefetch=4,
        grid=(n_blocks,),
        in_specs=[
            pl.BlockSpec(blk, lambda i, be, nu, nx, pa: (jnp.minimum(i, nu[0] - 1), 0)),
            pl.BlockSpec(memory_space=pl.ANY),
            pl.BlockSpec((1, 1, 2 * D_FF), lambda i, be, nu, nx, pa: (be[i], 0, 0)),
            pl.BlockSpec(memory_space=pl.ANY),
            pl.BlockSpec((1, 1, D_MODEL), lambda i, be, nu, nx, pa: (be[i], 0, 0)),
        ],
        out_specs=pl.BlockSpec(blk, lambda i, be, nu, nx, pa: (i, 0)),
        scratch_shapes=[
            pltpu.VMEM((2, D_MODEL, 2 * D_FF), F32),
            pltpu.VMEM((2, D_FF, D_MODEL), F32),
            pltpu.VMEM((D_MODEL, 2 * D_FF), BF16),
            pltpu.VMEM((D_FF, D_MODEL), BF16),
            pltpu.SemaphoreType.DMA((2, 2)),
        ],
    )
    return pl.pallas_call(
        _expert_kernel,
        grid_spec=grid_spec,
        out_shape=jax.ShapeDtypeStruct((n_blocks * blk[0], LANES), F32),
        compiler_params=pltpu.CompilerParams(
            dimension_semantics=("arbitrary",), vmem_limit_bytes=VMEM_LIMIT),
        name="experts",
    )(block_expert, n_used, next_expert, parity, xs_tiles, w_up, b_up[:, None, :], w_down, b_down[:, None, :])


def _combine_kernel(dst_cur_ref, dst_nxt_ref, y_hbm, h1_ref, route_ref, g_ref, o_ref, gbuf, sem):
    i = pl.program_id(0)
    last = pl.num_programs(0) - 1
    tm = h1_ref.shape[0]
    rows = TOP_K * tm
    slot = i % 2

    def gather_group(idx_ref, s, row0):
        srcs = [idx_ref[row0 + r] for r in range(_ISSUE_UNROLL)]
        for r in range(_ISSUE_UNROLL):
            dst = gbuf.at[s, pl.ds((row0 + r) * TILE_ROWS, TILE_ROWS), :]
            pltpu.make_async_copy(_token_tile(y_hbm, srcs[r]), dst, sem.at[s]).start(priority=r % 2)

    @pl.when(i == 0)
    def _():
        def body(o, carry):
            gather_group(dst_cur_ref, 0, pl.multiple_of(o * _ISSUE_UNROLL, _ISSUE_UNROLL))
            return carry
        lax.fori_loop(0, rows // _ISSUE_UNROLL, body, 0)

    for g in range(rows // _ISSUE_UNROLL):
        gather_group(dst_nxt_ref, 1 - slot, g * _ISSUE_UNROLL)
    _tiles_wait_copy(y_hbm, gbuf.at[slot], rows, sem.at[slot]).wait()
    assert tm == LANES
    rt = jnp.concatenate([route_ref[...], jnp.zeros((LANES - _ROUTE_ROWS, tm), F32)], axis=0).T
    acc = h1_ref[...]
    for k in range(TOP_K):
        gate = rt[:, _ROUTE_GATE + k:_ROUTE_GATE + k + 1]
        acc = acc + gate * _load_token_tiles(gbuf.at[slot], k * tm, tm)
    ms = jnp.mean(acc * acc, axis=-1, keepdims=True)
    o_ref[...] = acc * lax.rsqrt(ms + RMS_EPS) * g_ref[...]

    @pl.when(i == last)
    def _():
        _tiles_wait_copy(y_hbm, gbuf.at[1 - slot], rows, sem.at[1 - slot]).wait()


def _combine(dest_kmajor, y, h1, route, g, tm):
    n = h1.shape[0]
    n_tiles = n // tm
    return pl.pallas_call(
        _combine_kernel,
        grid=(n_tiles,),
        in_specs=[
            pl.BlockSpec((TOP_K * tm,), lambda i: (i,), memory_space=pltpu.SMEM),
            pl.BlockSpec((TOP_K * tm,), lambda i: (jnp.minimum(i + 1, n_tiles - 1),), memory_space=pltpu.SMEM),
            pl.BlockSpec(memory_space=pl.ANY),
            pl.BlockSpec((tm, D_MODEL), lambda i: (i, 0)),
            pl.BlockSpec((_ROUTE_ROWS, tm), lambda i: (0, i)),
            pl.BlockSpec((1, D_MODEL), lambda i: (0, 0)),
        ],
        out_specs=pl.BlockSpec((tm, D_MODEL), lambda i: (i, 0)),
        out_shape=jax.ShapeDtypeStruct((n, D_MODEL), F32),
        scratch_shapes=[pltpu.VMEM((2, TOP_K * tm * TILE_ROWS, LANES), F32),
                        pltpu.SemaphoreType.DMA((2,))],
        compiler_params=pltpu.CompilerParams(
            dimension_semantics=("arbitrary",), vmem_limit_bytes=VMEM_LIMIT),
        name="combine",
    )(dest_kmajor, dest_kmajor, y, h1, route, g)


def _routing_tables(route, counts, n, tokens_dispatch, tokens_combine):
    tm = EXPERT_ROWS
    i32 = jnp.int32
    n_blocks = (n * TOP_K + N_EXPERTS * (tm - 1)) // tm
    idx = route[_ROUTE_IDX:_ROUTE_IDX + TOP_K].astype(i32)
    rank = route[_ROUTE_RANK:_ROUTE_RANK + TOP_K].astype(i32)
    cnt = counts[:, 0].astype(i32)
    eid = jnp.arange(N_EXPERTS, dtype=i32)
    upto = eid[None, :] <= eid[:, None]
    blocks_e = (cnt + tm - 1) // tm
    blocks_end = jnp.sum(jnp.where(upto, blocks_e[None, :], 0), axis=1)
    row_start = (blocks_end - blocks_e) * tm
    n_used = blocks_end[N_EXPERTS - 1]
    used = blocks_e > 0

    def lookup(table, keys):
        hit = keys[None] == eid.reshape((N_EXPERTS,) + (1,) * keys.ndim)
        return jnp.sum(jnp.where(hit, table.reshape((N_EXPERTS,) + (1,) * keys.ndim), 0), axis=0)

    dest = lookup(row_start, idx) + rank
    blk = jnp.arange(n_blocks, dtype=i32)
    last_used = jnp.max(jnp.where(used, eid, 0))
    be = jnp.where(blk < n_used, jnp.sum((blocks_end[None, :] <= blk[:, None]).astype(i32), axis=1), last_used)
    later_used = used[None, :] & (eid[None, :] > eid[:, None])
    after = jnp.min(jnp.where(later_used, eid[None, :], N_EXPERTS), axis=1)
    next_e = jnp.where(after < N_EXPERTS, after, -1)
    parity_e = (jnp.sum((upto & used[None, :]).astype(i32), axis=1) - 1) % 2

    def k_major(tokens):
        return dest.reshape(TOP_K, n // tokens, tokens).transpose(1, 0, 2).reshape(-1)

    return (be.astype(i32), n_used.reshape(1), lookup(next_e, be).astype(i32), lookup(parity_e, be).astype(i32),
            k_major(tokens_dispatch), k_major(tokens_combine), row_start + cnt, blocks_e * tm - cnt,
            n_blocks * tm)


def kernel(x, meta_tokens, mix_norm_g, w_in, ssm_a_re, ssm_a_im, ssm_log_dt, ssm_b_re, ssm_b_im,
           ssm_c_re, ssm_c_im, ssm_d, w_ssm_glu, attn_sinks, w_attn_o, w_out, ffn_norm_g,
           router_w, router_b, w_up, b_up, w_down, b_down, final_norm_g):
    bsz, seq, d = x.shape
    assert d == D_MODEL and seq % max(WINDOW, SSM_CHUNK) == 0
    assert mix_norm_g.shape[0] == 1, "single-layer trunk"
    n = bsz * seq
    tm_proj = min(512, n)
    tm_mix = min(256, n)
    tm_comb = min(128, n)
    x2 = x.reshape(n, D_MODEL)

    w_in_bf = w_in[0].astype(BF16)
    g_mix = mix_norm_g[0][None, :]
    u, q, kv, gs, ga = _in_proj(x2, g_mix, w_in_bf, tm_proj)
    u_m, _, kv_m, _, _ = _in_proj(meta_tokens, g_mix, w_in_bf, N_META)

    ssm_par = _ssm_params(ssm_a_re[0], ssm_a_im[0], ssm_log_dt[0], ssm_b_re[0], ssm_b_im[0],
                          ssm_c_re[0], ssm_c_im[0], ssm_d[0], SSM_CHUNK)
    y_ssm = _ssm(u, u_m, *ssm_par, batch=bsz, chunk=SSM_CHUNK)

    attn = _attention(attn_sinks[0], q, kv, kv_m, seq // WINDOW)

    h1, hf, route, counts = _mix(
        x2, y_ssm, attn, gs, ga, w_ssm_glu[0].astype(BF16), w_attn_o[0].astype(BF16),
        w_out[0].astype(BF16), ffn_norm_g[0][None, :], router_w[0], router_b[0][None, :], tm_mix)

    tok_disp = min(1024, n)
    be, n_used, next_e, parity, dest_disp, dest_comb, pad_start, pad_len, n_rows = _routing_tables(
        route, counts, n, tok_disp, tm_comb)
    xs = _dispatch(dest_disp, pad_start, pad_len, n_used, hf, n_rows, tok_disp)
    y = _experts(be, n_used, next_e, parity, xs, w_up[0], b_up[0], w_down[0], b_down[0])
    out = _combine(dest_comb, y, h1, route, final_norm_g[None, :], tm_comb)
    return out.reshape(bsz, seq, D_MODEL)
```

```python
import functools
import math

import jax
import jax.numpy as jnp
from jax import lax
from jax.experimental import pallas as pl
from jax.experimental.pallas import tpu as pltpu

F32 = jnp.float32
BF16 = jnp.bfloat16

D_MODEL = 1024
N_META = 16
SSM_WIDTH = 512
SSM_GROUP = 16
SSM_GROUPS = 32
SSM_STATE = 64
HEAD_DIM = 64
N_HEADS = 16
N_KV_HEADS = 2
KV_REP = N_HEADS // N_KV_HEADS
WINDOW = 128
Q_WIDTH = N_HEADS * HEAD_DIM
KV_WIDTH = N_KV_HEADS * HEAD_DIM
N_EXPERTS = 32
TOP_K = 4
D_FF = 1024
SWIGLU_ALPHA = 1.702
SWIGLU_LIMIT = 7.0
RMS_EPS = 1e-5
NEG_INF = -1e30

_U0, _Q0, _KV0, _GS0, _GA0, _IN_END = 0, 512, 1536, 1792, 2816, 3840

SSM_CH_BLOCK = 128
SSM_HALF = (SSM_CH_BLOCK // SSM_GROUP) * SSM_STATE
SSM_CHUNK = 32
EXPERT_ROWS = 256
VMEM_LIMIT = 56 * 1024 * 1024


def _dot(a, b):
    return jnp.dot(a, b, preferred_element_type=F32)


def _dot_nt(a, b):
    return lax.dot_general(a, b, (((1,), (1,)), ((), ())), preferred_element_type=F32)


LANES = 128
TILE_ROWS = D_MODEL // LANES


def _store_token_tiles(ref, x):
    rows = x.shape[0]
    for j in range(TILE_ROWS):
        ref[pl.ds(j, rows, stride=TILE_ROWS), :] = x[:, j * LANES:(j + 1) * LANES]


def _load_token_tiles(ref, start_row, rows):
    return jnp.concatenate(
        [ref[pl.ds(start_row * TILE_ROWS + j, rows, stride=TILE_ROWS), :] for j in range(TILE_ROWS)], axis=1)


def _token_tile(ref, row):
    return ref.at[pl.ds(pl.multiple_of(row * TILE_ROWS, TILE_ROWS), TILE_ROWS), :]


def _in_proj_kernel(x_ref, g_ref, w_ref, u_ref, q_ref, kv_ref, gs_ref, ga_ref):
    x = x_ref[...]
    ms = jnp.mean(x * x, axis=-1, keepdims=True)
    hn = (x * lax.rsqrt(ms + RMS_EPS) * g_ref[...]).astype(BF16)
    u_ref[...] = _dot(hn, w_ref[:, _U0:_Q0])
    q_ref[...] = (_dot(hn, w_ref[:, _Q0:_KV0]) * (HEAD_DIM ** -0.5)).astype(BF16)
    kv_ref[...] = _dot(hn, w_ref[:, _KV0:_GS0]).astype(BF16)
    gs_ref[...] = jax.nn.sigmoid(_dot(hn, w_ref[:, _GS0:_GA0])).astype(BF16)
    ga_ref[...] = jax.nn.sigmoid(_dot(hn, w_ref[:, _GA0:_IN_END])).astype(BF16)


def _in_proj(x2, g, w_bf, tm):
    n = x2.shape[0]
    row = lambda w: pl.BlockSpec((tm, w), lambda i: (i, 0))
    full = lambda a: pl.BlockSpec(a.shape, lambda i: (0,) * a.ndim)
    return pl.pallas_call(
        _in_proj_kernel,
        grid=(n // tm,),
        in_specs=[row(D_MODEL), full(g), full(w_bf)],
        out_specs=[row(SSM_WIDTH), row(Q_WIDTH), row(2 * KV_WIDTH), row(D_MODEL), row(D_MODEL)],
        out_shape=[
            jax.ShapeDtypeStruct((n, SSM_WIDTH), F32),
            jax.ShapeDtypeStruct((n, Q_WIDTH), BF16),
            jax.ShapeDtypeStruct((n, 2 * KV_WIDTH), BF16),
            jax.ShapeDtypeStruct((n, D_MODEL), BF16),
            jax.ShapeDtypeStruct((n, D_MODEL), BF16),
        ],
        compiler_params=pltpu.CompilerParams(
            dimension_semantics=("arbitrary",), vmem_limit_bytes=VMEM_LIMIT),
        name="in_proj",
    )(x2, g, w_bf)


def _ssm_kernel(u_ref, um_ref, bm_ref, cm_ref, ar_ref, ai_ref, atr_ref, ati_ref, d_ref,
                y_ref, sre, sim, *, chunk, rows, batch):
    h = SSM_HALF
    bm = bm_ref[0]
    cm = cm_ref[0]
    ar, ai = ar_ref[0], ai_ref[0]
    atr, ati = atr_ref[0], ati_ref[0]
    dsk = d_ref[0]
    n_chunks = rows // batch

    def advance(sr, si, bu):
        return ar * sr - ai * si + bu[:, :h], ar * si + ai * sr + bu[:, h:]

    bum = _dot(um_ref[...].astype(BF16), bm)
    mr = jnp.zeros((1, h), F32)
    mi = jnp.zeros((1, h), F32)
    for j in range(N_META):
        mr, mi = advance(mr, mi, bum[j:j + 1, :])

    def u_step(t):
        return u_ref[pl.ds(t, rows, stride=chunk), :]

    sre[...] = jnp.zeros_like(sre)
    sim[...] = jnp.zeros_like(sim)

    def pass_a(t, carry):
        bu = _dot(u_step(t).astype(BF16), bm)
        nr, ni = advance(sre[...], sim[...], bu)
        sre[...] = nr
        sim[...] = ni
        return carry

    lax.fori_loop(0, chunk, pass_a, 0)

    def over_chunks(c, carry):
        new = []
        for b in range(batch):
            cr, ci = carry[2 * b], carry[2 * b + 1]
            row = pl.ds(b * n_chunks + c, 1)
            er, ei = sre[row, :], sim[row, :]
            sre[row, :] = cr
            sim[row, :] = ci
            new += [atr * cr - ati * ci + er, atr * ci + ati * cr + ei]
        return tuple(new)

    lax.fori_loop(0, n_chunks, over_chunks, (mr, mi) * batch)

    def pass_b(t, carry):
        ut = u_step(t)
        bu = _dot(ut.astype(BF16), bm)
        nr, ni = advance(sre[...], sim[...], bu)
        sre[...] = nr
        sim[...] = ni
        y = _dot(nr.astype(BF16), cm[:h, :]) + _dot(ni.astype(BF16), cm[h:, :]) + dsk * ut
        y_ref[pl.ds(t, rows, stride=chunk), :] = y
        return carry

    lax.fori_loop(0, chunk, pass_b, 0)


def _ssm(u, u_meta, bmat, cmat, a_re, a_im, at_re, at_im, dskip, batch, chunk):
    n = u.shape[0]
    rows = n // chunk
    nblk = SSM_WIDTH // SSM_CH_BLOCK
    col = lambda r: pl.BlockSpec((r, SSM_CH_BLOCK), lambda j: (0, j))
    par = lambda a: pl.BlockSpec((1,) + a.shape[1:], lambda j: (j, 0, 0))
    return pl.pallas_call(
        functools.partial(_ssm_kernel, chunk=chunk, rows=rows, batch=batch),
        grid=(nblk,),
        in_specs=[col(n), col(N_META), par(bmat), par(cmat), par(a_re), par(a_im),
                  par(at_re), par(at_im), par(dskip)],
        out_specs=col(n),
        out_shape=jax.ShapeDtypeStruct((n, SSM_WIDTH), F32),
        scratch_shapes=[pltpu.VMEM((rows, SSM_HALF), F32), pltpu.VMEM((rows, SSM_HALF), F32)],
        compiler_params=pltpu.CompilerParams(
            dimension_semantics=("arbitrary",), vmem_limit_bytes=VMEM_LIMIT),
        name="ssm",
    )(u, u_meta, bmat, cmat, a_re, a_im, at_re, at_im, dskip)


def _ssm_params(a_re, a_im, log_dt, b_re, b_im, c_re, c_im, d_skip, chunk):
    dt = jnp.exp(log_dt)[:, None]
    mag = jnp.exp(a_re * dt)
    ang = a_im * dt
    abar_re, abar_im = mag * jnp.cos(ang), mag * jnp.sin(ang)
    den = a_re * a_re + a_im * a_im
    nr, ni = abar_re - 1.0, abar_im
    coef_re = ((nr * a_re + ni * a_im) / den)[..., None]
    coef_im = ((ni * a_re - nr * a_im) / den)[..., None]
    bbar_re = coef_re * b_re - coef_im * b_im
    bbar_im = coef_re * b_im + coef_im * b_re
    magt = jnp.exp(a_re * dt * chunk)
    at_re, at_im = magt * jnp.cos(ang * chunk), magt * jnp.sin(ang * chunk)

    nblk = SSM_WIDTH // SSM_CH_BLOCK
    gpb = SSM_GROUPS // nblk
    eye = jnp.eye(gpb, dtype=F32)

    def in_map(b):
        b = b.reshape(nblk, gpb, SSM_STATE, SSM_GROUP)
        return jnp.einsum('jgpc,gh->jgchp', b, eye).reshape(nblk, SSM_CH_BLOCK, gpb * SSM_STATE)

    def out_map(c):
        c = c.reshape(nblk, gpb, SSM_GROUP, SSM_STATE)
        return jnp.einsum('jgcp,gh->jgphc', c, eye).reshape(nblk, gpb * SSM_STATE, SSM_CH_BLOCK)

    bmat = jnp.concatenate([in_map(bbar_re), in_map(bbar_im)], axis=2).astype(BF16)
    cmat = jnp.concatenate([out_map(c_re), -out_map(c_im)], axis=1).astype(BF16)
    vec = lambda v: v.reshape(nblk, 1, SSM_HALF)
    return (bmat, cmat, vec(abar_re), vec(abar_im), vec(at_re), vec(at_im),
            d_skip.reshape(nblk, 1, SSM_CH_BLOCK))


def _attn_kernel(sink_ref, q_ref, kvc_ref, kvp_ref, kvm_ref, o_ref, *, blocks_per_seq):
    n = pl.program_id(0) % blocks_per_seq
    w = WINDOW
    hd = HEAD_DIM
    qi = lax.broadcasted_iota(jnp.int32, (w, w), 0)
    lane = lax.broadcasted_iota(jnp.int32, (w, w), 1)
    vis_prev = (lane > qi) & (n > 0)
    vis_cur = lane <= qi
    left = lane < hd
    meta_l = lane < N_META
    meta_r = (lane >= N_META) & (lane < 2 * N_META)

    def placed(x_bf):
        x = x_bf.astype(F32)
        xr = pltpu.roll(x, hd, 1)
        lm = lax.broadcasted_iota(jnp.int32, x.shape, 1) < hd
        z = jnp.zeros_like(x)
        return {(0, 0): jnp.where(lm, x, z), (0, 1): jnp.where(lm, z, xr),
                (1, 0): jnp.where(lm, xr, z), (1, 1): jnp.where(lm, z, x)}

    kp, kc, km = placed(kvp_ref[:, :KV_WIDTH]), placed(kvc_ref[:, :KV_WIDTH]), placed(kvm_ref[:, :KV_WIDTH])
    vp, vc, vm = placed(kvp_ref[:, KV_WIDTH:]), placed(kvc_ref[:, KV_WIDTH:]), placed(kvm_ref[:, KV_WIDTH:])
    pad_rows = w - 2 * N_META
    zpad = jnp.zeros((pad_rows, w), F32)
    krow = lax.broadcasted_iota(jnp.int32, (5 * w, w), 0)
    klane = lax.broadcasted_iota(jnp.int32, (5 * w, w), 1)
    row_l = (krow < 2 * w) | ((krow >= 4 * w) & (krow < 4 * w + N_META))
    row_r = ((krow >= 2 * w) & (krow < 4 * w)) | ((krow >= 4 * w + N_META) & (krow < 4 * w + 2 * N_META))
    den_cols = jnp.where((row_l & (klane < hd)) | (row_r & (klane >= hd)), 1.0, 0.0)

    for j in range(N_KV_HEADS):
        kcat = jnp.concatenate(
            [kp[j, 0], kc[j, 0], kp[j, 1], kc[j, 1], km[j, 0], km[j, 1], zpad], axis=0).astype(BF16)
        vcat = jnp.concatenate(
            [jnp.concatenate([vp[j, 0], vc[j, 0], vp[j, 1], vc[j, 1], vm[j, 0], vm[j, 1], zpad], axis=0),
             den_cols], axis=1).astype(BF16)
        for r in range(KV_REP // 2):
            pr = j * (KV_REP // 2) + r
            s = _dot_nt(q_ref[:, pr * w:(pr + 1) * w], kcat)
            s_l = jnp.where(vis_cur, s[:, w:2 * w], jnp.where(vis_prev, s[:, 0:w], NEG_INF))
            s_r = jnp.where(vis_cur, s[:, 3 * w:4 * w], jnp.where(vis_prev, s[:, 2 * w:3 * w], NEG_INF))
            s_m = s[:, 4 * w:]
            sink_l, sink_r = sink_ref[2 * pr], sink_ref[2 * pr + 1]
            m_l = jnp.maximum(jnp.max(jnp.maximum(s_l, jnp.where(meta_l, s_m, NEG_INF)),
                                      axis=1, keepdims=True), sink_l)
            m_r = jnp.maximum(jnp.max(jnp.maximum(s_r, jnp.where(meta_r, s_m, NEG_INF)),
                                      axis=1, keepdims=True), sink_r)
            s_m = jnp.where(meta_l, s_m - m_l, jnp.where(meta_r, s_m - m_r, NEG_INF))
            e_l, e_r = jnp.exp(s_l - m_l), jnp.exp(s_r - m_r)
            e = jnp.concatenate([jnp.where(vis_cur, 0.0, e_l), jnp.where(vis_cur, e_l, 0.0),
                                 jnp.where(vis_cur, 0.0, e_r), jnp.where(vis_cur, e_r, 0.0),
                                 jnp.exp(s_m)], axis=1).astype(BF16)
            acc = _dot(e, vcat)
            den = acc[:, w:] + jnp.where(left, jnp.exp(sink_l - m_l), jnp.exp(sink_r - m_r))
            o_ref[:, pr * w:(pr + 1) * w] = (acc[:, :w] / den).astype(BF16)


def _attention(sinks, q, kv, kv_meta, blocks_per_seq):
    n = q.shape[0]
    return pl.pallas_call(
        functools.partial(_attn_kernel, blocks_per_seq=blocks_per_seq),
        grid=(n // WINDOW,),
        in_specs=[
            pl.BlockSpec(memory_space=pltpu.SMEM),
            pl.BlockSpec((WINDOW, Q_WIDTH), lambda g: (g, 0)),
            pl.BlockSpec((WINDOW, 2 * KV_WIDTH), lambda g: (g, 0)),
            pl.BlockSpec((WINDOW, 2 * KV_WIDTH), lambda g: (jnp.maximum(g - 1, 0), 0)),
            pl.BlockSpec((N_META, 2 * KV_WIDTH), lambda g: (0, 0)),
        ],
        out_specs=pl.BlockSpec((WINDOW, Q_WIDTH), lambda g: (g, 0)),
        out_shape=jax.ShapeDtypeStruct((n, Q_WIDTH), BF16),
        compiler_params=pltpu.CompilerParams(dimension_semantics=("arbitrary",)),
        name="attn",
    )(sinks, q, kv, kv, kv_meta)


_ROUTE_IDX, _ROUTE_RANK, _ROUTE_GATE, _ROUTE_ROWS = 0, TOP_K, 2 * TOP_K, 16


def _mix_kernel(x_ref, y_ref, at_ref, gs_ref, ga_ref, wglu_ref, wo_ref, wout_ref, fg_ref, rwh_ref, rwl_ref, rb_ref,
                h1_ref, hf_ref, route_ref, cnt_ref, cnt_scr):
    tm = x_ref.shape[0]

    @pl.when(pl.program_id(0) == 0)
    def _():
        cnt_scr[...] = jnp.zeros_like(cnt_scr)

    glu = _dot(jax.nn.gelu(y_ref[...]).astype(BF16), wglu_ref[...])
    branch_ssm = glu[:, :D_MODEL] * jax.nn.sigmoid(glu[:, D_MODEL:])
    branch_attn = _dot(at_ref[...], wo_ref[...])
    merged = gs_ref[...].astype(F32) * branch_ssm + ga_ref[...].astype(F32) * branch_attn
    h1 = x_ref[...] + _dot(merged.astype(BF16), wout_ref[...])
    h1_ref[...] = h1
    ms = jnp.mean(h1 * h1, axis=-1, keepdims=True)
    hf = h1 * lax.rsqrt(ms + RMS_EPS) * fg_ref[...]
    _store_token_tiles(hf_ref, hf)

    hf_hi = hf.astype(BF16)
    hf_lo = (hf - hf_hi.astype(F32)).astype(BF16)
    logits = (_dot(hf_hi, rwh_ref[...]) + (_dot(hf_hi, rwl_ref[...]) + _dot(hf_lo, rwh_ref[...]))
              + rb_ref[...])
    lt = logits.T[:N_EXPERTS, :]
    erow = lax.broadcasted_iota(jnp.int32, (N_EXPERTS, tm), 0)
    vals, idxs, hots = [], [], []
    rest = lt
    for _ in range(TOP_K):
        m = jnp.max(rest, axis=0, keepdims=True)
        first = jnp.min(jnp.where(rest == m, erow, N_EXPERTS), axis=0, keepdims=True)
        hot = erow == first
        vals.append(m)
        idxs.append(first)
        hots.append(hot)
        rest = jnp.where(hot, -jnp.inf, rest)
    exps = [jnp.exp(v - vals[0]) for v in vals]
    tot = exps[0] + exps[1] + exps[2] + exps[3]

    sel = (hots[0] | hots[1] | hots[2] | hots[3]).astype(F32)
    ti = lax.broadcasted_iota(jnp.int32, (tm, tm), 0)
    tj = lax.broadcasted_iota(jnp.int32, (tm, tm), 1)
    earlier = (ti < tj).astype(BF16)
    rank_e = _dot(sel.astype(BF16), earlier) + cnt_scr[...]
    cnt_scr[...] = cnt_scr[...] + jnp.sum(sel, axis=1, keepdims=True)
    cnt_ref[...] = cnt_scr[...]

    rrow = lax.broadcasted_iota(jnp.int32, (_ROUTE_ROWS, tm), 0)
    route = jnp.zeros((_ROUTE_ROWS, tm), F32)
    for k in range(TOP_K):
        rank_k = jnp.sum(jnp.where(hots[k], rank_e, 0.0), axis=0, keepdims=True)
        route = jnp.where(rrow == _ROUTE_IDX + k, idxs[k].astype(F32), route)
        route = jnp.where(rrow == _ROUTE_RANK + k, rank_k, route)
        route = jnp.where(rrow == _ROUTE_GATE + k, exps[k] / tot, route)
    route_ref[...] = route


def _mix(x2, y, attn, gs, ga, wglu, wo, wout, fg, rw, rb, tm):
    n = x2.shape[0]
    rw = jnp.pad(rw, ((0, 0), (0, LANES - N_EXPERTS)))
    rb = jnp.pad(rb, ((0, 0), (0, LANES - N_EXPERTS)), constant_values=NEG_INF)
    rw_hi = rw.astype(BF16)
    rw_lo = (rw - rw_hi.astype(F32)).astype(BF16)
    row = lambda w: pl.BlockSpec((tm, w), lambda i: (i, 0))
    full = lambda a: pl.BlockSpec(a.shape, lambda i: (0,) * a.ndim)
    return pl.pallas_call(
        _mix_kernel,
        grid=(n // tm,),
        in_specs=[row(D_MODEL), row(SSM_WIDTH), row(Q_WIDTH), row(D_MODEL), row(D_MODEL),
                  full(wglu), full(wo), full(wout), full(fg), full(rw_hi), full(rw_lo), full(rb)],
        out_specs=[row(D_MODEL), pl.BlockSpec((tm * TILE_ROWS, LANES), lambda i: (i, 0)),
                   pl.BlockSpec((_ROUTE_ROWS, tm), lambda i: (0, i)),
                   pl.BlockSpec((N_EXPERTS, 1), lambda i: (0, 0))],
        out_shape=[
            jax.ShapeDtypeStruct((n, D_MODEL), F32),
            jax.ShapeDtypeStruct((n * TILE_ROWS, LANES), F32),
            jax.ShapeDtypeStruct((_ROUTE_ROWS, n), F32),
            jax.ShapeDtypeStruct((N_EXPERTS, 1), F32),
        ],
        scratch_shapes=[pltpu.VMEM((N_EXPERTS, 1), F32)],
        compiler_params=pltpu.CompilerParams(
            dimension_semantics=("arbitrary",), vmem_limit_bytes=VMEM_LIMIT),
        name="mix_router",
    )(x2, y, attn, gs, ga, wglu, wo, wout, fg, rw_hi, rw_lo, rb)


def _tiles_wait_copy(src_hbm, dst, n_tiles, sem):
    rows = n_tiles * TILE_ROWS
    return pltpu.make_async_copy(src_hbm.at[pl.ds(0, rows), :], dst.at[pl.ds(0, rows), :], sem)


_ISSUE_UNROLL = 16


def _dispatch_kernel(dst_ref, pad_start_ref, pad_len_ref, nu_ref, hf_ref, xs_hbm, zero_blk, sem, pad_sem):
    i = pl.program_id(0)
    last = pl.num_programs(0) - 1
    pairs = dst_ref.shape[0]
    n_blocks = xs_hbm.shape[0] // (EXPERT_ROWS * TILE_ROWS)

    def zero_copy(row, rows):
        src = zero_blk.at[pl.ds(0, rows * TILE_ROWS), :]
        dst = xs_hbm.at[pl.ds(pl.multiple_of(row * TILE_ROWS, TILE_ROWS), rows * TILE_ROWS), :]
        return pltpu.make_async_copy(src, dst, pad_sem)

    def for_each_pad(fn):
        def per_expert(e, carry):
            row, left = pad_start_ref[e], pad_len_ref[e]
            size = EXPERT_ROWS // 2
            while size >= 1:
                take = left & size

                @pl.when(take != 0)
                def _(row=row, size=size):
                    fn(zero_copy(row, size))

                row = row + take
                size //= 2
            return carry
        lax.fori_loop(0, N_EXPERTS, per_expert, 0)

        def per_block(b, carry):
            fn(zero_copy(b * EXPERT_ROWS, EXPERT_ROWS))
            return carry
        lax.fori_loop(nu_ref[0], n_blocks, per_block, 0)

    @pl.when(i == 0)
    def _():
        zero_blk[...] = jnp.zeros_like(zero_blk)
        for_each_pad(lambda cp: cp.start())

    tokens = pairs // TOP_K
    for k in range(TOP_K):
        def issue(o, carry, k=k):
            tok0 = pl.multiple_of(o * _ISSUE_UNROLL, _ISSUE_UNROLL)
            dsts = [dst_ref[k * tokens + tok0 + r] for r in range(_ISSUE_UNROLL)]
            for r in range(_ISSUE_UNROLL):
                pltpu.make_async_copy(_token_tile(hf_ref, tok0 + r), _token_tile(xs_hbm, dsts[r]),
                                      sem).start(priority=r % 2)
            return carry

        lax.fori_loop(0, tokens // _ISSUE_UNROLL, issue, 0)
    for _ in range(TOP_K):
        pltpu.make_async_copy(hf_ref, xs_hbm.at[pl.ds(0, hf_ref.shape[0]), :], sem).wait()

    @pl.when(i == last)
    def _():
        for_each_pad(lambda cp: cp.wait())


def _dispatch(dest, pad_start, pad_len, n_used, hf_tiles, n_rows, tokens_per_step):
    n = dest.shape[0] // TOP_K
    pairs = tokens_per_step * TOP_K
    assert n % tokens_per_step == 0 and tokens_per_step % _ISSUE_UNROLL == 0
    smem = lambda: pl.BlockSpec(memory_space=pltpu.SMEM)
    return pl.pallas_call(
        _dispatch_kernel,
        grid=(n // tokens_per_step,),
        in_specs=[pl.BlockSpec((pairs,), lambda i: (i,), memory_space=pltpu.SMEM), smem(), smem(), smem(),
                  pl.BlockSpec((tokens_per_step * TILE_ROWS, LANES), lambda i: (i, 0))],
        out_specs=pl.BlockSpec(memory_space=pl.ANY),
        out_shape=jax.ShapeDtypeStruct((n_rows * TILE_ROWS, LANES), F32),
        scratch_shapes=[pltpu.VMEM((EXPERT_ROWS * TILE_ROWS, LANES), F32), pltpu.SemaphoreType.DMA,
                        pltpu.SemaphoreType.DMA],
        compiler_params=pltpu.CompilerParams(
            dimension_semantics=("arbitrary",), vmem_limit_bytes=VMEM_LIMIT),
        name="dispatch",
    )(dest, pad_start, pad_len, n_used, hf_tiles)


def _expert_kernel(be_ref, nu_ref, nxt_ref, par_ref, xs_ref, wu_hbm, bu_ref, wd_hbm, bd_ref, y_ref,
                   wu_f32, wd_f32, wu_bf, wd_bf, wsem):
    i = pl.program_id(0)

    def weight_copies(expert, s):
        return (pltpu.make_async_copy(wu_hbm.at[expert], wu_f32.at[s], wsem.at[0, s]),
                pltpu.make_async_copy(wd_hbm.at[expert], wd_f32.at[s], wsem.at[1, s]))

    @pl.when(i == 0)
    def _():
        for cp in weight_copies(be_ref[0], par_ref[0]):
            cp.start()

    @pl.when((i == 0) | (be_ref[i] != be_ref[jnp.maximum(i - 1, 0)]))
    def _():
        s = par_ref[i]
        for cp in weight_copies(be_ref[i], s):
            cp.wait()

        @pl.when(nxt_ref[i] >= 0)
        def _():
            for cp in weight_copies(nxt_ref[i], 1 - s):
                cp.start()

        wu_bf[...] = wu_f32[s].astype(BF16)
        wd_bf[...] = wd_f32[s].astype(BF16)

    @pl.when(i < nu_ref[0])
    def _():
        xb = _load_token_tiles(xs_ref, 0, EXPERT_ROWS).astype(BF16)
        up = _dot(xb, wu_bf[...]) + bu_ref[0]
        x_glu = jnp.minimum(up[:, :D_FF], SWIGLU_LIMIT)
        x_lin = jnp.clip(up[:, D_FF:], -SWIGLU_LIMIT, SWIGLU_LIMIT)
        act = x_glu * jax.nn.sigmoid(SWIGLU_ALPHA * x_glu) * (x_lin + 1.0)
        _store_token_tiles(y_ref, _dot(act.astype(BF16), wd_bf[...]) + bd_ref[0])

    @pl.when(i >= nu_ref[0])
    def _():
        y_ref[...] = jnp.zeros_like(y_ref)


def _experts(block_expert, n_used, next_expert, parity, xs_tiles, w_up, b_up, w_down, b_down):
    n_blocks = block_expert.shape[0]
    blk = (EXPERT_ROWS * TILE_ROWS, LANES)
    grid_spec = pltpu.PrefetchScalarGridSpec(
        num_scalar_prefetch=4,
        grid=(n_blocks,),
        in_specs=[
            pl.BlockSpec(blk, lambda i, be, nu, nx, pa: (jnp.minimum(i, nu[0] - 1), 0)),
            pl.BlockSpec(memory_space=pl.ANY),
            pl.BlockSpec((1, 1, 2 * D_FF), lambda i, be, nu, nx, pa: (be[i], 0, 0)),
            pl.BlockSpec(memory_space=pl.ANY),
            pl.BlockSpec((1, 1, D_MODEL), lambda i, be, nu, nx, pa: (be[i], 0, 0)),
        ],
        out_specs=pl.BlockSpec(blk, lambda i, be, nu, nx, pa: (i, 0)),
        scratch_shapes=[
            pltpu.VMEM((2, D_MODEL, 2 * D_FF), F32),
            pltpu.VMEM((2, D_FF, D_MODEL), F32),
            pltpu.VMEM((D_MODEL, 2 * D_FF), BF16),
            pltpu.VMEM((D_FF, D_MODEL), BF16),
            pltpu.SemaphoreType.DMA((2, 2)),
        ],
    )
    return pl.pallas_call(
        _expert_kernel,
        grid_spec=grid_spec,
        out_shape=jax.ShapeDtypeStruct((n_blocks * blk[0], LANES), F32),
        compiler_params=pltpu.CompilerParams(
            dimension_semantics=("arbitrary",), vmem_limit_bytes=VMEM_LIMIT),
        name="experts",
    )(block_expert, n_used, next_expert, parity, xs_tiles, w_up, b_up[:, None, :], w_down, b_down[:, None, :])


def _combine_kernel(dst_cur_ref, dst_nxt_ref, y_hbm, h1_ref, route_ref, g_ref, o_ref, gbuf_even, gbuf_odd, sem):
    i = pl.program_id(0)
    last = pl.num_programs(0) - 1
    tm = h1_ref.shape[0]
    rows = TOP_K * tm

    def gather_group(idx_ref, buf, s, row0):
        srcs = [idx_ref[row0 + r] for r in range(_ISSUE_UNROLL)]
        for r in range(_ISSUE_UNROLL):
            pltpu.make_async_copy(_token_tile(y_hbm, srcs[r]), _token_tile(buf, row0 + r),
                                  sem.at[s]).start(priority=r % 2)

    @pl.when(i == 0)
    def _():
        def body(o, carry):
            gather_group(dst_cur_ref, gbuf_even, 0, pl.multiple_of(o * _ISSUE_UNROLL, _ISSUE_UNROLL))
            return carry
        lax.fori_loop(0, rows // _ISSUE_UNROLL, body, 0)

    def step(cur, nxt, s):
        _tiles_wait_copy(y_hbm, cur, rows, sem.at[s]).wait()
        for g in range(rows // _ISSUE_UNROLL):
            gather_group(dst_nxt_ref, nxt, 1 - s, g * _ISSUE_UNROLL)
        assert tm == LANES
        rt = jnp.concatenate([route_ref[...], jnp.zeros((LANES - _ROUTE_ROWS, tm), F32)], axis=0).T
        acc = h1_ref[...]
        for k in range(TOP_K):
            gate = rt[:, _ROUTE_GATE + k:_ROUTE_GATE + k + 1]
            acc = acc + gate * _load_token_tiles(cur, k * tm, tm)
        ms = jnp.mean(acc * acc, axis=-1, keepdims=True)
        o_ref[...] = acc * lax.rsqrt(ms + RMS_EPS) * g_ref[...]

        @pl.when(i == last)
        def _():
            _tiles_wait_copy(y_hbm, nxt, rows, sem.at[1 - s]).wait()

    @pl.when(i % 2 == 0)
    def _():
        step(gbuf_even, gbuf_odd, 0)

    @pl.when(i % 2 == 1)
    def _():
        step(gbuf_odd, gbuf_even, 1)


def _combine(dest_kmajor, y, h1, route, g, tm):
    n = h1.shape[0]
    n_tiles = n // tm
    return pl.pallas_call(
        _combine_kernel,
        grid=(n_tiles,),
        in_specs=[
            pl.BlockSpec((TOP_K * tm,), lambda i: (i,), memory_space=pltpu.SMEM),
            pl.BlockSpec((TOP_K * tm,), lambda i: (jnp.minimum(i + 1, n_tiles - 1),), memory_space=pltpu.SMEM),
            pl.BlockSpec(memory_space=pl.ANY),
            pl.BlockSpec((tm, D_MODEL), lambda i: (i, 0)),
            pl.BlockSpec((_ROUTE_ROWS, tm), lambda i: (0, i)),
            pl.BlockSpec((1, D_MODEL), lambda i: (0, 0)),
        ],
        out_specs=pl.BlockSpec((tm, D_MODEL), lambda i: (i, 0)),
        out_shape=jax.ShapeDtypeStruct((n, D_MODEL), F32),
        scratch_shapes=[pltpu.VMEM((TOP_K * tm * TILE_ROWS, LANES), F32),
                        pltpu.VMEM((TOP_K * tm * TILE_ROWS, LANES), F32),
                        pltpu.SemaphoreType.DMA((2,))],
        compiler_params=pltpu.CompilerParams(
            dimension_semantics=("arbitrary",), vmem_limit_bytes=VMEM_LIMIT),
        name="combine",
    )(dest_kmajor, dest_kmajor, y, h1, route, g)


def _routing_tables(route, counts, n, tokens_dispatch, tokens_combine):
    tm = EXPERT_ROWS
    i32 = jnp.int32
    n_blocks = (n * TOP_K + N_EXPERTS * (tm - 1)) // tm
    idx = route[_ROUTE_IDX:_ROUTE_IDX + TOP_K].astype(i32)
    rank = route[_ROUTE_RANK:_ROUTE_RANK + TOP_K].astype(i32)
    cnt = counts[:, 0].astype(i32)
    eid = jnp.arange(N_EXPERTS, dtype=i32)
    upto = eid[None, :] <= eid[:, None]
    blocks_e = (cnt + tm - 1) // tm
    blocks_end = jnp.sum(jnp.where(upto, blocks_e[None, :], 0), axis=1)
    row_start = (blocks_end - blocks_e) * tm
    n_used = blocks_end[N_EXPERTS - 1]
    used = blocks_e > 0

    def lookup(table, keys):
        hit = keys[None] == eid.reshape((N_EXPERTS,) + (1,) * keys.ndim)
        return jnp.sum(jnp.where(hit, table.reshape((N_EXPERTS,) + (1,) * keys.ndim), 0), axis=0)

    dest = lookup(row_start, idx) + rank
    blk = jnp.arange(n_blocks, dtype=i32)
    last_used = jnp.max(jnp.where(used, eid, 0))
    be = jnp.where(blk < n_used, jnp.sum((blocks_end[None, :] <= blk[:, None]).astype(i32), axis=1), last_used)
    later_used = used[None, :] & (eid[None, :] > eid[:, None])
    after = jnp.min(jnp.where(later_used, eid[None, :], N_EXPERTS), axis=1)
    next_e = jnp.where(after < N_EXPERTS, after, -1)
    parity_e = (jnp.sum((upto & used[None, :]).astype(i32), axis=1) - 1) % 2

    def k_major(tokens):
        return dest.reshape(TOP_K, n // tokens, tokens).transpose(1, 0, 2).reshape(-1)

    return (be.astype(i32), n_used.reshape(1), lookup(next_e, be).astype(i32), lookup(parity_e, be).astype(i32),
            k_major(tokens_dispatch), k_major(tokens_combine), row_start + cnt, blocks_e * tm - cnt,
            n_blocks * tm)


def kernel(x, meta_tokens, mix_norm_g, w_in, ssm_a_re, ssm_a_im, ssm_log_dt, ssm_b_re, ssm_b_im,
           ssm_c_re, ssm_c_im, ssm_d, w_ssm_glu, attn_sinks, w_attn_o, w_out, ffn_norm_g,
           router_w, router_b, w_up, b_up, w_down, b_down, final_norm_g):
    bsz, seq, d = x.shape
    assert d == D_MODEL and seq % max(WINDOW, SSM_CHUNK) == 0
    assert mix_norm_g.shape[0] == 1, "single-layer trunk"
    n = bsz * seq
    tm_proj = min(512, n)
    tm_mix = min(256, n)
    tm_comb = min(128, n)
    x2 = x.reshape(n, D_MODEL)

    w_in_bf = w_in[0].astype(BF16)
    g_mix = mix_norm_g[0][None, :]
    u, q, kv, gs, ga = _in_proj(x2, g_mix, w_in_bf, tm_proj)
    u_m, _, kv_m, _, _ = _in_proj(meta_tokens, g_mix, w_in_bf, N_META)

    ssm_par = _ssm_params(ssm_a_re[0], ssm_a_im[0], ssm_log_dt[0], ssm_b_re[0], ssm_b_im[0],
                          ssm_c_re[0], ssm_c_im[0], ssm_d[0], SSM_CHUNK)
    y_ssm = _ssm(u, u_m, *ssm_par, batch=bsz, chunk=SSM_CHUNK)

    attn = _attention(attn_sinks[0], q, kv, kv_m, seq // WINDOW)

    h1, hf, route, counts = _mix(
        x2, y_ssm, attn, gs, ga, w_ssm_glu[0].astype(BF16), w_attn_o[0].astype(BF16),
        w_out[0].astype(BF16), ffn_norm_g[0][None, :], router_w[0], router_b[0][None, :], tm_mix)

    tok_disp = min(1024, n)
    be, n_used, next_e, parity, dest_disp, dest_comb, pad_start, pad_len, n_rows = _routing_tables(
        route, counts, n, tok_disp, tm_comb)
    xs = _dispatch(dest_disp, pad_start, pad_len, n_used, hf, n_rows, tok_disp)
    y = _experts(be, n_used, next_e, parity, xs, w_up[0], b_up[0], w_down[0], b_down[0])
    out = _combine(dest_comb, y, h1, route, final_norm_g[None, :], tm_comb)
    return out.reshape(bsz, seq, D_MODEL)
```

```python
import functools
import math

import jax
import jax.numpy as jnp
from jax import lax
from jax.experimental import pallas as pl
from jax.experimental.pallas import tpu as pltpu

F32 = jnp.float32
BF16 = jnp.bfloat16

D_MODEL = 1024
N_META = 16
SSM_WIDTH = 512
SSM_GROUP = 16
SSM_GROUPS = 32
SSM_STATE = 64
HEAD_DIM = 64
N_HEADS = 16
N_KV_HEADS = 2
KV_REP = N_HEADS // N_KV_HEADS
WINDOW = 128
Q_WIDTH = N_HEADS * HEAD_DIM
KV_WIDTH = N_KV_HEADS * HEAD_DIM
N_EXPERTS = 32
TOP_K = 4
D_FF = 1024
SWIGLU_ALPHA = 1.702
SWIGLU_LIMIT = 7.0
RMS_EPS = 1e-5
NEG_INF = -1e30

_U0, _Q0, _KV0, _GS0, _GA0, _IN_END = 0, 512, 1536, 1792, 2816, 3840

SSM_CH_BLOCK = 128
SSM_HALF = (SSM_CH_BLOCK // SSM_GROUP) * SSM_STATE
SSM_CHUNK = 32
EXPERT_ROWS = 512
VMEM_LIMIT = 56 * 1024 * 1024


def _dot(a, b):
    return jnp.dot(a, b, preferred_element_type=F32)


def _dot_nt(a, b):
    return lax.dot_general(a, b, (((1,), (1,)), ((), ())), preferred_element_type=F32)


LANES = 128
TILE_ROWS = D_MODEL // LANES


def _store_token_tiles(ref, x):
    rows = x.shape[0]
    for j in range(TILE_ROWS):
        ref[pl.ds(j, rows, stride=TILE_ROWS), :] = x[:, j * LANES:(j + 1) * LANES]


def _load_token_tiles(ref, start_row, rows):
    return jnp.concatenate(
        [ref[pl.ds(start_row * TILE_ROWS + j, rows, stride=TILE_ROWS), :] for j in range(TILE_ROWS)], axis=1)


def _token_tile(ref, row):
    return ref.at[pl.ds(pl.multiple_of(row * TILE_ROWS, TILE_ROWS), TILE_ROWS), :]


def _in_proj_kernel(x_ref, g_ref, w_ref, u_ref, q_ref, kv_ref, gs_ref, ga_ref):
    x = x_ref[...]
    ms = jnp.mean(x * x, axis=-1, keepdims=True)
    hn = (x * lax.rsqrt(ms + RMS_EPS) * g_ref[...]).astype(BF16)
    u_ref[...] = _dot(hn, w_ref[:, _U0:_Q0])
    q_ref[...] = (_dot(hn, w_ref[:, _Q0:_KV0]) * (HEAD_DIM ** -0.5)).astype(BF16)
    kv_ref[...] = _dot(hn, w_ref[:, _KV0:_GS0]).astype(BF16)
    gs_ref[...] = jax.nn.sigmoid(_dot(hn, w_ref[:, _GS0:_GA0])).astype(BF16)
    ga_ref[...] = jax.nn.sigmoid(_dot(hn, w_ref[:, _GA0:_IN_END])).astype(BF16)


def _in_proj(x2, g, w_bf, tm):
    n = x2.shape[0]
    row = lambda w: pl.BlockSpec((tm, w), lambda i: (i, 0))
    full = lambda a: pl.BlockSpec(a.shape, lambda i: (0,) * a.ndim)
    return pl.pallas_call(
        _in_proj_kernel,
        grid=(n // tm,),
        in_specs=[row(D_MODEL), full(g), full(w_bf)],
        out_specs=[row(SSM_WIDTH), row(Q_WIDTH), row(2 * KV_WIDTH), row(D_MODEL), row(D_MODEL)],
        out_shape=[
            jax.ShapeDtypeStruct((n, SSM_WIDTH), F32),
            jax.ShapeDtypeStruct((n, Q_WIDTH), BF16),
            jax.ShapeDtypeStruct((n, 2 * KV_WIDTH), BF16),
            jax.ShapeDtypeStruct((n, D_MODEL), BF16),
            jax.ShapeDtypeStruct((n, D_MODEL), BF16),
        ],
        compiler_params=pltpu.CompilerParams(
            dimension_semantics=("arbitrary",), vmem_limit_bytes=VMEM_LIMIT),
        name="in_proj",
    )(x2, g, w_bf)


def _ssm_kernel(u_ref, um_ref, bm_ref, cm_ref, ar_ref, ai_ref, atr_ref, ati_ref, d_ref,
                y_ref, sre, sim, *, chunk, rows, batch):
    h = SSM_HALF
    bm = bm_ref[0]
    cm = cm_ref[0]
    ar, ai = ar_ref[0], ai_ref[0]
    atr, ati = atr_ref[0], ati_ref[0]
    dsk = d_ref[0]
    n_chunks = rows // batch

    def advance(sr, si, bu):
        return ar * sr - ai * si + bu[:, :h], ar * si + ai * sr + bu[:, h:]

    bum = _dot(um_ref[...].astype(BF16), bm)
    mr = jnp.zeros((1, h), F32)
    mi = jnp.zeros((1, h), F32)
    for j in range(N_META):
        mr, mi = advance(mr, mi, bum[j:j + 1, :])

    def u_step(t):
        return u_ref[pl.ds(t, rows, stride=chunk), :]

    sre[...] = jnp.zeros_like(sre)
    sim[...] = jnp.zeros_like(sim)

    def pass_a(t, carry):
        bu = _dot(u_step(t).astype(BF16), bm)
        nr, ni = advance(sre[...], sim[...], bu)
        sre[...] = nr
        sim[...] = ni
        return carry

    lax.fori_loop(0, chunk, pass_a, 0)

    def over_chunks(c, carry):
        new = []
        for b in range(batch):
            cr, ci = carry[2 * b], carry[2 * b + 1]
            row = pl.ds(b * n_chunks + c, 1)
            er, ei = sre[row, :], sim[row, :]
            sre[row, :] = cr
            sim[row, :] = ci
            new += [atr * cr - ati * ci + er, atr * ci + ati * cr + ei]
        return tuple(new)

    lax.fori_loop(0, n_chunks, over_chunks, (mr, mi) * batch)

    def pass_b(t, carry):
        ut = u_step(t)
        bu = _dot(ut.astype(BF16), bm)
        nr, ni = advance(sre[...], sim[...], bu)
        sre[...] = nr
        sim[...] = ni
        y = _dot(nr.astype(BF16), cm[:h, :]) + _dot(ni.astype(BF16), cm[h:, :]) + dsk * ut
        y_ref[pl.ds(t, rows, stride=chunk), :] = y
        return carry

    lax.fori_loop(0, chunk, pass_b, 0)


def _ssm(u, u_meta, bmat, cmat, a_re, a_im, at_re, at_im, dskip, batch, chunk):
    n = u.shape[0]
    rows = n // chunk
    nblk = SSM_WIDTH // SSM_CH_BLOCK
    col = lambda r: pl.BlockSpec((r, SSM_CH_BLOCK), lambda j: (0, j))
    par = lambda a: pl.BlockSpec((1,) + a.shape[1:], lambda j: (j, 0, 0))
    return pl.pallas_call(
        functools.partial(_ssm_kernel, chunk=chunk, rows=rows, batch=batch),
        grid=(nblk,),
        in_specs=[col(n), col(N_META), par(bmat), par(cmat), par(a_re), par(a_im),
                  par(at_re), par(at_im), par(dskip)],
        out_specs=col(n),
        out_shape=jax.ShapeDtypeStruct((n, SSM_WIDTH), F32),
        scratch_shapes=[pltpu.VMEM((rows, SSM_HALF), F32), pltpu.VMEM((rows, SSM_HALF), F32)],
        compiler_params=pltpu.CompilerParams(
            dimension_semantics=("arbitrary",), vmem_limit_bytes=VMEM_LIMIT),
        name="ssm",
    )(u, u_meta, bmat, cmat, a_re, a_im, at_re, at_im, dskip)


def _ssm_params(a_re, a_im, log_dt, b_re, b_im, c_re, c_im, d_skip, chunk):
    dt = jnp.exp(log_dt)[:, None]
    mag = jnp.exp(a_re * dt)
    ang = a_im * dt
    abar_re, abar_im = mag * jnp.cos(ang), mag * jnp.sin(ang)
    den = a_re * a_re + a_im * a_im
    nr, ni = abar_re - 1.0, abar_im
    coef_re = ((nr * a_re + ni * a_im) / den)[..., None]
    coef_im = ((ni * a_re - nr * a_im) / den)[..., None]
    bbar_re = coef_re * b_re - coef_im * b_im
    bbar_im = coef_re * b_im + coef_im * b_re
    magt = jnp.exp(a_re * dt * chunk)
    at_re, at_im = magt * jnp.cos(ang * chunk), magt * jnp.sin(ang * chunk)

    nblk = SSM_WIDTH // SSM_CH_BLOCK
    gpb = SSM_GROUPS // nblk
    eye = jnp.eye(gpb, dtype=F32)

    def in_map(b):
        b = b.reshape(nblk, gpb, SSM_STATE, SSM_GROUP)
        return jnp.einsum('jgpc,gh->jgchp', b, eye).reshape(nblk, SSM_CH_BLOCK, gpb * SSM_STATE)

    def out_map(c):
        c = c.reshape(nblk, gpb, SSM_GROUP, SSM_STATE)
        return jnp.einsum('jgcp,gh->jgphc', c, eye).reshape(nblk, gpb * SSM_STATE, SSM_CH_BLOCK)

    bmat = jnp.concatenate([in_map(bbar_re), in_map(bbar_im)], axis=2).astype(BF16)
    cmat = jnp.concatenate([out_map(c_re), -out_map(c_im)], axis=1).astype(BF16)
    vec = lambda v: v.reshape(nblk, 1, SSM_HALF)
    return (bmat, cmat, vec(abar_re), vec(abar_im), vec(at_re), vec(at_im),
            d_skip.reshape(nblk, 1, SSM_CH_BLOCK))


def _attn_kernel(sink_ref, q_ref, kvc_ref, kvp_ref, kvm_ref, o_ref, *, blocks_per_seq):
    n = pl.program_id(0) % blocks_per_seq
    w = WINDOW
    hd = HEAD_DIM
    qi = lax.broadcasted_iota(jnp.int32, (w, w), 0)
    lane = lax.broadcasted_iota(jnp.int32, (w, w), 1)
    vis_prev = (lane > qi) & (n > 0)
    vis_cur = lane <= qi
    left = lane < hd
    meta_l = lane < N_META
    meta_r = (lane >= N_META) & (lane < 2 * N_META)

    def placed(x_bf):
        x = x_bf.astype(F32)
        xr = pltpu.roll(x, hd, 1)
        lm = lax.broadcasted_iota(jnp.int32, x.shape, 1) < hd
        z = jnp.zeros_like(x)
        return {(0, 0): jnp.where(lm, x, z), (0, 1): jnp.where(lm, z, xr),
                (1, 0): jnp.where(lm, xr, z), (1, 1): jnp.where(lm, z, x)}

    kp, kc, km = placed(kvp_ref[:, :KV_WIDTH]), placed(kvc_ref[:, :KV_WIDTH]), placed(kvm_ref[:, :KV_WIDTH])
    vp, vc, vm = placed(kvp_ref[:, KV_WIDTH:]), placed(kvc_ref[:, KV_WIDTH:]), placed(kvm_ref[:, KV_WIDTH:])
    pad_rows = w - 2 * N_META
    zpad = jnp.zeros((pad_rows, w), F32)
    krow = lax.broadcasted_iota(jnp.int32, (5 * w, w), 0)
    klane = lax.broadcasted_iota(jnp.int32, (5 * w, w), 1)
    row_l = (krow < 2 * w) | ((krow >= 4 * w) & (krow < 4 * w + N_META))
    row_r = ((krow >= 2 * w) & (krow < 4 * w)) | ((krow >= 4 * w + N_META) & (krow < 4 * w + 2 * N_META))
    den_cols = jnp.where((row_l & (klane < hd)) | (row_r & (klane >= hd)), 1.0, 0.0)

    for j in range(N_KV_HEADS):
        kcat = jnp.concatenate(
            [kp[j, 0], kc[j, 0], kp[j, 1], kc[j, 1], km[j, 0], km[j, 1], zpad], axis=0).astype(BF16)
        vcat = jnp.concatenate(
            [jnp.concatenate([vp[j, 0], vc[j, 0], vp[j, 1], vc[j, 1], vm[j, 0], vm[j, 1], zpad], axis=0),
             den_cols], axis=1).astype(BF16)
        for r in range(KV_REP // 2):
            pr = j * (KV_REP // 2) + r
            s = _dot_nt(q_ref[:, pr * w:(pr + 1) * w], kcat)
            s_l = jnp.where(vis_cur, s[:, w:2 * w], jnp.where(vis_prev, s[:, 0:w], NEG_INF))
            s_r = jnp.where(vis_cur, s[:, 3 * w:4 * w], jnp.where(vis_prev, s[:, 2 * w:3 * w], NEG_INF))
            s_m = s[:, 4 * w:]
            sink_l, sink_r = sink_ref[2 * pr], sink_ref[2 * pr + 1]
            m_l = jnp.maximum(jnp.max(jnp.maximum(s_l, jnp.where(meta_l, s_m, NEG_INF)),
                                      axis=1, keepdims=True), sink_l)
            m_r = jnp.maximum(jnp.max(jnp.maximum(s_r, jnp.where(meta_r, s_m, NEG_INF)),
                                      axis=1, keepdims=True), sink_r)
            s_m = jnp.where(meta_l, s_m - m_l, jnp.where(meta_r, s_m - m_r, NEG_INF))
            e_l, e_r = jnp.exp(s_l - m_l), jnp.exp(s_r - m_r)
            e = jnp.concatenate([jnp.where(vis_cur, 0.0, e_l), jnp.where(vis_cur, e_l, 0.0),
                                 jnp.where(vis_cur, 0.0, e_r), jnp.where(vis_cur, e_r, 0.0),
                                 jnp.exp(s_m)], axis=1).astype(BF16)
            acc = _dot(e, vcat)
            den = acc[:, w:] + jnp.where(left, jnp.exp(sink_l - m_l), jnp.exp(sink_r - m_r))
            o_ref[:, pr * w:(pr + 1) * w] = (acc[:, :w] / den).astype(BF16)


def _attention(sinks, q, kv, kv_meta, blocks_per_seq):
    n = q.shape[0]
    return pl.pallas_call(
        functools.partial(_attn_kernel, blocks_per_seq=blocks_per_seq),
        grid=(n // WINDOW,),
        in_specs=[
            pl.BlockSpec(memory_space=pltpu.SMEM),
            pl.BlockSpec((WINDOW, Q_WIDTH), lambda g: (g, 0)),
            pl.BlockSpec((WINDOW, 2 * KV_WIDTH), lambda g: (g, 0)),
            pl.BlockSpec((WINDOW, 2 * KV_WIDTH), lambda g: (jnp.maximum(g - 1, 0), 0)),
            pl.BlockSpec((N_META, 2 * KV_WIDTH), lambda g: (0, 0)),
        ],
        out_specs=pl.BlockSpec((WINDOW, Q_WIDTH), lambda g: (g, 0)),
        out_shape=jax.ShapeDtypeStruct((n, Q_WIDTH), BF16),
        compiler_params=pltpu.CompilerParams(dimension_semantics=("arbitrary",)),
        name="attn",
    )(sinks, q, kv, kv, kv_meta)


_ROUTE_IDX, _ROUTE_RANK, _ROUTE_GATE, _ROUTE_ROWS = 0, TOP_K, 2 * TOP_K, 16


def _mix_kernel(x_ref, y_ref, at_ref, gs_ref, ga_ref, wglu_ref, wo_ref, wout_ref, fg_ref, rwh_ref, rwl_ref, rb_ref,
                h1_ref, hf_ref, route_ref, cnt_ref, cnt_scr):
    tm = x_ref.shape[0]

    @pl.when(pl.program_id(0) == 0)
    def _():
        cnt_scr[...] = jnp.zeros_like(cnt_scr)

    glu = _dot(jax.nn.gelu(y_ref[...]).astype(BF16), wglu_ref[...])
    branch_ssm = glu[:, :D_MODEL] * jax.nn.sigmoid(glu[:, D_MODEL:])
    branch_attn = _dot(at_ref[...], wo_ref[...])
    merged = gs_ref[...].astype(F32) * branch_ssm + ga_ref[...].astype(F32) * branch_attn
    h1 = x_ref[...] + _dot(merged.astype(BF16), wout_ref[...])
    h1_ref[...] = h1
    ms = jnp.mean(h1 * h1, axis=-1, keepdims=True)
    hf = h1 * lax.rsqrt(ms + RMS_EPS) * fg_ref[...]
    _store_token_tiles(hf_ref, hf)

    hf_hi = hf.astype(BF16)
    hf_lo = (hf - hf_hi.astype(F32)).astype(BF16)
    logits = (_dot(hf_hi, rwh_ref[...]) + (_dot(hf_hi, rwl_ref[...]) + _dot(hf_lo, rwh_ref[...]))
              + rb_ref[...])
    lt = logits.T[:N_EXPERTS, :]
    erow = lax.broadcasted_iota(jnp.int32, (N_EXPERTS, tm), 0)
    vals, idxs, hots = [], [], []
    rest = lt
    for _ in range(TOP_K):
        m = jnp.max(rest, axis=0, keepdims=True)
        first = jnp.min(jnp.where(rest == m, erow, N_EXPERTS), axis=0, keepdims=True)
        hot = erow == first
        vals.append(m)
        idxs.append(first)
        hots.append(hot)
        rest = jnp.where(hot, -jnp.inf, rest)
    exps = [jnp.exp(v - vals[0]) for v in vals]
    tot = exps[0] + exps[1] + exps[2] + exps[3]

    sel = (hots[0] | hots[1] | hots[2] | hots[3]).astype(F32)
    ti = lax.broadcasted_iota(jnp.int32, (tm, tm), 0)
    tj = lax.broadcasted_iota(jnp.int32, (tm, tm), 1)
    earlier = (ti < tj).astype(BF16)
    rank_e = _dot(sel.astype(BF16), earlier) + cnt_scr[...]
    cnt_scr[...] = cnt_scr[...] + jnp.sum(sel, axis=1, keepdims=True)
    cnt_ref[...] = cnt_scr[...]

    rrow = lax.broadcasted_iota(jnp.int32, (_ROUTE_ROWS, tm), 0)
    route = jnp.zeros((_ROUTE_ROWS, tm), F32)
    for k in range(TOP_K):
        rank_k = jnp.sum(jnp.where(hots[k], rank_e, 0.0), axis=0, keepdims=True)
        route = jnp.where(rrow == _ROUTE_IDX + k, idxs[k].astype(F32), route)
        route = jnp.where(rrow == _ROUTE_RANK + k, rank_k, route)
        route = jnp.where(rrow == _ROUTE_GATE + k, exps[k] / tot, route)
    route_ref[...] = route


def _mix(x2, y, attn, gs, ga, wglu, wo, wout, fg, rw, rb, tm):
    n = x2.shape[0]
    rw = jnp.pad(rw, ((0, 0), (0, LANES - N_EXPERTS)))
    rb = jnp.pad(rb, ((0, 0), (0, LANES - N_EXPERTS)), constant_values=NEG_INF)
    rw_hi = rw.astype(BF16)
    rw_lo = (rw - rw_hi.astype(F32)).astype(BF16)
    row = lambda w: pl.BlockSpec((tm, w), lambda i: (i, 0))
    full = lambda a: pl.BlockSpec(a.shape, lambda i: (0,) * a.ndim)
    return pl.pallas_call(
        _mix_kernel,
        grid=(n // tm,),
        in_specs=[row(D_MODEL), row(SSM_WIDTH), row(Q_WIDTH), row(D_MODEL), row(D_MODEL),
                  full(wglu), full(wo), full(wout), full(fg), full(rw_hi), full(rw_lo), full(rb)],
        out_specs=[row(D_MODEL), pl.BlockSpec((tm * TILE_ROWS, LANES), lambda i: (i, 0)),
                   pl.BlockSpec((_ROUTE_ROWS, tm), lambda i: (0, i)),
                   pl.BlockSpec((N_EXPERTS, 1), lambda i: (0, 0))],
        out_shape=[
            jax.ShapeDtypeStruct((n, D_MODEL), F32),
            jax.ShapeDtypeStruct((n * TILE_ROWS, LANES), F32),
            jax.ShapeDtypeStruct((_ROUTE_ROWS, n), F32),
            jax.ShapeDtypeStruct((N_EXPERTS, 1), F32),
        ],
        scratch_shapes=[pltpu.VMEM((N_EXPERTS, 1), F32)],
        compiler_params=pltpu.CompilerParams(
            dimension_semantics=("arbitrary",), vmem_limit_bytes=VMEM_LIMIT),
        name="mix_router",
    )(x2, y, attn, gs, ga, wglu, wo, wout, fg, rw_hi, rw_lo, rb)


def _tiles_wait_copy(src_hbm, dst, n_tiles, sem):
    rows = n_tiles * TILE_ROWS
    return pltpu.make_async_copy(src_hbm.at[pl.ds(0, rows), :], dst.at[pl.ds(0, rows), :], sem)


_ISSUE_UNROLL = 16


def _dispatch_kernel(dst_ref, pad_start_ref, pad_len_ref, nu_ref, hf_ref, xs_hbm, zero_blk, sem, pad_sem):
    i = pl.program_id(0)
    last = pl.num_programs(0) - 1
    pairs = dst_ref.shape[0]
    n_blocks = xs_hbm.shape[0] // (EXPERT_ROWS * TILE_ROWS)

    def zero_copy(row, rows):
        src = zero_blk.at[pl.ds(0, rows * TILE_ROWS), :]
        dst = xs_hbm.at[pl.ds(pl.multiple_of(row * TILE_ROWS, TILE_ROWS), rows * TILE_ROWS), :]
        return pltpu.make_async_copy(src, dst, pad_sem)

    def for_each_pad(fn):
        def per_expert(e, carry):
            row, left = pad_start_ref[e], pad_len_ref[e]
            size = EXPERT_ROWS // 2
            while size >= 1:
                take = left & size

                @pl.when(take != 0)
                def _(row=row, size=size):
                    fn(zero_copy(row, size))

                row = row + take
                size //= 2
            return carry
        lax.fori_loop(0, N_EXPERTS, per_expert, 0)

        def per_block(b, carry):
            fn(zero_copy(b * EXPERT_ROWS, EXPERT_ROWS))
            return carry
        lax.fori_loop(nu_ref[0], n_blocks, per_block, 0)

    @pl.when(i == 0)
    def _():
        zero_blk[...] = jnp.zeros_like(zero_blk)
        for_each_pad(lambda cp: cp.start())

    tokens = pairs // TOP_K
    for k in range(TOP_K):
        def issue(o, carry, k=k):
            tok0 = pl.multiple_of(o * _ISSUE_UNROLL, _ISSUE_UNROLL)
            dsts = [dst_ref[k * tokens + tok0 + r] for r in range(_ISSUE_UNROLL)]
            for r in range(_ISSUE_UNROLL):
                pltpu.make_async_copy(_token_tile(hf_ref, tok0 + r), _token_tile(xs_hbm, dsts[r]),
                                      sem).start(priority=r % 2)
            return carry

        lax.fori_loop(0, tokens // _ISSUE_UNROLL, issue, 0)
    for _ in range(TOP_K):
        pltpu.make_async_copy(hf_ref, xs_hbm.at[pl.ds(0, hf_ref.shape[0]), :], sem).wait()

    @pl.when(i == last)
    def _():
        for_each_pad(lambda cp: cp.wait())


def _dispatch(dest, pad_start, pad_len, n_used, hf_tiles, n_rows, tokens_per_step):
    n = dest.shape[0] // TOP_K
    pairs = tokens_per_step * TOP_K
    assert n % tokens_per_step == 0 and tokens_per_step % _ISSUE_UNROLL == 0
    smem = lambda: pl.BlockSpec(memory_space=pltpu.SMEM)
    return pl.pallas_call(
        _dispatch_kernel,
        grid=(n // tokens_per_step,),
        in_specs=[pl.BlockSpec((pairs,), lambda i: (i,), memory_space=pltpu.SMEM), smem(), smem(), smem(),
                  pl.BlockSpec((tokens_per_step * TILE_ROWS, LANES), lambda i: (i, 0))],
        out_specs=pl.BlockSpec(memory_space=pl.ANY),
        out_shape=jax.ShapeDtypeStruct((n_rows * TILE_ROWS, LANES), F32),
        scratch_shapes=[pltpu.VMEM((EXPERT_ROWS * TILE_ROWS, LANES), F32), pltpu.SemaphoreType.DMA,
                        pltpu.SemaphoreType.DMA],
        compiler_params=pltpu.CompilerParams(
            dimension_semantics=("arbitrary",), vmem_limit_bytes=VMEM_LIMIT),
        name="dispatch",
    )(dest, pad_start, pad_len, n_used, hf_tiles)


def _expert_kernel(be_ref, nu_ref, nxt_ref, par_ref, xs_ref, wu_hbm, bu_ref, wd_hbm, bd_ref, y_ref,
                   wu_f32, wd_f32, wu_bf, wd_bf, wsem):
    i = pl.program_id(0)

    def weight_copies(expert, s):
        return (pltpu.make_async_copy(wu_hbm.at[expert], wu_f32.at[s], wsem.at[0, s]),
                pltpu.make_async_copy(wd_hbm.at[expert], wd_f32.at[s], wsem.at[1, s]))

    @pl.when(i == 0)
    def _():
        for cp in weight_copies(be_ref[0], par_ref[0]):
            cp.start()

    @pl.when((i == 0) | (be_ref[i] != be_ref[jnp.maximum(i - 1, 0)]))
    def _():
        s = par_ref[i]
        for cp in weight_copies(be_ref[i], s):
            cp.wait()

        @pl.when(nxt_ref[i] >= 0)
        def _():
            for cp in weight_copies(nxt_ref[i], 1 - s):
                cp.start()

        wu_bf[...] = wu_f32[s].astype(BF16)
        wd_bf[...] = wd_f32[s].astype(BF16)

    @pl.when(i < nu_ref[0])
    def _():
        xb = _load_token_tiles(xs_ref, 0, EXPERT_ROWS).astype(BF16)
        up = _dot(xb, wu_bf[...]) + bu_ref[0]
        x_glu = jnp.minimum(up[:, :D_FF], SWIGLU_LIMIT)
        x_lin = jnp.clip(up[:, D_FF:], -SWIGLU_LIMIT, SWIGLU_LIMIT)
        act = x_glu * jax.nn.sigmoid(SWIGLU_ALPHA * x_glu) * (x_lin + 1.0)
        _store_token_tiles(y_ref, _dot(act.astype(BF16), wd_bf[...]) + bd_ref[0])

    @pl.when(i >= nu_ref[0])
    def _():
        y_ref[...] = jnp.zeros_like(y_ref)


def _experts(block_expert, n_used, next_expert, parity, xs_tiles, w_up, b_up, w_down, b_down):
    n_blocks = block_expert.shape[0]
    blk = (EXPERT_ROWS * TILE_ROWS, LANES)
    grid_spec = pltpu.PrefetchScalarGridSpec(
        num_scalar_prefetch=4,
        grid=(n_blocks,),
        in_specs=[
            pl.BlockSpec(blk, lambda i, be, nu, nx, pa: (jnp.minimum(i, nu[0] - 1), 0)),
            pl.BlockSpec(memory_space=pl.ANY),
            pl.BlockSpec((1, 1, 2 * D_FF), lambda i, be, nu, nx, pa: (be[i], 0, 0)),
            pl.BlockSpec(memory_space=pl.ANY),
            pl.BlockSpec((1, 1, D_MODEL), lambda i, be, nu, nx, pa: (be[i], 0, 0)),
        ],
        out_specs=pl.BlockSpec(blk, lambda i, be, nu, nx, pa: (i, 0)),
        scratch_shapes=[
            pltpu.VMEM((2, D_MODEL, 2 * D_FF), F32),
            pltpu.VMEM((2, D_FF, D_MODEL), F32),
            pltpu.VMEM((D_MODEL, 2 * D_FF), BF16),
            pltpu.VMEM((D_FF, D_MODEL), BF16),
            pltpu.SemaphoreType.DMA((2, 2)),
        ],
    )
    return pl.pallas_call(
        _expert_kernel,
        grid_spec=grid_spec,
        out_shape=jax.ShapeDtypeStruct((n_blocks * blk[0], LANES), F32),
        compiler_params=pltpu.CompilerParams(
            dimension_semantics=("arbitrary",), vmem_limit_bytes=VMEM_LIMIT),
        name="experts",
    )(block_expert, n_used, next_expert, parity, xs_tiles, w_up, b_up[:, None, :], w_down, b_down[:, None, :])


_COMBINE_RING = 3


def _combine_kernel(dst0_ref, dst1_ref, dst2_ref, y_hbm, h1_ref, route_ref, g_ref, o_ref, *scratch):
    bufs, sem = scratch[:_COMBINE_RING], scratch[_COMBINE_RING]
    i = pl.program_id(0)
    last = pl.num_programs(0) - 1
    tm = h1_ref.shape[0]
    rows = TOP_K * tm

    def gather_group(idx_ref, s, row0):
        srcs = [idx_ref[row0 + r] for r in range(_ISSUE_UNROLL)]
        for r in range(_ISSUE_UNROLL):
            pltpu.make_async_copy(_token_tile(y_hbm, srcs[r]), _token_tile(bufs[s], row0 + r),
                                  sem.at[s]).start(priority=r % 2)

    @pl.when(i == 0)
    def _():
        for s, idx_ref in ((0, dst0_ref), (1, dst1_ref)):
            def body(o, carry, s=s, idx_ref=idx_ref):
                gather_group(idx_ref, s, pl.multiple_of(o * _ISSUE_UNROLL, _ISSUE_UNROLL))
                return carry
            lax.fori_loop(0, rows // _ISSUE_UNROLL, body, 0)

    def step(s):
        cur = bufs[s]
        ahead = (s + 2) % _COMBINE_RING
        _tiles_wait_copy(y_hbm, cur, rows, sem.at[s]).wait()
        for g in range(rows // _ISSUE_UNROLL):
            gather_group(dst2_ref, ahead, g * _ISSUE_UNROLL)
        assert tm == LANES
        rt = jnp.concatenate([route_ref[...], jnp.zeros((LANES - _ROUTE_ROWS, tm), F32)], axis=0).T
        acc = h1_ref[...]
        for k in range(TOP_K):
            gate = rt[:, _ROUTE_GATE + k:_ROUTE_GATE + k + 1]
            acc = acc + gate * _load_token_tiles(cur, k * tm, tm)
        ms = jnp.mean(acc * acc, axis=-1, keepdims=True)
        o_ref[...] = acc * lax.rsqrt(ms + RMS_EPS) * g_ref[...]

        @pl.when(i == last)
        def _():
            for t in ((s + 1) % _COMBINE_RING, ahead):
                _tiles_wait_copy(y_hbm, bufs[t], rows, sem.at[t]).wait()

    for s in range(_COMBINE_RING):
        pl.when(i % _COMBINE_RING == s)(functools.partial(step, s))


def _combine(dest_kmajor, y, h1, route, g, tm):
    n = h1.shape[0]
    n_tiles = n // tm
    idx_spec = lambda ahead: pl.BlockSpec(
        (TOP_K * tm,), lambda i: (jnp.minimum(i + ahead, n_tiles - 1),), memory_space=pltpu.SMEM)
    ring_buf = pltpu.VMEM((TOP_K * tm * TILE_ROWS, LANES), F32)
    return pl.pallas_call(
        _combine_kernel,
        grid=(n_tiles,),
        in_specs=[
            idx_spec(0), idx_spec(1), idx_spec(2),
            pl.BlockSpec(memory_space=pl.ANY),
            pl.BlockSpec((tm, D_MODEL), lambda i: (i, 0)),
            pl.BlockSpec((_ROUTE_ROWS, tm), lambda i: (0, i)),
            pl.BlockSpec((1, D_MODEL), lambda i: (0, 0)),
        ],
        out_specs=pl.BlockSpec((tm, D_MODEL), lambda i: (i, 0)),
        out_shape=jax.ShapeDtypeStruct((n, D_MODEL), F32),
        scratch_shapes=[ring_buf] * _COMBINE_RING + [pltpu.SemaphoreType.DMA((_COMBINE_RING,))],
        compiler_params=pltpu.CompilerParams(
            dimension_semantics=("arbitrary",), vmem_limit_bytes=VMEM_LIMIT),
        name="combine",
    )(dest_kmajor, dest_kmajor, dest_kmajor, y, h1, route, g)


def _routing_tables(route, counts, n, tokens_dispatch, tokens_combine):
    tm = EXPERT_ROWS
    i32 = jnp.int32
    n_blocks = (n * TOP_K + N_EXPERTS * (tm - 1)) // tm
    idx = route[_ROUTE_IDX:_ROUTE_IDX + TOP_K].astype(i32)
    rank = route[_ROUTE_RANK:_ROUTE_RANK + TOP_K].astype(i32)
    cnt = counts[:, 0].astype(i32)
    eid = jnp.arange(N_EXPERTS, dtype=i32)
    upto = eid[None, :] <= eid[:, None]
    blocks_e = (cnt + tm - 1) // tm
    blocks_end = jnp.sum(jnp.where(upto, blocks_e[None, :], 0), axis=1)
    row_start = (blocks_end - blocks_e) * tm
    n_used = blocks_end[N_EXPERTS - 1]
    used = blocks_e > 0

    def lookup(table, keys):
        hit = keys[None] == eid.reshape((N_EXPERTS,) + (1,) * keys.ndim)
        return jnp.sum(jnp.where(hit, table.reshape((N_EXPERTS,) + (1,) * keys.ndim), 0), axis=0)

    dest = lookup(row_start, idx) + rank
    blk = jnp.arange(n_blocks, dtype=i32)
    last_used = jnp.max(jnp.where(used, eid, 0))
    be = jnp.where(blk < n_used, jnp.sum((blocks_end[None, :] <= blk[:, None]).astype(i32), axis=1), last_used)
    later_used = used[None, :] & (eid[None, :] > eid[:, None])
    after = jnp.min(jnp.where(later_used, eid[None, :], N_EXPERTS), axis=1)
    next_e = jnp.where(after < N_EXPERTS, after, -1)
    parity_e = (jnp.sum((upto & used[None, :]).astype(i32), axis=1) - 1) % 2

    def k_major(tokens):
        return dest.reshape(TOP_K, n // tokens, tokens).transpose(1, 0, 2).reshape(-1)

    return (be.astype(i32), n_used.reshape(1), lookup(next_e, be).astype(i32), lookup(parity_e, be).astype(i32),
            k_major(tokens_dispatch), k_major(tokens_combine), row_start + cnt, blocks_e * tm - cnt,
            n_blocks * tm)


def kernel(x, meta_tokens, mix_norm_g, w_in, ssm_a_re, ssm_a_im, ssm_log_dt, ssm_b_re, ssm_b_im,
           ssm_c_re, ssm_c_im, ssm_d, w_ssm_glu, attn_sinks, w_attn_o, w_out, ffn_norm_g,
           router_w, router_b, w_up, b_up, w_down, b_down, final_norm_g):
    bsz, seq, d = x.shape
    assert d == D_MODEL and seq % max(WINDOW, SSM_CHUNK) == 0
    assert mix_norm_g.shape[0] == 1, "single-layer trunk"
    n = bsz * seq
    tm_proj = min(512, n)
    tm_mix = min(256, n)
    tm_comb = min(128, n)
    x2 = x.reshape(n, D_MODEL)

    w_in_bf = w_in[0].astype(BF16)
    g_mix = mix_norm_g[0][None, :]
    u, q, kv, gs, ga = _in_proj(x2, g_mix, w_in_bf, tm_proj)
    u_m, _, kv_m, _, _ = _in_proj(meta_tokens, g_mix, w_in_bf, N_META)

    ssm_par = _ssm_params(ssm_a_re[0], ssm_a_im[0], ssm_log_dt[0], ssm_b_re[0], ssm_b_im[0],
                          ssm_c_re[0], ssm_c_im[0], ssm_d[0], SSM_CHUNK)
    y_ssm = _ssm(u, u_m, *ssm_par, batch=bsz, chunk=SSM_CHUNK)

    attn = _attention(attn_sinks[0], q, kv, kv_m, seq // WINDOW)

    h1, hf, route, counts = _mix(
        x2, y_ssm, attn, gs, ga, w_ssm_glu[0].astype(BF16), w_attn_o[0].astype(BF16),
        w_out[0].astype(BF16), ffn_norm_g[0][None, :], router_w[0], router_b[0][None, :], tm_mix)

    tok_disp = min(1024, n)
    be, n_used, next_e, parity, dest_disp, dest_comb, pad_start, pad_len, n_rows = _routing_tables(
        route, counts, n, tok_disp, tm_comb)
    xs = _dispatch(dest_disp, pad_start, pad_len, n_used, hf, n_rows, tok_disp)
    y = _experts(be, n_used, next_e, parity, xs, w_up[0], b_up[0], w_down[0], b_down[0])
    out = _combine(dest_comb, y, h1, route, final_norm_g[None, :], tm_comb)
    return out.reshape(bsz, seq, D_MODEL)
```

```python
import functools
import math

import jax
import jax.numpy as jnp
from jax import lax
from jax.experimental import pallas as pl
from jax.experimental.pallas import tpu as pltpu

F32 = jnp.float32
BF16 = jnp.bfloat16

D_MODEL = 1024
N_META = 16
SSM_WIDTH = 512
SSM_GROUP = 16
SSM_GROUPS = 32
SSM_STATE = 64
HEAD_DIM = 64
N_HEADS = 16
N_KV_HEADS = 2
KV_REP = N_HEADS // N_KV_HEADS
WINDOW = 128
Q_WIDTH = N_HEADS * HEAD_DIM
KV_WIDTH = N_KV_HEADS * HEAD_DIM
N_EXPERTS = 32
TOP_K = 4
D_FF = 1024
SWIGLU_ALPHA = 1.702
SWIGLU_LIMIT = 7.0
RMS_EPS = 1e-5
NEG_INF = -1e30

_U0, _Q0, _KV0, _GS0, _GA0, _IN_END = 0, 512, 1536, 1792, 2816, 3840

SSM_CH_BLOCK = 128
SSM_HALF = (SSM_CH_BLOCK // SSM_GROUP) * SSM_STATE
SSM_CHUNK = 32
SSM_PITCH = 40
LOG2_E = math.log2(math.e)
EXPERT_ROWS = 512
VMEM_LIMIT = 56 * 1024 * 1024


def _dot(a, b):
    return jnp.dot(a, b, preferred_element_type=F32)


def _dot_nt(a, b):
    return lax.dot_general(a, b, (((1,), (1,)), ((), ())), preferred_element_type=F32)


LANES = 128
TILE_ROWS = D_MODEL // LANES


def _store_token_tiles(ref, x):
    rows = x.shape[0]
    for j in range(TILE_ROWS):
        ref[pl.ds(j, rows, stride=TILE_ROWS), :] = x[:, j * LANES:(j + 1) * LANES]


def _load_token_tiles(ref, start_row, rows):
    return jnp.concatenate(
        [ref[pl.ds(start_row * TILE_ROWS + j, rows, stride=TILE_ROWS), :] for j in range(TILE_ROWS)], axis=1)


def _token_tile(ref, row):
    return ref.at[pl.ds(pl.multiple_of(row * TILE_ROWS, TILE_ROWS), TILE_ROWS), :]


def _in_proj_kernel(x_ref, g_ref, w_ref, u_ref, q_ref, kv_ref, gs_ref, ga_ref, *, pitched):
    x = x_ref[...]
    ms = jnp.mean(x * x, axis=-1, keepdims=True)
    hn = (x * lax.rsqrt(ms + RMS_EPS) * g_ref[...]).astype(BF16)
    u = _dot(hn, w_ref[:, _U0:_Q0])
    if pitched:
        for c in range(u.shape[0] // SSM_CHUNK):
            u_ref[c * SSM_PITCH:c * SSM_PITCH + SSM_CHUNK, :] = u[c * SSM_CHUNK:(c + 1) * SSM_CHUNK, :]
            u_ref[c * SSM_PITCH + SSM_CHUNK:(c + 1) * SSM_PITCH, :] = jnp.zeros(
                (SSM_PITCH - SSM_CHUNK, SSM_WIDTH), F32)
    else:
        u_ref[...] = u
    q_ref[...] = (_dot(hn, w_ref[:, _Q0:_KV0]) * (HEAD_DIM ** -0.5 * LOG2_E)).astype(BF16)
    kv_ref[...] = _dot(hn, w_ref[:, _KV0:_GS0]).astype(BF16)
    gs_ref[...] = jax.nn.sigmoid(_dot(hn, w_ref[:, _GS0:_GA0])).astype(BF16)
    ga_ref[...] = jax.nn.sigmoid(_dot(hn, w_ref[:, _GA0:_IN_END])).astype(BF16)


def _in_proj(x2, g, w_bf, tm, pitched):
    n = x2.shape[0]
    row = lambda w: pl.BlockSpec((tm, w), lambda i: (i, 0))
    full = lambda a: pl.BlockSpec(a.shape, lambda i: (0,) * a.ndim)
    u_rows = (lambda r: r // SSM_CHUNK * SSM_PITCH) if pitched else (lambda r: r)
    return pl.pallas_call(
        functools.partial(_in_proj_kernel, pitched=pitched),
        grid=(n // tm,),
        in_specs=[row(D_MODEL), full(g), full(w_bf)],
        out_specs=[pl.BlockSpec((u_rows(tm), SSM_WIDTH), lambda i: (i, 0)),
                   row(Q_WIDTH), row(2 * KV_WIDTH), row(D_MODEL), row(D_MODEL)],
        out_shape=[
            jax.ShapeDtypeStruct((u_rows(n), SSM_WIDTH), F32),
            jax.ShapeDtypeStruct((n, Q_WIDTH), BF16),
            jax.ShapeDtypeStruct((n, 2 * KV_WIDTH), BF16),
            jax.ShapeDtypeStruct((n, D_MODEL), BF16),
            jax.ShapeDtypeStruct((n, D_MODEL), BF16),
        ],
        compiler_params=pltpu.CompilerParams(
            dimension_semantics=("arbitrary",), vmem_limit_bytes=VMEM_LIMIT),
        name="in_proj",
    )(x2, g, w_bf)


def _ssm_kernel(u_ref, um_ref, bm_ref, cm_ref, ar_ref, ai_ref, atr_ref, ati_ref, d_ref,
                y_ref, sre, sim, *, chunk, rows, batch):
    h = SSM_HALF
    bm = bm_ref[0]
    cm = cm_ref[0]
    ar, ai = ar_ref[0], ai_ref[0]
    atr, ati = atr_ref[0], ati_ref[0]
    dsk = d_ref[0]
    n_chunks = rows // batch

    def advance(sr, si, bu):
        return ar * sr - ai * si + bu[:, :h], ar * si + ai * sr + bu[:, h:]

    bum = _dot(um_ref[...].astype(BF16), bm)
    mr = jnp.zeros((1, h), F32)
    mi = jnp.zeros((1, h), F32)
    for j in range(N_META):
        mr, mi = advance(mr, mi, bum[j:j + 1, :])

    def u_step(t):
        return u_ref[pl.ds(t, rows, stride=SSM_PITCH), :]

    for pad in range(chunk, SSM_PITCH):
        y_ref[pl.ds(pad, rows, stride=SSM_PITCH), :] = jnp.zeros((rows, SSM_CH_BLOCK), F32)

    sre[...] = jnp.zeros_like(sre)
    sim[...] = jnp.zeros_like(sim)

    def pass_a(t, carry):
        bu = _dot(u_step(t).astype(BF16), bm)
        nr, ni = advance(sre[...], sim[...], bu)
        sre[...] = nr
        sim[...] = ni
        return carry

    lax.fori_loop(0, chunk, pass_a, 0)

    def over_chunks(c, carry):
        new = []
        for b in range(batch):
            cr, ci = carry[2 * b], carry[2 * b + 1]
            row = pl.ds(b * n_chunks + c, 1)
            er, ei = sre[row, :], sim[row, :]
            sre[row, :] = cr
            sim[row, :] = ci
            new += [atr * cr - ati * ci + er, atr * ci + ati * cr + ei]
        return tuple(new)

    lax.fori_loop(0, n_chunks, over_chunks, (mr, mi) * batch)

    def pass_b(t, carry):
        ut = u_step(t)
        bu = _dot(ut.astype(BF16), bm)
        nr, ni = advance(sre[...], sim[...], bu)
        sre[...] = nr
        sim[...] = ni
        y = _dot(nr.astype(BF16), cm[:h, :]) + _dot(ni.astype(BF16), cm[h:, :]) + dsk * ut
        y_ref[pl.ds(t, rows, stride=SSM_PITCH), :] = y
        return carry

    lax.fori_loop(0, chunk, pass_b, 0)


def _ssm(u, u_meta, bmat, cmat, a_re, a_im, at_re, at_im, dskip, batch, chunk):
    n = u.shape[0]
    rows = n // SSM_PITCH
    nblk = SSM_WIDTH // SSM_CH_BLOCK
    col = lambda r: pl.BlockSpec((r, SSM_CH_BLOCK), lambda j: (0, j))
    par = lambda a: pl.BlockSpec((1,) + a.shape[1:], lambda j: (j, 0, 0))
    return pl.pallas_call(
        functools.partial(_ssm_kernel, chunk=chunk, rows=rows, batch=batch),
        grid=(nblk,),
        in_specs=[col(n), col(N_META), par(bmat), par(cmat), par(a_re), par(a_im),
                  par(at_re), par(at_im), par(dskip)],
        out_specs=col(n),
        out_shape=jax.ShapeDtypeStruct((n, SSM_WIDTH), F32),
        scratch_shapes=[pltpu.VMEM((rows, SSM_HALF), F32), pltpu.VMEM((rows, SSM_HALF), F32)],
        compiler_params=pltpu.CompilerParams(
            dimension_semantics=("arbitrary",), vmem_limit_bytes=VMEM_LIMIT),
        name="ssm",
    )(u, u_meta, bmat, cmat, a_re, a_im, at_re, at_im, dskip)


def _ssm_params(a_re, a_im, log_dt, b_re, b_im, c_re, c_im, d_skip, chunk):
    dt = jnp.exp(log_dt)[:, None]
    mag = jnp.exp(a_re * dt)
    ang = a_im * dt
    abar_re, abar_im = mag * jnp.cos(ang), mag * jnp.sin(ang)
    den = a_re * a_re + a_im * a_im
    nr, ni = abar_re - 1.0, abar_im
    coef_re = ((nr * a_re + ni * a_im) / den)[..., None]
    coef_im = ((ni * a_re - nr * a_im) / den)[..., None]
    bbar_re = coef_re * b_re - coef_im * b_im
    bbar_im = coef_re * b_im + coef_im * b_re
    magt = jnp.exp(a_re * dt * chunk)
    at_re, at_im = magt * jnp.cos(ang * chunk), magt * jnp.sin(ang * chunk)

    nblk = SSM_WIDTH // SSM_CH_BLOCK
    gpb = SSM_GROUPS // nblk
    eye = jnp.eye(gpb, dtype=F32)

    def in_map(b):
        b = b.reshape(nblk, gpb, SSM_STATE, SSM_GROUP)
        return jnp.einsum('jgpc,gh->jgchp', b, eye).reshape(nblk, SSM_CH_BLOCK, gpb * SSM_STATE)

    def out_map(c):
        c = c.reshape(nblk, gpb, SSM_GROUP, SSM_STATE)
        return jnp.einsum('jgcp,gh->jgphc', c, eye).reshape(nblk, gpb * SSM_STATE, SSM_CH_BLOCK)

    bmat = jnp.concatenate([in_map(bbar_re), in_map(bbar_im)], axis=2).astype(BF16)
    cmat = jnp.concatenate([out_map(c_re), -out_map(c_im)], axis=1).astype(BF16)
    vec = lambda v: v.reshape(nblk, 1, SSM_HALF)
    return (bmat, cmat, vec(abar_re), vec(abar_im), vec(at_re), vec(at_im),
            d_skip.reshape(nblk, 1, SSM_CH_BLOCK))


def _attn_kernel(sink_ref, q_ref, kvc_ref, kvp_ref, kvm_ref, o_ref, *, blocks_per_seq):
    n = pl.program_id(0) % blocks_per_seq
    w = WINDOW
    hd = HEAD_DIM
    qi = lax.broadcasted_iota(jnp.int32, (w, w), 0)
    lane = lax.broadcasted_iota(jnp.int32, (w, w), 1)
    vis_prev = (lane > qi) & (n > 0)
    vis_cur = lane <= qi
    left = lane < hd
    meta_l = lane < N_META
    meta_r = (lane >= N_META) & (lane < 2 * N_META)

    def placed(x_bf):
        x = x_bf.astype(F32)
        xr = pltpu.roll(x, hd, 1)
        lm = lax.broadcasted_iota(jnp.int32, x.shape, 1) < hd
        z = jnp.zeros_like(x)
        return {(0, 0): jnp.where(lm, x, z), (0, 1): jnp.where(lm, z, xr),
                (1, 0): jnp.where(lm, xr, z), (1, 1): jnp.where(lm, z, x)}

    kp, kc, km = placed(kvp_ref[:, :KV_WIDTH]), placed(kvc_ref[:, :KV_WIDTH]), placed(kvm_ref[:, :KV_WIDTH])
    vp, vc, vm = placed(kvp_ref[:, KV_WIDTH:]), placed(kvc_ref[:, KV_WIDTH:]), placed(kvm_ref[:, KV_WIDTH:])
    pad_rows = w - 2 * N_META
    zpad = jnp.zeros((pad_rows, w), F32)
    krow = lax.broadcasted_iota(jnp.int32, (5 * w, w), 0)
    klane = lax.broadcasted_iota(jnp.int32, (5 * w, w), 1)
    row_l = (krow < 2 * w) | ((krow >= 4 * w) & (krow < 4 * w + N_META))
    row_r = ((krow >= 2 * w) & (krow < 4 * w)) | ((krow >= 4 * w + N_META) & (krow < 4 * w + 2 * N_META))
    den_cols = jnp.where((row_l & (klane < hd)) | (row_r & (klane >= hd)), 1.0, 0.0)

    for j in range(N_KV_HEADS):
        kcat = jnp.concatenate(
            [kp[j, 0], kc[j, 0], kp[j, 1], kc[j, 1], km[j, 0], km[j, 1], zpad], axis=0).astype(BF16)
        vcat = jnp.concatenate(
            [jnp.concatenate([vp[j, 0], vc[j, 0], vp[j, 1], vc[j, 1], vm[j, 0], vm[j, 1], zpad], axis=0),
             den_cols], axis=1).astype(BF16)
        for r in range(KV_REP // 2):
            pr = j * (KV_REP // 2) + r
            s = _dot_nt(q_ref[:, pr * w:(pr + 1) * w], kcat)
            s_l = jnp.where(vis_cur, s[:, w:2 * w], jnp.where(vis_prev, s[:, 0:w], NEG_INF))
            s_r = jnp.where(vis_cur, s[:, 3 * w:4 * w], jnp.where(vis_prev, s[:, 2 * w:3 * w], NEG_INF))
            s_m = s[:, 4 * w:]
            sink_l, sink_r = sink_ref[2 * pr] * LOG2_E, sink_ref[2 * pr + 1] * LOG2_E
            m_l = jnp.maximum(jnp.max(jnp.maximum(s_l, jnp.where(meta_l, s_m, NEG_INF)),
                                      axis=1, keepdims=True), sink_l)
            m_r = jnp.maximum(jnp.max(jnp.maximum(s_r, jnp.where(meta_r, s_m, NEG_INF)),
                                      axis=1, keepdims=True), sink_r)
            s_m = jnp.where(meta_l, s_m - m_l, jnp.where(meta_r, s_m - m_r, NEG_INF))
            e_l, e_r = jnp.exp2(s_l - m_l), jnp.exp2(s_r - m_r)
            e = jnp.concatenate([jnp.where(vis_cur, 0.0, e_l), jnp.where(vis_cur, e_l, 0.0),
                                 jnp.where(vis_cur, 0.0, e_r), jnp.where(vis_cur, e_r, 0.0),
                                 jnp.exp2(s_m)], axis=1).astype(BF16)
            acc = _dot(e, vcat)
            den = acc[:, w:] + jnp.where(left, jnp.exp2(sink_l - m_l), jnp.exp2(sink_r - m_r))
            o_ref[:, pr * w:(pr + 1) * w] = (acc[:, :w] / den).astype(BF16)


def _attention(sinks, q, kv, kv_meta, blocks_per_seq):
    n = q.shape[0]
    return pl.pallas_call(
        functools.partial(_attn_kernel, blocks_per_seq=blocks_per_seq),
        grid=(n // WINDOW,),
        in_specs=[
            pl.BlockSpec(memory_space=pltpu.SMEM),
            pl.BlockSpec((WINDOW, Q_WIDTH), lambda g: (g, 0)),
            pl.BlockSpec((WINDOW, 2 * KV_WIDTH), lambda g: (g, 0)),
            pl.BlockSpec((WINDOW, 2 * KV_WIDTH), lambda g: (jnp.maximum(g - 1, 0), 0)),
            pl.BlockSpec((N_META, 2 * KV_WIDTH), lambda g: (0, 0)),
        ],
        out_specs=pl.BlockSpec((WINDOW, Q_WIDTH), lambda g: (g, 0)),
        out_shape=jax.ShapeDtypeStruct((n, Q_WIDTH), BF16),
        compiler_params=pltpu.CompilerParams(dimension_semantics=("arbitrary",)),
        name="attn",
    )(sinks, q, kv, kv, kv_meta)


_ROUTE_IDX, _ROUTE_RANK, _ROUTE_GATE, _ROUTE_ROWS = 0, TOP_K, 2 * TOP_K, 16


def _mix_kernel(x_ref, y_ref, at_ref, gs_ref, ga_ref, wglu_ref, wo_ref, wout_ref, fg_ref, rwh_ref, rwl_ref, rb_ref,
                h1_ref, hf_ref, route_ref, cnt_ref, cnt_scr):
    tm = x_ref.shape[0]

    @pl.when(pl.program_id(0) == 0)
    def _():
        cnt_scr[...] = jnp.zeros_like(cnt_scr)

    y_ssm = jnp.concatenate([y_ref[c * SSM_PITCH:c * SSM_PITCH + SSM_CHUNK, :]
                             for c in range(tm // SSM_CHUNK)], axis=0)
    glu = _dot(jax.nn.gelu(y_ssm).astype(BF16), wglu_ref[...])
    branch_ssm = glu[:, :D_MODEL] * jax.nn.sigmoid(glu[:, D_MODEL:])
    branch_attn = _dot(at_ref[...], wo_ref[...])
    merged = gs_ref[...].astype(F32) * branch_ssm + ga_ref[...].astype(F32) * branch_attn
    h1 = x_ref[...] + _dot(merged.astype(BF16), wout_ref[...])
    h1_ref[...] = h1
    ms = jnp.mean(h1 * h1, axis=-1, keepdims=True)
    hf = h1 * lax.rsqrt(ms + RMS_EPS) * fg_ref[...]
    _store_token_tiles(hf_ref, hf)

    hf_hi = hf.astype(BF16)
    hf_lo = (hf - hf_hi.astype(F32)).astype(BF16)
    logits = (_dot(hf_hi, rwh_ref[...]) + (_dot(hf_hi, rwl_ref[...]) + _dot(hf_lo, rwh_ref[...]))
              + rb_ref[...])
    lt = logits.T[:N_EXPERTS, :]
    erow = lax.broadcasted_iota(jnp.int32, (N_EXPERTS, tm), 0)
    vals, idxs, hots = [], [], []
    rest = lt
    for _ in range(TOP_K):
        m = jnp.max(rest, axis=0, keepdims=True)
        first = jnp.min(jnp.where(rest == m, erow, N_EXPERTS), axis=0, keepdims=True)
        hot = erow == first
        vals.append(m)
        idxs.append(first)
        hots.append(hot)
        rest = jnp.where(hot, -jnp.inf, rest)
    exps = [jnp.exp(v - vals[0]) for v in vals]
    tot = exps[0] + exps[1] + exps[2] + exps[3]

    sel = (hots[0] | hots[1] | hots[2] | hots[3]).astype(F32)
    ti = lax.broadcasted_iota(jnp.int32, (tm, tm), 0)
    tj = lax.broadcasted_iota(jnp.int32, (tm, tm), 1)
    earlier = (ti < tj).astype(BF16)
    rank_e = _dot(sel.astype(BF16), earlier) + cnt_scr[...]
    cnt_scr[...] = cnt_scr[...] + jnp.sum(sel, axis=1, keepdims=True)
    cnt_ref[...] = cnt_scr[...]

    rrow = lax.broadcasted_iota(jnp.int32, (_ROUTE_ROWS, tm), 0)
    route = jnp.zeros((_ROUTE_ROWS, tm), F32)
    for k in range(TOP_K):
        rank_k = jnp.sum(jnp.where(hots[k], rank_e, 0.0), axis=0, keepdims=True)
        route = jnp.where(rrow == _ROUTE_IDX + k, idxs[k].astype(F32), route)
        route = jnp.where(rrow == _ROUTE_RANK + k, rank_k, route)
        route = jnp.where(rrow == _ROUTE_GATE + k, exps[k] / tot, route)
    route_ref[...] = route


def _mix(x2, y, attn, gs, ga, wglu, wo, wout, fg, rw, rb, tm):
    n = x2.shape[0]
    rw = jnp.pad(rw, ((0, 0), (0, LANES - N_EXPERTS)))
    rb = jnp.pad(rb, ((0, 0), (0, LANES - N_EXPERTS)), constant_values=NEG_INF)
    rw_hi = rw.astype(BF16)
    rw_lo = (rw - rw_hi.astype(F32)).astype(BF16)
    row = lambda w: pl.BlockSpec((tm, w), lambda i: (i, 0))
    full = lambda a: pl.BlockSpec(a.shape, lambda i: (0,) * a.ndim)
    return pl.pallas_call(
        _mix_kernel,
        grid=(n // tm,),
        in_specs=[row(D_MODEL), pl.BlockSpec((tm // SSM_CHUNK * SSM_PITCH, SSM_WIDTH), lambda i: (i, 0)),
                  row(Q_WIDTH), row(D_MODEL), row(D_MODEL),
                  full(wglu), full(wo), full(wout), full(fg), full(rw_hi), full(rw_lo), full(rb)],
        out_specs=[row(D_MODEL), pl.BlockSpec((tm * TILE_ROWS, LANES), lambda i: (i, 0)),
                   pl.BlockSpec((_ROUTE_ROWS, tm), lambda i: (0, i)),
                   pl.BlockSpec((N_EXPERTS, 1), lambda i: (0, 0))],
        out_shape=[
            jax.ShapeDtypeStruct((n, D_MODEL), F32),
            jax.ShapeDtypeStruct((n * TILE_ROWS, LANES), F32),
            jax.ShapeDtypeStruct((_ROUTE_ROWS, n), F32),
            jax.ShapeDtypeStruct((N_EXPERTS, 1), F32),
        ],
        scratch_shapes=[pltpu.VMEM((N_EXPERTS, 1), F32)],
        compiler_params=pltpu.CompilerParams(
            dimension_semantics=("arbitrary",), vmem_limit_bytes=VMEM_LIMIT),
        name="mix_router",
    )(x2, y, attn, gs, ga, wglu, wo, wout, fg, rw_hi, rw_lo, rb)


def _tiles_wait_copy(src_hbm, dst, n_tiles, sem):
    rows = n_tiles * TILE_ROWS
    return pltpu.make_async_copy(src_hbm.at[pl.ds(0, rows), :], dst.at[pl.ds(0, rows), :], sem)


_ISSUE_UNROLL = 16


def _dispatch_kernel(dst_ref, pad_start_ref, pad_len_ref, nu_ref, hf_ref, xs_hbm, zero_blk, sem, pad_sem):
    i = pl.program_id(0)
    last = pl.num_programs(0) - 1
    pairs = dst_ref.shape[0]
    n_blocks = xs_hbm.shape[0] // (EXPERT_ROWS * TILE_ROWS)

    def zero_copy(row, rows):
        src = zero_blk.at[pl.ds(0, rows * TILE_ROWS), :]
        dst = xs_hbm.at[pl.ds(pl.multiple_of(row * TILE_ROWS, TILE_ROWS), rows * TILE_ROWS), :]
        return pltpu.make_async_copy(src, dst, pad_sem)

    def for_each_pad(fn):
        def per_expert(e, carry):
            row, left = pad_start_ref[e], pad_len_ref[e]
            size = EXPERT_ROWS // 2
            while size >= 1:
                take = left & size

                @pl.when(take != 0)
                def _(row=row, size=size):
                    fn(zero_copy(row, size))

                row = row + take
                size //= 2
            return carry
        lax.fori_loop(0, N_EXPERTS, per_expert, 0)

        def per_block(b, carry):
            fn(zero_copy(b * EXPERT_ROWS, EXPERT_ROWS))
            return carry
        lax.fori_loop(nu_ref[0], n_blocks, per_block, 0)

    @pl.when(i == 0)
    def _():
        zero_blk[...] = jnp.zeros_like(zero_blk)
        for_each_pad(lambda cp: cp.start())

    tokens = pairs // TOP_K
    for k in range(TOP_K):
        def issue(o, carry, k=k):
            tok0 = pl.multiple_of(o * _ISSUE_UNROLL, _ISSUE_UNROLL)
            dsts = [dst_ref[k * tokens + tok0 + r] for r in range(_ISSUE_UNROLL)]
            for r in range(_ISSUE_UNROLL):
                pltpu.make_async_copy(_token_tile(hf_ref, tok0 + r), _token_tile(xs_hbm, dsts[r]),
                                      sem).start(priority=r % 2)
            return carry

        lax.fori_loop(0, tokens // _ISSUE_UNROLL, issue, 0)
    for _ in range(TOP_K):
        pltpu.make_async_copy(hf_ref, xs_hbm.at[pl.ds(0, hf_ref.shape[0]), :], sem).wait()

    @pl.when(i == last)
    def _():
        for_each_pad(lambda cp: cp.wait())


def _dispatch(dest, pad_start, pad_len, n_used, hf_tiles, n_rows, tokens_per_step):
    n = dest.shape[0] // TOP_K
    pairs = tokens_per_step * TOP_K
    assert n % tokens_per_step == 0 and tokens_per_step % _ISSUE_UNROLL == 0
    smem = lambda: pl.BlockSpec(memory_space=pltpu.SMEM)
    return pl.pallas_call(
        _dispatch_kernel,
        grid=(n // tokens_per_step,),
        in_specs=[pl.BlockSpec((pairs,), lambda i: (i,), memory_space=pltpu.SMEM), smem(), smem(), smem(),
                  pl.BlockSpec((tokens_per_step * TILE_ROWS, LANES), lambda i: (i, 0))],
        out_specs=pl.BlockSpec(memory_space=pl.ANY),
        out_shape=jax.ShapeDtypeStruct((n_rows * TILE_ROWS, LANES), F32),
        scratch_shapes=[pltpu.VMEM((EXPERT_ROWS * TILE_ROWS, LANES), F32), pltpu.SemaphoreType.DMA,
                        pltpu.SemaphoreType.DMA],
        compiler_params=pltpu.CompilerParams(
            dimension_semantics=("arbitrary",), vmem_limit_bytes=VMEM_LIMIT),
        name="dispatch",
    )(dest, pad_start, pad_len, n_used, hf_tiles)


def _expert_kernel(be_ref, nu_ref, nxt_ref, par_ref, xs_ref, wu_hbm, bu_ref, wd_hbm, bd_ref, y_ref,
                   wu_f32, wd_f32, wu_bf, wd_bf, wsem):
    i = pl.program_id(0)

    def weight_copies(expert, s):
        return (pltpu.make_async_copy(wu_hbm.at[expert], wu_f32.at[s], wsem.at[0, s]),
                pltpu.make_async_copy(wd_hbm.at[expert], wd_f32.at[s], wsem.at[1, s]))

    @pl.when(i == 0)
    def _():
        for cp in weight_copies(be_ref[0], par_ref[0]):
            cp.start()

    @pl.when((i == 0) | (be_ref[i] != be_ref[jnp.maximum(i - 1, 0)]))
    def _():
        s = par_ref[i]
        for cp in weight_copies(be_ref[i], s):
            cp.wait()

        @pl.when(nxt_ref[i] >= 0)
        def _():
            for cp in weight_copies(nxt_ref[i], 1 - s):
                cp.start()

        wu_bf[...] = wu_f32[s].astype(BF16)
        wd_bf[...] = wd_f32[s].astype(BF16)

    @pl.when(i < nu_ref[0])
    def _():
        xb = _load_token_tiles(xs_ref, 0, EXPERT_ROWS).astype(BF16)
        up = _dot(xb, wu_bf[...]) + bu_ref[0]
        x_glu = jnp.minimum(up[:, :D_FF], SWIGLU_LIMIT)
        x_lin = jnp.clip(up[:, D_FF:], -SWIGLU_LIMIT, SWIGLU_LIMIT)
        act = x_glu * jax.nn.sigmoid(SWIGLU_ALPHA * x_glu) * (x_lin + 1.0)
        _store_token_tiles(y_ref, _dot(act.astype(BF16), wd_bf[...]) + bd_ref[0])

    @pl.when(i >= nu_ref[0])
    def _():
        y_ref[...] = jnp.zeros_like(y_ref)


def _experts(block_expert, n_used, next_expert, parity, xs_tiles, w_up, b_up, w_down, b_down):
    n_blocks = block_expert.shape[0]
    blk = (EXPERT_ROWS * TILE_ROWS, LANES)
    grid_spec = pltpu.PrefetchScalarGridSpec(
        num_scalar_prefetch=4,
        grid=(n_blocks,),
        in_specs=[
            pl.BlockSpec(blk, lambda i, be, nu, nx, pa: (jnp.minimum(i, nu[0] - 1), 0)),
            pl.BlockSpec(memory_space=pl.ANY),
            pl.BlockSpec((1, 1, 2 * D_FF), lambda i, be, nu, nx, pa: (be[i], 0, 0)),
            pl.BlockSpec(memory_space=pl.ANY),
            pl.BlockSpec((1, 1, D_MODEL), lambda i, be, nu, nx, pa: (be[i], 0, 0)),
        ],
        out_specs=pl.BlockSpec(blk, lambda i, be, nu, nx, pa: (i, 0)),
        scratch_shapes=[
            pltpu.VMEM((2, D_MODEL, 2 * D_FF), F32),
            pltpu.VMEM((2, D_FF, D_MODEL), F32),
            pltpu.VMEM((D_MODEL, 2 * D_FF), BF16),
            pltpu.VMEM((D_FF, D_MODEL), BF16),
            pltpu.SemaphoreType.DMA((2, 2)),
        ],
    )
    return pl.pallas_call(
        _expert_kernel,
        grid_spec=grid_spec,
        out_shape=jax.ShapeDtypeStruct((n_blocks * blk[0], LANES), F32),
        compiler_params=pltpu.CompilerParams(
            dimension_semantics=("arbitrary",), vmem_limit_bytes=VMEM_LIMIT),
        name="experts",
    )(block_expert, n_used, next_expert, parity, xs_tiles, w_up, b_up[:, None, :], w_down, b_down[:, None, :])


_COMBINE_RING = 3


def _combine_kernel(dst0_ref, dst1_ref, dst2_ref, y_hbm, h1_ref, route_ref, g_ref, o_ref, *scratch):
    bufs, sem = scratch[:_COMBINE_RING], scratch[_COMBINE_RING]
    i = pl.program_id(0)
    last = pl.num_programs(0) - 1
    tm = h1_ref.shape[0]
    rows = TOP_K * tm

    def gather_group(idx_ref, s, row0):
        srcs = [idx_ref[row0 + r] for r in range(_ISSUE_UNROLL)]
        for r in range(_ISSUE_UNROLL):
            pltpu.make_async_copy(_token_tile(y_hbm, srcs[r]), _token_tile(bufs[s], row0 + r),
                                  sem.at[s]).start(priority=r % 2)

    @pl.when(i == 0)
    def _():
        for s, idx_ref in ((0, dst0_ref), (1, dst1_ref)):
            def body(o, carry, s=s, idx_ref=idx_ref):
                gather_group(idx_ref, s, pl.multiple_of(o * _ISSUE_UNROLL, _ISSUE_UNROLL))
                return carry
            lax.fori_loop(0, rows // _ISSUE_UNROLL, body, 0)

    def step(s):
        cur = bufs[s]
        ahead = (s + 2) % _COMBINE_RING
        _tiles_wait_copy(y_hbm, cur, rows, sem.at[s]).wait()
        for g in range(rows // _ISSUE_UNROLL):
            gather_group(dst2_ref, ahead, g * _ISSUE_UNROLL)
        assert tm == LANES
        rt = jnp.concatenate([route_ref[...], jnp.zeros((LANES - _ROUTE_ROWS, tm), F32)], axis=0).T
        acc = h1_ref[...]
        for k in range(TOP_K):
            gate = rt[:, _ROUTE_GATE + k:_ROUTE_GATE + k + 1]
            acc = acc + gate * _load_token_tiles(cur, k * tm, tm)
        ms = jnp.mean(acc * acc, axis=-1, keepdims=True)
        o_ref[...] = acc * lax.rsqrt(ms + RMS_EPS) * g_ref[...]

        @pl.when(i == last)
        def _():
            for t in ((s + 1) % _COMBINE_RING, ahead):
                _tiles_wait_copy(y_hbm, bufs[t], rows, sem.at[t]).wait()

    for s in range(_COMBINE_RING):
        pl.when(i % _COMBINE_RING == s)(functools.partial(step, s))


def _combine(dest_kmajor, y, h1, route, g, tm):
    n = h1.shape[0]
    n_tiles = n // tm
    idx_spec = lambda ahead: pl.BlockSpec(
        (TOP_K * tm,), lambda i: (jnp.minimum(i + ahead, n_tiles - 1),), memory_space=pltpu.SMEM)
    ring_buf = pltpu.VMEM((TOP_K * tm * TILE_ROWS, LANES), F32)
    return pl.pallas_call(
        _combine_kernel,
        grid=(n_tiles,),
        in_specs=[
            idx_spec(0), idx_spec(1), idx_spec(2),
            pl.BlockSpec(memory_space=pl.ANY),
            pl.BlockSpec((tm, D_MODEL), lambda i: (i, 0)),
            pl.BlockSpec((_ROUTE_ROWS, tm), lambda i: (0, i)),
            pl.BlockSpec((1, D_MODEL), lambda i: (0, 0)),
        ],
        out_specs=pl.BlockSpec((tm, D_MODEL), lambda i: (i, 0)),
        out_shape=jax.ShapeDtypeStruct((n, D_MODEL), F32),
        scratch_shapes=[ring_buf] * _COMBINE_RING + [pltpu.SemaphoreType.DMA((_COMBINE_RING,))],
        compiler_params=pltpu.CompilerParams(
            dimension_semantics=("arbitrary",), vmem_limit_bytes=VMEM_LIMIT),
        name="combine",
    )(dest_kmajor, dest_kmajor, dest_kmajor, y, h1, route, g)


def _routing_tables(route, counts, n, tokens_dispatch, tokens_combine):
    tm = EXPERT_ROWS
    i32 = jnp.int32
    n_blocks = (n * TOP_K + N_EXPERTS * (tm - 1)) // tm
    idx = route[_ROUTE_IDX:_ROUTE_IDX + TOP_K].astype(i32)
    rank = route[_ROUTE_RANK:_ROUTE_RANK + TOP_K].astype(i32)
    cnt = counts[:, 0].astype(i32)
    eid = jnp.arange(N_EXPERTS, dtype=i32)
    upto = eid[None, :] <= eid[:, None]
    blocks_e = (cnt + tm - 1) // tm
    blocks_end = jnp.sum(jnp.where(upto, blocks_e[None, :], 0), axis=1)
    row_start = (blocks_end - blocks_e) * tm
    n_used = blocks_end[N_EXPERTS - 1]
    used = blocks_e > 0

    def lookup(table, keys):
        hit = keys[None] == eid.reshape((N_EXPERTS,) + (1,) * keys.ndim)
        return jnp.sum(jnp.where(hit, table.reshape((N_EXPERTS,) + (1,) * keys.ndim), 0), axis=0)

    dest = lookup(row_start, idx) + rank
    blk = jnp.arange(n_blocks, dtype=i32)
    last_used = jnp.max(jnp.where(used, eid, 0))
    be = jnp.where(blk < n_used, jnp.sum((blocks_end[None, :] <= blk[:, None]).astype(i32), axis=1), last_used)
    later_used = used[None, :] & (eid[None, :] > eid[:, None])
    after = jnp.min(jnp.where(later_used, eid[None, :], N_EXPERTS), axis=1)
    next_e = jnp.where(after < N_EXPERTS, after, -1)
    parity_e = (jnp.sum((upto & used[None, :]).astype(i32), axis=1) - 1) % 2

    def k_major(tokens):
        return dest.reshape(TOP_K, n // tokens, tokens).transpose(1, 0, 2).reshape(-1)

    return (be.astype(i32), n_used.reshape(1), lookup(next_e, be).astype(i32), lookup(parity_e, be).astype(i32),
            k_major(tokens_dispatch), k_major(tokens_combine), row_start + cnt, blocks_e * tm - cnt,
            n_blocks * tm)


def kernel(x, meta_tokens, mix_norm_g, w_in, ssm_a_re, ssm_a_im, ssm_log_dt, ssm_b_re, ssm_b_im,
           ssm_c_re, ssm_c_im, ssm_d, w_ssm_glu, attn_sinks, w_attn_o, w_out, ffn_norm_g,
           router_w, router_b, w_up, b_up, w_down, b_down, final_norm_g):
    bsz, seq, d = x.shape
    assert d == D_MODEL and seq % max(WINDOW, SSM_CHUNK) == 0
    assert mix_norm_g.shape[0] == 1, "single-layer trunk"
    n = bsz * seq
    tm_proj = min(512, n)
    tm_mix = min(256, n)
    tm_comb = min(128, n)
    x2 = x.reshape(n, D_MODEL)

    w_in_bf = w_in[0].astype(BF16)
    g_mix = mix_norm_g[0][None, :]
    u, q, kv, gs, ga = _in_proj(x2, g_mix, w_in_bf, tm_proj, pitched=True)
    u_m, _, kv_m, _, _ = _in_proj(meta_tokens, g_mix, w_in_bf, N_META, pitched=False)

    ssm_par = _ssm_params(ssm_a_re[0], ssm_a_im[0], ssm_log_dt[0], ssm_b_re[0], ssm_b_im[0],
                          ssm_c_re[0], ssm_c_im[0], ssm_d[0], SSM_CHUNK)
    y_ssm = _ssm(u, u_m, *ssm_par, batch=bsz, chunk=SSM_CHUNK)

    attn = _attention(attn_sinks[0], q, kv, kv_m, seq // WINDOW)

    h1, hf, route, counts = _mix(
        x2, y_ssm, attn, gs, ga, w_ssm_glu[0].astype(BF16), w_attn_o[0].astype(BF16),
        w_out[0].astype(BF16), ffn_norm_g[0][None, :], router_w[0], router_b[0][None, :], tm_mix)

    tok_disp = min(1024, n)
    be, n_used, next_e, parity, dest_disp, dest_comb, pad_start, pad_len, n_rows = _routing_tables(
        route, counts, n, tok_disp, tm_comb)
    xs = _dispatch(dest_disp, pad_start, pad_len, n_used, hf, n_rows, tok_disp)
    y = _experts(be, n_used, next_e, parity, xs, w_up[0], b_up[0], w_down[0], b_down[0])
    out = _combine(dest_comb, y, h1, route, final_norm_g[None, :], tm_comb)
    return out.reshape(bsz, seq, D_MODEL)
```

```python
import functools
import math

import jax
import jax.numpy as jnp
from jax import lax
from jax.experimental import pallas as pl
from jax.experimental.pallas import tpu as pltpu

F32 = jnp.float32
BF16 = jnp.bfloat16

D_MODEL = 1024
N_META = 16
SSM_WIDTH = 512
SSM_GROUP = 16
SSM_GROUPS = 32
SSM_STATE = 64
HEAD_DIM = 64
N_HEADS = 16
N_KV_HEADS = 2
KV_REP = N_HEADS // N_KV_HEADS
WINDOW = 128
Q_WIDTH = N_HEADS * HEAD_DIM
KV_WIDTH = N_KV_HEADS * HEAD_DIM
N_EXPERTS = 32
TOP_K = 4
D_FF = 1024
SWIGLU_ALPHA = 1.702
SWIGLU_LIMIT = 7.0
RMS_EPS = 1e-5
NEG_INF = -1e30

_U0, _Q0, _KV0, _GS0, _GA0, _IN_END = 0, 512, 1536, 1792, 2816, 3840

SSM_CH_BLOCK = 128
SSM_HALF = (SSM_CH_BLOCK // SSM_GROUP) * SSM_STATE
SSM_CHUNK = 32
SSM_PITCH = 40
LOG2_E = math.log2(math.e)
EXPERT_ROWS = 512
EXPERT_SUB_ROWS = 128
VMEM_LIMIT = 56 * 1024 * 1024


def _dot(a, b):
    return jnp.dot(a, b, preferred_element_type=F32)


def _dot_nt(a, b):
    return lax.dot_general(a, b, (((1,), (1,)), ((), ())), preferred_element_type=F32)


LANES = 128
TILE_ROWS = D_MODEL // LANES


def _store_token_tiles(ref, x, start_row=0):
    rows = x.shape[0]
    for j in range(TILE_ROWS):
        ref[pl.ds(start_row * TILE_ROWS + j, rows, stride=TILE_ROWS), :] = x[:, j * LANES:(j + 1) * LANES]


def _load_token_tiles(ref, start_row, rows):
    return jnp.concatenate(
        [ref[pl.ds(start_row * TILE_ROWS + j, rows, stride=TILE_ROWS), :] for j in range(TILE_ROWS)], axis=1)


def _token_tile(ref, row):
    return ref.at[pl.ds(pl.multiple_of(row * TILE_ROWS, TILE_ROWS), TILE_ROWS), :]


def _in_proj_kernel(x_ref, g_ref, w_ref, u_ref, q_ref, kv_ref, gs_ref, ga_ref, *, pitched):
    x = x_ref[...]
    ms = jnp.mean(x * x, axis=-1, keepdims=True)
    hn = (x * lax.rsqrt(ms + RMS_EPS) * g_ref[...]).astype(BF16)
    u = _dot(hn, w_ref[:, _U0:_Q0])
    if pitched:
        for c in range(u.shape[0] // SSM_CHUNK):
            u_ref[c * SSM_PITCH:c * SSM_PITCH + SSM_CHUNK, :] = u[c * SSM_CHUNK:(c + 1) * SSM_CHUNK, :]
            u_ref[c * SSM_PITCH + SSM_CHUNK:(c + 1) * SSM_PITCH, :] = jnp.zeros(
                (SSM_PITCH - SSM_CHUNK, SSM_WIDTH), F32)
    else:
        u_ref[...] = u
    q_ref[...] = (_dot(hn, w_ref[:, _Q0:_KV0]) * (HEAD_DIM ** -0.5 * LOG2_E)).astype(BF16)
    kv_ref[...] = _dot(hn, w_ref[:, _KV0:_GS0]).astype(BF16)
    gs_ref[...] = jax.nn.sigmoid(_dot(hn, w_ref[:, _GS0:_GA0])).astype(BF16)
    ga_ref[...] = jax.nn.sigmoid(_dot(hn, w_ref[:, _GA0:_IN_END])).astype(BF16)


def _in_proj(x2, g, w_bf, tm, pitched):
    n = x2.shape[0]
    row = lambda w: pl.BlockSpec((tm, w), lambda i: (i, 0))
    full = lambda a: pl.BlockSpec(a.shape, lambda i: (0,) * a.ndim)
    u_rows = (lambda r: r // SSM_CHUNK * SSM_PITCH) if pitched else (lambda r: r)
    return pl.pallas_call(
        functools.partial(_in_proj_kernel, pitched=pitched),
        grid=(n // tm,),
        in_specs=[row(D_MODEL), full(g), full(w_bf)],
        out_specs=[pl.BlockSpec((u_rows(tm), SSM_WIDTH), lambda i: (i, 0)),
                   row(Q_WIDTH), row(2 * KV_WIDTH), row(D_MODEL), row(D_MODEL)],
        out_shape=[
            jax.ShapeDtypeStruct((u_rows(n), SSM_WIDTH), F32),
            jax.ShapeDtypeStruct((n, Q_WIDTH), BF16),
            jax.ShapeDtypeStruct((n, 2 * KV_WIDTH), BF16),
            jax.ShapeDtypeStruct((n, D_MODEL), BF16),
            jax.ShapeDtypeStruct((n, D_MODEL), BF16),
        ],
        compiler_params=pltpu.CompilerParams(
            dimension_semantics=("arbitrary",), vmem_limit_bytes=VMEM_LIMIT),
        name="in_proj",
    )(x2, g, w_bf)


def _ssm_kernel(u_ref, um_ref, bm_ref, cm_ref, ar_ref, ai_ref, atr_ref, ati_ref, d_ref,
                y_ref, sre, sim, *, chunk, rows, batch):
    h = SSM_HALF
    bm = bm_ref[0]
    cm = cm_ref[0]
    ar, ai = ar_ref[0], ai_ref[0]
    atr, ati = atr_ref[0], ati_ref[0]
    dsk = d_ref[0]
    n_chunks = rows // batch

    def advance(sr, si, bu):
        return ar * sr - ai * si + bu[:, :h], ar * si + ai * sr + bu[:, h:]

    bum = _dot(um_ref[...].astype(BF16), bm)
    mr = jnp.zeros((1, h), F32)
    mi = jnp.zeros((1, h), F32)
    for j in range(N_META):
        mr, mi = advance(mr, mi, bum[j:j + 1, :])

    def u_step(t):
        return u_ref[pl.ds(t, rows, stride=SSM_PITCH), :]

    for pad in range(chunk, SSM_PITCH):
        y_ref[pl.ds(pad, rows, stride=SSM_PITCH), :] = jnp.zeros((rows, SSM_CH_BLOCK), F32)

    sre[...] = jnp.zeros_like(sre)
    sim[...] = jnp.zeros_like(sim)

    def pass_a(t, carry):
        bu = _dot(u_step(t).astype(BF16), bm)
        nr, ni = advance(sre[...], sim[...], bu)
        sre[...] = nr
        sim[...] = ni
        return carry

    lax.fori_loop(0, chunk, pass_a, 0)

    def over_chunks(c, carry):
        new = []
        for b in range(batch):
            cr, ci = carry[2 * b], carry[2 * b + 1]
            row = pl.ds(b * n_chunks + c, 1)
            er, ei = sre[row, :], sim[row, :]
            sre[row, :] = cr
            sim[row, :] = ci
            new += [atr * cr - ati * ci + er, atr * ci + ati * cr + ei]
        return tuple(new)

    lax.fori_loop(0, n_chunks, over_chunks, (mr, mi) * batch)

    def pass_b(t, carry):
        ut = u_step(t)
        bu = _dot(ut.astype(BF16), bm)
        nr, ni = advance(sre[...], sim[...], bu)
        sre[...] = nr
        sim[...] = ni
        y = _dot(nr.astype(BF16), cm[:h, :]) + _dot(ni.astype(BF16), cm[h:, :]) + dsk * ut
        y_ref[pl.ds(t, rows, stride=SSM_PITCH), :] = y
        return carry

    lax.fori_loop(0, chunk, pass_b, 0)


def _ssm(u, u_meta, bmat, cmat, a_re, a_im, at_re, at_im, dskip, batch, chunk):
    n = u.shape[0]
    rows = n // SSM_PITCH
    nblk = SSM_WIDTH // SSM_CH_BLOCK
    col = lambda r: pl.BlockSpec((r, SSM_CH_BLOCK), lambda j: (0, j))
    par = lambda a: pl.BlockSpec((1,) + a.shape[1:], lambda j: (j, 0, 0))
    return pl.pallas_call(
        functools.partial(_ssm_kernel, chunk=chunk, rows=rows, batch=batch),
        grid=(nblk,),
        in_specs=[col(n), col(N_META), par(bmat), par(cmat), par(a_re), par(a_im),
                  par(at_re), par(at_im), par(dskip)],
        out_specs=col(n),
        out_shape=jax.ShapeDtypeStruct((n, SSM_WIDTH), F32),
        scratch_shapes=[pltpu.VMEM((rows, SSM_HALF), F32), pltpu.VMEM((rows, SSM_HALF), F32)],
        compiler_params=pltpu.CompilerParams(
            dimension_semantics=("arbitrary",), vmem_limit_bytes=VMEM_LIMIT),
        name="ssm",
    )(u, u_meta, bmat, cmat, a_re, a_im, at_re, at_im, dskip)


def _ssm_params(a_re, a_im, log_dt, b_re, b_im, c_re, c_im, d_skip, chunk):
    dt = jnp.exp(log_dt)[:, None]
    mag = jnp.exp(a_re * dt)
    ang = a_im * dt
    abar_re, abar_im = mag * jnp.cos(ang), mag * jnp.sin(ang)
    den = a_re * a_re + a_im * a_im
    nr, ni = abar_re - 1.0, abar_im
    coef_re = ((nr * a_re + ni * a_im) / den)[..., None]
    coef_im = ((ni * a_re - nr * a_im) / den)[..., None]
    bbar_re = coef_re * b_re - coef_im * b_im
    bbar_im = coef_re * b_im + coef_im * b_re
    magt = jnp.exp(a_re * dt * chunk)
    at_re, at_im = magt * jnp.cos(ang * chunk), magt * jnp.sin(ang * chunk)

    nblk = SSM_WIDTH // SSM_CH_BLOCK
    gpb = SSM_GROUPS // nblk
    eye = jnp.eye(gpb, dtype=F32)

    def in_map(b):
        b = b.reshape(nblk, gpb, SSM_STATE, SSM_GROUP)
        return jnp.einsum('jgpc,gh->jgchp', b, eye).reshape(nblk, SSM_CH_BLOCK, gpb * SSM_STATE)

    def out_map(c):
        c = c.reshape(nblk, gpb, SSM_GROUP, SSM_STATE)
        return jnp.einsum('jgcp,gh->jgphc', c, eye).reshape(nblk, gpb * SSM_STATE, SSM_CH_BLOCK)

    bmat = jnp.concatenate([in_map(bbar_re), in_map(bbar_im)], axis=2).astype(BF16)
    cmat = jnp.concatenate([out_map(c_re), -out_map(c_im)], axis=1).astype(BF16)
    vec = lambda v: v.reshape(nblk, 1, SSM_HALF)
    return (bmat, cmat, vec(abar_re), vec(abar_im), vec(at_re), vec(at_im),
            d_skip.reshape(nblk, 1, SSM_CH_BLOCK))


def _attn_kernel(sink_ref, q_ref, kvc_ref, kvp_ref, kvm_ref, o_ref, *, blocks_per_seq):
    n = pl.program_id(0) % blocks_per_seq
    w = WINDOW
    hd = HEAD_DIM
    qi = lax.broadcasted_iota(jnp.int32, (w, w), 0)
    lane = lax.broadcasted_iota(jnp.int32, (w, w), 1)
    vis_prev = (lane > qi) & (n > 0)
    vis_cur = lane <= qi
    left = lane < hd
    meta_l = lane < N_META
    meta_r = (lane >= N_META) & (lane < 2 * N_META)

    def placed(x_bf):
        x = x_bf.astype(F32)
        xr = pltpu.roll(x, hd, 1)
        lm = lax.broadcasted_iota(jnp.int32, x.shape, 1) < hd
        z = jnp.zeros_like(x)
        return {(0, 0): jnp.where(lm, x, z), (0, 1): jnp.where(lm, z, xr),
                (1, 0): jnp.where(lm, xr, z), (1, 1): jnp.where(lm, z, x)}

    kp, kc, km = placed(kvp_ref[:, :KV_WIDTH]), placed(kvc_ref[:, :KV_WIDTH]), placed(kvm_ref[:, :KV_WIDTH])
    vp, vc, vm = placed(kvp_ref[:, KV_WIDTH:]), placed(kvc_ref[:, KV_WIDTH:]), placed(kvm_ref[:, KV_WIDTH:])
    pad_rows = w - 2 * N_META
    zpad = jnp.zeros((pad_rows, w), F32)
    krow = lax.broadcasted_iota(jnp.int32, (5 * w, w), 0)
    klane = lax.broadcasted_iota(jnp.int32, (5 * w, w), 1)
    row_l = (krow < 2 * w) | ((krow >= 4 * w) & (krow < 4 * w + N_META))
    row_r = ((krow >= 2 * w) & (krow < 4 * w)) | ((krow >= 4 * w + N_META) & (krow < 4 * w + 2 * N_META))
    den_cols = jnp.where((row_l & (klane < hd)) | (row_r & (klane >= hd)), 1.0, 0.0)

    for j in range(N_KV_HEADS):
        kcat = jnp.concatenate(
            [kp[j, 0], kc[j, 0], kp[j, 1], kc[j, 1], km[j, 0], km[j, 1], zpad], axis=0).astype(BF16)
        vcat = jnp.concatenate(
            [jnp.concatenate([vp[j, 0], vc[j, 0], vp[j, 1], vc[j, 1], vm[j, 0], vm[j, 1], zpad], axis=0),
             den_cols], axis=1).astype(BF16)
        for r in range(KV_REP // 2):
            pr = j * (KV_REP // 2) + r
            s = _dot_nt(q_ref[:, pr * w:(pr + 1) * w], kcat)
            s_l = jnp.where(vis_cur, s[:, w:2 * w], jnp.where(vis_prev, s[:, 0:w], NEG_INF))
            s_r = jnp.where(vis_cur, s[:, 3 * w:4 * w], jnp.where(vis_prev, s[:, 2 * w:3 * w], NEG_INF))
            s_m = s[:, 4 * w:]
            sink_l, sink_r = sink_ref[2 * pr] * LOG2_E, sink_ref[2 * pr + 1] * LOG2_E
            m_l = jnp.maximum(jnp.max(jnp.maximum(s_l, jnp.where(meta_l, s_m, NEG_INF)),
                                      axis=1, keepdims=True), sink_l)
            m_r = jnp.maximum(jnp.max(jnp.maximum(s_r, jnp.where(meta_r, s_m, NEG_INF)),
                                      axis=1, keepdims=True), sink_r)
            s_m = jnp.where(meta_l, s_m - m_l, jnp.where(meta_r, s_m - m_r, NEG_INF))
            e_l, e_r = jnp.exp2(s_l - m_l), jnp.exp2(s_r - m_r)
            e = jnp.concatenate([jnp.where(vis_cur, 0.0, e_l), jnp.where(vis_cur, e_l, 0.0),
                                 jnp.where(vis_cur, 0.0, e_r), jnp.where(vis_cur, e_r, 0.0),
                                 jnp.exp2(s_m)], axis=1).astype(BF16)
            acc = _dot(e, vcat)
            den = acc[:, w:] + jnp.where(left, jnp.exp2(sink_l - m_l), jnp.exp2(sink_r - m_r))
            o_ref[:, pr * w:(pr + 1) * w] = (acc[:, :w] / den).astype(BF16)


def _attention(sinks, q, kv, kv_meta, blocks_per_seq):
    n = q.shape[0]
    return pl.pallas_call(
        functools.partial(_attn_kernel, blocks_per_seq=blocks_per_seq),
        grid=(n // WINDOW,),
        in_specs=[
            pl.BlockSpec(memory_space=pltpu.SMEM),
            pl.BlockSpec((WINDOW, Q_WIDTH), lambda g: (g, 0)),
            pl.BlockSpec((WINDOW, 2 * KV_WIDTH), lambda g: (g, 0)),
            pl.BlockSpec((WINDOW, 2 * KV_WIDTH), lambda g: (jnp.maximum(g - 1, 0), 0)),
            pl.BlockSpec((N_META, 2 * KV_WIDTH), lambda g: (0, 0)),
        ],
        out_specs=pl.BlockSpec((WINDOW, Q_WIDTH), lambda g: (g, 0)),
        out_shape=jax.ShapeDtypeStruct((n, Q_WIDTH), BF16),
        compiler_params=pltpu.CompilerParams(dimension_semantics=("arbitrary",)),
        name="attn",
    )(sinks, q, kv, kv, kv_meta)


_ROUTE_IDX, _ROUTE_RANK, _ROUTE_GATE, _ROUTE_ROWS = 0, TOP_K, 2 * TOP_K, 16


def _mix_kernel(x_ref, y_ref, at_ref, gs_ref, ga_ref, wglu_ref, wo_ref, wout_ref, fg_ref, rwh_ref, rwl_ref, rb_ref,
                h1_ref, hf_ref, route_ref, cnt_ref, cnt_scr):
    tm = x_ref.shape[0]

    @pl.when(pl.program_id(0) == 0)
    def _():
        cnt_scr[...] = jnp.zeros_like(cnt_scr)

    y_ssm = jnp.concatenate([y_ref[c * SSM_PITCH:c * SSM_PITCH + SSM_CHUNK, :]
                             for c in range(tm // SSM_CHUNK)], axis=0)
    glu = _dot(jax.nn.gelu(y_ssm).astype(BF16), wglu_ref[...])
    branch_ssm = glu[:, :D_MODEL] * jax.nn.sigmoid(glu[:, D_MODEL:])
    branch_attn = _dot(at_ref[...], wo_ref[...])
    merged = gs_ref[...].astype(F32) * branch_ssm + ga_ref[...].astype(F32) * branch_attn
    h1 = x_ref[...] + _dot(merged.astype(BF16), wout_ref[...])
    h1_ref[...] = h1
    ms = jnp.mean(h1 * h1, axis=-1, keepdims=True)
    hf = h1 * lax.rsqrt(ms + RMS_EPS) * fg_ref[...]
    _store_token_tiles(hf_ref, hf)

    hf_hi = hf.astype(BF16)
    hf_lo = (hf - hf_hi.astype(F32)).astype(BF16)
    logits = (_dot(hf_hi, rwh_ref[...]) + (_dot(hf_hi, rwl_ref[...]) + _dot(hf_lo, rwh_ref[...]))
              + rb_ref[...])
    lt = logits.T[:N_EXPERTS, :]
    erow = lax.broadcasted_iota(jnp.int32, (N_EXPERTS, tm), 0)
    vals, idxs, hots = [], [], []
    rest = lt
    for _ in range(TOP_K):
        m = jnp.max(rest, axis=0, keepdims=True)
        first = jnp.min(jnp.where(rest == m, erow, N_EXPERTS), axis=0, keepdims=True)
        hot = erow == first
        vals.append(m)
        idxs.append(first)
        hots.append(hot)
        rest = jnp.where(hot, -jnp.inf, rest)
    exps = [jnp.exp(v - vals[0]) for v in vals]
    tot = exps[0] + exps[1] + exps[2] + exps[3]

    sel = (hots[0] | hots[1] | hots[2] | hots[3]).astype(F32)
    ti = lax.broadcasted_iota(jnp.int32, (tm, tm), 0)
    tj = lax.broadcasted_iota(jnp.int32, (tm, tm), 1)
    earlier = (ti < tj).astype(BF16)
    rank_e = _dot(sel.astype(BF16), earlier) + cnt_scr[...]
    cnt_scr[...] = cnt_scr[...] + jnp.sum(sel, axis=1, keepdims=True)
    cnt_ref[...] = cnt_scr[...]

    rrow = lax.broadcasted_iota(jnp.int32, (_ROUTE_ROWS, tm), 0)
    route = jnp.zeros((_ROUTE_ROWS, tm), F32)
    for k in range(TOP_K):
        rank_k = jnp.sum(jnp.where(hots[k], rank_e, 0.0), axis=0, keepdims=True)
        route = jnp.where(rrow == _ROUTE_IDX + k, idxs[k].astype(F32), route)
        route = jnp.where(rrow == _ROUTE_RANK + k, rank_k, route)
        route = jnp.where(rrow == _ROUTE_GATE + k, exps[k] / tot, route)
    route_ref[...] = route


def _mix(x2, y, attn, gs, ga, wglu, wo, wout, fg, rw, rb, tm):
    n = x2.shape[0]
    rw = jnp.pad(rw, ((0, 0), (0, LANES - N_EXPERTS)))
    rb = jnp.pad(rb, ((0, 0), (0, LANES - N_EXPERTS)), constant_values=NEG_INF)
    rw_hi = rw.astype(BF16)
    rw_lo = (rw - rw_hi.astype(F32)).astype(BF16)
    row = lambda w: pl.BlockSpec((tm, w), lambda i: (i, 0))
    full = lambda a: pl.BlockSpec(a.shape, lambda i: (0,) * a.ndim)
    return pl.pallas_call(
        _mix_kernel,
        grid=(n // tm,),
        in_specs=[row(D_MODEL), pl.BlockSpec((tm // SSM_CHUNK * SSM_PITCH, SSM_WIDTH), lambda i: (i, 0)),
                  row(Q_WIDTH), row(D_MODEL), row(D_MODEL),
                  full(wglu), full(wo), full(wout), full(fg), full(rw_hi), full(rw_lo), full(rb)],
        out_specs=[row(D_MODEL), pl.BlockSpec((tm * TILE_ROWS, LANES), lambda i: (i, 0)),
                   pl.BlockSpec((_ROUTE_ROWS, tm), lambda i: (0, i)),
                   pl.BlockSpec((N_EXPERTS, 1), lambda i: (0, 0))],
        out_shape=[
            jax.ShapeDtypeStruct((n, D_MODEL), F32),
            jax.ShapeDtypeStruct((n * TILE_ROWS, LANES), F32),
            jax.ShapeDtypeStruct((_ROUTE_ROWS, n), F32),
            jax.ShapeDtypeStruct((N_EXPERTS, 1), F32),
        ],
        scratch_shapes=[pltpu.VMEM((N_EXPERTS, 1), F32)],
        compiler_params=pltpu.CompilerParams(
            dimension_semantics=("arbitrary",), vmem_limit_bytes=VMEM_LIMIT),
        name="mix_router",
    )(x2, y, attn, gs, ga, wglu, wo, wout, fg, rw_hi, rw_lo, rb)


def _tiles_wait_copy(src_hbm, dst, n_tiles, sem):
    rows = n_tiles * TILE_ROWS
    return pltpu.make_async_copy(src_hbm.at[pl.ds(0, rows), :], dst.at[pl.ds(0, rows), :], sem)


_ISSUE_UNROLL = 16


def _dispatch_kernel(dst_ref, pad_start_ref, pad_len_ref, nu_ref, hf_ref, xs_hbm, zero_blk, sem, pad_sem):
    i = pl.program_id(0)
    last = pl.num_programs(0) - 1
    pairs = dst_ref.shape[0]
    n_blocks = xs_hbm.shape[0] // (EXPERT_ROWS * TILE_ROWS)

    def zero_copy(row, rows):
        src = zero_blk.at[pl.ds(0, rows * TILE_ROWS), :]
        dst = xs_hbm.at[pl.ds(pl.multiple_of(row * TILE_ROWS, TILE_ROWS), rows * TILE_ROWS), :]
        return pltpu.make_async_copy(src, dst, pad_sem)

    def for_each_pad(fn):
        def per_expert(e, carry):
            row, left = pad_start_ref[e], pad_len_ref[e]
            size = EXPERT_ROWS // 2
            while size >= 1:
                take = left & size

                @pl.when(take != 0)
                def _(row=row, size=size):
                    fn(zero_copy(row, size))

                row = row + take
                size //= 2
            return carry
        lax.fori_loop(0, N_EXPERTS, per_expert, 0)

        def per_block(b, carry):
            fn(zero_copy(b * EXPERT_ROWS, EXPERT_ROWS))
            return carry
        lax.fori_loop(nu_ref[0], n_blocks, per_block, 0)

    @pl.when(i == 0)
    def _():
        zero_blk[...] = jnp.zeros_like(zero_blk)
        for_each_pad(lambda cp: cp.start())

    tokens = pairs // TOP_K
    for k in range(TOP_K):
        def issue(o, carry, k=k):
            tok0 = pl.multiple_of(o * _ISSUE_UNROLL, _ISSUE_UNROLL)
            dsts = [dst_ref[k * tokens + tok0 + r] for r in range(_ISSUE_UNROLL)]
            for r in range(_ISSUE_UNROLL):
                pltpu.make_async_copy(_token_tile(hf_ref, tok0 + r), _token_tile(xs_hbm, dsts[r]),
                                      sem).start(priority=r % 2)
            return carry

        lax.fori_loop(0, tokens // _ISSUE_UNROLL, issue, 0)
    for _ in range(TOP_K):
        pltpu.make_async_copy(hf_ref, xs_hbm.at[pl.ds(0, hf_ref.shape[0]), :], sem).wait()

    @pl.when(i == last)
    def _():
        for_each_pad(lambda cp: cp.wait())


def _dispatch(dest, pad_start, pad_len, n_used, hf_tiles, n_rows, tokens_per_step):
    n = dest.shape[0] // TOP_K
    pairs = tokens_per_step * TOP_K
    assert n % tokens_per_step == 0 and tokens_per_step % _ISSUE_UNROLL == 0
    smem = lambda: pl.BlockSpec(memory_space=pltpu.SMEM)
    return pl.pallas_call(
        _dispatch_kernel,
        grid=(n // tokens_per_step,),
        in_specs=[pl.BlockSpec((pairs,), lambda i: (i,), memory_space=pltpu.SMEM), smem(), smem(), smem(),
                  pl.BlockSpec((tokens_per_step * TILE_ROWS, LANES), lambda i: (i, 0))],
        out_specs=pl.BlockSpec(memory_space=pl.ANY),
        out_shape=jax.ShapeDtypeStruct((n_rows * TILE_ROWS, LANES), F32),
        scratch_shapes=[pltpu.VMEM((EXPERT_ROWS * TILE_ROWS, LANES), F32), pltpu.SemaphoreType.DMA,
                        pltpu.SemaphoreType.DMA],
        compiler_params=pltpu.CompilerParams(
            dimension_semantics=("arbitrary",), vmem_limit_bytes=VMEM_LIMIT),
        name="dispatch",
    )(dest, pad_start, pad_len, n_used, hf_tiles)


def _expert_kernel(be_ref, nu_ref, nxt_ref, par_ref, val_ref, xs_ref, wu_hbm, bu_ref, wd_hbm, bd_ref, y_ref,
                   wu_f32, wd_f32, wu_bf, wd_bf, wsem):
    i = pl.program_id(0)

    def mlp(xb):
        up = _dot(xb, wu_bf[...]) + bu_ref[0]
        x_glu = jnp.minimum(up[:, :D_FF], SWIGLU_LIMIT)
        x_lin = jnp.clip(up[:, D_FF:], -SWIGLU_LIMIT, SWIGLU_LIMIT)
        act = x_glu * jax.nn.sigmoid(SWIGLU_ALPHA * x_glu) * (x_lin + 1.0)
        return _dot(act.astype(BF16), wd_bf[...]) + bd_ref[0]

    def weight_copies(expert, s):
        return (pltpu.make_async_copy(wu_hbm.at[expert], wu_f32.at[s], wsem.at[0, s]),
                pltpu.make_async_copy(wd_hbm.at[expert], wd_f32.at[s], wsem.at[1, s]))

    @pl.when(i == 0)
    def _():
        for cp in weight_copies(be_ref[0], par_ref[0]):
            cp.start()

    @pl.when((i == 0) | (be_ref[i] != be_ref[jnp.maximum(i - 1, 0)]))
    def _():
        s = par_ref[i]
        for cp in weight_copies(be_ref[i], s):
            cp.wait()

        @pl.when(nxt_ref[i] >= 0)
        def _():
            for cp in weight_copies(nxt_ref[i], 1 - s):
                cp.start()

        wu_bf[...] = wu_f32[s].astype(BF16)
        wd_bf[...] = wd_f32[s].astype(BF16)

    used = i < nu_ref[0]
    full = val_ref[i] == EXPERT_ROWS

    @pl.when(used & full)
    def _():
        _store_token_tiles(y_ref, mlp(_load_token_tiles(xs_ref, 0, EXPERT_ROWS).astype(BF16)))

    @pl.when(used & jnp.logical_not(full))
    def _():
        sub = EXPERT_SUB_ROWS
        n_sub = lax.shift_right_logical(val_ref[i] + (sub - 1), sub.bit_length() - 1)

        def compute(j, carry):
            row0 = pl.multiple_of(j * sub, sub)
            _store_token_tiles(y_ref, mlp(_load_token_tiles(xs_ref, row0, sub).astype(BF16)), row0)
            return carry

        def clear(j, carry):
            row0 = pl.multiple_of(j * sub * TILE_ROWS, sub * TILE_ROWS)
            y_ref[pl.ds(row0, sub * TILE_ROWS), :] = jnp.zeros((sub * TILE_ROWS, LANES), F32)
            return carry

        lax.fori_loop(0, n_sub, compute, 0)
        lax.fori_loop(n_sub, EXPERT_ROWS // sub, clear, 0)

    @pl.when(jnp.logical_not(used))
    def _():
        y_ref[...] = jnp.zeros_like(y_ref)


def _experts(block_expert, n_used, next_expert, parity, valid_rows, xs_tiles, w_up, b_up, w_down, b_down):
    n_blocks = block_expert.shape[0]
    blk = (EXPERT_ROWS * TILE_ROWS, LANES)
    grid_spec = pltpu.PrefetchScalarGridSpec(
        num_scalar_prefetch=5,
        grid=(n_blocks,),
        in_specs=[
            pl.BlockSpec(blk, lambda i, be, nu, *_: (jnp.minimum(i, nu[0] - 1), 0)),
            pl.BlockSpec(memory_space=pl.ANY),
            pl.BlockSpec((1, 1, 2 * D_FF), lambda i, be, *_: (be[i], 0, 0)),
            pl.BlockSpec(memory_space=pl.ANY),
            pl.BlockSpec((1, 1, D_MODEL), lambda i, be, *_: (be[i], 0, 0)),
        ],
        out_specs=pl.BlockSpec(blk, lambda i, *_: (i, 0)),
        scratch_shapes=[
            pltpu.VMEM((2, D_MODEL, 2 * D_FF), F32),
            pltpu.VMEM((2, D_FF, D_MODEL), F32),
            pltpu.VMEM((D_MODEL, 2 * D_FF), BF16),
            pltpu.VMEM((D_FF, D_MODEL), BF16),
            pltpu.SemaphoreType.DMA((2, 2)),
        ],
    )
    return pl.pallas_call(
        _expert_kernel,
        grid_spec=grid_spec,
        out_shape=jax.ShapeDtypeStruct((n_blocks * blk[0], LANES), F32),
        compiler_params=pltpu.CompilerParams(
            dimension_semantics=("arbitrary",), vmem_limit_bytes=VMEM_LIMIT),
        name="experts",
    )(block_expert, n_used, next_expert, parity, valid_rows, xs_tiles,
      w_up, b_up[:, None, :], w_down, b_down[:, None, :])


_COMBINE_RING = 3


def _combine_kernel(dst0_ref, dst1_ref, dst2_ref, y_hbm, h1_ref, route_ref, g_ref, o_ref, *scratch):
    bufs, sem = scratch[:_COMBINE_RING], scratch[_COMBINE_RING]
    i = pl.program_id(0)
    last = pl.num_programs(0) - 1
    tm = h1_ref.shape[0]
    rows = TOP_K * tm

    def gather_group(idx_ref, s, row0):
        srcs = [idx_ref[row0 + r] for r in range(_ISSUE_UNROLL)]
        for r in range(_ISSUE_UNROLL):
            pltpu.make_async_copy(_token_tile(y_hbm, srcs[r]), _token_tile(bufs[s], row0 + r),
                                  sem.at[s]).start(priority=r % 2)

    @pl.when(i == 0)
    def _():
        for s, idx_ref in ((0, dst0_ref), (1, dst1_ref)):
            def body(o, carry, s=s, idx_ref=idx_ref):
                gather_group(idx_ref, s, pl.multiple_of(o * _ISSUE_UNROLL, _ISSUE_UNROLL))
                return carry
            lax.fori_loop(0, rows // _ISSUE_UNROLL, body, 0)

    def step(s):
        cur = bufs[s]
        ahead = (s + 2) % _COMBINE_RING
        _tiles_wait_copy(y_hbm, cur, rows, sem.at[s]).wait()
        for g in range(rows // _ISSUE_UNROLL):
            gather_group(dst2_ref, ahead, g * _ISSUE_UNROLL)
        assert tm == LANES
        rt = jnp.concatenate([route_ref[...], jnp.zeros((LANES - _ROUTE_ROWS, tm), F32)], axis=0).T
        acc = h1_ref[...]
        for k in range(TOP_K):
            gate = rt[:, _ROUTE_GATE + k:_ROUTE_GATE + k + 1]
            acc = acc + gate * _load_token_tiles(cur, k * tm, tm)
        ms = jnp.mean(acc * acc, axis=-1, keepdims=True)
        o_ref[...] = acc * lax.rsqrt(ms + RMS_EPS) * g_ref[...]

        @pl.when(i == last)
        def _():
            for t in ((s + 1) % _COMBINE_RING, ahead):
                _tiles_wait_copy(y_hbm, bufs[t], rows, sem.at[t]).wait()

    for s in range(_COMBINE_RING):
        pl.when(i % _COMBINE_RING == s)(functools.partial(step, s))


def _combine(dest_kmajor, y, h1, route, g, tm):
    n = h1.shape[0]
    n_tiles = n // tm
    idx_spec = lambda ahead: pl.BlockSpec(
        (TOP_K * tm,), lambda i: (jnp.minimum(i + ahead, n_tiles - 1),), memory_space=pltpu.SMEM)
    ring_buf = pltpu.VMEM((TOP_K * tm * TILE_ROWS, LANES), F32)
    return pl.pallas_call(
        _combine_kernel,
        grid=(n_tiles,),
        in_specs=[
            idx_spec(0), idx_spec(1), idx_spec(2),
            pl.BlockSpec(memory_space=pl.ANY),
            pl.BlockSpec((tm, D_MODEL), lambda i: (i, 0)),
            pl.BlockSpec((_ROUTE_ROWS, tm), lambda i: (0, i)),
            pl.BlockSpec((1, D_MODEL), lambda i: (0, 0)),
        ],
        out_specs=pl.BlockSpec((tm, D_MODEL), lambda i: (i, 0)),
        out_shape=jax.ShapeDtypeStruct((n, D_MODEL), F32),
        scratch_shapes=[ring_buf] * _COMBINE_RING + [pltpu.SemaphoreType.DMA((_COMBINE_RING,))],
        compiler_params=pltpu.CompilerParams(
            dimension_semantics=("arbitrary",), vmem_limit_bytes=VMEM_LIMIT),
        name="combine",
    )(dest_kmajor, dest_kmajor, dest_kmajor, y, h1, route, g)


def _routing_tables(route, counts, n, tokens_dispatch, tokens_combine):
    tm = EXPERT_ROWS
    i32 = jnp.int32
    n_blocks = (n * TOP_K + N_EXPERTS * (tm - 1)) // tm
    idx = route[_ROUTE_IDX:_ROUTE_IDX + TOP_K].astype(i32)
    rank = route[_ROUTE_RANK:_ROUTE_RANK + TOP_K].astype(i32)
    cnt = counts[:, 0].astype(i32)
    eid = jnp.arange(N_EXPERTS, dtype=i32)
    upto = eid[None, :] <= eid[:, None]
    blocks_e = (cnt + tm - 1) // tm
    blocks_end = jnp.sum(jnp.where(upto, blocks_e[None, :], 0), axis=1)
    row_start = (blocks_end - blocks_e) * tm
    n_used = blocks_end[N_EXPERTS - 1]
    used = blocks_e > 0

    def lookup(table, keys):
        hit = keys[None] == eid.reshape((N_EXPERTS,) + (1,) * keys.ndim)
        return jnp.sum(jnp.where(hit, table.reshape((N_EXPERTS,) + (1,) * keys.ndim), 0), axis=0)

    dest = lookup(row_start, idx) + rank
    blk = jnp.arange(n_blocks, dtype=i32)
    last_used = jnp.max(jnp.where(used, eid, 0))
    be = jnp.where(blk < n_used, jnp.sum((blocks_end[None, :] <= blk[:, None]).astype(i32), axis=1), last_used)
    later_used = used[None, :] & (eid[None, :] > eid[:, None])
    after = jnp.min(jnp.where(later_used, eid[None, :], N_EXPERTS), axis=1)
    next_e = jnp.where(after < N_EXPERTS, after, -1)
    parity_e = (jnp.sum((upto & used[None, :]).astype(i32), axis=1) - 1) % 2

    def k_major(tokens):
        return dest.reshape(TOP_K, n // tokens, tokens).transpose(1, 0, 2).reshape(-1)

    valid = jnp.clip(lookup(row_start + cnt, be) - blk * tm, 0, tm)
    return (be.astype(i32), n_used.reshape(1), lookup(next_e, be).astype(i32), lookup(parity_e, be).astype(i32),
            valid.astype(i32), k_major(tokens_dispatch), k_major(tokens_combine), row_start + cnt,
            blocks_e * tm - cnt, n_blocks * tm)


def kernel(x, meta_tokens, mix_norm_g, w_in, ssm_a_re, ssm_a_im, ssm_log_dt, ssm_b_re, ssm_b_im,
           ssm_c_re, ssm_c_im, ssm_d, w_ssm_glu, attn_sinks, w_attn_o, w_out, ffn_norm_g,
           router_w, router_b, w_up, b_up, w_down, b_down, final_norm_g):
    bsz, seq, d = x.shape
    assert d == D_MODEL and seq % max(WINDOW, SSM_CHUNK) == 0
    assert mix_norm_g.shape[0] == 1, "single-layer trunk"
    n = bsz * seq
    tm_proj = min(512, n)
    tm_mix = min(256, n)
    tm_comb = min(128, n)
    x2 = x.reshape(n, D_MODEL)

    w_in_bf = w_in[0].astype(BF16)
    g_mix = mix_norm_g[0][None, :]
    u, q, kv, gs, ga = _in_proj(x2, g_mix, w_in_bf, tm_proj, pitched=True)
    u_m, _, kv_m, _, _ = _in_proj(meta_tokens, g_mix, w_in_bf, N_META, pitched=False)

    ssm_par = _ssm_params(ssm_a_re[0], ssm_a_im[0], ssm_log_dt[0], ssm_b_re[0], ssm_b_im[0],
                          ssm_c_re[0], ssm_c_im[0], ssm_d[0], SSM_CHUNK)
    y_ssm = _ssm(u, u_m, *ssm_par, batch=bsz, chunk=SSM_CHUNK)

    attn = _attention(attn_sinks[0], q, kv, kv_m, seq // WINDOW)

    h1, hf, route, counts = _mix(
        x2, y_ssm, attn, gs, ga, w_ssm_glu[0].astype(BF16), w_attn_o[0].astype(BF16),
        w_out[0].astype(BF16), ffn_norm_g[0][None, :], router_w[0], router_b[0][None, :], tm_mix)

    tok_disp = min(1024, n)
    be, n_used, next_e, parity, valid, dest_disp, dest_comb, pad_start, pad_len, n_rows = _routing_tables(
        route, counts, n, tok_disp, tm_comb)
    xs = _dispatch(dest_disp, pad_start, pad_len, n_used, hf, n_rows, tok_disp)
    y = _experts(be, n_used, next_e, parity, valid, xs, w_up[0], b_up[0], w_down[0], b_down[0])
    out = _combine(dest_comb, y, h1, route, final_norm_g[None, :], tm_comb)
    return out.reshape(bsz, seq, D_MODEL)
```

```python
import functools
import math

import jax
import jax.numpy as jnp
from jax import lax
from jax.experimental import pallas as pl
from jax.experimental.pallas import tpu as pltpu

F32 = jnp.float32
BF16 = jnp.bfloat16

D_MODEL = 1024
N_META = 16
SSM_WIDTH = 512
SSM_GROUP = 16
SSM_GROUPS = 32
SSM_STATE = 64
HEAD_DIM = 64
N_HEADS = 16
N_KV_HEADS = 2
KV_REP = N_HEADS // N_KV_HEADS
WINDOW = 128
Q_WIDTH = N_HEADS * HEAD_DIM
KV_WIDTH = N_KV_HEADS * HEAD_DIM
N_EXPERTS = 32
TOP_K = 4
D_FF = 1024
SWIGLU_ALPHA = 1.702
SWIGLU_LIMIT = 7.0
RMS_EPS = 1e-5
NEG_INF = -1e30

_U0, _Q0, _KV0, _GS0, _GA0, _IN_END = 0, 512, 1536, 1792, 2816, 3840

SSM_CH_BLOCK = 128
SSM_HALF = (SSM_CH_BLOCK // SSM_GROUP) * SSM_STATE
SSM_CHUNK = 32
SSM_PITCH = 40
LOG2_E = math.log2(math.e)
EXPERT_ROWS = 512
EXPERT_SUB_ROWS = 128
VMEM_LIMIT = 56 * 1024 * 1024


def _dot(a, b):
    return jnp.dot(a, b, preferred_element_type=F32)


def _dot_nt(a, b):
    return lax.dot_general(a, b, (((1,), (1,)), ((), ())), preferred_element_type=F32)


LANES = 128
TILE_ROWS = D_MODEL // LANES


def _store_token_tiles(ref, x, start_row=0):
    rows = x.shape[0]
    for j in range(TILE_ROWS):
        ref[pl.ds(start_row * TILE_ROWS + j, rows, stride=TILE_ROWS), :] = x[:, j * LANES:(j + 1) * LANES]


def _load_token_tiles(ref, start_row, rows):
    return jnp.concatenate(
        [ref[pl.ds(start_row * TILE_ROWS + j, rows, stride=TILE_ROWS), :] for j in range(TILE_ROWS)], axis=1)


def _token_tile(ref, row):
    return ref.at[pl.ds(pl.multiple_of(row * TILE_ROWS, TILE_ROWS), TILE_ROWS), :]


def _in_proj_kernel(x_ref, g_ref, w_ref, u_ref, q_ref, kv_ref, gs_ref, ga_ref, *, pitched):
    x = x_ref[...]
    ms = jnp.mean(x * x, axis=-1, keepdims=True)
    hn = (x * lax.rsqrt(ms + RMS_EPS) * g_ref[...]).astype(BF16)
    u = _dot(hn, w_ref[:, _U0:_Q0])
    if pitched:
        for c in range(u.shape[0] // SSM_CHUNK):
            u_ref[c * SSM_PITCH:c * SSM_PITCH + SSM_CHUNK, :] = u[c * SSM_CHUNK:(c + 1) * SSM_CHUNK, :]
            u_ref[c * SSM_PITCH + SSM_CHUNK:(c + 1) * SSM_PITCH, :] = jnp.zeros(
                (SSM_PITCH - SSM_CHUNK, SSM_WIDTH), F32)
    else:
        u_ref[...] = u
    q_ref[...] = (_dot(hn, w_ref[:, _Q0:_KV0]) * (HEAD_DIM ** -0.5 * LOG2_E)).astype(BF16)
    kv_ref[...] = _dot(hn, w_ref[:, _KV0:_GS0]).astype(BF16)
    gs_ref[...] = jax.nn.sigmoid(_dot(hn, w_ref[:, _GS0:_GA0])).astype(BF16)
    ga_ref[...] = jax.nn.sigmoid(_dot(hn, w_ref[:, _GA0:_IN_END])).astype(BF16)


def _in_proj(x2, g, w_bf, tm, pitched):
    n = x2.shape[0]
    row = lambda w: pl.BlockSpec((tm, w), lambda i: (i, 0))
    full = lambda a: pl.BlockSpec(a.shape, lambda i: (0,) * a.ndim)
    u_rows = (lambda r: r // SSM_CHUNK * SSM_PITCH) if pitched else (lambda r: r)
    return pl.pallas_call(
        functools.partial(_in_proj_kernel, pitched=pitched),
        grid=(n // tm,),
        in_specs=[row(D_MODEL), full(g), full(w_bf)],
        out_specs=[pl.BlockSpec((u_rows(tm), SSM_WIDTH), lambda i: (i, 0)),
                   row(Q_WIDTH), row(2 * KV_WIDTH), row(D_MODEL), row(D_MODEL)],
        out_shape=[
            jax.ShapeDtypeStruct((u_rows(n), SSM_WIDTH), F32),
            jax.ShapeDtypeStruct((n, Q_WIDTH), BF16),
            jax.ShapeDtypeStruct((n, 2 * KV_WIDTH), BF16),
            jax.ShapeDtypeStruct((n, D_MODEL), BF16),
            jax.ShapeDtypeStruct((n, D_MODEL), BF16),
        ],
        compiler_params=pltpu.CompilerParams(
            dimension_semantics=("arbitrary",), vmem_limit_bytes=VMEM_LIMIT),
        name="in_proj",
    )(x2, g, w_bf)


def _ssm_kernel(u_ref, um_ref, bm_ref, cm_ref, ar_ref, ai_ref, atr_ref, ati_ref, d_ref,
                y_ref, sre, sim, *, chunk, rows, batch):
    h = SSM_HALF
    bm = bm_ref[0]
    cm = cm_ref[0]
    ar, ai = ar_ref[0], ai_ref[0]
    atr, ati = atr_ref[0], ati_ref[0]
    dsk = d_ref[0]
    n_chunks = rows // batch

    def advance(sr, si, bu):
        return ar * sr - ai * si + bu[:, :h], ar * si + ai * sr + bu[:, h:]

    bum = _dot(um_ref[...].astype(BF16), bm)
    mr = jnp.zeros((1, h), F32)
    mi = jnp.zeros((1, h), F32)
    for j in range(N_META):
        mr, mi = advance(mr, mi, bum[j:j + 1, :])

    def u_step(t):
        return u_ref[pl.ds(t, rows, stride=SSM_PITCH), :]

    for pad in range(chunk, SSM_PITCH):
        y_ref[pl.ds(pad, rows, stride=SSM_PITCH), :] = jnp.zeros((rows, SSM_CH_BLOCK), F32)

    sre[...] = jnp.zeros_like(sre)
    sim[...] = jnp.zeros_like(sim)

    def pass_a(t, carry):
        bu = _dot(u_step(t).astype(BF16), bm)
        nr, ni = advance(sre[...], sim[...], bu)
        sre[...] = nr
        sim[...] = ni
        return carry

    lax.fori_loop(0, chunk, pass_a, 0)

    def over_chunks(c, carry):
        new = []
        for b in range(batch):
            cr, ci = carry[2 * b], carry[2 * b + 1]
            row = pl.ds(b * n_chunks + c, 1)
            er, ei = sre[row, :], sim[row, :]
            sre[row, :] = cr
            sim[row, :] = ci
            new += [atr * cr - ati * ci + er, atr * ci + ati * cr + ei]
        return tuple(new)

    lax.fori_loop(0, n_chunks, over_chunks, (mr, mi) * batch)

    def pass_b(t, carry):
        ut = u_step(t)
        bu = _dot(ut.astype(BF16), bm)
        nr, ni = advance(sre[...], sim[...], bu)
        sre[...] = nr
        sim[...] = ni
        y = _dot(nr.astype(BF16), cm[:h, :]) + _dot(ni.astype(BF16), cm[h:, :]) + dsk * ut
        y_ref[pl.ds(t, rows, stride=SSM_PITCH), :] = y
        return carry

    lax.fori_loop(0, chunk, pass_b, 0)


def _ssm(u, u_meta, bmat, cmat, a_re, a_im, at_re, at_im, dskip, batch, chunk):
    n = u.shape[0]
    rows = n // SSM_PITCH
    nblk = SSM_WIDTH // SSM_CH_BLOCK
    col = lambda r: pl.BlockSpec((r, SSM_CH_BLOCK), lambda j: (0, j))
    par = lambda a: pl.BlockSpec((1,) + a.shape[1:], lambda j: (j, 0, 0))
    return pl.pallas_call(
        functools.partial(_ssm_kernel, chunk=chunk, rows=rows, batch=batch),
        grid=(nblk,),
        in_specs=[col(n), col(N_META), par(bmat), par(cmat), par(a_re), par(a_im),
                  par(at_re), par(at_im), par(dskip)],
        out_specs=col(n),
        out_shape=jax.ShapeDtypeStruct((n, SSM_WIDTH), F32),
        scratch_shapes=[pltpu.VMEM((rows, SSM_HALF), F32), pltpu.VMEM((rows, SSM_HALF), F32)],
        compiler_params=pltpu.CompilerParams(
            dimension_semantics=("arbitrary",), vmem_limit_bytes=VMEM_LIMIT),
        name="ssm",
    )(u, u_meta, bmat, cmat, a_re, a_im, at_re, at_im, dskip)


def _ssm_params(a_re, a_im, log_dt, b_re, b_im, c_re, c_im, d_skip, chunk):
    dt = jnp.exp(log_dt)[:, None]
    mag = jnp.exp(a_re * dt)
    ang = a_im * dt
    abar_re, abar_im = mag * jnp.cos(ang), mag * jnp.sin(ang)
    den = a_re * a_re + a_im * a_im
    nr, ni = abar_re - 1.0, abar_im
    coef_re = ((nr * a_re + ni * a_im) / den)[..., None]
    coef_im = ((ni * a_re - nr * a_im) / den)[..., None]
    bbar_re = coef_re * b_re - coef_im * b_im
    bbar_im = coef_re * b_im + coef_im * b_re
    magt = jnp.exp(a_re * dt * chunk)
    at_re, at_im = magt * jnp.cos(ang * chunk), magt * jnp.sin(ang * chunk)

    nblk = SSM_WIDTH // SSM_CH_BLOCK
    gpb = SSM_GROUPS // nblk
    eye = jnp.eye(gpb, dtype=F32)

    def in_map(b):
        b = b.reshape(nblk, gpb, SSM_STATE, SSM_GROUP)
        return jnp.einsum('jgpc,gh->jgchp', b, eye).reshape(nblk, SSM_CH_BLOCK, gpb * SSM_STATE)

    def out_map(c):
        c = c.reshape(nblk, gpb, SSM_GROUP, SSM_STATE)
        return jnp.einsum('jgcp,gh->jgphc', c, eye).reshape(nblk, gpb * SSM_STATE, SSM_CH_BLOCK)

    bmat = jnp.concatenate([in_map(bbar_re), in_map(bbar_im)], axis=2).astype(BF16)
    cmat = jnp.concatenate([out_map(c_re), -out_map(c_im)], axis=1).astype(BF16)
    vec = lambda v: v.reshape(nblk, 1, SSM_HALF)
    return (bmat, cmat, vec(abar_re), vec(abar_im), vec(at_re), vec(at_im),
            d_skip.reshape(nblk, 1, SSM_CH_BLOCK))


def _attn_kernel(sink_ref, q_ref, kvc_ref, kvp_ref, kvm_ref, o_ref, *, blocks_per_seq):
    n = pl.program_id(0) % blocks_per_seq
    w = WINDOW
    hd = HEAD_DIM
    qi = lax.broadcasted_iota(jnp.int32, (w, w), 0)
    lane = lax.broadcasted_iota(jnp.int32, (w, w), 1)
    vis_prev = (lane > qi) & (n > 0)
    vis_cur = lane <= qi
    left = lane < hd
    meta_l = lane < N_META
    meta_r = (lane >= N_META) & (lane < 2 * N_META)

    def placed(x_bf):
        x = x_bf.astype(F32)
        xr = pltpu.roll(x, hd, 1)
        lm = lax.broadcasted_iota(jnp.int32, x.shape, 1) < hd
        z = jnp.zeros_like(x)
        return {(0, 0): jnp.where(lm, x, z), (0, 1): jnp.where(lm, z, xr),
                (1, 0): jnp.where(lm, xr, z), (1, 1): jnp.where(lm, z, x)}

    kp, kc, km = placed(kvp_ref[:, :KV_WIDTH]), placed(kvc_ref[:, :KV_WIDTH]), placed(kvm_ref[:, :KV_WIDTH])
    vp, vc, vm = placed(kvp_ref[:, KV_WIDTH:]), placed(kvc_ref[:, KV_WIDTH:]), placed(kvm_ref[:, KV_WIDTH:])
    pad_rows = w - 2 * N_META
    zpad = jnp.zeros((pad_rows, w), F32)
    krow = lax.broadcasted_iota(jnp.int32, (5 * w, w), 0)
    klane = lax.broadcasted_iota(jnp.int32, (5 * w, w), 1)
    row_l = (krow < 2 * w) | ((krow >= 4 * w) & (krow < 4 * w + N_META))
    row_r = ((krow >= 2 * w) & (krow < 4 * w)) | ((krow >= 4 * w + N_META) & (krow < 4 * w + 2 * N_META))
    den_cols = jnp.where((row_l & (klane < hd)) | (row_r & (klane >= hd)), 1.0, 0.0)

    for j in range(N_KV_HEADS):
        kcat = jnp.concatenate(
            [kp[j, 0], kc[j, 0], kp[j, 1], kc[j, 1], km[j, 0], km[j, 1], zpad], axis=0).astype(BF16)
        vcat = jnp.concatenate(
            [jnp.concatenate([vp[j, 0], vc[j, 0], vp[j, 1], vc[j, 1], vm[j, 0], vm[j, 1], zpad], axis=0),
             den_cols], axis=1).astype(BF16)
        for r in range(KV_REP // 2):
            pr = j * (KV_REP // 2) + r
            s = _dot_nt(q_ref[:, pr * w:(pr + 1) * w], kcat)
            s_l = jnp.where(vis_cur, s[:, w:2 * w], jnp.where(vis_prev, s[:, 0:w], NEG_INF))
            s_r = jnp.where(vis_cur, s[:, 3 * w:4 * w], jnp.where(vis_prev, s[:, 2 * w:3 * w], NEG_INF))
            s_m = s[:, 4 * w:]
            sink_l, sink_r = sink_ref[2 * pr] * LOG2_E, sink_ref[2 * pr + 1] * LOG2_E
            m_l = jnp.maximum(jnp.max(jnp.maximum(s_l, jnp.where(meta_l, s_m, NEG_INF)),
                                      axis=1, keepdims=True), sink_l)
            m_r = jnp.maximum(jnp.max(jnp.maximum(s_r, jnp.where(meta_r, s_m, NEG_INF)),
                                      axis=1, keepdims=True), sink_r)
            s_m = jnp.where(meta_l, s_m - m_l, jnp.where(meta_r, s_m - m_r, NEG_INF))
            e_l, e_r = jnp.exp2(s_l - m_l), jnp.exp2(s_r - m_r)
            e = jnp.concatenate([jnp.where(vis_cur, 0.0, e_l), jnp.where(vis_cur, e_l, 0.0),
                                 jnp.where(vis_cur, 0.0, e_r), jnp.where(vis_cur, e_r, 0.0),
                                 jnp.exp2(s_m)], axis=1).astype(BF16)
            acc = _dot(e, vcat)
            den = acc[:, w:] + jnp.where(left, jnp.exp2(sink_l - m_l), jnp.exp2(sink_r - m_r))
            o_ref[:, pr * w:(pr + 1) * w] = (acc[:, :w] / den).astype(BF16)


def _attention(sinks, q, kv, kv_meta, blocks_per_seq):
    n = q.shape[0]
    return pl.pallas_call(
        functools.partial(_attn_kernel, blocks_per_seq=blocks_per_seq),
        grid=(n // WINDOW,),
        in_specs=[
            pl.BlockSpec(memory_space=pltpu.SMEM),
            pl.BlockSpec((WINDOW, Q_WIDTH), lambda g: (g, 0)),
            pl.BlockSpec((WINDOW, 2 * KV_WIDTH), lambda g: (g, 0)),
            pl.BlockSpec((WINDOW, 2 * KV_WIDTH), lambda g: (jnp.maximum(g - 1, 0), 0)),
            pl.BlockSpec((N_META, 2 * KV_WIDTH), lambda g: (0, 0)),
        ],
        out_specs=pl.BlockSpec((WINDOW, Q_WIDTH), lambda g: (g, 0)),
        out_shape=jax.ShapeDtypeStruct((n, Q_WIDTH), BF16),
        compiler_params=pltpu.CompilerParams(dimension_semantics=("arbitrary",)),
        name="attn",
    )(sinks, q, kv, kv, kv_meta)


_ROUTE_IDX, _ROUTE_RANK, _ROUTE_GATE, _ROUTE_ROWS = 0, TOP_K, 2 * TOP_K, 16


def _mix_kernel(x_ref, y_ref, at_ref, gs_ref, ga_ref, wglu_ref, wo_ref, wout_ref, fg_ref, rw_ref, rb_ref,
                h1_ref, hf_ref, route_ref, cnt_ref, cnt_scr):
    tm = x_ref.shape[0]

    @pl.when(pl.program_id(0) == 0)
    def _():
        cnt_scr[...] = jnp.zeros_like(cnt_scr)

    y_ssm = jnp.concatenate([y_ref[c * SSM_PITCH:c * SSM_PITCH + SSM_CHUNK, :]
                             for c in range(tm // SSM_CHUNK)], axis=0)
    glu = _dot(jax.nn.gelu(y_ssm).astype(BF16), wglu_ref[...])
    branch_ssm = glu[:, :D_MODEL] * jax.nn.sigmoid(glu[:, D_MODEL:])
    branch_attn = _dot(at_ref[...], wo_ref[...])
    merged = gs_ref[...].astype(F32) * branch_ssm + ga_ref[...].astype(F32) * branch_attn
    h1 = x_ref[...] + _dot(merged.astype(BF16), wout_ref[...])
    h1_ref[...] = h1
    ms = jnp.mean(h1 * h1, axis=-1, keepdims=True)
    hf = h1 * lax.rsqrt(ms + RMS_EPS) * fg_ref[...]
    _store_token_tiles(hf_ref, hf)

    hf_hi = hf.astype(BF16)
    hf_lo = (hf - hf_hi.astype(F32)).astype(BF16)
    hi_prod = _dot(hf_hi, rw_ref[...])
    logits = (hi_prod[:, :LANES] + (hi_prod[:, LANES:] + _dot(hf_lo, rw_ref[:, :LANES]))
              + rb_ref[...])
    lt = logits.T[:N_EXPERTS, :]
    erow = lax.broadcasted_iota(jnp.int32, (N_EXPERTS, tm), 0)
    vals, idxs, hots = [], [], []
    rest = lt
    for _ in range(TOP_K):
        m = jnp.max(rest, axis=0, keepdims=True)
        first = jnp.min(jnp.where(rest == m, erow, N_EXPERTS), axis=0, keepdims=True)
        hot = erow == first
        vals.append(m)
        idxs.append(first)
        hots.append(hot)
        rest = jnp.where(hot, -jnp.inf, rest)
    exps = [jnp.exp(v - vals[0]) for v in vals]
    tot = exps[0] + exps[1] + exps[2] + exps[3]

    sel = (hots[0] | hots[1] | hots[2] | hots[3]).astype(F32)
    ti = lax.broadcasted_iota(jnp.int32, (tm, tm), 0)
    tj = lax.broadcasted_iota(jnp.int32, (tm, tm), 1)
    earlier = (ti < tj).astype(BF16)
    rank_e = _dot(sel.astype(BF16), earlier) + cnt_scr[...]
    cnt_scr[...] = cnt_scr[...] + jnp.sum(sel, axis=1, keepdims=True)
    cnt_ref[...] = cnt_scr[...]

    rrow = lax.broadcasted_iota(jnp.int32, (_ROUTE_ROWS, tm), 0)
    route = jnp.zeros((_ROUTE_ROWS, tm), F32)
    for k in range(TOP_K):
        rank_k = jnp.sum(jnp.where(hots[k], rank_e, 0.0), axis=0, keepdims=True)
        route = jnp.where(rrow == _ROUTE_IDX + k, idxs[k].astype(F32), route)
        route = jnp.where(rrow == _ROUTE_RANK + k, rank_k, route)
        route = jnp.where(rrow == _ROUTE_GATE + k, exps[k] / tot, route)
    route_ref[...] = route


def _mix(x2, y, attn, gs, ga, wglu, wo, wout, fg, rw, rb, tm):
    n = x2.shape[0]
    rw = jnp.pad(rw, ((0, 0), (0, LANES - N_EXPERTS)))
    rb = jnp.pad(rb, ((0, 0), (0, LANES - N_EXPERTS)), constant_values=NEG_INF)
    rw_hi = rw.astype(BF16)
    rw_split = jnp.concatenate([rw_hi, (rw - rw_hi.astype(F32)).astype(BF16)], axis=1)
    row = lambda w: pl.BlockSpec((tm, w), lambda i: (i, 0))
    full = lambda a: pl.BlockSpec(a.shape, lambda i: (0,) * a.ndim)
    return pl.pallas_call(
        _mix_kernel,
        grid=(n // tm,),
        in_specs=[row(D_MODEL), pl.BlockSpec((tm // SSM_CHUNK * SSM_PITCH, SSM_WIDTH), lambda i: (i, 0)),
                  row(Q_WIDTH), row(D_MODEL), row(D_MODEL),
                  full(wglu), full(wo), full(wout), full(fg), full(rw_split), full(rb)],
        out_specs=[row(D_MODEL), pl.BlockSpec((tm * TILE_ROWS, LANES), lambda i: (i, 0)),
                   pl.BlockSpec((_ROUTE_ROWS, tm), lambda i: (0, i)),
                   pl.BlockSpec((N_EXPERTS, 1), lambda i: (0, 0))],
        out_shape=[
            jax.ShapeDtypeStruct((n, D_MODEL), F32),
            jax.ShapeDtypeStruct((n * TILE_ROWS, LANES), F32),
            jax.ShapeDtypeStruct((_ROUTE_ROWS, n), F32),
            jax.ShapeDtypeStruct((N_EXPERTS, 1), F32),
        ],
        scratch_shapes=[pltpu.VMEM((N_EXPERTS, 1), F32)],
        compiler_params=pltpu.CompilerParams(
            dimension_semantics=("arbitrary",), vmem_limit_bytes=VMEM_LIMIT),
        name="mix_router",
    )(x2, y, attn, gs, ga, wglu, wo, wout, fg, rw_split, rb)


def _tiles_wait_copy(src_hbm, dst, n_tiles, sem):
    rows = n_tiles * TILE_ROWS
    return pltpu.make_async_copy(src_hbm.at[pl.ds(0, rows), :], dst.at[pl.ds(0, rows), :], sem)


_ISSUE_UNROLL = 16


def _dispatch_kernel(dst_ref, pad_start_ref, pad_len_ref, nu_ref, hf_ref, xs_hbm, zero_blk, sem, pad_sem):
    i = pl.program_id(0)
    last = pl.num_programs(0) - 1
    pairs = dst_ref.shape[0]
    n_blocks = xs_hbm.shape[0] // (EXPERT_ROWS * TILE_ROWS)

    def zero_copy(row, rows):
        src = zero_blk.at[pl.ds(0, rows * TILE_ROWS), :]
        dst = xs_hbm.at[pl.ds(pl.multiple_of(row * TILE_ROWS, TILE_ROWS), rows * TILE_ROWS), :]
        return pltpu.make_async_copy(src, dst, pad_sem)

    def for_each_pad(fn):
        def per_expert(e, carry):
            row, left = pad_start_ref[e], pad_len_ref[e]
            size = EXPERT_ROWS // 2
            while size >= 1:
                take = left & size

                @pl.when(take != 0)
                def _(row=row, size=size):
                    fn(zero_copy(row, size))

                row = row + take
                size //= 2
            return carry
        lax.fori_loop(0, N_EXPERTS, per_expert, 0)

        def per_block(b, carry):
            fn(zero_copy(b * EXPERT_ROWS, EXPERT_ROWS))
            return carry
        lax.fori_loop(nu_ref[0], n_blocks, per_block, 0)

    @pl.when(i == 0)
    def _():
        zero_blk[...] = jnp.zeros_like(zero_blk)
        for_each_pad(lambda cp: cp.start())

    tokens = pairs // TOP_K
    for k in range(TOP_K):
        def issue(o, carry, k=k):
            tok0 = pl.multiple_of(o * _ISSUE_UNROLL, _ISSUE_UNROLL)
            dsts = [dst_ref[k * tokens + tok0 + r] for r in range(_ISSUE_UNROLL)]
            for r in range(_ISSUE_UNROLL):
                pltpu.make_async_copy(_token_tile(hf_ref, tok0 + r), _token_tile(xs_hbm, dsts[r]),
                                      sem).start(priority=r % 2)
            return carry

        lax.fori_loop(0, tokens // _ISSUE_UNROLL, issue, 0)
    for _ in range(TOP_K):
        pltpu.make_async_copy(hf_ref, xs_hbm.at[pl.ds(0, hf_ref.shape[0]), :], sem).wait()

    @pl.when(i == last)
    def _():
        for_each_pad(lambda cp: cp.wait())


def _dispatch(dest, pad_start, pad_len, n_used, hf_tiles, n_rows, tokens_per_step):
    n = dest.shape[0] // TOP_K
    pairs = tokens_per_step * TOP_K
    assert n % tokens_per_step == 0 and tokens_per_step % _ISSUE_UNROLL == 0
    smem = lambda: pl.BlockSpec(memory_space=pltpu.SMEM)
    return pl.pallas_call(
        _dispatch_kernel,
        grid=(n // tokens_per_step,),
        in_specs=[pl.BlockSpec((pairs,), lambda i: (i,), memory_space=pltpu.SMEM), smem(), smem(), smem(),
                  pl.BlockSpec((tokens_per_step * TILE_ROWS, LANES), lambda i: (i, 0))],
        out_specs=pl.BlockSpec(memory_space=pl.ANY),
        out_shape=jax.ShapeDtypeStruct((n_rows * TILE_ROWS, LANES), F32),
        scratch_shapes=[pltpu.VMEM((EXPERT_ROWS * TILE_ROWS, LANES), F32), pltpu.SemaphoreType.DMA,
                        pltpu.SemaphoreType.DMA],
        compiler_params=pltpu.CompilerParams(
            dimension_semantics=("arbitrary",), vmem_limit_bytes=VMEM_LIMIT),
        name="dispatch",
    )(dest, pad_start, pad_len, n_used, hf_tiles)


def _expert_kernel(be_ref, nu_ref, nxt_ref, par_ref, val_ref, xs_ref, wu_hbm, bu_ref, wd_hbm, bd_ref, y_ref,
                   wu_f32, wd_f32, wu_bf, wd_bf, wsem):
    i = pl.program_id(0)

    def mlp(xb):
        up = _dot(xb, wu_bf[...]) + bu_ref[0]
        x_glu = jnp.minimum(up[:, :D_FF], SWIGLU_LIMIT)
        x_lin = jnp.clip(up[:, D_FF:], -SWIGLU_LIMIT, SWIGLU_LIMIT)
        act = x_glu * jax.nn.sigmoid(SWIGLU_ALPHA * x_glu) * (x_lin + 1.0)
        return _dot(act.astype(BF16), wd_bf[...]) + bd_ref[0]

    def weight_copies(expert, s):
        return (pltpu.make_async_copy(wu_hbm.at[expert], wu_f32.at[s], wsem.at[0, s]),
                pltpu.make_async_copy(wd_hbm.at[expert], wd_f32.at[s], wsem.at[1, s]))

    @pl.when(i == 0)
    def _():
        for cp in weight_copies(be_ref[0], par_ref[0]):
            cp.start()

    @pl.when((i == 0) | (be_ref[i] != be_ref[jnp.maximum(i - 1, 0)]))
    def _():
        s = par_ref[i]
        for cp in weight_copies(be_ref[i], s):
            cp.wait()

        @pl.when(nxt_ref[i] >= 0)
        def _():
            for cp in weight_copies(nxt_ref[i], 1 - s):
                cp.start()

        wu_bf[...] = wu_f32[s].astype(BF16)
        wd_bf[...] = wd_f32[s].astype(BF16)

    used = i < nu_ref[0]
    full = val_ref[i] > EXPERT_ROWS - EXPERT_SUB_ROWS

    @pl.when(used & full)
    def _():
        _store_token_tiles(y_ref, mlp(_load_token_tiles(xs_ref, 0, EXPERT_ROWS).astype(BF16)))

    @pl.when(used & jnp.logical_not(full))
    def _():
        sub = EXPERT_SUB_ROWS
        n_sub = lax.shift_right_logical(val_ref[i] + (sub - 1), sub.bit_length() - 1)

        def compute(j, carry):
            row0 = pl.multiple_of(j * sub, sub)
            _store_token_tiles(y_ref, mlp(_load_token_tiles(xs_ref, row0, sub).astype(BF16)), row0)
            return carry

        def clear(j, carry):
            row0 = pl.multiple_of(j * sub * TILE_ROWS, sub * TILE_ROWS)
            y_ref[pl.ds(row0, sub * TILE_ROWS), :] = jnp.zeros((sub * TILE_ROWS, LANES), F32)
            return carry

        lax.fori_loop(0, n_sub, compute, 0)
        lax.fori_loop(n_sub, EXPERT_ROWS // sub, clear, 0)

    @pl.when(jnp.logical_not(used))
    def _():
        y_ref[...] = jnp.zeros_like(y_ref)


def _experts(block_expert, n_used, next_expert, parity, valid_rows, xs_tiles, w_up, b_up, w_down, b_down):
    n_blocks = block_expert.shape[0]
    blk = (EXPERT_ROWS * TILE_ROWS, LANES)
    grid_spec = pltpu.PrefetchScalarGridSpec(
        num_scalar_prefetch=5,
        grid=(n_blocks,),
        in_specs=[
            pl.BlockSpec(blk, lambda i, be, nu, *_: (jnp.minimum(i, nu[0] - 1), 0)),
            pl.BlockSpec(memory_space=pl.ANY),
            pl.BlockSpec((1, 1, 2 * D_FF), lambda i, be, *_: (be[i], 0, 0)),
            pl.BlockSpec(memory_space=pl.ANY),
            pl.BlockSpec((1, 1, D_MODEL), lambda i, be, *_: (be[i], 0, 0)),
        ],
        out_specs=pl.BlockSpec(blk, lambda i, *_: (i, 0)),
        scratch_shapes=[
            pltpu.VMEM((2, D_MODEL, 2 * D_FF), F32),
            pltpu.VMEM((2, D_FF, D_MODEL), F32),
            pltpu.VMEM((D_MODEL, 2 * D_FF), BF16),
            pltpu.VMEM((D_FF, D_MODEL), BF16),
            pltpu.SemaphoreType.DMA((2, 2)),
        ],
    )
    return pl.pallas_call(
        _expert_kernel,
        grid_spec=grid_spec,
        out_shape=jax.ShapeDtypeStruct((n_blocks * blk[0], LANES), F32),
        compiler_params=pltpu.CompilerParams(
            dimension_semantics=("arbitrary",), vmem_limit_bytes=VMEM_LIMIT),
        name="experts",
    )(block_expert, n_used, next_expert, parity, valid_rows, xs_tiles,
      w_up, b_up[:, None, :], w_down, b_down[:, None, :])


_COMBINE_RING = 3


def _combine_kernel(dst0_ref, dst1_ref, dst2_ref, y_hbm, h1_ref, route_ref, g_ref, o_ref, *scratch):
    bufs, sem = scratch[:_COMBINE_RING], scratch[_COMBINE_RING]
    i = pl.program_id(0)
    last = pl.num_programs(0) - 1
    tm = h1_ref.shape[0]
    rows = TOP_K * tm

    def gather_group(idx_ref, s, row0):
        srcs = [idx_ref[row0 + r] for r in range(_ISSUE_UNROLL)]
        for r in range(_ISSUE_UNROLL):
            pltpu.make_async_copy(_token_tile(y_hbm, srcs[r]), _token_tile(bufs[s], row0 + r),
                                  sem.at[s]).start(priority=r % 2)

    @pl.when(i == 0)
    def _():
        for s, idx_ref in ((0, dst0_ref), (1, dst1_ref)):
            def body(o, carry, s=s, idx_ref=idx_ref):
                gather_group(idx_ref, s, pl.multiple_of(o * _ISSUE_UNROLL, _ISSUE_UNROLL))
                return carry
            lax.fori_loop(0, rows // _ISSUE_UNROLL, body, 0)

    def step(s):
        cur = bufs[s]
        ahead = (s + 2) % _COMBINE_RING
        _tiles_wait_copy(y_hbm, cur, rows, sem.at[s]).wait()
        for g in range(rows // _ISSUE_UNROLL):
            gather_group(dst2_ref, ahead, g * _ISSUE_UNROLL)
        assert tm == LANES
        rt = jnp.concatenate([route_ref[...], jnp.zeros((LANES - _ROUTE_ROWS, tm), F32)], axis=0).T
        acc = h1_ref[...]
        for k in range(TOP_K):
            gate = rt[:, _ROUTE_GATE + k:_ROUTE_GATE + k + 1]
            acc = acc + gate * _load_token_tiles(cur, k * tm, tm)
        ms = jnp.mean(acc * acc, axis=-1, keepdims=True)
        o_ref[...] = acc * lax.rsqrt(ms + RMS_EPS) * g_ref[...]

        @pl.when(i == last)
        def _():
            for t in ((s + 1) % _COMBINE_RING, ahead):
                _tiles_wait_copy(y_hbm, bufs[t], rows, sem.at[t]).wait()

    for s in range(_COMBINE_RING):
        pl.when(i % _COMBINE_RING == s)(functools.partial(step, s))


def _combine(dest_kmajor, y, h1, route, g, tm):
    n = h1.shape[0]
    n_tiles = n // tm
    idx_spec = lambda ahead: pl.BlockSpec(
        (TOP_K * tm,), lambda i: (jnp.minimum(i + ahead, n_tiles - 1),), memory_space=pltpu.SMEM)
    ring_buf = pltpu.VMEM((TOP_K * tm * TILE_ROWS, LANES), F32)
    return pl.pallas_call(
        _combine_kernel,
        grid=(n_tiles,),
        in_specs=[
            idx_spec(0), idx_spec(1), idx_spec(2),
            pl.BlockSpec(memory_space=pl.ANY),
            pl.BlockSpec((tm, D_MODEL), lambda i: (i, 0)),
            pl.BlockSpec((_ROUTE_ROWS, tm), lambda i: (0, i)),
            pl.BlockSpec((1, D_MODEL), lambda i: (0, 0)),
        ],
        out_specs=pl.BlockSpec((tm, D_MODEL), lambda i: (i, 0)),
        out_shape=jax.ShapeDtypeStruct((n, D_MODEL), F32),
        scratch_shapes=[ring_buf] * _COMBINE_RING + [pltpu.SemaphoreType.DMA((_COMBINE_RING,))],
        compiler_params=pltpu.CompilerParams(
            dimension_semantics=("arbitrary",), vmem_limit_bytes=VMEM_LIMIT),
        name="combine",
    )(dest_kmajor, dest_kmajor, dest_kmajor, y, h1, route, g)


def _routing_tables(route, counts, n, tokens_dispatch, tokens_combine):
    tm = EXPERT_ROWS
    i32 = jnp.int32
    n_blocks = (n * TOP_K + N_EXPERTS * (tm - 1)) // tm
    idx = route[_ROUTE_IDX:_ROUTE_IDX + TOP_K].astype(i32)
    rank = route[_ROUTE_RANK:_ROUTE_RANK + TOP_K].astype(i32)
    cnt = counts[:, 0].astype(i32)
    eid = jnp.arange(N_EXPERTS, dtype=i32)
    upto = eid[None, :] <= eid[:, None]
    blocks_e = (cnt + tm - 1) // tm
    blocks_end = jnp.sum(jnp.where(upto, blocks_e[None, :], 0), axis=1)
    row_start = (blocks_end - blocks_e) * tm
    n_used = blocks_end[N_EXPERTS - 1]
    used = blocks_e > 0

    def lookup(table, keys):
        hit = keys[None] == eid.reshape((N_EXPERTS,) + (1,) * keys.ndim)
        return jnp.sum(jnp.where(hit, table.reshape((N_EXPERTS,) + (1,) * keys.ndim), 0), axis=0)

    dest = lookup(row_start, idx) + rank
    blk = jnp.arange(n_blocks, dtype=i32)
    last_used = jnp.max(jnp.where(used, eid, 0))
    be = jnp.where(blk < n_used, jnp.sum((blocks_end[None, :] <= blk[:, None]).astype(i32), axis=1), last_used)
    later_used = used[None, :] & (eid[None, :] > eid[:, None])
    after = jnp.min(jnp.where(later_used, eid[None, :], N_EXPERTS), axis=1)
    next_e = jnp.where(after < N_EXPERTS, after, -1)
    parity_e = (jnp.sum((upto & used[None, :]).astype(i32), axis=1) - 1) % 2

    def k_major(tokens):
        return dest.reshape(TOP_K, n // tokens, tokens).transpose(1, 0, 2).reshape(-1)

    valid = jnp.clip(lookup(row_start + cnt, be) - blk * tm, 0, tm)
    return (be.astype(i32), n_used.reshape(1), lookup(next_e, be).astype(i32), lookup(parity_e, be).astype(i32),
            valid.astype(i32), k_major(tokens_dispatch), k_major(tokens_combine), row_start + cnt,
            blocks_e * tm - cnt, n_blocks * tm)


def kernel(x, meta_tokens, mix_norm_g, w_in, ssm_a_re, ssm_a_im, ssm_log_dt, ssm_b_re, ssm_b_im,
           ssm_c_re, ssm_c_im, ssm_d, w_ssm_glu, attn_sinks, w_attn_o, w_out, ffn_norm_g,
           router_w, router_b, w_up, b_up, w_down, b_down, final_norm_g):
    bsz, seq, d = x.shape
    assert d == D_MODEL and seq % max(WINDOW, SSM_CHUNK) == 0
    assert mix_norm_g.shape[0] == 1, "single-layer trunk"
    n = bsz * seq
    tm_proj = min(512, n)
    tm_mix = min(256, n)
    tm_comb = min(128, n)
    x2 = x.reshape(n, D_MODEL)

    w_in_bf = w_in[0].astype(BF16)
    g_mix = mix_norm_g[0][None, :]
    u, q, kv, gs, ga = _in_proj(x2, g_mix, w_in_bf, tm_proj, pitched=True)
    u_m, _, kv_m, _, _ = _in_proj(meta_tokens, g_mix, w_in_bf, N_META, pitched=False)

    ssm_par = _ssm_params(ssm_a_re[0], ssm_a_im[0], ssm_log_dt[0], ssm_b_re[0], ssm_b_im[0],
                          ssm_c_re[0], ssm_c_im[0], ssm_d[0], SSM_CHUNK)
    y_ssm = _ssm(u, u_m, *ssm_par, batch=bsz, chunk=SSM_CHUNK)

    attn = _attention(attn_sinks[0], q, kv, kv_m, seq // WINDOW)

    h1, hf, route, counts = _mix(
        x2, y_ssm, attn, gs, ga, w_ssm_glu[0].astype(BF16), w_attn_o[0].astype(BF16),
        w_out[0].astype(BF16), ffn_norm_g[0][None, :], router_w[0], router_b[0][None, :], tm_mix)

    tok_disp = min(1024, n)
    be, n_used, next_e, parity, valid, dest_disp, dest_comb, pad_start, pad_len, n_rows = _routing_tables(
        route, counts, n, tok_disp, tm_comb)
    xs = _dispatch(dest_disp, pad_start, pad_len, n_used, hf, n_rows, tok_disp)
    y = _experts(be, n_used, next_e, parity, valid, xs, w_up[0], b_up[0], w_down[0], b_down[0])
    out = _combine(dest_comb, y, h1, route, final_norm_g[None, :], tm_comb)
    return out.reshape(bsz, seq, D_MODEL)
```

```python
import functools
import math

import jax
import jax.numpy as jnp
from jax import lax
from jax.experimental import pallas as pl
from jax.experimental.pallas import tpu as pltpu

F32 = jnp.float32
BF16 = jnp.bfloat16

D_MODEL = 1024
N_META = 16
SSM_WIDTH = 512
SSM_GROUP = 16
SSM_GROUPS = 32
SSM_STATE = 64
HEAD_DIM = 64
N_HEADS = 16
N_KV_HEADS = 2
KV_REP = N_HEADS // N_KV_HEADS
WINDOW = 128
Q_WIDTH = N_HEADS * HEAD_DIM
KV_WIDTH = N_KV_HEADS * HEAD_DIM
N_EXPERTS = 32
TOP_K = 4
D_FF = 1024
SWIGLU_ALPHA = 1.702
SWIGLU_LIMIT = 7.0
RMS_EPS = 1e-5
NEG_INF = -1e30

_U0, _Q0, _KV0, _GS0, _GA0, _IN_END = 0, 512, 1536, 1792, 2816, 3840

SSM_CH_BLOCK = 128
SSM_HALF = (SSM_CH_BLOCK // SSM_GROUP) * SSM_STATE
SSM_CHUNK = 32
SSM_PITCH = 40
LOG2_E = math.log2(math.e)
EXPERT_ROWS = 512
EXPERT_SUB_ROWS = 128
VMEM_LIMIT = 56 * 1024 * 1024


def _dot(a, b):
    return jnp.dot(a, b, preferred_element_type=F32)


def _dot_nt(a, b):
    return lax.dot_general(a, b, (((1,), (1,)), ((), ())), preferred_element_type=F32)


LANES = 128
TILE_ROWS = D_MODEL // LANES


def _store_token_tiles(ref, x, start_row=0):
    rows = x.shape[0]
    for j in range(TILE_ROWS):
        ref[pl.ds(start_row * TILE_ROWS + j, rows, stride=TILE_ROWS), :] = x[:, j * LANES:(j + 1) * LANES]


def _load_token_tiles(ref, start_row, rows):
    return jnp.concatenate(
        [ref[pl.ds(start_row * TILE_ROWS + j, rows, stride=TILE_ROWS), :] for j in range(TILE_ROWS)], axis=1)


def _token_tile(ref, row):
    return ref.at[pl.ds(pl.multiple_of(row * TILE_ROWS, TILE_ROWS), TILE_ROWS), :]


def _in_proj_kernel(x_ref, g_ref, w_ref, u_ref, q_ref, kv_ref, gs_ref, ga_ref, *, pitched):
    x = x_ref[...]
    ms = jnp.mean(x * x, axis=-1, keepdims=True)
    hn = (x * lax.rsqrt(ms + RMS_EPS) * g_ref[...]).astype(BF16)
    u = _dot(hn, w_ref[:, _U0:_Q0])
    if pitched:
        for c in range(u.shape[0] // SSM_CHUNK):
            u_ref[c * SSM_PITCH:c * SSM_PITCH + SSM_CHUNK, :] = u[c * SSM_CHUNK:(c + 1) * SSM_CHUNK, :]
            u_ref[c * SSM_PITCH + SSM_CHUNK:(c + 1) * SSM_PITCH, :] = jnp.zeros(
                (SSM_PITCH - SSM_CHUNK, SSM_WIDTH), F32)
    else:
        u_ref[...] = u
    q_ref[...] = (_dot(hn, w_ref[:, _Q0:_KV0]) * (HEAD_DIM ** -0.5 * LOG2_E)).astype(BF16)
    kv_ref[...] = _dot(hn, w_ref[:, _KV0:_GS0]).astype(BF16)
    gs_ref[...] = jax.nn.sigmoid(_dot(hn, w_ref[:, _GS0:_GA0])).astype(BF16)
    ga_ref[...] = jax.nn.sigmoid(_dot(hn, w_ref[:, _GA0:_IN_END])).astype(BF16)


def _in_proj(x2, g, w_bf, tm, pitched):
    n = x2.shape[0]
    row = lambda w: pl.BlockSpec((tm, w), lambda i: (i, 0))
    full = lambda a: pl.BlockSpec(a.shape, lambda i: (0,) * a.ndim)
    u_rows = (lambda r: r // SSM_CHUNK * SSM_PITCH) if pitched else (lambda r: r)
    return pl.pallas_call(
        functools.partial(_in_proj_kernel, pitched=pitched),
        grid=(n // tm,),
        in_specs=[row(D_MODEL), full(g), full(w_bf)],
        out_specs=[pl.BlockSpec((u_rows(tm), SSM_WIDTH), lambda i: (i, 0)),
                   row(Q_WIDTH), row(2 * KV_WIDTH), row(D_MODEL), row(D_MODEL)],
        out_shape=[
            jax.ShapeDtypeStruct((u_rows(n), SSM_WIDTH), F32),
            jax.ShapeDtypeStruct((n, Q_WIDTH), BF16),
            jax.ShapeDtypeStruct((n, 2 * KV_WIDTH), BF16),
            jax.ShapeDtypeStruct((n, D_MODEL), BF16),
            jax.ShapeDtypeStruct((n, D_MODEL), BF16),
        ],
        compiler_params=pltpu.CompilerParams(
            dimension_semantics=("arbitrary",), vmem_limit_bytes=VMEM_LIMIT),
        name="in_proj",
    )(x2, g, w_bf)


def _ssm_kernel(u_ref, um_ref, bm_ref, cm_ref, ar_ref, ai_ref, atr_ref, ati_ref, d_ref,
                y_ref, sre, sim, *, chunk, rows, batch):
    h = SSM_HALF
    bm = bm_ref[0]
    cm = cm_ref[0]
    ar, ai = ar_ref[0], ai_ref[0]
    atr, ati = atr_ref[0], ati_ref[0]
    dsk = d_ref[0]
    n_chunks = rows // batch

    def advance(sr, si, bu):
        return ar * sr - ai * si + bu[:, :h], ar * si + ai * sr + bu[:, h:]

    bum = _dot(um_ref[...].astype(BF16), bm)
    mr = jnp.zeros((1, h), F32)
    mi = jnp.zeros((1, h), F32)
    for j in range(N_META):
        mr, mi = advance(mr, mi, bum[j:j + 1, :])

    def u_step(t):
        return u_ref[pl.ds(t, rows, stride=SSM_PITCH), :]

    for pad in range(chunk, SSM_PITCH):
        y_ref[pl.ds(pad, rows, stride=SSM_PITCH), :] = jnp.zeros((rows, SSM_CH_BLOCK), F32)

    sre[...] = jnp.zeros_like(sre)
    sim[...] = jnp.zeros_like(sim)

    def pass_a(t, carry):
        bu = _dot(u_step(t).astype(BF16), bm)
        nr, ni = advance(sre[...], sim[...], bu)
        sre[...] = nr
        sim[...] = ni
        return carry

    lax.fori_loop(0, chunk, pass_a, 0)

    def over_chunks(c, carry):
        new = []
        for b in range(batch):
            cr, ci = carry[2 * b], carry[2 * b + 1]
            row = pl.ds(b * n_chunks + c, 1)
            er, ei = sre[row, :], sim[row, :]
            sre[row, :] = cr
            sim[row, :] = ci
            new += [atr * cr - ati * ci + er, atr * ci + ati * cr + ei]
        return tuple(new)

    lax.fori_loop(0, n_chunks, over_chunks, (mr, mi) * batch)

    def pass_b(t, carry):
        ut = u_step(t)
        bu = _dot(ut.astype(BF16), bm)
        nr, ni = advance(sre[...], sim[...], bu)
        sre[...] = nr
        sim[...] = ni
        y = _dot(nr.astype(BF16), cm[:h, :]) + _dot(ni.astype(BF16), cm[h:, :]) + dsk * ut
        y_ref[pl.ds(t, rows, stride=SSM_PITCH), :] = y
        return carry

    lax.fori_loop(0, chunk, pass_b, 0)


def _ssm(u, u_meta, bmat, cmat, a_re, a_im, at_re, at_im, dskip, batch, chunk):
    n = u.shape[0]
    rows = n // SSM_PITCH
    nblk = SSM_WIDTH // SSM_CH_BLOCK
    col = lambda r: pl.BlockSpec((r, SSM_CH_BLOCK), lambda j: (0, j))
    par = lambda a: pl.BlockSpec((1,) + a.shape[1:], lambda j: (j, 0, 0))
    return pl.pallas_call(
        functools.partial(_ssm_kernel, chunk=chunk, rows=rows, batch=batch),
        grid=(nblk,),
        in_specs=[col(n), col(N_META), par(bmat), par(cmat), par(a_re), par(a_im),
                  par(at_re), par(at_im), par(dskip)],
        out_specs=col(n),
        out_shape=jax.ShapeDtypeStruct((n, SSM_WIDTH), F32),
        scratch_shapes=[pltpu.VMEM((rows, SSM_HALF), F32), pltpu.VMEM((rows, SSM_HALF), F32)],
        compiler_params=pltpu.CompilerParams(
            dimension_semantics=("arbitrary",), vmem_limit_bytes=VMEM_LIMIT),
        name="ssm",
    )(u, u_meta, bmat, cmat, a_re, a_im, at_re, at_im, dskip)


def _ssm_params(a_re, a_im, log_dt, b_re, b_im, c_re, c_im, d_skip, chunk):
    dt = jnp.exp(log_dt)[:, None]
    mag = jnp.exp(a_re * dt)
    ang = a_im * dt
    abar_re, abar_im = mag * jnp.cos(ang), mag * jnp.sin(ang)
    den = a_re * a_re + a_im * a_im
    nr, ni = abar_re - 1.0, abar_im
    coef_re = ((nr * a_re + ni * a_im) / den)[..., None]
    coef_im = ((ni * a_re - nr * a_im) / den)[..., None]
    bbar_re = coef_re * b_re - coef_im * b_im
    bbar_im = coef_re * b_im + coef_im * b_re
    magt = jnp.exp(a_re * dt * chunk)
    at_re, at_im = magt * jnp.cos(ang * chunk), magt * jnp.sin(ang * chunk)

    nblk = SSM_WIDTH // SSM_CH_BLOCK
    gpb = SSM_GROUPS // nblk
    eye = jnp.eye(gpb, dtype=F32)

    def in_map(b):
        b = b.reshape(nblk, gpb, SSM_STATE, SSM_GROUP)
        return jnp.einsum('jgpc,gh->jgchp', b, eye).reshape(nblk, SSM_CH_BLOCK, gpb * SSM_STATE)

    def out_map(c):
        c = c.reshape(nblk, gpb, SSM_GROUP, SSM_STATE)
        return jnp.einsum('jgcp,gh->jgphc', c, eye).reshape(nblk, gpb * SSM_STATE, SSM_CH_BLOCK)

    bmat = jnp.concatenate([in_map(bbar_re), in_map(bbar_im)], axis=2).astype(BF16)
    cmat = jnp.concatenate([out_map(c_re), -out_map(c_im)], axis=1).astype(BF16)
    vec = lambda v: v.reshape(nblk, 1, SSM_HALF)
    return (bmat, cmat, vec(abar_re), vec(abar_im), vec(at_re), vec(at_im),
            d_skip.reshape(nblk, 1, SSM_CH_BLOCK))


def _attn_kernel(sink_ref, q_ref, kvc_ref, kvp_ref, kvm_ref, o_ref, *, blocks_per_seq):
    n = pl.program_id(0) % blocks_per_seq
    w = WINDOW
    hd = HEAD_DIM
    qi = lax.broadcasted_iota(jnp.int32, (w, w), 0)
    lane = lax.broadcasted_iota(jnp.int32, (w, w), 1)
    vis_prev = (lane > qi) & (n > 0)
    vis_cur = lane <= qi
    left = lane < hd
    meta_l = lane < N_META
    meta_r = (lane >= N_META) & (lane < 2 * N_META)

    def placed(x_bf):
        x = x_bf.astype(F32)
        xr = pltpu.roll(x, hd, 1)
        lm = lax.broadcasted_iota(jnp.int32, x.shape, 1) < hd
        z = jnp.zeros_like(x)
        return {(0, 0): jnp.where(lm, x, z), (0, 1): jnp.where(lm, z, xr),
                (1, 0): jnp.where(lm, xr, z), (1, 1): jnp.where(lm, z, x)}

    kp, kc, km = placed(kvp_ref[:, :KV_WIDTH]), placed(kvc_ref[:, :KV_WIDTH]), placed(kvm_ref[:, :KV_WIDTH])
    vp, vc, vm = placed(kvp_ref[:, KV_WIDTH:]), placed(kvc_ref[:, KV_WIDTH:]), placed(kvm_ref[:, KV_WIDTH:])
    pad_rows = w - 2 * N_META
    zpad = jnp.zeros((pad_rows, w), F32)
    krow = lax.broadcasted_iota(jnp.int32, (5 * w, w), 0)
    klane = lax.broadcasted_iota(jnp.int32, (5 * w, w), 1)
    row_l = (krow < 2 * w) | ((krow >= 4 * w) & (krow < 4 * w + N_META))
    row_r = ((krow >= 2 * w) & (krow < 4 * w)) | ((krow >= 4 * w + N_META) & (krow < 4 * w + 2 * N_META))
    den_cols = jnp.where((row_l & (klane < hd)) | (row_r & (klane >= hd)), 1.0, 0.0)

    for j in range(N_KV_HEADS):
        kcat = jnp.concatenate(
            [kp[j, 0], kc[j, 0], kp[j, 1], kc[j, 1], km[j, 0], km[j, 1], zpad], axis=0).astype(BF16)
        vcat = jnp.concatenate(
            [jnp.concatenate([vp[j, 0], vc[j, 0], vp[j, 1], vc[j, 1], vm[j, 0], vm[j, 1], zpad], axis=0),
             den_cols], axis=1).astype(BF16)
        for r in range(KV_REP // 2):
            pr = j * (KV_REP // 2) + r
            s = _dot_nt(q_ref[:, pr * w:(pr + 1) * w], kcat)
            s_l = jnp.where(vis_cur, s[:, w:2 * w], jnp.where(vis_prev, s[:, 0:w], NEG_INF))
            s_r = jnp.where(vis_cur, s[:, 3 * w:4 * w], jnp.where(vis_prev, s[:, 2 * w:3 * w], NEG_INF))
            s_m = s[:, 4 * w:]
            sink_l, sink_r = sink_ref[2 * pr] * LOG2_E, sink_ref[2 * pr + 1] * LOG2_E
            m_l = jnp.maximum(jnp.max(jnp.maximum(s_l, jnp.where(meta_l, s_m, NEG_INF)),
                                      axis=1, keepdims=True), sink_l)
            m_r = jnp.maximum(jnp.max(jnp.maximum(s_r, jnp.where(meta_r, s_m, NEG_INF)),
                                      axis=1, keepdims=True), sink_r)
            s_m = jnp.where(meta_l, s_m - m_l, jnp.where(meta_r, s_m - m_r, NEG_INF))
            e_l, e_r = jnp.exp2(s_l - m_l), jnp.exp2(s_r - m_r)
            e = jnp.concatenate([jnp.where(vis_cur, 0.0, e_l), jnp.where(vis_cur, e_l, 0.0),
                                 jnp.where(vis_cur, 0.0, e_r), jnp.where(vis_cur, e_r, 0.0),
                                 jnp.exp2(s_m)], axis=1).astype(BF16)
            acc = _dot(e, vcat)
            den = acc[:, w:] + jnp.where(left, jnp.exp2(sink_l - m_l), jnp.exp2(sink_r - m_r))
            o_ref[:, pr * w:(pr + 1) * w] = (acc[:, :w] / den).astype(BF16)


def _attention(sinks, q, kv, kv_meta, blocks_per_seq):
    n = q.shape[0]
    return pl.pallas_call(
        functools.partial(_attn_kernel, blocks_per_seq=blocks_per_seq),
        grid=(n // WINDOW,),
        in_specs=[
            pl.BlockSpec(memory_space=pltpu.SMEM),
            pl.BlockSpec((WINDOW, Q_WIDTH), lambda g: (g, 0)),
            pl.BlockSpec((WINDOW, 2 * KV_WIDTH), lambda g: (g, 0)),
            pl.BlockSpec((WINDOW, 2 * KV_WIDTH), lambda g: (jnp.maximum(g - 1, 0), 0)),
            pl.BlockSpec((N_META, 2 * KV_WIDTH), lambda g: (0, 0)),
        ],
        out_specs=pl.BlockSpec((WINDOW, Q_WIDTH), lambda g: (g, 0)),
        out_shape=jax.ShapeDtypeStruct((n, Q_WIDTH), BF16),
        compiler_params=pltpu.CompilerParams(dimension_semantics=("arbitrary",)),
        name="attn",
    )(sinks, q, kv, kv, kv_meta)


_ROUTE_IDX, _ROUTE_RANK, _ROUTE_GATE, _ROUTE_ROWS = 0, TOP_K, 2 * TOP_K, 16


def _mix_kernel(x_ref, y_ref, at_ref, gs_ref, ga_ref, wglu_ref, wo_ref, wout_ref, fg_ref, rw_ref, rb_ref,
                h1_ref, hf_ref, route_ref, cnt_ref, cnt_scr):
    tm = x_ref.shape[0]

    @pl.when(pl.program_id(0) == 0)
    def _():
        cnt_scr[...] = jnp.zeros_like(cnt_scr)

    y_ssm = jnp.concatenate([y_ref[c * SSM_PITCH:c * SSM_PITCH + SSM_CHUNK, :]
                             for c in range(tm // SSM_CHUNK)], axis=0)
    glu = _dot(jax.nn.gelu(y_ssm).astype(BF16), wglu_ref[...])
    branch_ssm = glu[:, :D_MODEL] * jax.nn.sigmoid(glu[:, D_MODEL:])
    branch_attn = _dot(at_ref[...], wo_ref[...])
    merged = gs_ref[...].astype(F32) * branch_ssm + ga_ref[...].astype(F32) * branch_attn
    h1 = x_ref[...] + _dot(merged.astype(BF16), wout_ref[...])
    h1_ref[...] = h1
    ms = jnp.mean(h1 * h1, axis=-1, keepdims=True)
    hf = h1 * lax.rsqrt(ms + RMS_EPS) * fg_ref[...]
    _store_token_tiles(hf_ref, hf)

    hf_hi = hf.astype(BF16)
    hf_lo = (hf - hf_hi.astype(F32)).astype(BF16)
    hi_prod = _dot(hf_hi, rw_ref[...])
    logits = (hi_prod[:, :LANES] + (hi_prod[:, LANES:] + _dot(hf_lo, rw_ref[:, :LANES]))
              + rb_ref[...])
    lt = logits.T[:N_EXPERTS, :]
    erow = lax.broadcasted_iota(jnp.int32, (N_EXPERTS, tm), 0)
    vals, idxs, hots = [], [], []
    rest = lt
    for _ in range(TOP_K):
        m = jnp.max(rest, axis=0, keepdims=True)
        first = jnp.min(jnp.where(rest == m, erow, N_EXPERTS), axis=0, keepdims=True)
        hot = erow == first
        vals.append(m)
        idxs.append(first)
        hots.append(hot)
        rest = jnp.where(hot, -jnp.inf, rest)
    exps = [jnp.exp(v - vals[0]) for v in vals]
    tot = exps[0] + exps[1] + exps[2] + exps[3]

    sel = (hots[0] | hots[1] | hots[2] | hots[3]).astype(F32)
    ti = lax.broadcasted_iota(jnp.int32, (tm, tm), 0)
    tj = lax.broadcasted_iota(jnp.int32, (tm, tm), 1)
    earlier = (ti < tj).astype(BF16)
    rank_e = _dot(sel.astype(BF16), earlier) + cnt_scr[...]
    cnt_scr[...] = cnt_scr[...] + jnp.sum(sel, axis=1, keepdims=True)
    cnt_ref[...] = cnt_scr[...]

    rrow = lax.broadcasted_iota(jnp.int32, (_ROUTE_ROWS, tm), 0)
    route = jnp.zeros((_ROUTE_ROWS, tm), F32)
    for k in range(TOP_K):
        rank_k = jnp.sum(jnp.where(hots[k], rank_e, 0.0), axis=0, keepdims=True)
        route = jnp.where(rrow == _ROUTE_IDX + k, idxs[k].astype(F32), route)
        route = jnp.where(rrow == _ROUTE_RANK + k, rank_k, route)
        route = jnp.where(rrow == _ROUTE_GATE + k, exps[k] / tot, route)
    route_ref[...] = route


def _mix(x2, y, attn, gs, ga, wglu, wo, wout, fg, rw, rb, tm):
    n = x2.shape[0]
    rw = jnp.pad(rw, ((0, 0), (0, LANES - N_EXPERTS)))
    rb = jnp.pad(rb, ((0, 0), (0, LANES - N_EXPERTS)), constant_values=NEG_INF)
    rw_hi = rw.astype(BF16)
    rw_split = jnp.concatenate([rw_hi, (rw - rw_hi.astype(F32)).astype(BF16)], axis=1)
    row = lambda w: pl.BlockSpec((tm, w), lambda i: (i, 0))
    full = lambda a: pl.BlockSpec(a.shape, lambda i: (0,) * a.ndim)
    return pl.pallas_call(
        _mix_kernel,
        grid=(n // tm,),
        in_specs=[row(D_MODEL), pl.BlockSpec((tm // SSM_CHUNK * SSM_PITCH, SSM_WIDTH), lambda i: (i, 0)),
                  row(Q_WIDTH), row(D_MODEL), row(D_MODEL),
                  full(wglu), full(wo), full(wout), full(fg), full(rw_split), full(rb)],
        out_specs=[row(D_MODEL), pl.BlockSpec((tm * TILE_ROWS, LANES), lambda i: (i, 0)),
                   pl.BlockSpec((_ROUTE_ROWS, tm), lambda i: (0, i)),
                   pl.BlockSpec((N_EXPERTS, 1), lambda i: (0, 0))],
        out_shape=[
            jax.ShapeDtypeStruct((n, D_MODEL), F32),
            jax.ShapeDtypeStruct((n * TILE_ROWS, LANES), F32),
            jax.ShapeDtypeStruct((_ROUTE_ROWS, n), F32),
            jax.ShapeDtypeStruct((N_EXPERTS, 1), F32),
        ],
        scratch_shapes=[pltpu.VMEM((N_EXPERTS, 1), F32)],
        compiler_params=pltpu.CompilerParams(
            dimension_semantics=("arbitrary",), vmem_limit_bytes=VMEM_LIMIT),
        name="mix_router",
    )(x2, y, attn, gs, ga, wglu, wo, wout, fg, rw_split, rb)


def _tiles_wait_copy(src_hbm, dst, n_tiles, sem):
    rows = n_tiles * TILE_ROWS
    return pltpu.make_async_copy(src_hbm.at[pl.ds(0, rows), :], dst.at[pl.ds(0, rows), :], sem)


_ISSUE_UNROLL = 16


_DISPATCH_RING = 4


def _dispatch_kernel(dst_ref, pad_start_ref, pad_len_ref, nu_ref, hf_hbm, xs_hbm, zero_blk, ring, in_sem,
                     out_sem, pad_sem, *, n_steps):
    i = pl.program_id(0)
    last = n_steps - 1
    pairs = dst_ref.shape[0]
    tokens = pairs // TOP_K
    blk_rows = tokens * TILE_ROWS
    n_blocks = xs_hbm.shape[0] // (EXPERT_ROWS * TILE_ROWS)
    slot = i % _DISPATCH_RING

    def load(step, s):
        src = hf_hbm.at[pl.ds(pl.multiple_of(step * blk_rows, blk_rows), blk_rows), :]
        return pltpu.make_async_copy(src, ring.at[s], in_sem.at[s])

    def wait_copies(s):
        for _ in range(TOP_K):
            pltpu.make_async_copy(ring.at[s], xs_hbm.at[pl.ds(0, blk_rows), :], out_sem.at[s]).wait()

    def zero_copy(row, rows):
        src = zero_blk.at[pl.ds(0, rows * TILE_ROWS), :]
        dst = xs_hbm.at[pl.ds(pl.multiple_of(row * TILE_ROWS, TILE_ROWS), rows * TILE_ROWS), :]
        return pltpu.make_async_copy(src, dst, pad_sem)

    def for_each_pad(fn):
        def per_expert(e, carry):
            row, left = pad_start_ref[e], pad_len_ref[e]
            size = EXPERT_ROWS // 2
            while size >= 1:
                take = left & size

                @pl.when(take != 0)
                def _(row=row, size=size):
                    fn(zero_copy(row, size))

                row = row + take
                size //= 2
            return carry
        lax.fori_loop(0, N_EXPERTS, per_expert, 0)

        def per_block(b, carry):
            fn(zero_copy(b * EXPERT_ROWS, EXPERT_ROWS))
            return carry
        lax.fori_loop(nu_ref[0], n_blocks, per_block, 0)

    @pl.when(i == 0)
    def _():
        for s in range(min(2, n_steps)):
            load(s, s).start()
        zero_blk[...] = jnp.zeros_like(zero_blk)
        for_each_pad(lambda cp: cp.start())

    @pl.when(i >= 2)
    def _():
        wait_copies((i + 2) % _DISPATCH_RING)

    @pl.when(i + 2 < n_steps)
    def _():
        load(i + 2, (i + 2) % _DISPATCH_RING).start()

    load(i, slot).wait()
    src_blk = ring.at[slot]
    for k in range(TOP_K):
        def issue(o, carry, k=k):
            tok0 = pl.multiple_of(o * _ISSUE_UNROLL, _ISSUE_UNROLL)
            dsts = [dst_ref[k * tokens + tok0 + r] for r in range(_ISSUE_UNROLL)]
            for r in range(_ISSUE_UNROLL):
                pltpu.make_async_copy(_token_tile(src_blk, tok0 + r), _token_tile(xs_hbm, dsts[r]),
                                      out_sem.at[slot]).start(priority=r % 2)
            return carry

        lax.fori_loop(0, tokens // _ISSUE_UNROLL, issue, 0)

    @pl.when(i == last)
    def _():
        if n_steps > 1:
            wait_copies((i + _DISPATCH_RING - 1) % _DISPATCH_RING)
        wait_copies(slot)
        for_each_pad(lambda cp: cp.wait())


def _dispatch(dest, pad_start, pad_len, n_used, hf_tiles, n_rows, tokens_per_step):
    n = dest.shape[0] // TOP_K
    pairs = tokens_per_step * TOP_K
    assert n % tokens_per_step == 0 and tokens_per_step % _ISSUE_UNROLL == 0
    smem = lambda: pl.BlockSpec(memory_space=pltpu.SMEM)
    n_steps = n // tokens_per_step
    return pl.pallas_call(
        functools.partial(_dispatch_kernel, n_steps=n_steps),
        grid=(n_steps,),
        in_specs=[pl.BlockSpec((pairs,), lambda i: (i,), memory_space=pltpu.SMEM), smem(), smem(), smem(),
                  pl.BlockSpec(memory_space=pl.ANY)],
        out_specs=pl.BlockSpec(memory_space=pl.ANY),
        out_shape=jax.ShapeDtypeStruct((n_rows * TILE_ROWS, LANES), F32),
        scratch_shapes=[pltpu.VMEM((EXPERT_ROWS * TILE_ROWS, LANES), F32),
                        pltpu.VMEM((_DISPATCH_RING, tokens_per_step * TILE_ROWS, LANES), F32),
                        pltpu.SemaphoreType.DMA((_DISPATCH_RING,)), pltpu.SemaphoreType.DMA((_DISPATCH_RING,)),
                        pltpu.SemaphoreType.DMA],
        compiler_params=pltpu.CompilerParams(
            dimension_semantics=("arbitrary",), vmem_limit_bytes=VMEM_LIMIT),
        name="dispatch",
    )(dest, pad_start, pad_len, n_used, hf_tiles)


def _expert_kernel(be_ref, nu_ref, nxt_ref, par_ref, val_ref, xs_ref, wu_hbm, bu_ref, wd_hbm, bd_ref, y_ref,
                   wu_f32, wd_f32, wu_bf, wd_bf, wsem):
    i = pl.program_id(0)

    def mlp(xb):
        up = _dot(xb, wu_bf[...]) + bu_ref[0]
        x_glu = jnp.minimum(up[:, :D_FF], SWIGLU_LIMIT)
        x_lin = jnp.clip(up[:, D_FF:], -SWIGLU_LIMIT, SWIGLU_LIMIT)
        act = x_glu * jax.nn.sigmoid(SWIGLU_ALPHA * x_glu) * (x_lin + 1.0)
        return _dot(act.astype(BF16), wd_bf[...]) + bd_ref[0]

    def weight_copies(expert, s):
        return (pltpu.make_async_copy(wu_hbm.at[expert], wu_f32.at[s], wsem.at[0, s]),
                pltpu.make_async_copy(wd_hbm.at[expert], wd_f32.at[s], wsem.at[1, s]))

    @pl.when(i == 0)
    def _():
        for cp in weight_copies(be_ref[0], par_ref[0]):
            cp.start()

    @pl.when((i == 0) | (be_ref[i] != be_ref[jnp.maximum(i - 1, 0)]))
    def _():
        s = par_ref[i]
        for cp in weight_copies(be_ref[i], s):
            cp.wait()

        @pl.when(nxt_ref[i] >= 0)
        def _():
            for cp in weight_copies(nxt_ref[i], 1 - s):
                cp.start()

        wu_bf[...] = wu_f32[s].astype(BF16)
        wd_bf[...] = wd_f32[s].astype(BF16)

    used = i < nu_ref[0]
    full = val_ref[i] > EXPERT_ROWS - EXPERT_SUB_ROWS

    @pl.when(used & full)
    def _():
        _store_token_tiles(y_ref, mlp(_load_token_tiles(xs_ref, 0, EXPERT_ROWS).astype(BF16)))

    @pl.when(used & jnp.logical_not(full))
    def _():
        sub = EXPERT_SUB_ROWS
        n_sub = lax.shift_right_logical(val_ref[i] + (sub - 1), sub.bit_length() - 1)

        def compute(j, carry):
            row0 = pl.multiple_of(j * sub, sub)
            _store_token_tiles(y_ref, mlp(_load_token_tiles(xs_ref, row0, sub).astype(BF16)), row0)
            return carry

        def clear(j, carry):
            row0 = pl.multiple_of(j * sub * TILE_ROWS, sub * TILE_ROWS)
            y_ref[pl.ds(row0, sub * TILE_ROWS), :] = jnp.zeros((sub * TILE_ROWS, LANES), F32)
            return carry

        lax.fori_loop(0, n_sub, compute, 0)
        lax.fori_loop(n_sub, EXPERT_ROWS // sub, clear, 0)

    @pl.when(jnp.logical_not(used))
    def _():
        y_ref[...] = jnp.zeros_like(y_ref)


def _experts(block_expert, n_used, next_expert, parity, valid_rows, xs_tiles, w_up, b_up, w_down, b_down):
    n_blocks = block_expert.shape[0]
    blk = (EXPERT_ROWS * TILE_ROWS, LANES)
    grid_spec = pltpu.PrefetchScalarGridSpec(
        num_scalar_prefetch=5,
        grid=(n_blocks,),
        in_specs=[
            pl.BlockSpec(blk, lambda i, be, nu, *_: (jnp.minimum(i, nu[0] - 1), 0)),
            pl.BlockSpec(memory_space=pl.ANY),
            pl.BlockSpec((1, 1, 2 * D_FF), lambda i, be, *_: (be[i], 0, 0)),
            pl.BlockSpec(memory_space=pl.ANY),
            pl.BlockSpec((1, 1, D_MODEL), lambda i, be, *_: (be[i], 0, 0)),
        ],
        out_specs=pl.BlockSpec(blk, lambda i, *_: (i, 0)),
        scratch_shapes=[
            pltpu.VMEM((2, D_MODEL, 2 * D_FF), F32),
            pltpu.VMEM((2, D_FF, D_MODEL), F32),
            pltpu.VMEM((D_MODEL, 2 * D_FF), BF16),
            pltpu.VMEM((D_FF, D_MODEL), BF16),
            pltpu.SemaphoreType.DMA((2, 2)),
        ],
    )
    return pl.pallas_call(
        _expert_kernel,
        grid_spec=grid_spec,
        out_shape=jax.ShapeDtypeStruct((n_blocks * blk[0], LANES), F32),
        compiler_params=pltpu.CompilerParams(
            dimension_semantics=("arbitrary",), vmem_limit_bytes=VMEM_LIMIT),
        name="experts",
    )(block_expert, n_used, next_expert, parity, valid_rows, xs_tiles,
      w_up, b_up[:, None, :], w_down, b_down[:, None, :])


_COMBINE_RING = 3


def _combine_kernel(dst0_ref, dst1_ref, dst2_ref, y_hbm, h1_ref, route_ref, g_ref, o_ref, *scratch):
    bufs, sem = scratch[:_COMBINE_RING], scratch[_COMBINE_RING]
    i = pl.program_id(0)
    last = pl.num_programs(0) - 1
    tm = h1_ref.shape[0]
    rows = TOP_K * tm

    def gather_group(idx_ref, s, row0):
        srcs = [idx_ref[row0 + r] for r in range(_ISSUE_UNROLL)]
        for r in range(_ISSUE_UNROLL):
            pltpu.make_async_copy(_token_tile(y_hbm, srcs[r]), _token_tile(bufs[s], row0 + r),
                                  sem.at[s]).start(priority=r % 2)

    @pl.when(i == 0)
    def _():
        for s, idx_ref in ((0, dst0_ref), (1, dst1_ref)):
            def body(o, carry, s=s, idx_ref=idx_ref):
                gather_group(idx_ref, s, pl.multiple_of(o * _ISSUE_UNROLL, _ISSUE_UNROLL))
                return carry
            lax.fori_loop(0, rows // _ISSUE_UNROLL, body, 0)

    def step(s):
        cur = bufs[s]
        ahead = (s + 2) % _COMBINE_RING
        _tiles_wait_copy(y_hbm, cur, rows, sem.at[s]).wait()
        for g in range(rows // _ISSUE_UNROLL):
            gather_group(dst2_ref, ahead, g * _ISSUE_UNROLL)
        assert tm == LANES
        rt = jnp.concatenate([route_ref[...], jnp.zeros((LANES - _ROUTE_ROWS, tm), F32)], axis=0).T
        acc = h1_ref[...]
        for k in range(TOP_K):
            gate = rt[:, _ROUTE_GATE + k:_ROUTE_GATE + k + 1]
            acc = acc + gate * _load_token_tiles(cur, k * tm, tm)
        ms = jnp.mean(acc * acc, axis=-1, keepdims=True)
        o_ref[...] = acc * lax.rsqrt(ms + RMS_EPS) * g_ref[...]

        @pl.when(i == last)
        def _():
            for t in ((s + 1) % _COMBINE_RING, ahead):
                _tiles_wait_copy(y_hbm, bufs[t], rows, sem.at[t]).wait()

    for s in range(_COMBINE_RING):
        pl.when(i % _COMBINE_RING == s)(functools.partial(step, s))


def _combine(dest_kmajor, y, h1, route, g, tm):
    n = h1.shape[0]
    n_tiles = n // tm
    idx_spec = lambda ahead: pl.BlockSpec(
        (TOP_K * tm,), lambda i: (jnp.minimum(i + ahead, n_tiles - 1),), memory_space=pltpu.SMEM)
    ring_buf = pltpu.VMEM((TOP_K * tm * TILE_ROWS, LANES), F32)
    return pl.pallas_call(
        _combine_kernel,
        grid=(n_tiles,),
        in_specs=[
            idx_spec(0), idx_spec(1), idx_spec(2),
            pl.BlockSpec(memory_space=pl.ANY),
            pl.BlockSpec((tm, D_MODEL), lambda i: (i, 0)),
            pl.BlockSpec((_ROUTE_ROWS, tm), lambda i: (0, i)),
            pl.BlockSpec((1, D_MODEL), lambda i: (0, 0)),
        ],
        out_specs=pl.BlockSpec((tm, D_MODEL), lambda i: (i, 0)),
        out_shape=jax.ShapeDtypeStruct((n, D_MODEL), F32),
        scratch_shapes=[ring_buf] * _COMBINE_RING + [pltpu.SemaphoreType.DMA((_COMBINE_RING,))],
        compiler_params=pltpu.CompilerParams(
            dimension_semantics=("arbitrary",), vmem_limit_bytes=VMEM_LIMIT),
        name="combine",
    )(dest_kmajor, dest_kmajor, dest_kmajor, y, h1, route, g)


def _routing_tables(route, counts, n, tokens_dispatch, tokens_combine):
    tm = EXPERT_ROWS
    i32 = jnp.int32
    n_blocks = (n * TOP_K + N_EXPERTS * (tm - 1)) // tm
    idx = route[_ROUTE_IDX:_ROUTE_IDX + TOP_K].astype(i32)
    rank = route[_ROUTE_RANK:_ROUTE_RANK + TOP_K].astype(i32)
    cnt = counts[:, 0].astype(i32)
    eid = jnp.arange(N_EXPERTS, dtype=i32)
    upto = eid[None, :] <= eid[:, None]
    blocks_e = (cnt + tm - 1) // tm
    blocks_end = jnp.sum(jnp.where(upto, blocks_e[None, :], 0), axis=1)
    row_start = (blocks_end - blocks_e) * tm
    n_used = blocks_end[N_EXPERTS - 1]
    used = blocks_e > 0

    def lookup(table, keys):
        hit = keys[None] == eid.reshape((N_EXPERTS,) + (1,) * keys.ndim)
        return jnp.sum(jnp.where(hit, table.reshape((N_EXPERTS,) + (1,) * keys.ndim), 0), axis=0)

    dest = lookup(row_start, idx) + rank
    blk = jnp.arange(n_blocks, dtype=i32)
    last_used = jnp.max(jnp.where(used, eid, 0))
    be = jnp.where(blk < n_used, jnp.sum((blocks_end[None, :] <= blk[:, None]).astype(i32), axis=1), last_used)
    later_used = used[None, :] & (eid[None, :] > eid[:, None])
    after = jnp.min(jnp.where(later_used, eid[None, :], N_EXPERTS), axis=1)
    next_e = jnp.where(after < N_EXPERTS, after, -1)
    parity_e = (jnp.sum((upto & used[None, :]).astype(i32), axis=1) - 1) % 2

    def k_major(tokens):
        return dest.reshape(TOP_K, n // tokens, tokens).transpose(1, 0, 2).reshape(-1)

    valid = jnp.clip(lookup(row_start + cnt, be) - blk * tm, 0, tm)
    return (be.astype(i32), n_used.reshape(1), lookup(next_e, be).astype(i32), lookup(parity_e, be).astype(i32),
            valid.astype(i32), k_major(tokens_dispatch), k_major(tokens_combine), row_start + cnt,
            blocks_e * tm - cnt, n_blocks * tm)


def kernel(x, meta_tokens, mix_norm_g, w_in, ssm_a_re, ssm_a_im, ssm_log_dt, ssm_b_re, ssm_b_im,
           ssm_c_re, ssm_c_im, ssm_d, w_ssm_glu, attn_sinks, w_attn_o, w_out, ffn_norm_g,
           router_w, router_b, w_up, b_up, w_down, b_down, final_norm_g):
    bsz, seq, d = x.shape
    assert d == D_MODEL and seq % max(WINDOW, SSM_CHUNK) == 0
    assert mix_norm_g.shape[0] == 1, "single-layer trunk"
    n = bsz * seq
    tm_proj = min(512, n)
    tm_mix = min(256, n)
    tm_comb = min(128, n)
    x2 = x.reshape(n, D_MODEL)

    w_in_bf = w_in[0].astype(BF16)
    g_mix = mix_norm_g[0][None, :]
    u, q, kv, gs, ga = _in_proj(x2, g_mix, w_in_bf, tm_proj, pitched=True)
    u_m, _, kv_m, _, _ = _in_proj(meta_tokens, g_mix, w_in_bf, N_META, pitched=False)

    ssm_par = _ssm_params(ssm_a_re[0], ssm_a_im[0], ssm_log_dt[0], ssm_b_re[0], ssm_b_im[0],
                          ssm_c_re[0], ssm_c_im[0], ssm_d[0], SSM_CHUNK)
    y_ssm = _ssm(u, u_m, *ssm_par, batch=bsz, chunk=SSM_CHUNK)

    attn = _attention(attn_sinks[0], q, kv, kv_m, seq // WINDOW)

    h1, hf, route, counts = _mix(
        x2, y_ssm, attn, gs, ga, w_ssm_glu[0].astype(BF16), w_attn_o[0].astype(BF16),
        w_out[0].astype(BF16), ffn_norm_g[0][None, :], router_w[0], router_b[0][None, :], tm_mix)

    tok_disp = min(1024, n)
    be, n_used, next_e, parity, valid, dest_disp, dest_comb, pad_start, pad_len, n_rows = _routing_tables(
        route, counts, n, tok_disp, tm_comb)
    xs = _dispatch(dest_disp, pad_start, pad_len, n_used, hf, n_rows, tok_disp)
    y = _experts(be, n_used, next_e, parity, valid, xs, w_up[0], b_up[0], w_down[0], b_down[0])
    out = _combine(dest_comb, y, h1, route, final_norm_g[None, :], tm_comb)
    return out.reshape(bsz, seq, D_MODEL)
```

```python
import functools
import math

import jax
import jax.numpy as jnp
from jax import lax
from jax.experimental import pallas as pl
from jax.experimental.pallas import tpu as pltpu

F32 = jnp.float32
BF16 = jnp.bfloat16

D_MODEL = 1024
N_META = 16
SSM_WIDTH = 512
SSM_GROUP = 16
SSM_GROUPS = 32
SSM_STATE = 64
HEAD_DIM = 64
N_HEADS = 16
N_KV_HEADS = 2
KV_REP = N_HEADS // N_KV_HEADS
WINDOW = 128
Q_WIDTH = N_HEADS * HEAD_DIM
KV_WIDTH = N_KV_HEADS * HEAD_DIM
N_EXPERTS = 32
TOP_K = 4
D_FF = 1024
SWIGLU_ALPHA = 1.702
SWIGLU_LIMIT = 7.0
RMS_EPS = 1e-5
NEG_INF = -1e30

_U0, _Q0, _KV0, _GS0, _GA0, _IN_END = 0, 512, 1536, 1792, 2816, 3840

SSM_CH_BLOCK = 128
SSM_HALF = (SSM_CH_BLOCK // SSM_GROUP) * SSM_STATE
SSM_CHUNK = 32
SSM_PITCH = 40
LOG2_E = math.log2(math.e)
EXPERT_ROWS = 512
EXPERT_SUB_ROWS = 128
VMEM_LIMIT = 56 * 1024 * 1024


def _dot(a, b):
    return jnp.dot(a, b, preferred_element_type=F32)


def _dot_nt(a, b):
    return lax.dot_general(a, b, (((1,), (1,)), ((), ())), preferred_element_type=F32)


LANES = 128
TILE_ROWS = D_MODEL // LANES


def _store_token_tiles(ref, x, start_row=0):
    rows = x.shape[0]
    for j in range(TILE_ROWS):
        ref[pl.ds(start_row * TILE_ROWS + j, rows, stride=TILE_ROWS), :] = x[:, j * LANES:(j + 1) * LANES]


def _load_token_tiles(ref, start_row, rows):
    return jnp.concatenate(
        [ref[pl.ds(start_row * TILE_ROWS + j, rows, stride=TILE_ROWS), :] for j in range(TILE_ROWS)], axis=1)


def _token_tile(ref, row):
    return ref.at[pl.ds(pl.multiple_of(row * TILE_ROWS, TILE_ROWS), TILE_ROWS), :]


def _in_proj_kernel(x_ref, g_ref, w_ref, u_ref, q_ref, kv_ref, gs_ref, ga_ref, *, pitched):
    x = x_ref[...]
    ms = jnp.mean(x * x, axis=-1, keepdims=True)
    hn = (x * lax.rsqrt(ms + RMS_EPS) * g_ref[...]).astype(BF16)
    u = _dot(hn, w_ref[:, _U0:_Q0])
    if pitched:
        for c in range(u.shape[0] // SSM_CHUNK):
            u_ref[c * SSM_PITCH:c * SSM_PITCH + SSM_CHUNK, :] = u[c * SSM_CHUNK:(c + 1) * SSM_CHUNK, :]
            u_ref[c * SSM_PITCH + SSM_CHUNK:(c + 1) * SSM_PITCH, :] = jnp.zeros(
                (SSM_PITCH - SSM_CHUNK, SSM_WIDTH), F32)
    else:
        u_ref[...] = u
    q_ref[...] = (_dot(hn, w_ref[:, _Q0:_KV0]) * (HEAD_DIM ** -0.5 * LOG2_E)).astype(BF16)
    kv_ref[...] = _dot(hn, w_ref[:, _KV0:_GS0]).astype(BF16)
    gs_ref[...] = jax.nn.sigmoid(_dot(hn, w_ref[:, _GS0:_GA0])).astype(BF16)
    ga_ref[...] = jax.nn.sigmoid(_dot(hn, w_ref[:, _GA0:_IN_END])).astype(BF16)


def _in_proj(x2, g, w_bf, tm, pitched):
    n = x2.shape[0]
    row = lambda w: pl.BlockSpec((tm, w), lambda i: (i, 0))
    full = lambda a: pl.BlockSpec(a.shape, lambda i: (0,) * a.ndim)
    u_rows = (lambda r: r // SSM_CHUNK * SSM_PITCH) if pitched else (lambda r: r)
    return pl.pallas_call(
        functools.partial(_in_proj_kernel, pitched=pitched),
        grid=(n // tm,),
        in_specs=[row(D_MODEL), full(g), full(w_bf)],
        out_specs=[pl.BlockSpec((u_rows(tm), SSM_WIDTH), lambda i: (i, 0)),
                   row(Q_WIDTH), row(2 * KV_WIDTH), row(D_MODEL), row(D_MODEL)],
        out_shape=[
            jax.ShapeDtypeStruct((u_rows(n), SSM_WIDTH), F32),
            jax.ShapeDtypeStruct((n, Q_WIDTH), BF16),
            jax.ShapeDtypeStruct((n, 2 * KV_WIDTH), BF16),
            jax.ShapeDtypeStruct((n, D_MODEL), BF16),
            jax.ShapeDtypeStruct((n, D_MODEL), BF16),
        ],
        compiler_params=pltpu.CompilerParams(
            dimension_semantics=("arbitrary",), vmem_limit_bytes=VMEM_LIMIT),
        name="in_proj",
    )(x2, g, w_bf)


def _ssm_kernel(u_ref, um_ref, bm_ref, cm_ref, ar_ref, ai_ref, atr_ref, ati_ref, d_ref,
                y_ref, sre, sim, *, chunk, rows, batch):
    h = SSM_HALF
    bm = bm_ref[0]
    cm = cm_ref[0]
    ar, ai = ar_ref[0], ai_ref[0]
    atr, ati = atr_ref[0], ati_ref[0]
    dsk = d_ref[0]
    n_chunks = rows // batch

    def advance(sr, si, bu):
        return ar * sr - ai * si + bu[:, :h], ar * si + ai * sr + bu[:, h:]

    bum = _dot(um_ref[...].astype(BF16), bm)
    mr = jnp.zeros((1, h), F32)
    mi = jnp.zeros((1, h), F32)
    for j in range(N_META):
        mr, mi = advance(mr, mi, bum[j:j + 1, :])

    def u_step(t):
        return u_ref[pl.ds(t, rows, stride=SSM_PITCH), :]

    for pad in range(chunk, SSM_PITCH):
        y_ref[pl.ds(pad, rows, stride=SSM_PITCH), :] = jnp.zeros((rows, SSM_CH_BLOCK), F32)

    sre[...] = jnp.zeros_like(sre)
    sim[...] = jnp.zeros_like(sim)

    def pass_a(t, carry):
        bu = _dot(u_step(t).astype(BF16), bm)
        nr, ni = advance(sre[...], sim[...], bu)
        sre[...] = nr
        sim[...] = ni
        return carry

    lax.fori_loop(0, chunk, pass_a, 0)

    def over_chunks(c, carry):
        new = []
        for b in range(batch):
            cr, ci = carry[2 * b], carry[2 * b + 1]
            row = pl.ds(b * n_chunks + c, 1)
            er, ei = sre[row, :], sim[row, :]
            sre[row, :] = cr
            sim[row, :] = ci
            new += [atr * cr - ati * ci + er, atr * ci + ati * cr + ei]
        return tuple(new)

    lax.fori_loop(0, n_chunks, over_chunks, (mr, mi) * batch)

    def pass_b(t, carry):
        ut = u_step(t)
        bu = _dot(ut.astype(BF16), bm)
        nr, ni = advance(sre[...], sim[...], bu)
        sre[...] = nr
        sim[...] = ni
        y = _dot(nr.astype(BF16), cm[:h, :]) + _dot(ni.astype(BF16), cm[h:, :]) + dsk * ut
        y_ref[pl.ds(t, rows, stride=SSM_PITCH), :] = y
        return carry

    lax.fori_loop(0, chunk, pass_b, 0)


def _ssm(u, u_meta, bmat, cmat, a_re, a_im, at_re, at_im, dskip, batch, chunk):
    n = u.shape[0]
    rows = n // SSM_PITCH
    nblk = SSM_WIDTH // SSM_CH_BLOCK
    col = lambda r: pl.BlockSpec((r, SSM_CH_BLOCK), lambda j: (0, j))
    par = lambda a: pl.BlockSpec((1,) + a.shape[1:], lambda j: (j, 0, 0))
    return pl.pallas_call(
        functools.partial(_ssm_kernel, chunk=chunk, rows=rows, batch=batch),
        grid=(nblk,),
        in_specs=[col(n), col(N_META), par(bmat), par(cmat), par(a_re), par(a_im),
                  par(at_re), par(at_im), par(dskip)],
        out_specs=col(n),
        out_shape=jax.ShapeDtypeStruct((n, SSM_WIDTH), F32),
        scratch_shapes=[pltpu.VMEM((rows, SSM_HALF), F32), pltpu.VMEM((rows, SSM_HALF), F32)],
        compiler_params=pltpu.CompilerParams(
            dimension_semantics=("arbitrary",), vmem_limit_bytes=VMEM_LIMIT),
        name="ssm",
    )(u, u_meta, bmat, cmat, a_re, a_im, at_re, at_im, dskip)


def _ssm_params(a_re, a_im, log_dt, b_re, b_im, c_re, c_im, d_skip, chunk):
    dt = jnp.exp(log_dt)[:, None]
    mag = jnp.exp(a_re * dt)
    ang = a_im * dt
    abar_re, abar_im = mag * jnp.cos(ang), mag * jnp.sin(ang)
    den = a_re * a_re + a_im * a_im
    nr, ni = abar_re - 1.0, abar_im
    coef_re = ((nr * a_re + ni * a_im) / den)[..., None]
    coef_im = ((ni * a_re - nr * a_im) / den)[..., None]
    bbar_re = coef_re * b_re - coef_im * b_im
    bbar_im = coef_re * b_im + coef_im * b_re
    magt = jnp.exp(a_re * dt * chunk)
    at_re, at_im = magt * jnp.cos(ang * chunk), magt * jnp.sin(ang * chunk)

    nblk = SSM_WIDTH // SSM_CH_BLOCK
    gpb = SSM_GROUPS // nblk
    eye = jnp.eye(gpb, dtype=F32)

    def in_map(b):
        b = b.reshape(nblk, gpb, SSM_STATE, SSM_GROUP)
        return jnp.einsum('jgpc,gh->jgchp', b, eye).reshape(nblk, SSM_CH_BLOCK, gpb * SSM_STATE)

    def out_map(c):
        c = c.reshape(nblk, gpb, SSM_GROUP, SSM_STATE)
        return jnp.einsum('jgcp,gh->jgphc', c, eye).reshape(nblk, gpb * SSM_STATE, SSM_CH_BLOCK)

    bmat = jnp.concatenate([in_map(bbar_re), in_map(bbar_im)], axis=2).astype(BF16)
    cmat = jnp.concatenate([out_map(c_re), -out_map(c_im)], axis=1).astype(BF16)
    vec = lambda v: v.reshape(nblk, 1, SSM_HALF)
    return (bmat, cmat, vec(abar_re), vec(abar_im), vec(at_re), vec(at_im),
            d_skip.reshape(nblk, 1, SSM_CH_BLOCK))


def _attn_kernel(sink_ref, q_ref, kvc_ref, kvp_ref, kvm_ref, o_ref, *, blocks_per_seq):
    n = pl.program_id(0) % blocks_per_seq
    w = WINDOW
    hd = HEAD_DIM
    qi = lax.broadcasted_iota(jnp.int32, (w, w), 0)
    lane = lax.broadcasted_iota(jnp.int32, (w, w), 1)
    vis_prev = (lane > qi) & (n > 0)
    vis_cur = lane <= qi
    left = lane < hd
    meta_l = lane < N_META
    meta_r = (lane >= N_META) & (lane < 2 * N_META)

    def placed(x_bf):
        x = x_bf.astype(F32)
        xr = pltpu.roll(x, hd, 1)
        lm = lax.broadcasted_iota(jnp.int32, x.shape, 1) < hd
        z = jnp.zeros_like(x)
        return {(0, 0): jnp.where(lm, x, z), (0, 1): jnp.where(lm, z, xr),
                (1, 0): jnp.where(lm, xr, z), (1, 1): jnp.where(lm, z, x)}

    kp, kc, km = placed(kvp_ref[:, :KV_WIDTH]), placed(kvc_ref[:, :KV_WIDTH]), placed(kvm_ref[:, :KV_WIDTH])
    vp, vc, vm = placed(kvp_ref[:, KV_WIDTH:]), placed(kvc_ref[:, KV_WIDTH:]), placed(kvm_ref[:, KV_WIDTH:])
    pad_rows = w - 2 * N_META
    zpad = jnp.zeros((pad_rows, w), F32)
    krow = lax.broadcasted_iota(jnp.int32, (5 * w, w), 0)
    klane = lax.broadcasted_iota(jnp.int32, (5 * w, w), 1)
    row_l = (krow < 2 * w) | ((krow >= 4 * w) & (krow < 4 * w + N_META))
    row_r = ((krow >= 2 * w) & (krow < 4 * w)) | ((krow >= 4 * w + N_META) & (krow < 4 * w + 2 * N_META))
    den_cols = jnp.where((row_l & (klane < hd)) | (row_r & (klane >= hd)), 1.0, 0.0)

    for j in range(N_KV_HEADS):
        kcat = jnp.concatenate(
            [kp[j, 0], kc[j, 0], kp[j, 1], kc[j, 1], km[j, 0], km[j, 1], zpad], axis=0).astype(BF16)
        vcat = jnp.concatenate(
            [jnp.concatenate([vp[j, 0], vc[j, 0], vp[j, 1], vc[j, 1], vm[j, 0], vm[j, 1], zpad], axis=0),
             den_cols], axis=1).astype(BF16)
        for r in range(KV_REP // 2):
            pr = j * (KV_REP // 2) + r
            s = _dot_nt(q_ref[:, pr * w:(pr + 1) * w], kcat)
            s_l = jnp.where(vis_cur, s[:, w:2 * w], jnp.where(vis_prev, s[:, 0:w], NEG_INF))
            s_r = jnp.where(vis_cur, s[:, 3 * w:4 * w], jnp.where(vis_prev, s[:, 2 * w:3 * w], NEG_INF))
            s_m = s[:, 4 * w:]
            sink_l, sink_r = sink_ref[2 * pr] * LOG2_E, sink_ref[2 * pr + 1] * LOG2_E
            m_l = jnp.maximum(jnp.max(jnp.maximum(s_l, jnp.where(meta_l, s_m, NEG_INF)),
                                      axis=1, keepdims=True), sink_l)
            m_r = jnp.maximum(jnp.max(jnp.maximum(s_r, jnp.where(meta_r, s_m, NEG_INF)),
                                      axis=1, keepdims=True), sink_r)
            s_m = jnp.where(meta_l, s_m - m_l, jnp.where(meta_r, s_m - m_r, NEG_INF))
            e_l, e_r = jnp.exp2(s_l - m_l), jnp.exp2(s_r - m_r)
            e = jnp.concatenate([jnp.where(vis_cur, 0.0, e_l), jnp.where(vis_cur, e_l, 0.0),
                                 jnp.where(vis_cur, 0.0, e_r), jnp.where(vis_cur, e_r, 0.0),
                                 jnp.exp2(s_m)], axis=1).astype(BF16)
            acc = _dot(e, vcat)
            den = acc[:, w:] + jnp.where(left, jnp.exp2(sink_l - m_l), jnp.exp2(sink_r - m_r))
            o_ref[:, pr * w:(pr + 1) * w] = (acc[:, :w] / den).astype(BF16)


def _attention(sinks, q, kv, kv_meta, blocks_per_seq):
    n = q.shape[0]
    return pl.pallas_call(
        functools.partial(_attn_kernel, blocks_per_seq=blocks_per_seq),
        grid=(n // WINDOW,),
        in_specs=[
            pl.BlockSpec(memory_space=pltpu.SMEM),
            pl.BlockSpec((WINDOW, Q_WIDTH), lambda g: (g, 0)),
            pl.BlockSpec((WINDOW, 2 * KV_WIDTH), lambda g: (g, 0)),
            pl.BlockSpec((WINDOW, 2 * KV_WIDTH), lambda g: (jnp.maximum(g - 1, 0), 0)),
            pl.BlockSpec((N_META, 2 * KV_WIDTH), lambda g: (0, 0)),
        ],
        out_specs=pl.BlockSpec((WINDOW, Q_WIDTH), lambda g: (g, 0)),
        out_shape=jax.ShapeDtypeStruct((n, Q_WIDTH), BF16),
        compiler_params=pltpu.CompilerParams(dimension_semantics=("arbitrary",)),
        name="attn",
    )(sinks, q, kv, kv, kv_meta)


_ROUTE_IDX, _ROUTE_RANK, _ROUTE_GATE, _ROUTE_ROWS = 0, TOP_K, 2 * TOP_K, 16


def _mix_kernel(x_ref, y_ref, at_ref, gs_ref, ga_ref, wglu_ref, wo_ref, wout_ref, fg_ref, rw_ref, rb_ref,
                h1_ref, hf_ref, route_ref, cnt_ref, cnt_scr):
    tm = x_ref.shape[0]

    @pl.when(pl.program_id(0) == 0)
    def _():
        cnt_scr[...] = jnp.zeros_like(cnt_scr)

    y_ssm = jnp.concatenate([y_ref[c * SSM_PITCH:c * SSM_PITCH + SSM_CHUNK, :]
                             for c in range(tm // SSM_CHUNK)], axis=0)
    glu = _dot(jax.nn.gelu(y_ssm).astype(BF16), wglu_ref[...])
    branch_ssm = glu[:, :D_MODEL] * jax.nn.sigmoid(glu[:, D_MODEL:])
    branch_attn = _dot(at_ref[...], wo_ref[...])
    merged = gs_ref[...].astype(F32) * branch_ssm + ga_ref[...].astype(F32) * branch_attn
    h1 = x_ref[...] + _dot(merged.astype(BF16), wout_ref[...])
    h1_ref[...] = h1
    ms = jnp.mean(h1 * h1, axis=-1, keepdims=True)
    hf = h1 * lax.rsqrt(ms + RMS_EPS) * fg_ref[...]
    _store_token_tiles(hf_ref, hf)

    hf_hi = hf.astype(BF16)
    hf_lo = (hf - hf_hi.astype(F32)).astype(BF16)
    hi_prod = _dot(hf_hi, rw_ref[...])
    logits = (hi_prod[:, :LANES] + (hi_prod[:, LANES:] + _dot(hf_lo, rw_ref[:, :LANES]))
              + rb_ref[...])
    lt = logits.T[:N_EXPERTS, :]
    erow = lax.broadcasted_iota(jnp.int32, (N_EXPERTS, tm), 0)
    vals, idxs, hots = [], [], []
    rest = lt
    for _ in range(TOP_K):
        m = jnp.max(rest, axis=0, keepdims=True)
        first = jnp.min(jnp.where(rest == m, erow, N_EXPERTS), axis=0, keepdims=True)
        hot = erow == first
        vals.append(m)
        idxs.append(first)
        hots.append(hot)
        rest = jnp.where(hot, -jnp.inf, rest)
    exps = [jnp.exp(v - vals[0]) for v in vals]
    tot = exps[0] + exps[1] + exps[2] + exps[3]

    sel = (hots[0] | hots[1] | hots[2] | hots[3]).astype(F32)
    ti = lax.broadcasted_iota(jnp.int32, (tm, tm), 0)
    tj = lax.broadcasted_iota(jnp.int32, (tm, tm), 1)
    earlier = (ti < tj).astype(BF16)
    rank_e = _dot(sel.astype(BF16), earlier) + cnt_scr[...]
    cnt_scr[...] = cnt_scr[...] + jnp.sum(sel, axis=1, keepdims=True)
    cnt_ref[...] = cnt_scr[...]

    rrow = lax.broadcasted_iota(jnp.int32, (_ROUTE_ROWS, tm), 0)
    route = jnp.zeros((_ROUTE_ROWS, tm), F32)
    for k in range(TOP_K):
        rank_k = jnp.sum(jnp.where(hots[k], rank_e, 0.0), axis=0, keepdims=True)
        route = jnp.where(rrow == _ROUTE_IDX + k, idxs[k].astype(F32), route)
        route = jnp.where(rrow == _ROUTE_RANK + k, rank_k, route)
        route = jnp.where(rrow == _ROUTE_GATE + k, exps[k] / tot, route)
    route_ref[...] = route


def _mix(x2, y, attn, gs, ga, wglu, wo, wout, fg, rw, rb, tm):
    n = x2.shape[0]
    rw = jnp.pad(rw, ((0, 0), (0, LANES - N_EXPERTS)))
    rb = jnp.pad(rb, ((0, 0), (0, LANES - N_EXPERTS)), constant_values=NEG_INF)
    rw_hi = rw.astype(BF16)
    rw_split = jnp.concatenate([rw_hi, (rw - rw_hi.astype(F32)).astype(BF16)], axis=1)
    row = lambda w: pl.BlockSpec((tm, w), lambda i: (i, 0))
    full = lambda a: pl.BlockSpec(a.shape, lambda i: (0,) * a.ndim)
    return pl.pallas_call(
        _mix_kernel,
        grid=(n // tm,),
        in_specs=[row(D_MODEL), pl.BlockSpec((tm // SSM_CHUNK * SSM_PITCH, SSM_WIDTH), lambda i: (i, 0)),
                  row(Q_WIDTH), row(D_MODEL), row(D_MODEL),
                  full(wglu), full(wo), full(wout), full(fg), full(rw_split), full(rb)],
        out_specs=[row(D_MODEL), pl.BlockSpec((tm * TILE_ROWS, LANES), lambda i: (i, 0)),
                   pl.BlockSpec((_ROUTE_ROWS, tm), lambda i: (0, i)),
                   pl.BlockSpec((N_EXPERTS, 1), lambda i: (0, 0))],
        out_shape=[
            jax.ShapeDtypeStruct((n, D_MODEL), F32),
            jax.ShapeDtypeStruct((n * TILE_ROWS, LANES), F32),
            jax.ShapeDtypeStruct((_ROUTE_ROWS, n), F32),
            jax.ShapeDtypeStruct((N_EXPERTS, 1), F32),
        ],
        scratch_shapes=[pltpu.VMEM((N_EXPERTS, 1), F32)],
        compiler_params=pltpu.CompilerParams(
            dimension_semantics=("arbitrary",), vmem_limit_bytes=VMEM_LIMIT),
        name="mix_router",
    )(x2, y, attn, gs, ga, wglu, wo, wout, fg, rw_split, rb)


def _tiles_wait_copy(src_hbm, dst, n_tiles, sem):
    rows = n_tiles * TILE_ROWS
    return pltpu.make_async_copy(src_hbm.at[pl.ds(0, rows), :], dst.at[pl.ds(0, rows), :], sem)


_ISSUE_UNROLL = 16


_DISPATCH_RING = 4


def _dispatch_kernel(dst_ref, pad_start_ref, pad_len_ref, nu_ref, hf_hbm, xs_hbm, zero_blk, ring, in_sem,
                     out_sem, pad_sem, *, n_steps):
    i = pl.program_id(0)
    last = n_steps - 1
    pairs = dst_ref.shape[0]
    tokens = pairs // TOP_K
    blk_rows = tokens * TILE_ROWS
    n_blocks = xs_hbm.shape[0] // (EXPERT_ROWS * TILE_ROWS)
    slot = i % _DISPATCH_RING

    def load(step, s):
        src = hf_hbm.at[pl.ds(pl.multiple_of(step * blk_rows, blk_rows), blk_rows), :]
        return pltpu.make_async_copy(src, ring.at[s], in_sem.at[s])

    def wait_copies(s):
        for _ in range(TOP_K):
            pltpu.make_async_copy(ring.at[s], xs_hbm.at[pl.ds(0, blk_rows), :], out_sem.at[s]).wait()

    def zero_copy(row, rows):
        src = zero_blk.at[pl.ds(0, rows * TILE_ROWS), :]
        dst = xs_hbm.at[pl.ds(pl.multiple_of(row * TILE_ROWS, TILE_ROWS), rows * TILE_ROWS), :]
        return pltpu.make_async_copy(src, dst, pad_sem)

    def for_each_pad(fn):
        def per_expert(e, carry):
            row, left = pad_start_ref[e], pad_len_ref[e]
            size = EXPERT_ROWS // 2
            while size >= 1:
                take = left & size

                @pl.when(take != 0)
                def _(row=row, size=size):
                    fn(zero_copy(row, size))

                row = row + take
                size //= 2
            return carry
        lax.fori_loop(0, N_EXPERTS, per_expert, 0)

        def per_block(b, carry):
            fn(zero_copy(b * EXPERT_ROWS, EXPERT_ROWS))
            return carry
        lax.fori_loop(nu_ref[0], n_blocks, per_block, 0)

    @pl.when(i == 0)
    def _():
        for s in range(min(2, n_steps)):
            load(s, s).start()
        zero_blk[...] = jnp.zeros_like(zero_blk)
        for_each_pad(lambda cp: cp.start())

    @pl.when(i >= 2)
    def _():
        wait_copies((i + 2) % _DISPATCH_RING)

    @pl.when(i + 2 < n_steps)
    def _():
        load(i + 2, (i + 2) % _DISPATCH_RING).start()

    load(i, slot).wait()
    src_blk = ring.at[slot]
    for k in range(TOP_K):
        def issue(o, carry, k=k):
            tok0 = pl.multiple_of(o * _ISSUE_UNROLL, _ISSUE_UNROLL)
            dsts = [dst_ref[k * tokens + tok0 + r] for r in range(_ISSUE_UNROLL)]
            for r in range(_ISSUE_UNROLL):
                pltpu.make_async_copy(_token_tile(src_blk, tok0 + r), _token_tile(xs_hbm, dsts[r]),
                                      out_sem.at[slot]).start(priority=r % 2)
            return carry

        lax.fori_loop(0, tokens // _ISSUE_UNROLL, issue, 0)

    @pl.when(i == last)
    def _():
        if n_steps > 1:
            wait_copies((i + _DISPATCH_RING - 1) % _DISPATCH_RING)
        wait_copies(slot)
        for_each_pad(lambda cp: cp.wait())


def _dispatch(dest, pad_start, pad_len, n_used, hf_tiles, n_rows, tokens_per_step):
    n = dest.shape[0] // TOP_K
    pairs = tokens_per_step * TOP_K
    assert n % tokens_per_step == 0 and tokens_per_step % _ISSUE_UNROLL == 0
    smem = lambda: pl.BlockSpec(memory_space=pltpu.SMEM)
    n_steps = n // tokens_per_step
    return pl.pallas_call(
        functools.partial(_dispatch_kernel, n_steps=n_steps),
        grid=(n_steps,),
        in_specs=[pl.BlockSpec((pairs,), lambda i: (i,), memory_space=pltpu.SMEM), smem(), smem(), smem(),
                  pl.BlockSpec(memory_space=pl.ANY)],
        out_specs=pl.BlockSpec(memory_space=pl.ANY),
        out_shape=jax.ShapeDtypeStruct((n_rows * TILE_ROWS, LANES), F32),
        scratch_shapes=[pltpu.VMEM((EXPERT_ROWS * TILE_ROWS, LANES), F32),
                        pltpu.VMEM((_DISPATCH_RING, tokens_per_step * TILE_ROWS, LANES), F32),
                        pltpu.SemaphoreType.DMA((_DISPATCH_RING,)), pltpu.SemaphoreType.DMA((_DISPATCH_RING,)),
                        pltpu.SemaphoreType.DMA],
        compiler_params=pltpu.CompilerParams(
            dimension_semantics=("arbitrary",), vmem_limit_bytes=VMEM_LIMIT),
        name="dispatch",
    )(dest, pad_start, pad_len, n_used, hf_tiles)


def _expert_kernel(be_ref, nu_ref, nxt_ref, par_ref, val_ref, xs_ref, wu_hbm, bu_ref, wd_hbm, bd_ref, y_ref,
                   wu_f32, wd_f32, wu_bf, wd_bf, wsem):
    i = pl.program_id(0)

    def mlp(xb):
        up = _dot(xb, wu_bf[...]) + bu_ref[0]
        x_glu = jnp.minimum(up[:, :D_FF], SWIGLU_LIMIT)
        x_lin = jnp.clip(up[:, D_FF:], -SWIGLU_LIMIT, SWIGLU_LIMIT)
        act = x_glu * jax.nn.sigmoid(SWIGLU_ALPHA * x_glu) * (x_lin + 1.0)
        return _dot(act.astype(BF16), wd_bf[...]) + bd_ref[0]

    def weight_copies(expert, s):
        return (pltpu.make_async_copy(wu_hbm.at[expert], wu_f32.at[s], wsem.at[0, s]),
                pltpu.make_async_copy(wd_hbm.at[expert], wd_f32.at[s], wsem.at[1, s]))

    @pl.when(i == 0)
    def _():
        for cp in weight_copies(be_ref[0], par_ref[0]):
            cp.start()

    @pl.when((i == 0) | (be_ref[i] != be_ref[jnp.maximum(i - 1, 0)]))
    def _():
        s = par_ref[i]
        for cp in weight_copies(be_ref[i], s):
            cp.wait()

        @pl.when(nxt_ref[i] >= 0)
        def _():
            for cp in weight_copies(nxt_ref[i], 1 - s):
                cp.start()

        wu_bf[...] = wu_f32[s].astype(BF16)
        wd_bf[...] = wd_f32[s].astype(BF16)

    used = i < nu_ref[0]
    full = val_ref[i] > EXPERT_ROWS - EXPERT_SUB_ROWS

    @pl.when(used & full)
    def _():
        _store_token_tiles(y_ref, mlp(_load_token_tiles(xs_ref, 0, EXPERT_ROWS).astype(BF16)))

    @pl.when(used & jnp.logical_not(full))
    def _():
        sub = EXPERT_SUB_ROWS
        n_sub = lax.shift_right_logical(val_ref[i] + (sub - 1), sub.bit_length() - 1)

        def compute(j, carry):
            row0 = pl.multiple_of(j * sub, sub)
            _store_token_tiles(y_ref, mlp(_load_token_tiles(xs_ref, row0, sub).astype(BF16)), row0)
            return carry

        def clear(j, carry):
            row0 = pl.multiple_of(j * sub * TILE_ROWS, sub * TILE_ROWS)
            y_ref[pl.ds(row0, sub * TILE_ROWS), :] = jnp.zeros((sub * TILE_ROWS, LANES), F32)
            return carry

        lax.fori_loop(0, n_sub, compute, 0)
        lax.fori_loop(n_sub, EXPERT_ROWS // sub, clear, 0)

    @pl.when(jnp.logical_not(used))
    def _():
        y_ref[...] = jnp.zeros_like(y_ref)


def _experts(block_expert, n_used, next_expert, parity, valid_rows, xs_tiles, w_up, b_up, w_down, b_down):
    n_blocks = block_expert.shape[0]
    blk = (EXPERT_ROWS * TILE_ROWS, LANES)
    grid_spec = pltpu.PrefetchScalarGridSpec(
        num_scalar_prefetch=5,
        grid=(n_blocks,),
        in_specs=[
            pl.BlockSpec(blk, lambda i, be, nu, *_: (jnp.minimum(i, nu[0] - 1), 0)),
            pl.BlockSpec(memory_space=pl.ANY),
            pl.BlockSpec((1, 1, 2 * D_FF), lambda i, be, *_: (be[i], 0, 0)),
            pl.BlockSpec(memory_space=pl.ANY),
            pl.BlockSpec((1, 1, D_MODEL), lambda i, be, *_: (be[i], 0, 0)),
        ],
        out_specs=pl.BlockSpec(blk, lambda i, *_: (i, 0)),
        scratch_shapes=[
            pltpu.VMEM((2, D_MODEL, 2 * D_FF), F32),
            pltpu.VMEM((2, D_FF, D_MODEL), F32),
            pltpu.VMEM((D_MODEL, 2 * D_FF), BF16),
            pltpu.VMEM((D_FF, D_MODEL), BF16),
            pltpu.SemaphoreType.DMA((2, 2)),
        ],
    )
    return pl.pallas_call(
        _expert_kernel,
        grid_spec=grid_spec,
        out_shape=jax.ShapeDtypeStruct((n_blocks * blk[0], LANES), F32),
        compiler_params=pltpu.CompilerParams(
            dimension_semantics=("arbitrary",), vmem_limit_bytes=VMEM_LIMIT),
        name="experts",
    )(block_expert, n_used, next_expert, parity, valid_rows, xs_tiles,
      w_up, b_up[:, None, :], w_down, b_down[:, None, :])


_COMBINE_RING = 3


def _combine_kernel(dst0_ref, dst1_ref, dst2_ref, y_hbm, h1_ref, route_ref, g_ref, o_ref, *scratch):
    bufs, sem = scratch[:_COMBINE_RING], scratch[_COMBINE_RING]
    i = pl.program_id(0)
    last = pl.num_programs(0) - 1
    tm = h1_ref.shape[0]
    rows = TOP_K * tm

    def gather_group(idx_ref, s, row0):
        srcs = [idx_ref[row0 + r] for r in range(_ISSUE_UNROLL)]
        for r in range(_ISSUE_UNROLL):
            pltpu.make_async_copy(_token_tile(y_hbm, srcs[r]), _token_tile(bufs[s], row0 + r),
                                  sem.at[s]).start(priority=r % 2)

    @pl.when(i == 0)
    def _():
        for s, idx_ref in ((0, dst0_ref), (1, dst1_ref)):
            def body(o, carry, s=s, idx_ref=idx_ref):
                gather_group(idx_ref, s, pl.multiple_of(o * _ISSUE_UNROLL, _ISSUE_UNROLL))
                return carry
            lax.fori_loop(0, rows // _ISSUE_UNROLL, body, 0)

    def step(s):
        cur = bufs[s]
        ahead = (s + 2) % _COMBINE_RING
        _tiles_wait_copy(y_hbm, cur, rows, sem.at[s]).wait()
        for g in range(rows // _ISSUE_UNROLL):
            gather_group(dst2_ref, ahead, g * _ISSUE_UNROLL)
        assert tm == LANES
        rt = jnp.concatenate([route_ref[...], jnp.zeros((LANES - _ROUTE_ROWS, tm), F32)], axis=0).T
        acc = h1_ref[...]
        for k in range(TOP_K):
            gate = rt[:, _ROUTE_GATE + k:_ROUTE_GATE + k + 1]
            acc = acc + gate * _load_token_tiles(cur, k * tm, tm)
        ms = jnp.mean(acc * acc, axis=-1, keepdims=True)
        o_ref[...] = acc * lax.rsqrt(ms + RMS_EPS) * g_ref[...]

        @pl.when(i == last)
        def _():
            for t in ((s + 1) % _COMBINE_RING, ahead):
                _tiles_wait_copy(y_hbm, bufs[t], rows, sem.at[t]).wait()

    for s in range(_COMBINE_RING):
        pl.when(i % _COMBINE_RING == s)(functools.partial(step, s))


def _combine(dest_kmajor, y, h1, route, g, tm):
    n = h1.shape[0]
    n_tiles = n // tm
    idx_spec = lambda ahead: pl.BlockSpec(
        (TOP_K * tm,), lambda i: (jnp.minimum(i + ahead, n_tiles - 1),), memory_space=pltpu.SMEM)
    ring_buf = pltpu.VMEM((TOP_K * tm * TILE_ROWS, LANES), F32)
    return pl.pallas_call(
        _combine_kernel,
        grid=(n_tiles,),
        in_specs=[
            idx_spec(0), idx_spec(1), idx_spec(2),
            pl.BlockSpec(memory_space=pl.ANY),
            pl.BlockSpec((tm, D_MODEL), lambda i: (i, 0)),
            pl.BlockSpec((_ROUTE_ROWS, tm), lambda i: (0, i)),
            pl.BlockSpec((1, D_MODEL), lambda i: (0, 0)),
        ],
        out_specs=pl.BlockSpec((tm, D_MODEL), lambda i: (i, 0)),
        out_shape=jax.ShapeDtypeStruct((n, D_MODEL), F32),
        scratch_shapes=[ring_buf] * _COMBINE_RING + [pltpu.SemaphoreType.DMA((_COMBINE_RING,))],
        compiler_params=pltpu.CompilerParams(
            dimension_semantics=("arbitrary",), vmem_limit_bytes=VMEM_LIMIT),
        name="combine",
    )(dest_kmajor, dest_kmajor, dest_kmajor, y, h1, route, g)


def _routing_tables(route, counts, n, tokens_dispatch, tokens_combine):
    tm = EXPERT_ROWS
    i32 = jnp.int32
    n_blocks = (n * TOP_K + N_EXPERTS * (tm - 1)) // tm
    idx = route[_ROUTE_IDX:_ROUTE_IDX + TOP_K].astype(i32)
    rank = route[_ROUTE_RANK:_ROUTE_RANK + TOP_K].astype(i32)
    cnt = counts[:, 0].astype(i32)
    eid = jnp.arange(N_EXPERTS, dtype=i32)
    upto = eid[None, :] <= eid[:, None]
    blocks_e = (cnt + tm - 1) // tm
    blocks_end = jnp.sum(jnp.where(upto, blocks_e[None, :], 0), axis=1)
    row_start = (blocks_end - blocks_e) * tm
    n_used = blocks_end[N_EXPERTS - 1]
    used = blocks_e > 0

    def lookup(table, keys):
        hit = keys[None] == eid.reshape((N_EXPERTS,) + (1,) * keys.ndim)
        return jnp.sum(jnp.where(hit, table.reshape((N_EXPERTS,) + (1,) * keys.ndim), 0), axis=0)

    dest = lookup(row_start, idx) + rank
    blk = jnp.arange(n_blocks, dtype=i32)
    last_used = jnp.max(jnp.where(used, eid, 0))
    be = jnp.where(blk < n_used, jnp.sum((blocks_end[None, :] <= blk[:, None]).astype(i32), axis=1), last_used)
    later_used = used[None, :] & (eid[None, :] > eid[:, None])
    after = jnp.min(jnp.where(later_used, eid[None, :], N_EXPERTS), axis=1)
    next_e = jnp.where(after < N_EXPERTS, after, -1)
    parity_e = (jnp.sum((upto & used[None, :]).astype(i32), axis=1) - 1) % 2

    def k_major(tokens):
        return dest.reshape(TOP_K, n // tokens, tokens).transpose(1, 0, 2).reshape(-1)

    valid = jnp.clip(lookup(row_start + cnt, be) - blk * tm, 0, tm)
    return (be.astype(i32), n_used.reshape(1), lookup(next_e, be).astype(i32), lookup(parity_e, be).astype(i32),
            valid.astype(i32), k_major(tokens_dispatch), k_major(tokens_combine), row_start + cnt,
            blocks_e * tm - cnt, n_blocks * tm)


def kernel(x, meta_tokens, mix_norm_g, w_in, ssm_a_re, ssm_a_im, ssm_log_dt, ssm_b_re, ssm_b_im,
           ssm_c_re, ssm_c_im, ssm_d, w_ssm_glu, attn_sinks, w_attn_o, w_out, ffn_norm_g,
           router_w, router_b, w_up, b_up, w_down, b_down, final_norm_g):
    bsz, seq, d = x.shape
    assert d == D_MODEL and seq % max(WINDOW, SSM_CHUNK) == 0
    assert mix_norm_g.shape[0] == 1, "single-layer trunk"
    n = bsz * seq
    tm_proj = min(1024, n)
    tm_mix = min(512, n)
    tm_comb = min(128, n)
    x2 = x.reshape(n, D_MODEL)

    w_in_bf = w_in[0].astype(BF16)
    g_mix = mix_norm_g[0][None, :]
    u, q, kv, gs, ga = _in_proj(x2, g_mix, w_in_bf, tm_proj, pitched=True)
    u_m, _, kv_m, _, _ = _in_proj(meta_tokens, g_mix, w_in_bf, N_META, pitched=False)

    ssm_par = _ssm_params(ssm_a_re[0], ssm_a_im[0], ssm_log_dt[0], ssm_b_re[0], ssm_b_im[0],
                          ssm_c_re[0], ssm_c_im[0], ssm_d[0], SSM_CHUNK)
    y_ssm = _ssm(u, u_m, *ssm_par, batch=bsz, chunk=SSM_CHUNK)

    attn = _attention(attn_sinks[0], q, kv, kv_m, seq // WINDOW)

    h1, hf, route, counts = _mix(
        x2, y_ssm, attn, gs, ga, w_ssm_glu[0].astype(BF16), w_attn_o[0].astype(BF16),
        w_out[0].astype(BF16), ffn_norm_g[0][None, :], router_w[0], router_b[0][None, :], tm_mix)

    tok_disp = min(1024, n)
    be, n_used, next_e, parity, valid, dest_disp, dest_comb, pad_start, pad_len, n_rows = _routing_tables(
        route, counts, n, tok_disp, tm_comb)
    xs = _dispatch(dest_disp, pad_start, pad_len, n_used, hf, n_rows, tok_disp)
    y = _experts(be, n_used, next_e, parity, valid, xs, w_up[0], b_up[0], w_down[0], b_down[0])
    out = _combine(dest_comb, y, h1, route, final_norm_g[None, :], tm_comb)
    return out.reshape(bsz, seq, D_MODEL)
```

```python
import functools
import math

import jax
import jax.numpy as jnp
from jax import lax
from jax.experimental import pallas as pl
from jax.experimental.pallas import tpu as pltpu

F32 = jnp.float32
BF16 = jnp.bfloat16

D_MODEL = 1024
N_META = 16
SSM_WIDTH = 512
SSM_GROUP = 16
SSM_GROUPS = 32
SSM_STATE = 64
HEAD_DIM = 64
N_HEADS = 16
N_KV_HEADS = 2
KV_REP = N_HEADS // N_KV_HEADS
WINDOW = 128
Q_WIDTH = N_HEADS * HEAD_DIM
KV_WIDTH = N_KV_HEADS * HEAD_DIM
N_EXPERTS = 32
TOP_K = 4
D_FF = 1024
SWIGLU_ALPHA = 1.702
SWIGLU_LIMIT = 7.0
RMS_EPS = 1e-5
NEG_INF = -1e30

_U0, _Q0, _KV0, _GS0, _GA0, _IN_END = 0, 512, 1536, 1792, 2816, 3840

SSM_CH_BLOCK = 128
SSM_HALF = (SSM_CH_BLOCK // SSM_GROUP) * SSM_STATE
SSM_CHUNK = 32
SSM_PITCH = 40
LOG2_E = math.log2(math.e)
EXPERT_ROWS = 1024
EXPERT_SUB_ROWS = 128
VMEM_LIMIT = 56 * 1024 * 1024


def _dot(a, b):
    return jnp.dot(a, b, preferred_element_type=F32)


def _dot_nt(a, b):
    return lax.dot_general(a, b, (((1,), (1,)), ((), ())), preferred_element_type=F32)


LANES = 128
TILE_ROWS = D_MODEL // LANES


def _store_token_tiles(ref, x, start_row=0):
    rows = x.shape[0]
    for j in range(TILE_ROWS):
        ref[pl.ds(start_row * TILE_ROWS + j, rows, stride=TILE_ROWS), :] = x[:, j * LANES:(j + 1) * LANES]


def _load_token_tiles(ref, start_row, rows):
    return jnp.concatenate(
        [ref[pl.ds(start_row * TILE_ROWS + j, rows, stride=TILE_ROWS), :] for j in range(TILE_ROWS)], axis=1)


def _token_tile(ref, row):
    return ref.at[pl.ds(pl.multiple_of(row * TILE_ROWS, TILE_ROWS), TILE_ROWS), :]


def _in_proj_kernel(x_ref, g_ref, w_ref, u_ref, q_ref, kv_ref, gs_ref, ga_ref, *, pitched):
    x = x_ref[...]
    ms = jnp.mean(x * x, axis=-1, keepdims=True)
    hn = (x * lax.rsqrt(ms + RMS_EPS) * g_ref[...]).astype(BF16)
    u = _dot(hn, w_ref[:, _U0:_Q0])
    if pitched:
        for c in range(u.shape[0] // SSM_CHUNK):
            u_ref[c * SSM_PITCH:c * SSM_PITCH + SSM_CHUNK, :] = u[c * SSM_CHUNK:(c + 1) * SSM_CHUNK, :]
            u_ref[c * SSM_PITCH + SSM_CHUNK:(c + 1) * SSM_PITCH, :] = jnp.zeros(
                (SSM_PITCH - SSM_CHUNK, SSM_WIDTH), F32)
    else:
        u_ref[...] = u
    q_ref[...] = (_dot(hn, w_ref[:, _Q0:_KV0]) * (HEAD_DIM ** -0.5 * LOG2_E)).astype(BF16)
    kv_ref[...] = _dot(hn, w_ref[:, _KV0:_GS0]).astype(BF16)
    gs_ref[...] = jax.nn.sigmoid(_dot(hn, w_ref[:, _GS0:_GA0])).astype(BF16)
    ga_ref[...] = jax.nn.sigmoid(_dot(hn, w_ref[:, _GA0:_IN_END])).astype(BF16)


def _in_proj(x2, g, w_bf, tm, pitched):
    n = x2.shape[0]
    row = lambda w: pl.BlockSpec((tm, w), lambda i: (i, 0))
    full = lambda a: pl.BlockSpec(a.shape, lambda i: (0,) * a.ndim)
    u_rows = (lambda r: r // SSM_CHUNK * SSM_PITCH) if pitched else (lambda r: r)
    return pl.pallas_call(
        functools.partial(_in_proj_kernel, pitched=pitched),
        grid=(n // tm,),
        in_specs=[row(D_MODEL), full(g), full(w_bf)],
        out_specs=[pl.BlockSpec((u_rows(tm), SSM_WIDTH), lambda i: (i, 0)),
                   row(Q_WIDTH), row(2 * KV_WIDTH), row(D_MODEL), row(D_MODEL)],
        out_shape=[
            jax.ShapeDtypeStruct((u_rows(n), SSM_WIDTH), F32),
            jax.ShapeDtypeStruct((n, Q_WIDTH), BF16),
            jax.ShapeDtypeStruct((n, 2 * KV_WIDTH), BF16),
            jax.ShapeDtypeStruct((n, D_MODEL), BF16),
            jax.ShapeDtypeStruct((n, D_MODEL), BF16),
        ],
        compiler_params=pltpu.CompilerParams(
            dimension_semantics=("arbitrary",), vmem_limit_bytes=VMEM_LIMIT),
        name="in_proj",
    )(x2, g, w_bf)


def _ssm_kernel(u_ref, um_ref, bm_ref, cm_ref, ar_ref, ai_ref, atr_ref, ati_ref, d_ref,
                y_ref, sre, sim, *, chunk, rows, batch):
    h = SSM_HALF
    bm = bm_ref[0]
    cm = cm_ref[0]
    ar, ai = ar_ref[0], ai_ref[0]
    atr, ati = atr_ref[0], ati_ref[0]
    dsk = d_ref[0]
    n_chunks = rows // batch

    def advance(sr, si, bu):
        return ar * sr - ai * si + bu[:, :h], ar * si + ai * sr + bu[:, h:]

    bum = _dot(um_ref[...].astype(BF16), bm)
    mr = jnp.zeros((1, h), F32)
    mi = jnp.zeros((1, h), F32)
    for j in range(N_META):
        mr, mi = advance(mr, mi, bum[j:j + 1, :])

    def u_step(t):
        return u_ref[pl.ds(t, rows, stride=SSM_PITCH), :]

    for pad in range(chunk, SSM_PITCH):
        y_ref[pl.ds(pad, rows, stride=SSM_PITCH), :] = jnp.zeros((rows, SSM_CH_BLOCK), F32)

    sre[...] = jnp.zeros_like(sre)
    sim[...] = jnp.zeros_like(sim)

    def pass_a(t, carry):
        bu = _dot(u_step(t).astype(BF16), bm)
        nr, ni = advance(sre[...], sim[...], bu)
        sre[...] = nr
        sim[...] = ni
        return carry

    lax.fori_loop(0, chunk, pass_a, 0)

    def over_chunks(c, carry):
        new = []
        for b in range(batch):
            cr, ci = carry[2 * b], carry[2 * b + 1]
            row = pl.ds(b * n_chunks + c, 1)
            er, ei = sre[row, :], sim[row, :]
            sre[row, :] = cr
            sim[row, :] = ci
            new += [atr * cr - ati * ci + er, atr * ci + ati * cr + ei]
        return tuple(new)

    lax.fori_loop(0, n_chunks, over_chunks, (mr, mi) * batch)

    def pass_b(t, carry):
        ut = u_step(t)
        bu = _dot(ut.astype(BF16), bm)
        nr, ni = advance(sre[...], sim[...], bu)
        sre[...] = nr
        sim[...] = ni
        y = _dot(nr.astype(BF16), cm[:h, :]) + _dot(ni.astype(BF16), cm[h:, :]) + dsk * ut
        y_ref[pl.ds(t, rows, stride=SSM_PITCH), :] = y
        return carry

    lax.fori_loop(0, chunk, pass_b, 0)


def _ssm(u, u_meta, bmat, cmat, a_re, a_im, at_re, at_im, dskip, batch, chunk):
    n = u.shape[0]
    rows = n // SSM_PITCH
    nblk = SSM_WIDTH // SSM_CH_BLOCK
    col = lambda r: pl.BlockSpec((r, SSM_CH_BLOCK), lambda j: (0, j))
    par = lambda a: pl.BlockSpec((1,) + a.shape[1:], lambda j: (j, 0, 0))
    return pl.pallas_call(
        functools.partial(_ssm_kernel, chunk=chunk, rows=rows, batch=batch),
        grid=(nblk,),
        in_specs=[col(n), col(N_META), par(bmat), par(cmat), par(a_re), par(a_im),
                  par(at_re), par(at_im), par(dskip)],
        out_specs=col(n),
        out_shape=jax.ShapeDtypeStruct((n, SSM_WIDTH), F32),
        scratch_shapes=[pltpu.VMEM((rows, SSM_HALF), F32), pltpu.VMEM((rows, SSM_HALF), F32)],
        compiler_params=pltpu.CompilerParams(
            dimension_semantics=("arbitrary",), vmem_limit_bytes=VMEM_LIMIT),
        name="ssm",
    )(u, u_meta, bmat, cmat, a_re, a_im, at_re, at_im, dskip)


def _ssm_params(a_re, a_im, log_dt, b_re, b_im, c_re, c_im, d_skip, chunk):
    dt = jnp.exp(log_dt)[:, None]
    mag = jnp.exp(a_re * dt)
    ang = a_im * dt
    abar_re, abar_im = mag * jnp.cos(ang), mag * jnp.sin(ang)
    den = a_re * a_re + a_im * a_im
    nr, ni = abar_re - 1.0, abar_im
    coef_re = ((nr * a_re + ni * a_im) / den)[..., None]
    coef_im = ((ni * a_re - nr * a_im) / den)[..., None]
    bbar_re = coef_re * b_re - coef_im * b_im
    bbar_im = coef_re * b_im + coef_im * b_re
    magt = jnp.exp(a_re * dt * chunk)
    at_re, at_im = magt * jnp.cos(ang * chunk), magt * jnp.sin(ang * chunk)

    nblk = SSM_WIDTH // SSM_CH_BLOCK
    gpb = SSM_GROUPS // nblk
    eye = jnp.eye(gpb, dtype=F32)

    def in_map(b):
        b = b.reshape(nblk, gpb, SSM_STATE, SSM_GROUP)
        return jnp.einsum('jgpc,gh->jgchp', b, eye).reshape(nblk, SSM_CH_BLOCK, gpb * SSM_STATE)

    def out_map(c):
        c = c.reshape(nblk, gpb, SSM_GROUP, SSM_STATE)
        return jnp.einsum('jgcp,gh->jgphc', c, eye).reshape(nblk, gpb * SSM_STATE, SSM_CH_BLOCK)

    bmat = jnp.concatenate([in_map(bbar_re), in_map(bbar_im)], axis=2).astype(BF16)
    cmat = jnp.concatenate([out_map(c_re), -out_map(c_im)], axis=1).astype(BF16)
    vec = lambda v: v.reshape(nblk, 1, SSM_HALF)
    return (bmat, cmat, vec(abar_re), vec(abar_im), vec(at_re), vec(at_im),
            d_skip.reshape(nblk, 1, SSM_CH_BLOCK))


def _attn_kernel(sink_ref, q_ref, kvc_ref, kvp_ref, kvm_ref, o_ref, *, blocks_per_seq):
    n = pl.program_id(0) % blocks_per_seq
    w = WINDOW
    hd = HEAD_DIM
    qi = lax.broadcasted_iota(jnp.int32, (w, w), 0)
    lane = lax.broadcasted_iota(jnp.int32, (w, w), 1)
    vis_prev = (lane > qi) & (n > 0)
    vis_cur = lane <= qi
    left = lane < hd
    meta_l = lane < N_META
    meta_r = (lane >= N_META) & (lane < 2 * N_META)

    def placed(x_bf):
        x = x_bf.astype(F32)
        xr = pltpu.roll(x, hd, 1)
        lm = lax.broadcasted_iota(jnp.int32, x.shape, 1) < hd
        z = jnp.zeros_like(x)
        return {(0, 0): jnp.where(lm, x, z), (0, 1): jnp.where(lm, z, xr),
                (1, 0): jnp.where(lm, xr, z), (1, 1): jnp.where(lm, z, x)}

    kp, kc, km = placed(kvp_ref[:, :KV_WIDTH]), placed(kvc_ref[:, :KV_WIDTH]), placed(kvm_ref[:, :KV_WIDTH])
    vp, vc, vm = placed(kvp_ref[:, KV_WIDTH:]), placed(kvc_ref[:, KV_WIDTH:]), placed(kvm_ref[:, KV_WIDTH:])
    pad_rows = w - 2 * N_META
    zpad = jnp.zeros((pad_rows, w), F32)
    krow = lax.broadcasted_iota(jnp.int32, (5 * w, w), 0)
    klane = lax.broadcasted_iota(jnp.int32, (5 * w, w), 1)
    row_l = (krow < 2 * w) | ((krow >= 4 * w) & (krow < 4 * w + N_META))
    row_r = ((krow >= 2 * w) & (krow < 4 * w)) | ((krow >= 4 * w + N_META) & (krow < 4 * w + 2 * N_META))
    den_cols = jnp.where((row_l & (klane < hd)) | (row_r & (klane >= hd)), 1.0, 0.0)

    for j in range(N_KV_HEADS):
        kcat = jnp.concatenate(
            [kp[j, 0], kc[j, 0], kp[j, 1], kc[j, 1], km[j, 0], km[j, 1], zpad], axis=0).astype(BF16)
        vcat = jnp.concatenate(
            [jnp.concatenate([vp[j, 0], vc[j, 0], vp[j, 1], vc[j, 1], vm[j, 0], vm[j, 1], zpad], axis=0),
             den_cols], axis=1).astype(BF16)
        for r in range(KV_REP // 2):
            pr = j * (KV_REP // 2) + r
            s = _dot_nt(q_ref[:, pr * w:(pr + 1) * w], kcat)
            s_l = jnp.where(vis_cur, s[:, w:2 * w], jnp.where(vis_prev, s[:, 0:w], NEG_INF))
            s_r = jnp.where(vis_cur, s[:, 3 * w:4 * w], jnp.where(vis_prev, s[:, 2 * w:3 * w], NEG_INF))
            s_m = s[:, 4 * w:]
            sink_l, sink_r = sink_ref[2 * pr] * LOG2_E, sink_ref[2 * pr + 1] * LOG2_E
            m_l = jnp.maximum(jnp.max(jnp.maximum(s_l, jnp.where(meta_l, s_m, NEG_INF)),
                                      axis=1, keepdims=True), sink_l)
            m_r = jnp.maximum(jnp.max(jnp.maximum(s_r, jnp.where(meta_r, s_m, NEG_INF)),
                                      axis=1, keepdims=True), sink_r)
            s_m = jnp.where(meta_l, s_m - m_l, jnp.where(meta_r, s_m - m_r, NEG_INF))
            e_l, e_r = jnp.exp2(s_l - m_l), jnp.exp2(s_r - m_r)
            e = jnp.concatenate([jnp.where(vis_cur, 0.0, e_l), jnp.where(vis_cur, e_l, 0.0),
                                 jnp.where(vis_cur, 0.0, e_r), jnp.where(vis_cur, e_r, 0.0),
                                 jnp.exp2(s_m)], axis=1).astype(BF16)
            acc = _dot(e, vcat)
            den = acc[:, w:] + jnp.where(left, jnp.exp2(sink_l - m_l), jnp.exp2(sink_r - m_r))
            o_ref[:, pr * w:(pr + 1) * w] = (acc[:, :w] / den).astype(BF16)


def _attention(sinks, q, kv, kv_meta, blocks_per_seq):
    n = q.shape[0]
    return pl.pallas_call(
        functools.partial(_attn_kernel, blocks_per_seq=blocks_per_seq),
        grid=(n // WINDOW,),
        in_specs=[
            pl.BlockSpec(memory_space=pltpu.SMEM),
            pl.BlockSpec((WINDOW, Q_WIDTH), lambda g: (g, 0)),
            pl.BlockSpec((WINDOW, 2 * KV_WIDTH), lambda g: (g, 0)),
            pl.BlockSpec((WINDOW, 2 * KV_WIDTH), lambda g: (jnp.maximum(g - 1, 0), 0)),
            pl.BlockSpec((N_META, 2 * KV_WIDTH), lambda g: (0, 0)),
        ],
        out_specs=pl.BlockSpec((WINDOW, Q_WIDTH), lambda g: (g, 0)),
        out_shape=jax.ShapeDtypeStruct((n, Q_WIDTH), BF16),
        compiler_params=pltpu.CompilerParams(dimension_semantics=("arbitrary",)),
        name="attn",
    )(sinks, q, kv, kv, kv_meta)


_ROUTE_IDX, _ROUTE_RANK, _ROUTE_GATE, _ROUTE_ROWS = 0, TOP_K, 2 * TOP_K, 16


def _mix_kernel(x_ref, y_ref, at_ref, gs_ref, ga_ref, wglu_ref, wo_ref, wout_ref, fg_ref, rw_ref, rb_ref,
                h1_ref, hf_ref, route_ref, cnt_ref, cnt_scr):
    tm = x_ref.shape[0]

    @pl.when(pl.program_id(0) == 0)
    def _():
        cnt_scr[...] = jnp.zeros_like(cnt_scr)

    y_ssm = jnp.concatenate([y_ref[c * SSM_PITCH:c * SSM_PITCH + SSM_CHUNK, :]
                             for c in range(tm // SSM_CHUNK)], axis=0)
    glu = _dot(jax.nn.gelu(y_ssm).astype(BF16), wglu_ref[...])
    branch_ssm = glu[:, :D_MODEL] * jax.nn.sigmoid(glu[:, D_MODEL:])
    branch_attn = _dot(at_ref[...], wo_ref[...])
    merged = gs_ref[...].astype(F32) * branch_ssm + ga_ref[...].astype(F32) * branch_attn
    h1 = x_ref[...] + _dot(merged.astype(BF16), wout_ref[...])
    h1_ref[...] = h1
    ms = jnp.mean(h1 * h1, axis=-1, keepdims=True)
    hf = h1 * lax.rsqrt(ms + RMS_EPS) * fg_ref[...]
    _store_token_tiles(hf_ref, hf)

    hf_hi = hf.astype(BF16)
    hf_lo = (hf - hf_hi.astype(F32)).astype(BF16)
    hi_prod = _dot(hf_hi, rw_ref[...])
    logits = (hi_prod[:, :LANES] + (hi_prod[:, LANES:] + _dot(hf_lo, rw_ref[:, :LANES]))
              + rb_ref[...])
    lt = logits.T[:N_EXPERTS, :]
    erow = lax.broadcasted_iota(jnp.int32, (N_EXPERTS, tm), 0)
    vals, idxs, hots = [], [], []
    rest = lt
    for _ in range(TOP_K):
        m = jnp.max(rest, axis=0, keepdims=True)
        first = jnp.min(jnp.where(rest == m, erow, N_EXPERTS), axis=0, keepdims=True)
        hot = erow == first
        vals.append(m)
        idxs.append(first)
        hots.append(hot)
        rest = jnp.where(hot, -jnp.inf, rest)
    exps = [jnp.exp(v - vals[0]) for v in vals]
    tot = exps[0] + exps[1] + exps[2] + exps[3]

    sel = (hots[0] | hots[1] | hots[2] | hots[3]).astype(F32)
    ti = lax.broadcasted_iota(jnp.int32, (tm, tm), 0)
    tj = lax.broadcasted_iota(jnp.int32, (tm, tm), 1)
    earlier = (ti < tj).astype(BF16)
    rank_e = _dot(sel.astype(BF16), earlier) + cnt_scr[...]
    cnt_scr[...] = cnt_scr[...] + jnp.sum(sel, axis=1, keepdims=True)
    cnt_ref[...] = cnt_scr[...]

    rrow = lax.broadcasted_iota(jnp.int32, (_ROUTE_ROWS, tm), 0)
    route = jnp.zeros((_ROUTE_ROWS, tm), F32)
    for k in range(TOP_K):
        rank_k = jnp.sum(jnp.where(hots[k], rank_e, 0.0), axis=0, keepdims=True)
        route = jnp.where(rrow == _ROUTE_IDX + k, idxs[k].astype(F32), route)
        route = jnp.where(rrow == _ROUTE_RANK + k, rank_k, route)
        route = jnp.where(rrow == _ROUTE_GATE + k, exps[k] / tot, route)
    route_ref[...] = route


def _mix(x2, y, attn, gs, ga, wglu, wo, wout, fg, rw, rb, tm):
    n = x2.shape[0]
    rw = jnp.pad(rw, ((0, 0), (0, LANES - N_EXPERTS)))
    rb = jnp.pad(rb, ((0, 0), (0, LANES - N_EXPERTS)), constant_values=NEG_INF)
    rw_hi = rw.astype(BF16)
    rw_split = jnp.concatenate([rw_hi, (rw - rw_hi.astype(F32)).astype(BF16)], axis=1)
    row = lambda w: pl.BlockSpec((tm, w), lambda i: (i, 0))
    full = lambda a: pl.BlockSpec(a.shape, lambda i: (0,) * a.ndim)
    return pl.pallas_call(
        _mix_kernel,
        grid=(n // tm,),
        in_specs=[row(D_MODEL), pl.BlockSpec((tm // SSM_CHUNK * SSM_PITCH, SSM_WIDTH), lambda i: (i, 0)),
                  row(Q_WIDTH), row(D_MODEL), row(D_MODEL),
                  full(wglu), full(wo), full(wout), full(fg), full(rw_split), full(rb)],
        out_specs=[row(D_MODEL), pl.BlockSpec((tm * TILE_ROWS, LANES), lambda i: (i, 0)),
                   pl.BlockSpec((_ROUTE_ROWS, tm), lambda i: (0, i)),
                   pl.BlockSpec((N_EXPERTS, 1), lambda i: (0, 0))],
        out_shape=[
            jax.ShapeDtypeStruct((n, D_MODEL), F32),
            jax.ShapeDtypeStruct((n * TILE_ROWS, LANES), F32),
            jax.ShapeDtypeStruct((_ROUTE_ROWS, n), F32),
            jax.ShapeDtypeStruct((N_EXPERTS, 1), F32),
        ],
        scratch_shapes=[pltpu.VMEM((N_EXPERTS, 1), F32)],
        compiler_params=pltpu.CompilerParams(
            dimension_semantics=("arbitrary",), vmem_limit_bytes=VMEM_LIMIT),
        name="mix_router",
    )(x2, y, attn, gs, ga, wglu, wo, wout, fg, rw_split, rb)


def _tiles_wait_copy(src_hbm, dst, n_tiles, sem):
    rows = n_tiles * TILE_ROWS
    return pltpu.make_async_copy(src_hbm.at[pl.ds(0, rows), :], dst.at[pl.ds(0, rows), :], sem)


_ISSUE_UNROLL = 16


_DISPATCH_RING = 4


def _dispatch_kernel(dst_ref, pad_start_ref, pad_len_ref, nu_ref, hf_hbm, xs_hbm, zero_blk, ring, in_sem,
                     out_sem, pad_sem, *, n_steps):
    i = pl.program_id(0)
    last = n_steps - 1
    pairs = dst_ref.shape[0]
    tokens = pairs // TOP_K
    blk_rows = tokens * TILE_ROWS
    n_blocks = xs_hbm.shape[0] // (EXPERT_ROWS * TILE_ROWS)
    slot = i % _DISPATCH_RING

    def load(step, s):
        src = hf_hbm.at[pl.ds(pl.multiple_of(step * blk_rows, blk_rows), blk_rows), :]
        return pltpu.make_async_copy(src, ring.at[s], in_sem.at[s])

    def wait_copies(s):
        for _ in range(TOP_K):
            pltpu.make_async_copy(ring.at[s], xs_hbm.at[pl.ds(0, blk_rows), :], out_sem.at[s]).wait()

    def zero_copy(row, rows):
        src = zero_blk.at[pl.ds(0, rows * TILE_ROWS), :]
        dst = xs_hbm.at[pl.ds(pl.multiple_of(row * TILE_ROWS, TILE_ROWS), rows * TILE_ROWS), :]
        return pltpu.make_async_copy(src, dst, pad_sem)

    def for_each_pad(fn):
        def per_expert(e, carry):
            row, left = pad_start_ref[e], pad_len_ref[e]
            size = EXPERT_ROWS // 2
            while size >= 1:
                take = left & size

                @pl.when(take != 0)
                def _(row=row, size=size):
                    fn(zero_copy(row, size))

                row = row + take
                size //= 2
            return carry
        lax.fori_loop(0, N_EXPERTS, per_expert, 0)

        def per_block(b, carry):
            fn(zero_copy(b * EXPERT_ROWS, EXPERT_ROWS))
            return carry
        lax.fori_loop(nu_ref[0], n_blocks, per_block, 0)

    @pl.when(i == 0)
    def _():
        for s in range(min(2, n_steps)):
            load(s, s).start()
        zero_blk[...] = jnp.zeros_like(zero_blk)
        for_each_pad(lambda cp: cp.start())

    @pl.when(i >= 2)
    def _():
        wait_copies((i + 2) % _DISPATCH_RING)

    @pl.when(i + 2 < n_steps)
    def _():
        load(i + 2, (i + 2) % _DISPATCH_RING).start()

    load(i, slot).wait()
    src_blk = ring.at[slot]
    for k in range(TOP_K):
        def issue(o, carry, k=k):
            tok0 = pl.multiple_of(o * _ISSUE_UNROLL, _ISSUE_UNROLL)
            dsts = [dst_ref[k * tokens + tok0 + r] for r in range(_ISSUE_UNROLL)]
            for r in range(_ISSUE_UNROLL):
                pltpu.make_async_copy(_token_tile(src_blk, tok0 + r), _token_tile(xs_hbm, dsts[r]),
                                      out_sem.at[slot]).start(priority=r % 2)
            return carry

        lax.fori_loop(0, tokens // _ISSUE_UNROLL, issue, 0)

    @pl.when(i == last)
    def _():
        if n_steps > 1:
            wait_copies((i + _DISPATCH_RING - 1) % _DISPATCH_RING)
        wait_copies(slot)
        for_each_pad(lambda cp: cp.wait())


def _dispatch(dest, pad_start, pad_len, n_used, hf_tiles, n_rows, tokens_per_step):
    n = dest.shape[0] // TOP_K
    pairs = tokens_per_step * TOP_K
    assert n % tokens_per_step == 0 and tokens_per_step % _ISSUE_UNROLL == 0
    smem = lambda: pl.BlockSpec(memory_space=pltpu.SMEM)
    n_steps = n // tokens_per_step
    return pl.pallas_call(
        functools.partial(_dispatch_kernel, n_steps=n_steps),
        grid=(n_steps,),
        in_specs=[pl.BlockSpec((pairs,), lambda i: (i,), memory_space=pltpu.SMEM), smem(), smem(), smem(),
                  pl.BlockSpec(memory_space=pl.ANY)],
        out_specs=pl.BlockSpec(memory_space=pl.ANY),
        out_shape=jax.ShapeDtypeStruct((n_rows * TILE_ROWS, LANES), F32),
        scratch_shapes=[pltpu.VMEM((EXPERT_ROWS * TILE_ROWS, LANES), F32),
                        pltpu.VMEM((_DISPATCH_RING, tokens_per_step * TILE_ROWS, LANES), F32),
                        pltpu.SemaphoreType.DMA((_DISPATCH_RING,)), pltpu.SemaphoreType.DMA((_DISPATCH_RING,)),
                        pltpu.SemaphoreType.DMA],
        compiler_params=pltpu.CompilerParams(
            dimension_semantics=("arbitrary",), vmem_limit_bytes=VMEM_LIMIT),
        name="dispatch",
    )(dest, pad_start, pad_len, n_used, hf_tiles)


def _expert_kernel(be_ref, nu_ref, nxt_ref, val_ref, xs_ref, wu_hbm, bu_ref, wd_hbm, bd_ref, y_ref,
                   wu_f32, wd_f32, wu_bf, wd_bf, wsem):
    i = pl.program_id(0)

    def mlp(xb):
        up = _dot(xb, wu_bf[...]) + bu_ref[0]
        x_glu = jnp.minimum(up[:, :D_FF], SWIGLU_LIMIT)
        x_lin = jnp.clip(up[:, D_FF:], -SWIGLU_LIMIT, SWIGLU_LIMIT)
        act = x_glu * jax.nn.sigmoid(SWIGLU_ALPHA * x_glu) * (x_lin + 1.0)
        return _dot(act.astype(BF16), wd_bf[...]) + bd_ref[0]

    def weight_copies(expert):
        return (pltpu.make_async_copy(wu_hbm.at[expert], wu_f32, wsem.at[0]),
                pltpu.make_async_copy(wd_hbm.at[expert], wd_f32, wsem.at[1]))

    @pl.when(i == 0)
    def _():
        for cp in weight_copies(be_ref[0]):
            cp.start()

    @pl.when((i == 0) | (be_ref[i] != be_ref[jnp.maximum(i - 1, 0)]))
    def _():
        for cp in weight_copies(be_ref[i]):
            cp.wait()
        wu_bf[...] = wu_f32[...].astype(BF16)
        wd_bf[...] = wd_f32[...].astype(BF16)

        @pl.when(nxt_ref[i] >= 0)
        def _():
            for cp in weight_copies(nxt_ref[i]):
                cp.start()

    used = i < nu_ref[0]
    full = val_ref[i] > EXPERT_ROWS - EXPERT_SUB_ROWS

    @pl.when(used & full)
    def _():
        _store_token_tiles(y_ref, mlp(_load_token_tiles(xs_ref, 0, EXPERT_ROWS).astype(BF16)))

    @pl.when(used & jnp.logical_not(full))
    def _():
        sub = EXPERT_SUB_ROWS
        n_sub = lax.shift_right_logical(val_ref[i] + (sub - 1), sub.bit_length() - 1)

        def compute(j, carry):
            row0 = pl.multiple_of(j * sub, sub)
            _store_token_tiles(y_ref, mlp(_load_token_tiles(xs_ref, row0, sub).astype(BF16)), row0)
            return carry

        def clear(j, carry):
            row0 = pl.multiple_of(j * sub * TILE_ROWS, sub * TILE_ROWS)
            y_ref[pl.ds(row0, sub * TILE_ROWS), :] = jnp.zeros((sub * TILE_ROWS, LANES), F32)
            return carry

        lax.fori_loop(0, n_sub, compute, 0)
        lax.fori_loop(n_sub, EXPERT_ROWS // sub, clear, 0)

    @pl.when(jnp.logical_not(used))
    def _():
        y_ref[...] = jnp.zeros_like(y_ref)


def _experts(block_expert, n_used, next_expert, valid_rows, xs_tiles, w_up, b_up, w_down, b_down):
    n_blocks = block_expert.shape[0]
    blk = (EXPERT_ROWS * TILE_ROWS, LANES)
    grid_spec = pltpu.PrefetchScalarGridSpec(
        num_scalar_prefetch=4,
        grid=(n_blocks,),
        in_specs=[
            pl.BlockSpec(blk, lambda i, be, nu, *_: (jnp.minimum(i, nu[0] - 1), 0)),
            pl.BlockSpec(memory_space=pl.ANY),
            pl.BlockSpec((1, 1, 2 * D_FF), lambda i, be, *_: (be[i], 0, 0)),
            pl.BlockSpec(memory_space=pl.ANY),
            pl.BlockSpec((1, 1, D_MODEL), lambda i, be, *_: (be[i], 0, 0)),
        ],
        out_specs=pl.BlockSpec(blk, lambda i, *_: (i, 0)),
        scratch_shapes=[
            pltpu.VMEM((D_MODEL, 2 * D_FF), F32),
            pltpu.VMEM((D_FF, D_MODEL), F32),
            pltpu.VMEM((D_MODEL, 2 * D_FF), BF16),
            pltpu.VMEM((D_FF, D_MODEL), BF16),
            pltpu.SemaphoreType.DMA((2,)),
        ],
    )
    return pl.pallas_call(
        _expert_kernel,
        grid_spec=grid_spec,
        out_shape=jax.ShapeDtypeStruct((n_blocks * blk[0], LANES), F32),
        compiler_params=pltpu.CompilerParams(
            dimension_semantics=("arbitrary",), vmem_limit_bytes=VMEM_LIMIT),
        name="experts",
    )(block_expert, n_used, next_expert, valid_rows, xs_tiles,
      w_up, b_up[:, None, :], w_down, b_down[:, None, :])


_COMBINE_RING = 3


def _combine_kernel(dst0_ref, dst1_ref, dst2_ref, y_hbm, h1_ref, route_ref, g_ref, o_ref, *scratch):
    bufs, sem = scratch[:_COMBINE_RING], scratch[_COMBINE_RING]
    i = pl.program_id(0)
    last = pl.num_programs(0) - 1
    tm = h1_ref.shape[0]
    rows = TOP_K * tm

    def gather_group(idx_ref, s, row0):
        srcs = [idx_ref[row0 + r] for r in range(_ISSUE_UNROLL)]
        for r in range(_ISSUE_UNROLL):
            pltpu.make_async_copy(_token_tile(y_hbm, srcs[r]), _token_tile(bufs[s], row0 + r),
                                  sem.at[s]).start(priority=r % 2)

    @pl.when(i == 0)
    def _():
        for s, idx_ref in ((0, dst0_ref), (1, dst1_ref)):
            def body(o, carry, s=s, idx_ref=idx_ref):
                gather_group(idx_ref, s, pl.multiple_of(o * _ISSUE_UNROLL, _ISSUE_UNROLL))
                return carry
            lax.fori_loop(0, rows // _ISSUE_UNROLL, body, 0)

    def step(s):
        cur = bufs[s]
        ahead = (s + 2) % _COMBINE_RING
        _tiles_wait_copy(y_hbm, cur, rows, sem.at[s]).wait()
        for g in range(rows // _ISSUE_UNROLL):
            gather_group(dst2_ref, ahead, g * _ISSUE_UNROLL)
        assert tm == LANES
        rt = jnp.concatenate([route_ref[...], jnp.zeros((LANES - _ROUTE_ROWS, tm), F32)], axis=0).T
        acc = h1_ref[...]
        for k in range(TOP_K):
            gate = rt[:, _ROUTE_GATE + k:_ROUTE_GATE + k + 1]
            acc = acc + gate * _load_token_tiles(cur, k * tm, tm)
        ms = jnp.mean(acc * acc, axis=-1, keepdims=True)
        o_ref[...] = acc * lax.rsqrt(ms + RMS_EPS) * g_ref[...]

        @pl.when(i == last)
        def _():
            for t in ((s + 1) % _COMBINE_RING, ahead):
                _tiles_wait_copy(y_hbm, bufs[t], rows, sem.at[t]).wait()

    for s in range(_COMBINE_RING):
        pl.when(i % _COMBINE_RING == s)(functools.partial(step, s))


def _combine(dest_kmajor, y, h1, route, g, tm):
    n = h1.shape[0]
    n_tiles = n // tm
    idx_spec = lambda ahead: pl.BlockSpec(
        (TOP_K * tm,), lambda i: (jnp.minimum(i + ahead, n_tiles - 1),), memory_space=pltpu.SMEM)
    ring_buf = pltpu.VMEM((TOP_K * tm * TILE_ROWS, LANES), F32)
    return pl.pallas_call(
        _combine_kernel,
        grid=(n_tiles,),
        in_specs=[
            idx_spec(0), idx_spec(1), idx_spec(2),
            pl.BlockSpec(memory_space=pl.ANY),
            pl.BlockSpec((tm, D_MODEL), lambda i: (i, 0)),
            pl.BlockSpec((_ROUTE_ROWS, tm), lambda i: (0, i)),
            pl.BlockSpec((1, D_MODEL), lambda i: (0, 0)),
        ],
        out_specs=pl.BlockSpec((tm, D_MODEL), lambda i: (i, 0)),
        out_shape=jax.ShapeDtypeStruct((n, D_MODEL), F32),
        scratch_shapes=[ring_buf] * _COMBINE_RING + [pltpu.SemaphoreType.DMA((_COMBINE_RING,))],
        compiler_params=pltpu.CompilerParams(
            dimension_semantics=("arbitrary",), vmem_limit_bytes=VMEM_LIMIT),
        name="combine",
    )(dest_kmajor, dest_kmajor, dest_kmajor, y, h1, route, g)


def _routing_tables(route, counts, n, tokens_dispatch, tokens_combine):
    tm = EXPERT_ROWS
    i32 = jnp.int32
    n_blocks = (n * TOP_K + N_EXPERTS * (tm - 1)) // tm
    idx = route[_ROUTE_IDX:_ROUTE_IDX + TOP_K].astype(i32)
    rank = route[_ROUTE_RANK:_ROUTE_RANK + TOP_K].astype(i32)
    cnt = counts[:, 0].astype(i32)
    eid = jnp.arange(N_EXPERTS, dtype=i32)
    upto = eid[None, :] <= eid[:, None]
    blocks_e = (cnt + tm - 1) // tm
    blocks_end = jnp.sum(jnp.where(upto, blocks_e[None, :], 0), axis=1)
    row_start = (blocks_end - blocks_e) * tm
    n_used = blocks_end[N_EXPERTS - 1]
    used = blocks_e > 0

    def lookup(table, keys):
        hit = keys[None] == eid.reshape((N_EXPERTS,) + (1,) * keys.ndim)
        return jnp.sum(jnp.where(hit, table.reshape((N_EXPERTS,) + (1,) * keys.ndim), 0), axis=0)

    dest = lookup(row_start, idx) + rank
    blk = jnp.arange(n_blocks, dtype=i32)
    last_used = jnp.max(jnp.where(used, eid, 0))
    be = jnp.where(blk < n_used, jnp.sum((blocks_end[None, :] <= blk[:, None]).astype(i32), axis=1), last_used)
    later_used = used[None, :] & (eid[None, :] > eid[:, None])
    after = jnp.min(jnp.where(later_used, eid[None, :], N_EXPERTS), axis=1)
    next_e = jnp.where(after < N_EXPERTS, after, -1)

    def k_major(tokens):
        return dest.reshape(TOP_K, n // tokens, tokens).transpose(1, 0, 2).reshape(-1)

    valid = jnp.clip(lookup(row_start + cnt, be) - blk * tm, 0, tm)
    return (be.astype(i32), n_used.reshape(1), lookup(next_e, be).astype(i32), valid.astype(i32),
            k_major(tokens_dispatch), k_major(tokens_combine), row_start + cnt, blocks_e * tm - cnt,
            n_blocks * tm)


def kernel(x, meta_tokens, mix_norm_g, w_in, ssm_a_re, ssm_a_im, ssm_log_dt, ssm_b_re, ssm_b_im,
           ssm_c_re, ssm_c_im, ssm_d, w_ssm_glu, attn_sinks, w_attn_o, w_out, ffn_norm_g,
           router_w, router_b, w_up, b_up, w_down, b_down, final_norm_g):
    bsz, seq, d = x.shape
    assert d == D_MODEL and seq % max(WINDOW, SSM_CHUNK) == 0
    assert mix_norm_g.shape[0] == 1, "single-layer trunk"
    n = bsz * seq
    tm_proj = min(1024, n)
    tm_mix = min(512, n)
    tm_comb = min(128, n)
    x2 = x.reshape(n, D_MODEL)

    w_in_bf = w_in[0].astype(BF16)
    g_mix = mix_norm_g[0][None, :]
    u, q, kv, gs, ga = _in_proj(x2, g_mix, w_in_bf, tm_proj, pitched=True)
    u_m, _, kv_m, _, _ = _in_proj(meta_tokens, g_mix, w_in_bf, N_META, pitched=False)

    ssm_par = _ssm_params(ssm_a_re[0], ssm_a_im[0], ssm_log_dt[0], ssm_b_re[0], ssm_b_im[0],
                          ssm_c_re[0], ssm_c_im[0], ssm_d[0], SSM_CHUNK)
    y_ssm = _ssm(u, u_m, *ssm_par, batch=bsz, chunk=SSM_CHUNK)

    attn = _attention(attn_sinks[0], q, kv, kv_m, seq // WINDOW)

    h1, hf, route, counts = _mix(
        x2, y_ssm, attn, gs, ga, w_ssm_glu[0].astype(BF16), w_attn_o[0].astype(BF16),
        w_out[0].astype(BF16), ffn_norm_g[0][None, :], router_w[0], router_b[0][None, :], tm_mix)

    tok_disp = min(1024, n)
    be, n_used, next_e, valid, dest_disp, dest_comb, pad_start, pad_len, n_rows = _routing_tables(
        route, counts, n, tok_disp, tm_comb)
    xs = _dispatch(dest_disp, pad_start, pad_len, n_used, hf, n_rows, tok_disp)
    y = _experts(be, n_used, next_e, valid, xs, w_up[0], b_up[0], w_down[0], b_down[0])
    out = _combine(dest_comb, y, h1, route, final_norm_g[None, :], tm_comb)
    return out.reshape(bsz, seq, D_MODEL)
```

```python
import functools
import math

import jax
import jax.numpy as jnp
from jax import lax
from jax.experimental import pallas as pl
from jax.experimental.pallas import tpu as pltpu

F32 = jnp.float32
BF16 = jnp.bfloat16

D_MODEL = 1024
N_META = 16
SSM_WIDTH = 512
SSM_GROUP = 16
SSM_GROUPS = 32
SSM_STATE = 64
HEAD_DIM = 64
N_HEADS = 16
N_KV_HEADS = 2
KV_REP = N_HEADS // N_KV_HEADS
WINDOW = 128
Q_WIDTH = N_HEADS * HEAD_DIM
KV_WIDTH = N_KV_HEADS * HEAD_DIM
N_EXPERTS = 32
TOP_K = 4
D_FF = 1024
SWIGLU_ALPHA = 1.702
SWIGLU_LIMIT = 7.0
RMS_EPS = 1e-5
NEG_INF = -1e30

_U0, _Q0, _KV0, _GS0, _GA0, _IN_END = 0, 512, 1536, 1792, 2816, 3840

SSM_CH_BLOCK = 128
SSM_HALF = (SSM_CH_BLOCK // SSM_GROUP) * SSM_STATE
SSM_CHUNK = 32
SSM_PITCH = 40
LOG2_E = math.log2(math.e)
EXPERT_ROWS = 512
EXPERT_SUB_ROWS = 128
VMEM_LIMIT = 56 * 1024 * 1024


def _dot(a, b):
    return jnp.dot(a, b, preferred_element_type=F32)


def _dot_nt(a, b):
    return lax.dot_general(a, b, (((1,), (1,)), ((), ())), preferred_element_type=F32)


LANES = 128
TILE_ROWS = D_MODEL // LANES


def _store_token_tiles(ref, x, start_row=0):
    rows = x.shape[0]
    for j in range(TILE_ROWS):
        ref[pl.ds(start_row * TILE_ROWS + j, rows, stride=TILE_ROWS), :] = x[:, j * LANES:(j + 1) * LANES]


def _load_token_tiles(ref, start_row, rows):
    return jnp.concatenate(
        [ref[pl.ds(start_row * TILE_ROWS + j, rows, stride=TILE_ROWS), :] for j in range(TILE_ROWS)], axis=1)


def _token_tile(ref, row):
    return ref.at[pl.ds(pl.multiple_of(row * TILE_ROWS, TILE_ROWS), TILE_ROWS), :]


def _in_proj_kernel(x_ref, g_ref, w_ref, u_ref, q_ref, kv_ref, gs_ref, ga_ref, *, pitched):
    x = x_ref[...]
    ms = jnp.mean(x * x, axis=-1, keepdims=True)
    hn = (x * lax.rsqrt(ms + RMS_EPS) * g_ref[...]).astype(BF16)
    u = _dot(hn, w_ref[:, _U0:_Q0])
    if pitched:
        for c in range(u.shape[0] // SSM_CHUNK):
            u_ref[c * SSM_PITCH:c * SSM_PITCH + SSM_CHUNK, :] = u[c * SSM_CHUNK:(c + 1) * SSM_CHUNK, :]
            u_ref[c * SSM_PITCH + SSM_CHUNK:(c + 1) * SSM_PITCH, :] = jnp.zeros(
                (SSM_PITCH - SSM_CHUNK, SSM_WIDTH), F32)
    else:
        u_ref[...] = u
    q_ref[...] = (_dot(hn, w_ref[:, _Q0:_KV0]) * (HEAD_DIM ** -0.5 * LOG2_E)).astype(BF16)
    kv_ref[...] = _dot(hn, w_ref[:, _KV0:_GS0]).astype(BF16)
    gs_ref[...] = jax.nn.sigmoid(_dot(hn, w_ref[:, _GS0:_GA0])).astype(BF16)
    ga_ref[...] = jax.nn.sigmoid(_dot(hn, w_ref[:, _GA0:_IN_END])).astype(BF16)


def _in_proj(x2, g, w_bf, tm, pitched):
    n = x2.shape[0]
    row = lambda w: pl.BlockSpec((tm, w), lambda i: (i, 0))
    full = lambda a: pl.BlockSpec(a.shape, lambda i: (0,) * a.ndim)
    u_rows = (lambda r: r // SSM_CHUNK * SSM_PITCH) if pitched else (lambda r: r)
    return pl.pallas_call(
        functools.partial(_in_proj_kernel, pitched=pitched),
        grid=(n // tm,),
        in_specs=[row(D_MODEL), full(g), full(w_bf)],
        out_specs=[pl.BlockSpec((u_rows(tm), SSM_WIDTH), lambda i: (i, 0)),
                   row(Q_WIDTH), row(2 * KV_WIDTH), row(D_MODEL), row(D_MODEL)],
        out_shape=[
            jax.ShapeDtypeStruct((u_rows(n), SSM_WIDTH), F32),
            jax.ShapeDtypeStruct((n, Q_WIDTH), BF16),
            jax.ShapeDtypeStruct((n, 2 * KV_WIDTH), BF16),
            jax.ShapeDtypeStruct((n, D_MODEL), BF16),
            jax.ShapeDtypeStruct((n, D_MODEL), BF16),
        ],
        compiler_params=pltpu.CompilerParams(
            dimension_semantics=("arbitrary",), vmem_limit_bytes=VMEM_LIMIT),
        name="in_proj",
    )(x2, g, w_bf)


def _ssm_kernel(u_ref, um_ref, bm_ref, cm_ref, ar_ref, ai_ref, atr_ref, ati_ref, d_ref,
                y_ref, sre, sim, *, chunk, rows, batch):
    h = SSM_HALF
    bm = bm_ref[0]
    cm = cm_ref[0]
    ar, ai = ar_ref[0], ai_ref[0]
    atr, ati = atr_ref[0], ati_ref[0]
    dsk = d_ref[0]
    n_chunks = rows // batch

    def advance(sr, si, bu):
        return ar * sr - ai * si + bu[:, :h], ar * si + ai * sr + bu[:, h:]

    bum = _dot(um_ref[...].astype(BF16), bm)
    mr = jnp.zeros((1, h), F32)
    mi = jnp.zeros((1, h), F32)
    for j in range(N_META):
        mr, mi = advance(mr, mi, bum[j:j + 1, :])

    def u_step(t):
        return u_ref[pl.ds(t, rows, stride=SSM_PITCH), :]

    for pad in range(chunk, SSM_PITCH):
        y_ref[pl.ds(pad, rows, stride=SSM_PITCH), :] = jnp.zeros((rows, SSM_CH_BLOCK), F32)

    sre[...] = jnp.zeros_like(sre)
    sim[...] = jnp.zeros_like(sim)

    def pass_a(t, carry):
        bu = _dot(u_step(t).astype(BF16), bm)
        nr, ni = advance(sre[...], sim[...], bu)
        sre[...] = nr
        sim[...] = ni
        return carry

    lax.fori_loop(0, chunk, pass_a, 0)

    def over_chunks(c, carry):
        new = []
        for b in range(batch):
            cr, ci = carry[2 * b], carry[2 * b + 1]
            row = pl.ds(b * n_chunks + c, 1)
            er, ei = sre[row, :], sim[row, :]
            sre[row, :] = cr
            sim[row, :] = ci
            new += [atr * cr - ati * ci + er, atr * ci + ati * cr + ei]
        return tuple(new)

    lax.fori_loop(0, n_chunks, over_chunks, (mr, mi) * batch)

    def pass_b(t, carry):
        ut = u_step(t)
        bu = _dot(ut.astype(BF16), bm)
        nr, ni = advance(sre[...], sim[...], bu)
        sre[...] = nr
        sim[...] = ni
        y = _dot(nr.astype(BF16), cm[:h, :]) + _dot(ni.astype(BF16), cm[h:, :]) + dsk * ut
        y_ref[pl.ds(t, rows, stride=SSM_PITCH), :] = y
        return carry

    lax.fori_loop(0, chunk, pass_b, 0)


def _ssm(u, u_meta, bmat, cmat, a_re, a_im, at_re, at_im, dskip, batch, chunk):
    n = u.shape[0]
    rows = n // SSM_PITCH
    nblk = SSM_WIDTH // SSM_CH_BLOCK
    col = lambda r: pl.BlockSpec((r, SSM_CH_BLOCK), lambda j: (0, j))
    par = lambda a: pl.BlockSpec((1,) + a.shape[1:], lambda j: (j, 0, 0))
    return pl.pallas_call(
        functools.partial(_ssm_kernel, chunk=chunk, rows=rows, batch=batch),
        grid=(nblk,),
        in_specs=[col(n), col(N_META), par(bmat), par(cmat), par(a_re), par(a_im),
                  par(at_re), par(at_im), par(dskip)],
        out_specs=col(n),
        out_shape=jax.ShapeDtypeStruct((n, SSM_WIDTH), F32),
        scratch_shapes=[pltpu.VMEM((rows, SSM_HALF), F32), pltpu.VMEM((rows, SSM_HALF), F32)],
        compiler_params=pltpu.CompilerParams(
            dimension_semantics=("arbitrary",), vmem_limit_bytes=VMEM_LIMIT),
        name="ssm",
    )(u, u_meta, bmat, cmat, a_re, a_im, at_re, at_im, dskip)


def _ssm_params(a_re, a_im, log_dt, b_re, b_im, c_re, c_im, d_skip, chunk):
    dt = jnp.exp(log_dt)[:, None]
    mag = jnp.exp(a_re * dt)
    ang = a_im * dt
    abar_re, abar_im = mag * jnp.cos(ang), mag * jnp.sin(ang)
    den = a_re * a_re + a_im * a_im
    nr, ni = abar_re - 1.0, abar_im
    coef_re = ((nr * a_re + ni * a_im) / den)[..., None]
    coef_im = ((ni * a_re - nr * a_im) / den)[..., None]
    bbar_re = coef_re * b_re - coef_im * b_im
    bbar_im = coef_re * b_im + coef_im * b_re
    magt = jnp.exp(a_re * dt * chunk)
    at_re, at_im = magt * jnp.cos(ang * chunk), magt * jnp.sin(ang * chunk)

    nblk = SSM_WIDTH // SSM_CH_BLOCK
    gpb = SSM_GROUPS // nblk
    eye = jnp.eye(gpb, dtype=F32)

    def in_map(b):
        b = b.reshape(nblk, gpb, SSM_STATE, SSM_GROUP)
        return jnp.einsum('jgpc,gh->jgchp', b, eye).reshape(nblk, SSM_CH_BLOCK, gpb * SSM_STATE)

    def out_map(c):
        c = c.reshape(nblk, gpb, SSM_GROUP, SSM_STATE)
        return jnp.einsum('jgcp,gh->jgphc', c, eye).reshape(nblk, gpb * SSM_STATE, SSM_CH_BLOCK)

    bmat = jnp.concatenate([in_map(bbar_re), in_map(bbar_im)], axis=2).astype(BF16)
    cmat = jnp.concatenate([out_map(c_re), -out_map(c_im)], axis=1).astype(BF16)
    vec = lambda v: v.reshape(nblk, 1, SSM_HALF)
    return (bmat, cmat, vec(abar_re), vec(abar_im), vec(at_re), vec(at_im),
            d_skip.reshape(nblk, 1, SSM_CH_BLOCK))


def _attn_kernel(sink_ref, q_ref, kvc_ref, kvp_ref, kvm_ref, o_ref, *, blocks_per_seq):
    n = pl.program_id(0) % blocks_per_seq
    w = WINDOW
    hd = HEAD_DIM
    qi = lax.broadcasted_iota(jnp.int32, (w, w), 0)
    lane = lax.broadcasted_iota(jnp.int32, (w, w), 1)
    vis_prev = (lane > qi) & (n > 0)
    vis_cur = lane <= qi
    left = lane < hd
    meta_l = lane < N_META
    meta_r = (lane >= N_META) & (lane < 2 * N_META)

    def placed(x_bf):
        x = x_bf.astype(F32)
        xr = pltpu.roll(x, hd, 1)
        lm = lax.broadcasted_iota(jnp.int32, x.shape, 1) < hd
        z = jnp.zeros_like(x)
        return {(0, 0): jnp.where(lm, x, z), (0, 1): jnp.where(lm, z, xr),
                (1, 0): jnp.where(lm, xr, z), (1, 1): jnp.where(lm, z, x)}

    kp, kc, km = placed(kvp_ref[:, :KV_WIDTH]), placed(kvc_ref[:, :KV_WIDTH]), placed(kvm_ref[:, :KV_WIDTH])
    vp, vc, vm = placed(kvp_ref[:, KV_WIDTH:]), placed(kvc_ref[:, KV_WIDTH:]), placed(kvm_ref[:, KV_WIDTH:])
    pad_rows = w - 2 * N_META
    zpad = jnp.zeros((pad_rows, w), F32)
    krow = lax.broadcasted_iota(jnp.int32, (5 * w, w), 0)
    klane = lax.broadcasted_iota(jnp.int32, (5 * w, w), 1)
    row_l = (krow < 2 * w) | ((krow >= 4 * w) & (krow < 4 * w + N_META))
    row_r = ((krow >= 2 * w) & (krow < 4 * w)) | ((krow >= 4 * w + N_META) & (krow < 4 * w + 2 * N_META))
    den_cols = jnp.where((row_l & (klane < hd)) | (row_r & (klane >= hd)), 1.0, 0.0)

    for j in range(N_KV_HEADS):
        kcat = jnp.concatenate(
            [kp[j, 0], kc[j, 0], kp[j, 1], kc[j, 1], km[j, 0], km[j, 1], zpad], axis=0).astype(BF16)
        vcat = jnp.concatenate(
            [jnp.concatenate([vp[j, 0], vc[j, 0], vp[j, 1], vc[j, 1], vm[j, 0], vm[j, 1], zpad], axis=0),
             den_cols], axis=1).astype(BF16)
        for r in range(KV_REP // 2):
            pr = j * (KV_REP // 2) + r
            s = _dot_nt(q_ref[:, pr * w:(pr + 1) * w], kcat)
            s_l = jnp.where(vis_cur, s[:, w:2 * w], jnp.where(vis_prev, s[:, 0:w], NEG_INF))
            s_r = jnp.where(vis_cur, s[:, 3 * w:4 * w], jnp.where(vis_prev, s[:, 2 * w:3 * w], NEG_INF))
            s_m = s[:, 4 * w:]
            sink_l, sink_r = sink_ref[2 * pr] * LOG2_E, sink_ref[2 * pr + 1] * LOG2_E
            m_l = jnp.maximum(jnp.max(jnp.maximum(s_l, jnp.where(meta_l, s_m, NEG_INF)),
                                      axis=1, keepdims=True), sink_l)
            m_r = jnp.maximum(jnp.max(jnp.maximum(s_r, jnp.where(meta_r, s_m, NEG_INF)),
                                      axis=1, keepdims=True), sink_r)
            s_m = jnp.where(meta_l, s_m - m_l, jnp.where(meta_r, s_m - m_r, NEG_INF))
            e_l, e_r = jnp.exp2(s_l - m_l), jnp.exp2(s_r - m_r)
            e = jnp.concatenate([jnp.where(vis_cur, 0.0, e_l), jnp.where(vis_cur, e_l, 0.0),
                                 jnp.where(vis_cur, 0.0, e_r), jnp.where(vis_cur, e_r, 0.0),
                                 jnp.exp2(s_m)], axis=1).astype(BF16)
            acc = _dot(e, vcat)
            den = acc[:, w:] + jnp.where(left, jnp.exp2(sink_l - m_l), jnp.exp2(sink_r - m_r))
            o_ref[:, pr * w:(pr + 1) * w] = (acc[:, :w] / den).astype(BF16)


def _attention(sinks, q, kv, kv_meta, blocks_per_seq):
    n = q.shape[0]
    return pl.pallas_call(
        functools.partial(_attn_kernel, blocks_per_seq=blocks_per_seq),
        grid=(n // WINDOW,),
        in_specs=[
            pl.BlockSpec(memory_space=pltpu.SMEM),
            pl.BlockSpec((WINDOW, Q_WIDTH), lambda g: (g, 0)),
            pl.BlockSpec((WINDOW, 2 * KV_WIDTH), lambda g: (g, 0)),
            pl.BlockSpec((WINDOW, 2 * KV_WIDTH), lambda g: (jnp.maximum(g - 1, 0), 0)),
            pl.BlockSpec((N_META, 2 * KV_WIDTH), lambda g: (0, 0)),
        ],
        out_specs=pl.BlockSpec((WINDOW, Q_WIDTH), lambda g: (g, 0)),
        out_shape=jax.ShapeDtypeStruct((n, Q_WIDTH), BF16),
        compiler_params=pltpu.CompilerParams(dimension_semantics=("arbitrary",)),
        name="attn",
    )(sinks, q, kv, kv, kv_meta)


_ROUTE_IDX, _ROUTE_RANK, _ROUTE_GATE, _ROUTE_ROWS = 0, TOP_K, 2 * TOP_K, 16


def _mix_kernel(x_ref, y_ref, at_ref, gs_ref, ga_ref, wglu_ref, wo_ref, wout_ref, fg_ref, rw_ref, rb_ref,
                h1_ref, hf_ref, route_ref, cnt_ref, cnt_scr):
    tm = x_ref.shape[0]

    @pl.when(pl.program_id(0) == 0)
    def _():
        cnt_scr[...] = jnp.zeros_like(cnt_scr)

    y_ssm = jnp.concatenate([y_ref[c * SSM_PITCH:c * SSM_PITCH + SSM_CHUNK, :]
                             for c in range(tm // SSM_CHUNK)], axis=0)
    glu = _dot(jax.nn.gelu(y_ssm).astype(BF16), wglu_ref[...])
    branch_ssm = glu[:, :D_MODEL] * jax.nn.sigmoid(glu[:, D_MODEL:])
    branch_attn = _dot(at_ref[...], wo_ref[...])
    merged = gs_ref[...].astype(F32) * branch_ssm + ga_ref[...].astype(F32) * branch_attn
    h1 = x_ref[...] + _dot(merged.astype(BF16), wout_ref[...])
    h1_ref[...] = h1
    ms = jnp.mean(h1 * h1, axis=-1, keepdims=True)
    hf = h1 * lax.rsqrt(ms + RMS_EPS) * fg_ref[...]
    _store_token_tiles(hf_ref, hf)

    hf_hi = hf.astype(BF16)
    hf_lo = (hf - hf_hi.astype(F32)).astype(BF16)
    hi_prod = _dot(hf_hi, rw_ref[...])
    logits = (hi_prod[:, :LANES] + (hi_prod[:, LANES:] + _dot(hf_lo, rw_ref[:, :LANES]))
              + rb_ref[...])
    lt = logits.T[:N_EXPERTS, :]
    erow = lax.broadcasted_iota(jnp.int32, (N_EXPERTS, tm), 0)
    vals, idxs, hots = [], [], []
    rest = lt
    for _ in range(TOP_K):
        m = jnp.max(rest, axis=0, keepdims=True)
        first = jnp.min(jnp.where(rest == m, erow, N_EXPERTS), axis=0, keepdims=True)
        hot = erow == first
        vals.append(m)
        idxs.append(first)
        hots.append(hot)
        rest = jnp.where(hot, -jnp.inf, rest)
    exps = [jnp.exp(v - vals[0]) for v in vals]
    tot = exps[0] + exps[1] + exps[2] + exps[3]

    sel = (hots[0] | hots[1] | hots[2] | hots[3]).astype(F32)
    ti = lax.broadcasted_iota(jnp.int32, (tm, tm), 0)
    tj = lax.broadcasted_iota(jnp.int32, (tm, tm), 1)
    earlier = (ti < tj).astype(BF16)
    rank_e = _dot(sel.astype(BF16), earlier) + cnt_scr[...]
    cnt_scr[...] = cnt_scr[...] + jnp.sum(sel, axis=1, keepdims=True)
    cnt_ref[...] = cnt_scr[...]

    rrow = lax.broadcasted_iota(jnp.int32, (_ROUTE_ROWS, tm), 0)
    route = jnp.zeros((_ROUTE_ROWS, tm), F32)
    for k in range(TOP_K):
        rank_k = jnp.sum(jnp.where(hots[k], rank_e, 0.0), axis=0, keepdims=True)
        route = jnp.where(rrow == _ROUTE_IDX + k, idxs[k].astype(F32), route)
        route = jnp.where(rrow == _ROUTE_RANK + k, rank_k, route)
        route = jnp.where(rrow == _ROUTE_GATE + k, exps[k] / tot, route)
    route_ref[...] = route


def _mix(x2, y, attn, gs, ga, wglu, wo, wout, fg, rw, rb, tm):
    n = x2.shape[0]
    rw = jnp.pad(rw, ((0, 0), (0, LANES - N_EXPERTS)))
    rb = jnp.pad(rb, ((0, 0), (0, LANES - N_EXPERTS)), constant_values=NEG_INF)
    rw_hi = rw.astype(BF16)
    rw_split = jnp.concatenate([rw_hi, (rw - rw_hi.astype(F32)).astype(BF16)], axis=1)
    row = lambda w: pl.BlockSpec((tm, w), lambda i: (i, 0))
    full = lambda a: pl.BlockSpec(a.shape, lambda i: (0,) * a.ndim)
    return pl.pallas_call(
        _mix_kernel,
        grid=(n // tm,),
        in_specs=[row(D_MODEL), pl.BlockSpec((tm // SSM_CHUNK * SSM_PITCH, SSM_WIDTH), lambda i: (i, 0)),
                  row(Q_WIDTH), row(D_MODEL), row(D_MODEL),
                  full(wglu), full(wo), full(wout), full(fg), full(rw_split), full(rb)],
        out_specs=[row(D_MODEL), pl.BlockSpec((tm * TILE_ROWS, LANES), lambda i: (i, 0)),
                   pl.BlockSpec((_ROUTE_ROWS, tm), lambda i: (0, i)),
                   pl.BlockSpec((N_EXPERTS, 1), lambda i: (0, 0))],
        out_shape=[
            jax.ShapeDtypeStruct((n, D_MODEL), F32),
            jax.ShapeDtypeStruct((n * TILE_ROWS, LANES), F32),
            jax.ShapeDtypeStruct((_ROUTE_ROWS, n), F32),
            jax.ShapeDtypeStruct((N_EXPERTS, 1), F32),
        ],
        scratch_shapes=[pltpu.VMEM((N_EXPERTS, 1), F32)],
        compiler_params=pltpu.CompilerParams(
            dimension_semantics=("arbitrary",), vmem_limit_bytes=VMEM_LIMIT),
        name="mix_router",
    )(x2, y, attn, gs, ga, wglu, wo, wout, fg, rw_split, rb)


def _tiles_wait_copy(src_hbm, dst, n_tiles, sem):
    rows = n_tiles * TILE_ROWS
    return pltpu.make_async_copy(src_hbm.at[pl.ds(0, rows), :], dst.at[pl.ds(0, rows), :], sem)


_ISSUE_UNROLL = 16


_DISPATCH_RING = 4


def _dispatch_kernel(dst_ref, pad_start_ref, pad_len_ref, nu_ref, hf_hbm, xs_hbm, zero_blk, ring, in_sem,
                     out_sem, pad_sem, *, n_steps):
    i = pl.program_id(0)
    last = n_steps - 1
    pairs = dst_ref.shape[0]
    tokens = pairs // TOP_K
    blk_rows = tokens * TILE_ROWS
    n_blocks = xs_hbm.shape[0] // (EXPERT_ROWS * TILE_ROWS)
    slot = i % _DISPATCH_RING

    def load(step, s):
        src = hf_hbm.at[pl.ds(pl.multiple_of(step * blk_rows, blk_rows), blk_rows), :]
        return pltpu.make_async_copy(src, ring.at[s], in_sem.at[s])

    def wait_copies(s):
        for _ in range(TOP_K):
            pltpu.make_async_copy(ring.at[s], xs_hbm.at[pl.ds(0, blk_rows), :], out_sem.at[s]).wait()

    def zero_copy(row, rows):
        src = zero_blk.at[pl.ds(0, rows * TILE_ROWS), :]
        dst = xs_hbm.at[pl.ds(pl.multiple_of(row * TILE_ROWS, TILE_ROWS), rows * TILE_ROWS), :]
        return pltpu.make_async_copy(src, dst, pad_sem)

    def for_each_pad(fn):
        def per_expert(e, carry):
            row, left = pad_start_ref[e], pad_len_ref[e]
            size = EXPERT_ROWS // 2
            while size >= 1:
                take = left & size

                @pl.when(take != 0)
                def _(row=row, size=size):
                    fn(zero_copy(row, size))

                row = row + take
                size //= 2
            return carry
        lax.fori_loop(0, N_EXPERTS, per_expert, 0)

        def per_block(b, carry):
            fn(zero_copy(b * EXPERT_ROWS, EXPERT_ROWS))
            return carry
        lax.fori_loop(nu_ref[0], n_blocks, per_block, 0)

    @pl.when(i == 0)
    def _():
        for s in range(min(2, n_steps)):
            load(s, s).start()
        zero_blk[...] = jnp.zeros_like(zero_blk)
        for_each_pad(lambda cp: cp.start())

    @pl.when(i >= 2)
    def _():
        wait_copies((i + 2) % _DISPATCH_RING)

    @pl.when(i + 2 < n_steps)
    def _():
        load(i + 2, (i + 2) % _DISPATCH_RING).start()

    load(i, slot).wait()
    src_blk = ring.at[slot]
    for k in range(TOP_K):
        def issue(o, carry, k=k):
            tok0 = pl.multiple_of(o * _ISSUE_UNROLL, _ISSUE_UNROLL)
            dsts = [dst_ref[k * tokens + tok0 + r] for r in range(_ISSUE_UNROLL)]
            for r in range(_ISSUE_UNROLL):
                pltpu.make_async_copy(_token_tile(src_blk, tok0 + r), _token_tile(xs_hbm, dsts[r]),
                                      out_sem.at[slot]).start(priority=r % 2)
            return carry

        lax.fori_loop(0, tokens // _ISSUE_UNROLL, issue, 0)

    @pl.when(i == last)
    def _():
        if n_steps > 1:
            wait_copies((i + _DISPATCH_RING - 1) % _DISPATCH_RING)
        wait_copies(slot)
        for_each_pad(lambda cp: cp.wait())


def _dispatch(dest, pad_start, pad_len, n_used, hf_tiles, n_rows, tokens_per_step):
    n = dest.shape[0] // TOP_K
    pairs = tokens_per_step * TOP_K
    assert n % tokens_per_step == 0 and tokens_per_step % _ISSUE_UNROLL == 0
    smem = lambda: pl.BlockSpec(memory_space=pltpu.SMEM)
    n_steps = n // tokens_per_step
    return pl.pallas_call(
        functools.partial(_dispatch_kernel, n_steps=n_steps),
        grid=(n_steps,),
        in_specs=[pl.BlockSpec((pairs,), lambda i: (i,), memory_space=pltpu.SMEM), smem(), smem(), smem(),
                  pl.BlockSpec(memory_space=pl.ANY)],
        out_specs=pl.BlockSpec(memory_space=pl.ANY),
        out_shape=jax.ShapeDtypeStruct((n_rows * TILE_ROWS, LANES), F32),
        scratch_shapes=[pltpu.VMEM((EXPERT_ROWS * TILE_ROWS, LANES), F32),
                        pltpu.VMEM((_DISPATCH_RING, tokens_per_step * TILE_ROWS, LANES), F32),
                        pltpu.SemaphoreType.DMA((_DISPATCH_RING,)), pltpu.SemaphoreType.DMA((_DISPATCH_RING,)),
                        pltpu.SemaphoreType.DMA],
        compiler_params=pltpu.CompilerParams(
            dimension_semantics=("arbitrary",), vmem_limit_bytes=VMEM_LIMIT),
        name="dispatch",
    )(dest, pad_start, pad_len, n_used, hf_tiles)


def _expert_kernel(be_ref, nu_ref, nxt_ref, val_ref, xs_ref, wu_hbm, bu_ref, wd_hbm, bd_ref, y_ref,
                   wu_f32, wd_f32, wu_bf, wd_bf, wsem):
    i = pl.program_id(0)

    def mlp(xb):
        up = _dot(xb, wu_bf[...]) + bu_ref[0]
        x_glu = jnp.minimum(up[:, :D_FF], SWIGLU_LIMIT)
        x_lin = jnp.clip(up[:, D_FF:], -SWIGLU_LIMIT, SWIGLU_LIMIT)
        act = x_glu * jax.nn.sigmoid(SWIGLU_ALPHA * x_glu) * (x_lin + 1.0)
        return _dot(act.astype(BF16), wd_bf[...]) + bd_ref[0]

    def weight_copies(expert):
        return (pltpu.make_async_copy(wu_hbm.at[expert], wu_f32, wsem.at[0]),
                pltpu.make_async_copy(wd_hbm.at[expert], wd_f32, wsem.at[1]))

    @pl.when(i == 0)
    def _():
        for cp in weight_copies(be_ref[0]):
            cp.start()

    @pl.when((i == 0) | (be_ref[i] != be_ref[jnp.maximum(i - 1, 0)]))
    def _():
        for cp in weight_copies(be_ref[i]):
            cp.wait()
        wu_bf[...] = wu_f32[...].astype(BF16)
        wd_bf[...] = wd_f32[...].astype(BF16)

        @pl.when(nxt_ref[i] >= 0)
        def _():
            for cp in weight_copies(nxt_ref[i]):
                cp.start()

    used = i < nu_ref[0]
    full = val_ref[i] > EXPERT_ROWS - EXPERT_SUB_ROWS

    @pl.when(used & full)
    def _():
        _store_token_tiles(y_ref, mlp(_load_token_tiles(xs_ref, 0, EXPERT_ROWS).astype(BF16)))

    @pl.when(used & jnp.logical_not(full))
    def _():
        sub = EXPERT_SUB_ROWS
        n_sub = lax.shift_right_logical(val_ref[i] + (sub - 1), sub.bit_length() - 1)

        def compute(j, carry):
            row0 = pl.multiple_of(j * sub, sub)
            _store_token_tiles(y_ref, mlp(_load_token_tiles(xs_ref, row0, sub).astype(BF16)), row0)
            return carry

        def clear(j, carry):
            row0 = pl.multiple_of(j * sub * TILE_ROWS, sub * TILE_ROWS)
            y_ref[pl.ds(row0, sub * TILE_ROWS), :] = jnp.zeros((sub * TILE_ROWS, LANES), F32)
            return carry

        lax.fori_loop(0, n_sub, compute, 0)
        lax.fori_loop(n_sub, EXPERT_ROWS // sub, clear, 0)

    @pl.when(jnp.logical_not(used))
    def _():
        y_ref[...] = jnp.zeros_like(y_ref)


def _experts(block_expert, n_used, next_expert, valid_rows, xs_tiles, w_up, b_up, w_down, b_down):
    n_blocks = block_expert.shape[0]
    blk = (EXPERT_ROWS * TILE_ROWS, LANES)
    grid_spec = pltpu.PrefetchScalarGridSpec(
        num_scalar_prefetch=4,
        grid=(n_blocks,),
        in_specs=[
            pl.BlockSpec(blk, lambda i, be, nu, *_: (jnp.minimum(i, nu[0] - 1), 0)),
            pl.BlockSpec(memory_space=pl.ANY),
            pl.BlockSpec((1, 1, 2 * D_FF), lambda i, be, *_: (be[i], 0, 0)),
            pl.BlockSpec(memory_space=pl.ANY),
            pl.BlockSpec((1, 1, D_MODEL), lambda i, be, *_: (be[i], 0, 0)),
        ],
        out_specs=pl.BlockSpec(blk, lambda i, *_: (i, 0)),
        scratch_shapes=[
            pltpu.VMEM((D_MODEL, 2 * D_FF), F32),
            pltpu.VMEM((D_FF, D_MODEL), F32),
            pltpu.VMEM((D_MODEL, 2 * D_FF), BF16),
            pltpu.VMEM((D_FF, D_MODEL), BF16),
            pltpu.SemaphoreType.DMA((2,)),
        ],
    )
    return pl.pallas_call(
        _expert_kernel,
        grid_spec=grid_spec,
        out_shape=jax.ShapeDtypeStruct((n_blocks * blk[0], LANES), F32),
        compiler_params=pltpu.CompilerParams(
            dimension_semantics=("arbitrary",), vmem_limit_bytes=VMEM_LIMIT),
        name="experts",
    )(block_expert, n_used, next_expert, valid_rows, xs_tiles,
      w_up, b_up[:, None, :], w_down, b_down[:, None, :])


_COMBINE_RING = 3


def _combine_kernel(dst0_ref, dst1_ref, dst2_ref, y_hbm, h1_ref, route_ref, g_ref, o_ref, *scratch):
    bufs, sem = scratch[:_COMBINE_RING], scratch[_COMBINE_RING]
    i = pl.program_id(0)
    last = pl.num_programs(0) - 1
    tm = h1_ref.shape[0]
    rows = TOP_K * tm

    def gather_group(idx_ref, s, row0):
        srcs = [idx_ref[row0 + r] for r in range(_ISSUE_UNROLL)]
        for r in range(_ISSUE_UNROLL):
            pltpu.make_async_copy(_token_tile(y_hbm, srcs[r]), _token_tile(bufs[s], row0 + r),
                                  sem.at[s]).start(priority=r % 2)

    @pl.when(i == 0)
    def _():
        for s, idx_ref in ((0, dst0_ref), (1, dst1_ref)):
            def body(o, carry, s=s, idx_ref=idx_ref):
                gather_group(idx_ref, s, pl.multiple_of(o * _ISSUE_UNROLL, _ISSUE_UNROLL))
                return carry
            lax.fori_loop(0, rows // _ISSUE_UNROLL, body, 0)

    def step(s):
        cur = bufs[s]
        ahead = (s + 2) % _COMBINE_RING
        _tiles_wait_copy(y_hbm, cur, rows, sem.at[s]).wait()
        for g in range(rows // _ISSUE_UNROLL):
            gather_group(dst2_ref, ahead, g * _ISSUE_UNROLL)
        assert tm == LANES
        rt = jnp.concatenate([route_ref[...], jnp.zeros((LANES - _ROUTE_ROWS, tm), F32)], axis=0).T
        acc = h1_ref[...]
        for k in range(TOP_K):
            gate = rt[:, _ROUTE_GATE + k:_ROUTE_GATE + k + 1]
            acc = acc + gate * _load_token_tiles(cur, k * tm, tm)
        ms = jnp.mean(acc * acc, axis=-1, keepdims=True)
        o_ref[...] = acc * lax.rsqrt(ms + RMS_EPS) * g_ref[...]

        @pl.when(i == last)
        def _():
            for t in ((s + 1) % _COMBINE_RING, ahead):
                _tiles_wait_copy(y_hbm, bufs[t], rows, sem.at[t]).wait()

    for s in range(_COMBINE_RING):
        pl.when(i % _COMBINE_RING == s)(functools.partial(step, s))


def _combine(dest_kmajor, y, h1, route, g, tm):
    n = h1.shape[0]
    n_tiles = n // tm
    idx_spec = lambda ahead: pl.BlockSpec(
        (TOP_K * tm,), lambda i: (jnp.minimum(i + ahead, n_tiles - 1),), memory_space=pltpu.SMEM)
    ring_buf = pltpu.VMEM((TOP_K * tm * TILE_ROWS, LANES), F32)
    return pl.pallas_call(
        _combine_kernel,
        grid=(n_tiles,),
        in_specs=[
            idx_spec(0), idx_spec(1), idx_spec(2),
            pl.BlockSpec(memory_space=pl.ANY),
            pl.BlockSpec((tm, D_MODEL), lambda i: (i, 0)),
            pl.BlockSpec((_ROUTE_ROWS, tm), lambda i: (0, i)),
            pl.BlockSpec((1, D_MODEL), lambda i: (0, 0)),
        ],
        out_specs=pl.BlockSpec((tm, D_MODEL), lambda i: (i, 0)),
        out_shape=jax.ShapeDtypeStruct((n, D_MODEL), F32),
        scratch_shapes=[ring_buf] * _COMBINE_RING + [pltpu.SemaphoreType.DMA((_COMBINE_RING,))],
        compiler_params=pltpu.CompilerParams(
            dimension_semantics=("arbitrary",), vmem_limit_bytes=VMEM_LIMIT),
        name="combine",
    )(dest_kmajor, dest_kmajor, dest_kmajor, y, h1, route, g)


def _routing_tables(route, counts, n, tokens_dispatch, tokens_combine):
    tm = EXPERT_ROWS
    i32 = jnp.int32
    n_blocks = (n * TOP_K + N_EXPERTS * (tm - 1)) // tm
    idx = route[_ROUTE_IDX:_ROUTE_IDX + TOP_K].astype(i32)
    rank = route[_ROUTE_RANK:_ROUTE_RANK + TOP_K].astype(i32)
    cnt = counts[:, 0].astype(i32)
    eid = jnp.arange(N_EXPERTS, dtype=i32)
    upto = eid[None, :] <= eid[:, None]
    blocks_e = (cnt + tm - 1) // tm
    blocks_end = jnp.sum(jnp.where(upto, blocks_e[None, :], 0), axis=1)
    row_start = (blocks_end - blocks_e) * tm
    n_used = blocks_end[N_EXPERTS - 1]
    used = blocks_e > 0

    def lookup(table, keys):
        hit = keys[None] == eid.reshape((N_EXPERTS,) + (1,) * keys.ndim)
        return jnp.sum(jnp.where(hit, table.reshape((N_EXPERTS,) + (1,) * keys.ndim), 0), axis=0)

    dest = lookup(row_start, idx) + rank
    blk = jnp.arange(n_blocks, dtype=i32)
    last_used = jnp.max(jnp.where(used, eid, 0))
    be = jnp.where(blk < n_used, jnp.sum((blocks_end[None, :] <= blk[:, None]).astype(i32), axis=1), last_used)
    later_used = used[None, :] & (eid[None, :] > eid[:, None])
    after = jnp.min(jnp.where(later_used, eid[None, :], N_EXPERTS), axis=1)
    next_e = jnp.where(after < N_EXPERTS, after, -1)

    def k_major(tokens):
        return dest.reshape(TOP_K, n // tokens, tokens).transpose(1, 0, 2).reshape(-1)

    valid = jnp.clip(lookup(row_start + cnt, be) - blk * tm, 0, tm)
    return (be.astype(i32), n_used.reshape(1), lookup(next_e, be).astype(i32), valid.astype(i32),
            k_major(tokens_dispatch), k_major(tokens_combine), row_start + cnt, blocks_e * tm - cnt,
            n_blocks * tm)


def kernel(x, meta_tokens, mix_norm_g, w_in, ssm_a_re, ssm_a_im, ssm_log_dt, ssm_b_re, ssm_b_im,
           ssm_c_re, ssm_c_im, ssm_d, w_ssm_glu, attn_sinks, w_attn_o, w_out, ffn_norm_g,
           router_w, router_b, w_up, b_up, w_down, b_down, final_norm_g):
    bsz, seq, d = x.shape
    assert d == D_MODEL and seq % max(WINDOW, SSM_CHUNK) == 0
    assert mix_norm_g.shape[0] == 1, "single-layer trunk"
    n = bsz * seq
    tm_proj = min(1024, n)
    tm_mix = min(512, n)
    tm_comb = min(128, n)
    x2 = x.reshape(n, D_MODEL)

    w_in_bf = w_in[0].astype(BF16)
    g_mix = mix_norm_g[0][None, :]
    u, q, kv, gs, ga = _in_proj(x2, g_mix, w_in_bf, tm_proj, pitched=True)
    u_m, _, kv_m, _, _ = _in_proj(meta_tokens, g_mix, w_in_bf, N_META, pitched=False)

    ssm_par = _ssm_params(ssm_a_re[0], ssm_a_im[0], ssm_log_dt[0], ssm_b_re[0], ssm_b_im[0],
                          ssm_c_re[0], ssm_c_im[0], ssm_d[0], SSM_CHUNK)
    y_ssm = _ssm(u, u_m, *ssm_par, batch=bsz, chunk=SSM_CHUNK)

    attn = _attention(attn_sinks[0], q, kv, kv_m, seq // WINDOW)

    h1, hf, route, counts = _mix(
        x2, y_ssm, attn, gs, ga, w_ssm_glu[0].astype(BF16), w_attn_o[0].astype(BF16),
        w_out[0].astype(BF16), ffn_norm_g[0][None, :], router_w[0], router_b[0][None, :], tm_mix)

    tok_disp = min(1024, n)
    be, n_used, next_e, valid, dest_disp, dest_comb, pad_start, pad_len, n_rows = _routing_tables(
        route, counts, n, tok_disp, tm_comb)
    xs = _dispatch(dest_disp, pad_start, pad_len, n_used, hf, n_rows, tok_disp)
    y = _experts(be, n_used, next_e, valid, xs, w_up[0], b_up[0], w_down[0], b_down[0])
    out = _combine(dest_comb, y, h1, route, final_norm_g[None, :], tm_comb)
    return out.reshape(bsz, seq, D_MODEL)
```

```python
import functools
import math

import jax
import jax.numpy as jnp
from jax import lax
from jax.experimental import pallas as pl
from jax.experimental.pallas import tpu as pltpu

F32 = jnp.float32
BF16 = jnp.bfloat16

D_MODEL = 1024
N_META = 16
SSM_WIDTH = 512
SSM_GROUP = 16
SSM_GROUPS = 32
SSM_STATE = 64
HEAD_DIM = 64
N_HEADS = 16
N_KV_HEADS = 2
KV_REP = N_HEADS // N_KV_HEADS
WINDOW = 128
Q_WIDTH = N_HEADS * HEAD_DIM
KV_WIDTH = N_KV_HEADS * HEAD_DIM
N_EXPERTS = 32
TOP_K = 4
D_FF = 1024
SWIGLU_ALPHA = 1.702
SWIGLU_LIMIT = 7.0
RMS_EPS = 1e-5
NEG_INF = -1e30

_U0, _Q0, _KV0, _GS0, _GA0, _IN_END = 0, 512, 1536, 1792, 2816, 3840

SSM_CH_BLOCK = 128
SSM_HALF = (SSM_CH_BLOCK // SSM_GROUP) * SSM_STATE
SSM_CHUNK = 32
SSM_PITCH = 40
LOG2_E = math.log2(math.e)
EXPERT_ROWS = 512
EXPERT_SUB_ROWS = 128
VMEM_LIMIT = 56 * 1024 * 1024


def _dot(a, b):
    return jnp.dot(a, b, preferred_element_type=F32)


def _dot_nt(a, b):
    return lax.dot_general(a, b, (((1,), (1,)), ((), ())), preferred_element_type=F32)


LANES = 128
TILE_ROWS = D_MODEL // LANES


def _store_token_tiles(ref, x, start_row=0):
    rows = x.shape[0]
    for j in range(TILE_ROWS):
        ref[pl.ds(start_row * TILE_ROWS + j, rows, stride=TILE_ROWS), :] = x[:, j * LANES:(j + 1) * LANES]


def _load_token_tiles(ref, start_row, rows):
    return jnp.concatenate(
        [ref[pl.ds(start_row * TILE_ROWS + j, rows, stride=TILE_ROWS), :] for j in range(TILE_ROWS)], axis=1)


def _token_tile(ref, row):
    return ref.at[pl.ds(pl.multiple_of(row * TILE_ROWS, TILE_ROWS), TILE_ROWS), :]


def _in_proj_kernel(x_ref, g_ref, w_ref, u_ref, q_ref, kv_ref, gs_ref, ga_ref, *, pitched):
    x = x_ref[...]
    ms = jnp.mean(x * x, axis=-1, keepdims=True)
    hn = (x * lax.rsqrt(ms + RMS_EPS) * g_ref[...]).astype(BF16)
    u = _dot(hn, w_ref[:, _U0:_Q0])
    if pitched:
        for c in range(u.shape[0] // SSM_CHUNK):
            u_ref[c * SSM_PITCH:c * SSM_PITCH + SSM_CHUNK, :] = u[c * SSM_CHUNK:(c + 1) * SSM_CHUNK, :]
            u_ref[c * SSM_PITCH + SSM_CHUNK:(c + 1) * SSM_PITCH, :] = jnp.zeros(
                (SSM_PITCH - SSM_CHUNK, SSM_WIDTH), F32)
    else:
        u_ref[...] = u
    q_ref[...] = (_dot(hn, w_ref[:, _Q0:_KV0]) * (HEAD_DIM ** -0.5 * LOG2_E)).astype(BF16)
    kv_ref[...] = _dot(hn, w_ref[:, _KV0:_GS0]).astype(BF16)
    gs_ref[...] = jax.nn.sigmoid(_dot(hn, w_ref[:, _GS0:_GA0])).astype(BF16)
    ga_ref[...] = jax.nn.sigmoid(_dot(hn, w_ref[:, _GA0:_IN_END])).astype(BF16)


def _in_proj(x2, g, w_bf, tm, pitched):
    n = x2.shape[0]
    row = lambda w: pl.BlockSpec((tm, w), lambda i: (i, 0))
    full = lambda a: pl.BlockSpec(a.shape, lambda i: (0,) * a.ndim)
    u_rows = (lambda r: r // SSM_CHUNK * SSM_PITCH) if pitched else (lambda r: r)
    return pl.pallas_call(
        functools.partial(_in_proj_kernel, pitched=pitched),
        grid=(n // tm,),
        in_specs=[row(D_MODEL), full(g), full(w_bf)],
        out_specs=[pl.BlockSpec((u_rows(tm), SSM_WIDTH), lambda i: (i, 0)),
                   row(Q_WIDTH), row(2 * KV_WIDTH), row(D_MODEL), row(D_MODEL)],
        out_shape=[
            jax.ShapeDtypeStruct((u_rows(n), SSM_WIDTH), F32),
            jax.ShapeDtypeStruct((n, Q_WIDTH), BF16),
            jax.ShapeDtypeStruct((n, 2 * KV_WIDTH), BF16),
            jax.ShapeDtypeStruct((n, D_MODEL), BF16),
            jax.ShapeDtypeStruct((n, D_MODEL), BF16),
        ],
        compiler_params=pltpu.CompilerParams(
            dimension_semantics=("arbitrary",), vmem_limit_bytes=VMEM_LIMIT),
        name="in_proj",
    )(x2, g, w_bf)


def _ssm_kernel(u_ref, um_ref, bm_ref, cm_ref, ar_ref, ai_ref, atr_ref, ati_ref, d_ref,
                y_ref, sre, sim, *, chunk, rows, batch):
    h = SSM_HALF
    bm = bm_ref[0]
    cm = cm_ref[0]
    ar, ai = ar_ref[0], ai_ref[0]
    atr, ati = atr_ref[0], ati_ref[0]
    dsk = d_ref[0]
    n_chunks = rows // batch

    def advance(sr, si, bu):
        return ar * sr - ai * si + bu[:, :h], ar * si + ai * sr + bu[:, h:]

    bum = _dot(um_ref[...].astype(BF16), bm)
    mr = jnp.zeros((1, h), F32)
    mi = jnp.zeros((1, h), F32)
    for j in range(N_META):
        mr, mi = advance(mr, mi, bum[j:j + 1, :])

    def u_step(t):
        return u_ref[pl.ds(t, rows, stride=SSM_PITCH), :]

    for pad in range(chunk, SSM_PITCH):
        y_ref[pl.ds(pad, rows, stride=SSM_PITCH), :] = jnp.zeros((rows, SSM_CH_BLOCK), F32)

    sre[...] = jnp.zeros_like(sre)
    sim[...] = jnp.zeros_like(sim)

    def pass_a(t, carry):
        bu = _dot(u_step(t).astype(BF16), bm)
        nr, ni = advance(sre[...], sim[...], bu)
        sre[...] = nr
        sim[...] = ni
        return carry

    lax.fori_loop(0, chunk, pass_a, 0)

    def over_chunks(c, carry):
        new = []
        for b in range(batch):
            cr, ci = carry[2 * b], carry[2 * b + 1]
            row = pl.ds(b * n_chunks + c, 1)
            er, ei = sre[row, :], sim[row, :]
            sre[row, :] = cr
            sim[row, :] = ci
            new += [atr * cr - ati * ci + er, atr * ci + ati * cr + ei]
        return tuple(new)

    lax.fori_loop(0, n_chunks, over_chunks, (mr, mi) * batch)

    def pass_b(t, carry):
        ut = u_step(t)
        bu = _dot(ut.astype(BF16), bm)
        nr, ni = advance(sre[...], sim[...], bu)
        sre[...] = nr
        sim[...] = ni
        y = _dot(nr.astype(BF16), cm[:h, :]) + _dot(ni.astype(BF16), cm[h:, :]) + dsk * ut
        y_ref[pl.ds(t, rows, stride=SSM_PITCH), :] = y
        return carry

    lax.fori_loop(0, chunk, pass_b, 0)


def _ssm(u, u_meta, bmat, cmat, a_re, a_im, at_re, at_im, dskip, batch, chunk):
    n = u.shape[0]
    rows = n // SSM_PITCH
    nblk = SSM_WIDTH // SSM_CH_BLOCK
    col = lambda r: pl.BlockSpec((r, SSM_CH_BLOCK), lambda j: (0, j))
    par = lambda a: pl.BlockSpec((1,) + a.shape[1:], lambda j: (j, 0, 0))
    return pl.pallas_call(
        functools.partial(_ssm_kernel, chunk=chunk, rows=rows, batch=batch),
        grid=(nblk,),
        in_specs=[col(n), col(N_META), par(bmat), par(cmat), par(a_re), par(a_im),
                  par(at_re), par(at_im), par(dskip)],
        out_specs=col(n),
        out_shape=jax.ShapeDtypeStruct((n, SSM_WIDTH), F32),
        scratch_shapes=[pltpu.VMEM((rows, SSM_HALF), F32), pltpu.VMEM((rows, SSM_HALF), F32)],
        compiler_params=pltpu.CompilerParams(
            dimension_semantics=("arbitrary",), vmem_limit_bytes=VMEM_LIMIT),
        name="ssm",
    )(u, u_meta, bmat, cmat, a_re, a_im, at_re, at_im, dskip)


def _ssm_params(a_re, a_im, log_dt, b_re, b_im, c_re, c_im, d_skip, chunk):
    dt = jnp.exp(log_dt)[:, None]
    mag = jnp.exp(a_re * dt)
    ang = a_im * dt
    abar_re, abar_im = mag * jnp.cos(ang), mag * jnp.sin(ang)
    den = a_re * a_re + a_im * a_im
    nr, ni = abar_re - 1.0, abar_im
    coef_re = ((nr * a_re + ni * a_im) / den)[..., None]
    coef_im = ((ni * a_re - nr * a_im) / den)[..., None]
    bbar_re = coef_re * b_re - coef_im * b_im
    bbar_im = coef_re * b_im + coef_im * b_re
    magt = jnp.exp(a_re * dt * chunk)
    at_re, at_im = magt * jnp.cos(ang * chunk), magt * jnp.sin(ang * chunk)

    nblk = SSM_WIDTH // SSM_CH_BLOCK
    gpb = SSM_GROUPS // nblk
    eye = jnp.eye(gpb, dtype=F32)

    def in_map(b):
        b = b.reshape(nblk, gpb, SSM_STATE, SSM_GROUP)
        return jnp.einsum('jgpc,gh->jgchp', b, eye).reshape(nblk, SSM_CH_BLOCK, gpb * SSM_STATE)

    def out_map(c):
        c = c.reshape(nblk, gpb, SSM_GROUP, SSM_STATE)
        return jnp.einsum('jgcp,gh->jgphc', c, eye).reshape(nblk, gpb * SSM_STATE, SSM_CH_BLOCK)

    bmat = jnp.concatenate([in_map(bbar_re), in_map(bbar_im)], axis=2).astype(BF16)
    cmat = jnp.concatenate([out_map(c_re), -out_map(c_im)], axis=1).astype(BF16)
    vec = lambda v: v.reshape(nblk, 1, SSM_HALF)
    return (bmat, cmat, vec(abar_re), vec(abar_im), vec(at_re), vec(at_im),
            d_skip.reshape(nblk, 1, SSM_CH_BLOCK))


def _attn_kernel(sink_ref, q_ref, kvc_ref, kvp_ref, kvm_ref, o_ref, *, blocks_per_seq):
    n = pl.program_id(0) % blocks_per_seq
    w = WINDOW
    hd = HEAD_DIM
    qi = lax.broadcasted_iota(jnp.int32, (w, w), 0)
    lane = lax.broadcasted_iota(jnp.int32, (w, w), 1)
    vis_prev = (lane > qi) & (n > 0)
    vis_cur = lane <= qi
    left = lane < hd
    meta_l = lane < N_META
    meta_r = (lane >= N_META) & (lane < 2 * N_META)

    def placed(x_bf):
        x = x_bf.astype(F32)
        xr = pltpu.roll(x, hd, 1)
        lm = lax.broadcasted_iota(jnp.int32, x.shape, 1) < hd
        z = jnp.zeros_like(x)
        return {(0, 0): jnp.where(lm, x, z), (0, 1): jnp.where(lm, z, xr),
                (1, 0): jnp.where(lm, xr, z), (1, 1): jnp.where(lm, z, x)}

    kp, kc, km = placed(kvp_ref[:, :KV_WIDTH]), placed(kvc_ref[:, :KV_WIDTH]), placed(kvm_ref[:, :KV_WIDTH])
    vp, vc, vm = placed(kvp_ref[:, KV_WIDTH:]), placed(kvc_ref[:, KV_WIDTH:]), placed(kvm_ref[:, KV_WIDTH:])
    pad_rows = w - 2 * N_META
    zpad = jnp.zeros((pad_rows, w), F32)
    krow = lax.broadcasted_iota(jnp.int32, (5 * w, w), 0)
    klane = lax.broadcasted_iota(jnp.int32, (5 * w, w), 1)
    row_l = (krow < 2 * w) | ((krow >= 4 * w) & (krow < 4 * w + N_META))
    row_r = ((krow >= 2 * w) & (krow < 4 * w)) | ((krow >= 4 * w + N_META) & (krow < 4 * w + 2 * N_META))
    den_cols = jnp.where((row_l & (klane < hd)) | (row_r & (klane >= hd)), 1.0, 0.0)

    for j in range(N_KV_HEADS):
        kcat = jnp.concatenate(
            [kp[j, 0], kc[j, 0], kp[j, 1], kc[j, 1], km[j, 0], km[j, 1], zpad], axis=0).astype(BF16)
        vcat = jnp.concatenate(
            [jnp.concatenate([vp[j, 0], vc[j, 0], vp[j, 1], vc[j, 1], vm[j, 0], vm[j, 1], zpad], axis=0),
             den_cols], axis=1).astype(BF16)
        for r in range(KV_REP // 2):
            pr = j * (KV_REP // 2) + r
            s = _dot_nt(q_ref[:, pr * w:(pr + 1) * w], kcat)
            s_l = jnp.where(vis_cur, s[:, w:2 * w], jnp.where(vis_prev, s[:, 0:w], NEG_INF))
            s_r = jnp.where(vis_cur, s[:, 3 * w:4 * w], jnp.where(vis_prev, s[:, 2 * w:3 * w], NEG_INF))
            s_m = s[:, 4 * w:]
            sink_l, sink_r = sink_ref[2 * pr] * LOG2_E, sink_ref[2 * pr + 1] * LOG2_E
            m_l = jnp.maximum(jnp.max(jnp.maximum(s_l, jnp.where(meta_l, s_m, NEG_INF)),
                                      axis=1, keepdims=True), sink_l)
            m_r = jnp.maximum(jnp.max(jnp.maximum(s_r, jnp.where(meta_r, s_m, NEG_INF)),
                                      axis=1, keepdims=True), sink_r)
            s_m = jnp.where(meta_l, s_m - m_l, jnp.where(meta_r, s_m - m_r, NEG_INF))
            e_l, e_r = jnp.exp2(s_l - m_l), jnp.exp2(s_r - m_r)
            e = jnp.concatenate([jnp.where(vis_cur, 0.0, e_l), jnp.where(vis_cur, e_l, 0.0),
                                 jnp.where(vis_cur, 0.0, e_r), jnp.where(vis_cur, e_r, 0.0),
                                 jnp.exp2(s_m)], axis=1).astype(BF16)
            acc = _dot(e, vcat)
            den = acc[:, w:] + jnp.where(left, jnp.exp2(sink_l - m_l), jnp.exp2(sink_r - m_r))
            o_ref[:, pr * w:(pr + 1) * w] = (acc[:, :w] / den).astype(BF16)


def _attention(sinks, q, kv, kv_meta, blocks_per_seq):
    n = q.shape[0]
    return pl.pallas_call(
        functools.partial(_attn_kernel, blocks_per_seq=blocks_per_seq),
        grid=(n // WINDOW,),
        in_specs=[
            pl.BlockSpec(memory_space=pltpu.SMEM),
            pl.BlockSpec((WINDOW, Q_WIDTH), lambda g: (g, 0)),
            pl.BlockSpec((WINDOW, 2 * KV_WIDTH), lambda g: (g, 0)),
            pl.BlockSpec((WINDOW, 2 * KV_WIDTH), lambda g: (jnp.maximum(g - 1, 0), 0)),
            pl.BlockSpec((N_META, 2 * KV_WIDTH), lambda g: (0, 0)),
        ],
        out_specs=pl.BlockSpec((WINDOW, Q_WIDTH), lambda g: (g, 0)),
        out_shape=jax.ShapeDtypeStruct((n, Q_WIDTH), BF16),
        compiler_params=pltpu.CompilerParams(dimension_semantics=("arbitrary",)),
        name="attn",
    )(sinks, q, kv, kv, kv_meta)


_ROUTE_IDX, _ROUTE_RANK, _ROUTE_GATE, _ROUTE_ROWS = 0, TOP_K, 2 * TOP_K, 16


def _mix_kernel(x_ref, y_ref, at_ref, gs_ref, ga_ref, wglu_ref, wo_ref, wout_ref, fg_ref, rw_ref, rb_ref,
                h1_ref, hf_ref, route_ref, cnt_ref, cnt_scr):
    tm = x_ref.shape[0]

    @pl.when(pl.program_id(0) == 0)
    def _():
        cnt_scr[...] = jnp.zeros_like(cnt_scr)

    y_ssm = jnp.concatenate([y_ref[c * SSM_PITCH:c * SSM_PITCH + SSM_CHUNK, :]
                             for c in range(tm // SSM_CHUNK)], axis=0)
    glu = _dot(jax.nn.gelu(y_ssm).astype(BF16), wglu_ref[...])
    branch_ssm = glu[:, :D_MODEL] * jax.nn.sigmoid(glu[:, D_MODEL:])
    branch_attn = _dot(at_ref[...], wo_ref[...])
    merged = gs_ref[...].astype(F32) * branch_ssm + ga_ref[...].astype(F32) * branch_attn
    h1 = x_ref[...] + _dot(merged.astype(BF16), wout_ref[...])
    h1_ref[...] = h1
    ms = jnp.mean(h1 * h1, axis=-1, keepdims=True)
    hf = h1 * lax.rsqrt(ms + RMS_EPS) * fg_ref[...]
    _store_token_tiles(hf_ref, hf)

    hf_hi = hf.astype(BF16)
    hf_lo = (hf - hf_hi.astype(F32)).astype(BF16)
    hi_prod = _dot(hf_hi, rw_ref[...])
    logits = (hi_prod[:, :LANES] + (hi_prod[:, LANES:] + _dot(hf_lo, rw_ref[:, :LANES]))
              + rb_ref[...])
    lt = logits.T[:N_EXPERTS, :]
    erow = lax.broadcasted_iota(jnp.int32, (N_EXPERTS, tm), 0)
    vals, idxs, hots = [], [], []
    rest = lt
    for _ in range(TOP_K):
        m = jnp.max(rest, axis=0, keepdims=True)
        first = jnp.min(jnp.where(rest == m, erow, N_EXPERTS), axis=0, keepdims=True)
        hot = erow == first
        vals.append(m)
        idxs.append(first)
        hots.append(hot)
        rest = jnp.where(hot, -jnp.inf, rest)
    exps = [jnp.exp(v - vals[0]) for v in vals]
    tot = exps[0] + exps[1] + exps[2] + exps[3]

    sel = (hots[0] | hots[1] | hots[2] | hots[3]).astype(F32)
    ti = lax.broadcasted_iota(jnp.int32, (tm, tm), 0)
    tj = lax.broadcasted_iota(jnp.int32, (tm, tm), 1)
    earlier = (ti < tj).astype(BF16)
    rank_e = _dot(sel.astype(BF16), earlier) + cnt_scr[...]
    cnt_scr[...] = cnt_scr[...] + jnp.sum(sel, axis=1, keepdims=True)
    cnt_ref[...] = cnt_scr[...]

    rrow = lax.broadcasted_iota(jnp.int32, (_ROUTE_ROWS, tm), 0)
    route = jnp.zeros((_ROUTE_ROWS, tm), F32)
    for k in range(TOP_K):
        rank_k = jnp.sum(jnp.where(hots[k], rank_e, 0.0), axis=0, keepdims=True)
        route = jnp.where(rrow == _ROUTE_IDX + k, idxs[k].astype(F32), route)
        route = jnp.where(rrow == _ROUTE_RANK + k, rank_k, route)
        route = jnp.where(rrow == _ROUTE_GATE + k, exps[k] / tot, route)
    route_ref[...] = route


def _mix(x2, y, attn, gs, ga, wglu, wo, wout, fg, rw, rb, tm):
    n = x2.shape[0]
    rw = jnp.pad(rw, ((0, 0), (0, LANES - N_EXPERTS)))
    rb = jnp.pad(rb, ((0, 0), (0, LANES - N_EXPERTS)), constant_values=NEG_INF)
    rw_hi = rw.astype(BF16)
    rw_split = jnp.concatenate([rw_hi, (rw - rw_hi.astype(F32)).astype(BF16)], axis=1)
    row = lambda w: pl.BlockSpec((tm, w), lambda i: (i, 0))
    full = lambda a: pl.BlockSpec(a.shape, lambda i: (0,) * a.ndim)
    return pl.pallas_call(
        _mix_kernel,
        grid=(n // tm,),
        in_specs=[row(D_MODEL), pl.BlockSpec((tm // SSM_CHUNK * SSM_PITCH, SSM_WIDTH), lambda i: (i, 0)),
                  row(Q_WIDTH), row(D_MODEL), row(D_MODEL),
                  full(wglu), full(wo), full(wout), full(fg), full(rw_split), full(rb)],
        out_specs=[row(D_MODEL), pl.BlockSpec((tm * TILE_ROWS, LANES), lambda i: (i, 0)),
                   pl.BlockSpec((_ROUTE_ROWS, tm), lambda i: (0, i)),
                   pl.BlockSpec((N_EXPERTS, 1), lambda i: (0, 0))],
        out_shape=[
            jax.ShapeDtypeStruct((n, D_MODEL), F32),
            jax.ShapeDtypeStruct((n * TILE_ROWS, LANES), F32),
            jax.ShapeDtypeStruct((_ROUTE_ROWS, n), F32),
            jax.ShapeDtypeStruct((N_EXPERTS, 1), F32),
        ],
        scratch_shapes=[pltpu.VMEM((N_EXPERTS, 1), F32)],
        compiler_params=pltpu.CompilerParams(
            dimension_semantics=("arbitrary",), vmem_limit_bytes=VMEM_LIMIT),
        name="mix_router",
    )(x2, y, attn, gs, ga, wglu, wo, wout, fg, rw_split, rb)


def _tiles_wait_copy(src_hbm, dst, n_tiles, sem):
    rows = n_tiles * TILE_ROWS
    return pltpu.make_async_copy(src_hbm.at[pl.ds(0, rows), :], dst.at[pl.ds(0, rows), :], sem)


_ISSUE_UNROLL = 16


_DISPATCH_RING = 4


def _dispatch_kernel(dst_ref, pad_start_ref, pad_len_ref, nu_ref, hf_hbm, xs_hbm, zero_blk, ring, in_sem,
                     out_sem, pad_sem, *, n_steps):
    i = pl.program_id(0)
    last = n_steps - 1
    pairs = dst_ref.shape[0]
    tokens = pairs // TOP_K
    blk_rows = tokens * TILE_ROWS
    n_blocks = xs_hbm.shape[0] // (EXPERT_ROWS * TILE_ROWS)
    slot = i % _DISPATCH_RING

    def load(step, s):
        src = hf_hbm.at[pl.ds(pl.multiple_of(step * blk_rows, blk_rows), blk_rows), :]
        return pltpu.make_async_copy(src, ring.at[s], in_sem.at[s])

    def wait_copies(s):
        for _ in range(TOP_K):
            pltpu.make_async_copy(ring.at[s], xs_hbm.at[pl.ds(0, blk_rows), :], out_sem.at[s]).wait()

    def zero_copy(row, rows):
        src = zero_blk.at[pl.ds(0, rows * TILE_ROWS), :]
        dst = xs_hbm.at[pl.ds(pl.multiple_of(row * TILE_ROWS, TILE_ROWS), rows * TILE_ROWS), :]
        return pltpu.make_async_copy(src, dst, pad_sem)

    def for_each_pad(fn):
        def per_expert(e, carry):
            row, left = pad_start_ref[e], pad_len_ref[e]
            size = EXPERT_ROWS // 2
            while size >= 1:
                take = left & size

                @pl.when(take != 0)
                def _(row=row, size=size):
                    fn(zero_copy(row, size))

                row = row + take
                size //= 2
            return carry
        lax.fori_loop(0, N_EXPERTS, per_expert, 0)

        def per_block(b, carry):
            fn(zero_copy(b * EXPERT_ROWS, EXPERT_ROWS))
            return carry
        lax.fori_loop(nu_ref[0], n_blocks, per_block, 0)

    @pl.when(i == 0)
    def _():
        for s in range(min(2, n_steps)):
            load(s, s).start()
        zero_blk[...] = jnp.zeros_like(zero_blk)
        for_each_pad(lambda cp: cp.start())

    @pl.when(i >= 2)
    def _():
        wait_copies((i + 2) % _DISPATCH_RING)

    @pl.when(i + 2 < n_steps)
    def _():
        load(i + 2, (i + 2) % _DISPATCH_RING).start()

    load(i, slot).wait()
    src_blk = ring.at[slot]
    for k in range(TOP_K):
        def issue(o, carry, k=k):
            tok0 = pl.multiple_of(o * _ISSUE_UNROLL, _ISSUE_UNROLL)
            dsts = [dst_ref[k * tokens + tok0 + r] for r in range(_ISSUE_UNROLL)]
            for r in range(_ISSUE_UNROLL):
                pltpu.make_async_copy(_token_tile(src_blk, tok0 + r), _token_tile(xs_hbm, dsts[r]),
                                      out_sem.at[slot]).start(priority=r % 2)
            return carry

        lax.fori_loop(0, tokens // _ISSUE_UNROLL, issue, 0)

    @pl.when(i == last)
    def _():
        if n_steps > 1:
            wait_copies((i + _DISPATCH_RING - 1) % _DISPATCH_RING)
        wait_copies(slot)
        for_each_pad(lambda cp: cp.wait())


def _dispatch(dest, pad_start, pad_len, n_used, hf_tiles, n_rows, tokens_per_step):
    n = dest.shape[0] // TOP_K
    pairs = tokens_per_step * TOP_K
    assert n % tokens_per_step == 0 and tokens_per_step % _ISSUE_UNROLL == 0
    smem = lambda: pl.BlockSpec(memory_space=pltpu.SMEM)
    n_steps = n // tokens_per_step
    return pl.pallas_call(
        functools.partial(_dispatch_kernel, n_steps=n_steps),
        grid=(n_steps,),
        in_specs=[pl.BlockSpec((pairs,), lambda i: (i,), memory_space=pltpu.SMEM), smem(), smem(), smem(),
                  pl.BlockSpec(memory_space=pl.ANY)],
        out_specs=pl.BlockSpec(memory_space=pl.ANY),
        out_shape=jax.ShapeDtypeStruct((n_rows * TILE_ROWS, LANES), F32),
        scratch_shapes=[pltpu.VMEM((EXPERT_ROWS * TILE_ROWS, LANES), F32),
                        pltpu.VMEM((_DISPATCH_RING, tokens_per_step * TILE_ROWS, LANES), F32),
                        pltpu.SemaphoreType.DMA((_DISPATCH_RING,)), pltpu.SemaphoreType.DMA((_DISPATCH_RING,)),
                        pltpu.SemaphoreType.DMA],
        compiler_params=pltpu.CompilerParams(
            dimension_semantics=("arbitrary",), vmem_limit_bytes=VMEM_LIMIT),
        name="dispatch",
    )(dest, pad_start, pad_len, n_used, hf_tiles)


def _expert_kernel(be_ref, nu_ref, nxt_ref, val_ref, xs_ref, wu_hbm, bu_ref, wd_hbm, bd_ref, y_ref,
                   wu_f32, wd_f32, wu_bf, wd_bf, wsem):
    i = pl.program_id(0)

    def mlp(xb):
        up = _dot(xb, wu_bf[...]) + bu_ref[0]
        x_glu = jnp.minimum(up[:, :D_FF], SWIGLU_LIMIT)
        x_lin = jnp.clip(up[:, D_FF:], -SWIGLU_LIMIT, SWIGLU_LIMIT)
        act = x_glu * jax.nn.sigmoid(SWIGLU_ALPHA * x_glu) * (x_lin + 1.0)
        return _dot(act.astype(BF16), wd_bf[...]) + bd_ref[0]

    def weight_copies(expert):
        return (pltpu.make_async_copy(wu_hbm.at[expert], wu_f32, wsem.at[0]),
                pltpu.make_async_copy(wd_hbm.at[expert], wd_f32, wsem.at[1]))

    @pl.when(i == 0)
    def _():
        for cp in weight_copies(be_ref[0]):
            cp.start()

    @pl.when((i == 0) | (be_ref[i] != be_ref[jnp.maximum(i - 1, 0)]))
    def _():
        for cp in weight_copies(be_ref[i]):
            cp.wait()
        wu_bf[...] = wu_f32[...].astype(BF16)
        wd_bf[...] = wd_f32[...].astype(BF16)

        @pl.when(nxt_ref[i] >= 0)
        def _():
            for cp in weight_copies(nxt_ref[i]):
                cp.start()

    used = i < nu_ref[0]
    full = val_ref[i] > EXPERT_ROWS - EXPERT_SUB_ROWS

    @pl.when(used & full)
    def _():
        _store_token_tiles(y_ref, mlp(_load_token_tiles(xs_ref, 0, EXPERT_ROWS).astype(BF16)))

    @pl.when(used & jnp.logical_not(full))
    def _():
        sub = EXPERT_SUB_ROWS
        n_sub = lax.shift_right_logical(val_ref[i] + (sub - 1), sub.bit_length() - 1)

        def compute(j, carry):
            row0 = pl.multiple_of(j * sub, sub)
            _store_token_tiles(y_ref, mlp(_load_token_tiles(xs_ref, row0, sub).astype(BF16)), row0)
            return carry

        def clear(j, carry):
            row0 = pl.multiple_of(j * sub * TILE_ROWS, sub * TILE_ROWS)
            y_ref[pl.ds(row0, sub * TILE_ROWS), :] = jnp.zeros((sub * TILE_ROWS, LANES), F32)
            return carry

        lax.fori_loop(0, n_sub, compute, 0)
        lax.fori_loop(n_sub, EXPERT_ROWS // sub, clear, 0)

    @pl.when(jnp.logical_not(used))
    def _():
        y_ref[...] = jnp.zeros_like(y_ref)


def _experts(block_expert, n_used, next_expert, valid_rows, xs_tiles, w_up, b_up, w_down, b_down):
    n_blocks = block_expert.shape[0]
    blk = (EXPERT_ROWS * TILE_ROWS, LANES)
    grid_spec = pltpu.PrefetchScalarGridSpec(
        num_scalar_prefetch=4,
        grid=(n_blocks,),
        in_specs=[
            pl.BlockSpec(blk, lambda i, be, nu, *_: (jnp.minimum(i, nu[0] - 1), 0)),
            pl.BlockSpec(memory_space=pl.ANY),
            pl.BlockSpec((1, 1, 2 * D_FF), lambda i, be, *_: (be[i], 0, 0)),
            pl.BlockSpec(memory_space=pl.ANY),
            pl.BlockSpec((1, 1, D_MODEL), lambda i, be, *_: (be[i], 0, 0)),
        ],
        out_specs=pl.BlockSpec(blk, lambda i, *_: (i, 0)),
        scratch_shapes=[
            pltpu.VMEM((D_MODEL, 2 * D_FF), F32),
            pltpu.VMEM((D_FF, D_MODEL), F32),
            pltpu.VMEM((D_MODEL, 2 * D_FF), BF16),
            pltpu.VMEM((D_FF, D_MODEL), BF16),
            pltpu.SemaphoreType.DMA((2,)),
        ],
    )
    return pl.pallas_call(
        _expert_kernel,
        grid_spec=grid_spec,
        out_shape=jax.ShapeDtypeStruct((n_blocks * blk[0], LANES), F32),
        compiler_params=pltpu.CompilerParams(
            dimension_semantics=("arbitrary",), vmem_limit_bytes=VMEM_LIMIT),
        name="experts",
    )(block_expert, n_used, next_expert, valid_rows, xs_tiles,
      w_up, b_up[:, None, :], w_down, b_down[:, None, :])


_COMBINE_RING = 3


def _combine_kernel(dst0_ref, dst1_ref, dst2_ref, y_hbm, h1_ref, route_ref, g_ref, o_ref, *scratch):
    bufs, sem = scratch[:_COMBINE_RING], scratch[_COMBINE_RING]
    i = pl.program_id(0)
    last = pl.num_programs(0) - 1
    tm = h1_ref.shape[0]
    rows = TOP_K * tm

    def gather_group(idx_ref, s, row0):
        srcs = [idx_ref[row0 + r] for r in range(_ISSUE_UNROLL)]
        for r in range(_ISSUE_UNROLL):
            pltpu.make_async_copy(_token_tile(y_hbm, srcs[r]), _token_tile(bufs[s], row0 + r),
                                  sem.at[s]).start(priority=r % 2)

    @pl.when(i == 0)
    def _():
        for s, idx_ref in ((0, dst0_ref), (1, dst1_ref)):
            def body(o, carry, s=s, idx_ref=idx_ref):
                gather_group(idx_ref, s, pl.multiple_of(o * _ISSUE_UNROLL, _ISSUE_UNROLL))
                return carry
            lax.fori_loop(0, rows // _ISSUE_UNROLL, body, 0)

    def step(s):
        cur = bufs[s]
        ahead = (s + 2) % _COMBINE_RING
        _tiles_wait_copy(y_hbm, cur, rows, sem.at[s]).wait()
        for g in range(rows // _ISSUE_UNROLL):
            gather_group(dst2_ref, ahead, g * _ISSUE_UNROLL)
        assert tm == LANES
        rt = jnp.concatenate([route_ref[...], jnp.zeros((LANES - _ROUTE_ROWS, tm), F32)], axis=0).T
        acc = h1_ref[...]
        for k in range(TOP_K):
            gate = rt[:, _ROUTE_GATE + k:_ROUTE_GATE + k + 1]
            acc = acc + gate * _load_token_tiles(cur, k * tm, tm)
        ms = jnp.mean(acc * acc, axis=-1, keepdims=True)
        o_ref[...] = acc * lax.rsqrt(ms + RMS_EPS) * g_ref[...]

        @pl.when(i == last)
        def _():
            for t in ((s + 1) % _COMBINE_RING, ahead):
                _tiles_wait_copy(y_hbm, bufs[t], rows, sem.at[t]).wait()

    for s in range(_COMBINE_RING):
        pl.when(i % _COMBINE_RING == s)(functools.partial(step, s))


def _combine(dest_kmajor, y, h1, route, g, tm):
    n = h1.shape[0]
    n_tiles = n // tm
    idx_spec = lambda ahead: pl.BlockSpec(
        (TOP_K * tm,), lambda i: (jnp.minimum(i + ahead, n_tiles - 1),), memory_space=pltpu.SMEM)
    ring_buf = pltpu.VMEM((TOP_K * tm * TILE_ROWS, LANES), F32)
    return pl.pallas_call(
        _combine_kernel,
        grid=(n_tiles,),
        in_specs=[
            idx_spec(0), idx_spec(1), idx_spec(2),
            pl.BlockSpec(memory_space=pl.ANY),
            pl.BlockSpec((tm, D_MODEL), lambda i: (i, 0)),
            pl.BlockSpec((_ROUTE_ROWS, tm), lambda i: (0, i)),
            pl.BlockSpec((1, D_MODEL), lambda i: (0, 0)),
        ],
        out_specs=pl.BlockSpec((tm, D_MODEL), lambda i: (i, 0)),
        out_shape=jax.ShapeDtypeStruct((n, D_MODEL), F32),
        scratch_shapes=[ring_buf] * _COMBINE_RING + [pltpu.SemaphoreType.DMA((_COMBINE_RING,))],
        compiler_params=pltpu.CompilerParams(
            dimension_semantics=("arbitrary",), vmem_limit_bytes=VMEM_LIMIT),
        name="combine",
    )(dest_kmajor, dest_kmajor, dest_kmajor, y, h1, route, g)


def _routing_tables(route, counts, n, tokens_dispatch, tokens_combine):
    tm = EXPERT_ROWS
    i32 = jnp.int32
    n_blocks = (n * TOP_K + N_EXPERTS * (tm - 1)) // tm
    idx = route[_ROUTE_IDX:_ROUTE_IDX + TOP_K].astype(i32)
    rank = route[_ROUTE_RANK:_ROUTE_RANK + TOP_K].astype(i32)
    cnt = counts[:, 0].astype(i32)
    eid = jnp.arange(N_EXPERTS, dtype=i32)
    upto = eid[None, :] <= eid[:, None]
    blocks_e = (cnt + tm - 1) // tm
    blocks_end = jnp.sum(jnp.where(upto, blocks_e[None, :], 0), axis=1)
    row_start = (blocks_end - blocks_e) * tm
    n_used = blocks_end[N_EXPERTS - 1]
    used = blocks_e > 0

    def lookup(table, keys):
        hit = keys[None] == eid.reshape((N_EXPERTS,) + (1,) * keys.ndim)
        return jnp.sum(jnp.where(hit, table.reshape((N_EXPERTS,) + (1,) * keys.ndim), 0), axis=0)

    dest = lookup(row_start, idx) + rank
    blk = jnp.arange(n_blocks, dtype=i32)
    last_used = jnp.max(jnp.where(used, eid, 0))
    be = jnp.where(blk < n_used, jnp.sum((blocks_end[None, :] <= blk[:, None]).astype(i32), axis=1), last_used)
    later_used = used[None, :] & (eid[None, :] > eid[:, None])
    after = jnp.min(jnp.where(later_used, eid[None, :], N_EXPERTS), axis=1)
    next_e = jnp.where(after < N_EXPERTS, after, -1)

    def k_major(tokens):
        return dest.reshape(TOP_K, n // tokens, tokens).transpose(1, 0, 2).reshape(-1)

    valid = jnp.clip(lookup(row_start + cnt, be) - blk * tm, 0, tm)
    return (be.astype(i32), n_used.reshape(1), lookup(next_e, be).astype(i32), valid.astype(i32),
            k_major(tokens_dispatch), k_major(tokens_combine), row_start + cnt, blocks_e * tm - cnt,
            n_blocks * tm)


def kernel(x, meta_tokens, mix_norm_g, w_in, ssm_a_re, ssm_a_im, ssm_log_dt, ssm_b_re, ssm_b_im,
           ssm_c_re, ssm_c_im, ssm_d, w_ssm_glu, attn_sinks, w_attn_o, w_out, ffn_norm_g,
           router_w, router_b, w_up, b_up, w_down, b_down, final_norm_g):
    bsz, seq, d = x.shape
    assert d == D_MODEL and seq % max(WINDOW, SSM_CHUNK) == 0
    assert mix_norm_g.shape[0] == 1, "single-layer trunk"
    n = bsz * seq
    tm_proj = min(1024, n)
    tm_mix = min(1024, n)
    tm_comb = min(128, n)
    x2 = x.reshape(n, D_MODEL)

    w_in_bf = w_in[0].astype(BF16)
    g_mix = mix_norm_g[0][None, :]
    u, q, kv, gs, ga = _in_proj(x2, g_mix, w_in_bf, tm_proj, pitched=True)
    u_m, _, kv_m, _, _ = _in_proj(meta_tokens, g_mix, w_in_bf, N_META, pitched=False)

    ssm_par = _ssm_params(ssm_a_re[0], ssm_a_im[0], ssm_log_dt[0], ssm_b_re[0], ssm_b_im[0],
                          ssm_c_re[0], ssm_c_im[0], ssm_d[0], SSM_CHUNK)
    y_ssm = _ssm(u, u_m, *ssm_par, batch=bsz, chunk=SSM_CHUNK)

    attn = _attention(attn_sinks[0], q, kv, kv_m, seq // WINDOW)

    h1, hf, route, counts = _mix(
        x2, y_ssm, attn, gs, ga, w_ssm_glu[0].astype(BF16), w_attn_o[0].astype(BF16),
        w_out[0].astype(BF16), ffn_norm_g[0][None, :], router_w[0], router_b[0][None, :], tm_mix)

    tok_disp = min(1024, n)
    be, n_used, next_e, valid, dest_disp, dest_comb, pad_start, pad_len, n_rows = _routing_tables(
        route, counts, n, tok_disp, tm_comb)
    xs = _dispatch(dest_disp, pad_start, pad_len, n_used, hf, n_rows, tok_disp)
    y = _experts(be, n_used, next_e, valid, xs, w_up[0], b_up[0], w_down[0], b_down[0])
    out = _combine(dest_comb, y, h1, route, final_norm_g[None, :], tm_comb)
    return out.reshape(bsz, seq, D_MODEL)
```

```python
import functools
import math

import jax
import jax.numpy as jnp
from jax import lax
from jax.experimental import pallas as pl
from jax.experimental.pallas import tpu as pltpu

F32 = jnp.float32
BF16 = jnp.bfloat16

D_MODEL = 1024
N_META = 16
SSM_WIDTH = 512
SSM_GROUP = 16
SSM_GROUPS = 32
SSM_STATE = 64
HEAD_DIM = 64
N_HEADS = 16
N_KV_HEADS = 2
KV_REP = N_HEADS // N_KV_HEADS
WINDOW = 128
Q_WIDTH = N_HEADS * HEAD_DIM
KV_WIDTH = N_KV_HEADS * HEAD_DIM
N_EXPERTS = 32
TOP_K = 4
D_FF = 1024
SWIGLU_ALPHA = 1.702
SWIGLU_LIMIT = 7.0
RMS_EPS = 1e-5
NEG_INF = -1e30

_U0, _Q0, _KV0, _GS0, _GA0, _IN_END = 0, 512, 1536, 1792, 2816, 3840

SSM_CH_BLOCK = 128
SSM_HALF = (SSM_CH_BLOCK // SSM_GROUP) * SSM_STATE
SSM_CHUNK = 32
SSM_PITCH = 40
LOG2_E = math.log2(math.e)
EXPERT_ROWS = 512
EXPERT_SUB_ROWS = 128
VMEM_LIMIT = 56 * 1024 * 1024


def _dot(a, b):
    return jnp.dot(a, b, preferred_element_type=F32)


def _dot_nt(a, b):
    return lax.dot_general(a, b, (((1,), (1,)), ((), ())), preferred_element_type=F32)


LANES = 128
TILE_ROWS = D_MODEL // LANES


def _store_token_tiles(ref, x, start_row=0):
    rows = x.shape[0]
    for j in range(TILE_ROWS):
        ref[pl.ds(start_row * TILE_ROWS + j, rows, stride=TILE_ROWS), :] = x[:, j * LANES:(j + 1) * LANES]


def _load_token_tiles(ref, start_row, rows):
    return jnp.concatenate(
        [ref[pl.ds(start_row * TILE_ROWS + j, rows, stride=TILE_ROWS), :] for j in range(TILE_ROWS)], axis=1)


def _token_tile(ref, row):
    return ref.at[pl.ds(pl.multiple_of(row * TILE_ROWS, TILE_ROWS), TILE_ROWS), :]


def _in_proj_kernel(x_ref, g_ref, w_ref, u_ref, q_ref, kv_ref, gs_ref, ga_ref, *, pitched):
    x = x_ref[...]
    ms = jnp.mean(x * x, axis=-1, keepdims=True)
    hn = (x * lax.rsqrt(ms + RMS_EPS) * g_ref[...]).astype(BF16)
    u = _dot(hn, w_ref[:, _U0:_Q0])
    if pitched:
        for c in range(u.shape[0] // SSM_CHUNK):
            u_ref[c * SSM_PITCH:c * SSM_PITCH + SSM_CHUNK, :] = u[c * SSM_CHUNK:(c + 1) * SSM_CHUNK, :]
            u_ref[c * SSM_PITCH + SSM_CHUNK:(c + 1) * SSM_PITCH, :] = jnp.zeros(
                (SSM_PITCH - SSM_CHUNK, SSM_WIDTH), F32)
    else:
        u_ref[...] = u
    q_ref[...] = (_dot(hn, w_ref[:, _Q0:_KV0]) * (HEAD_DIM ** -0.5 * LOG2_E)).astype(BF16)
    kv_ref[...] = _dot(hn, w_ref[:, _KV0:_GS0]).astype(BF16)
    gs_ref[...] = jax.nn.sigmoid(_dot(hn, w_ref[:, _GS0:_GA0])).astype(BF16)
    ga_ref[...] = jax.nn.sigmoid(_dot(hn, w_ref[:, _GA0:_IN_END])).astype(BF16)


def _in_proj(x2, g, w_bf, tm, pitched):
    n = x2.shape[0]
    row = lambda w: pl.BlockSpec((tm, w), lambda i: (i, 0))
    full = lambda a: pl.BlockSpec(a.shape, lambda i: (0,) * a.ndim)
    u_rows = (lambda r: r // SSM_CHUNK * SSM_PITCH) if pitched else (lambda r: r)
    return pl.pallas_call(
        functools.partial(_in_proj_kernel, pitched=pitched),
        grid=(n // tm,),
        in_specs=[row(D_MODEL), full(g), full(w_bf)],
        out_specs=[pl.BlockSpec((u_rows(tm), SSM_WIDTH), lambda i: (i, 0)),
                   row(Q_WIDTH), row(2 * KV_WIDTH), row(D_MODEL), row(D_MODEL)],
        out_shape=[
            jax.ShapeDtypeStruct((u_rows(n), SSM_WIDTH), F32),
            jax.ShapeDtypeStruct((n, Q_WIDTH), BF16),
            jax.ShapeDtypeStruct((n, 2 * KV_WIDTH), BF16),
            jax.ShapeDtypeStruct((n, D_MODEL), BF16),
            jax.ShapeDtypeStruct((n, D_MODEL), BF16),
        ],
        compiler_params=pltpu.CompilerParams(
            dimension_semantics=("arbitrary",), vmem_limit_bytes=VMEM_LIMIT),
        name="in_proj",
    )(x2, g, w_bf)


def _ssm_kernel(u_ref, um_ref, bm_ref, cm_ref, ar_ref, ai_ref, atr_ref, ati_ref, d_ref,
                y_ref, sre, sim, *, chunk, rows, batch):
    h = SSM_HALF
    bm = bm_ref[0]
    cm = cm_ref[0]
    ar, ai = ar_ref[0], ai_ref[0]
    atr, ati = atr_ref[0], ati_ref[0]
    dsk = d_ref[0]
    n_chunks = rows // batch

    def advance(sr, si, bu):
        return ar * sr - ai * si + bu[:, :h], ar * si + ai * sr + bu[:, h:]

    bum = _dot(um_ref[...].astype(BF16), bm)
    mr = jnp.zeros((1, h), F32)
    mi = jnp.zeros((1, h), F32)
    for j in range(N_META):
        mr, mi = advance(mr, mi, bum[j:j + 1, :])

    def u_step(t):
        return u_ref[pl.ds(t, rows, stride=SSM_PITCH), :]

    for pad in range(chunk, SSM_PITCH):
        y_ref[pl.ds(pad, rows, stride=SSM_PITCH), :] = jnp.zeros((rows, SSM_CH_BLOCK), F32)

    sre[...] = jnp.zeros_like(sre)
    sim[...] = jnp.zeros_like(sim)

    def pass_a(t, carry):
        bu = _dot(u_step(t).astype(BF16), bm)
        nr, ni = advance(sre[...], sim[...], bu)
        sre[...] = nr
        sim[...] = ni
        return carry

    lax.fori_loop(0, chunk, pass_a, 0)

    def over_chunks(c, carry):
        new = []
        for b in range(batch):
            cr, ci = carry[2 * b], carry[2 * b + 1]
            row = pl.ds(b * n_chunks + c, 1)
            er, ei = sre[row, :], sim[row, :]
            sre[row, :] = cr
            sim[row, :] = ci
            new += [atr * cr - ati * ci + er, atr * ci + ati * cr + ei]
        return tuple(new)

    lax.fori_loop(0, n_chunks, over_chunks, (mr, mi) * batch)

    def pass_b(t, carry):
        ut = u_step(t)
        bu = _dot(ut.astype(BF16), bm)
        nr, ni = advance(sre[...], sim[...], bu)
        sre[...] = nr
        sim[...] = ni
        y = _dot(nr.astype(BF16), cm[:h, :]) + _dot(ni.astype(BF16), cm[h:, :]) + dsk * ut
        y_ref[pl.ds(t, rows, stride=SSM_PITCH), :] = y
        return carry

    lax.fori_loop(0, chunk, pass_b, 0)


def _ssm(u, u_meta, bmat, cmat, a_re, a_im, at_re, at_im, dskip, batch, chunk):
    n = u.shape[0]
    rows = n // SSM_PITCH
    nblk = SSM_WIDTH // SSM_CH_BLOCK
    col = lambda r: pl.BlockSpec((r, SSM_CH_BLOCK), lambda j: (0, j))
    par = lambda a: pl.BlockSpec((1,) + a.shape[1:], lambda j: (j, 0, 0))
    return pl.pallas_call(
        functools.partial(_ssm_kernel, chunk=chunk, rows=rows, batch=batch),
        grid=(nblk,),
        in_specs=[col(n), col(N_META), par(bmat), par(cmat), par(a_re), par(a_im),
                  par(at_re), par(at_im), par(dskip)],
        out_specs=col(n),
        out_shape=jax.ShapeDtypeStruct((n, SSM_WIDTH), F32),
        scratch_shapes=[pltpu.VMEM((rows, SSM_HALF), F32), pltpu.VMEM((rows, SSM_HALF), F32)],
        compiler_params=pltpu.CompilerParams(
            dimension_semantics=("arbitrary",), vmem_limit_bytes=VMEM_LIMIT),
        name="ssm",
    )(u, u_meta, bmat, cmat, a_re, a_im, at_re, at_im, dskip)


def _ssm_params(a_re, a_im, log_dt, b_re, b_im, c_re, c_im, d_skip, chunk):
    dt = jnp.exp(log_dt)[:, None]
    mag = jnp.exp(a_re * dt)
    ang = a_im * dt
    abar_re, abar_im = mag * jnp.cos(ang), mag * jnp.sin(ang)
    den = a_re * a_re + a_im * a_im
    nr, ni = abar_re - 1.0, abar_im
    coef_re = ((nr * a_re + ni * a_im) / den)[..., None]
    coef_im = ((ni * a_re - nr * a_im) / den)[..., None]
    bbar_re = coef_re * b_re - coef_im * b_im
    bbar_im = coef_re * b_im + coef_im * b_re
    magt = jnp.exp(a_re * dt * chunk)
    at_re, at_im = magt * jnp.cos(ang * chunk), magt * jnp.sin(ang * chunk)

    nblk = SSM_WIDTH // SSM_CH_BLOCK
    gpb = SSM_GROUPS // nblk
    eye = jnp.eye(gpb, dtype=F32)

    def in_map(b):
        b = b.reshape(nblk, gpb, SSM_STATE, SSM_GROUP)
        return jnp.einsum('jgpc,gh->jgchp', b, eye).reshape(nblk, SSM_CH_BLOCK, gpb * SSM_STATE)

    def out_map(c):
        c = c.reshape(nblk, gpb, SSM_GROUP, SSM_STATE)
        return jnp.einsum('jgcp,gh->jgphc', c, eye).reshape(nblk, gpb * SSM_STATE, SSM_CH_BLOCK)

    bmat = jnp.concatenate([in_map(bbar_re), in_map(bbar_im)], axis=2).astype(BF16)
    cmat = jnp.concatenate([out_map(c_re), -out_map(c_im)], axis=1).astype(BF16)
    vec = lambda v: v.reshape(nblk, 1, SSM_HALF)
    return (bmat, cmat, vec(abar_re), vec(abar_im), vec(at_re), vec(at_im),
            d_skip.reshape(nblk, 1, SSM_CH_BLOCK))


def _attn_kernel(sink_ref, q_ref, kvc_ref, kvp_ref, kvm_ref, o_ref, *, blocks_per_seq):
    n = pl.program_id(0) % blocks_per_seq
    w = WINDOW
    hd = HEAD_DIM
    qi = lax.broadcasted_iota(jnp.int32, (w, w), 0)
    lane = lax.broadcasted_iota(jnp.int32, (w, w), 1)
    vis_prev = (lane > qi) & (n > 0)
    vis_cur = lane <= qi
    left = lane < hd
    meta_l = lane < N_META
    meta_r = (lane >= N_META) & (lane < 2 * N_META)

    def placed(x_bf):
        x = x_bf.astype(F32)
        xr = pltpu.roll(x, hd, 1)
        lm = lax.broadcasted_iota(jnp.int32, x.shape, 1) < hd
        z = jnp.zeros_like(x)
        return {(0, 0): jnp.where(lm, x, z), (0, 1): jnp.where(lm, z, xr),
                (1, 0): jnp.where(lm, xr, z), (1, 1): jnp.where(lm, z, x)}

    kp, kc, km = placed(kvp_ref[:, :KV_WIDTH]), placed(kvc_ref[:, :KV_WIDTH]), placed(kvm_ref[:, :KV_WIDTH])
    vp, vc, vm = placed(kvp_ref[:, KV_WIDTH:]), placed(kvc_ref[:, KV_WIDTH:]), placed(kvm_ref[:, KV_WIDTH:])
    pad_rows = w - 2 * N_META
    zpad = jnp.zeros((pad_rows, w), F32)
    krow = lax.broadcasted_iota(jnp.int32, (5 * w, w), 0)
    klane = lax.broadcasted_iota(jnp.int32, (5 * w, w), 1)
    row_l = (krow < 2 * w) | ((krow >= 4 * w) & (krow < 4 * w + N_META))
    row_r = ((krow >= 2 * w) & (krow < 4 * w)) | ((krow >= 4 * w + N_META) & (krow < 4 * w + 2 * N_META))
    den_cols = jnp.where((row_l & (klane < hd)) | (row_r & (klane >= hd)), 1.0, 0.0)

    for j in range(N_KV_HEADS):
        kcat = jnp.concatenate(
            [kp[j, 0], kc[j, 0], kp[j, 1], kc[j, 1], km[j, 0], km[j, 1], zpad], axis=0).astype(BF16)
        vcat = jnp.concatenate(
            [jnp.concatenate([vp[j, 0], vc[j, 0], vp[j, 1], vc[j, 1], vm[j, 0], vm[j, 1], zpad], axis=0),
             den_cols], axis=1).astype(BF16)
        for r in range(KV_REP // 2):
            pr = j * (KV_REP // 2) + r
            s = _dot_nt(q_ref[:, pr * w:(pr + 1) * w], kcat)
            s_l = jnp.where(vis_cur, s[:, w:2 * w], jnp.where(vis_prev, s[:, 0:w], NEG_INF))
            s_r = jnp.where(vis_cur, s[:, 3 * w:4 * w], jnp.where(vis_prev, s[:, 2 * w:3 * w], NEG_INF))
            s_m = s[:, 4 * w:]
            sink_l, sink_r = sink_ref[2 * pr] * LOG2_E, sink_ref[2 * pr + 1] * LOG2_E
            m_l = jnp.maximum(jnp.max(jnp.maximum(s_l, jnp.where(meta_l, s_m, NEG_INF)),
                                      axis=1, keepdims=True), sink_l)
            m_r = jnp.maximum(jnp.max(jnp.maximum(s_r, jnp.where(meta_r, s_m, NEG_INF)),
                                      axis=1, keepdims=True), sink_r)
            s_m = jnp.where(meta_l, s_m - m_l, jnp.where(meta_r, s_m - m_r, NEG_INF))
            e_l, e_r = jnp.exp2(s_l - m_l), jnp.exp2(s_r - m_r)
            e = jnp.concatenate([jnp.where(vis_cur, 0.0, e_l), jnp.where(vis_cur, e_l, 0.0),
                                 jnp.where(vis_cur, 0.0, e_r), jnp.where(vis_cur, e_r, 0.0),
                                 jnp.exp2(s_m)], axis=1).astype(BF16)
            acc = _dot(e, vcat)
            den = acc[:, w:] + jnp.where(left, jnp.exp2(sink_l - m_l), jnp.exp2(sink_r - m_r))
            o_ref[:, pr * w:(pr + 1) * w] = (acc[:, :w] / den).astype(BF16)


def _attention(sinks, q, kv, kv_meta, blocks_per_seq):
    n = q.shape[0]
    return pl.pallas_call(
        functools.partial(_attn_kernel, blocks_per_seq=blocks_per_seq),
        grid=(n // WINDOW,),
        in_specs=[
            pl.BlockSpec(memory_space=pltpu.SMEM),
            pl.BlockSpec((WINDOW, Q_WIDTH), lambda g: (g, 0)),
            pl.BlockSpec((WINDOW, 2 * KV_WIDTH), lambda g: (g, 0)),
            pl.BlockSpec((WINDOW, 2 * KV_WIDTH), lambda g: (jnp.maximum(g - 1, 0), 0)),
            pl.BlockSpec((N_META, 2 * KV_WIDTH), lambda g: (0, 0)),
        ],
        out_specs=pl.BlockSpec((WINDOW, Q_WIDTH), lambda g: (g, 0)),
        out_shape=jax.ShapeDtypeStruct((n, Q_WIDTH), BF16),
        compiler_params=pltpu.CompilerParams(dimension_semantics=("arbitrary",)),
        name="attn",
    )(sinks, q, kv, kv, kv_meta)


_ROUTE_IDX, _ROUTE_RANK, _ROUTE_GATE, _ROUTE_ROWS = 0, TOP_K, 2 * TOP_K, 16


def _mix_kernel(x_ref, y_ref, at_ref, gs_ref, ga_ref, wglu_ref, wo_ref, wout_ref, fg_ref, rw_ref, rb_ref,
                h1_ref, hf_ref, route_ref, cnt_ref, cnt_scr):
    tm = x_ref.shape[0]

    @pl.when(pl.program_id(0) == 0)
    def _():
        cnt_scr[...] = jnp.zeros_like(cnt_scr)

    y_ssm = jnp.concatenate([y_ref[c * SSM_PITCH:c * SSM_PITCH + SSM_CHUNK, :]
                             for c in range(tm // SSM_CHUNK)], axis=0)
    glu = _dot(jax.nn.gelu(y_ssm).astype(BF16), wglu_ref[...])
    branch_ssm = glu[:, :D_MODEL] * jax.nn.sigmoid(glu[:, D_MODEL:])
    branch_attn = _dot(at_ref[...], wo_ref[...])
    merged = gs_ref[...].astype(F32) * branch_ssm + ga_ref[...].astype(F32) * branch_attn
    h1 = x_ref[...] + _dot(merged.astype(BF16), wout_ref[...])
    h1_ref[...] = h1
    ms = jnp.mean(h1 * h1, axis=-1, keepdims=True)
    hf = h1 * lax.rsqrt(ms + RMS_EPS) * fg_ref[...]
    _store_token_tiles(hf_ref, hf)

    hf_hi = hf.astype(BF16)
    hf_lo = (hf - hf_hi.astype(F32)).astype(BF16)
    hi_prod = _dot(hf_hi, rw_ref[...])
    logits = (hi_prod[:, :LANES] + (hi_prod[:, LANES:] + _dot(hf_lo, rw_ref[:, :LANES]))
              + rb_ref[...])
    lt = logits.T[:N_EXPERTS, :]
    erow = lax.broadcasted_iota(jnp.int32, (N_EXPERTS, tm), 0)
    vals, idxs, hots = [], [], []
    rest = lt
    for _ in range(TOP_K):
        m = jnp.max(rest, axis=0, keepdims=True)
        first = jnp.min(jnp.where(rest == m, erow, N_EXPERTS), axis=0, keepdims=True)
        hot = erow == first
        vals.append(m)
        idxs.append(first)
        hots.append(hot)
        rest = jnp.where(hot, -jnp.inf, rest)
    exps = [jnp.exp(v - vals[0]) for v in vals]
    tot = exps[0] + exps[1] + exps[2] + exps[3]

    sel = (hots[0] | hots[1] | hots[2] | hots[3]).astype(F32)
    ti = lax.broadcasted_iota(jnp.int32, (tm, tm), 0)
    tj = lax.broadcasted_iota(jnp.int32, (tm, tm), 1)
    earlier = (ti < tj).astype(BF16)
    rank_e = _dot(sel.astype(BF16), earlier) + cnt_scr[...]
    cnt_scr[...] = cnt_scr[...] + jnp.sum(sel, axis=1, keepdims=True)
    cnt_ref[...] = cnt_scr[...]

    rrow = lax.broadcasted_iota(jnp.int32, (_ROUTE_ROWS, tm), 0)
    route = jnp.zeros((_ROUTE_ROWS, tm), F32)
    for k in range(TOP_K):
        rank_k = jnp.sum(jnp.where(hots[k], rank_e, 0.0), axis=0, keepdims=True)
        route = jnp.where(rrow == _ROUTE_IDX + k, idxs[k].astype(F32), route)
        route = jnp.where(rrow == _ROUTE_RANK + k, rank_k, route)
        route = jnp.where(rrow == _ROUTE_GATE + k, exps[k] / tot, route)
    route_ref[...] = route


def _mix(x2, y, attn, gs, ga, wglu, wo, wout, fg, rw, rb, tm):
    n = x2.shape[0]
    rw = jnp.pad(rw, ((0, 0), (0, LANES - N_EXPERTS)))
    rb = jnp.pad(rb, ((0, 0), (0, LANES - N_EXPERTS)), constant_values=NEG_INF)
    rw_hi = rw.astype(BF16)
    rw_split = jnp.concatenate([rw_hi, (rw - rw_hi.astype(F32)).astype(BF16)], axis=1)
    row = lambda w: pl.BlockSpec((tm, w), lambda i: (i, 0))
    full = lambda a: pl.BlockSpec(a.shape, lambda i: (0,) * a.ndim)
    return pl.pallas_call(
        _mix_kernel,
        grid=(n // tm,),
        in_specs=[row(D_MODEL), pl.BlockSpec((tm // SSM_CHUNK * SSM_PITCH, SSM_WIDTH), lambda i: (i, 0)),
                  row(Q_WIDTH), row(D_MODEL), row(D_MODEL),
                  full(wglu), full(wo), full(wout), full(fg), full(rw_split), full(rb)],
        out_specs=[row(D_MODEL), pl.BlockSpec((tm * TILE_ROWS, LANES), lambda i: (i, 0)),
                   pl.BlockSpec((_ROUTE_ROWS, tm), lambda i: (0, i)),
                   pl.BlockSpec((N_EXPERTS, 1), lambda i: (0, 0))],
        out_shape=[
            jax.ShapeDtypeStruct((n, D_MODEL), F32),
            jax.ShapeDtypeStruct((n * TILE_ROWS, LANES), F32),
            jax.ShapeDtypeStruct((_ROUTE_ROWS, n), F32),
            jax.ShapeDtypeStruct((N_EXPERTS, 1), F32),
        ],
        scratch_shapes=[pltpu.VMEM((N_EXPERTS, 1), F32)],
        compiler_params=pltpu.CompilerParams(
            dimension_semantics=("arbitrary",), vmem_limit_bytes=VMEM_LIMIT),
        name="mix_router",
    )(x2, y, attn, gs, ga, wglu, wo, wout, fg, rw_split, rb)


def _tiles_wait_copy(src_hbm, dst, n_tiles, sem):
    rows = n_tiles * TILE_ROWS
    return pltpu.make_async_copy(src_hbm.at[pl.ds(0, rows), :], dst.at[pl.ds(0, rows), :], sem)


_ISSUE_UNROLL = 16


_DISPATCH_RING = 4


def _dispatch_kernel(dst_ref, pad_start_ref, pad_len_ref, nu_ref, hf_hbm, xs_hbm, zero_blk, ring, in_sem,
                     out_sem, pad_sem, *, n_steps):
    i = pl.program_id(0)
    last = n_steps - 1
    pairs = dst_ref.shape[0]
    tokens = pairs // TOP_K
    blk_rows = tokens * TILE_ROWS
    n_blocks = xs_hbm.shape[0] // (EXPERT_ROWS * TILE_ROWS)
    slot = i % _DISPATCH_RING

    def load(step, s):
        src = hf_hbm.at[pl.ds(pl.multiple_of(step * blk_rows, blk_rows), blk_rows), :]
        return pltpu.make_async_copy(src, ring.at[s], in_sem.at[s])

    def wait_copies(s):
        for _ in range(TOP_K):
            pltpu.make_async_copy(ring.at[s], xs_hbm.at[pl.ds(0, blk_rows), :], out_sem.at[s]).wait()

    def zero_copy(row, rows):
        src = zero_blk.at[pl.ds(0, rows * TILE_ROWS), :]
        dst = xs_hbm.at[pl.ds(pl.multiple_of(row * TILE_ROWS, TILE_ROWS), rows * TILE_ROWS), :]
        return pltpu.make_async_copy(src, dst, pad_sem)

    def for_each_pad(fn):
        def per_expert(e, carry):
            row, left = pad_start_ref[e], pad_len_ref[e]
            size = EXPERT_ROWS // 2
            while size >= 1:
                take = left & size

                @pl.when(take != 0)
                def _(row=row, size=size):
                    fn(zero_copy(row, size))

                row = row + take
                size //= 2
            return carry
        lax.fori_loop(0, N_EXPERTS, per_expert, 0)

        def per_block(b, carry):
            fn(zero_copy(b * EXPERT_ROWS, EXPERT_ROWS))
            return carry
        lax.fori_loop(nu_ref[0], n_blocks, per_block, 0)

    @pl.when(i == 0)
    def _():
        for s in range(min(2, n_steps)):
            load(s, s).start()
        zero_blk[...] = jnp.zeros_like(zero_blk)
        for_each_pad(lambda cp: cp.start())

    @pl.when(i >= 2)
    def _():
        wait_copies((i + 2) % _DISPATCH_RING)

    @pl.when(i + 2 < n_steps)
    def _():
        load(i + 2, (i + 2) % _DISPATCH_RING).start()

    load(i, slot).wait()
    src_blk = ring.at[slot]
    for k in range(TOP_K):
        def issue(o, carry, k=k):
            tok0 = pl.multiple_of(o * _ISSUE_UNROLL, _ISSUE_UNROLL)
            dsts = [dst_ref[k * tokens + tok0 + r] for r in range(_ISSUE_UNROLL)]
            for r in range(_ISSUE_UNROLL):
                pltpu.make_async_copy(_token_tile(src_blk, tok0 + r), _token_tile(xs_hbm, dsts[r]),
                                      out_sem.at[slot]).start(priority=r % 2)
            return carry

        lax.fori_loop(0, tokens // _ISSUE_UNROLL, issue, 0)

    @pl.when(i == last)
    def _():
        if n_steps > 1:
            wait_copies((i + _DISPATCH_RING - 1) % _DISPATCH_RING)
        wait_copies(slot)
        for_each_pad(lambda cp: cp.wait())


def _dispatch(dest, pad_start, pad_len, n_used, hf_tiles, n_rows, tokens_per_step):
    n = dest.shape[0] // TOP_K
    pairs = tokens_per_step * TOP_K
    assert n % tokens_per_step == 0 and tokens_per_step % _ISSUE_UNROLL == 0
    smem = lambda: pl.BlockSpec(memory_space=pltpu.SMEM)
    n_steps = n // tokens_per_step
    return pl.pallas_call(
        functools.partial(_dispatch_kernel, n_steps=n_steps),
        grid=(n_steps,),
        in_specs=[pl.BlockSpec((pairs,), lambda i: (i,), memory_space=pltpu.SMEM), smem(), smem(), smem(),
                  pl.BlockSpec(memory_space=pl.ANY)],
        out_specs=pl.BlockSpec(memory_space=pl.ANY),
        out_shape=jax.ShapeDtypeStruct((n_rows * TILE_ROWS, LANES), F32),
        scratch_shapes=[pltpu.VMEM((EXPERT_ROWS * TILE_ROWS, LANES), F32),
                        pltpu.VMEM((_DISPATCH_RING, tokens_per_step * TILE_ROWS, LANES), F32),
                        pltpu.SemaphoreType.DMA((_DISPATCH_RING,)), pltpu.SemaphoreType.DMA((_DISPATCH_RING,)),
                        pltpu.SemaphoreType.DMA],
        compiler_params=pltpu.CompilerParams(
            dimension_semantics=("arbitrary",), vmem_limit_bytes=VMEM_LIMIT),
        name="dispatch",
    )(dest, pad_start, pad_len, n_used, hf_tiles)


def _expert_kernel(be_ref, nu_ref, nxt_ref, val_ref, xs_ref, wu_hbm, bu_ref, wd_hbm, bd_ref, y_ref,
                   wu_f32, wd_f32, wu_bf, wd_bf, wsem):
    i = pl.program_id(0)

    def mlp(xb):
        up = _dot(xb, wu_bf[...]) + bu_ref[0]
        x_glu = jnp.minimum(up[:, :D_FF], SWIGLU_LIMIT)
        x_lin = jnp.clip(up[:, D_FF:], -SWIGLU_LIMIT, SWIGLU_LIMIT)
        act = x_glu * jax.nn.sigmoid(SWIGLU_ALPHA * x_glu) * (x_lin + 1.0)
        return _dot(act.astype(BF16), wd_bf[...]) + bd_ref[0]

    def weight_copies(expert):
        return (pltpu.make_async_copy(wu_hbm.at[expert], wu_f32, wsem.at[0]),
                pltpu.make_async_copy(wd_hbm.at[expert], wd_f32, wsem.at[1]))

    @pl.when(i == 0)
    def _():
        for cp in weight_copies(be_ref[0]):
            cp.start()

    first = (i == 0) | (be_ref[i] != be_ref[jnp.maximum(i - 1, 0)])
    used = i < nu_ref[0]
    full = used & (val_ref[i] > EXPERT_ROWS - EXPERT_SUB_ROWS)

    def take_weights():
        for cp in weight_copies(be_ref[i]):
            cp.wait()
        wu_bf[...] = wu_f32[...].astype(BF16)
        wd_bf[...] = wd_f32[...].astype(BF16)

    def fetch_next_weights():
        @pl.when(nxt_ref[i] >= 0)
        def _():
            for cp in weight_copies(nxt_ref[i]):
                cp.start()

    def whole_block():
        _store_token_tiles(y_ref, mlp(_load_token_tiles(xs_ref, 0, EXPERT_ROWS).astype(BF16)))

    @pl.when(first & full)
    def _():
        take_weights()
        whole_block()
        fetch_next_weights()

    @pl.when(first & jnp.logical_not(full))
    def _():
        take_weights()
        fetch_next_weights()

    @pl.when(full & jnp.logical_not(first))
    def _():
        whole_block()

    @pl.when(used & jnp.logical_not(full))
    def _():
        sub = EXPERT_SUB_ROWS
        n_sub = lax.shift_right_logical(val_ref[i] + (sub - 1), sub.bit_length() - 1)

        def compute(j, carry):
            row0 = pl.multiple_of(j * sub, sub)
            _store_token_tiles(y_ref, mlp(_load_token_tiles(xs_ref, row0, sub).astype(BF16)), row0)
            return carry

        def clear(j, carry):
            row0 = pl.multiple_of(j * sub * TILE_ROWS, sub * TILE_ROWS)
            y_ref[pl.ds(row0, sub * TILE_ROWS), :] = jnp.zeros((sub * TILE_ROWS, LANES), F32)
            return carry

        lax.fori_loop(0, n_sub, compute, 0)
        lax.fori_loop(n_sub, EXPERT_ROWS // sub, clear, 0)

    @pl.when(jnp.logical_not(used))
    def _():
        y_ref[...] = jnp.zeros_like(y_ref)


def _experts(block_expert, n_used, next_expert, valid_rows, xs_tiles, w_up, b_up, w_down, b_down):
    n_blocks = block_expert.shape[0]
    blk = (EXPERT_ROWS * TILE_ROWS, LANES)
    grid_spec = pltpu.PrefetchScalarGridSpec(
        num_scalar_prefetch=4,
        grid=(n_blocks,),
        in_specs=[
            pl.BlockSpec(blk, lambda i, be, nu, *_: (jnp.minimum(i, nu[0] - 1), 0)),
            pl.BlockSpec(memory_space=pl.ANY),
            pl.BlockSpec((1, 1, 2 * D_FF), lambda i, be, *_: (be[i], 0, 0)),
            pl.BlockSpec(memory_space=pl.ANY),
            pl.BlockSpec((1, 1, D_MODEL), lambda i, be, *_: (be[i], 0, 0)),
        ],
        out_specs=pl.BlockSpec(blk, lambda i, *_: (i, 0)),
        scratch_shapes=[
            pltpu.VMEM((D_MODEL, 2 * D_FF), F32),
            pltpu.VMEM((D_FF, D_MODEL), F32),
            pltpu.VMEM((D_MODEL, 2 * D_FF), BF16),
            pltpu.VMEM((D_FF, D_MODEL), BF16),
            pltpu.SemaphoreType.DMA((2,)),
        ],
    )
    return pl.pallas_call(
        _expert_kernel,
        grid_spec=grid_spec,
        out_shape=jax.ShapeDtypeStruct((n_blocks * blk[0], LANES), F32),
        compiler_params=pltpu.CompilerParams(
            dimension_semantics=("arbitrary",), vmem_limit_bytes=VMEM_LIMIT),
        name="experts",
    )(block_expert, n_used, next_expert, valid_rows, xs_tiles,
      w_up, b_up[:, None, :], w_down, b_down[:, None, :])


_COMBINE_RING = 3


def _combine_kernel(dst0_ref, dst1_ref, dst2_ref, y_hbm, h1_ref, route_ref, g_ref, o_ref, *scratch):
    bufs, sem = scratch[:_COMBINE_RING], scratch[_COMBINE_RING]
    i = pl.program_id(0)
    last = pl.num_programs(0) - 1
    tm = h1_ref.shape[0]
    rows = TOP_K * tm

    def gather_group(idx_ref, s, row0):
        srcs = [idx_ref[row0 + r] for r in range(_ISSUE_UNROLL)]
        for r in range(_ISSUE_UNROLL):
            pltpu.make_async_copy(_token_tile(y_hbm, srcs[r]), _token_tile(bufs[s], row0 + r),
                                  sem.at[s]).start(priority=r % 2)

    @pl.when(i == 0)
    def _():
        for s, idx_ref in ((0, dst0_ref), (1, dst1_ref)):
            def body(o, carry, s=s, idx_ref=idx_ref):
                gather_group(idx_ref, s, pl.multiple_of(o * _ISSUE_UNROLL, _ISSUE_UNROLL))
                return carry
            lax.fori_loop(0, rows // _ISSUE_UNROLL, body, 0)

    def step(s):
        cur = bufs[s]
        ahead = (s + 2) % _COMBINE_RING
        _tiles_wait_copy(y_hbm, cur, rows, sem.at[s]).wait()
        for g in range(rows // _ISSUE_UNROLL):
            gather_group(dst2_ref, ahead, g * _ISSUE_UNROLL)
        assert tm == LANES
        rt = jnp.concatenate([route_ref[...], jnp.zeros((LANES - _ROUTE_ROWS, tm), F32)], axis=0).T
        acc = h1_ref[...]
        for k in range(TOP_K):
            gate = rt[:, _ROUTE_GATE + k:_ROUTE_GATE + k + 1]
            acc = acc + gate * _load_token_tiles(cur, k * tm, tm)
        ms = jnp.mean(acc * acc, axis=-1, keepdims=True)
        o_ref[...] = acc * lax.rsqrt(ms + RMS_EPS) * g_ref[...]

        @pl.when(i == last)
        def _():
            for t in ((s + 1) % _COMBINE_RING, ahead):
                _tiles_wait_copy(y_hbm, bufs[t], rows, sem.at[t]).wait()

    for s in range(_COMBINE_RING):
        pl.when(i % _COMBINE_RING == s)(functools.partial(step, s))


def _combine(dest_kmajor, y, h1, route, g, tm):
    n = h1.shape[0]
    n_tiles = n // tm
    idx_spec = lambda ahead: pl.BlockSpec(
        (TOP_K * tm,), lambda i: (jnp.minimum(i + ahead, n_tiles - 1),), memory_space=pltpu.SMEM)
    ring_buf = pltpu.VMEM((TOP_K * tm * TILE_ROWS, LANES), F32)
    return pl.pallas_call(
        _combine_kernel,
        grid=(n_tiles,),
        in_specs=[
            idx_spec(0), idx_spec(1), idx_spec(2),
            pl.BlockSpec(memory_space=pl.ANY),
            pl.BlockSpec((tm, D_MODEL), lambda i: (i, 0)),
            pl.BlockSpec((_ROUTE_ROWS, tm), lambda i: (0, i)),
            pl.BlockSpec((1, D_MODEL), lambda i: (0, 0)),
        ],
        out_specs=pl.BlockSpec((tm, D_MODEL), lambda i: (i, 0)),
        out_shape=jax.ShapeDtypeStruct((n, D_MODEL), F32),
        scratch_shapes=[ring_buf] * _COMBINE_RING + [pltpu.SemaphoreType.DMA((_COMBINE_RING,))],
        compiler_params=pltpu.CompilerParams(
            dimension_semantics=("arbitrary",), vmem_limit_bytes=VMEM_LIMIT),
        name="combine",
    )(dest_kmajor, dest_kmajor, dest_kmajor, y, h1, route, g)


def _routing_tables(route, counts, n, tokens_dispatch, tokens_combine):
    tm = EXPERT_ROWS
    i32 = jnp.int32
    n_blocks = (n * TOP_K + N_EXPERTS * (tm - 1)) // tm
    idx = route[_ROUTE_IDX:_ROUTE_IDX + TOP_K].astype(i32)
    rank = route[_ROUTE_RANK:_ROUTE_RANK + TOP_K].astype(i32)
    cnt = counts[:, 0].astype(i32)
    eid = jnp.arange(N_EXPERTS, dtype=i32)
    upto = eid[None, :] <= eid[:, None]
    blocks_e = (cnt + tm - 1) // tm
    blocks_end = jnp.sum(jnp.where(upto, blocks_e[None, :], 0), axis=1)
    row_start = (blocks_end - blocks_e) * tm
    n_used = blocks_end[N_EXPERTS - 1]
    used = blocks_e > 0

    def lookup(table, keys):
        hit = keys[None] == eid.reshape((N_EXPERTS,) + (1,) * keys.ndim)
        return jnp.sum(jnp.where(hit, table.reshape((N_EXPERTS,) + (1,) * keys.ndim), 0), axis=0)

    dest = lookup(row_start, idx) + rank
    blk = jnp.arange(n_blocks, dtype=i32)
    last_used = jnp.max(jnp.where(used, eid, 0))
    be = jnp.where(blk < n_used, jnp.sum((blocks_end[None, :] <= blk[:, None]).astype(i32), axis=1), last_used)
    later_used = used[None, :] & (eid[None, :] > eid[:, None])
    after = jnp.min(jnp.where(later_used, eid[None, :], N_EXPERTS), axis=1)
    next_e = jnp.where(after < N_EXPERTS, after, -1)

    def k_major(tokens):
        return dest.reshape(TOP_K, n // tokens, tokens).transpose(1, 0, 2).reshape(-1)

    valid = jnp.clip(lookup(row_start + cnt, be) - blk * tm, 0, tm)
    return (be.astype(i32), n_used.reshape(1), lookup(next_e, be).astype(i32), valid.astype(i32),
            k_major(tokens_dispatch), k_major(tokens_combine), row_start + cnt, blocks_e * tm - cnt,
            n_blocks * tm)


def kernel(x, meta_tokens, mix_norm_g, w_in, ssm_a_re, ssm_a_im, ssm_log_dt, ssm_b_re, ssm_b_im,
           ssm_c_re, ssm_c_im, ssm_d, w_ssm_glu, attn_sinks, w_attn_o, w_out, ffn_norm_g,
           router_w, router_b, w_up, b_up, w_down, b_down, final_norm_g):
    bsz, seq, d = x.shape
    assert d == D_MODEL and seq % max(WINDOW, SSM_CHUNK) == 0
    assert mix_norm_g.shape[0] == 1, "single-layer trunk"
    n = bsz * seq
    tm_proj = min(1024, n)
    tm_mix = min(1024, n)
    tm_comb = min(128, n)
    x2 = x.reshape(n, D_MODEL)

    w_in_bf = w_in[0].astype(BF16)
    g_mix = mix_norm_g[0][None, :]
    u, q, kv, gs, ga = _in_proj(x2, g_mix, w_in_bf, tm_proj, pitched=True)
    u_m, _, kv_m, _, _ = _in_proj(meta_tokens, g_mix, w_in_bf, N_META, pitched=False)

    ssm_par = _ssm_params(ssm_a_re[0], ssm_a_im[0], ssm_log_dt[0], ssm_b_re[0], ssm_b_im[0],
                          ssm_c_re[0], ssm_c_im[0], ssm_d[0], SSM_CHUNK)
    y_ssm = _ssm(u, u_m, *ssm_par, batch=bsz, chunk=SSM_CHUNK)

    attn = _attention(attn_sinks[0], q, kv, kv_m, seq // WINDOW)

    h1, hf, route, counts = _mix(
        x2, y_ssm, attn, gs, ga, w_ssm_glu[0].astype(BF16), w_attn_o[0].astype(BF16),
        w_out[0].astype(BF16), ffn_norm_g[0][None, :], router_w[0], router_b[0][None, :], tm_mix)

    tok_disp = min(1024, n)
    be, n_used, next_e, valid, dest_disp, dest_comb, pad_start, pad_len, n_rows = _routing_tables(
        route, counts, n, tok_disp, tm_comb)
    xs = _dispatch(dest_disp, pad_start, pad_len, n_used, hf, n_rows, tok_disp)
    y = _experts(be, n_used, next_e, valid, xs, w_up[0], b_up[0], w_down[0], b_down[0])
    out = _combine(dest_comb, y, h1, route, final_norm_g[None, :], tm_comb)
    return out.reshape(bsz, seq, D_MODEL)
```

```python
import functools
import math

import jax
import jax.numpy as jnp
from jax import lax
from jax.experimental import pallas as pl
from jax.experimental.pallas import tpu as pltpu

F32 = jnp.float32
BF16 = jnp.bfloat16

D_MODEL = 1024
N_META = 16
SSM_WIDTH = 512
SSM_GROUP = 16
SSM_GROUPS = 32
SSM_STATE = 64
HEAD_DIM = 64
N_HEADS = 16
N_KV_HEADS = 2
KV_REP = N_HEADS // N_KV_HEADS
WINDOW = 128
Q_WIDTH = N_HEADS * HEAD_DIM
KV_WIDTH = N_KV_HEADS * HEAD_DIM
N_EXPERTS = 32
TOP_K = 4
D_FF = 1024
SWIGLU_ALPHA = 1.702
SWIGLU_LIMIT = 7.0
RMS_EPS = 1e-5
NEG_INF = -1e30

_U0, _Q0, _KV0, _GS0, _GA0, _IN_END = 0, 512, 1536, 1792, 2816, 3840

SSM_CH_BLOCK = 128
SSM_HALF = (SSM_CH_BLOCK // SSM_GROUP) * SSM_STATE
SSM_CHUNK = 32
SSM_PITCH = 40
SSM_UNROLL_A = 8
SSM_UNROLL_B = 4
LOG2_E = math.log2(math.e)
EXPERT_ROWS = 512
EXPERT_SUB_ROWS = 128
VMEM_LIMIT = 56 * 1024 * 1024


def _dot(a, b):
    return jnp.dot(a, b, preferred_element_type=F32)


def _dot_nt(a, b):
    return lax.dot_general(a, b, (((1,), (1,)), ((), ())), preferred_element_type=F32)


LANES = 128
TILE_ROWS = D_MODEL // LANES


def _store_token_tiles(ref, x, start_row=0):
    rows = x.shape[0]
    for j in range(TILE_ROWS):
        ref[pl.ds(start_row * TILE_ROWS + j, rows, stride=TILE_ROWS), :] = x[:, j * LANES:(j + 1) * LANES]


def _load_token_tiles(ref, start_row, rows):
    return jnp.concatenate(
        [ref[pl.ds(start_row * TILE_ROWS + j, rows, stride=TILE_ROWS), :] for j in range(TILE_ROWS)], axis=1)


def _token_tile(ref, row):
    return ref.at[pl.ds(pl.multiple_of(row * TILE_ROWS, TILE_ROWS), TILE_ROWS), :]


def _in_proj_kernel(x_ref, g_ref, w_ref, u_ref, q_ref, kv_ref, gs_ref, ga_ref, *, pitched):
    x = x_ref[...]
    ms = jnp.mean(x * x, axis=-1, keepdims=True)
    hn = (x * lax.rsqrt(ms + RMS_EPS) * g_ref[...]).astype(BF16)
    u = _dot(hn, w_ref[:, _U0:_Q0])
    if pitched:
        for c in range(u.shape[0] // SSM_CHUNK):
            u_ref[c * SSM_PITCH:c * SSM_PITCH + SSM_CHUNK, :] = u[c * SSM_CHUNK:(c + 1) * SSM_CHUNK, :]
            u_ref[c * SSM_PITCH + SSM_CHUNK:(c + 1) * SSM_PITCH, :] = jnp.zeros(
                (SSM_PITCH - SSM_CHUNK, SSM_WIDTH), F32)
    else:
        u_ref[...] = u
    q_ref[...] = (_dot(hn, w_ref[:, _Q0:_KV0]) * (HEAD_DIM ** -0.5 * LOG2_E)).astype(BF16)
    kv_ref[...] = _dot(hn, w_ref[:, _KV0:_GS0]).astype(BF16)
    gs_ref[...] = jax.nn.sigmoid(_dot(hn, w_ref[:, _GS0:_GA0])).astype(BF16)
    ga_ref[...] = jax.nn.sigmoid(_dot(hn, w_ref[:, _GA0:_IN_END])).astype(BF16)


def _in_proj(x2, g, w_bf, tm, pitched):
    n = x2.shape[0]
    row = lambda w: pl.BlockSpec((tm, w), lambda i: (i, 0))
    full = lambda a: pl.BlockSpec(a.shape, lambda i: (0,) * a.ndim)
    u_rows = (lambda r: r // SSM_CHUNK * SSM_PITCH) if pitched else (lambda r: r)
    return pl.pallas_call(
        functools.partial(_in_proj_kernel, pitched=pitched),
        grid=(n // tm,),
        in_specs=[row(D_MODEL), full(g), full(w_bf)],
        out_specs=[pl.BlockSpec((u_rows(tm), SSM_WIDTH), lambda i: (i, 0)),
                   row(Q_WIDTH), row(2 * KV_WIDTH), row(D_MODEL), row(D_MODEL)],
        out_shape=[
            jax.ShapeDtypeStruct((u_rows(n), SSM_WIDTH), F32),
            jax.ShapeDtypeStruct((n, Q_WIDTH), BF16),
            jax.ShapeDtypeStruct((n, 2 * KV_WIDTH), BF16),
            jax.ShapeDtypeStruct((n, D_MODEL), BF16),
            jax.ShapeDtypeStruct((n, D_MODEL), BF16),
        ],
        compiler_params=pltpu.CompilerParams(
            dimension_semantics=("arbitrary",), vmem_limit_bytes=VMEM_LIMIT),
        name="in_proj",
    )(x2, g, w_bf)


def _ssm_kernel(u_ref, um_ref, bm_ref, cm_ref, ar_ref, ai_ref, atr_ref, ati_ref, d_ref,
                y_ref, sre, sim, *, chunk, rows, batch):
    h = SSM_HALF
    bm = bm_ref[0]
    cm = cm_ref[0]
    ar, ai = ar_ref[0], ai_ref[0]
    atr, ati = atr_ref[0], ati_ref[0]
    dsk = d_ref[0]
    n_chunks = rows // batch

    def advance(sr, si, bu):
        return ar * sr - ai * si + bu[:, :h], ar * si + ai * sr + bu[:, h:]

    bum = _dot(um_ref[...].astype(BF16), bm)
    mr = jnp.zeros((1, h), F32)
    mi = jnp.zeros((1, h), F32)
    for j in range(N_META):
        mr, mi = advance(mr, mi, bum[j:j + 1, :])

    def u_step(t):
        return u_ref[pl.ds(t, rows, stride=SSM_PITCH), :]

    for pad in range(chunk, SSM_PITCH):
        y_ref[pl.ds(pad, rows, stride=SSM_PITCH), :] = jnp.zeros((rows, SSM_CH_BLOCK), F32)

    sre[...] = jnp.zeros_like(sre)
    sim[...] = jnp.zeros_like(sim)

    def pass_a(o, carry):
        for k in range(SSM_UNROLL_A):
            t = o * SSM_UNROLL_A + k
            bu = _dot(u_step(t).astype(BF16), bm)
            nr, ni = advance(sre[...], sim[...], bu)
            sre[...] = nr
            sim[...] = ni
        return carry

    lax.fori_loop(0, chunk // SSM_UNROLL_A, pass_a, 0)

    def over_chunks(c, carry):
        new = []
        for b in range(batch):
            cr, ci = carry[2 * b], carry[2 * b + 1]
            row = pl.ds(b * n_chunks + c, 1)
            er, ei = sre[row, :], sim[row, :]
            sre[row, :] = cr
            sim[row, :] = ci
            new += [atr * cr - ati * ci + er, atr * ci + ati * cr + ei]
        return tuple(new)

    lax.fori_loop(0, n_chunks, over_chunks, (mr, mi) * batch)

    def pass_b(o, carry):
        for k in range(SSM_UNROLL_B):
            t = o * SSM_UNROLL_B + k
            ut = u_step(t)
            bu = _dot(ut.astype(BF16), bm)
            nr, ni = advance(sre[...], sim[...], bu)
            sre[...] = nr
            sim[...] = ni
            y = _dot(nr.astype(BF16), cm[:h, :]) + _dot(ni.astype(BF16), cm[h:, :]) + dsk * ut
            y_ref[pl.ds(t, rows, stride=SSM_PITCH), :] = y
        return carry

    lax.fori_loop(0, chunk // SSM_UNROLL_B, pass_b, 0)


def _ssm(u, u_meta, bmat, cmat, a_re, a_im, at_re, at_im, dskip, batch, chunk):
    n = u.shape[0]
    rows = n // SSM_PITCH
    nblk = SSM_WIDTH // SSM_CH_BLOCK
    col = lambda r: pl.BlockSpec((r, SSM_CH_BLOCK), lambda j: (0, j))
    par = lambda a: pl.BlockSpec((1,) + a.shape[1:], lambda j: (j, 0, 0))
    return pl.pallas_call(
        functools.partial(_ssm_kernel, chunk=chunk, rows=rows, batch=batch),
        grid=(nblk,),
        in_specs=[col(n), col(N_META), par(bmat), par(cmat), par(a_re), par(a_im),
                  par(at_re), par(at_im), par(dskip)],
        out_specs=col(n),
        out_shape=jax.ShapeDtypeStruct((n, SSM_WIDTH), F32),
        scratch_shapes=[pltpu.VMEM((rows, SSM_HALF), F32), pltpu.VMEM((rows, SSM_HALF), F32)],
        compiler_params=pltpu.CompilerParams(
            dimension_semantics=("arbitrary",), vmem_limit_bytes=VMEM_LIMIT),
        name="ssm",
    )(u, u_meta, bmat, cmat, a_re, a_im, at_re, at_im, dskip)


def _ssm_params(a_re, a_im, log_dt, b_re, b_im, c_re, c_im, d_skip, chunk):
    dt = jnp.exp(log_dt)[:, None]
    mag = jnp.exp(a_re * dt)
    ang = a_im * dt
    abar_re, abar_im = mag * jnp.cos(ang), mag * jnp.sin(ang)
    den = a_re * a_re + a_im * a_im
    nr, ni = abar_re - 1.0, abar_im
    coef_re = ((nr * a_re + ni * a_im) / den)[..., None]
    coef_im = ((ni * a_re - nr * a_im) / den)[..., None]
    bbar_re = coef_re * b_re - coef_im * b_im
    bbar_im = coef_re * b_im + coef_im * b_re
    magt = jnp.exp(a_re * dt * chunk)
    at_re, at_im = magt * jnp.cos(ang * chunk), magt * jnp.sin(ang * chunk)

    nblk = SSM_WIDTH // SSM_CH_BLOCK
    gpb = SSM_GROUPS // nblk
    eye = jnp.eye(gpb, dtype=F32)

    def in_map(b):
        b = b.reshape(nblk, gpb, SSM_STATE, SSM_GROUP)
        return jnp.einsum('jgpc,gh->jgchp', b, eye).reshape(nblk, SSM_CH_BLOCK, gpb * SSM_STATE)

    def out_map(c):
        c = c.reshape(nblk, gpb, SSM_GROUP, SSM_STATE)
        return jnp.einsum('jgcp,gh->jgphc', c, eye).reshape(nblk, gpb * SSM_STATE, SSM_CH_BLOCK)

    bmat = jnp.concatenate([in_map(bbar_re), in_map(bbar_im)], axis=2).astype(BF16)
    cmat = jnp.concatenate([out_map(c_re), -out_map(c_im)], axis=1).astype(BF16)
    vec = lambda v: v.reshape(nblk, 1, SSM_HALF)
    return (bmat, cmat, vec(abar_re), vec(abar_im), vec(at_re), vec(at_im),
            d_skip.reshape(nblk, 1, SSM_CH_BLOCK))


def _attn_kernel(sink_ref, q_ref, kvc_ref, kvp_ref, kvm_ref, o_ref, *, blocks_per_seq):
    n = pl.program_id(0) % blocks_per_seq
    w = WINDOW
    hd = HEAD_DIM
    qi = lax.broadcasted_iota(jnp.int32, (w, w), 0)
    lane = lax.broadcasted_iota(jnp.int32, (w, w), 1)
    vis_prev = (lane > qi) & (n > 0)
    vis_cur = lane <= qi
    left = lane < hd
    meta_l = lane < N_META
    meta_r = (lane >= N_META) & (lane < 2 * N_META)

    def placed(x_bf):
        x = x_bf.astype(F32)
        xr = pltpu.roll(x, hd, 1)
        lm = lax.broadcasted_iota(jnp.int32, x.shape, 1) < hd
        z = jnp.zeros_like(x)
        return {(0, 0): jnp.where(lm, x, z), (0, 1): jnp.where(lm, z, xr),
                (1, 0): jnp.where(lm, xr, z), (1, 1): jnp.where(lm, z, x)}

    kp, kc, km = placed(kvp_ref[:, :KV_WIDTH]), placed(kvc_ref[:, :KV_WIDTH]), placed(kvm_ref[:, :KV_WIDTH])
    vp, vc, vm = placed(kvp_ref[:, KV_WIDTH:]), placed(kvc_ref[:, KV_WIDTH:]), placed(kvm_ref[:, KV_WIDTH:])
    pad_rows = w - 2 * N_META
    zpad = jnp.zeros((pad_rows, w), F32)
    krow = lax.broadcasted_iota(jnp.int32, (5 * w, w), 0)
    klane = lax.broadcasted_iota(jnp.int32, (5 * w, w), 1)
    row_l = (krow < 2 * w) | ((krow >= 4 * w) & (krow < 4 * w + N_META))
    row_r = ((krow >= 2 * w) & (krow < 4 * w)) | ((krow >= 4 * w + N_META) & (krow < 4 * w + 2 * N_META))
    den_cols = jnp.where((row_l & (klane < hd)) | (row_r & (klane >= hd)), 1.0, 0.0)

    for j in range(N_KV_HEADS):
        kcat = jnp.concatenate(
            [kp[j, 0], kc[j, 0], kp[j, 1], kc[j, 1], km[j, 0], km[j, 1], zpad], axis=0).astype(BF16)
        vcat = jnp.concatenate(
            [jnp.concatenate([vp[j, 0], vc[j, 0], vp[j, 1], vc[j, 1], vm[j, 0], vm[j, 1], zpad], axis=0),
             den_cols], axis=1).astype(BF16)
        for r in range(KV_REP // 2):
            pr = j * (KV_REP // 2) + r
            s = _dot_nt(q_ref[:, pr * w:(pr + 1) * w], kcat)
            s_l = jnp.where(vis_cur, s[:, w:2 * w], jnp.where(vis_prev, s[:, 0:w], NEG_INF))
            s_r = jnp.where(vis_cur, s[:, 3 * w:4 * w], jnp.where(vis_prev, s[:, 2 * w:3 * w], NEG_INF))
            s_m = s[:, 4 * w:]
            sink_l, sink_r = sink_ref[2 * pr] * LOG2_E, sink_ref[2 * pr + 1] * LOG2_E
            m_l = jnp.maximum(jnp.max(jnp.maximum(s_l, jnp.where(meta_l, s_m, NEG_INF)),
                                      axis=1, keepdims=True), sink_l)
            m_r = jnp.maximum(jnp.max(jnp.maximum(s_r, jnp.where(meta_r, s_m, NEG_INF)),
                                      axis=1, keepdims=True), sink_r)
            s_m = jnp.where(meta_l, s_m - m_l, jnp.where(meta_r, s_m - m_r, NEG_INF))
            e_l, e_r = jnp.exp2(s_l - m_l), jnp.exp2(s_r - m_r)
            e = jnp.concatenate([jnp.where(vis_cur, 0.0, e_l), jnp.where(vis_cur, e_l, 0.0),
                                 jnp.where(vis_cur, 0.0, e_r), jnp.where(vis_cur, e_r, 0.0),
                                 jnp.exp2(s_m)], axis=1).astype(BF16)
            acc = _dot(e, vcat)
            den = acc[:, w:] + jnp.where(left, jnp.exp2(sink_l - m_l), jnp.exp2(sink_r - m_r))
            o_ref[:, pr * w:(pr + 1) * w] = (acc[:, :w] / den).astype(BF16)


def _attention(sinks, q, kv, kv_meta, blocks_per_seq):
    n = q.shape[0]
    return pl.pallas_call(
        functools.partial(_attn_kernel, blocks_per_seq=blocks_per_seq),
        grid=(n // WINDOW,),
        in_specs=[
            pl.BlockSpec(memory_space=pltpu.SMEM),
            pl.BlockSpec((WINDOW, Q_WIDTH), lambda g: (g, 0)),
            pl.BlockSpec((WINDOW, 2 * KV_WIDTH), lambda g: (g, 0)),
            pl.BlockSpec((WINDOW, 2 * KV_WIDTH), lambda g: (jnp.maximum(g - 1, 0), 0)),
            pl.BlockSpec((N_META, 2 * KV_WIDTH), lambda g: (0, 0)),
        ],
        out_specs=pl.BlockSpec((WINDOW, Q_WIDTH), lambda g: (g, 0)),
        out_shape=jax.ShapeDtypeStruct((n, Q_WIDTH), BF16),
        compiler_params=pltpu.CompilerParams(dimension_semantics=("arbitrary",)),
        name="attn",
    )(sinks, q, kv, kv, kv_meta)


_ROUTE_IDX, _ROUTE_RANK, _ROUTE_GATE, _ROUTE_ROWS = 0, TOP_K, 2 * TOP_K, 16


def _mix_kernel(x_ref, y_ref, at_ref, gs_ref, ga_ref, wglu_ref, wo_ref, wout_ref, fg_ref, rw_ref, rb_ref,
                h1_ref, hf_ref, route_ref, cnt_ref, cnt_scr):
    tm = x_ref.shape[0]

    @pl.when(pl.program_id(0) == 0)
    def _():
        cnt_scr[...] = jnp.zeros_like(cnt_scr)

    y_ssm = jnp.concatenate([y_ref[c * SSM_PITCH:c * SSM_PITCH + SSM_CHUNK, :]
                             for c in range(tm // SSM_CHUNK)], axis=0)
    glu = _dot(jax.nn.gelu(y_ssm).astype(BF16), wglu_ref[...])
    branch_ssm = glu[:, :D_MODEL] * jax.nn.sigmoid(glu[:, D_MODEL:])
    branch_attn = _dot(at_ref[...], wo_ref[...])
    merged = gs_ref[...].astype(F32) * branch_ssm + ga_ref[...].astype(F32) * branch_attn
    h1 = x_ref[...] + _dot(merged.astype(BF16), wout_ref[...])
    h1_ref[...] = h1
    ms = jnp.mean(h1 * h1, axis=-1, keepdims=True)
    hf = h1 * lax.rsqrt(ms + RMS_EPS) * fg_ref[...]
    _store_token_tiles(hf_ref, hf)

    hf_hi = hf.astype(BF16)
    hf_lo = (hf - hf_hi.astype(F32)).astype(BF16)
    hi_prod = _dot(hf_hi, rw_ref[...])
    logits = (hi_prod[:, :LANES] + (hi_prod[:, LANES:] + _dot(hf_lo, rw_ref[:, :LANES]))
              + rb_ref[...])
    lt = logits.T[:N_EXPERTS, :]
    erow = lax.broadcasted_iota(jnp.int32, (N_EXPERTS, tm), 0)
    vals, idxs, hots = [], [], []
    rest = lt
    for _ in range(TOP_K):
        m = jnp.max(rest, axis=0, keepdims=True)
        first = jnp.min(jnp.where(rest == m, erow, N_EXPERTS), axis=0, keepdims=True)
        hot = erow == first
        vals.append(m)
        idxs.append(first)
        hots.append(hot)
        rest = jnp.where(hot, -jnp.inf, rest)
    exps = [jnp.exp(v - vals[0]) for v in vals]
    tot = exps[0] + exps[1] + exps[2] + exps[3]

    sel = (hots[0] | hots[1] | hots[2] | hots[3]).astype(F32)
    ti = lax.broadcasted_iota(jnp.int32, (tm, tm), 0)
    tj = lax.broadcasted_iota(jnp.int32, (tm, tm), 1)
    earlier = (ti < tj).astype(BF16)
    rank_e = _dot(sel.astype(BF16), earlier) + cnt_scr[...]
    cnt_scr[...] = cnt_scr[...] + jnp.sum(sel, axis=1, keepdims=True)
    cnt_ref[...] = cnt_scr[...]

    rrow = lax.broadcasted_iota(jnp.int32, (_ROUTE_ROWS, tm), 0)
    route = jnp.zeros((_ROUTE_ROWS, tm), F32)
    for k in range(TOP_K):
        rank_k = jnp.sum(jnp.where(hots[k], rank_e, 0.0), axis=0, keepdims=True)
        route = jnp.where(rrow == _ROUTE_IDX + k, idxs[k].astype(F32), route)
        route = jnp.where(rrow == _ROUTE_RANK + k, rank_k, route)
        route = jnp.where(rrow == _ROUTE_GATE + k, exps[k] / tot, route)
    route_ref[...] = route


def _mix(x2, y, attn, gs, ga, wglu, wo, wout, fg, rw, rb, tm):
    n = x2.shape[0]
    rw = jnp.pad(rw, ((0, 0), (0, LANES - N_EXPERTS)))
    rb = jnp.pad(rb, ((0, 0), (0, LANES - N_EXPERTS)), constant_values=NEG_INF)
    rw_hi = rw.astype(BF16)
    rw_split = jnp.concatenate([rw_hi, (rw - rw_hi.astype(F32)).astype(BF16)], axis=1)
    row = lambda w: pl.BlockSpec((tm, w), lambda i: (i, 0))
    full = lambda a: pl.BlockSpec(a.shape, lambda i: (0,) * a.ndim)
    return pl.pallas_call(
        _mix_kernel,
        grid=(n // tm,),
        in_specs=[row(D_MODEL), pl.BlockSpec((tm // SSM_CHUNK * SSM_PITCH, SSM_WIDTH), lambda i: (i, 0)),
                  row(Q_WIDTH), row(D_MODEL), row(D_MODEL),
                  full(wglu), full(wo), full(wout), full(fg), full(rw_split), full(rb)],
        out_specs=[row(D_MODEL), pl.BlockSpec((tm * TILE_ROWS, LANES), lambda i: (i, 0)),
                   pl.BlockSpec((_ROUTE_ROWS, tm), lambda i: (0, i)),
                   pl.BlockSpec((N_EXPERTS, 1), lambda i: (0, 0))],
        out_shape=[
            jax.ShapeDtypeStruct((n, D_MODEL), F32),
            jax.ShapeDtypeStruct((n * TILE_ROWS, LANES), F32),
            jax.ShapeDtypeStruct((_ROUTE_ROWS, n), F32),
            jax.ShapeDtypeStruct((N_EXPERTS, 1), F32),
        ],
        scratch_shapes=[pltpu.VMEM((N_EXPERTS, 1), F32)],
        compiler_params=pltpu.CompilerParams(
            dimension_semantics=("arbitrary",), vmem_limit_bytes=VMEM_LIMIT),
        name="mix_router",
    )(x2, y, attn, gs, ga, wglu, wo, wout, fg, rw_split, rb)


def _tiles_wait_copy(src_hbm, dst, n_tiles, sem):
    rows = n_tiles * TILE_ROWS
    return pltpu.make_async_copy(src_hbm.at[pl.ds(0, rows), :], dst.at[pl.ds(0, rows), :], sem)


_ISSUE_UNROLL = 16


_DISPATCH_RING = 4


def _dispatch_kernel(dst_ref, pad_start_ref, pad_len_ref, nu_ref, hf_hbm, xs_hbm, zero_blk, ring, in_sem,
                     out_sem, pad_sem, *, n_steps):
    i = pl.program_id(0)
    last = n_steps - 1
    pairs = dst_ref.shape[0]
    tokens = pairs // TOP_K
    blk_rows = tokens * TILE_ROWS
    n_blocks = xs_hbm.shape[0] // (EXPERT_ROWS * TILE_ROWS)
    slot = i % _DISPATCH_RING

    def load(step, s):
        src = hf_hbm.at[pl.ds(pl.multiple_of(step * blk_rows, blk_rows), blk_rows), :]
        return pltpu.make_async_copy(src, ring.at[s], in_sem.at[s])

    def wait_copies(s):
        for _ in range(TOP_K):
            pltpu.make_async_copy(ring.at[s], xs_hbm.at[pl.ds(0, blk_rows), :], out_sem.at[s]).wait()

    def zero_copy(row, rows):
        src = zero_blk.at[pl.ds(0, rows * TILE_ROWS), :]
        dst = xs_hbm.at[pl.ds(pl.multiple_of(row * TILE_ROWS, TILE_ROWS), rows * TILE_ROWS), :]
        return pltpu.make_async_copy(src, dst, pad_sem)

    def for_each_pad(fn):
        def per_expert(e, carry):
            row, left = pad_start_ref[e], pad_len_ref[e]
            size = EXPERT_ROWS // 2
            while size >= 1:
                take = left & size

                @pl.when(take != 0)
                def _(row=row, size=size):
                    fn(zero_copy(row, size))

                row = row + take
                size //= 2
            return carry
        lax.fori_loop(0, N_EXPERTS, per_expert, 0)

        def per_block(b, carry):
            fn(zero_copy(b * EXPERT_ROWS, EXPERT_ROWS))
            return carry
        lax.fori_loop(nu_ref[0], n_blocks, per_block, 0)

    @pl.when(i == 0)
    def _():
        for s in range(min(2, n_steps)):
            load(s, s).start()
        zero_blk[...] = jnp.zeros_like(zero_blk)
        for_each_pad(lambda cp: cp.start())

    @pl.when(i >= 2)
    def _():
        wait_copies((i + 2) % _DISPATCH_RING)

    @pl.when(i + 2 < n_steps)
    def _():
        load(i + 2, (i + 2) % _DISPATCH_RING).start()

    load(i, slot).wait()
    src_blk = ring.at[slot]
    for k in range(TOP_K):
        def issue(o, carry, k=k):
            tok0 = pl.multiple_of(o * _ISSUE_UNROLL, _ISSUE_UNROLL)
            dsts = [dst_ref[k * tokens + tok0 + r] for r in range(_ISSUE_UNROLL)]
            for r in range(_ISSUE_UNROLL):
                pltpu.make_async_copy(_token_tile(src_blk, tok0 + r), _token_tile(xs_hbm, dsts[r]),
                                      out_sem.at[slot]).start(priority=r % 2)
            return carry

        lax.fori_loop(0, tokens // _ISSUE_UNROLL, issue, 0)

    @pl.when(i == last)
    def _():
        if n_steps > 1:
            wait_copies((i + _DISPATCH_RING - 1) % _DISPATCH_RING)
        wait_copies(slot)
        for_each_pad(lambda cp: cp.wait())


def _dispatch(dest, pad_start, pad_len, n_used, hf_tiles, n_rows, tokens_per_step):
    n = dest.shape[0] // TOP_K
    pairs = tokens_per_step * TOP_K
    assert n % tokens_per_step == 0 and tokens_per_step % _ISSUE_UNROLL == 0
    smem = lambda: pl.BlockSpec(memory_space=pltpu.SMEM)
    n_steps = n // tokens_per_step
    return pl.pallas_call(
        functools.partial(_dispatch_kernel, n_steps=n_steps),
        grid=(n_steps,),
        in_specs=[pl.BlockSpec((pairs,), lambda i: (i,), memory_space=pltpu.SMEM), smem(), smem(), smem(),
                  pl.BlockSpec(memory_space=pl.ANY)],
        out_specs=pl.BlockSpec(memory_space=pl.ANY),
        out_shape=jax.ShapeDtypeStruct((n_rows * TILE_ROWS, LANES), F32),
        scratch_shapes=[pltpu.VMEM((EXPERT_ROWS * TILE_ROWS, LANES), F32),
                        pltpu.VMEM((_DISPATCH_RING, tokens_per_step * TILE_ROWS, LANES), F32),
                        pltpu.SemaphoreType.DMA((_DISPATCH_RING,)), pltpu.SemaphoreType.DMA((_DISPATCH_RING,)),
                        pltpu.SemaphoreType.DMA],
        compiler_params=pltpu.CompilerParams(
            dimension_semantics=("arbitrary",), vmem_limit_bytes=VMEM_LIMIT),
        name="dispatch",
    )(dest, pad_start, pad_len, n_used, hf_tiles)


def _expert_kernel(be_ref, nu_ref, nxt_ref, val_ref, xs_ref, wu_hbm, bu_ref, wd_hbm, bd_ref, y_ref,
                   wu_f32, wd_f32, wu_bf, wd_bf, wsem):
    i = pl.program_id(0)

    def mlp(xb):
        up = _dot(xb, wu_bf[...]) + bu_ref[0]
        x_glu = jnp.minimum(up[:, :D_FF], SWIGLU_LIMIT)
        x_lin = jnp.clip(up[:, D_FF:], -SWIGLU_LIMIT, SWIGLU_LIMIT)
        act = x_glu * jax.nn.sigmoid(SWIGLU_ALPHA * x_glu) * (x_lin + 1.0)
        return _dot(act.astype(BF16), wd_bf[...]) + bd_ref[0]

    def weight_copies(expert):
        return (pltpu.make_async_copy(wu_hbm.at[expert], wu_f32, wsem.at[0]),
                pltpu.make_async_copy(wd_hbm.at[expert], wd_f32, wsem.at[1]))

    @pl.when(i == 0)
    def _():
        for cp in weight_copies(be_ref[0]):
            cp.start()

    first = (i == 0) | (be_ref[i] != be_ref[jnp.maximum(i - 1, 0)])
    used = i < nu_ref[0]
    full = used & (val_ref[i] > EXPERT_ROWS - EXPERT_SUB_ROWS)

    def take_weights():
        for cp in weight_copies(be_ref[i]):
            cp.wait()
        wu_bf[...] = wu_f32[...].astype(BF16)
        wd_bf[...] = wd_f32[...].astype(BF16)

    def fetch_next_weights():
        @pl.when(nxt_ref[i] >= 0)
        def _():
            for cp in weight_copies(nxt_ref[i]):
                cp.start()

    def whole_block():
        _store_token_tiles(y_ref, mlp(_load_token_tiles(xs_ref, 0, EXPERT_ROWS).astype(BF16)))

    @pl.when(first & full)
    def _():
        take_weights()
        whole_block()
        fetch_next_weights()

    @pl.when(first & jnp.logical_not(full))
    def _():
        take_weights()
        fetch_next_weights()

    @pl.when(full & jnp.logical_not(first))
    def _():
        whole_block()

    @pl.when(used & jnp.logical_not(full))
    def _():
        sub = EXPERT_SUB_ROWS
        n_sub = lax.shift_right_logical(val_ref[i] + (sub - 1), sub.bit_length() - 1)

        def compute(j, carry):
            row0 = pl.multiple_of(j * sub, sub)
            _store_token_tiles(y_ref, mlp(_load_token_tiles(xs_ref, row0, sub).astype(BF16)), row0)
            return carry

        def clear(j, carry):
            row0 = pl.multiple_of(j * sub * TILE_ROWS, sub * TILE_ROWS)
            y_ref[pl.ds(row0, sub * TILE_ROWS), :] = jnp.zeros((sub * TILE_ROWS, LANES), F32)
            return carry

        lax.fori_loop(0, n_sub, compute, 0)
        lax.fori_loop(n_sub, EXPERT_ROWS // sub, clear, 0)

    @pl.when(jnp.logical_not(used))
    def _():
        y_ref[...] = jnp.zeros_like(y_ref)


def _experts(block_expert, n_used, next_expert, valid_rows, xs_tiles, w_up, b_up, w_down, b_down):
    n_blocks = block_expert.shape[0]
    blk = (EXPERT_ROWS * TILE_ROWS, LANES)
    grid_spec = pltpu.PrefetchScalarGridSpec(
        num_scalar_prefetch=4,
        grid=(n_blocks,),
        in_specs=[
            pl.BlockSpec(blk, lambda i, be, nu, *_: (jnp.minimum(i, nu[0] - 1), 0)),
            pl.BlockSpec(memory_space=pl.ANY),
            pl.BlockSpec((1, 1, 2 * D_FF), lambda i, be, *_: (be[i], 0, 0)),
            pl.BlockSpec(memory_space=pl.ANY),
            pl.BlockSpec((1, 1, D_MODEL), lambda i, be, *_: (be[i], 0, 0)),
        ],
        out_specs=pl.BlockSpec(blk, lambda i, *_: (i, 0)),
        scratch_shapes=[
            pltpu.VMEM((D_MODEL, 2 * D_FF), F32),
            pltpu.VMEM((D_FF, D_MODEL), F32),
            pltpu.VMEM((D_MODEL, 2 * D_FF), BF16),
            pltpu.VMEM((D_FF, D_MODEL), BF16),
            pltpu.SemaphoreType.DMA((2,)),
        ],
    )
    return pl.pallas_call(
        _expert_kernel,
        grid_spec=grid_spec,
        out_shape=jax.ShapeDtypeStruct((n_blocks * blk[0], LANES), F32),
        compiler_params=pltpu.CompilerParams(
            dimension_semantics=("arbitrary",), vmem_limit_bytes=VMEM_LIMIT),
        name="experts",
    )(block_expert, n_used, next_expert, valid_rows, xs_tiles,
      w_up, b_up[:, None, :], w_down, b_down[:, None, :])


_COMBINE_RING = 3


def _combine_kernel(dst0_ref, dst1_ref, dst2_ref, y_hbm, h1_ref, route_ref, g_ref, o_ref, *scratch):
    bufs, sem = scratch[:_COMBINE_RING], scratch[_COMBINE_RING]
    i = pl.program_id(0)
    last = pl.num_programs(0) - 1
    tm = h1_ref.shape[0]
    rows = TOP_K * tm

    def gather_group(idx_ref, s, row0):
        srcs = [idx_ref[row0 + r] for r in range(_ISSUE_UNROLL)]
        for r in range(_ISSUE_UNROLL):
            pltpu.make_async_copy(_token_tile(y_hbm, srcs[r]), _token_tile(bufs[s], row0 + r),
                                  sem.at[s]).start(priority=r % 2)

    @pl.when(i == 0)
    def _():
        for s, idx_ref in ((0, dst0_ref), (1, dst1_ref)):
            def body(o, carry, s=s, idx_ref=idx_ref):
                gather_group(idx_ref, s, pl.multiple_of(o * _ISSUE_UNROLL, _ISSUE_UNROLL))
                return carry
            lax.fori_loop(0, rows // _ISSUE_UNROLL, body, 0)

    def step(s):
        cur = bufs[s]
        ahead = (s + 2) % _COMBINE_RING
        _tiles_wait_copy(y_hbm, cur, rows, sem.at[s]).wait()
        for g in range(rows // _ISSUE_UNROLL):
            gather_group(dst2_ref, ahead, g * _ISSUE_UNROLL)
        assert tm == LANES
        rt = jnp.concatenate([route_ref[...], jnp.zeros((LANES - _ROUTE_ROWS, tm), F32)], axis=0).T
        acc = h1_ref[...]
        for k in range(TOP_K):
            gate = rt[:, _ROUTE_GATE + k:_ROUTE_GATE + k + 1]
            acc = acc + gate * _load_token_tiles(cur, k * tm, tm)
        ms = jnp.mean(acc * acc, axis=-1, keepdims=True)
        o_ref[...] = acc * lax.rsqrt(ms + RMS_EPS) * g_ref[...]

        @pl.when(i == last)
        def _():
            for t in ((s + 1) % _COMBINE_RING, ahead):
                _tiles_wait_copy(y_hbm, bufs[t], rows, sem.at[t]).wait()

    for s in range(_COMBINE_RING):
        pl.when(i % _COMBINE_RING == s)(functools.partial(step, s))


def _combine(dest_kmajor, y, h1, route, g, tm):
    n = h1.shape[0]
    n_tiles = n // tm
    idx_spec = lambda ahead: pl.BlockSpec(
        (TOP_K * tm,), lambda i: (jnp.minimum(i + ahead, n_tiles - 1),), memory_space=pltpu.SMEM)
    ring_buf = pltpu.VMEM((TOP_K * tm * TILE_ROWS, LANES), F32)
    return pl.pallas_call(
        _combine_kernel,
        grid=(n_tiles,),
        in_specs=[
            idx_spec(0), idx_spec(1), idx_spec(2),
            pl.BlockSpec(memory_space=pl.ANY),
            pl.BlockSpec((tm, D_MODEL), lambda i: (i, 0)),
            pl.BlockSpec((_ROUTE_ROWS, tm), lambda i: (0, i)),
            pl.BlockSpec((1, D_MODEL), lambda i: (0, 0)),
        ],
        out_specs=pl.BlockSpec((tm, D_MODEL), lambda i: (i, 0)),
        out_shape=jax.ShapeDtypeStruct((n, D_MODEL), F32),
        scratch_shapes=[ring_buf] * _COMBINE_RING + [pltpu.SemaphoreType.DMA((_COMBINE_RING,))],
        compiler_params=pltpu.CompilerParams(
            dimension_semantics=("arbitrary",), vmem_limit_bytes=VMEM_LIMIT),
        name="combine",
    )(dest_kmajor, dest_kmajor, dest_kmajor, y, h1, route, g)


def _routing_tables(route, counts, n, tokens_dispatch, tokens_combine):
    tm = EXPERT_ROWS
    i32 = jnp.int32
    n_blocks = (n * TOP_K + N_EXPERTS * (tm - 1)) // tm
    idx = route[_ROUTE_IDX:_ROUTE_IDX + TOP_K].astype(i32)
    rank = route[_ROUTE_RANK:_ROUTE_RANK + TOP_K].astype(i32)
    cnt = counts[:, 0].astype(i32)
    eid = jnp.arange(N_EXPERTS, dtype=i32)
    upto = eid[None, :] <= eid[:, None]
    blocks_e = (cnt + tm - 1) // tm
    blocks_end = jnp.sum(jnp.where(upto, blocks_e[None, :], 0), axis=1)
    row_start = (blocks_end - blocks_e) * tm
    n_used = blocks_end[N_EXPERTS - 1]
    used = blocks_e > 0

    def lookup(table, keys):
        hit = keys[None] == eid.reshape((N_EXPERTS,) + (1,) * keys.ndim)
        return jnp.sum(jnp.where(hit, table.reshape((N_EXPERTS,) + (1,) * keys.ndim), 0), axis=0)

    dest = lookup(row_start, idx) + rank
    blk = jnp.arange(n_blocks, dtype=i32)
    last_used = jnp.max(jnp.where(used, eid, 0))
    be = jnp.where(blk < n_used, jnp.sum((blocks_end[None, :] <= blk[:, None]).astype(i32), axis=1), last_used)
    later_used = used[None, :] & (eid[None, :] > eid[:, None])
    after = jnp.min(jnp.where(later_used, eid[None, :], N_EXPERTS), axis=1)
    next_e = jnp.where(after < N_EXPERTS, after, -1)

    def k_major(tokens):
        return dest.reshape(TOP_K, n // tokens, tokens).transpose(1, 0, 2).reshape(-1)

    valid = jnp.clip(lookup(row_start + cnt, be) - blk * tm, 0, tm)
    return (be.astype(i32), n_used.reshape(1), lookup(next_e, be).astype(i32), valid.astype(i32),
            k_major(tokens_dispatch), k_major(tokens_combine), row_start + cnt, blocks_e * tm - cnt,
            n_blocks * tm)


def kernel(x, meta_tokens, mix_norm_g, w_in, ssm_a_re, ssm_a_im, ssm_log_dt, ssm_b_re, ssm_b_im,
           ssm_c_re, ssm_c_im, ssm_d, w_ssm_glu, attn_sinks, w_attn_o, w_out, ffn_norm_g,
           router_w, router_b, w_up, b_up, w_down, b_down, final_norm_g):
    bsz, seq, d = x.shape
    assert d == D_MODEL and seq % max(WINDOW, SSM_CHUNK) == 0
    assert mix_norm_g.shape[0] == 1, "single-layer trunk"
    n = bsz * seq
    tm_proj = min(1024, n)
    tm_mix = min(1024, n)
    tm_comb = min(128, n)
    x2 = x.reshape(n, D_MODEL)

    w_in_bf = w_in[0].astype(BF16)
    g_mix = mix_norm_g[0][None, :]
    u, q, kv, gs, ga = _in_proj(x2, g_mix, w_in_bf, tm_proj, pitched=True)
    u_m, _, kv_m, _, _ = _in_proj(meta_tokens, g_mix, w_in_bf, N_META, pitched=False)

    ssm_par = _ssm_params(ssm_a_re[0], ssm_a_im[0], ssm_log_dt[0], ssm_b_re[0], ssm_b_im[0],
                          ssm_c_re[0], ssm_c_im[0], ssm_d[0], SSM_CHUNK)
    y_ssm = _ssm(u, u_m, *ssm_par, batch=bsz, chunk=SSM_CHUNK)

    attn = _attention(attn_sinks[0], q, kv, kv_m, seq // WINDOW)

    h1, hf, route, counts = _mix(
        x2, y_ssm, attn, gs, ga, w_ssm_glu[0].astype(BF16), w_attn_o[0].astype(BF16),
        w_out[0].astype(BF16), ffn_norm_g[0][None, :], router_w[0], router_b[0][None, :], tm_mix)

    tok_disp = min(1024, n)
    be, n_used, next_e, valid, dest_disp, dest_comb, pad_start, pad_len, n_rows = _routing_tables(
        route, counts, n, tok_disp, tm_comb)
    xs = _dispatch(dest_disp, pad_start, pad_len, n_used, hf, n_rows, tok_disp)
    y = _experts(be, n_used, next_e, valid, xs, w_up[0], b_up[0], w_down[0], b_down[0])
    out = _combine(dest_comb, y, h1, route, final_norm_g[None, :], tm_comb)
    return out.reshape(bsz, seq, D_MODEL)
```

```python
import functools
import math

import jax
import jax.numpy as jnp
from jax import lax
from jax.experimental import pallas as pl
from jax.experimental.pallas import tpu as pltpu

F32 = jnp.float32
BF16 = jnp.bfloat16

D_MODEL = 1024
N_META = 16
SSM_WIDTH = 512
SSM_GROUP = 16
SSM_GROUPS = 32
SSM_STATE = 64
HEAD_DIM = 64
N_HEADS = 16
N_KV_HEADS = 2
KV_REP = N_HEADS // N_KV_HEADS
WINDOW = 128
Q_WIDTH = N_HEADS * HEAD_DIM
KV_WIDTH = N_KV_HEADS * HEAD_DIM
N_EXPERTS = 32
TOP_K = 4
D_FF = 1024
SWIGLU_ALPHA = 1.702
SWIGLU_LIMIT = 7.0
RMS_EPS = 1e-5
NEG_INF = -1e30

_U0, _Q0, _KV0, _GS0, _GA0, _IN_END = 0, 512, 1536, 1792, 2816, 3840

SSM_CH_BLOCK = 128
SSM_HALF = (SSM_CH_BLOCK // SSM_GROUP) * SSM_STATE
SSM_CHUNK = 32
SSM_PITCH = 40
SSM_UNROLL_A = 8
SSM_UNROLL_B = 4
SSM_CHUNK_UNROLL = 4
LOG2_E = math.log2(math.e)
EXPERT_ROWS = 512
EXPERT_SUB_ROWS = 128
VMEM_LIMIT = 56 * 1024 * 1024


def _dot(a, b):
    return jnp.dot(a, b, preferred_element_type=F32)


def _dot_nt(a, b):
    return lax.dot_general(a, b, (((1,), (1,)), ((), ())), preferred_element_type=F32)


LANES = 128
TILE_ROWS = D_MODEL // LANES


def _store_token_tiles(ref, x, start_row=0):
    rows = x.shape[0]
    for j in range(TILE_ROWS):
        ref[pl.ds(start_row * TILE_ROWS + j, rows, stride=TILE_ROWS), :] = x[:, j * LANES:(j + 1) * LANES]


def _load_token_tiles(ref, start_row, rows):
    return jnp.concatenate(
        [ref[pl.ds(start_row * TILE_ROWS + j, rows, stride=TILE_ROWS), :] for j in range(TILE_ROWS)], axis=1)


def _token_tile(ref, row):
    return ref.at[pl.ds(pl.multiple_of(row * TILE_ROWS, TILE_ROWS), TILE_ROWS), :]


def _in_proj_kernel(x_ref, g_ref, w_ref, u_ref, q_ref, kv_ref, gs_ref, ga_ref, *, pitched):
    x = x_ref[...]
    ms = jnp.mean(x * x, axis=-1, keepdims=True)
    hn = (x * lax.rsqrt(ms + RMS_EPS) * g_ref[...]).astype(BF16)
    u = _dot(hn, w_ref[:, _U0:_Q0])
    if pitched:
        for c in range(u.shape[0] // SSM_CHUNK):
            u_ref[c * SSM_PITCH:c * SSM_PITCH + SSM_CHUNK, :] = u[c * SSM_CHUNK:(c + 1) * SSM_CHUNK, :]
            u_ref[c * SSM_PITCH + SSM_CHUNK:(c + 1) * SSM_PITCH, :] = jnp.zeros(
                (SSM_PITCH - SSM_CHUNK, SSM_WIDTH), F32)
    else:
        u_ref[...] = u
    q_ref[...] = (_dot(hn, w_ref[:, _Q0:_KV0]) * (HEAD_DIM ** -0.5 * LOG2_E)).astype(BF16)
    kv_ref[...] = _dot(hn, w_ref[:, _KV0:_GS0]).astype(BF16)
    gs_ref[...] = jax.nn.sigmoid(_dot(hn, w_ref[:, _GS0:_GA0])).astype(BF16)
    ga_ref[...] = jax.nn.sigmoid(_dot(hn, w_ref[:, _GA0:_IN_END])).astype(BF16)


def _in_proj(x2, g, w_bf, tm, pitched):
    n = x2.shape[0]
    row = lambda w: pl.BlockSpec((tm, w), lambda i: (i, 0))
    full = lambda a: pl.BlockSpec(a.shape, lambda i: (0,) * a.ndim)
    u_rows = (lambda r: r // SSM_CHUNK * SSM_PITCH) if pitched else (lambda r: r)
    return pl.pallas_call(
        functools.partial(_in_proj_kernel, pitched=pitched),
        grid=(n // tm,),
        in_specs=[row(D_MODEL), full(g), full(w_bf)],
        out_specs=[pl.BlockSpec((u_rows(tm), SSM_WIDTH), lambda i: (i, 0)),
                   row(Q_WIDTH), row(2 * KV_WIDTH), row(D_MODEL), row(D_MODEL)],
        out_shape=[
            jax.ShapeDtypeStruct((u_rows(n), SSM_WIDTH), F32),
            jax.ShapeDtypeStruct((n, Q_WIDTH), BF16),
            jax.ShapeDtypeStruct((n, 2 * KV_WIDTH), BF16),
            jax.ShapeDtypeStruct((n, D_MODEL), BF16),
            jax.ShapeDtypeStruct((n, D_MODEL), BF16),
        ],
        compiler_params=pltpu.CompilerParams(
            dimension_semantics=("arbitrary",), vmem_limit_bytes=VMEM_LIMIT),
        name="in_proj",
    )(x2, g, w_bf)


def _ssm_kernel(u_ref, um_ref, bm_ref, cm_ref, ar_ref, ai_ref, atr_ref, ati_ref, d_ref,
                y_ref, sre, sim, *, chunk, rows, batch):
    h = SSM_HALF
    bm = bm_ref[0]
    cm = cm_ref[0]
    ar, ai = ar_ref[0], ai_ref[0]
    atr, ati = atr_ref[0], ati_ref[0]
    dsk = d_ref[0]
    n_chunks = rows // batch

    def advance(sr, si, bu):
        return ar * sr - ai * si + bu[:, :h], ar * si + ai * sr + bu[:, h:]

    bum = _dot(um_ref[...].astype(BF16), bm)
    mr = jnp.zeros((1, h), F32)
    mi = jnp.zeros((1, h), F32)
    for j in range(N_META):
        mr, mi = advance(mr, mi, bum[j:j + 1, :])

    def u_step(t):
        return u_ref[pl.ds(t, rows, stride=SSM_PITCH), :]

    for pad in range(chunk, SSM_PITCH):
        y_ref[pl.ds(pad, rows, stride=SSM_PITCH), :] = jnp.zeros((rows, SSM_CH_BLOCK), F32)

    sre[...] = jnp.zeros_like(sre)
    sim[...] = jnp.zeros_like(sim)

    def pass_a(o, carry):
        for k in range(SSM_UNROLL_A):
            t = o * SSM_UNROLL_A + k
            bu = _dot(u_step(t).astype(BF16), bm)
            nr, ni = advance(sre[...], sim[...], bu)
            sre[...] = nr
            sim[...] = ni
        return carry

    lax.fori_loop(0, chunk // SSM_UNROLL_A, pass_a, 0)

    def over_chunks(o, carry):
        rows_of = [[pl.ds(b * n_chunks + o * SSM_CHUNK_UNROLL + k, 1) for b in range(batch)]
                   for k in range(SSM_CHUNK_UNROLL)]
        ends = [[(sre[row, :], sim[row, :]) for row in rows_k] for rows_k in rows_of]
        carry = list(carry)
        for k in range(SSM_CHUNK_UNROLL):
            for b in range(batch):
                cr, ci = carry[2 * b], carry[2 * b + 1]
                er, ei = ends[k][b]
                sre[rows_of[k][b], :] = cr
                sim[rows_of[k][b], :] = ci
                carry[2 * b], carry[2 * b + 1] = atr * cr - ati * ci + er, atr * ci + ati * cr + ei
        return tuple(carry)

    assert n_chunks % SSM_CHUNK_UNROLL == 0
    lax.fori_loop(0, n_chunks // SSM_CHUNK_UNROLL, over_chunks, (mr, mi) * batch)

    def pass_b(o, carry):
        for k in range(SSM_UNROLL_B):
            t = o * SSM_UNROLL_B + k
            ut = u_step(t)
            bu = _dot(ut.astype(BF16), bm)
            nr, ni = advance(sre[...], sim[...], bu)
            sre[...] = nr
            sim[...] = ni
            y = _dot(nr.astype(BF16), cm[:h, :]) + _dot(ni.astype(BF16), cm[h:, :]) + dsk * ut
            y_ref[pl.ds(t, rows, stride=SSM_PITCH), :] = y
        return carry

    lax.fori_loop(0, chunk // SSM_UNROLL_B, pass_b, 0)


def _ssm(u, u_meta, bmat, cmat, a_re, a_im, at_re, at_im, dskip, batch, chunk):
    n = u.shape[0]
    rows = n // SSM_PITCH
    nblk = SSM_WIDTH // SSM_CH_BLOCK
    col = lambda r: pl.BlockSpec((r, SSM_CH_BLOCK), lambda j: (0, j))
    par = lambda a: pl.BlockSpec((1,) + a.shape[1:], lambda j: (j, 0, 0))
    return pl.pallas_call(
        functools.partial(_ssm_kernel, chunk=chunk, rows=rows, batch=batch),
        grid=(nblk,),
        in_specs=[col(n), col(N_META), par(bmat), par(cmat), par(a_re), par(a_im),
                  par(at_re), par(at_im), par(dskip)],
        out_specs=col(n),
        out_shape=jax.ShapeDtypeStruct((n, SSM_WIDTH), F32),
        scratch_shapes=[pltpu.VMEM((rows, SSM_HALF), F32), pltpu.VMEM((rows, SSM_HALF), F32)],
        compiler_params=pltpu.CompilerParams(
            dimension_semantics=("arbitrary",), vmem_limit_bytes=VMEM_LIMIT),
        name="ssm",
    )(u, u_meta, bmat, cmat, a_re, a_im, at_re, at_im, dskip)


def _ssm_params(a_re, a_im, log_dt, b_re, b_im, c_re, c_im, d_skip, chunk):
    dt = jnp.exp(log_dt)[:, None]
    mag = jnp.exp(a_re * dt)
    ang = a_im * dt
    abar_re, abar_im = mag * jnp.cos(ang), mag * jnp.sin(ang)
    den = a_re * a_re + a_im * a_im
    nr, ni = abar_re - 1.0, abar_im
    coef_re = ((nr * a_re + ni * a_im) / den)[..., None]
    coef_im = ((ni * a_re - nr * a_im) / den)[..., None]
    bbar_re = coef_re * b_re - coef_im * b_im
    bbar_im = coef_re * b_im + coef_im * b_re
    magt = jnp.exp(a_re * dt * chunk)
    at_re, at_im = magt * jnp.cos(ang * chunk), magt * jnp.sin(ang * chunk)

    nblk = SSM_WIDTH // SSM_CH_BLOCK
    gpb = SSM_GROUPS // nblk
    eye = jnp.eye(gpb, dtype=F32)

    def in_map(b):
        b = b.reshape(nblk, gpb, SSM_STATE, SSM_GROUP)
        return jnp.einsum('jgpc,gh->jgchp', b, eye).reshape(nblk, SSM_CH_BLOCK, gpb * SSM_STATE)

    def out_map(c):
        c = c.reshape(nblk, gpb, SSM_GROUP, SSM_STATE)
        return jnp.einsum('jgcp,gh->jgphc', c, eye).reshape(nblk, gpb * SSM_STATE, SSM_CH_BLOCK)

    bmat = jnp.concatenate([in_map(bbar_re), in_map(bbar_im)], axis=2).astype(BF16)
    cmat = jnp.concatenate([out_map(c_re), -out_map(c_im)], axis=1).astype(BF16)
    vec = lambda v: v.reshape(nblk, 1, SSM_HALF)
    return (bmat, cmat, vec(abar_re), vec(abar_im), vec(at_re), vec(at_im),
            d_skip.reshape(nblk, 1, SSM_CH_BLOCK))


ATTN_STEP_BLOCKS = 8


def _attn_kernel(sink_ref, q_ref, kvc_ref, kvp_ref, kvm_ref, o_ref, *, blocks_per_seq, step_blocks):
    for sb in range(step_blocks):
        rows = pl.ds(sb * WINDOW, WINDOW)
        prev = kvp_ref if sb == 0 else kvc_ref.at[pl.ds((sb - 1) * WINDOW, WINDOW), :]
        _attn_block(sink_ref, q_ref.at[rows, :], kvc_ref.at[rows, :], prev, kvm_ref, o_ref.at[rows, :],
                    (pl.program_id(0) * step_blocks + sb) % blocks_per_seq)


def _attn_block(sink_ref, q_ref, kvc_ref, kvp_ref, kvm_ref, o_ref, n):
    w = WINDOW
    hd = HEAD_DIM
    qi = lax.broadcasted_iota(jnp.int32, (w, w), 0)
    lane = lax.broadcasted_iota(jnp.int32, (w, w), 1)
    vis_prev = (lane > qi) & (n > 0)
    vis_cur = lane <= qi
    left = lane < hd
    meta_l = lane < N_META
    meta_r = (lane >= N_META) & (lane < 2 * N_META)

    def placed(x_bf):
        x = x_bf.astype(F32)
        xr = pltpu.roll(x, hd, 1)
        lm = lax.broadcasted_iota(jnp.int32, x.shape, 1) < hd
        z = jnp.zeros_like(x)
        return {(0, 0): jnp.where(lm, x, z), (0, 1): jnp.where(lm, z, xr),
                (1, 0): jnp.where(lm, xr, z), (1, 1): jnp.where(lm, z, x)}

    kp, kc, km = placed(kvp_ref[:, :KV_WIDTH]), placed(kvc_ref[:, :KV_WIDTH]), placed(kvm_ref[:, :KV_WIDTH])
    vp, vc, vm = placed(kvp_ref[:, KV_WIDTH:]), placed(kvc_ref[:, KV_WIDTH:]), placed(kvm_ref[:, KV_WIDTH:])
    pad_rows = w - 2 * N_META
    zpad = jnp.zeros((pad_rows, w), F32)
    krow = lax.broadcasted_iota(jnp.int32, (5 * w, w), 0)
    klane = lax.broadcasted_iota(jnp.int32, (5 * w, w), 1)
    row_l = (krow < 2 * w) | ((krow >= 4 * w) & (krow < 4 * w + N_META))
    row_r = ((krow >= 2 * w) & (krow < 4 * w)) | ((krow >= 4 * w + N_META) & (krow < 4 * w + 2 * N_META))
    den_cols = jnp.where((row_l & (klane < hd)) | (row_r & (klane >= hd)), 1.0, 0.0)

    for j in range(N_KV_HEADS):
        kcat = jnp.concatenate(
            [kp[j, 0], kc[j, 0], kp[j, 1], kc[j, 1], km[j, 0], km[j, 1], zpad], axis=0).astype(BF16)
        vcat = jnp.concatenate(
            [jnp.concatenate([vp[j, 0], vc[j, 0], vp[j, 1], vc[j, 1], vm[j, 0], vm[j, 1], zpad], axis=0),
             den_cols], axis=1).astype(BF16)
        for r in range(KV_REP // 2):
            pr = j * (KV_REP // 2) + r
            s = _dot_nt(q_ref[:, pr * w:(pr + 1) * w], kcat)
            s_l = jnp.where(vis_cur, s[:, w:2 * w], jnp.where(vis_prev, s[:, 0:w], NEG_INF))
            s_r = jnp.where(vis_cur, s[:, 3 * w:4 * w], jnp.where(vis_prev, s[:, 2 * w:3 * w], NEG_INF))
            s_m = s[:, 4 * w:]
            sink_l, sink_r = sink_ref[2 * pr] * LOG2_E, sink_ref[2 * pr + 1] * LOG2_E
            m_l = jnp.maximum(jnp.max(jnp.maximum(s_l, jnp.where(meta_l, s_m, NEG_INF)),
                                      axis=1, keepdims=True), sink_l)
            m_r = jnp.maximum(jnp.max(jnp.maximum(s_r, jnp.where(meta_r, s_m, NEG_INF)),
                                      axis=1, keepdims=True), sink_r)
            s_m = jnp.where(meta_l, s_m - m_l, jnp.where(meta_r, s_m - m_r, NEG_INF))
            e_l, e_r = jnp.exp2(s_l - m_l), jnp.exp2(s_r - m_r)
            e = jnp.concatenate([jnp.where(vis_cur, 0.0, e_l), jnp.where(vis_cur, e_l, 0.0),
                                 jnp.where(vis_cur, 0.0, e_r), jnp.where(vis_cur, e_r, 0.0),
                                 jnp.exp2(s_m)], axis=1).astype(BF16)
            acc = _dot(e, vcat)
            den = acc[:, w:] + jnp.where(left, jnp.exp2(sink_l - m_l), jnp.exp2(sink_r - m_r))
            o_ref[:, pr * w:(pr + 1) * w] = (acc[:, :w] / den).astype(BF16)


def _attention(sinks, q, kv, kv_meta, blocks_per_seq):
    n = q.shape[0]
    step_blocks = math.gcd(ATTN_STEP_BLOCKS, blocks_per_seq)
    step = step_blocks * WINDOW
    return pl.pallas_call(
        functools.partial(_attn_kernel, blocks_per_seq=blocks_per_seq, step_blocks=step_blocks),
        grid=(n // step,),
        in_specs=[
            pl.BlockSpec(memory_space=pltpu.SMEM),
            pl.BlockSpec((step, Q_WIDTH), lambda g: (g, 0)),
            pl.BlockSpec((step, 2 * KV_WIDTH), lambda g: (g, 0)),
            pl.BlockSpec((WINDOW, 2 * KV_WIDTH), lambda g: (jnp.maximum(g * step_blocks - 1, 0), 0)),
            pl.BlockSpec((N_META, 2 * KV_WIDTH), lambda g: (0, 0)),
        ],
        out_specs=pl.BlockSpec((step, Q_WIDTH), lambda g: (g, 0)),
        out_shape=jax.ShapeDtypeStruct((n, Q_WIDTH), BF16),
        compiler_params=pltpu.CompilerParams(dimension_semantics=("arbitrary",)),
        name="attn",
    )(sinks, q, kv, kv, kv_meta)


_ROUTE_IDX, _ROUTE_RANK, _ROUTE_GATE, _ROUTE_ROWS = 0, TOP_K, 2 * TOP_K, 16


def _mix_kernel(x_ref, y_ref, at_ref, gs_ref, ga_ref, wglu_ref, wo_ref, wout_ref, fg_ref, rw_ref, rb_ref,
                h1_ref, hf_ref, route_ref, cnt_ref, cnt_scr):
    tm = x_ref.shape[0]

    @pl.when(pl.program_id(0) == 0)
    def _():
        cnt_scr[...] = jnp.zeros_like(cnt_scr)

    y_ssm = jnp.concatenate([y_ref[c * SSM_PITCH:c * SSM_PITCH + SSM_CHUNK, :]
                             for c in range(tm // SSM_CHUNK)], axis=0)
    glu = _dot(jax.nn.gelu(y_ssm).astype(BF16), wglu_ref[...])
    branch_ssm = glu[:, :D_MODEL] * jax.nn.sigmoid(glu[:, D_MODEL:])
    branch_attn = _dot(at_ref[...], wo_ref[...])
    merged = gs_ref[...].astype(F32) * branch_ssm + ga_ref[...].astype(F32) * branch_attn
    h1 = x_ref[...] + _dot(merged.astype(BF16), wout_ref[...])
    h1_ref[...] = h1
    ms = jnp.mean(h1 * h1, axis=-1, keepdims=True)
    hf = h1 * lax.rsqrt(ms + RMS_EPS) * fg_ref[...]
    _store_token_tiles(hf_ref, hf)

    hf_hi = hf.astype(BF16)
    hf_lo = (hf - hf_hi.astype(F32)).astype(BF16)
    hi_prod = _dot(hf_hi, rw_ref[...])
    logits = (hi_prod[:, :LANES] + (hi_prod[:, LANES:] + _dot(hf_lo, rw_ref[:, :LANES]))
              + rb_ref[...])
    lt = logits.T[:N_EXPERTS, :]
    erow = lax.broadcasted_iota(jnp.int32, (N_EXPERTS, tm), 0)
    vals, idxs, hots = [], [], []
    rest = lt
    for _ in range(TOP_K):
        m = jnp.max(rest, axis=0, keepdims=True)
        first = jnp.min(jnp.where(rest == m, erow, N_EXPERTS), axis=0, keepdims=True)
        hot = erow == first
        vals.append(m)
        idxs.append(first)
        hots.append(hot)
        rest = jnp.where(hot, -jnp.inf, rest)
    exps = [jnp.exp(v - vals[0]) for v in vals]
    tot = exps[0] + exps[1] + exps[2] + exps[3]

    sel = (hots[0] | hots[1] | hots[2] | hots[3]).astype(F32)
    ti = lax.broadcasted_iota(jnp.int32, (tm, tm), 0)
    tj = lax.broadcasted_iota(jnp.int32, (tm, tm), 1)
    earlier = (ti < tj).astype(BF16)
    rank_e = _dot(sel.astype(BF16), earlier) + cnt_scr[...]
    cnt_scr[...] = cnt_scr[...] + jnp.sum(sel, axis=1, keepdims=True)
    cnt_ref[...] = cnt_scr[...]

    rrow = lax.broadcasted_iota(jnp.int32, (_ROUTE_ROWS, tm), 0)
    route = jnp.zeros((_ROUTE_ROWS, tm), F32)
    for k in range(TOP_K):
        rank_k = jnp.sum(jnp.where(hots[k], rank_e, 0.0), axis=0, keepdims=True)
        route = jnp.where(rrow == _ROUTE_IDX + k, idxs[k].astype(F32), route)
        route = jnp.where(rrow == _ROUTE_RANK + k, rank_k, route)
        route = jnp.where(rrow == _ROUTE_GATE + k, exps[k] / tot, route)
    route_ref[...] = route


def _mix(x2, y, attn, gs, ga, wglu, wo, wout, fg, rw, rb, tm):
    n = x2.shape[0]
    rw = jnp.pad(rw, ((0, 0), (0, LANES - N_EXPERTS)))
    rb = jnp.pad(rb, ((0, 0), (0, LANES - N_EXPERTS)), constant_values=NEG_INF)
    rw_hi = rw.astype(BF16)
    rw_split = jnp.concatenate([rw_hi, (rw - rw_hi.astype(F32)).astype(BF16)], axis=1)
    row = lambda w: pl.BlockSpec((tm, w), lambda i: (i, 0))
    full = lambda a: pl.BlockSpec(a.shape, lambda i: (0,) * a.ndim)
    return pl.pallas_call(
        _mix_kernel,
        grid=(n // tm,),
        in_specs=[row(D_MODEL), pl.BlockSpec((tm // SSM_CHUNK * SSM_PITCH, SSM_WIDTH), lambda i: (i, 0)),
                  row(Q_WIDTH), row(D_MODEL), row(D_MODEL),
                  full(wglu), full(wo), full(wout), full(fg), full(rw_split), full(rb)],
        out_specs=[row(D_MODEL), pl.BlockSpec((tm * TILE_ROWS, LANES), lambda i: (i, 0)),
                   pl.BlockSpec((_ROUTE_ROWS, tm), lambda i: (0, i)),
                   pl.BlockSpec((N_EXPERTS, 1), lambda i: (0, 0))],
        out_shape=[
            jax.ShapeDtypeStruct((n, D_MODEL), F32),
            jax.ShapeDtypeStruct((n * TILE_ROWS, LANES), F32),
            jax.ShapeDtypeStruct((_ROUTE_ROWS, n), F32),
            jax.ShapeDtypeStruct((N_EXPERTS, 1), F32),
        ],
        scratch_shapes=[pltpu.VMEM((N_EXPERTS, 1), F32)],
        compiler_params=pltpu.CompilerParams(
            dimension_semantics=("arbitrary",), vmem_limit_bytes=VMEM_LIMIT),
        name="mix_router",
    )(x2, y, attn, gs, ga, wglu, wo, wout, fg, rw_split, rb)


def _tiles_wait_copy(src_hbm, dst, n_tiles, sem):
    rows = n_tiles * TILE_ROWS
    return pltpu.make_async_copy(src_hbm.at[pl.ds(0, rows), :], dst.at[pl.ds(0, rows), :], sem)


_ISSUE_UNROLL = 16


_DISPATCH_RING = 4


def _dispatch_kernel(dst_ref, pad_start_ref, pad_len_ref, nu_ref, hf_hbm, xs_hbm, zero_blk, ring, in_sem,
                     out_sem, pad_sem, *, n_steps):
    i = pl.program_id(0)
    last = n_steps - 1
    pairs = dst_ref.shape[0]
    tokens = pairs // TOP_K
    blk_rows = tokens * TILE_ROWS
    n_blocks = xs_hbm.shape[0] // (EXPERT_ROWS * TILE_ROWS)
    slot = i % _DISPATCH_RING

    def load(step, s):
        src = hf_hbm.at[pl.ds(pl.multiple_of(step * blk_rows, blk_rows), blk_rows), :]
        return pltpu.make_async_copy(src, ring.at[s], in_sem.at[s])

    def wait_copies(s):
        for _ in range(TOP_K):
            pltpu.make_async_copy(ring.at[s], xs_hbm.at[pl.ds(0, blk_rows), :], out_sem.at[s]).wait()

    def zero_copy(row, rows):
        src = zero_blk.at[pl.ds(0, rows * TILE_ROWS), :]
        dst = xs_hbm.at[pl.ds(pl.multiple_of(row * TILE_ROWS, TILE_ROWS), rows * TILE_ROWS), :]
        return pltpu.make_async_copy(src, dst, pad_sem)

    def for_each_pad(fn):
        def per_expert(e, carry):
            row, left = pad_start_ref[e], pad_len_ref[e]
            size = EXPERT_ROWS // 2
            while size >= 1:
                take = left & size

                @pl.when(take != 0)
                def _(row=row, size=size):
                    fn(zero_copy(row, size))

                row = row + take
                size //= 2
            return carry
        lax.fori_loop(0, N_EXPERTS, per_expert, 0)

        def per_block(b, carry):
            fn(zero_copy(b * EXPERT_ROWS, EXPERT_ROWS))
            return carry
        lax.fori_loop(nu_ref[0], n_blocks, per_block, 0)

    @pl.when(i == 0)
    def _():
        for s in range(min(2, n_steps)):
            load(s, s).start()
        zero_blk[...] = jnp.zeros_like(zero_blk)
        for_each_pad(lambda cp: cp.start())

    @pl.when(i >= 2)
    def _():
        wait_copies((i + 2) % _DISPATCH_RING)

    @pl.when(i + 2 < n_steps)
    def _():
        load(i + 2, (i + 2) % _DISPATCH_RING).start()

    load(i, slot).wait()
    src_blk = ring.at[slot]
    for k in range(TOP_K):
        def issue(o, carry, k=k):
            tok0 = pl.multiple_of(o * _ISSUE_UNROLL, _ISSUE_UNROLL)
            dsts = [dst_ref[k * tokens + tok0 + r] for r in range(_ISSUE_UNROLL)]
            for r in range(_ISSUE_UNROLL):
                pltpu.make_async_copy(_token_tile(src_blk, tok0 + r), _token_tile(xs_hbm, dsts[r]),
                                      out_sem.at[slot]).start(priority=r % 2)
            return carry

        lax.fori_loop(0, tokens // _ISSUE_UNROLL, issue, 0)

    @pl.when(i == last)
    def _():
        if n_steps > 1:
            wait_copies((i + _DISPATCH_RING - 1) % _DISPATCH_RING)
        wait_copies(slot)
        for_each_pad(lambda cp: cp.wait())


def _dispatch(dest, pad_start, pad_len, n_used, hf_tiles, n_rows, tokens_per_step):
    n = dest.shape[0] // TOP_K
    pairs = tokens_per_step * TOP_K
    assert n % tokens_per_step == 0 and tokens_per_step % _ISSUE_UNROLL == 0
    smem = lambda: pl.BlockSpec(memory_space=pltpu.SMEM)
    n_steps = n // tokens_per_step
    return pl.pallas_call(
        functools.partial(_dispatch_kernel, n_steps=n_steps),
        grid=(n_steps,),
        in_specs=[pl.BlockSpec((pairs,), lambda i: (i,), memory_space=pltpu.SMEM), smem(), smem(), smem(),
                  pl.BlockSpec(memory_space=pl.ANY)],
        out_specs=pl.BlockSpec(memory_space=pl.ANY),
        out_shape=jax.ShapeDtypeStruct((n_rows * TILE_ROWS, LANES), F32),
        scratch_shapes=[pltpu.VMEM((EXPERT_ROWS * TILE_ROWS, LANES), F32),
                        pltpu.VMEM((_DISPATCH_RING, tokens_per_step * TILE_ROWS, LANES), F32),
                        pltpu.SemaphoreType.DMA((_DISPATCH_RING,)), pltpu.SemaphoreType.DMA((_DISPATCH_RING,)),
                        pltpu.SemaphoreType.DMA],
        compiler_params=pltpu.CompilerParams(
            dimension_semantics=("arbitrary",), vmem_limit_bytes=VMEM_LIMIT),
        name="dispatch",
    )(dest, pad_start, pad_len, n_used, hf_tiles)


def _expert_kernel(be_ref, nu_ref, nxt_ref, val_ref, xs_ref, wu_hbm, bu_ref, wd_hbm, bd_ref, y_ref,
                   wu_f32, wd_f32, wu_bf, wd_bf, wsem):
    i = pl.program_id(0)

    def mlp(xb):
        up = _dot(xb, wu_bf[...]) + bu_ref[0]
        x_glu = jnp.minimum(up[:, :D_FF], SWIGLU_LIMIT)
        x_lin = jnp.clip(up[:, D_FF:], -SWIGLU_LIMIT, SWIGLU_LIMIT)
        act = x_glu * jax.nn.sigmoid(SWIGLU_ALPHA * x_glu) * (x_lin + 1.0)
        return _dot(act.astype(BF16), wd_bf[...]) + bd_ref[0]

    def weight_copies(expert):
        return (pltpu.make_async_copy(wu_hbm.at[expert], wu_f32, wsem.at[0]),
                pltpu.make_async_copy(wd_hbm.at[expert], wd_f32, wsem.at[1]))

    @pl.when(i == 0)
    def _():
        for cp in weight_copies(be_ref[0]):
            cp.start()

    first = (i == 0) | (be_ref[i] != be_ref[jnp.maximum(i - 1, 0)])
    used = i < nu_ref[0]
    full = used & (val_ref[i] > EXPERT_ROWS - EXPERT_SUB_ROWS)

    def take_weights():
        for cp in weight_copies(be_ref[i]):
            cp.wait()
        wu_bf[...] = wu_f32[...].astype(BF16)
        wd_bf[...] = wd_f32[...].astype(BF16)

    def fetch_next_weights():
        @pl.when(nxt_ref[i] >= 0)
        def _():
            for cp in weight_copies(nxt_ref[i]):
                cp.start()

    def whole_block():
        _store_token_tiles(y_ref, mlp(_load_token_tiles(xs_ref, 0, EXPERT_ROWS).astype(BF16)))

    @pl.when(first & full)
    def _():
        take_weights()
        whole_block()
        fetch_next_weights()

    @pl.when(first & jnp.logical_not(full))
    def _():
        take_weights()
        fetch_next_weights()

    @pl.when(full & jnp.logical_not(first))
    def _():
        whole_block()

    @pl.when(used & jnp.logical_not(full))
    def _():
        sub = EXPERT_SUB_ROWS
        n_sub = lax.shift_right_logical(val_ref[i] + (sub - 1), sub.bit_length() - 1)

        def compute(j, carry):
            row0 = pl.multiple_of(j * sub, sub)
            _store_token_tiles(y_ref, mlp(_load_token_tiles(xs_ref, row0, sub).astype(BF16)), row0)
            return carry

        def clear(j, carry):
            row0 = pl.multiple_of(j * sub * TILE_ROWS, sub * TILE_ROWS)
            y_ref[pl.ds(row0, sub * TILE_ROWS), :] = jnp.zeros((sub * TILE_ROWS, LANES), F32)
            return carry

        lax.fori_loop(0, n_sub, compute, 0)
        lax.fori_loop(n_sub, EXPERT_ROWS // sub, clear, 0)

    @pl.when(jnp.logical_not(used))
    def _():
        y_ref[...] = jnp.zeros_like(y_ref)


def _experts(block_expert, n_used, next_expert, valid_rows, xs_tiles, w_up, b_up, w_down, b_down):
    n_blocks = block_expert.shape[0]
    blk = (EXPERT_ROWS * TILE_ROWS, LANES)
    grid_spec = pltpu.PrefetchScalarGridSpec(
        num_scalar_prefetch=4,
        grid=(n_blocks,),
        in_specs=[
            pl.BlockSpec(blk, lambda i, be, nu, *_: (jnp.minimum(i, nu[0] - 1), 0)),
            pl.BlockSpec(memory_space=pl.ANY),
            pl.BlockSpec((1, 1, 2 * D_FF), lambda i, be, *_: (be[i], 0, 0)),
            pl.BlockSpec(memory_space=pl.ANY),
            pl.BlockSpec((1, 1, D_MODEL), lambda i, be, *_: (be[i], 0, 0)),
        ],
        out_specs=pl.BlockSpec(blk, lambda i, *_: (i, 0)),
        scratch_shapes=[
            pltpu.VMEM((D_MODEL, 2 * D_FF), F32),
            pltpu.VMEM((D_FF, D_MODEL), F32),
            pltpu.VMEM((D_MODEL, 2 * D_FF), BF16),
            pltpu.VMEM((D_FF, D_MODEL), BF16),
            pltpu.SemaphoreType.DMA((2,)),
        ],
    )
    return pl.pallas_call(
        _expert_kernel,
        grid_spec=grid_spec,
        out_shape=jax.ShapeDtypeStruct((n_blocks * blk[0], LANES), F32),
        compiler_params=pltpu.CompilerParams(
            dimension_semantics=("arbitrary",), vmem_limit_bytes=VMEM_LIMIT),
        name="experts",
    )(block_expert, n_used, next_expert, valid_rows, xs_tiles,
      w_up, b_up[:, None, :], w_down, b_down[:, None, :])


_COMBINE_RING = 3


def _combine_kernel(dst0_ref, dst1_ref, dst2_ref, y_hbm, h1_ref, route_ref, g_ref, o_ref, *scratch):
    bufs, sem = scratch[:_COMBINE_RING], scratch[_COMBINE_RING]
    i = pl.program_id(0)
    last = pl.num_programs(0) - 1
    tm = h1_ref.shape[0]
    rows = TOP_K * tm

    def gather_group(idx_ref, s, row0):
        srcs = [idx_ref[row0 + r] for r in range(_ISSUE_UNROLL)]
        for r in range(_ISSUE_UNROLL):
            pltpu.make_async_copy(_token_tile(y_hbm, srcs[r]), _token_tile(bufs[s], row0 + r),
                                  sem.at[s]).start(priority=r % 2)

    @pl.when(i == 0)
    def _():
        for s, idx_ref in ((0, dst0_ref), (1, dst1_ref)):
            def body(o, carry, s=s, idx_ref=idx_ref):
                gather_group(idx_ref, s, pl.multiple_of(o * _ISSUE_UNROLL, _ISSUE_UNROLL))
                return carry
            lax.fori_loop(0, rows // _ISSUE_UNROLL, body, 0)

    def step(s):
        cur = bufs[s]
        ahead = (s + 2) % _COMBINE_RING
        _tiles_wait_copy(y_hbm, cur, rows, sem.at[s]).wait()
        for g in range(rows // _ISSUE_UNROLL):
            gather_group(dst2_ref, ahead, g * _ISSUE_UNROLL)
        assert tm == LANES
        rt = jnp.concatenate([route_ref[...], jnp.zeros((LANES - _ROUTE_ROWS, tm), F32)], axis=0).T
        acc = h1_ref[...]
        for k in range(TOP_K):
            gate = rt[:, _ROUTE_GATE + k:_ROUTE_GATE + k + 1]
            acc = acc + gate * _load_token_tiles(cur, k * tm, tm)
        ms = jnp.mean(acc * acc, axis=-1, keepdims=True)
        o_ref[...] = acc * lax.rsqrt(ms + RMS_EPS) * g_ref[...]

        @pl.when(i == last)
        def _():
            for t in ((s + 1) % _COMBINE_RING, ahead):
                _tiles_wait_copy(y_hbm, bufs[t], rows, sem.at[t]).wait()

    for s in range(_COMBINE_RING):
        pl.when(i % _COMBINE_RING == s)(functools.partial(step, s))


def _combine(dest_kmajor, y, h1, route, g, tm):
    n = h1.shape[0]
    n_tiles = n // tm
    idx_spec = lambda ahead: pl.BlockSpec(
        (TOP_K * tm,), lambda i: (jnp.minimum(i + ahead, n_tiles - 1),), memory_space=pltpu.SMEM)
    ring_buf = pltpu.VMEM((TOP_K * tm * TILE_ROWS, LANES), F32)
    return pl.pallas_call(
        _combine_kernel,
        grid=(n_tiles,),
        in_specs=[
            idx_spec(0), idx_spec(1), idx_spec(2),
            pl.BlockSpec(memory_space=pl.ANY),
            pl.BlockSpec((tm, D_MODEL), lambda i: (i, 0)),
            pl.BlockSpec((_ROUTE_ROWS, tm), lambda i: (0, i)),
            pl.BlockSpec((1, D_MODEL), lambda i: (0, 0)),
        ],
        out_specs=pl.BlockSpec((tm, D_MODEL), lambda i: (i, 0)),
        out_shape=jax.ShapeDtypeStruct((n, D_MODEL), F32),
        scratch_shapes=[ring_buf] * _COMBINE_RING + [pltpu.SemaphoreType.DMA((_COMBINE_RING,))],
        compiler_params=pltpu.CompilerParams(
            dimension_semantics=("arbitrary",), vmem_limit_bytes=VMEM_LIMIT),
        name="combine",
    )(dest_kmajor, dest_kmajor, dest_kmajor, y, h1, route, g)


def _routing_tables(route, counts, n, tokens_dispatch, tokens_combine):
    tm = EXPERT_ROWS
    i32 = jnp.int32
    n_blocks = (n * TOP_K + N_EXPERTS * (tm - 1)) // tm
    idx = route[_ROUTE_IDX:_ROUTE_IDX + TOP_K].astype(i32)
    rank = route[_ROUTE_RANK:_ROUTE_RANK + TOP_K].astype(i32)
    cnt = counts[:, 0].astype(i32)
    eid = jnp.arange(N_EXPERTS, dtype=i32)
    upto = eid[None, :] <= eid[:, None]
    blocks_e = (cnt + tm - 1) // tm
    blocks_end = jnp.sum(jnp.where(upto, blocks_e[None, :], 0), axis=1)
    row_start = (blocks_end - blocks_e) * tm
    n_used = blocks_end[N_EXPERTS - 1]
    used = blocks_e > 0

    def lookup(table, keys):
        hit = keys[None] == eid.reshape((N_EXPERTS,) + (1,) * keys.ndim)
        return jnp.sum(jnp.where(hit, table.reshape((N_EXPERTS,) + (1,) * keys.ndim), 0), axis=0)

    dest = lookup(row_start, idx) + rank
    blk = jnp.arange(n_blocks, dtype=i32)
    last_used = jnp.max(jnp.where(used, eid, 0))
    be = jnp.where(blk < n_used, jnp.sum((blocks_end[None, :] <= blk[:, None]).astype(i32), axis=1), last_used)
    later_used = used[None, :] & (eid[None, :] > eid[:, None])
    after = jnp.min(jnp.where(later_used, eid[None, :], N_EXPERTS), axis=1)
    next_e = jnp.where(after < N_EXPERTS, after, -1)

    def k_major(tokens):
        return dest.reshape(TOP_K, n // tokens, tokens).transpose(1, 0, 2).reshape(-1)

    valid = jnp.clip(lookup(row_start + cnt, be) - blk * tm, 0, tm)
    return (be.astype(i32), n_used.reshape(1), lookup(next_e, be).astype(i32), valid.astype(i32),
            k_major(tokens_dispatch), k_major(tokens_combine), row_start + cnt, blocks_e * tm - cnt,
            n_blocks * tm)


def kernel(x, meta_tokens, mix_norm_g, w_in, ssm_a_re, ssm_a_im, ssm_log_dt, ssm_b_re, ssm_b_im,
           ssm_c_re, ssm_c_im, ssm_d, w_ssm_glu, attn_sinks, w_attn_o, w_out, ffn_norm_g,
           router_w, router_b, w_up, b_up, w_down, b_down, final_norm_g):
    bsz, seq, d = x.shape
    assert d == D_MODEL and seq % max(WINDOW, SSM_CHUNK) == 0
    assert mix_norm_g.shape[0] == 1, "single-layer trunk"
    n = bsz * seq
    tm_proj = min(1024, n)
    tm_mix = min(1024, n)
    tm_comb = min(128, n)
    x2 = x.reshape(n, D_MODEL)

    w_in_bf = w_in[0].astype(BF16)
    g_mix = mix_norm_g[0][None, :]
    u, q, kv, gs, ga = _in_proj(x2, g_mix, w_in_bf, tm_proj, pitched=True)
    u_m, _, kv_m, _, _ = _in_proj(meta_tokens, g_mix, w_in_bf, N_META, pitched=False)

    ssm_par = _ssm_params(ssm_a_re[0], ssm_a_im[0], ssm_log_dt[0], ssm_b_re[0], ssm_b_im[0],
                          ssm_c_re[0], ssm_c_im[0], ssm_d[0], SSM_CHUNK)
    y_ssm = _ssm(u, u_m, *ssm_par, batch=bsz, chunk=SSM_CHUNK)

    attn = _attention(attn_sinks[0], q, kv, kv_m, seq // WINDOW)

    h1, hf, route, counts = _mix(
        x2, y_ssm, attn, gs, ga, w_ssm_glu[0].astype(BF16), w_attn_o[0].astype(BF16),
        w_out[0].astype(BF16), ffn_norm_g[0][None, :], router_w[0], router_b[0][None, :], tm_mix)

    tok_disp = min(1024, n)
    be, n_used, next_e, valid, dest_disp, dest_comb, pad_start, pad_len, n_rows = _routing_tables(
        route, counts, n, tok_disp, tm_comb)
    xs = _dispatch(dest_disp, pad_start, pad_len, n_used, hf, n_rows, tok_disp)
    y = _experts(be, n_used, next_e, valid, xs, w_up[0], b_up[0], w_down[0], b_down[0])
    out = _combine(dest_comb, y, h1, route, final_norm_g[None, :], tm_comb)
    return out.reshape(bsz, seq, D_MODEL)
```

```python
import functools
import math

import jax
import jax.numpy as jnp
from jax import lax
from jax.experimental import pallas as pl
from jax.experimental.pallas import tpu as pltpu

F32 = jnp.float32
BF16 = jnp.bfloat16

D_MODEL = 1024
N_META = 16
SSM_WIDTH = 512
SSM_GROUP = 16
SSM_GROUPS = 32
SSM_STATE = 64
HEAD_DIM = 64
N_HEADS = 16
N_KV_HEADS = 2
KV_REP = N_HEADS // N_KV_HEADS
WINDOW = 128
Q_WIDTH = N_HEADS * HEAD_DIM
KV_WIDTH = N_KV_HEADS * HEAD_DIM
N_EXPERTS = 32
TOP_K = 4
D_FF = 1024
SWIGLU_ALPHA = 1.702
SWIGLU_LIMIT = 7.0
RMS_EPS = 1e-5
NEG_INF = -1e30

_U0, _Q0, _KV0, _GS0, _GA0, _IN_END = 0, 512, 1536, 1792, 2816, 3840

SSM_CH_BLOCK = 128
SSM_HALF = (SSM_CH_BLOCK // SSM_GROUP) * SSM_STATE
SSM_CHUNK = 32
SSM_PITCH = 40
SSM_UNROLL_A = 8
SSM_UNROLL_B = 4
SSM_CHUNK_UNROLL = 4
LOG2_E = math.log2(math.e)
EXPERT_ROWS = 512
EXPERT_SUB_ROWS = 128
VMEM_LIMIT = 56 * 1024 * 1024


def _dot(a, b):
    return jnp.dot(a, b, preferred_element_type=F32)


def _dot_nt(a, b):
    return lax.dot_general(a, b, (((1,), (1,)), ((), ())), preferred_element_type=F32)


LANES = 128
TILE_ROWS = D_MODEL // LANES


def _store_token_tiles(ref, x, start_row=0):
    rows = x.shape[0]
    for j in range(TILE_ROWS):
        ref[pl.ds(start_row * TILE_ROWS + j, rows, stride=TILE_ROWS), :] = x[:, j * LANES:(j + 1) * LANES]


def _load_token_tiles(ref, start_row, rows):
    return jnp.concatenate(
        [ref[pl.ds(start_row * TILE_ROWS + j, rows, stride=TILE_ROWS), :] for j in range(TILE_ROWS)], axis=1)


def _token_tile(ref, row):
    return ref.at[pl.ds(pl.multiple_of(row * TILE_ROWS, TILE_ROWS), TILE_ROWS), :]


def _in_proj_kernel(x_ref, g_ref, w_ref, u_ref, q_ref, kv_ref, gs_ref, ga_ref, *, pitched):
    x = x_ref[...]
    ms = jnp.mean(x * x, axis=-1, keepdims=True)
    hn = (x * lax.rsqrt(ms + RMS_EPS) * g_ref[...]).astype(BF16)
    u = _dot(hn, w_ref[:, _U0:_Q0])
    if pitched:
        for c in range(u.shape[0] // SSM_CHUNK):
            u_ref[c * SSM_PITCH:c * SSM_PITCH + SSM_CHUNK, :] = u[c * SSM_CHUNK:(c + 1) * SSM_CHUNK, :]
            u_ref[c * SSM_PITCH + SSM_CHUNK:(c + 1) * SSM_PITCH, :] = jnp.zeros(
                (SSM_PITCH - SSM_CHUNK, SSM_WIDTH), F32)
    else:
        u_ref[...] = u
    q_ref[...] = (_dot(hn, w_ref[:, _Q0:_KV0]) * (HEAD_DIM ** -0.5 * LOG2_E)).astype(BF16)
    kv_ref[...] = _dot(hn, w_ref[:, _KV0:_GS0]).astype(BF16)
    gs_ref[...] = jax.nn.sigmoid(_dot(hn, w_ref[:, _GS0:_GA0])).astype(BF16)
    ga_ref[...] = jax.nn.sigmoid(_dot(hn, w_ref[:, _GA0:_IN_END])).astype(BF16)


def _in_proj(x2, g, w_bf, tm, pitched):
    n = x2.shape[0]
    row = lambda w: pl.BlockSpec((tm, w), lambda i: (i, 0))
    full = lambda a: pl.BlockSpec(a.shape, lambda i: (0,) * a.ndim)
    u_rows = (lambda r: r // SSM_CHUNK * SSM_PITCH) if pitched else (lambda r: r)
    return pl.pallas_call(
        functools.partial(_in_proj_kernel, pitched=pitched),
        grid=(n // tm,),
        in_specs=[row(D_MODEL), full(g), full(w_bf)],
        out_specs=[pl.BlockSpec((u_rows(tm), SSM_WIDTH), lambda i: (i, 0)),
                   row(Q_WIDTH), row(2 * KV_WIDTH), row(D_MODEL), row(D_MODEL)],
        out_shape=[
            jax.ShapeDtypeStruct((u_rows(n), SSM_WIDTH), F32),
            jax.ShapeDtypeStruct((n, Q_WIDTH), BF16),
            jax.ShapeDtypeStruct((n, 2 * KV_WIDTH), BF16),
            jax.ShapeDtypeStruct((n, D_MODEL), BF16),
            jax.ShapeDtypeStruct((n, D_MODEL), BF16),
        ],
        compiler_params=pltpu.CompilerParams(
            dimension_semantics=("arbitrary",), vmem_limit_bytes=VMEM_LIMIT),
        name="in_proj",
    )(x2, g, w_bf)


def _ssm_kernel(u_ref, um_ref, bm_ref, cm_ref, ar_ref, ai_ref, atr_ref, ati_ref, d_ref,
                y_ref, sre, sim, *, chunk, rows, batch):
    h = SSM_HALF
    bm = bm_ref[0]
    cm = cm_ref[0]
    ar, ai = ar_ref[0], ai_ref[0]
    atr, ati = atr_ref[0], ati_ref[0]
    dsk = d_ref[0]
    n_chunks = rows // batch

    def advance(sr, si, bu):
        return ar * sr - ai * si + bu[:, :h], ar * si + ai * sr + bu[:, h:]

    bum = _dot(um_ref[...].astype(BF16), bm)
    mr = jnp.zeros((1, h), F32)
    mi = jnp.zeros((1, h), F32)
    for j in range(N_META):
        mr, mi = advance(mr, mi, bum[j:j + 1, :])

    def u_step(t):
        return u_ref[pl.ds(t, rows, stride=SSM_PITCH), :]

    for pad in range(chunk, SSM_PITCH):
        y_ref[pl.ds(pad, rows, stride=SSM_PITCH), :] = jnp.zeros((rows, SSM_CH_BLOCK), F32)

    sre[...] = jnp.zeros_like(sre)
    sim[...] = jnp.zeros_like(sim)

    def pass_a(o, carry):
        for k in range(SSM_UNROLL_A):
            t = o * SSM_UNROLL_A + k
            bu = _dot(u_step(t).astype(BF16), bm)
            nr, ni = advance(sre[...], sim[...], bu)
            sre[...] = nr
            sim[...] = ni
        return carry

    lax.fori_loop(0, chunk // SSM_UNROLL_A, pass_a, 0)

    def over_chunks(o, carry):
        rows_of = [[pl.ds(b * n_chunks + o * SSM_CHUNK_UNROLL + k, 1) for b in range(batch)]
                   for k in range(SSM_CHUNK_UNROLL)]
        ends = [[(sre[row, :], sim[row, :]) for row in rows_k] for rows_k in rows_of]
        carry = list(carry)
        for k in range(SSM_CHUNK_UNROLL):
            for b in range(batch):
                cr, ci = carry[2 * b], carry[2 * b + 1]
                er, ei = ends[k][b]
                sre[rows_of[k][b], :] = cr
                sim[rows_of[k][b], :] = ci
                carry[2 * b], carry[2 * b + 1] = atr * cr - ati * ci + er, atr * ci + ati * cr + ei
        return tuple(carry)

    assert n_chunks % SSM_CHUNK_UNROLL == 0
    lax.fori_loop(0, n_chunks // SSM_CHUNK_UNROLL, over_chunks, (mr, mi) * batch)

    def pass_b(o, carry):
        for k in range(SSM_UNROLL_B):
            t = o * SSM_UNROLL_B + k
            ut = u_step(t)
            bu = _dot(ut.astype(BF16), bm)
            nr, ni = advance(sre[...], sim[...], bu)
            sre[...] = nr
            sim[...] = ni
            y = _dot(nr.astype(BF16), cm[:h, :]) + _dot(ni.astype(BF16), cm[h:, :]) + dsk * ut
            y_ref[pl.ds(t, rows, stride=SSM_PITCH), :] = y
        return carry

    lax.fori_loop(0, chunk // SSM_UNROLL_B, pass_b, 0)


def _ssm(u, u_meta, bmat, cmat, a_re, a_im, at_re, at_im, dskip, batch, chunk):
    n = u.shape[0]
    rows = n // SSM_PITCH
    nblk = SSM_WIDTH // SSM_CH_BLOCK
    col = lambda r: pl.BlockSpec((r, SSM_CH_BLOCK), lambda j: (0, j))
    par = lambda a: pl.BlockSpec((1,) + a.shape[1:], lambda j: (j, 0, 0))
    return pl.pallas_call(
        functools.partial(_ssm_kernel, chunk=chunk, rows=rows, batch=batch),
        grid=(nblk,),
        in_specs=[col(n), col(N_META), par(bmat), par(cmat), par(a_re), par(a_im),
                  par(at_re), par(at_im), par(dskip)],
        out_specs=col(n),
        out_shape=jax.ShapeDtypeStruct((n, SSM_WIDTH), F32),
        scratch_shapes=[pltpu.VMEM((rows, SSM_HALF), F32), pltpu.VMEM((rows, SSM_HALF), F32)],
        compiler_params=pltpu.CompilerParams(
            dimension_semantics=("arbitrary",), vmem_limit_bytes=VMEM_LIMIT),
        name="ssm",
    )(u, u_meta, bmat, cmat, a_re, a_im, at_re, at_im, dskip)


def _ssm_params(a_re, a_im, log_dt, b_re, b_im, c_re, c_im, d_skip, chunk):
    dt = jnp.exp(log_dt)[:, None]
    mag = jnp.exp(a_re * dt)
    ang = a_im * dt
    abar_re, abar_im = mag * jnp.cos(ang), mag * jnp.sin(ang)
    den = a_re * a_re + a_im * a_im
    nr, ni = abar_re - 1.0, abar_im
    coef_re = ((nr * a_re + ni * a_im) / den)[..., None]
    coef_im = ((ni * a_re - nr * a_im) / den)[..., None]
    bbar_re = coef_re * b_re - coef_im * b_im
    bbar_im = coef_re * b_im + coef_im * b_re
    magt = jnp.exp(a_re * dt * chunk)
    at_re, at_im = magt * jnp.cos(ang * chunk), magt * jnp.sin(ang * chunk)

    nblk = SSM_WIDTH // SSM_CH_BLOCK
    gpb = SSM_GROUPS // nblk
    eye = jnp.eye(gpb, dtype=F32)

    def in_map(b):
        b = b.reshape(nblk, gpb, SSM_STATE, SSM_GROUP)
        return jnp.einsum('jgpc,gh->jgchp', b, eye).reshape(nblk, SSM_CH_BLOCK, gpb * SSM_STATE)

    def out_map(c):
        c = c.reshape(nblk, gpb, SSM_GROUP, SSM_STATE)
        return jnp.einsum('jgcp,gh->jgphc', c, eye).reshape(nblk, gpb * SSM_STATE, SSM_CH_BLOCK)

    bmat = jnp.concatenate([in_map(bbar_re), in_map(bbar_im)], axis=2).astype(BF16)
    cmat = jnp.concatenate([out_map(c_re), -out_map(c_im)], axis=1).astype(BF16)
    vec = lambda v: v.reshape(nblk, 1, SSM_HALF)
    return (bmat, cmat, vec(abar_re), vec(abar_im), vec(at_re), vec(at_im),
            d_skip.reshape(nblk, 1, SSM_CH_BLOCK))


ATTN_STEP_BLOCKS = 8


def _attn_kernel(sink_ref, q_ref, kvc_ref, kvp_ref, kvm_ref, o_ref, *, blocks_per_seq, step_blocks):
    for sb in range(step_blocks):
        rows = pl.ds(sb * WINDOW, WINDOW)
        prev = kvp_ref if sb == 0 else kvc_ref.at[pl.ds((sb - 1) * WINDOW, WINDOW), :]
        _attn_block(sink_ref, q_ref.at[rows, :], kvc_ref.at[rows, :], prev, kvm_ref, o_ref.at[rows, :],
                    (pl.program_id(0) * step_blocks + sb) % blocks_per_seq)


def _attn_block(sink_ref, q_ref, kvc_ref, kvp_ref, kvm_ref, o_ref, n):
    w = WINDOW
    hd = HEAD_DIM
    qi = lax.broadcasted_iota(jnp.int32, (w, w), 0)
    lane = lax.broadcasted_iota(jnp.int32, (w, w), 1)
    vis_prev = (lane > qi) & (n > 0)
    vis_cur = lane <= qi
    left = lane < hd
    meta_l = lane < N_META
    meta_r = (lane >= N_META) & (lane < 2 * N_META)

    def placed(x_bf):
        x = x_bf.astype(F32)
        xr = pltpu.roll(x, hd, 1)
        lm = lax.broadcasted_iota(jnp.int32, x.shape, 1) < hd
        z = jnp.zeros_like(x)
        return {(0, 0): jnp.where(lm, x, z), (0, 1): jnp.where(lm, z, xr),
                (1, 0): jnp.where(lm, xr, z), (1, 1): jnp.where(lm, z, x)}

    kp, kc, km = placed(kvp_ref[:, :KV_WIDTH]), placed(kvc_ref[:, :KV_WIDTH]), placed(kvm_ref[:, :KV_WIDTH])
    vp, vc, vm = placed(kvp_ref[:, KV_WIDTH:]), placed(kvc_ref[:, KV_WIDTH:]), placed(kvm_ref[:, KV_WIDTH:])
    pad_rows = w - 2 * N_META
    zpad = jnp.zeros((pad_rows, w), F32)
    krow = lax.broadcasted_iota(jnp.int32, (5 * w, w), 0)
    klane = lax.broadcasted_iota(jnp.int32, (5 * w, w), 1)
    row_l = (krow < 2 * w) | ((krow >= 4 * w) & (krow < 4 * w + N_META))
    row_r = ((krow >= 2 * w) & (krow < 4 * w)) | ((krow >= 4 * w + N_META) & (krow < 4 * w + 2 * N_META))
    den_cols = jnp.where((row_l & (klane < hd)) | (row_r & (klane >= hd)), 1.0, 0.0)

    for j in range(N_KV_HEADS):
        kcat = jnp.concatenate(
            [kp[j, 0], kc[j, 0], kp[j, 1], kc[j, 1], km[j, 0], km[j, 1], zpad], axis=0).astype(BF16)
        vcat = jnp.concatenate(
            [jnp.concatenate([vp[j, 0], vc[j, 0], vp[j, 1], vc[j, 1], vm[j, 0], vm[j, 1], zpad], axis=0),
             den_cols], axis=1).astype(BF16)
        for r in range(KV_REP // 2):
            pr = j * (KV_REP // 2) + r
            s = _dot_nt(q_ref[:, pr * w:(pr + 1) * w], kcat)
            s_l = jnp.where(vis_cur, s[:, w:2 * w], jnp.where(vis_prev, s[:, 0:w], NEG_INF))
            s_r = jnp.where(vis_cur, s[:, 3 * w:4 * w], jnp.where(vis_prev, s[:, 2 * w:3 * w], NEG_INF))
            s_m = s[:, 4 * w:]
            sink_l, sink_r = sink_ref[2 * pr] * LOG2_E, sink_ref[2 * pr + 1] * LOG2_E
            m_l = jnp.maximum(jnp.max(jnp.maximum(s_l, jnp.where(meta_l, s_m, NEG_INF)),
                                      axis=1, keepdims=True), sink_l)
            m_r = jnp.maximum(jnp.max(jnp.maximum(s_r, jnp.where(meta_r, s_m, NEG_INF)),
                                      axis=1, keepdims=True), sink_r)
            s_m = jnp.where(meta_l, s_m - m_l, jnp.where(meta_r, s_m - m_r, NEG_INF))
            e_l, e_r = jnp.exp2(s_l - m_l), jnp.exp2(s_r - m_r)
            e = jnp.concatenate([jnp.where(vis_cur, 0.0, e_l), jnp.where(vis_cur, e_l, 0.0),
                                 jnp.where(vis_cur, 0.0, e_r), jnp.where(vis_cur, e_r, 0.0),
                                 jnp.exp2(s_m)], axis=1).astype(BF16)
            acc = _dot(e, vcat)
            den = acc[:, w:] + jnp.where(left, jnp.exp2(sink_l - m_l), jnp.exp2(sink_r - m_r))
            o_ref[:, pr * w:(pr + 1) * w] = (acc[:, :w] / den).astype(BF16)


def _attention(sinks, q, kv, kv_meta, blocks_per_seq):
    n = q.shape[0]
    step_blocks = math.gcd(ATTN_STEP_BLOCKS, blocks_per_seq)
    step = step_blocks * WINDOW
    return pl.pallas_call(
        functools.partial(_attn_kernel, blocks_per_seq=blocks_per_seq, step_blocks=step_blocks),
        grid=(n // step,),
        in_specs=[
            pl.BlockSpec(memory_space=pltpu.SMEM),
            pl.BlockSpec((step, Q_WIDTH), lambda g: (g, 0)),
            pl.BlockSpec((step, 2 * KV_WIDTH), lambda g: (g, 0)),
            pl.BlockSpec((WINDOW, 2 * KV_WIDTH), lambda g: (jnp.maximum(g * step_blocks - 1, 0), 0)),
            pl.BlockSpec((N_META, 2 * KV_WIDTH), lambda g: (0, 0)),
        ],
        out_specs=pl.BlockSpec((step, Q_WIDTH), lambda g: (g, 0)),
        out_shape=jax.ShapeDtypeStruct((n, Q_WIDTH), BF16),
        compiler_params=pltpu.CompilerParams(dimension_semantics=("arbitrary",)),
        name="attn",
    )(sinks, q, kv, kv, kv_meta)


_ROUTE_IDX, _ROUTE_RANK, _ROUTE_GATE, _ROUTE_ROWS = 0, TOP_K, 2 * TOP_K, 16


def _mix_kernel(x_ref, y_ref, at_ref, gs_ref, ga_ref, wglu_ref, wo_ref, wout_ref, fg_ref, rw_ref, rb_ref,
                h1_ref, hf_ref, route_ref, cnt_ref, cnt_scr):
    tm = x_ref.shape[0]

    @pl.when(pl.program_id(0) == 0)
    def _():
        cnt_scr[...] = jnp.zeros_like(cnt_scr)

    y_ssm = jnp.concatenate([y_ref[c * SSM_PITCH:c * SSM_PITCH + SSM_CHUNK, :]
                             for c in range(tm // SSM_CHUNK)], axis=0)
    glu = _dot(jax.nn.gelu(y_ssm).astype(BF16), wglu_ref[...])
    branch_ssm = glu[:, :D_MODEL] * jax.nn.sigmoid(glu[:, D_MODEL:])
    branch_attn = _dot(at_ref[...], wo_ref[...])
    merged = gs_ref[...].astype(F32) * branch_ssm + ga_ref[...].astype(F32) * branch_attn
    h1 = x_ref[...] + _dot(merged.astype(BF16), wout_ref[...])
    h1_ref[...] = h1
    ms = jnp.mean(h1 * h1, axis=-1, keepdims=True)
    hf = h1 * lax.rsqrt(ms + RMS_EPS) * fg_ref[...]
    _store_token_tiles(hf_ref, hf)

    hf_hi = hf.astype(BF16)
    hf_lo = (hf - hf_hi.astype(F32)).astype(BF16)
    hi_prod = _dot(hf_hi, rw_ref[...])
    logits = (hi_prod[:, :LANES] + (hi_prod[:, LANES:] + _dot(hf_lo, rw_ref[:, :LANES]))
              + rb_ref[...])
    lt = logits.T[:N_EXPERTS, :]
    erow = lax.broadcasted_iota(jnp.int32, (N_EXPERTS, tm), 0)
    vals, idxs, hots = [], [], []
    rest = lt
    for _ in range(TOP_K):
        m = jnp.max(rest, axis=0, keepdims=True)
        first = jnp.min(jnp.where(rest == m, erow, N_EXPERTS), axis=0, keepdims=True)
        hot = erow == first
        vals.append(m)
        idxs.append(first)
        hots.append(hot)
        rest = jnp.where(hot, -jnp.inf, rest)
    exps = [jnp.exp(v - vals[0]) for v in vals]
    tot = exps[0] + exps[1] + exps[2] + exps[3]

    sel = (hots[0] | hots[1] | hots[2] | hots[3]).astype(F32)
    ti = lax.broadcasted_iota(jnp.int32, (tm, tm), 0)
    tj = lax.broadcasted_iota(jnp.int32, (tm, tm), 1)
    earlier = (ti < tj).astype(BF16)
    rank_e = _dot(sel.astype(BF16), earlier) + cnt_scr[...]
    cnt_scr[...] = cnt_scr[...] + jnp.sum(sel, axis=1, keepdims=True)
    cnt_ref[...] = cnt_scr[...]

    rrow = lax.broadcasted_iota(jnp.int32, (_ROUTE_ROWS, tm), 0)
    route = jnp.zeros((_ROUTE_ROWS, tm), F32)
    for k in range(TOP_K):
        rank_k = jnp.sum(jnp.where(hots[k], rank_e, 0.0), axis=0, keepdims=True)
        route = jnp.where(rrow == _ROUTE_IDX + k, idxs[k].astype(F32), route)
        route = jnp.where(rrow == _ROUTE_RANK + k, rank_k, route)
        route = jnp.where(rrow == _ROUTE_GATE + k, exps[k] / tot, route)
    route_ref[...] = route


def _mix(x2, y, attn, gs, ga, wglu, wo, wout, fg, rw, rb, tm):
    n = x2.shape[0]
    rw = jnp.pad(rw, ((0, 0), (0, LANES - N_EXPERTS)))
    rb = jnp.pad(rb, ((0, 0), (0, LANES - N_EXPERTS)), constant_values=NEG_INF)
    rw_hi = rw.astype(BF16)
    rw_split = jnp.concatenate([rw_hi, (rw - rw_hi.astype(F32)).astype(BF16)], axis=1)
    row = lambda w: pl.BlockSpec((tm, w), lambda i: (i, 0))
    full = lambda a: pl.BlockSpec(a.shape, lambda i: (0,) * a.ndim)
    return pl.pallas_call(
        _mix_kernel,
        grid=(n // tm,),
        in_specs=[row(D_MODEL), pl.BlockSpec((tm // SSM_CHUNK * SSM_PITCH, SSM_WIDTH), lambda i: (i, 0)),
                  row(Q_WIDTH), row(D_MODEL), row(D_MODEL),
                  full(wglu), full(wo), full(wout), full(fg), full(rw_split), full(rb)],
        out_specs=[row(D_MODEL), pl.BlockSpec((tm * TILE_ROWS, LANES), lambda i: (i, 0)),
                   pl.BlockSpec((_ROUTE_ROWS, tm), lambda i: (0, i)),
                   pl.BlockSpec((N_EXPERTS, 1), lambda i: (0, 0))],
        out_shape=[
            jax.ShapeDtypeStruct((n, D_MODEL), F32),
            jax.ShapeDtypeStruct((n * TILE_ROWS, LANES), F32),
            jax.ShapeDtypeStruct((_ROUTE_ROWS, n), F32),
            jax.ShapeDtypeStruct((N_EXPERTS, 1), F32),
        ],
        scratch_shapes=[pltpu.VMEM((N_EXPERTS, 1), F32)],
        compiler_params=pltpu.CompilerParams(
            dimension_semantics=("arbitrary",), vmem_limit_bytes=VMEM_LIMIT),
        name="mix_router",
    )(x2, y, attn, gs, ga, wglu, wo, wout, fg, rw_split, rb)


def _tiles_wait_copy(src_hbm, dst, n_tiles, sem):
    rows = n_tiles * TILE_ROWS
    return pltpu.make_async_copy(src_hbm.at[pl.ds(0, rows), :], dst.at[pl.ds(0, rows), :], sem)


_ISSUE_UNROLL = 16


_DISPATCH_RING = 4


def _dispatch_kernel(dst_ref, pad_start_ref, pad_len_ref, nu_ref, hf_hbm, xs_hbm, zero_blk, ring, in_sem,
                     out_sem, pad_sem, *, n_steps):
    i = pl.program_id(0)
    last = n_steps - 1
    pairs = dst_ref.shape[0]
    tokens = pairs // TOP_K
    blk_rows = tokens * TILE_ROWS
    n_blocks = xs_hbm.shape[0] // (EXPERT_ROWS * TILE_ROWS)
    slot = i % _DISPATCH_RING

    def load(step, s):
        src = hf_hbm.at[pl.ds(pl.multiple_of(step * blk_rows, blk_rows), blk_rows), :]
        return pltpu.make_async_copy(src, ring.at[s], in_sem.at[s])

    def wait_copies(s):
        for _ in range(TOP_K):
            pltpu.make_async_copy(ring.at[s], xs_hbm.at[pl.ds(0, blk_rows), :], out_sem.at[s]).wait()

    def zero_copy(row, rows):
        src = zero_blk.at[pl.ds(0, rows * TILE_ROWS), :]
        dst = xs_hbm.at[pl.ds(pl.multiple_of(row * TILE_ROWS, TILE_ROWS), rows * TILE_ROWS), :]
        return pltpu.make_async_copy(src, dst, pad_sem)

    def for_each_pad(fn):
        def per_expert(e, carry):
            row, left = pad_start_ref[e], pad_len_ref[e]
            size = EXPERT_ROWS // 2
            while size >= 1:
                take = left & size

                @pl.when(take != 0)
                def _(row=row, size=size):
                    fn(zero_copy(row, size))

                row = row + take
                size //= 2
            return carry
        lax.fori_loop(0, N_EXPERTS, per_expert, 0)

        def per_block(b, carry):
            fn(zero_copy(b * EXPERT_ROWS, EXPERT_ROWS))
            return carry
        lax.fori_loop(nu_ref[0], n_blocks, per_block, 0)

    @pl.when(i == 0)
    def _():
        for s in range(min(2, n_steps)):
            load(s, s).start()
        zero_blk[...] = jnp.zeros_like(zero_blk)
        for_each_pad(lambda cp: cp.start())

    @pl.when(i >= 2)
    def _():
        wait_copies((i + 2) % _DISPATCH_RING)

    @pl.when(i + 2 < n_steps)
    def _():
        load(i + 2, (i + 2) % _DISPATCH_RING).start()

    load(i, slot).wait()
    src_blk = ring.at[slot]
    for k in range(TOP_K):
        def issue(o, carry, k=k):
            tok0 = pl.multiple_of(o * _ISSUE_UNROLL, _ISSUE_UNROLL)
            dsts = [dst_ref[k * tokens + tok0 + r] for r in range(_ISSUE_UNROLL)]
            for r in range(_ISSUE_UNROLL):
                pltpu.make_async_copy(_token_tile(src_blk, tok0 + r), _token_tile(xs_hbm, dsts[r]),
                                      out_sem.at[slot]).start(priority=r % 2)
            return carry

        lax.fori_loop(0, tokens // _ISSUE_UNROLL, issue, 0)

    @pl.when(i == last)
    def _():
        if n_steps > 1:
            wait_copies((i + _DISPATCH_RING - 1) % _DISPATCH_RING)
        wait_copies(slot)
        for_each_pad(lambda cp: cp.wait())


def _dispatch(dest, pad_start, pad_len, n_used, hf_tiles, n_rows, tokens_per_step):
    n = dest.shape[0] // TOP_K
    pairs = tokens_per_step * TOP_K
    assert n % tokens_per_step == 0 and tokens_per_step % _ISSUE_UNROLL == 0
    smem = lambda: pl.BlockSpec(memory_space=pltpu.SMEM)
    n_steps = n // tokens_per_step
    return pl.pallas_call(
        functools.partial(_dispatch_kernel, n_steps=n_steps),
        grid=(n_steps,),
        in_specs=[pl.BlockSpec((pairs,), lambda i: (i,), memory_space=pltpu.SMEM), smem(), smem(), smem(),
                  pl.BlockSpec(memory_space=pl.ANY)],
        out_specs=pl.BlockSpec(memory_space=pl.ANY),
        out_shape=jax.ShapeDtypeStruct((n_rows * TILE_ROWS, LANES), F32),
        scratch_shapes=[pltpu.VMEM((EXPERT_ROWS * TILE_ROWS, LANES), F32),
                        pltpu.VMEM((_DISPATCH_RING, tokens_per_step * TILE_ROWS, LANES), F32),
                        pltpu.SemaphoreType.DMA((_DISPATCH_RING,)), pltpu.SemaphoreType.DMA((_DISPATCH_RING,)),
                        pltpu.SemaphoreType.DMA],
        compiler_params=pltpu.CompilerParams(
            dimension_semantics=("arbitrary",), vmem_limit_bytes=VMEM_LIMIT),
        name="dispatch",
    )(dest, pad_start, pad_len, n_used, hf_tiles)


def _expert_kernel(be_ref, nu_ref, nxt_ref, val_ref, xs_ref, wu_hbm, bu_ref, wd_hbm, bd_ref, y_ref,
                   wu_f32, wd_f32, wu_bf, wd_bf, wsem):
    i = pl.program_id(0)

    def mlp(xb):
        up = _dot(xb, wu_bf[...]) + bu_ref[0]
        x_glu = jnp.minimum(up[:, :D_FF], SWIGLU_LIMIT)
        x_lin = jnp.clip(up[:, D_FF:], -SWIGLU_LIMIT, SWIGLU_LIMIT)
        act = x_glu * jax.nn.sigmoid(SWIGLU_ALPHA * x_glu) * (x_lin + 1.0)
        return _dot(act.astype(BF16), wd_bf[...]) + bd_ref[0]

    def weight_copies(expert):
        return (pltpu.make_async_copy(wu_hbm.at[expert], wu_f32, wsem.at[0]),
                pltpu.make_async_copy(wd_hbm.at[expert], wd_f32, wsem.at[1]))

    @pl.when(i == 0)
    def _():
        for cp in weight_copies(be_ref[0]):
            cp.start()

    first = (i == 0) | (be_ref[i] != be_ref[jnp.maximum(i - 1, 0)])
    used = i < nu_ref[0]
    full = used & (val_ref[i] > EXPERT_ROWS - EXPERT_SUB_ROWS)

    def take_weights():
        for cp in weight_copies(be_ref[i]):
            cp.wait()
        wu_bf[...] = wu_f32[...].astype(BF16)
        wd_bf[...] = wd_f32[...].astype(BF16)

    def fetch_next_weights():
        @pl.when(nxt_ref[i] >= 0)
        def _():
            for cp in weight_copies(nxt_ref[i]):
                cp.start(priority=1)

    def whole_block():
        _store_token_tiles(y_ref, mlp(_load_token_tiles(xs_ref, 0, EXPERT_ROWS).astype(BF16)))

    @pl.when(first & full)
    def _():
        take_weights()
        whole_block()
        fetch_next_weights()

    @pl.when(first & jnp.logical_not(full))
    def _():
        take_weights()
        fetch_next_weights()

    @pl.when(full & jnp.logical_not(first))
    def _():
        whole_block()

    @pl.when(used & jnp.logical_not(full))
    def _():
        sub = EXPERT_SUB_ROWS
        n_sub = lax.shift_right_logical(val_ref[i] + (sub - 1), sub.bit_length() - 1)

        def compute(j, carry):
            row0 = pl.multiple_of(j * sub, sub)
            _store_token_tiles(y_ref, mlp(_load_token_tiles(xs_ref, row0, sub).astype(BF16)), row0)
            return carry

        def clear(j, carry):
            row0 = pl.multiple_of(j * sub * TILE_ROWS, sub * TILE_ROWS)
            y_ref[pl.ds(row0, sub * TILE_ROWS), :] = jnp.zeros((sub * TILE_ROWS, LANES), F32)
            return carry

        lax.fori_loop(0, n_sub, compute, 0)
        lax.fori_loop(n_sub, EXPERT_ROWS // sub, clear, 0)

    @pl.when(jnp.logical_not(used))
    def _():
        y_ref[...] = jnp.zeros_like(y_ref)


def _experts(block_expert, n_used, next_expert, valid_rows, xs_tiles, w_up, b_up, w_down, b_down):
    n_blocks = block_expert.shape[0]
    blk = (EXPERT_ROWS * TILE_ROWS, LANES)
    grid_spec = pltpu.PrefetchScalarGridSpec(
        num_scalar_prefetch=4,
        grid=(n_blocks,),
        in_specs=[
            pl.BlockSpec(blk, lambda i, be, nu, *_: (jnp.minimum(i, nu[0] - 1), 0)),
            pl.BlockSpec(memory_space=pl.ANY),
            pl.BlockSpec((1, 1, 2 * D_FF), lambda i, be, *_: (be[i], 0, 0)),
            pl.BlockSpec(memory_space=pl.ANY),
            pl.BlockSpec((1, 1, D_MODEL), lambda i, be, *_: (be[i], 0, 0)),
        ],
        out_specs=pl.BlockSpec(blk, lambda i, *_: (i, 0)),
        scratch_shapes=[
            pltpu.VMEM((D_MODEL, 2 * D_FF), F32),
            pltpu.VMEM((D_FF, D_MODEL), F32),
            pltpu.VMEM((D_MODEL, 2 * D_FF), BF16),
            pltpu.VMEM((D_FF, D_MODEL), BF16),
            pltpu.SemaphoreType.DMA((2,)),
        ],
    )
    return pl.pallas_call(
        _expert_kernel,
        grid_spec=grid_spec,
        out_shape=jax.ShapeDtypeStruct((n_blocks * blk[0], LANES), F32),
        compiler_params=pltpu.CompilerParams(
            dimension_semantics=("arbitrary",), vmem_limit_bytes=VMEM_LIMIT),
        name="experts",
    )(block_expert, n_used, next_expert, valid_rows, xs_tiles,
      w_up, b_up[:, None, :], w_down, b_down[:, None, :])


_COMBINE_RING = 3


def _combine_kernel(dst0_ref, dst1_ref, dst2_ref, y_hbm, h1_ref, route_ref, g_ref, o_ref, *scratch):
    bufs, sem = scratch[:_COMBINE_RING], scratch[_COMBINE_RING]
    i = pl.program_id(0)
    last = pl.num_programs(0) - 1
    tm = h1_ref.shape[0]
    rows = TOP_K * tm

    def gather_group(idx_ref, s, row0):
        srcs = [idx_ref[row0 + r] for r in range(_ISSUE_UNROLL)]
        for r in range(_ISSUE_UNROLL):
            pltpu.make_async_copy(_token_tile(y_hbm, srcs[r]), _token_tile(bufs[s], row0 + r),
                                  sem.at[s]).start(priority=r % 2)

    @pl.when(i == 0)
    def _():
        for s, idx_ref in ((0, dst0_ref), (1, dst1_ref)):
            def body(o, carry, s=s, idx_ref=idx_ref):
                gather_group(idx_ref, s, pl.multiple_of(o * _ISSUE_UNROLL, _ISSUE_UNROLL))
                return carry
            lax.fori_loop(0, rows // _ISSUE_UNROLL, body, 0)

    def step(s):
        cur = bufs[s]
        ahead = (s + 2) % _COMBINE_RING
        _tiles_wait_copy(y_hbm, cur, rows, sem.at[s]).wait()
        for g in range(rows // _ISSUE_UNROLL):
            gather_group(dst2_ref, ahead, g * _ISSUE_UNROLL)
        assert tm == LANES
        rt = jnp.concatenate([route_ref[...], jnp.zeros((LANES - _ROUTE_ROWS, tm), F32)], axis=0).T
        acc = h1_ref[...]
        for k in range(TOP_K):
            gate = rt[:, _ROUTE_GATE + k:_ROUTE_GATE + k + 1]
            acc = acc + gate * _load_token_tiles(cur, k * tm, tm)
        ms = jnp.mean(acc * acc, axis=-1, keepdims=True)
        o_ref[...] = acc * lax.rsqrt(ms + RMS_EPS) * g_ref[...]

        @pl.when(i == last)
        def _():
            for t in ((s + 1) % _COMBINE_RING, ahead):
                _tiles_wait_copy(y_hbm, bufs[t], rows, sem.at[t]).wait()

    for s in range(_COMBINE_RING):
        pl.when(i % _COMBINE_RING == s)(functools.partial(step, s))


def _combine(dest_kmajor, y, h1, route, g, tm):
    n = h1.shape[0]
    n_tiles = n // tm
    idx_spec = lambda ahead: pl.BlockSpec(
        (TOP_K * tm,), lambda i: (jnp.minimum(i + ahead, n_tiles - 1),), memory_space=pltpu.SMEM)
    ring_buf = pltpu.VMEM((TOP_K * tm * TILE_ROWS, LANES), F32)
    return pl.pallas_call(
        _combine_kernel,
        grid=(n_tiles,),
        in_specs=[
            idx_spec(0), idx_spec(1), idx_spec(2),
            pl.BlockSpec(memory_space=pl.ANY),
            pl.BlockSpec((tm, D_MODEL), lambda i: (i, 0)),
            pl.BlockSpec((_ROUTE_ROWS, tm), lambda i: (0, i)),
            pl.BlockSpec((1, D_MODEL), lambda i: (0, 0)),
        ],
        out_specs=pl.BlockSpec((tm, D_MODEL), lambda i: (i, 0)),
        out_shape=jax.ShapeDtypeStruct((n, D_MODEL), F32),
        scratch_shapes=[ring_buf] * _COMBINE_RING + [pltpu.SemaphoreType.DMA((_COMBINE_RING,))],
        compiler_params=pltpu.CompilerParams(
            dimension_semantics=("arbitrary",), vmem_limit_bytes=VMEM_LIMIT),
        name="combine",
    )(dest_kmajor, dest_kmajor, dest_kmajor, y, h1, route, g)


def _routing_tables(route, counts, n, tokens_dispatch, tokens_combine):
    tm = EXPERT_ROWS
    i32 = jnp.int32
    n_blocks = (n * TOP_K + N_EXPERTS * (tm - 1)) // tm
    idx = route[_ROUTE_IDX:_ROUTE_IDX + TOP_K].astype(i32)
    rank = route[_ROUTE_RANK:_ROUTE_RANK + TOP_K].astype(i32)
    cnt = counts[:, 0].astype(i32)
    eid = jnp.arange(N_EXPERTS, dtype=i32)
    upto = eid[None, :] <= eid[:, None]
    blocks_e = (cnt + tm - 1) // tm
    blocks_end = jnp.sum(jnp.where(upto, blocks_e[None, :], 0), axis=1)
    row_start = (blocks_end - blocks_e) * tm
    n_used = blocks_end[N_EXPERTS - 1]
    used = blocks_e > 0

    def lookup(table, keys):
        hit = keys[None] == eid.reshape((N_EXPERTS,) + (1,) * keys.ndim)
        return jnp.sum(jnp.where(hit, table.reshape((N_EXPERTS,) + (1,) * keys.ndim), 0), axis=0)

    dest = lookup(row_start, idx) + rank
    blk = jnp.arange(n_blocks, dtype=i32)
    last_used = jnp.max(jnp.where(used, eid, 0))
    be = jnp.where(blk < n_used, jnp.sum((blocks_end[None, :] <= blk[:, None]).astype(i32), axis=1), last_used)
    later_used = used[None, :] & (eid[None, :] > eid[:, None])
    after = jnp.min(jnp.where(later_used, eid[None, :], N_EXPERTS), axis=1)
    next_e = jnp.where(after < N_EXPERTS, after, -1)

    def k_major(tokens):
        return dest.reshape(TOP_K, n // tokens, tokens).transpose(1, 0, 2).reshape(-1)

    valid = jnp.clip(lookup(row_start + cnt, be) - blk * tm, 0, tm)
    return (be.astype(i32), n_used.reshape(1), lookup(next_e, be).astype(i32), valid.astype(i32),
            k_major(tokens_dispatch), k_major(tokens_combine), row_start + cnt, blocks_e * tm - cnt,
            n_blocks * tm)


def kernel(x, meta_tokens, mix_norm_g, w_in, ssm_a_re, ssm_a_im, ssm_log_dt, ssm_b_re, ssm_b_im,
           ssm_c_re, ssm_c_im, ssm_d, w_ssm_glu, attn_sinks, w_attn_o, w_out, ffn_norm_g,
           router_w, router_b, w_up, b_up, w_down, b_down, final_norm_g):
    bsz, seq, d = x.shape
    assert d == D_MODEL and seq % max(WINDOW, SSM_CHUNK) == 0
    assert mix_norm_g.shape[0] == 1, "single-layer trunk"
    n = bsz * seq
    tm_proj = min(1024, n)
    tm_mix = min(1024, n)
    tm_comb = min(128, n)
    x2 = x.reshape(n, D_MODEL)

    w_in_bf = w_in[0].astype(BF16)
    g_mix = mix_norm_g[0][None, :]
    u, q, kv, gs, ga = _in_proj(x2, g_mix, w_in_bf, tm_proj, pitched=True)
    u_m, _, kv_m, _, _ = _in_proj(meta_tokens, g_mix, w_in_bf, N_META, pitched=False)

    ssm_par = _ssm_params(ssm_a_re[0], ssm_a_im[0], ssm_log_dt[0], ssm_b_re[0], ssm_b_im[0],
                          ssm_c_re[0], ssm_c_im[0], ssm_d[0], SSM_CHUNK)
    y_ssm = _ssm(u, u_m, *ssm_par, batch=bsz, chunk=SSM_CHUNK)

    attn = _attention(attn_sinks[0], q, kv, kv_m, seq // WINDOW)

    h1, hf, route, counts = _mix(
        x2, y_ssm, attn, gs, ga, w_ssm_glu[0].astype(BF16), w_attn_o[0].astype(BF16),
        w_out[0].astype(BF16), ffn_norm_g[0][None, :], router_w[0], router_b[0][None, :], tm_mix)

    tok_disp = min(1024, n)
    be, n_used, next_e, valid, dest_disp, dest_comb, pad_start, pad_len, n_rows = _routing_tables(
        route, counts, n, tok_disp, tm_comb)
    xs = _dispatch(dest_disp, pad_start, pad_len, n_used, hf, n_rows, tok_disp)
    y = _experts(be, n_used, next_e, valid, xs, w_up[0], b_up[0], w_down[0], b_down[0])
    out = _combine(dest_comb, y, h1, route, final_norm_g[None, :], tm_comb)
    return out.reshape(bsz, seq, D_MODEL)
```

```python
import functools
import math

import jax
import jax.numpy as jnp
from jax import lax
from jax.experimental import pallas as pl
from jax.experimental.pallas import tpu as pltpu

F32 = jnp.float32
BF16 = jnp.bfloat16

D_MODEL = 1024
N_META = 16
SSM_WIDTH = 512
SSM_GROUP = 16
SSM_GROUPS = 32
SSM_STATE = 64
HEAD_DIM = 64
N_HEADS = 16
N_KV_HEADS = 2
KV_REP = N_HEADS // N_KV_HEADS
WINDOW = 128
Q_WIDTH = N_HEADS * HEAD_DIM
KV_WIDTH = N_KV_HEADS * HEAD_DIM
N_EXPERTS = 32
TOP_K = 4
D_FF = 1024
SWIGLU_ALPHA = 1.702
SWIGLU_LIMIT = 7.0
RMS_EPS = 1e-5
NEG_INF = -1e30

_U0, _Q0, _KV0, _GS0, _GA0, _IN_END = 0, 512, 1536, 1792, 2816, 3840

SSM_CH_BLOCK = 128
SSM_HALF = (SSM_CH_BLOCK // SSM_GROUP) * SSM_STATE
SSM_CHUNK = 32
SSM_PITCH = 40
SSM_UNROLL_A = 16
SSM_UNROLL_B = 4
SSM_CHUNK_UNROLL = 4
LOG2_E = math.log2(math.e)
EXPERT_ROWS = 512
EXPERT_SUB_ROWS = 128
VMEM_LIMIT = 56 * 1024 * 1024


def _dot(a, b):
    return jnp.dot(a, b, preferred_element_type=F32)


def _dot_nt(a, b):
    return lax.dot_general(a, b, (((1,), (1,)), ((), ())), preferred_element_type=F32)


LANES = 128
TILE_ROWS = D_MODEL // LANES


def _store_token_tiles(ref, x, start_row=0):
    rows = x.shape[0]
    for j in range(TILE_ROWS):
        ref[pl.ds(start_row * TILE_ROWS + j, rows, stride=TILE_ROWS), :] = x[:, j * LANES:(j + 1) * LANES]


def _load_token_tiles(ref, start_row, rows):
    return jnp.concatenate(
        [ref[pl.ds(start_row * TILE_ROWS + j, rows, stride=TILE_ROWS), :] for j in range(TILE_ROWS)], axis=1)


def _token_tile(ref, row):
    return ref.at[pl.ds(pl.multiple_of(row * TILE_ROWS, TILE_ROWS), TILE_ROWS), :]


def _in_proj_kernel(x_ref, g_ref, w_ref, u_ref, q_ref, kv_ref, gs_ref, ga_ref, *, pitched):
    x = x_ref[...]
    ms = jnp.mean(x * x, axis=-1, keepdims=True)
    hn = (x * lax.rsqrt(ms + RMS_EPS) * g_ref[...]).astype(BF16)
    u = _dot(hn, w_ref[:, _U0:_Q0])
    if pitched:
        for c in range(u.shape[0] // SSM_CHUNK):
            u_ref[c * SSM_PITCH:c * SSM_PITCH + SSM_CHUNK, :] = u[c * SSM_CHUNK:(c + 1) * SSM_CHUNK, :]
            u_ref[c * SSM_PITCH + SSM_CHUNK:(c + 1) * SSM_PITCH, :] = jnp.zeros(
                (SSM_PITCH - SSM_CHUNK, SSM_WIDTH), F32)
    else:
        u_ref[...] = u
    q_ref[...] = (_dot(hn, w_ref[:, _Q0:_KV0]) * (HEAD_DIM ** -0.5 * LOG2_E)).astype(BF16)
    kv_ref[...] = _dot(hn, w_ref[:, _KV0:_GS0]).astype(BF16)
    gs_ref[...] = jax.nn.sigmoid(_dot(hn, w_ref[:, _GS0:_GA0])).astype(BF16)
    ga_ref[...] = jax.nn.sigmoid(_dot(hn, w_ref[:, _GA0:_IN_END])).astype(BF16)


def _in_proj(x2, g, w_bf, tm, pitched):
    n = x2.shape[0]
    row = lambda w: pl.BlockSpec((tm, w), lambda i: (i, 0))
    full = lambda a: pl.BlockSpec(a.shape, lambda i: (0,) * a.ndim)
    u_rows = (lambda r: r // SSM_CHUNK * SSM_PITCH) if pitched else (lambda r: r)
    return pl.pallas_call(
        functools.partial(_in_proj_kernel, pitched=pitched),
        grid=(n // tm,),
        in_specs=[row(D_MODEL), full(g), full(w_bf)],
        out_specs=[pl.BlockSpec((u_rows(tm), SSM_WIDTH), lambda i: (i, 0)),
                   row(Q_WIDTH), row(2 * KV_WIDTH), row(D_MODEL), row(D_MODEL)],
        out_shape=[
            jax.ShapeDtypeStruct((u_rows(n), SSM_WIDTH), F32),
            jax.ShapeDtypeStruct((n, Q_WIDTH), BF16),
            jax.ShapeDtypeStruct((n, 2 * KV_WIDTH), BF16),
            jax.ShapeDtypeStruct((n, D_MODEL), BF16),
            jax.ShapeDtypeStruct((n, D_MODEL), BF16),
        ],
        compiler_params=pltpu.CompilerParams(
            dimension_semantics=("arbitrary",), vmem_limit_bytes=VMEM_LIMIT),
        name="in_proj",
    )(x2, g, w_bf)


def _ssm_kernel(u_ref, um_ref, bm_ref, cm_ref, ar_ref, ai_ref, atr_ref, ati_ref, d_ref,
                y_ref, sre, sim, *, chunk, rows, batch):
    h = SSM_HALF
    bm = bm_ref[0]
    cm = cm_ref[0]
    ar, ai = ar_ref[0], ai_ref[0]
    atr, ati = atr_ref[0], ati_ref[0]
    dsk = d_ref[0]
    n_chunks = rows // batch

    def advance(sr, si, bu):
        return ar * sr - ai * si + bu[:, :h], ar * si + ai * sr + bu[:, h:]

    bum = _dot(um_ref[...].astype(BF16), bm)
    mr = jnp.zeros((1, h), F32)
    mi = jnp.zeros((1, h), F32)
    for j in range(N_META):
        mr, mi = advance(mr, mi, bum[j:j + 1, :])

    def u_step(t):
        return u_ref[pl.ds(t, rows, stride=SSM_PITCH), :]

    for pad in range(chunk, SSM_PITCH):
        y_ref[pl.ds(pad, rows, stride=SSM_PITCH), :] = jnp.zeros((rows, SSM_CH_BLOCK), F32)

    sre[...] = jnp.zeros_like(sre)
    sim[...] = jnp.zeros_like(sim)

    def pass_a(o, carry):
        for k in range(SSM_UNROLL_A):
            t = o * SSM_UNROLL_A + k
            bu = _dot(u_step(t).astype(BF16), bm)
            nr, ni = advance(sre[...], sim[...], bu)
            sre[...] = nr
            sim[...] = ni
        return carry

    lax.fori_loop(0, chunk // SSM_UNROLL_A, pass_a, 0)

    def over_chunks(o, carry):
        rows_of = [[pl.ds(b * n_chunks + o * SSM_CHUNK_UNROLL + k, 1) for b in range(batch)]
                   for k in range(SSM_CHUNK_UNROLL)]
        ends = [[(sre[row, :], sim[row, :]) for row in rows_k] for rows_k in rows_of]
        carry = list(carry)
        for k in range(SSM_CHUNK_UNROLL):
            for b in range(batch):
                cr, ci = carry[2 * b], carry[2 * b + 1]
                er, ei = ends[k][b]
                sre[rows_of[k][b], :] = cr
                sim[rows_of[k][b], :] = ci
                carry[2 * b], carry[2 * b + 1] = atr * cr - ati * ci + er, atr * ci + ati * cr + ei
        return tuple(carry)

    assert n_chunks % SSM_CHUNK_UNROLL == 0
    lax.fori_loop(0, n_chunks // SSM_CHUNK_UNROLL, over_chunks, (mr, mi) * batch)

    def pass_b(o, carry):
        for k in range(SSM_UNROLL_B):
            t = o * SSM_UNROLL_B + k
            ut = u_step(t)
            bu = _dot(ut.astype(BF16), bm)
            nr, ni = advance(sre[...], sim[...], bu)
            sre[...] = nr
            sim[...] = ni
            y = _dot(nr.astype(BF16), cm[:h, :]) + _dot(ni.astype(BF16), cm[h:, :]) + dsk * ut
            y_ref[pl.ds(t, rows, stride=SSM_PITCH), :] = y
        return carry

    lax.fori_loop(0, chunk // SSM_UNROLL_B, pass_b, 0)


def _ssm(u, u_meta, bmat, cmat, a_re, a_im, at_re, at_im, dskip, batch, chunk):
    n = u.shape[0]
    rows = n // SSM_PITCH
    nblk = SSM_WIDTH // SSM_CH_BLOCK
    col = lambda r: pl.BlockSpec((r, SSM_CH_BLOCK), lambda j: (0, j))
    par = lambda a: pl.BlockSpec((1,) + a.shape[1:], lambda j: (j, 0, 0))
    return pl.pallas_call(
        functools.partial(_ssm_kernel, chunk=chunk, rows=rows, batch=batch),
        grid=(nblk,),
        in_specs=[col(n), col(N_META), par(bmat), par(cmat), par(a_re), par(a_im),
                  par(at_re), par(at_im), par(dskip)],
        out_specs=col(n),
        out_shape=jax.ShapeDtypeStruct((n, SSM_WIDTH), F32),
        scratch_shapes=[pltpu.VMEM((rows, SSM_HALF), F32), pltpu.VMEM((rows, SSM_HALF), F32)],
        compiler_params=pltpu.CompilerParams(
            dimension_semantics=("arbitrary",), vmem_limit_bytes=VMEM_LIMIT),
        name="ssm",
    )(u, u_meta, bmat, cmat, a_re, a_im, at_re, at_im, dskip)


def _ssm_params(a_re, a_im, log_dt, b_re, b_im, c_re, c_im, d_skip, chunk):
    dt = jnp.exp(log_dt)[:, None]
    mag = jnp.exp(a_re * dt)
    ang = a_im * dt
    abar_re, abar_im = mag * jnp.cos(ang), mag * jnp.sin(ang)
    den = a_re * a_re + a_im * a_im
    nr, ni = abar_re - 1.0, abar_im
    coef_re = ((nr * a_re + ni * a_im) / den)[..., None]
    coef_im = ((ni * a_re - nr * a_im) / den)[..., None]
    bbar_re = coef_re * b_re - coef_im * b_im
    bbar_im = coef_re * b_im + coef_im * b_re
    magt = jnp.exp(a_re * dt * chunk)
    at_re, at_im = magt * jnp.cos(ang * chunk), magt * jnp.sin(ang * chunk)

    nblk = SSM_WIDTH // SSM_CH_BLOCK
    gpb = SSM_GROUPS // nblk
    eye = jnp.eye(gpb, dtype=F32)

    def in_map(b):
        b = b.reshape(nblk, gpb, SSM_STATE, SSM_GROUP)
        return jnp.einsum('jgpc,gh->jgchp', b, eye).reshape(nblk, SSM_CH_BLOCK, gpb * SSM_STATE)

    def out_map(c):
        c = c.reshape(nblk, gpb, SSM_GROUP, SSM_STATE)
        return jnp.einsum('jgcp,gh->jgphc', c, eye).reshape(nblk, gpb * SSM_STATE, SSM_CH_BLOCK)

    bmat = jnp.concatenate([in_map(bbar_re), in_map(bbar_im)], axis=2).astype(BF16)
    cmat = jnp.concatenate([out_map(c_re), -out_map(c_im)], axis=1).astype(BF16)
    vec = lambda v: v.reshape(nblk, 1, SSM_HALF)
    return (bmat, cmat, vec(abar_re), vec(abar_im), vec(at_re), vec(at_im),
            d_skip.reshape(nblk, 1, SSM_CH_BLOCK))


ATTN_STEP_BLOCKS = 16


def _attn_kernel(sink_ref, q_ref, kvc_ref, kvp_ref, kvm_ref, o_ref, *, blocks_per_seq, step_blocks):
    for sb in range(step_blocks):
        rows = pl.ds(sb * WINDOW, WINDOW)
        prev = kvp_ref if sb == 0 else kvc_ref.at[pl.ds((sb - 1) * WINDOW, WINDOW), :]
        _attn_block(sink_ref, q_ref.at[rows, :], kvc_ref.at[rows, :], prev, kvm_ref, o_ref.at[rows, :],
                    (pl.program_id(0) * step_blocks + sb) % blocks_per_seq)


def _attn_block(sink_ref, q_ref, kvc_ref, kvp_ref, kvm_ref, o_ref, n):
    w = WINDOW
    hd = HEAD_DIM
    qi = lax.broadcasted_iota(jnp.int32, (w, w), 0)
    lane = lax.broadcasted_iota(jnp.int32, (w, w), 1)
    vis_prev = (lane > qi) & (n > 0)
    vis_cur = lane <= qi
    left = lane < hd
    meta_l = lane < N_META
    meta_r = (lane >= N_META) & (lane < 2 * N_META)

    def placed(x_bf):
        x = x_bf.astype(F32)
        xr = pltpu.roll(x, hd, 1)
        lm = lax.broadcasted_iota(jnp.int32, x.shape, 1) < hd
        z = jnp.zeros_like(x)
        return {(0, 0): jnp.where(lm, x, z), (0, 1): jnp.where(lm, z, xr),
                (1, 0): jnp.where(lm, xr, z), (1, 1): jnp.where(lm, z, x)}

    kp, kc, km = placed(kvp_ref[:, :KV_WIDTH]), placed(kvc_ref[:, :KV_WIDTH]), placed(kvm_ref[:, :KV_WIDTH])
    vp, vc, vm = placed(kvp_ref[:, KV_WIDTH:]), placed(kvc_ref[:, KV_WIDTH:]), placed(kvm_ref[:, KV_WIDTH:])
    pad_rows = w - 2 * N_META
    zpad = jnp.zeros((pad_rows, w), F32)
    krow = lax.broadcasted_iota(jnp.int32, (5 * w, w), 0)
    klane = lax.broadcasted_iota(jnp.int32, (5 * w, w), 1)
    row_l = (krow < 2 * w) | ((krow >= 4 * w) & (krow < 4 * w + N_META))
    row_r = ((krow >= 2 * w) & (krow < 4 * w)) | ((krow >= 4 * w + N_META) & (krow < 4 * w + 2 * N_META))
    den_cols = jnp.where((row_l & (klane < hd)) | (row_r & (klane >= hd)), 1.0, 0.0)

    for j in range(N_KV_HEADS):
        kcat = jnp.concatenate(
            [kp[j, 0], kc[j, 0], kp[j, 1], kc[j, 1], km[j, 0], km[j, 1], zpad], axis=0).astype(BF16)
        vcat = jnp.concatenate(
            [jnp.concatenate([vp[j, 0], vc[j, 0], vp[j, 1], vc[j, 1], vm[j, 0], vm[j, 1], zpad], axis=0),
             den_cols], axis=1).astype(BF16)
        for r in range(KV_REP // 2):
            pr = j * (KV_REP // 2) + r
            s = _dot_nt(q_ref[:, pr * w:(pr + 1) * w], kcat)
            s_l = jnp.where(vis_cur, s[:, w:2 * w], jnp.where(vis_prev, s[:, 0:w], NEG_INF))
            s_r = jnp.where(vis_cur, s[:, 3 * w:4 * w], jnp.where(vis_prev, s[:, 2 * w:3 * w], NEG_INF))
            s_m = s[:, 4 * w:]
            sink_l, sink_r = sink_ref[2 * pr] * LOG2_E, sink_ref[2 * pr + 1] * LOG2_E
            m_l = jnp.maximum(jnp.max(jnp.maximum(s_l, jnp.where(meta_l, s_m, NEG_INF)),
                                      axis=1, keepdims=True), sink_l)
            m_r = jnp.maximum(jnp.max(jnp.maximum(s_r, jnp.where(meta_r, s_m, NEG_INF)),
                                      axis=1, keepdims=True), sink_r)
            s_m = jnp.where(meta_l, s_m - m_l, jnp.where(meta_r, s_m - m_r, NEG_INF))
            e_l, e_r = jnp.exp2(s_l - m_l), jnp.exp2(s_r - m_r)
            e = jnp.concatenate([jnp.where(vis_cur, 0.0, e_l), jnp.where(vis_cur, e_l, 0.0),
                                 jnp.where(vis_cur, 0.0, e_r), jnp.where(vis_cur, e_r, 0.0),
                                 jnp.exp2(s_m)], axis=1).astype(BF16)
            acc = _dot(e, vcat)
            den = acc[:, w:] + jnp.where(left, jnp.exp2(sink_l - m_l), jnp.exp2(sink_r - m_r))
            o_ref[:, pr * w:(pr + 1) * w] = (acc[:, :w] / den).astype(BF16)


def _attention(sinks, q, kv, kv_meta, blocks_per_seq):
    n = q.shape[0]
    step_blocks = math.gcd(ATTN_STEP_BLOCKS, blocks_per_seq)
    step = step_blocks * WINDOW
    return pl.pallas_call(
        functools.partial(_attn_kernel, blocks_per_seq=blocks_per_seq, step_blocks=step_blocks),
        grid=(n // step,),
        in_specs=[
            pl.BlockSpec(memory_space=pltpu.SMEM),
            pl.BlockSpec((step, Q_WIDTH), lambda g: (g, 0)),
            pl.BlockSpec((step, 2 * KV_WIDTH), lambda g: (g, 0)),
            pl.BlockSpec((WINDOW, 2 * KV_WIDTH), lambda g: (jnp.maximum(g * step_blocks - 1, 0), 0)),
            pl.BlockSpec((N_META, 2 * KV_WIDTH), lambda g: (0, 0)),
        ],
        out_specs=pl.BlockSpec((step, Q_WIDTH), lambda g: (g, 0)),
        out_shape=jax.ShapeDtypeStruct((n, Q_WIDTH), BF16),
        compiler_params=pltpu.CompilerParams(dimension_semantics=("arbitrary",)),
        name="attn",
    )(sinks, q, kv, kv, kv_meta)


_ROUTE_IDX, _ROUTE_RANK, _ROUTE_GATE, _ROUTE_ROWS = 0, TOP_K, 2 * TOP_K, 16


def _mix_kernel(x_ref, y_ref, at_ref, gs_ref, ga_ref, wglu_ref, wo_ref, wout_ref, fg_ref, rw_ref, rb_ref,
                h1_ref, hf_ref, route_ref, cnt_ref, cnt_scr):
    tm = x_ref.shape[0]

    @pl.when(pl.program_id(0) == 0)
    def _():
        cnt_scr[...] = jnp.zeros_like(cnt_scr)

    y_ssm = jnp.concatenate([y_ref[c * SSM_PITCH:c * SSM_PITCH + SSM_CHUNK, :]
                             for c in range(tm // SSM_CHUNK)], axis=0)
    glu = _dot(jax.nn.gelu(y_ssm).astype(BF16), wglu_ref[...])
    branch_ssm = glu[:, :D_MODEL] * jax.nn.sigmoid(glu[:, D_MODEL:])
    branch_attn = _dot(at_ref[...], wo_ref[...])
    merged = gs_ref[...].astype(F32) * branch_ssm + ga_ref[...].astype(F32) * branch_attn
    h1 = x_ref[...] + _dot(merged.astype(BF16), wout_ref[...])
    h1_ref[...] = h1
    ms = jnp.mean(h1 * h1, axis=-1, keepdims=True)
    hf = h1 * lax.rsqrt(ms + RMS_EPS) * fg_ref[...]
    _store_token_tiles(hf_ref, hf)

    hf_hi = hf.astype(BF16)
    hf_lo = (hf - hf_hi.astype(F32)).astype(BF16)
    hi_prod = _dot(hf_hi, rw_ref[...])
    logits = (hi_prod[:, :LANES] + (hi_prod[:, LANES:] + _dot(hf_lo, rw_ref[:, :LANES]))
              + rb_ref[...])
    lt = logits.T[:N_EXPERTS, :]
    erow = lax.broadcasted_iota(jnp.int32, (N_EXPERTS, tm), 0)
    vals, idxs, hots = [], [], []
    rest = lt
    for _ in range(TOP_K):
        m = jnp.max(rest, axis=0, keepdims=True)
        first = jnp.min(jnp.where(rest == m, erow, N_EXPERTS), axis=0, keepdims=True)
        hot = erow == first
        vals.append(m)
        idxs.append(first)
        hots.append(hot)
        rest = jnp.where(hot, -jnp.inf, rest)
    exps = [jnp.exp(v - vals[0]) for v in vals]
    tot = exps[0] + exps[1] + exps[2] + exps[3]

    sel = (hots[0] | hots[1] | hots[2] | hots[3]).astype(F32)
    ti = lax.broadcasted_iota(jnp.int32, (tm, tm), 0)
    tj = lax.broadcasted_iota(jnp.int32, (tm, tm), 1)
    earlier = (ti < tj).astype(BF16)
    rank_e = _dot(sel.astype(BF16), earlier) + cnt_scr[...]
    cnt_scr[...] = cnt_scr[...] + jnp.sum(sel, axis=1, keepdims=True)
    cnt_ref[...] = cnt_scr[...]

    rrow = lax.broadcasted_iota(jnp.int32, (_ROUTE_ROWS, tm), 0)
    route = jnp.zeros((_ROUTE_ROWS, tm), F32)
    for k in range(TOP_K):
        rank_k = jnp.sum(jnp.where(hots[k], rank_e, 0.0), axis=0, keepdims=True)
        route = jnp.where(rrow == _ROUTE_IDX + k, idxs[k].astype(F32), route)
        route = jnp.where(rrow == _ROUTE_RANK + k, rank_k, route)
        route = jnp.where(rrow == _ROUTE_GATE + k, exps[k] / tot, route)
    route_ref[...] = route


def _mix(x2, y, attn, gs, ga, wglu, wo, wout, fg, rw, rb, tm):
    n = x2.shape[0]
    rw = jnp.pad(rw, ((0, 0), (0, LANES - N_EXPERTS)))
    rb = jnp.pad(rb, ((0, 0), (0, LANES - N_EXPERTS)), constant_values=NEG_INF)
    rw_hi = rw.astype(BF16)
    rw_split = jnp.concatenate([rw_hi, (rw - rw_hi.astype(F32)).astype(BF16)], axis=1)
    row = lambda w: pl.BlockSpec((tm, w), lambda i: (i, 0))
    full = lambda a: pl.BlockSpec(a.shape, lambda i: (0,) * a.ndim)
    return pl.pallas_call(
        _mix_kernel,
        grid=(n // tm,),
        in_specs=[row(D_MODEL), pl.BlockSpec((tm // SSM_CHUNK * SSM_PITCH, SSM_WIDTH), lambda i: (i, 0)),
                  row(Q_WIDTH), row(D_MODEL), row(D_MODEL),
                  full(wglu), full(wo), full(wout), full(fg), full(rw_split), full(rb)],
        out_specs=[row(D_MODEL), pl.BlockSpec((tm * TILE_ROWS, LANES), lambda i: (i, 0)),
                   pl.BlockSpec((_ROUTE_ROWS, tm), lambda i: (0, i)),
                   pl.BlockSpec((N_EXPERTS, 1), lambda i: (0, 0))],
        out_shape=[
            jax.ShapeDtypeStruct((n, D_MODEL), F32),
            jax.ShapeDtypeStruct((n * TILE_ROWS, LANES), F32),
            jax.ShapeDtypeStruct((_ROUTE_ROWS, n), F32),
            jax.ShapeDtypeStruct((N_EXPERTS, 1), F32),
        ],
        scratch_shapes=[pltpu.VMEM((N_EXPERTS, 1), F32)],
        compiler_params=pltpu.CompilerParams(
            dimension_semantics=("arbitrary",), vmem_limit_bytes=VMEM_LIMIT),
        name="mix_router",
    )(x2, y, attn, gs, ga, wglu, wo, wout, fg, rw_split, rb)


def _tiles_wait_copy(src_hbm, dst, n_tiles, sem):
    rows = n_tiles * TILE_ROWS
    return pltpu.make_async_copy(src_hbm.at[pl.ds(0, rows), :], dst.at[pl.ds(0, rows), :], sem)


_ISSUE_UNROLL = 16


_DISPATCH_RING = 4


def _dispatch_kernel(dst_ref, pad_start_ref, pad_len_ref, nu_ref, hf_hbm, xs_hbm, zero_blk, ring, in_sem,
                     out_sem, pad_sem, *, n_steps):
    i = pl.program_id(0)
    last = n_steps - 1
    pairs = dst_ref.shape[0]
    tokens = pairs // TOP_K
    blk_rows = tokens * TILE_ROWS
    n_blocks = xs_hbm.shape[0] // (EXPERT_ROWS * TILE_ROWS)
    slot = i % _DISPATCH_RING

    def load(step, s):
        src = hf_hbm.at[pl.ds(pl.multiple_of(step * blk_rows, blk_rows), blk_rows), :]
        return pltpu.make_async_copy(src, ring.at[s], in_sem.at[s])

    def wait_copies(s):
        for _ in range(TOP_K):
            pltpu.make_async_copy(ring.at[s], xs_hbm.at[pl.ds(0, blk_rows), :], out_sem.at[s]).wait()

    def zero_copy(row, rows):
        src = zero_blk.at[pl.ds(0, rows * TILE_ROWS), :]
        dst = xs_hbm.at[pl.ds(pl.multiple_of(row * TILE_ROWS, TILE_ROWS), rows * TILE_ROWS), :]
        return pltpu.make_async_copy(src, dst, pad_sem)

    def for_each_pad(fn):
        def per_expert(e, carry):
            row, left = pad_start_ref[e], pad_len_ref[e]
            size = EXPERT_ROWS // 2
            while size >= 1:
                take = left & size

                @pl.when(take != 0)
                def _(row=row, size=size):
                    fn(zero_copy(row, size))

                row = row + take
                size //= 2
            return carry
        lax.fori_loop(0, N_EXPERTS, per_expert, 0)

        def per_block(b, carry):
            fn(zero_copy(b * EXPERT_ROWS, EXPERT_ROWS))
            return carry
        lax.fori_loop(nu_ref[0], n_blocks, per_block, 0)

    @pl.when(i == 0)
    def _():
        for s in range(min(2, n_steps)):
            load(s, s).start()
        zero_blk[...] = jnp.zeros_like(zero_blk)
        for_each_pad(lambda cp: cp.start())

    @pl.when(i >= 2)
    def _():
        wait_copies((i + 2) % _DISPATCH_RING)

    @pl.when(i + 2 < n_steps)
    def _():
        load(i + 2, (i + 2) % _DISPATCH_RING).start()

    load(i, slot).wait()
    src_blk = ring.at[slot]
    for k in range(TOP_K):
        def issue(o, carry, k=k):
            tok0 = pl.multiple_of(o * _ISSUE_UNROLL, _ISSUE_UNROLL)
            dsts = [dst_ref[k * tokens + tok0 + r] for r in range(_ISSUE_UNROLL)]
            for r in range(_ISSUE_UNROLL):
                pltpu.make_async_copy(_token_tile(src_blk, tok0 + r), _token_tile(xs_hbm, dsts[r]),
                                      out_sem.at[slot]).start(priority=r % 2)
            return carry

        lax.fori_loop(0, tokens // _ISSUE_UNROLL, issue, 0)

    @pl.when(i == last)
    def _():
        if n_steps > 1:
            wait_copies((i + _DISPATCH_RING - 1) % _DISPATCH_RING)
        wait_copies(slot)
        for_each_pad(lambda cp: cp.wait())


def _dispatch(dest, pad_start, pad_len, n_used, hf_tiles, n_rows, tokens_per_step):
    n = dest.shape[0] // TOP_K
    pairs = tokens_per_step * TOP_K
    assert n % tokens_per_step == 0 and tokens_per_step % _ISSUE_UNROLL == 0
    smem = lambda: pl.BlockSpec(memory_space=pltpu.SMEM)
    n_steps = n // tokens_per_step
    return pl.pallas_call(
        functools.partial(_dispatch_kernel, n_steps=n_steps),
        grid=(n_steps,),
        in_specs=[pl.BlockSpec((pairs,), lambda i: (i,), memory_space=pltpu.SMEM), smem(), smem(), smem(),
                  pl.BlockSpec(memory_space=pl.ANY)],
        out_specs=pl.BlockSpec(memory_space=pl.ANY),
        out_shape=jax.ShapeDtypeStruct((n_rows * TILE_ROWS, LANES), F32),
        scratch_shapes=[pltpu.VMEM((EXPERT_ROWS * TILE_ROWS, LANES), F32),
                        pltpu.VMEM((_DISPATCH_RING, tokens_per_step * TILE_ROWS, LANES), F32),
                        pltpu.SemaphoreType.DMA((_DISPATCH_RING,)), pltpu.SemaphoreType.DMA((_DISPATCH_RING,)),
                        pltpu.SemaphoreType.DMA],
        compiler_params=pltpu.CompilerParams(
            dimension_semantics=("arbitrary",), vmem_limit_bytes=VMEM_LIMIT),
        name="dispatch",
    )(dest, pad_start, pad_len, n_used, hf_tiles)


def _expert_kernel(be_ref, nu_ref, nxt_ref, val_ref, xs_ref, wu_hbm, bu_ref, wd_hbm, bd_ref, y_ref,
                   wu_f32, wd_f32, wu_bf, wd_bf, wsem):
    i = pl.program_id(0)

    def mlp(xb):
        up = _dot(xb, wu_bf[...]) + bu_ref[0]
        x_glu = jnp.minimum(up[:, :D_FF], SWIGLU_LIMIT)
        x_lin = jnp.clip(up[:, D_FF:], -SWIGLU_LIMIT, SWIGLU_LIMIT)
        act = x_glu * jax.nn.sigmoid(SWIGLU_ALPHA * x_glu) * (x_lin + 1.0)
        return _dot(act.astype(BF16), wd_bf[...]) + bd_ref[0]

    def weight_copies(expert):
        return (pltpu.make_async_copy(wu_hbm.at[expert], wu_f32, wsem.at[0]),
                pltpu.make_async_copy(wd_hbm.at[expert], wd_f32, wsem.at[1]))

    @pl.when(i == 0)
    def _():
        for cp in weight_copies(be_ref[0]):
            cp.start()

    first = (i == 0) | (be_ref[i] != be_ref[jnp.maximum(i - 1, 0)])
    used = i < nu_ref[0]
    full = used & (val_ref[i] > EXPERT_ROWS - EXPERT_SUB_ROWS)

    def take_weights():
        for cp in weight_copies(be_ref[i]):
            cp.wait()
        wu_bf[...] = wu_f32[...].astype(BF16)
        wd_bf[...] = wd_f32[...].astype(BF16)

    def fetch_next_weights():
        @pl.when(nxt_ref[i] >= 0)
        def _():
            for cp in weight_copies(nxt_ref[i]):
                cp.start(priority=1)

    def whole_block():
        _store_token_tiles(y_ref, mlp(_load_token_tiles(xs_ref, 0, EXPERT_ROWS).astype(BF16)))

    @pl.when(first & full)
    def _():
        take_weights()
        whole_block()
        fetch_next_weights()

    @pl.when(first & jnp.logical_not(full))
    def _():
        take_weights()
        fetch_next_weights()

    @pl.when(full & jnp.logical_not(first))
    def _():
        whole_block()

    @pl.when(used & jnp.logical_not(full))
    def _():
        sub = EXPERT_SUB_ROWS
        n_sub = lax.shift_right_logical(val_ref[i] + (sub - 1), sub.bit_length() - 1)

        def compute(j, carry):
            row0 = pl.multiple_of(j * sub, sub)
            _store_token_tiles(y_ref, mlp(_load_token_tiles(xs_ref, row0, sub).astype(BF16)), row0)
            return carry

        def clear(j, carry):
            row0 = pl.multiple_of(j * sub * TILE_ROWS, sub * TILE_ROWS)
            y_ref[pl.ds(row0, sub * TILE_ROWS), :] = jnp.zeros((sub * TILE_ROWS, LANES), F32)
            return carry

        lax.fori_loop(0, n_sub, compute, 0)
        lax.fori_loop(n_sub, EXPERT_ROWS // sub, clear, 0)

    @pl.when(jnp.logical_not(used))
    def _():
        y_ref[...] = jnp.zeros_like(y_ref)


def _experts(block_expert, n_used, next_expert, valid_rows, xs_tiles, w_up, b_up, w_down, b_down):
    n_blocks = block_expert.shape[0]
    blk = (EXPERT_ROWS * TILE_ROWS, LANES)
    grid_spec = pltpu.PrefetchScalarGridSpec(
        num_scalar_prefetch=4,
        grid=(n_blocks,),
        in_specs=[
            pl.BlockSpec(blk, lambda i, be, nu, *_: (jnp.minimum(i, nu[0] - 1), 0)),
            pl.BlockSpec(memory_space=pl.ANY),
            pl.BlockSpec((1, 1, 2 * D_FF), lambda i, be, *_: (be[i], 0, 0)),
            pl.BlockSpec(memory_space=pl.ANY),
            pl.BlockSpec((1, 1, D_MODEL), lambda i, be, *_: (be[i], 0, 0)),
        ],
        out_specs=pl.BlockSpec(blk, lambda i, *_: (i, 0)),
        scratch_shapes=[
            pltpu.VMEM((D_MODEL, 2 * D_FF), F32),
            pltpu.VMEM((D_FF, D_MODEL), F32),
            pltpu.VMEM((D_MODEL, 2 * D_FF), BF16),
            pltpu.VMEM((D_FF, D_MODEL), BF16),
            pltpu.SemaphoreType.DMA((2,)),
        ],
    )
    return pl.pallas_call(
        _expert_kernel,
        grid_spec=grid_spec,
        out_shape=jax.ShapeDtypeStruct((n_blocks * blk[0], LANES), F32),
        compiler_params=pltpu.CompilerParams(
            dimension_semantics=("arbitrary",), vmem_limit_bytes=VMEM_LIMIT),
        name="experts",
    )(block_expert, n_used, next_expert, valid_rows, xs_tiles,
      w_up, b_up[:, None, :], w_down, b_down[:, None, :])


_COMBINE_RING = 3


def _combine_kernel(dst0_ref, dst1_ref, dst2_ref, y_hbm, h1_ref, route_ref, g_ref, o_ref, *scratch):
    bufs, sem = scratch[:_COMBINE_RING], scratch[_COMBINE_RING]
    i = pl.program_id(0)
    last = pl.num_programs(0) - 1
    tm = h1_ref.shape[0]
    rows = TOP_K * tm

    def gather_group(idx_ref, s, row0):
        srcs = [idx_ref[row0 + r] for r in range(_ISSUE_UNROLL)]
        for r in range(_ISSUE_UNROLL):
            pltpu.make_async_copy(_token_tile(y_hbm, srcs[r]), _token_tile(bufs[s], row0 + r),
                                  sem.at[s]).start(priority=r % 2)

    @pl.when(i == 0)
    def _():
        for s, idx_ref in ((0, dst0_ref), (1, dst1_ref)):
            def body(o, carry, s=s, idx_ref=idx_ref):
                gather_group(idx_ref, s, pl.multiple_of(o * _ISSUE_UNROLL, _ISSUE_UNROLL))
                return carry
            lax.fori_loop(0, rows // _ISSUE_UNROLL, body, 0)

    def step(s):
        cur = bufs[s]
        ahead = (s + 2) % _COMBINE_RING
        _tiles_wait_copy(y_hbm, cur, rows, sem.at[s]).wait()
        for g in range(rows // _ISSUE_UNROLL):
            gather_group(dst2_ref, ahead, g * _ISSUE_UNROLL)
        assert tm == LANES
        rt = jnp.concatenate([route_ref[...], jnp.zeros((LANES - _ROUTE_ROWS, tm), F32)], axis=0).T
        acc = h1_ref[...]
        for k in range(TOP_K):
            gate = rt[:, _ROUTE_GATE + k:_ROUTE_GATE + k + 1]
            acc = acc + gate * _load_token_tiles(cur, k * tm, tm)
        ms = jnp.mean(acc * acc, axis=-1, keepdims=True)
        o_ref[...] = acc * lax.rsqrt(ms + RMS_EPS) * g_ref[...]

        @pl.when(i == last)
        def _():
            for t in ((s + 1) % _COMBINE_RING, ahead):
                _tiles_wait_copy(y_hbm, bufs[t], rows, sem.at[t]).wait()

    for s in range(_COMBINE_RING):
        pl.when(i % _COMBINE_RING == s)(functools.partial(step, s))


def _combine(dest_kmajor, y, h1, route, g, tm):
    n = h1.shape[0]
    n_tiles = n // tm
    idx_spec = lambda ahead: pl.BlockSpec(
        (TOP_K * tm,), lambda i: (jnp.minimum(i + ahead, n_tiles - 1),), memory_space=pltpu.SMEM)
    ring_buf = pltpu.VMEM((TOP_K * tm * TILE_ROWS, LANES), F32)
    return pl.pallas_call(
        _combine_kernel,
        grid=(n_tiles,),
        in_specs=[
            idx_spec(0), idx_spec(1), idx_spec(2),
            pl.BlockSpec(memory_space=pl.ANY),
            pl.BlockSpec((tm, D_MODEL), lambda i: (i, 0)),
            pl.BlockSpec((_ROUTE_ROWS, tm), lambda i: (0, i)),
            pl.BlockSpec((1, D_MODEL), lambda i: (0, 0)),
        ],
        out_specs=pl.BlockSpec((tm, D_MODEL), lambda i: (i, 0)),
        out_shape=jax.ShapeDtypeStruct((n, D_MODEL), F32),
        scratch_shapes=[ring_buf] * _COMBINE_RING + [pltpu.SemaphoreType.DMA((_COMBINE_RING,))],
        compiler_params=pltpu.CompilerParams(
            dimension_semantics=("arbitrary",), vmem_limit_bytes=VMEM_LIMIT),
        name="combine",
    )(dest_kmajor, dest_kmajor, dest_kmajor, y, h1, route, g)


def _routing_tables(route, counts, n, tokens_dispatch, tokens_combine):
    tm = EXPERT_ROWS
    i32 = jnp.int32
    n_blocks = (n * TOP_K + N_EXPERTS * (tm - 1)) // tm
    idx = route[_ROUTE_IDX:_ROUTE_IDX + TOP_K].astype(i32)
    rank = route[_ROUTE_RANK:_ROUTE_RANK + TOP_K].astype(i32)
    cnt = counts[:, 0].astype(i32)
    eid = jnp.arange(N_EXPERTS, dtype=i32)
    upto = eid[None, :] <= eid[:, None]
    blocks_e = (cnt + tm - 1) // tm
    blocks_end = jnp.sum(jnp.where(upto, blocks_e[None, :], 0), axis=1)
    row_start = (blocks_end - blocks_e) * tm
    n_used = blocks_end[N_EXPERTS - 1]
    used = blocks_e > 0

    def lookup(table, keys):
        hit = keys[None] == eid.reshape((N_EXPERTS,) + (1,) * keys.ndim)
        return jnp.sum(jnp.where(hit, table.reshape((N_EXPERTS,) + (1,) * keys.ndim), 0), axis=0)

    dest = lookup(row_start, idx) + rank
    blk = jnp.arange(n_blocks, dtype=i32)
    last_used = jnp.max(jnp.where(used, eid, 0))
    be = jnp.where(blk < n_used, jnp.sum((blocks_end[None, :] <= blk[:, None]).astype(i32), axis=1), last_used)
    later_used = used[None, :] & (eid[None, :] > eid[:, None])
    after = jnp.min(jnp.where(later_used, eid[None, :], N_EXPERTS), axis=1)
    next_e = jnp.where(after < N_EXPERTS, after, -1)

    def k_major(tokens):
        return dest.reshape(TOP_K, n // tokens, tokens).transpose(1, 0, 2).reshape(-1)

    valid = jnp.clip(lookup(row_start + cnt, be) - blk * tm, 0, tm)
    return (be.astype(i32), n_used.reshape(1), lookup(next_e, be).astype(i32), valid.astype(i32),
            k_major(tokens_dispatch), k_major(tokens_combine), row_start + cnt, blocks_e * tm - cnt,
            n_blocks * tm)


def kernel(x, meta_tokens, mix_norm_g, w_in, ssm_a_re, ssm_a_im, ssm_log_dt, ssm_b_re, ssm_b_im,
           ssm_c_re, ssm_c_im, ssm_d, w_ssm_glu, attn_sinks, w_attn_o, w_out, ffn_norm_g,
           router_w, router_b, w_up, b_up, w_down, b_down, final_norm_g):
    bsz, seq, d = x.shape
    assert d == D_MODEL and seq % max(WINDOW, SSM_CHUNK) == 0
    assert mix_norm_g.shape[0] == 1, "single-layer trunk"
    n = bsz * seq
    tm_proj = min(1024, n)
    tm_mix = min(1024, n)
    tm_comb = min(128, n)
    x2 = x.reshape(n, D_MODEL)

    w_in_bf = w_in[0].astype(BF16)
    g_mix = mix_norm_g[0][None, :]
    u, q, kv, gs, ga = _in_proj(x2, g_mix, w_in_bf, tm_proj, pitched=True)
    u_m, _, kv_m, _, _ = _in_proj(meta_tokens, g_mix, w_in_bf, N_META, pitched=False)

    ssm_par = _ssm_params(ssm_a_re[0], ssm_a_im[0], ssm_log_dt[0], ssm_b_re[0], ssm_b_im[0],
                          ssm_c_re[0], ssm_c_im[0], ssm_d[0], SSM_CHUNK)
    y_ssm = _ssm(u, u_m, *ssm_par, batch=bsz, chunk=SSM_CHUNK)

    attn = _attention(attn_sinks[0], q, kv, kv_m, seq // WINDOW)

    h1, hf, route, counts = _mix(
        x2, y_ssm, attn, gs, ga, w_ssm_glu[0].astype(BF16), w_attn_o[0].astype(BF16),
        w_out[0].astype(BF16), ffn_norm_g[0][None, :], router_w[0], router_b[0][None, :], tm_mix)

    tok_disp = min(1024, n)
    be, n_used, next_e, valid, dest_disp, dest_comb, pad_start, pad_len, n_rows = _routing_tables(
        route, counts, n, tok_disp, tm_comb)
    xs = _dispatch(dest_disp, pad_start, pad_len, n_used, hf, n_rows, tok_disp)
    y = _experts(be, n_used, next_e, valid, xs, w_up[0], b_up[0], w_down[0], b_down[0])
    out = _combine(dest_comb, y, h1, route, final_norm_g[None, :], tm_comb)
    return out.reshape(bsz, seq, D_MODEL)
```

```python
import functools
import math

import jax
import jax.numpy as jnp
from jax import lax
from jax.experimental import pallas as pl
from jax.experimental.pallas import tpu as pltpu

F32 = jnp.float32
BF16 = jnp.bfloat16

D_MODEL = 1024
N_META = 16
SSM_WIDTH = 512
SSM_GROUP = 16
SSM_GROUPS = 32
SSM_STATE = 64
HEAD_DIM = 64
N_HEADS = 16
N_KV_HEADS = 2
KV_REP = N_HEADS // N_KV_HEADS
WINDOW = 128
Q_WIDTH = N_HEADS * HEAD_DIM
KV_WIDTH = N_KV_HEADS * HEAD_DIM
N_EXPERTS = 32
TOP_K = 4
D_FF = 1024
SWIGLU_ALPHA = 1.702
SWIGLU_LIMIT = 7.0
RMS_EPS = 1e-5
NEG_INF = -1e30

_U0, _Q0, _KV0, _GS0, _GA0, _IN_END = 0, 512, 1536, 1792, 2816, 3840

SSM_CH_BLOCK = 128
SSM_HALF = (SSM_CH_BLOCK // SSM_GROUP) * SSM_STATE
SSM_CHUNK = 32
SSM_PITCH = 40
SSM_UNROLL_A = 16
SSM_UNROLL_B = 4
SSM_CHUNK_UNROLL = 4
LOG2_E = math.log2(math.e)
EXPERT_ROWS = 512
EXPERT_SUB_ROWS = 128
VMEM_LIMIT = 56 * 1024 * 1024


def _dot(a, b):
    return jnp.dot(a, b, preferred_element_type=F32)


def _dot_nt(a, b):
    return lax.dot_general(a, b, (((1,), (1,)), ((), ())), preferred_element_type=F32)


LANES = 128
TILE_ROWS = D_MODEL // LANES


def _store_token_tiles(ref, x, start_row=0):
    rows = x.shape[0]
    for j in range(TILE_ROWS):
        ref[pl.ds(start_row * TILE_ROWS + j, rows, stride=TILE_ROWS), :] = x[:, j * LANES:(j + 1) * LANES]


def _load_token_tiles(ref, start_row, rows):
    return jnp.concatenate(
        [ref[pl.ds(start_row * TILE_ROWS + j, rows, stride=TILE_ROWS), :] for j in range(TILE_ROWS)], axis=1)


def _token_tile(ref, row):
    return ref.at[pl.ds(pl.multiple_of(row * TILE_ROWS, TILE_ROWS), TILE_ROWS), :]


def _in_proj_kernel(x_ref, g_ref, w_ref, u_ref, q_ref, kv_ref, gs_ref, ga_ref, *, pitched):
    x = x_ref[...]
    ms = jnp.mean(x * x, axis=-1, keepdims=True)
    hn = (x * lax.rsqrt(ms + RMS_EPS) * g_ref[...]).astype(BF16)
    u = _dot(hn, w_ref[:, _U0:_Q0])
    if pitched:
        for c in range(u.shape[0] // SSM_CHUNK):
            u_ref[c * SSM_PITCH:c * SSM_PITCH + SSM_CHUNK, :] = u[c * SSM_CHUNK:(c + 1) * SSM_CHUNK, :]
            u_ref[c * SSM_PITCH + SSM_CHUNK:(c + 1) * SSM_PITCH, :] = jnp.zeros(
                (SSM_PITCH - SSM_CHUNK, SSM_WIDTH), F32)
    else:
        u_ref[...] = u
    q_ref[...] = (_dot(hn, w_ref[:, _Q0:_KV0]) * (HEAD_DIM ** -0.5 * LOG2_E)).astype(BF16)
    kv_ref[...] = _dot(hn, w_ref[:, _KV0:_GS0]).astype(BF16)
    gs_ref[...] = jax.nn.sigmoid(_dot(hn, w_ref[:, _GS0:_GA0])).astype(BF16)
    ga_ref[...] = jax.nn.sigmoid(_dot(hn, w_ref[:, _GA0:_IN_END])).astype(BF16)


def _in_proj(x2, g, w_bf, tm, pitched):
    n = x2.shape[0]
    row = lambda w: pl.BlockSpec((tm, w), lambda i: (i, 0))
    full = lambda a: pl.BlockSpec(a.shape, lambda i: (0,) * a.ndim)
    u_rows = (lambda r: r // SSM_CHUNK * SSM_PITCH) if pitched else (lambda r: r)
    return pl.pallas_call(
        functools.partial(_in_proj_kernel, pitched=pitched),
        grid=(n // tm,),
        in_specs=[row(D_MODEL), full(g), full(w_bf)],
        out_specs=[pl.BlockSpec((u_rows(tm), SSM_WIDTH), lambda i: (i, 0)),
                   row(Q_WIDTH), row(2 * KV_WIDTH), row(D_MODEL), row(D_MODEL)],
        out_shape=[
            jax.ShapeDtypeStruct((u_rows(n), SSM_WIDTH), F32),
            jax.ShapeDtypeStruct((n, Q_WIDTH), BF16),
            jax.ShapeDtypeStruct((n, 2 * KV_WIDTH), BF16),
            jax.ShapeDtypeStruct((n, D_MODEL), BF16),
            jax.ShapeDtypeStruct((n, D_MODEL), BF16),
        ],
        compiler_params=pltpu.CompilerParams(
            dimension_semantics=("arbitrary",), vmem_limit_bytes=VMEM_LIMIT),
        name="in_proj",
    )(x2, g, w_bf)


def _meta_proj_kernel(x_ref, g_ref, w_ref, wbf_ref, u_ref, kv_ref):
    wbf_ref[...] = w_ref[...].astype(BF16)
    x = x_ref[...]
    ms = jnp.mean(x * x, axis=-1, keepdims=True)
    hn = (x * lax.rsqrt(ms + RMS_EPS) * g_ref[...]).astype(BF16)
    u_ref[...] = _dot(hn, wbf_ref[:, _U0:_Q0])
    kv_ref[...] = _dot(hn, wbf_ref[:, _KV0:_GS0]).astype(BF16)


def _meta_proj(meta, g, w):
    full = lambda a: pl.BlockSpec(a.shape, lambda i: (0,) * a.ndim)
    shapes = [jax.ShapeDtypeStruct(w.shape, BF16), jax.ShapeDtypeStruct((N_META, SSM_WIDTH), F32),
              jax.ShapeDtypeStruct((N_META, 2 * KV_WIDTH), BF16)]
    return pl.pallas_call(
        _meta_proj_kernel,
        grid=(1,),
        in_specs=[full(meta), full(g), pl.BlockSpec(w.shape, lambda i: (0, 0), pipeline_mode=pl.Buffered(1))],
        out_specs=[full(s) for s in shapes],
        out_shape=shapes,
        compiler_params=pltpu.CompilerParams(
            dimension_semantics=("arbitrary",), vmem_limit_bytes=VMEM_LIMIT),
        name="meta_proj",
    )(meta, g, w)


def _ssm_kernel(u_ref, um_ref, bm_ref, cm_ref, ar_ref, ai_ref, atr_ref, ati_ref, d_ref,
                y_ref, sre, sim, *, chunk, rows, batch):
    h = SSM_HALF
    bm = bm_ref[0]
    cm = cm_ref[0]
    ar, ai = ar_ref[0], ai_ref[0]
    atr, ati = atr_ref[0], ati_ref[0]
    dsk = d_ref[0]
    n_chunks = rows // batch

    def advance(sr, si, bu):
        return ar * sr - ai * si + bu[:, :h], ar * si + ai * sr + bu[:, h:]

    bum = _dot(um_ref[...].astype(BF16), bm)
    mr = jnp.zeros((1, h), F32)
    mi = jnp.zeros((1, h), F32)
    for j in range(N_META):
        mr, mi = advance(mr, mi, bum[j:j + 1, :])

    def u_step(t):
        return u_ref[pl.ds(t, rows, stride=SSM_PITCH), :]

    for pad in range(chunk, SSM_PITCH):
        y_ref[pl.ds(pad, rows, stride=SSM_PITCH), :] = jnp.zeros((rows, SSM_CH_BLOCK), F32)

    sre[...] = jnp.zeros_like(sre)
    sim[...] = jnp.zeros_like(sim)

    def pass_a(o, carry):
        for k in range(SSM_UNROLL_A):
            t = o * SSM_UNROLL_A + k
            bu = _dot(u_step(t).astype(BF16), bm)
            nr, ni = advance(sre[...], sim[...], bu)
            sre[...] = nr
            sim[...] = ni
        return carry

    lax.fori_loop(0, chunk // SSM_UNROLL_A, pass_a, 0)

    def over_chunks(o, carry):
        rows_of = [[pl.ds(b * n_chunks + o * SSM_CHUNK_UNROLL + k, 1) for b in range(batch)]
                   for k in range(SSM_CHUNK_UNROLL)]
        ends = [[(sre[row, :], sim[row, :]) for row in rows_k] for rows_k in rows_of]
        carry = list(carry)
        for k in range(SSM_CHUNK_UNROLL):
            for b in range(batch):
                cr, ci = carry[2 * b], carry[2 * b + 1]
                er, ei = ends[k][b]
                sre[rows_of[k][b], :] = cr
                sim[rows_of[k][b], :] = ci
                carry[2 * b], carry[2 * b + 1] = atr * cr - ati * ci + er, atr * ci + ati * cr + ei
        return tuple(carry)

    assert n_chunks % SSM_CHUNK_UNROLL == 0
    lax.fori_loop(0, n_chunks // SSM_CHUNK_UNROLL, over_chunks, (mr, mi) * batch)

    def pass_b(o, carry):
        for k in range(SSM_UNROLL_B):
            t = o * SSM_UNROLL_B + k
            ut = u_step(t)
            bu = _dot(ut.astype(BF16), bm)
            nr, ni = advance(sre[...], sim[...], bu)
            sre[...] = nr
            sim[...] = ni
            y = _dot(nr.astype(BF16), cm[:h, :]) + _dot(ni.astype(BF16), cm[h:, :]) + dsk * ut
            y_ref[pl.ds(t, rows, stride=SSM_PITCH), :] = y
        return carry

    lax.fori_loop(0, chunk // SSM_UNROLL_B, pass_b, 0)


def _ssm(u, u_meta, bmat, cmat, a_re, a_im, at_re, at_im, dskip, batch, chunk):
    n = u.shape[0]
    rows = n // SSM_PITCH
    nblk = SSM_WIDTH // SSM_CH_BLOCK
    col = lambda r: pl.BlockSpec((r, SSM_CH_BLOCK), lambda j: (0, j))
    par = lambda a: pl.BlockSpec((1,) + a.shape[1:], lambda j: (j, 0, 0))
    return pl.pallas_call(
        functools.partial(_ssm_kernel, chunk=chunk, rows=rows, batch=batch),
        grid=(nblk,),
        in_specs=[col(n), col(N_META), par(bmat), par(cmat), par(a_re), par(a_im),
                  par(at_re), par(at_im), par(dskip)],
        out_specs=col(n),
        out_shape=jax.ShapeDtypeStruct((n, SSM_WIDTH), F32),
        scratch_shapes=[pltpu.VMEM((rows, SSM_HALF), F32), pltpu.VMEM((rows, SSM_HALF), F32)],
        compiler_params=pltpu.CompilerParams(
            dimension_semantics=("arbitrary",), vmem_limit_bytes=VMEM_LIMIT),
        name="ssm",
    )(u, u_meta, bmat, cmat, a_re, a_im, at_re, at_im, dskip)


def _ssm_params(a_re, a_im, log_dt, b_re, b_im, c_re, c_im, d_skip, chunk):
    dt = jnp.exp(log_dt)[:, None]
    mag = jnp.exp(a_re * dt)
    ang = a_im * dt
    abar_re, abar_im = mag * jnp.cos(ang), mag * jnp.sin(ang)
    den = a_re * a_re + a_im * a_im
    nr, ni = abar_re - 1.0, abar_im
    coef_re = ((nr * a_re + ni * a_im) / den)[..., None]
    coef_im = ((ni * a_re - nr * a_im) / den)[..., None]
    bbar_re = coef_re * b_re - coef_im * b_im
    bbar_im = coef_re * b_im + coef_im * b_re
    magt = jnp.exp(a_re * dt * chunk)
    at_re, at_im = magt * jnp.cos(ang * chunk), magt * jnp.sin(ang * chunk)

    nblk = SSM_WIDTH // SSM_CH_BLOCK
    gpb = SSM_GROUPS // nblk
    eye = jnp.eye(gpb, dtype=F32)

    def in_map(b):
        b = b.reshape(nblk, gpb, SSM_STATE, SSM_GROUP)
        return jnp.einsum('jgpc,gh->jgchp', b, eye).reshape(nblk, SSM_CH_BLOCK, gpb * SSM_STATE)

    def out_map(c):
        c = c.reshape(nblk, gpb, SSM_GROUP, SSM_STATE)
        return jnp.einsum('jgcp,gh->jgphc', c, eye).reshape(nblk, gpb * SSM_STATE, SSM_CH_BLOCK)

    bmat = jnp.concatenate([in_map(bbar_re), in_map(bbar_im)], axis=2).astype(BF16)
    cmat = jnp.concatenate([out_map(c_re), -out_map(c_im)], axis=1).astype(BF16)
    vec = lambda v: v.reshape(nblk, 1, SSM_HALF)
    return (bmat, cmat, vec(abar_re), vec(abar_im), vec(at_re), vec(at_im),
            d_skip.reshape(nblk, 1, SSM_CH_BLOCK))


ATTN_STEP_BLOCKS = 16


def _attn_kernel(sink_ref, q_ref, kvc_ref, kvp_ref, kvm_ref, o_ref, *, blocks_per_seq, step_blocks):
    for sb in range(step_blocks):
        rows = pl.ds(sb * WINDOW, WINDOW)
        prev = kvp_ref if sb == 0 else kvc_ref.at[pl.ds((sb - 1) * WINDOW, WINDOW), :]
        _attn_block(sink_ref, q_ref.at[rows, :], kvc_ref.at[rows, :], prev, kvm_ref, o_ref.at[rows, :],
                    (pl.program_id(0) * step_blocks + sb) % blocks_per_seq)


def _attn_block(sink_ref, q_ref, kvc_ref, kvp_ref, kvm_ref, o_ref, n):
    w = WINDOW
    hd = HEAD_DIM
    qi = lax.broadcasted_iota(jnp.int32, (w, w), 0)
    lane = lax.broadcasted_iota(jnp.int32, (w, w), 1)
    vis_prev = (lane > qi) & (n > 0)
    vis_cur = lane <= qi
    left = lane < hd
    meta_l = lane < N_META
    meta_r = (lane >= N_META) & (lane < 2 * N_META)

    def placed(x_bf):
        x = x_bf.astype(F32)
        xr = pltpu.roll(x, hd, 1)
        lm = lax.broadcasted_iota(jnp.int32, x.shape, 1) < hd
        z = jnp.zeros_like(x)
        return {(0, 0): jnp.where(lm, x, z), (0, 1): jnp.where(lm, z, xr),
                (1, 0): jnp.where(lm, xr, z), (1, 1): jnp.where(lm, z, x)}

    kp, kc, km = placed(kvp_ref[:, :KV_WIDTH]), placed(kvc_ref[:, :KV_WIDTH]), placed(kvm_ref[:, :KV_WIDTH])
    vp, vc, vm = placed(kvp_ref[:, KV_WIDTH:]), placed(kvc_ref[:, KV_WIDTH:]), placed(kvm_ref[:, KV_WIDTH:])
    pad_rows = w - 2 * N_META
    zpad = jnp.zeros((pad_rows, w), F32)
    krow = lax.broadcasted_iota(jnp.int32, (5 * w, w), 0)
    klane = lax.broadcasted_iota(jnp.int32, (5 * w, w), 1)
    row_l = (krow < 2 * w) | ((krow >= 4 * w) & (krow < 4 * w + N_META))
    row_r = ((krow >= 2 * w) & (krow < 4 * w)) | ((krow >= 4 * w + N_META) & (krow < 4 * w + 2 * N_META))
    den_cols = jnp.where((row_l & (klane < hd)) | (row_r & (klane >= hd)), 1.0, 0.0)

    for j in range(N_KV_HEADS):
        kcat = jnp.concatenate(
            [kp[j, 0], kc[j, 0], kp[j, 1], kc[j, 1], km[j, 0], km[j, 1], zpad], axis=0).astype(BF16)
        vcat = jnp.concatenate(
            [jnp.concatenate([vp[j, 0], vc[j, 0], vp[j, 1], vc[j, 1], vm[j, 0], vm[j, 1], zpad], axis=0),
             den_cols], axis=1).astype(BF16)
        for r in range(KV_REP // 2):
            pr = j * (KV_REP // 2) + r
            s = _dot_nt(q_ref[:, pr * w:(pr + 1) * w], kcat)
            s_l = jnp.where(vis_cur, s[:, w:2 * w], jnp.where(vis_prev, s[:, 0:w], NEG_INF))
            s_r = jnp.where(vis_cur, s[:, 3 * w:4 * w], jnp.where(vis_prev, s[:, 2 * w:3 * w], NEG_INF))
            s_m = s[:, 4 * w:]
            sink_l, sink_r = sink_ref[2 * pr] * LOG2_E, sink_ref[2 * pr + 1] * LOG2_E
            m_l = jnp.maximum(jnp.max(jnp.maximum(s_l, jnp.where(meta_l, s_m, NEG_INF)),
                                      axis=1, keepdims=True), sink_l)
            m_r = jnp.maximum(jnp.max(jnp.maximum(s_r, jnp.where(meta_r, s_m, NEG_INF)),
                                      axis=1, keepdims=True), sink_r)
            s_m = jnp.where(meta_l, s_m - m_l, jnp.where(meta_r, s_m - m_r, NEG_INF))
            e_l, e_r = jnp.exp2(s_l - m_l), jnp.exp2(s_r - m_r)
            e = jnp.concatenate([jnp.where(vis_cur, 0.0, e_l), jnp.where(vis_cur, e_l, 0.0),
                                 jnp.where(vis_cur, 0.0, e_r), jnp.where(vis_cur, e_r, 0.0),
                                 jnp.exp2(s_m)], axis=1).astype(BF16)
            acc = _dot(e, vcat)
            den = acc[:, w:] + jnp.where(left, jnp.exp2(sink_l - m_l), jnp.exp2(sink_r - m_r))
            o_ref[:, pr * w:(pr + 1) * w] = (acc[:, :w] / den).astype(BF16)


def _attention(sinks, q, kv, kv_meta, blocks_per_seq):
    n = q.shape[0]
    step_blocks = math.gcd(ATTN_STEP_BLOCKS, blocks_per_seq)
    step = step_blocks * WINDOW
    return pl.pallas_call(
        functools.partial(_attn_kernel, blocks_per_seq=blocks_per_seq, step_blocks=step_blocks),
        grid=(n // step,),
        in_specs=[
            pl.BlockSpec(memory_space=pltpu.SMEM),
            pl.BlockSpec((step, Q_WIDTH), lambda g: (g, 0)),
            pl.BlockSpec((step, 2 * KV_WIDTH), lambda g: (g, 0)),
            pl.BlockSpec((WINDOW, 2 * KV_WIDTH), lambda g: (jnp.maximum(g * step_blocks - 1, 0), 0)),
            pl.BlockSpec((N_META, 2 * KV_WIDTH), lambda g: (0, 0)),
        ],
        out_specs=pl.BlockSpec((step, Q_WIDTH), lambda g: (g, 0)),
        out_shape=jax.ShapeDtypeStruct((n, Q_WIDTH), BF16),
        compiler_params=pltpu.CompilerParams(dimension_semantics=("arbitrary",)),
        name="attn",
    )(sinks, q, kv, kv, kv_meta)


_ROUTE_IDX, _ROUTE_RANK, _ROUTE_GATE, _ROUTE_ROWS = 0, TOP_K, 2 * TOP_K, 16


def _mix_kernel(x_ref, y_ref, at_ref, gs_ref, ga_ref, wglu_ref, wo_ref, wout_ref, fg_ref, rw_ref, rb_ref,
                h1_ref, hf_ref, route_ref, cnt_ref, cnt_scr):
    tm = x_ref.shape[0]

    @pl.when(pl.program_id(0) == 0)
    def _():
        cnt_scr[...] = jnp.zeros_like(cnt_scr)

    y_ssm = jnp.concatenate([y_ref[c * SSM_PITCH:c * SSM_PITCH + SSM_CHUNK, :]
                             for c in range(tm // SSM_CHUNK)], axis=0)
    glu = _dot(jax.nn.gelu(y_ssm).astype(BF16), wglu_ref[...])
    branch_ssm = glu[:, :D_MODEL] * jax.nn.sigmoid(glu[:, D_MODEL:])
    branch_attn = _dot(at_ref[...], wo_ref[...])
    merged = gs_ref[...].astype(F32) * branch_ssm + ga_ref[...].astype(F32) * branch_attn
    h1 = x_ref[...] + _dot(merged.astype(BF16), wout_ref[...])
    h1_ref[...] = h1
    ms = jnp.mean(h1 * h1, axis=-1, keepdims=True)
    hf = h1 * lax.rsqrt(ms + RMS_EPS) * fg_ref[...]
    _store_token_tiles(hf_ref, hf)

    hf_hi = hf.astype(BF16)
    hf_lo = (hf - hf_hi.astype(F32)).astype(BF16)
    hi_prod = _dot(hf_hi, rw_ref[...])
    logits = (hi_prod[:, :LANES] + (hi_prod[:, LANES:] + _dot(hf_lo, rw_ref[:, :LANES]))
              + rb_ref[...])
    lt = logits.T[:N_EXPERTS, :]
    erow = lax.broadcasted_iota(jnp.int32, (N_EXPERTS, tm), 0)
    vals, idxs, hots = [], [], []
    rest = lt
    for _ in range(TOP_K):
        m = jnp.max(rest, axis=0, keepdims=True)
        first = jnp.min(jnp.where(rest == m, erow, N_EXPERTS), axis=0, keepdims=True)
        hot = erow == first
        vals.append(m)
        idxs.append(first)
        hots.append(hot)
        rest = jnp.where(hot, -jnp.inf, rest)
    exps = [jnp.exp(v - vals[0]) for v in vals]
    tot = exps[0] + exps[1] + exps[2] + exps[3]

    sel = (hots[0] | hots[1] | hots[2] | hots[3]).astype(F32)
    ti = lax.broadcasted_iota(jnp.int32, (tm, tm), 0)
    tj = lax.broadcasted_iota(jnp.int32, (tm, tm), 1)
    earlier = (ti < tj).astype(BF16)
    rank_e = _dot(sel.astype(BF16), earlier) + cnt_scr[...]
    cnt_scr[...] = cnt_scr[...] + jnp.sum(sel, axis=1, keepdims=True)
    cnt_ref[...] = cnt_scr[...]

    rrow = lax.broadcasted_iota(jnp.int32, (_ROUTE_ROWS, tm), 0)
    route = jnp.zeros((_ROUTE_ROWS, tm), F32)
    for k in range(TOP_K):
        rank_k = jnp.sum(jnp.where(hots[k], rank_e, 0.0), axis=0, keepdims=True)
        route = jnp.where(rrow == _ROUTE_IDX + k, idxs[k].astype(F32), route)
        route = jnp.where(rrow == _ROUTE_RANK + k, rank_k, route)
        route = jnp.where(rrow == _ROUTE_GATE + k, exps[k] / tot, route)
    route_ref[...] = route


def _mix(x2, y, attn, gs, ga, wglu, wo, wout, fg, rw, rb, tm):
    n = x2.shape[0]
    rw = jnp.pad(rw, ((0, 0), (0, LANES - N_EXPERTS)))
    rb = jnp.pad(rb, ((0, 0), (0, LANES - N_EXPERTS)), constant_values=NEG_INF)
    rw_hi = rw.astype(BF16)
    rw_split = jnp.concatenate([rw_hi, (rw - rw_hi.astype(F32)).astype(BF16)], axis=1)
    row = lambda w: pl.BlockSpec((tm, w), lambda i: (i, 0))
    full = lambda a: pl.BlockSpec(a.shape, lambda i: (0,) * a.ndim)
    return pl.pallas_call(
        _mix_kernel,
        grid=(n // tm,),
        in_specs=[row(D_MODEL), pl.BlockSpec((tm // SSM_CHUNK * SSM_PITCH, SSM_WIDTH), lambda i: (i, 0)),
                  row(Q_WIDTH), row(D_MODEL), row(D_MODEL),
                  full(wglu), full(wo), full(wout), full(fg), full(rw_split), full(rb)],
        out_specs=[row(D_MODEL), pl.BlockSpec((tm * TILE_ROWS, LANES), lambda i: (i, 0)),
                   pl.BlockSpec((_ROUTE_ROWS, tm), lambda i: (0, i)),
                   pl.BlockSpec((N_EXPERTS, 1), lambda i: (0, 0))],
        out_shape=[
            jax.ShapeDtypeStruct((n, D_MODEL), F32),
            jax.ShapeDtypeStruct((n * TILE_ROWS, LANES), F32),
            jax.ShapeDtypeStruct((_ROUTE_ROWS, n), F32),
            jax.ShapeDtypeStruct((N_EXPERTS, 1), F32),
        ],
        scratch_shapes=[pltpu.VMEM((N_EXPERTS, 1), F32)],
        compiler_params=pltpu.CompilerParams(
            dimension_semantics=("arbitrary",), vmem_limit_bytes=VMEM_LIMIT),
        name="mix_router",
    )(x2, y, attn, gs, ga, wglu, wo, wout, fg, rw_split, rb)


def _tiles_wait_copy(src_hbm, dst, n_tiles, sem):
    rows = n_tiles * TILE_ROWS
    return pltpu.make_async_copy(src_hbm.at[pl.ds(0, rows), :], dst.at[pl.ds(0, rows), :], sem)


_ISSUE_UNROLL = 16


_DISPATCH_RING = 4


def _dispatch_kernel(dst_ref, pad_start_ref, pad_len_ref, nu_ref, hf_hbm, xs_hbm, zero_blk, ring, in_sem,
                     out_sem, pad_sem, *, n_steps):
    i = pl.program_id(0)
    last = n_steps - 1
    pairs = dst_ref.shape[0]
    tokens = pairs // TOP_K
    blk_rows = tokens * TILE_ROWS
    n_blocks = xs_hbm.shape[0] // (EXPERT_ROWS * TILE_ROWS)
    slot = i % _DISPATCH_RING

    def load(step, s):
        src = hf_hbm.at[pl.ds(pl.multiple_of(step * blk_rows, blk_rows), blk_rows), :]
        return pltpu.make_async_copy(src, ring.at[s], in_sem.at[s])

    def wait_copies(s):
        for _ in range(TOP_K):
            pltpu.make_async_copy(ring.at[s], xs_hbm.at[pl.ds(0, blk_rows), :], out_sem.at[s]).wait()

    def zero_copy(row, rows):
        src = zero_blk.at[pl.ds(0, rows * TILE_ROWS), :]
        dst = xs_hbm.at[pl.ds(pl.multiple_of(row * TILE_ROWS, TILE_ROWS), rows * TILE_ROWS), :]
        return pltpu.make_async_copy(src, dst, pad_sem)

    def for_each_pad(fn):
        def per_expert(e, carry):
            row, left = pad_start_ref[e], pad_len_ref[e]
            size = EXPERT_ROWS // 2
            while size >= 1:
                take = left & size

                @pl.when(take != 0)
                def _(row=row, size=size):
                    fn(zero_copy(row, size))

                row = row + take
                size //= 2
            return carry
        lax.fori_loop(0, N_EXPERTS, per_expert, 0)

        def per_block(b, carry):
            fn(zero_copy(b * EXPERT_ROWS, EXPERT_ROWS))
            return carry
        lax.fori_loop(nu_ref[0], n_blocks, per_block, 0)

    @pl.when(i == 0)
    def _():
        for s in range(min(2, n_steps)):
            load(s, s).start()
        zero_blk[...] = jnp.zeros_like(zero_blk)
        for_each_pad(lambda cp: cp.start())

    @pl.when(i >= 2)
    def _():
        wait_copies((i + 2) % _DISPATCH_RING)

    @pl.when(i + 2 < n_steps)
    def _():
        load(i + 2, (i + 2) % _DISPATCH_RING).start()

    load(i, slot).wait()
    src_blk = ring.at[slot]
    for k in range(TOP_K):
        def issue(o, carry, k=k):
            tok0 = pl.multiple_of(o * _ISSUE_UNROLL, _ISSUE_UNROLL)
            dsts = [dst_ref[k * tokens + tok0 + r] for r in range(_ISSUE_UNROLL)]
            for r in range(_ISSUE_UNROLL):
                pltpu.make_async_copy(_token_tile(src_blk, tok0 + r), _token_tile(xs_hbm, dsts[r]),
                                      out_sem.at[slot]).start(priority=r % 2)
            return carry

        lax.fori_loop(0, tokens // _ISSUE_UNROLL, issue, 0)

    @pl.when(i == last)
    def _():
        if n_steps > 1:
            wait_copies((i + _DISPATCH_RING - 1) % _DISPATCH_RING)
        wait_copies(slot)
        for_each_pad(lambda cp: cp.wait())


def _dispatch(dest, pad_start, pad_len, n_used, hf_tiles, n_rows, tokens_per_step):
    n = dest.shape[0] // TOP_K
    pairs = tokens_per_step * TOP_K
    assert n % tokens_per_step == 0 and tokens_per_step % _ISSUE_UNROLL == 0
    smem = lambda: pl.BlockSpec(memory_space=pltpu.SMEM)
    n_steps = n // tokens_per_step
    return pl.pallas_call(
        functools.partial(_dispatch_kernel, n_steps=n_steps),
        grid=(n_steps,),
        in_specs=[pl.BlockSpec((pairs,), lambda i: (i,), memory_space=pltpu.SMEM), smem(), smem(), smem(),
                  pl.BlockSpec(memory_space=pl.ANY)],
        out_specs=pl.BlockSpec(memory_space=pl.ANY),
        out_shape=jax.ShapeDtypeStruct((n_rows * TILE_ROWS, LANES), F32),
        scratch_shapes=[pltpu.VMEM((EXPERT_ROWS * TILE_ROWS, LANES), F32),
                        pltpu.VMEM((_DISPATCH_RING, tokens_per_step * TILE_ROWS, LANES), F32),
                        pltpu.SemaphoreType.DMA((_DISPATCH_RING,)), pltpu.SemaphoreType.DMA((_DISPATCH_RING,)),
                        pltpu.SemaphoreType.DMA],
        compiler_params=pltpu.CompilerParams(
            dimension_semantics=("arbitrary",), vmem_limit_bytes=VMEM_LIMIT),
        name="dispatch",
    )(dest, pad_start, pad_len, n_used, hf_tiles)


def _expert_kernel(be_ref, nu_ref, nxt_ref, val_ref, xs_ref, wu_hbm, bu_ref, wd_hbm, bd_ref, y_ref,
                   wu_f32, wd_f32, wu_bf, wd_bf, wsem):
    i = pl.program_id(0)

    def mlp(xb):
        up = _dot(xb, wu_bf[...]) + bu_ref[0]
        x_glu = jnp.minimum(up[:, :D_FF], SWIGLU_LIMIT)
        x_lin = jnp.clip(up[:, D_FF:], -SWIGLU_LIMIT, SWIGLU_LIMIT)
        act = x_glu * jax.nn.sigmoid(SWIGLU_ALPHA * x_glu) * (x_lin + 1.0)
        return _dot(act.astype(BF16), wd_bf[...]) + bd_ref[0]

    def weight_copies(expert):
        return (pltpu.make_async_copy(wu_hbm.at[expert], wu_f32, wsem.at[0]),
                pltpu.make_async_copy(wd_hbm.at[expert], wd_f32, wsem.at[1]))

    @pl.when(i == 0)
    def _():
        for cp in weight_copies(be_ref[0]):
            cp.start()

    first = (i == 0) | (be_ref[i] != be_ref[jnp.maximum(i - 1, 0)])
    used = i < nu_ref[0]
    full = used & (val_ref[i] > EXPERT_ROWS - EXPERT_SUB_ROWS)

    def take_weights():
        for cp in weight_copies(be_ref[i]):
            cp.wait()
        wu_bf[...] = wu_f32[...].astype(BF16)
        wd_bf[...] = wd_f32[...].astype(BF16)

    def fetch_next_weights():
        @pl.when(nxt_ref[i] >= 0)
        def _():
            for cp in weight_copies(nxt_ref[i]):
                cp.start(priority=1)

    def whole_block():
        _store_token_tiles(y_ref, mlp(_load_token_tiles(xs_ref, 0, EXPERT_ROWS).astype(BF16)))

    @pl.when(first & full)
    def _():
        take_weights()
        whole_block()
        fetch_next_weights()

    @pl.when(first & jnp.logical_not(full))
    def _():
        take_weights()
        fetch_next_weights()

    @pl.when(full & jnp.logical_not(first))
    def _():
        whole_block()

    @pl.when(used & jnp.logical_not(full))
    def _():
        sub = EXPERT_SUB_ROWS
        n_sub = lax.shift_right_logical(val_ref[i] + (sub - 1), sub.bit_length() - 1)

        def compute(j, carry):
            row0 = pl.multiple_of(j * sub, sub)
            _store_token_tiles(y_ref, mlp(_load_token_tiles(xs_ref, row0, sub).astype(BF16)), row0)
            return carry

        def clear(j, carry):
            row0 = pl.multiple_of(j * sub * TILE_ROWS, sub * TILE_ROWS)
            y_ref[pl.ds(row0, sub * TILE_ROWS), :] = jnp.zeros((sub * TILE_ROWS, LANES), F32)
            return carry

        lax.fori_loop(0, n_sub, compute, 0)
        lax.fori_loop(n_sub, EXPERT_ROWS // sub, clear, 0)

    @pl.when(jnp.logical_not(used))
    def _():
        y_ref[...] = jnp.zeros_like(y_ref)


def _experts(block_expert, n_used, next_expert, valid_rows, xs_tiles, w_up, b_up, w_down, b_down):
    n_blocks = block_expert.shape[0]
    blk = (EXPERT_ROWS * TILE_ROWS, LANES)
    grid_spec = pltpu.PrefetchScalarGridSpec(
        num_scalar_prefetch=4,
        grid=(n_blocks,),
        in_specs=[
            pl.BlockSpec(blk, lambda i, be, nu, *_: (jnp.minimum(i, nu[0] - 1), 0)),
            pl.BlockSpec(memory_space=pl.ANY),
            pl.BlockSpec((1, 1, 2 * D_FF), lambda i, be, *_: (be[i], 0, 0)),
            pl.BlockSpec(memory_space=pl.ANY),
            pl.BlockSpec((1, 1, D_MODEL), lambda i, be, *_: (be[i], 0, 0)),
        ],
        out_specs=pl.BlockSpec(blk, lambda i, *_: (i, 0)),
        scratch_shapes=[
            pltpu.VMEM((D_MODEL, 2 * D_FF), F32),
            pltpu.VMEM((D_FF, D_MODEL), F32),
            pltpu.VMEM((D_MODEL, 2 * D_FF), BF16),
            pltpu.VMEM((D_FF, D_MODEL), BF16),
            pltpu.SemaphoreType.DMA((2,)),
        ],
    )
    return pl.pallas_call(
        _expert_kernel,
        grid_spec=grid_spec,
        out_shape=jax.ShapeDtypeStruct((n_blocks * blk[0], LANES), F32),
        compiler_params=pltpu.CompilerParams(
            dimension_semantics=("arbitrary",), vmem_limit_bytes=VMEM_LIMIT),
        name="experts",
    )(block_expert, n_used, next_expert, valid_rows, xs_tiles,
      w_up, b_up[:, None, :], w_down, b_down[:, None, :])


_COMBINE_RING = 3


def _combine_kernel(dst0_ref, dst1_ref, dst2_ref, y_hbm, h1_ref, route_ref, g_ref, o_ref, *scratch):
    bufs, sem = scratch[:_COMBINE_RING], scratch[_COMBINE_RING]
    i = pl.program_id(0)
    last = pl.num_programs(0) - 1
    tm = h1_ref.shape[0]
    rows = TOP_K * tm

    def gather_group(idx_ref, s, row0):
        srcs = [idx_ref[row0 + r] for r in range(_ISSUE_UNROLL)]
        for r in range(_ISSUE_UNROLL):
            pltpu.make_async_copy(_token_tile(y_hbm, srcs[r]), _token_tile(bufs[s], row0 + r),
                                  sem.at[s]).start(priority=r % 2)

    @pl.when(i == 0)
    def _():
        for s, idx_ref in ((0, dst0_ref), (1, dst1_ref)):
            def body(o, carry, s=s, idx_ref=idx_ref):
                gather_group(idx_ref, s, pl.multiple_of(o * _ISSUE_UNROLL, _ISSUE_UNROLL))
                return carry
            lax.fori_loop(0, rows // _ISSUE_UNROLL, body, 0)

    def step(s):
        cur = bufs[s]
        ahead = (s + 2) % _COMBINE_RING
        _tiles_wait_copy(y_hbm, cur, rows, sem.at[s]).wait()
        for g in range(rows // _ISSUE_UNROLL):
            gather_group(dst2_ref, ahead, g * _ISSUE_UNROLL)
        assert tm == LANES
        rt = jnp.concatenate([route_ref[...], jnp.zeros((LANES - _ROUTE_ROWS, tm), F32)], axis=0).T
        acc = h1_ref[...]
        for k in range(TOP_K):
            gate = rt[:, _ROUTE_GATE + k:_ROUTE_GATE + k + 1]
            acc = acc + gate * _load_token_tiles(cur, k * tm, tm)
        ms = jnp.mean(acc * acc, axis=-1, keepdims=True)
        o_ref[...] = acc * lax.rsqrt(ms + RMS_EPS) * g_ref[...]

        @pl.when(i == last)
        def _():
            for t in ((s + 1) % _COMBINE_RING, ahead):
                _tiles_wait_copy(y_hbm, bufs[t], rows, sem.at[t]).wait()

    for s in range(_COMBINE_RING):
        pl.when(i % _COMBINE_RING == s)(functools.partial(step, s))


def _combine(dest_kmajor, y, h1, route, g, tm):
    n = h1.shape[0]
    n_tiles = n // tm
    idx_spec = lambda ahead: pl.BlockSpec(
        (TOP_K * tm,), lambda i: (jnp.minimum(i + ahead, n_tiles - 1),), memory_space=pltpu.SMEM)
    ring_buf = pltpu.VMEM((TOP_K * tm * TILE_ROWS, LANES), F32)
    return pl.pallas_call(
        _combine_kernel,
        grid=(n_tiles,),
        in_specs=[
            idx_spec(0), idx_spec(1), idx_spec(2),
            pl.BlockSpec(memory_space=pl.ANY),
            pl.BlockSpec((tm, D_MODEL), lambda i: (i, 0)),
            pl.BlockSpec((_ROUTE_ROWS, tm), lambda i: (0, i)),
            pl.BlockSpec((1, D_MODEL), lambda i: (0, 0)),
        ],
        out_specs=pl.BlockSpec((tm, D_MODEL), lambda i: (i, 0)),
        out_shape=jax.ShapeDtypeStruct((n, D_MODEL), F32),
        scratch_shapes=[ring_buf] * _COMBINE_RING + [pltpu.SemaphoreType.DMA((_COMBINE_RING,))],
        compiler_params=pltpu.CompilerParams(
            dimension_semantics=("arbitrary",), vmem_limit_bytes=VMEM_LIMIT),
        name="combine",
    )(dest_kmajor, dest_kmajor, dest_kmajor, y, h1, route, g)


def _routing_tables(route, counts, n, tokens_dispatch, tokens_combine):
    tm = EXPERT_ROWS
    i32 = jnp.int32
    n_blocks = (n * TOP_K + N_EXPERTS * (tm - 1)) // tm
    idx = route[_ROUTE_IDX:_ROUTE_IDX + TOP_K].astype(i32)
    rank = route[_ROUTE_RANK:_ROUTE_RANK + TOP_K].astype(i32)
    cnt = counts[:, 0].astype(i32)
    eid = jnp.arange(N_EXPERTS, dtype=i32)
    upto = eid[None, :] <= eid[:, None]
    blocks_e = (cnt + tm - 1) // tm
    blocks_end = jnp.sum(jnp.where(upto, blocks_e[None, :], 0), axis=1)
    row_start = (blocks_end - blocks_e) * tm
    n_used = blocks_end[N_EXPERTS - 1]
    used = blocks_e > 0

    def lookup(table, keys):
        hit = keys[None] == eid.reshape((N_EXPERTS,) + (1,) * keys.ndim)
        return jnp.sum(jnp.where(hit, table.reshape((N_EXPERTS,) + (1,) * keys.ndim), 0), axis=0)

    dest = lookup(row_start, idx) + rank
    blk = jnp.arange(n_blocks, dtype=i32)
    last_used = jnp.max(jnp.where(used, eid, 0))
    be = jnp.where(blk < n_used, jnp.sum((blocks_end[None, :] <= blk[:, None]).astype(i32), axis=1), last_used)
    later_used = used[None, :] & (eid[None, :] > eid[:, None])
    after = jnp.min(jnp.where(later_used, eid[None, :], N_EXPERTS), axis=1)
    next_e = jnp.where(after < N_EXPERTS, after, -1)

    def k_major(tokens):
        return dest.reshape(TOP_K, n // tokens, tokens).transpose(1, 0, 2).reshape(-1)

    valid = jnp.clip(lookup(row_start + cnt, be) - blk * tm, 0, tm)
    return (be.astype(i32), n_used.reshape(1), lookup(next_e, be).astype(i32), valid.astype(i32),
            k_major(tokens_dispatch), k_major(tokens_combine), row_start + cnt, blocks_e * tm - cnt,
            n_blocks * tm)


def kernel(x, meta_tokens, mix_norm_g, w_in, ssm_a_re, ssm_a_im, ssm_log_dt, ssm_b_re, ssm_b_im,
           ssm_c_re, ssm_c_im, ssm_d, w_ssm_glu, attn_sinks, w_attn_o, w_out, ffn_norm_g,
           router_w, router_b, w_up, b_up, w_down, b_down, final_norm_g):
    bsz, seq, d = x.shape
    assert d == D_MODEL and seq % max(WINDOW, SSM_CHUNK) == 0
    assert mix_norm_g.shape[0] == 1, "single-layer trunk"
    n = bsz * seq
    tm_proj = min(1024, n)
    tm_mix = min(1024, n)
    tm_comb = min(128, n)
    x2 = x.reshape(n, D_MODEL)

    g_mix = mix_norm_g[0][None, :]
    w_in_bf, u_m, kv_m = _meta_proj(meta_tokens, g_mix, w_in[0])
    u, q, kv, gs, ga = _in_proj(x2, g_mix, w_in_bf, tm_proj, pitched=True)

    ssm_par = _ssm_params(ssm_a_re[0], ssm_a_im[0], ssm_log_dt[0], ssm_b_re[0], ssm_b_im[0],
                          ssm_c_re[0], ssm_c_im[0], ssm_d[0], SSM_CHUNK)
    y_ssm = _ssm(u, u_m, *ssm_par, batch=bsz, chunk=SSM_CHUNK)

    attn = _attention(attn_sinks[0], q, kv, kv_m, seq // WINDOW)

    h1, hf, route, counts = _mix(
        x2, y_ssm, attn, gs, ga, w_ssm_glu[0].astype(BF16), w_attn_o[0].astype(BF16),
        w_out[0].astype(BF16), ffn_norm_g[0][None, :], router_w[0], router_b[0][None, :], tm_mix)

    tok_disp = min(1024, n)
    be, n_used, next_e, valid, dest_disp, dest_comb, pad_start, pad_len, n_rows = _routing_tables(
        route, counts, n, tok_disp, tm_comb)
    xs = _dispatch(dest_disp, pad_start, pad_len, n_used, hf, n_rows, tok_disp)
    y = _experts(be, n_used, next_e, valid, xs, w_up[0], b_up[0], w_down[0], b_down[0])
    out = _combine(dest_comb, y, h1, route, final_norm_g[None, :], tm_comb)
    return out.reshape(bsz, seq, D_MODEL)
```

```python
import functools
import math

import jax
import jax.numpy as jnp
from jax import lax
from jax.experimental import pallas as pl
from jax.experimental.pallas import tpu as pltpu

F32 = jnp.float32
BF16 = jnp.bfloat16

D_MODEL = 1024
N_META = 16
SSM_WIDTH = 512
SSM_GROUP = 16
SSM_GROUPS = 32
SSM_STATE = 64
HEAD_DIM = 64
N_HEADS = 16
N_KV_HEADS = 2
KV_REP = N_HEADS // N_KV_HEADS
WINDOW = 128
Q_WIDTH = N_HEADS * HEAD_DIM
KV_WIDTH = N_KV_HEADS * HEAD_DIM
N_EXPERTS = 32
TOP_K = 4
D_FF = 1024
SWIGLU_ALPHA = 1.702
SWIGLU_LIMIT = 7.0
RMS_EPS = 1e-5
NEG_INF = -1e30

_U0, _Q0, _KV0, _GS0, _GA0, _IN_END = 0, 512, 1536, 1792, 2816, 3840

SSM_CH_BLOCK = 128
SSM_HALF = (SSM_CH_BLOCK // SSM_GROUP) * SSM_STATE
SSM_CHUNK = 32
SSM_PITCH = 40
SSM_UNROLL_A = 16
SSM_UNROLL_B = 4
SSM_CHUNK_UNROLL = 4
LOG2_E = math.log2(math.e)
EXPERT_ROWS = 512
EXPERT_SUB_ROWS = 128
VMEM_LIMIT = 56 * 1024 * 1024


def _dot(a, b):
    return jnp.dot(a, b, preferred_element_type=F32)


def _dot_nt(a, b):
    return lax.dot_general(a, b, (((1,), (1,)), ((), ())), preferred_element_type=F32)


LANES = 128
TILE_ROWS = D_MODEL // LANES


def _store_token_tiles(ref, x, start_row=0):
    rows = x.shape[0]
    for j in range(TILE_ROWS):
        ref[pl.ds(start_row * TILE_ROWS + j, rows, stride=TILE_ROWS), :] = x[:, j * LANES:(j + 1) * LANES]


def _load_token_tiles(ref, start_row, rows):
    return jnp.concatenate(
        [ref[pl.ds(start_row * TILE_ROWS + j, rows, stride=TILE_ROWS), :] for j in range(TILE_ROWS)], axis=1)


def _token_tile(ref, row):
    return ref.at[pl.ds(pl.multiple_of(row * TILE_ROWS, TILE_ROWS), TILE_ROWS), :]


def _in_proj_kernel(x_ref, g_ref, w_ref, u_ref, q_ref, kv_ref, gs_ref, ga_ref):
    x = x_ref[...]
    ms = jnp.mean(x * x, axis=-1, keepdims=True)
    hn = (x * lax.rsqrt(ms + RMS_EPS) * g_ref[...]).astype(BF16)
    u = _dot(hn, w_ref[:, _U0:_Q0])
    for c in range(u.shape[0] // SSM_CHUNK):
        u_ref[c * SSM_PITCH:c * SSM_PITCH + SSM_CHUNK, :] = u[c * SSM_CHUNK:(c + 1) * SSM_CHUNK, :]
        u_ref[c * SSM_PITCH + SSM_CHUNK:(c + 1) * SSM_PITCH, :] = jnp.zeros(
            (SSM_PITCH - SSM_CHUNK, SSM_WIDTH), F32)
    q_ref[...] = (_dot(hn, w_ref[:, _Q0:_KV0]) * (HEAD_DIM ** -0.5 * LOG2_E)).astype(BF16)
    kv_ref[...] = _dot(hn, w_ref[:, _KV0:_GS0]).astype(BF16)
    gs_ref[...] = jax.nn.sigmoid(_dot(hn, w_ref[:, _GS0:_GA0])).astype(BF16)
    ga_ref[...] = jax.nn.sigmoid(_dot(hn, w_ref[:, _GA0:_IN_END])).astype(BF16)


def _in_proj(x2, g, w_bf, tm):
    n = x2.shape[0]
    row = lambda w: pl.BlockSpec((tm, w), lambda i: (i, 0))
    full = lambda a: pl.BlockSpec(a.shape, lambda i: (0,) * a.ndim)
    u_rows = lambda r: r // SSM_CHUNK * SSM_PITCH
    return pl.pallas_call(
        _in_proj_kernel,
        grid=(n // tm,),
        in_specs=[row(D_MODEL), full(g), full(w_bf)],
        out_specs=[pl.BlockSpec((u_rows(tm), SSM_WIDTH), lambda i: (i, 0)),
                   row(Q_WIDTH), row(2 * KV_WIDTH), row(D_MODEL), row(D_MODEL)],
        out_shape=[
            jax.ShapeDtypeStruct((u_rows(n), SSM_WIDTH), F32),
            jax.ShapeDtypeStruct((n, Q_WIDTH), BF16),
            jax.ShapeDtypeStruct((n, 2 * KV_WIDTH), BF16),
            jax.ShapeDtypeStruct((n, D_MODEL), BF16),
            jax.ShapeDtypeStruct((n, D_MODEL), BF16),
        ],
        compiler_params=pltpu.CompilerParams(
            dimension_semantics=("arbitrary",), vmem_limit_bytes=VMEM_LIMIT),
        name="in_proj",
    )(x2, g, w_bf)


def _meta_proj_kernel(x_ref, g_ref, w_ref, wbf_ref, u_ref, kv_ref):
    wbf_ref[...] = w_ref[...].astype(BF16)
    x = x_ref[...]
    ms = jnp.mean(x * x, axis=-1, keepdims=True)
    hn = (x * lax.rsqrt(ms + RMS_EPS) * g_ref[...]).astype(BF16)
    u_ref[...] = _dot(hn, wbf_ref[:, _U0:_Q0])
    kv_ref[...] = _dot(hn, wbf_ref[:, _KV0:_GS0]).astype(BF16)


def _meta_proj(meta, g, w):
    full = lambda a: pl.BlockSpec(a.shape, lambda i: (0,) * a.ndim)
    shapes = [jax.ShapeDtypeStruct(w.shape, BF16), jax.ShapeDtypeStruct((N_META, SSM_WIDTH), F32),
              jax.ShapeDtypeStruct((N_META, 2 * KV_WIDTH), BF16)]
    return pl.pallas_call(
        _meta_proj_kernel,
        grid=(1,),
        in_specs=[full(meta), full(g), pl.BlockSpec(w.shape, lambda i: (0, 0), pipeline_mode=pl.Buffered(1))],
        out_specs=[full(s) for s in shapes],
        out_shape=shapes,
        compiler_params=pltpu.CompilerParams(
            dimension_semantics=("arbitrary",), vmem_limit_bytes=VMEM_LIMIT),
        name="meta_proj",
    )(meta, g, w)


def _ssm_kernel(u_ref, um_ref, bm_ref, cm_ref, ar_ref, ai_ref, atr_ref, ati_ref, d_ref,
                y_ref, sre, sim, *, chunk, rows, batch):
    h = SSM_HALF
    bm = bm_ref[0]
    cm = cm_ref[0]
    ar, ai = ar_ref[0], ai_ref[0]
    atr, ati = atr_ref[0], ati_ref[0]
    dsk = d_ref[0]
    n_chunks = rows // batch

    def advance(sr, si, bu):
        return ar * sr - ai * si + bu[:, :h], ar * si + ai * sr + bu[:, h:]

    bum = _dot(um_ref[...].astype(BF16), bm)
    mr = jnp.zeros((1, h), F32)
    mi = jnp.zeros((1, h), F32)
    for j in range(N_META):
        mr, mi = advance(mr, mi, bum[j:j + 1, :])

    def u_step(t):
        return u_ref[pl.ds(t, rows, stride=SSM_PITCH), :]

    for pad in range(chunk, SSM_PITCH):
        y_ref[pl.ds(pad, rows, stride=SSM_PITCH), :] = jnp.zeros((rows, SSM_CH_BLOCK), F32)

    sre[...] = jnp.zeros_like(sre)
    sim[...] = jnp.zeros_like(sim)

    def pass_a(o, carry):
        for k in range(SSM_UNROLL_A):
            t = o * SSM_UNROLL_A + k
            bu = _dot(u_step(t).astype(BF16), bm)
            nr, ni = advance(sre[...], sim[...], bu)
            sre[...] = nr
            sim[...] = ni
        return carry

    lax.fori_loop(0, chunk // SSM_UNROLL_A, pass_a, 0)

    def over_chunks(o, carry):
        rows_of = [[pl.ds(b * n_chunks + o * SSM_CHUNK_UNROLL + k, 1) for b in range(batch)]
                   for k in range(SSM_CHUNK_UNROLL)]
        ends = [[(sre[row, :], sim[row, :]) for row in rows_k] for rows_k in rows_of]
        carry = list(carry)
        for k in range(SSM_CHUNK_UNROLL):
            for b in range(batch):
                cr, ci = carry[2 * b], carry[2 * b + 1]
                er, ei = ends[k][b]
                sre[rows_of[k][b], :] = cr
                sim[rows_of[k][b], :] = ci
                carry[2 * b], carry[2 * b + 1] = atr * cr - ati * ci + er, atr * ci + ati * cr + ei
        return tuple(carry)

    assert n_chunks % SSM_CHUNK_UNROLL == 0
    lax.fori_loop(0, n_chunks // SSM_CHUNK_UNROLL, over_chunks, (mr, mi) * batch)

    def pass_b(o, carry):
        for k in range(SSM_UNROLL_B):
            t = o * SSM_UNROLL_B + k
            ut = u_step(t)
            bu = _dot(ut.astype(BF16), bm)
            nr, ni = advance(sre[...], sim[...], bu)
            sre[...] = nr
            sim[...] = ni
            y = _dot(nr.astype(BF16), cm[:h, :]) + _dot(ni.astype(BF16), cm[h:, :]) + dsk * ut
            y_ref[pl.ds(t, rows, stride=SSM_PITCH), :] = y
        return carry

    lax.fori_loop(0, chunk // SSM_UNROLL_B, pass_b, 0)


def _ssm(u, u_meta, bmat, cmat, a_re, a_im, at_re, at_im, dskip, batch, chunk):
    n = u.shape[0]
    rows = n // SSM_PITCH
    nblk = SSM_WIDTH // SSM_CH_BLOCK
    col = lambda r: pl.BlockSpec((r, SSM_CH_BLOCK), lambda j: (0, j))
    par = lambda a: pl.BlockSpec((1,) + a.shape[1:], lambda j: (j, 0, 0))
    return pl.pallas_call(
        functools.partial(_ssm_kernel, chunk=chunk, rows=rows, batch=batch),
        grid=(nblk,),
        in_specs=[col(n), col(N_META), par(bmat), par(cmat), par(a_re), par(a_im),
                  par(at_re), par(at_im), par(dskip)],
        out_specs=col(n),
        out_shape=jax.ShapeDtypeStruct((n, SSM_WIDTH), F32),
        scratch_shapes=[pltpu.VMEM((rows, SSM_HALF), F32), pltpu.VMEM((rows, SSM_HALF), F32)],
        compiler_params=pltpu.CompilerParams(
            dimension_semantics=("arbitrary",), vmem_limit_bytes=VMEM_LIMIT),
        name="ssm",
    )(u, u_meta, bmat, cmat, a_re, a_im, at_re, at_im, dskip)


def _ssm_params(a_re, a_im, log_dt, b_re, b_im, c_re, c_im, d_skip, chunk):
    dt = jnp.exp(log_dt)[:, None]
    mag = jnp.exp(a_re * dt)
    ang = a_im * dt
    abar_re, abar_im = mag * jnp.cos(ang), mag * jnp.sin(ang)
    den = a_re * a_re + a_im * a_im
    nr, ni = abar_re - 1.0, abar_im
    coef_re = ((nr * a_re + ni * a_im) / den)[..., None]
    coef_im = ((ni * a_re - nr * a_im) / den)[..., None]
    bbar_re = coef_re * b_re - coef_im * b_im
    bbar_im = coef_re * b_im + coef_im * b_re
    magt = jnp.exp(a_re * dt * chunk)
    at_re, at_im = magt * jnp.cos(ang * chunk), magt * jnp.sin(ang * chunk)

    nblk = SSM_WIDTH // SSM_CH_BLOCK
    gpb = SSM_GROUPS // nblk
    eye = jnp.eye(gpb, dtype=F32)

    def in_map(b):
        b = b.reshape(nblk, gpb, SSM_STATE, SSM_GROUP)
        return jnp.einsum('jgpc,gh->jgchp', b, eye).reshape(nblk, SSM_CH_BLOCK, gpb * SSM_STATE)

    def out_map(c):
        c = c.reshape(nblk, gpb, SSM_GROUP, SSM_STATE)
        return jnp.einsum('jgcp,gh->jgphc', c, eye).reshape(nblk, gpb * SSM_STATE, SSM_CH_BLOCK)

    bmat = jnp.concatenate([in_map(bbar_re), in_map(bbar_im)], axis=2).astype(BF16)
    cmat = jnp.concatenate([out_map(c_re), -out_map(c_im)], axis=1).astype(BF16)
    vec = lambda v: v.reshape(nblk, 1, SSM_HALF)
    return (bmat, cmat, vec(abar_re), vec(abar_im), vec(at_re), vec(at_im),
            d_skip.reshape(nblk, 1, SSM_CH_BLOCK))


ATTN_STEP_BLOCKS = 16


def _attn_kernel(sink_ref, q_ref, kvc_ref, kvp_ref, kvm_ref, o_ref, *, blocks_per_seq, step_blocks):
    for sb in range(step_blocks):
        rows = pl.ds(sb * WINDOW, WINDOW)
        prev = kvp_ref if sb == 0 else kvc_ref.at[pl.ds((sb - 1) * WINDOW, WINDOW), :]
        _attn_block(sink_ref, q_ref.at[rows, :], kvc_ref.at[rows, :], prev, kvm_ref, o_ref.at[rows, :],
                    (pl.program_id(0) * step_blocks + sb) % blocks_per_seq)


def _attn_block(sink_ref, q_ref, kvc_ref, kvp_ref, kvm_ref, o_ref, n):
    w = WINDOW
    hd = HEAD_DIM
    qi = lax.broadcasted_iota(jnp.int32, (w, w), 0)
    lane = lax.broadcasted_iota(jnp.int32, (w, w), 1)
    vis_prev = (lane > qi) & (n > 0)
    vis_cur = lane <= qi
    left = lane < hd
    meta_l = lane < N_META
    meta_r = (lane >= N_META) & (lane < 2 * N_META)

    def placed(x_bf):
        x = x_bf.astype(F32)
        xr = pltpu.roll(x, hd, 1)
        lm = lax.broadcasted_iota(jnp.int32, x.shape, 1) < hd
        z = jnp.zeros_like(x)
        return {(0, 0): jnp.where(lm, x, z), (0, 1): jnp.where(lm, z, xr),
                (1, 0): jnp.where(lm, xr, z), (1, 1): jnp.where(lm, z, x)}

    kp, kc, km = placed(kvp_ref[:, :KV_WIDTH]), placed(kvc_ref[:, :KV_WIDTH]), placed(kvm_ref[:, :KV_WIDTH])
    vp, vc, vm = placed(kvp_ref[:, KV_WIDTH:]), placed(kvc_ref[:, KV_WIDTH:]), placed(kvm_ref[:, KV_WIDTH:])
    pad_rows = w - 2 * N_META
    zpad = jnp.zeros((pad_rows, w), F32)
    krow = lax.broadcasted_iota(jnp.int32, (5 * w, w), 0)
    klane = lax.broadcasted_iota(jnp.int32, (5 * w, w), 1)
    row_l = (krow < 2 * w) | ((krow >= 4 * w) & (krow < 4 * w + N_META))
    row_r = ((krow >= 2 * w) & (krow < 4 * w)) | ((krow >= 4 * w + N_META) & (krow < 4 * w + 2 * N_META))
    den_cols = jnp.where((row_l & (klane < hd)) | (row_r & (klane >= hd)), 1.0, 0.0)

    for j in range(N_KV_HEADS):
        kcat = jnp.concatenate(
            [kp[j, 0], kc[j, 0], kp[j, 1], kc[j, 1], km[j, 0], km[j, 1], zpad], axis=0).astype(BF16)
        vcat = jnp.concatenate(
            [jnp.concatenate([vp[j, 0], vc[j, 0], vp[j, 1], vc[j, 1], vm[j, 0], vm[j, 1], zpad], axis=0),
             den_cols], axis=1).astype(BF16)
        for r in range(KV_REP // 2):
            pr = j * (KV_REP // 2) + r
            s = _dot_nt(q_ref[:, pr * w:(pr + 1) * w], kcat)
            s_l = jnp.where(vis_cur, s[:, w:2 * w], jnp.where(vis_prev, s[:, 0:w], NEG_INF))
            s_r = jnp.where(vis_cur, s[:, 3 * w:4 * w], jnp.where(vis_prev, s[:, 2 * w:3 * w], NEG_INF))
            s_m = s[:, 4 * w:]
            sink_l, sink_r = sink_ref[2 * pr] * LOG2_E, sink_ref[2 * pr + 1] * LOG2_E
            m_l = jnp.maximum(jnp.max(jnp.maximum(s_l, jnp.where(meta_l, s_m, NEG_INF)),
                                      axis=1, keepdims=True), sink_l)
            m_r = jnp.maximum(jnp.max(jnp.maximum(s_r, jnp.where(meta_r, s_m, NEG_INF)),
                                      axis=1, keepdims=True), sink_r)
            s_m = jnp.where(meta_l, s_m - m_l, jnp.where(meta_r, s_m - m_r, NEG_INF))
            e_l, e_r = jnp.exp2(s_l - m_l), jnp.exp2(s_r - m_r)
            e = jnp.concatenate([jnp.where(vis_cur, 0.0, e_l), jnp.where(vis_cur, e_l, 0.0),
                                 jnp.where(vis_cur, 0.0, e_r), jnp.where(vis_cur, e_r, 0.0),
                                 jnp.exp2(s_m)], axis=1).astype(BF16)
            acc = _dot(e, vcat)
            den = acc[:, w:] + jnp.where(left, jnp.exp2(sink_l - m_l), jnp.exp2(sink_r - m_r))
            o_ref[:, pr * w:(pr + 1) * w] = (acc[:, :w] / den).astype(BF16)


def _attention(sinks, q, kv, kv_meta, blocks_per_seq):
    n = q.shape[0]
    step_blocks = math.gcd(ATTN_STEP_BLOCKS, blocks_per_seq)
    step = step_blocks * WINDOW
    return pl.pallas_call(
        functools.partial(_attn_kernel, blocks_per_seq=blocks_per_seq, step_blocks=step_blocks),
        grid=(n // step,),
        in_specs=[
            pl.BlockSpec(memory_space=pltpu.SMEM),
            pl.BlockSpec((step, Q_WIDTH), lambda g: (g, 0)),
            pl.BlockSpec((step, 2 * KV_WIDTH), lambda g: (g, 0)),
            pl.BlockSpec((WINDOW, 2 * KV_WIDTH), lambda g: (jnp.maximum(g * step_blocks - 1, 0), 0)),
            pl.BlockSpec((N_META, 2 * KV_WIDTH), lambda g: (0, 0)),
        ],
        out_specs=pl.BlockSpec((step, Q_WIDTH), lambda g: (g, 0)),
        out_shape=jax.ShapeDtypeStruct((n, Q_WIDTH), BF16),
        compiler_params=pltpu.CompilerParams(dimension_semantics=("arbitrary",)),
        name="attn",
    )(sinks, q, kv, kv, kv_meta)


_ROUTE_IDX, _ROUTE_RANK, _ROUTE_GATE, _ROUTE_ROWS = 0, TOP_K, 2 * TOP_K, 16


def _mix_kernel(x_ref, y_ref, at_ref, gs_ref, ga_ref, wglu_ref, wo_ref, wout_ref, fg_ref, rw_ref, rb_ref,
                h1_ref, hf_ref, route_ref, cnt_ref, cnt_scr):
    tm = x_ref.shape[0]

    @pl.when(pl.program_id(0) == 0)
    def _():
        cnt_scr[...] = jnp.zeros_like(cnt_scr)

    y_ssm = jnp.concatenate([y_ref[c * SSM_PITCH:c * SSM_PITCH + SSM_CHUNK, :]
                             for c in range(tm // SSM_CHUNK)], axis=0)
    glu = _dot(jax.nn.gelu(y_ssm).astype(BF16), wglu_ref[...])
    branch_ssm = glu[:, :D_MODEL] * jax.nn.sigmoid(glu[:, D_MODEL:])
    branch_attn = _dot(at_ref[...], wo_ref[...])
    merged = gs_ref[...].astype(F32) * branch_ssm + ga_ref[...].astype(F32) * branch_attn
    h1 = x_ref[...] + _dot(merged.astype(BF16), wout_ref[...])
    h1_ref[...] = h1
    ms = jnp.mean(h1 * h1, axis=-1, keepdims=True)
    hf = h1 * lax.rsqrt(ms + RMS_EPS) * fg_ref[...]
    _store_token_tiles(hf_ref, hf)

    hf_hi = hf.astype(BF16)
    hf_lo = (hf - hf_hi.astype(F32)).astype(BF16)
    hi_prod = _dot(hf_hi, rw_ref[...])
    logits = (hi_prod[:, :LANES] + (hi_prod[:, LANES:] + _dot(hf_lo, rw_ref[:, :LANES]))
              + rb_ref[...])
    lt = logits.T[:N_EXPERTS, :]
    erow = lax.broadcasted_iota(jnp.int32, (N_EXPERTS, tm), 0)
    vals, idxs, hots = [], [], []
    rest = lt
    for _ in range(TOP_K):
        m = jnp.max(rest, axis=0, keepdims=True)
        first = jnp.min(jnp.where(rest == m, erow, N_EXPERTS), axis=0, keepdims=True)
        hot = erow == first
        vals.append(m)
        idxs.append(first)
        hots.append(hot)
        rest = jnp.where(hot, -jnp.inf, rest)
    exps = [jnp.exp(v - vals[0]) for v in vals]
    tot = exps[0] + exps[1] + exps[2] + exps[3]

    sel = (hots[0] | hots[1] | hots[2] | hots[3]).astype(F32)
    ti = lax.broadcasted_iota(jnp.int32, (tm, tm), 0)
    tj = lax.broadcasted_iota(jnp.int32, (tm, tm), 1)
    earlier = (ti < tj).astype(BF16)
    rank_e = _dot(sel.astype(BF16), earlier) + cnt_scr[...]
    cnt_scr[...] = cnt_scr[...] + jnp.sum(sel, axis=1, keepdims=True)
    cnt_ref[...] = cnt_scr[...]

    rrow = lax.broadcasted_iota(jnp.int32, (_ROUTE_ROWS, tm), 0)
    route = jnp.zeros((_ROUTE_ROWS, tm), F32)
    for k in range(TOP_K):
        rank_k = jnp.sum(jnp.where(hots[k], rank_e, 0.0), axis=0, keepdims=True)
        route = jnp.where(rrow == _ROUTE_IDX + k, idxs[k].astype(F32), route)
        route = jnp.where(rrow == _ROUTE_RANK + k, rank_k, route)
        route = jnp.where(rrow == _ROUTE_GATE + k, exps[k] / tot, route)
    route_ref[...] = route


def _mix(x2, y, attn, gs, ga, wglu, wo, wout, fg, rw, rb, tm):
    n = x2.shape[0]
    rw = jnp.pad(rw, ((0, 0), (0, LANES - N_EXPERTS)))
    rb = jnp.pad(rb, ((0, 0), (0, LANES - N_EXPERTS)), constant_values=NEG_INF)
    rw_hi = rw.astype(BF16)
    rw_split = jnp.concatenate([rw_hi, (rw - rw_hi.astype(F32)).astype(BF16)], axis=1)
    row = lambda w: pl.BlockSpec((tm, w), lambda i: (i, 0))
    full = lambda a: pl.BlockSpec(a.shape, lambda i: (0,) * a.ndim)
    return pl.pallas_call(
        _mix_kernel,
        grid=(n // tm,),
        in_specs=[row(D_MODEL), pl.BlockSpec((tm // SSM_CHUNK * SSM_PITCH, SSM_WIDTH), lambda i: (i, 0)),
                  row(Q_WIDTH), row(D_MODEL), row(D_MODEL),
                  full(wglu), full(wo), full(wout), full(fg), full(rw_split), full(rb)],
        out_specs=[row(D_MODEL), pl.BlockSpec((tm * TILE_ROWS, LANES), lambda i: (i, 0)),
                   pl.BlockSpec((_ROUTE_ROWS, tm), lambda i: (0, i)),
                   pl.BlockSpec((N_EXPERTS, 1), lambda i: (0, 0))],
        out_shape=[
            jax.ShapeDtypeStruct((n, D_MODEL), F32),
            jax.ShapeDtypeStruct((n * TILE_ROWS, LANES), F32),
            jax.ShapeDtypeStruct((_ROUTE_ROWS, n), F32),
            jax.ShapeDtypeStruct((N_EXPERTS, 1), F32),
        ],
        scratch_shapes=[pltpu.VMEM((N_EXPERTS, 1), F32)],
        compiler_params=pltpu.CompilerParams(
            dimension_semantics=("arbitrary",), vmem_limit_bytes=VMEM_LIMIT),
        name="mix_router",
    )(x2, y, attn, gs, ga, wglu, wo, wout, fg, rw_split, rb)


def _tiles_wait_copy(src_hbm, dst, n_tiles, sem):
    rows = n_tiles * TILE_ROWS
    return pltpu.make_async_copy(src_hbm.at[pl.ds(0, rows), :], dst.at[pl.ds(0, rows), :], sem)


_ISSUE_UNROLL = 16


_DISPATCH_RING = 4


def _dispatch_kernel(dst_ref, pad_start_ref, pad_len_ref, nu_ref, hf_hbm, xs_hbm, zero_blk, ring, in_sem,
                     out_sem, pad_sem, *, n_steps):
    i = pl.program_id(0)
    last = n_steps - 1
    pairs = dst_ref.shape[0]
    tokens = pairs // TOP_K
    blk_rows = tokens * TILE_ROWS
    n_blocks = xs_hbm.shape[0] // (EXPERT_ROWS * TILE_ROWS)
    slot = i % _DISPATCH_RING

    def load(step, s):
        src = hf_hbm.at[pl.ds(pl.multiple_of(step * blk_rows, blk_rows), blk_rows), :]
        return pltpu.make_async_copy(src, ring.at[s], in_sem.at[s])

    def wait_copies(s):
        for _ in range(TOP_K):
            pltpu.make_async_copy(ring.at[s], xs_hbm.at[pl.ds(0, blk_rows), :], out_sem.at[s]).wait()

    def zero_copy(row, rows):
        src = zero_blk.at[pl.ds(0, rows * TILE_ROWS), :]
        dst = xs_hbm.at[pl.ds(pl.multiple_of(row * TILE_ROWS, TILE_ROWS), rows * TILE_ROWS), :]
        return pltpu.make_async_copy(src, dst, pad_sem)

    def for_each_pad(fn):
        def per_expert(e, carry):
            row, left = pad_start_ref[e], pad_len_ref[e]
            size = EXPERT_ROWS // 2
            while size >= 1:
                take = left & size

                @pl.when(take != 0)
                def _(row=row, size=size):
                    fn(zero_copy(row, size))

                row = row + take
                size //= 2
            return carry
        lax.fori_loop(0, N_EXPERTS, per_expert, 0)

        def per_block(b, carry):
            fn(zero_copy(b * EXPERT_ROWS, EXPERT_ROWS))
            return carry
        lax.fori_loop(nu_ref[0], n_blocks, per_block, 0)

    @pl.when(i == 0)
    def _():
        for s in range(min(2, n_steps)):
            load(s, s).start()
        zero_blk[...] = jnp.zeros_like(zero_blk)
        for_each_pad(lambda cp: cp.start())

    @pl.when(i >= 2)
    def _():
        wait_copies((i + 2) % _DISPATCH_RING)

    @pl.when(i + 2 < n_steps)
    def _():
        load(i + 2, (i + 2) % _DISPATCH_RING).start()

    load(i, slot).wait()
    src_blk = ring.at[slot]
    for k in range(TOP_K):
        def issue(o, carry, k=k):
            tok0 = pl.multiple_of(o * _ISSUE_UNROLL, _ISSUE_UNROLL)
            dsts = [dst_ref[k * tokens + tok0 + r] for r in range(_ISSUE_UNROLL)]
            for r in range(_ISSUE_UNROLL):
                pltpu.make_async_copy(_token_tile(src_blk, tok0 + r), _token_tile(xs_hbm, dsts[r]),
                                      out_sem.at[slot]).start(priority=r % 2)
            return carry

        lax.fori_loop(0, tokens // _ISSUE_UNROLL, issue, 0)

    @pl.when(i == last)
    def _():
        if n_steps > 1:
            wait_copies((i + _DISPATCH_RING - 1) % _DISPATCH_RING)
        wait_copies(slot)
        for_each_pad(lambda cp: cp.wait())


def _dispatch(dest, pad_start, pad_len, n_used, hf_tiles, n_rows, tokens_per_step):
    n = dest.shape[0] // TOP_K
    pairs = tokens_per_step * TOP_K
    assert n % tokens_per_step == 0 and tokens_per_step % _ISSUE_UNROLL == 0
    smem = lambda: pl.BlockSpec(memory_space=pltpu.SMEM)
    n_steps = n // tokens_per_step
    return pl.pallas_call(
        functools.partial(_dispatch_kernel, n_steps=n_steps),
        grid=(n_steps,),
        in_specs=[pl.BlockSpec((pairs,), lambda i: (i,), memory_space=pltpu.SMEM), smem(), smem(), smem(),
                  pl.BlockSpec(memory_space=pl.ANY)],
        out_specs=pl.BlockSpec(memory_space=pl.ANY),
        out_shape=jax.ShapeDtypeStruct((n_rows * TILE_ROWS, LANES), F32),
        scratch_shapes=[pltpu.VMEM((EXPERT_ROWS * TILE_ROWS, LANES), F32),
                        pltpu.VMEM((_DISPATCH_RING, tokens_per_step * TILE_ROWS, LANES), F32),
                        pltpu.SemaphoreType.DMA((_DISPATCH_RING,)), pltpu.SemaphoreType.DMA((_DISPATCH_RING,)),
                        pltpu.SemaphoreType.DMA],
        compiler_params=pltpu.CompilerParams(
            dimension_semantics=("arbitrary",), vmem_limit_bytes=VMEM_LIMIT),
        name="dispatch",
    )(dest, pad_start, pad_len, n_used, hf_tiles)


def _expert_kernel(be_ref, nu_ref, nxt_ref, val_ref, xs_ref, wu_hbm, bu_ref, wd_hbm, bd_ref, y_ref,
                   wu_f32, wd_f32, wu_bf, wd_bf, wsem):
    i = pl.program_id(0)

    def mlp(xb):
        up = _dot(xb, wu_bf[...]) + bu_ref[0]
        x_glu = jnp.minimum(up[:, :D_FF], SWIGLU_LIMIT)
        x_lin = jnp.clip(up[:, D_FF:], -SWIGLU_LIMIT, SWIGLU_LIMIT)
        act = x_glu * jax.nn.sigmoid(SWIGLU_ALPHA * x_glu) * (x_lin + 1.0)
        return _dot(act.astype(BF16), wd_bf[...]) + bd_ref[0]

    def weight_copies(expert):
        return (pltpu.make_async_copy(wu_hbm.at[expert], wu_f32, wsem.at[0]),
                pltpu.make_async_copy(wd_hbm.at[expert], wd_f32, wsem.at[1]))

    @pl.when(i == 0)
    def _():
        for cp in weight_copies(be_ref[0]):
            cp.start()

    first = (i == 0) | (be_ref[i] != be_ref[jnp.maximum(i - 1, 0)])
    used = i < nu_ref[0]
    full = used & (val_ref[i] > EXPERT_ROWS - EXPERT_SUB_ROWS)

    def take_weights():
        for cp in weight_copies(be_ref[i]):
            cp.wait()
        wu_bf[...] = wu_f32[...].astype(BF16)
        wd_bf[...] = wd_f32[...].astype(BF16)

    def fetch_next_weights():
        @pl.when(nxt_ref[i] >= 0)
        def _():
            for cp in weight_copies(nxt_ref[i]):
                cp.start(priority=1)

    def whole_block():
        _store_token_tiles(y_ref, mlp(_load_token_tiles(xs_ref, 0, EXPERT_ROWS).astype(BF16)))

    @pl.when(first & full)
    def _():
        take_weights()
        whole_block()
        fetch_next_weights()

    @pl.when(first & jnp.logical_not(full))
    def _():
        take_weights()
        fetch_next_weights()

    @pl.when(full & jnp.logical_not(first))
    def _():
        whole_block()

    @pl.when(used & jnp.logical_not(full))
    def _():
        sub = EXPERT_SUB_ROWS
        n_sub = lax.shift_right_logical(val_ref[i] + (sub - 1), sub.bit_length() - 1)

        def compute(j, carry):
            row0 = pl.multiple_of(j * sub, sub)
            _store_token_tiles(y_ref, mlp(_load_token_tiles(xs_ref, row0, sub).astype(BF16)), row0)
            return carry

        def clear(j, carry):
            row0 = pl.multiple_of(j * sub * TILE_ROWS, sub * TILE_ROWS)
            y_ref[pl.ds(row0, sub * TILE_ROWS), :] = jnp.zeros((sub * TILE_ROWS, LANES), F32)
            return carry

        lax.fori_loop(0, n_sub, compute, 0)
        lax.fori_loop(n_sub, EXPERT_ROWS // sub, clear, 0)

    @pl.when(jnp.logical_not(used))
    def _():
        y_ref[...] = jnp.zeros_like(y_ref)


def _experts(block_expert, n_used, next_expert, valid_rows, xs_tiles, w_up, b_up, w_down, b_down):
    n_blocks = block_expert.shape[0]
    blk = (EXPERT_ROWS * TILE_ROWS, LANES)
    grid_spec = pltpu.PrefetchScalarGridSpec(
        num_scalar_prefetch=4,
        grid=(n_blocks,),
        in_specs=[
            pl.BlockSpec(blk, lambda i, be, nu, *_: (jnp.minimum(i, nu[0] - 1), 0)),
            pl.BlockSpec(memory_space=pl.ANY),
            pl.BlockSpec((1, 1, 2 * D_FF), lambda i, be, *_: (be[i], 0, 0)),
            pl.BlockSpec(memory_space=pl.ANY),
            pl.BlockSpec((1, 1, D_MODEL), lambda i, be, *_: (be[i], 0, 0)),
        ],
        out_specs=pl.BlockSpec(blk, lambda i, *_: (i, 0)),
        scratch_shapes=[
            pltpu.VMEM((D_MODEL, 2 * D_FF), F32),
            pltpu.VMEM((D_FF, D_MODEL), F32),
            pltpu.VMEM((D_MODEL, 2 * D_FF), BF16),
            pltpu.VMEM((D_FF, D_MODEL), BF16),
            pltpu.SemaphoreType.DMA((2,)),
        ],
    )
    return pl.pallas_call(
        _expert_kernel,
        grid_spec=grid_spec,
        out_shape=jax.ShapeDtypeStruct((n_blocks * blk[0], LANES), F32),
        compiler_params=pltpu.CompilerParams(
            dimension_semantics=("arbitrary",), vmem_limit_bytes=VMEM_LIMIT),
        name="experts",
    )(block_expert, n_used, next_expert, valid_rows, xs_tiles,
      w_up, b_up[:, None, :], w_down, b_down[:, None, :])


_COMBINE_RING = 3


def _combine_kernel(dst0_ref, dst1_ref, dst2_ref, y_hbm, h1_ref, route_ref, g_ref, o_ref, *scratch):
    bufs, sem = scratch[:_COMBINE_RING], scratch[_COMBINE_RING]
    i = pl.program_id(0)
    last = pl.num_programs(0) - 1
    tm = h1_ref.shape[0]
    rows = TOP_K * tm

    def gather_group(idx_ref, s, row0):
        srcs = [idx_ref[row0 + r] for r in range(_ISSUE_UNROLL)]
        for r in range(_ISSUE_UNROLL):
            pltpu.make_async_copy(_token_tile(y_hbm, srcs[r]), _token_tile(bufs[s], row0 + r),
                                  sem.at[s]).start(priority=r % 2)

    @pl.when(i == 0)
    def _():
        for s, idx_ref in ((0, dst0_ref), (1, dst1_ref)):
            def body(o, carry, s=s, idx_ref=idx_ref):
                gather_group(idx_ref, s, pl.multiple_of(o * _ISSUE_UNROLL, _ISSUE_UNROLL))
                return carry
            lax.fori_loop(0, rows // _ISSUE_UNROLL, body, 0)

    def step(s):
        cur = bufs[s]
        ahead = (s + 2) % _COMBINE_RING
        _tiles_wait_copy(y_hbm, cur, rows, sem.at[s]).wait()
        for g in range(rows // _ISSUE_UNROLL):
            gather_group(dst2_ref, ahead, g * _ISSUE_UNROLL)
        assert tm == LANES
        rt = jnp.concatenate([route_ref[...], jnp.zeros((LANES - _ROUTE_ROWS, tm), F32)], axis=0).T
        acc = h1_ref[...]
        for k in range(TOP_K):
            gate = rt[:, _ROUTE_GATE + k:_ROUTE_GATE + k + 1]
            acc = acc + gate * _load_token_tiles(cur, k * tm, tm)
        ms = jnp.mean(acc * acc, axis=-1, keepdims=True)
        o_ref[...] = acc * lax.rsqrt(ms + RMS_EPS) * g_ref[...]

        @pl.when(i == last)
        def _():
            for t in ((s + 1) % _COMBINE_RING, ahead):
                _tiles_wait_copy(y_hbm, bufs[t], rows, sem.at[t]).wait()

    for s in range(_COMBINE_RING):
        pl.when(i % _COMBINE_RING == s)(functools.partial(step, s))


def _combine(dest_kmajor, y, h1, route, g, tm):
    n = h1.shape[0]
    n_tiles = n // tm
    idx_spec = lambda ahead: pl.BlockSpec(
        (TOP_K * tm,), lambda i: (jnp.minimum(i + ahead, n_tiles - 1),), memory_space=pltpu.SMEM)
    ring_buf = pltpu.VMEM((TOP_K * tm * TILE_ROWS, LANES), F32)
    return pl.pallas_call(
        _combine_kernel,
        grid=(n_tiles,),
        in_specs=[
            idx_spec(0), idx_spec(1), idx_spec(2),
            pl.BlockSpec(memory_space=pl.ANY),
            pl.BlockSpec((tm, D_MODEL), lambda i: (i, 0)),
            pl.BlockSpec((_ROUTE_ROWS, tm), lambda i: (0, i)),
            pl.BlockSpec((1, D_MODEL), lambda i: (0, 0)),
        ],
        out_specs=pl.BlockSpec((tm, D_MODEL), lambda i: (i, 0)),
        out_shape=jax.ShapeDtypeStruct((n, D_MODEL), F32),
        scratch_shapes=[ring_buf] * _COMBINE_RING + [pltpu.SemaphoreType.DMA((_COMBINE_RING,))],
        compiler_params=pltpu.CompilerParams(
            dimension_semantics=("arbitrary",), vmem_limit_bytes=VMEM_LIMIT),
        name="combine",
    )(dest_kmajor, dest_kmajor, dest_kmajor, y, h1, route, g)


def _routing_tables(route, counts, n, tokens_dispatch, tokens_combine):
    tm = EXPERT_ROWS
    i32 = jnp.int32
    n_blocks = (n * TOP_K + N_EXPERTS * (tm - 1)) // tm
    idx = route[_ROUTE_IDX:_ROUTE_IDX + TOP_K].astype(i32)
    rank = route[_ROUTE_RANK:_ROUTE_RANK + TOP_K].astype(i32)
    cnt = counts[:, 0].astype(i32)
    eid = jnp.arange(N_EXPERTS, dtype=i32)
    upto = eid[None, :] <= eid[:, None]
    blocks_e = (cnt + tm - 1) // tm
    blocks_end = jnp.sum(jnp.where(upto, blocks_e[None, :], 0), axis=1)
    row_start = (blocks_end - blocks_e) * tm
    n_used = blocks_end[N_EXPERTS - 1]
    used = blocks_e > 0

    def lookup(table, keys):
        hit = keys[None] == eid.reshape((N_EXPERTS,) + (1,) * keys.ndim)
        return jnp.sum(jnp.where(hit, table.reshape((N_EXPERTS,) + (1,) * keys.ndim), 0), axis=0)

    dest = lookup(row_start, idx) + rank
    blk = jnp.arange(n_blocks, dtype=i32)
    last_used = jnp.max(jnp.where(used, eid, 0))
    be = jnp.where(blk < n_used, jnp.sum((blocks_end[None, :] <= blk[:, None]).astype(i32), axis=1), last_used)
    later_used = used[None, :] & (eid[None, :] > eid[:, None])
    after = jnp.min(jnp.where(later_used, eid[None, :], N_EXPERTS), axis=1)
    next_e = jnp.where(after < N_EXPERTS, after, -1)

    def k_major(tokens):
        return dest.reshape(TOP_K, n // tokens, tokens).transpose(1, 0, 2).reshape(-1)

    valid = jnp.clip(lookup(row_start + cnt, be) - blk * tm, 0, tm)
    return (be.astype(i32), n_used.reshape(1), lookup(next_e, be).astype(i32), valid.astype(i32),
            k_major(tokens_dispatch), k_major(tokens_combine), row_start + cnt, blocks_e * tm - cnt,
            n_blocks * tm)


def kernel(x, meta_tokens, mix_norm_g, w_in, ssm_a_re, ssm_a_im, ssm_log_dt, ssm_b_re, ssm_b_im,
           ssm_c_re, ssm_c_im, ssm_d, w_ssm_glu, attn_sinks, w_attn_o, w_out, ffn_norm_g,
           router_w, router_b, w_up, b_up, w_down, b_down, final_norm_g):
    bsz, seq, d = x.shape
    assert d == D_MODEL and seq % max(WINDOW, SSM_CHUNK) == 0
    assert mix_norm_g.shape[0] == 1, "single-layer trunk"
    n = bsz * seq
    tm_proj = min(1024, n)
    tm_mix = min(1024, n)
    tm_comb = min(128, n)
    x2 = x.reshape(n, D_MODEL)

    g_mix = mix_norm_g[0][None, :]
    w_in_bf, u_m, kv_m = _meta_proj(meta_tokens, g_mix, w_in[0])
    u, q, kv, gs, ga = _in_proj(x2, g_mix, w_in_bf, tm_proj)

    ssm_par = _ssm_params(ssm_a_re[0], ssm_a_im[0], ssm_log_dt[0], ssm_b_re[0], ssm_b_im[0],
                          ssm_c_re[0], ssm_c_im[0], ssm_d[0], SSM_CHUNK)
    y_ssm = _ssm(u, u_m, *ssm_par, batch=bsz, chunk=SSM_CHUNK)

    attn = _attention(attn_sinks[0], q, kv, kv_m, seq // WINDOW)

    h1, hf, route, counts = _mix(
        x2, y_ssm, attn, gs, ga, w_ssm_glu[0].astype(BF16), w_attn_o[0].astype(BF16),
        w_out[0].astype(BF16), ffn_norm_g[0][None, :], router_w[0], router_b[0][None, :], tm_mix)

    tok_disp = min(1024, n)
    be, n_used, next_e, valid, dest_disp, dest_comb, pad_start, pad_len, n_rows = _routing_tables(
        route, counts, n, tok_disp, tm_comb)
    xs = _dispatch(dest_disp, pad_start, pad_len, n_used, hf, n_rows, tok_disp)
    y = _experts(be, n_used, next_e, valid, xs, w_up[0], b_up[0], w_down[0], b_down[0])
    out = _combine(dest_comb, y, h1, route, final_norm_g[None, :], tm_comb)
    return out.reshape(bsz, seq, D_MODEL)
```
